```python
import jax, jax.numpy as jnp
from jax import lax
import numpy as np

D_MODEL = 1024
BATCH = 16
SEQ = 2048
DEPTH = 2

N_MIXERS = 2
RMS_EPS = 1e-6

SB_HEADS = 8
SB_HEAD_DIM = D_MODEL // SB_HEADS
SB_WIDTH = SB_HEADS * SB_HEAD_DIM
Q_BLOCK = 128

HG_HEAD_K = 128
HG_HEADS = D_MODEL // HG_HEAD_K
HG_HEAD_V = D_MODEL // HG_HEADS
HG_KEY_WIDTH = HG_HEADS * HG_HEAD_K
HG_VALUE_WIDTH = HG_HEADS * HG_HEAD_V
CHUNK = 64

kernel_name = "hybrid_stickbreaking_hgrn2_trunk"


def rms_norm(x, gain):
    xf = x.astype(jnp.float32)
    y = xf * lax.rsqrt(jnp.mean(xf * xf, axis=-1, keepdims=True) + RMS_EPS)
    return (y * gain.astype(jnp.float32)).astype(x.dtype)


def split_heads(t, n_heads):
    b, s, _ = t.shape
    return t.reshape(b, s, n_heads, -1).transpose(0, 2, 1, 3)


def merge_heads(t):
    b, h, s, d = t.shape
    return t.transpose(0, 2, 1, 3).reshape(b, s, h * d)


def stick_breaking_attention(q, k, v):
    seq = q.shape[2]
    scale = q.shape[-1] ** -0.5
    outs = []
    for blk in range(seq // Q_BLOCK):
        start = blk * Q_BLOCK
        end = start + Q_BLOCK
        qb = q[:, :, start:end]
        kb = k[:, :, :end]
        vb = v[:, :, :end]
        z = jnp.einsum('bhtd,bhsd->bhts', qb, kb).astype(jnp.float32) * scale
        t_idx = start + jnp.arange(Q_BLOCK)[:, None]
        s_idx = jnp.arange(end)[None, :]
        causal = s_idx < t_idx
        log_beta = jax.nn.log_sigmoid(z)
        log_skip_j = jnp.where(causal, jax.nn.log_sigmoid(-z), 0.0)
        log_skip = lax.cumsum(log_skip_j, axis=3, reverse=True) - log_skip_j
        weights = jnp.where(causal, jnp.exp(log_beta + log_skip), 0.0)
        outs.append(jnp.einsum('bhts,bhsd->bhtd', weights.astype(vb.dtype), vb))
    return jnp.concatenate(outs, axis=2)


def hgrn2_chunkwise(q, log_f, k, i):
    b, h, s, dk = q.shape
    dv = i.shape[-1]
    n = s // CHUNK

    def to_chunks(a):
        return a.astype(jnp.float32).reshape(b, h, n, CHUNK, a.shape[-1]).transpose(2, 0, 1, 3, 4)

    qc, gc, kc, ic = to_chunks(q), to_chunks(log_f), to_chunks(k), to_chunks(i)
    incl = jnp.tril(jnp.ones((CHUNK, CHUNK), dtype=bool))[None, None, :, :, None]

    def step(state, xs):
        qb, gb, kb, ib = xs
        g_cum = jnp.cumsum(gb, axis=2)
        rel = g_cum[:, :, :, None, :] - g_cum[:, :, None, :, :]
        decay = jnp.exp(jnp.where(incl, rel, -jnp.inf))
        scores = jnp.einsum('bhtd,bhsd,bhtsd->bhts', qb, kb, decay)
        o_intra = jnp.einsum('bhts,bhsv->bhtv', scores, ib)
        o_inter = jnp.einsum('bhtd,bhdv->bhtv', qb * jnp.exp(g_cum), state)
        g_last = g_cum[:, :, -1:, :]
        new_state = (jnp.exp(g_last[:, :, 0, :])[..., None] * state
                     + jnp.einsum('bhsd,bhsv->bhdv', kb * jnp.exp(g_last - g_cum), ib))
        return new_state, o_intra + o_inter

    state0 = jnp.zeros((b, h, dk, dv), jnp.float32)
    _, oc = lax.scan(step, state0, (qc, gc, kc, ic))
    return oc.transpose(1, 2, 0, 3, 4).reshape(b, h, s, dv).astype(i.dtype)


def stick_breaking_layer(h, norm_gain, w_in, q_gain, k_gain, w_out):
    u = rms_norm(h, norm_gain)
    proj = u @ w_in
    q, k, v, gate = jnp.split(proj, 4, axis=-1)
    q = rms_norm(split_heads(q, SB_HEADS), q_gain)
    k = rms_norm(split_heads(k, SB_HEADS), k_gain)
    v = split_heads(v, SB_HEADS)
    o = merge_heads(stick_breaking_attention(q, k, v))
    return (o * jax.nn.silu(gate)) @ w_out


def hgrn2_layer(h, norm_gain, w_in, o_gain, w_out, lower_bound):
    u = rms_norm(h, norm_gain)
    proj = u @ w_in
    q, f_pre, i, gate = jnp.split(
        proj, [HG_KEY_WIDTH, 2 * HG_KEY_WIDTH, 2 * HG_KEY_WIDTH + HG_VALUE_WIDTH], axis=-1)
    q = split_heads(jax.nn.silu(q), HG_HEADS)
    f_pre = split_heads(f_pre, HG_HEADS).astype(jnp.float32)
    lb = lower_bound.astype(jnp.float32).reshape(HG_HEADS, 1, HG_HEAD_K)
    log_f = jnp.logaddexp(jnp.log(lb), jnp.log1p(-lb) + jax.nn.log_sigmoid(f_pre))
    k_in = -jnp.expm1(log_f)
    o = hgrn2_chunkwise(q, log_f, k_in, split_heads(i, HG_HEADS))
    o = merge_heads(rms_norm(o, o_gain))
    return (o * jax.nn.silu(gate)) @ w_out


def _fwd_setup_inputs(seed: int = 0) -> dict:
    key = jax.random.key(seed)
    ks = jax.random.split(key, 12)
    n_a = (DEPTH + 1) // 2
    n_b = DEPTH // 2

    def normal(k, shape, scale):
        return jax.random.normal(k, shape, jnp.float32) * scale

    return {
        "x": normal(ks[0], (BATCH, SEQ, D_MODEL), 1.0),
        "sb_norm": 1.0 + normal(ks[1], (n_a, D_MODEL), 0.02),
        "sb_w_in": normal(ks[2], (n_a, D_MODEL, 4 * SB_WIDTH), D_MODEL ** -0.5),
        "sb_q_gain": 1.0 + normal(ks[3], (n_a, SB_HEAD_DIM), 0.02),
        "sb_k_gain": 1.0 + normal(ks[4], (n_a, SB_HEAD_DIM), 0.02),
        "sb_w_out": normal(ks[5], (n_a, SB_WIDTH, D_MODEL), SB_WIDTH ** -0.5),
        "hg_norm": 1.0 + normal(ks[6], (n_b, D_MODEL), 0.02),
        "hg_w_in": normal(ks[7], (n_b, D_MODEL, 2 * HG_KEY_WIDTH + 2 * HG_VALUE_WIDTH), D_MODEL ** -0.5),
        "hg_o_gain": 1.0 + normal(ks[8], (n_b, HG_HEAD_V), 0.02),
        "hg_w_out": normal(ks[9], (n_b, HG_VALUE_WIDTH, D_MODEL), HG_VALUE_WIDTH ** -0.5),
        "hg_lb_logits": normal(ks[10], (DEPTH, HG_KEY_WIDTH), 0.1),
    }


def _fwd_reference(x, sb_norm, sb_w_in, sb_q_gain, sb_k_gain, sb_w_out,
              hg_norm, hg_w_in, hg_o_gain, hg_w_out, hg_lb_logits):
    lb_p = jax.nn.softmax(hg_lb_logits.astype(jnp.float32), axis=0)
    lb_table = jnp.cumsum(lb_p, axis=0) - lb_p[0]
    h = x
    for layer in range(DEPTH):
        j = layer // N_MIXERS
        if layer % N_MIXERS == 0:
            h = h + stick_breaking_layer(h, sb_norm[j], sb_w_in[j], sb_q_gain[j],
                                         sb_k_gain[j], sb_w_out[j])
        else:
            h = h + hgrn2_layer(h, hg_norm[j], hg_w_in[j], hg_o_gain[j], hg_w_out[j],
                                lb_table[layer])
    return h


import jax as _jax
import jax.numpy as _jnp

TWIN_FORMAT = 'train_step'
FWD_PARAMS = ['x', 'sb_norm', 'sb_w_in', 'sb_q_gain', 'sb_k_gain', 'sb_w_out', 'hg_norm', 'hg_w_in', 'hg_o_gain', 'hg_w_out', 'hg_lb_logits']
TWIN_WEIGHTS = ['sb_norm', 'sb_w_in', 'sb_q_gain', 'sb_k_gain', 'sb_w_out', 'hg_norm', 'hg_w_in', 'hg_o_gain', 'hg_w_out', 'hg_lb_logits']
TWIN_DIFF_INPUT = 'x'
TWIN_INPUTS = ['x', 'sb_norm', 'sb_w_in', 'sb_q_gain', 'sb_k_gain', 'sb_w_out', 'hg_norm', 'hg_w_in', 'hg_o_gain', 'hg_w_out', 'hg_lb_logits', 'loss_target', 'm_sb_norm', 'm_sb_w_in', 'm_sb_q_gain', 'm_sb_k_gain', 'm_sb_w_out', 'm_hg_norm', 'm_hg_w_in', 'm_hg_o_gain', 'm_hg_w_out', 'm_hg_lb_logits', 'v_sb_norm', 'v_sb_w_in', 'v_sb_q_gain', 'v_sb_k_gain', 'v_sb_w_out', 'v_hg_norm', 'v_hg_w_in', 'v_hg_o_gain', 'v_hg_w_out', 'v_hg_lb_logits']
TWIN_OUTPUTS = ['loss', 'grad_x', 'grad_sb_norm', 'grad_sb_w_in', 'grad_sb_q_gain', 'grad_sb_k_gain', 'grad_sb_w_out', 'grad_hg_norm', 'grad_hg_w_in', 'grad_hg_o_gain', 'grad_hg_w_out', 'grad_hg_lb_logits', 'delta_sb_norm', 'delta_sb_w_in', 'delta_sb_q_gain', 'delta_sb_k_gain', 'delta_sb_w_out', 'delta_hg_norm', 'delta_hg_w_in', 'delta_hg_o_gain', 'delta_hg_w_out', 'delta_hg_lb_logits', 'new_m_sb_norm', 'new_m_sb_w_in', 'new_m_sb_q_gain', 'new_m_sb_k_gain', 'new_m_sb_w_out', 'new_m_hg_norm', 'new_m_hg_w_in', 'new_m_hg_o_gain', 'new_m_hg_w_out', 'new_m_hg_lb_logits', 'new_v_sb_norm', 'new_v_sb_w_in', 'new_v_sb_q_gain', 'new_v_sb_k_gain', 'new_v_sb_w_out', 'new_v_hg_norm', 'new_v_hg_w_in', 'new_v_hg_o_gain', 'new_v_hg_w_out', 'new_v_hg_lb_logits']
TWIN_LEAF_KINDS = {'loss': 'loss', 'grad_x': 'grad_x', 'grad_sb_norm': 'grad_w', 'grad_sb_w_in': 'grad_w', 'grad_sb_q_gain': 'grad_w', 'grad_sb_k_gain': 'grad_w', 'grad_sb_w_out': 'grad_w', 'grad_hg_norm': 'grad_w', 'grad_hg_w_in': 'grad_w', 'grad_hg_o_gain': 'grad_w', 'grad_hg_w_out': 'grad_w', 'grad_hg_lb_logits': 'grad_w', 'delta_sb_norm': 'delta_w', 'delta_sb_w_in': 'delta_w', 'delta_sb_q_gain': 'delta_w', 'delta_sb_k_gain': 'delta_w', 'delta_sb_w_out': 'delta_w', 'delta_hg_norm': 'delta_w', 'delta_hg_w_in': 'delta_w', 'delta_hg_o_gain': 'delta_w', 'delta_hg_w_out': 'delta_w', 'delta_hg_lb_logits': 'delta_w', 'new_m_sb_norm': 'new_m', 'new_m_sb_w_in': 'new_m', 'new_m_sb_q_gain': 'new_m', 'new_m_sb_k_gain': 'new_m', 'new_m_sb_w_out': 'new_m', 'new_m_hg_norm': 'new_m', 'new_m_hg_w_in': 'new_m', 'new_m_hg_o_gain': 'new_m', 'new_m_hg_w_out': 'new_m', 'new_m_hg_lb_logits': 'new_m', 'new_v_sb_norm': 'new_v', 'new_v_sb_w_in': 'new_v', 'new_v_sb_q_gain': 'new_v', 'new_v_sb_k_gain': 'new_v', 'new_v_sb_w_out': 'new_v', 'new_v_hg_norm': 'new_v', 'new_v_hg_w_in': 'new_v', 'new_v_hg_o_gain': 'new_v', 'new_v_hg_w_out': 'new_v', 'new_v_hg_lb_logits': 'new_v'}


def _forward(args):
    return _fwd_reference(*[args[k] for k in FWD_PARAMS])


def _output_shape():
    out = _jax.eval_shape(lambda: _forward(_fwd_setup_inputs(0)))
    return out.shape, out.dtype

N_MICROBATCH = 1
ADAM_LR = 0.001
ADAM_B1 = 0.9
ADAM_B2 = 0.999
ADAM_EPS = 1e-08
ADAM_WD = 0.01
ADAM_STEP = 10
PER_EXAMPLE_BATCH_AXIS = {'x': 0, 'loss_target': 0}
SHARED_INPUTS = []
_WEIGHT_DTYPES = {'sb_norm': _jnp.float32, 'sb_w_in': _jnp.float32, 'sb_q_gain': _jnp.float32, 'sb_k_gain': _jnp.float32, 'sb_w_out': _jnp.float32, 'hg_norm': _jnp.float32, 'hg_w_in': _jnp.float32, 'hg_o_gain': _jnp.float32, 'hg_w_out': _jnp.float32, 'hg_lb_logits': _jnp.float32}
MOMENT_SCALE = {'sb_norm': 1.023175e+01, 'sb_w_in': 1.627280e-01, 'sb_q_gain': 5.603305e+00, 'sb_k_gain': 5.596565e+00, 'sb_w_out': 1.878093e-01, 'hg_norm': 1.323560e+01, 'hg_w_in': 2.100663e-01, 'hg_o_gain': 9.085012e+01, 'hg_w_out': 2.841639e-01, 'hg_lb_logits': 2.186299e-02}


def _to_microbatches(a, axis):
    t = _jnp.moveaxis(a, axis, 0)
    t = t.reshape((N_MICROBATCH, t.shape[0] // N_MICROBATCH) + t.shape[1:])
    return _jnp.moveaxis(t, 1, axis + 1)


def setup_inputs(seed: int = 0) -> dict:
    inp = _fwd_setup_inputs(seed)
    key = _jax.random.fold_in(_jax.random.key(seed), 7919)
    shape, _ = _output_shape()
    out = dict(inp)
    out["loss_target"] = _jax.random.normal(_jax.random.fold_in(key, 0), shape, _jnp.float32)
    for i, name in enumerate(TWIN_WEIGHTS):
        w = inp[name].astype(_jnp.float32)
        if MOMENT_SCALE is None:
            s = _jnp.sqrt(_jnp.mean(_jnp.square(w)) + 1e-30)
        else:
            s = MOMENT_SCALE[name]
        km, kv = _jax.random.split(_jax.random.fold_in(key, i + 1))
        out[name] = w
        out["m_" + name] = s * _jax.random.normal(km, w.shape, _jnp.float32)
        out["v_" + name] = (s * s) * _jax.random.uniform(kv, w.shape, _jnp.float32, 0.5, 1.5)
    if N_MICROBATCH > 1:
        for name, axis in PER_EXAMPLE_BATCH_AXIS.items():
            out[name] = _to_microbatches(out[name], axis)
    return {'x': out['x'], 'sb_norm': out['sb_norm'], 'sb_w_in': out['sb_w_in'], 'sb_q_gain': out['sb_q_gain'], 'sb_k_gain': out['sb_k_gain'], 'sb_w_out': out['sb_w_out'], 'hg_norm': out['hg_norm'], 'hg_w_in': out['hg_w_in'], 'hg_o_gain': out['hg_o_gain'], 'hg_w_out': out['hg_w_out'], 'hg_lb_logits': out['hg_lb_logits'], 'loss_target': out['loss_target'], 'm_sb_norm': out['m_sb_norm'], 'm_sb_w_in': out['m_sb_w_in'], 'm_sb_q_gain': out['m_sb_q_gain'], 'm_sb_k_gain': out['m_sb_k_gain'], 'm_sb_w_out': out['m_sb_w_out'], 'm_hg_norm': out['m_hg_norm'], 'm_hg_w_in': out['m_hg_w_in'], 'm_hg_o_gain': out['m_hg_o_gain'], 'm_hg_w_out': out['m_hg_w_out'], 'm_hg_lb_logits': out['m_hg_lb_logits'], 'v_sb_norm': out['v_sb_norm'], 'v_sb_w_in': out['v_sb_w_in'], 'v_sb_q_gain': out['v_sb_q_gain'], 'v_sb_k_gain': out['v_sb_k_gain'], 'v_sb_w_out': out['v_sb_w_out'], 'v_hg_norm': out['v_hg_norm'], 'v_hg_w_in': out['v_hg_w_in'], 'v_hg_o_gain': out['v_hg_o_gain'], 'v_hg_w_out': out['v_hg_w_out'], 'v_hg_lb_logits': out['v_hg_lb_logits']}


def _loss(weights, diff, rest, loss_target):
    with _jax.named_scope("forward"):
        args = {**rest, TWIN_DIFF_INPUT: diff, **{k: w.astype(_WEIGHT_DTYPES[k]) for k, w in weights.items()}}
        y = _forward(args)
    with _jax.named_scope("loss_head"):
        err = _jnp.square(y.astype(_jnp.float32) - loss_target)
        return 0.5 * _jnp.sum(_jnp.mean(err, axis=-1)) if err.ndim else 0.5 * err


def _adamw(w, g, m, v):
    m = ADAM_B1 * m + (1.0 - ADAM_B1) * g
    v = ADAM_B2 * v + (1.0 - ADAM_B2) * _jnp.square(g)
    m_hat = m / (1.0 - ADAM_B1 ** ADAM_STEP)
    v_hat = v / (1.0 - ADAM_B2 ** ADAM_STEP)
    delta = -ADAM_LR * (m_hat / (_jnp.sqrt(v_hat) + ADAM_EPS) + ADAM_WD * w)
    return delta, m, v


def reference(x, sb_norm, sb_w_in, sb_q_gain, sb_k_gain, sb_w_out, hg_norm, hg_w_in, hg_o_gain, hg_w_out, hg_lb_logits, loss_target, m_sb_norm, m_sb_w_in, m_sb_q_gain, m_sb_k_gain, m_sb_w_out, m_hg_norm, m_hg_w_in, m_hg_o_gain, m_hg_w_out, m_hg_lb_logits, v_sb_norm, v_sb_w_in, v_sb_q_gain, v_sb_k_gain, v_sb_w_out, v_hg_norm, v_hg_w_in, v_hg_o_gain, v_hg_w_out, v_hg_lb_logits):
    given = dict(x=x, sb_norm=sb_norm, sb_w_in=sb_w_in, sb_q_gain=sb_q_gain, sb_k_gain=sb_k_gain, sb_w_out=sb_w_out, hg_norm=hg_norm, hg_w_in=hg_w_in, hg_o_gain=hg_o_gain, hg_w_out=hg_w_out, hg_lb_logits=hg_lb_logits, loss_target=loss_target, m_sb_norm=m_sb_norm, m_sb_w_in=m_sb_w_in, m_sb_q_gain=m_sb_q_gain, m_sb_k_gain=m_sb_k_gain, m_sb_w_out=m_sb_w_out, m_hg_norm=m_hg_norm, m_hg_w_in=m_hg_w_in, m_hg_o_gain=m_hg_o_gain, m_hg_w_out=m_hg_w_out, m_hg_lb_logits=m_hg_lb_logits, v_sb_norm=v_sb_norm, v_sb_w_in=v_sb_w_in, v_sb_q_gain=v_sb_q_gain, v_sb_k_gain=v_sb_k_gain, v_sb_w_out=v_sb_w_out, v_hg_norm=v_hg_norm, v_hg_w_in=v_hg_w_in, v_hg_o_gain=v_hg_o_gain, v_hg_w_out=v_hg_w_out, v_hg_lb_logits=v_hg_lb_logits)
    weights = {n: given[n] for n in TWIN_WEIGHTS}
    shared = {n: given[n] for n in SHARED_INPUTS}
    per_example = {n: given[n] for n in ['x']}
    grad_fn = _jax.value_and_grad(_loss, argnums=(0, 1))

    def one_microbatch(ex, loss_target):
        ex = dict(ex)
        diff = ex.pop(TWIN_DIFF_INPUT)
        return grad_fn(weights, diff, {**shared, **ex}, loss_target)

    if N_MICROBATCH == 1:
        loss, (grad_w, grad_x) = one_microbatch(per_example, given["loss_target"])
    else:
        def body(carry, xs):
            loss_sum, grad_sum = carry
            l_k, (gw_k, gx_k) = one_microbatch(xs[0], xs[1])
            with _jax.named_scope("update"):
                return (loss_sum + l_k, _jax.tree.map(_jnp.add, grad_sum, gw_k)), gx_k

        init = (_jnp.zeros((), _jnp.float32), _jax.tree.map(_jnp.zeros_like, weights))
        (loss, grad_w), grad_x = _jax.lax.scan(body, init, (per_example, given["loss_target"]))
    with _jax.named_scope("update"):
        delta_w, new_m, new_v = {}, {}, {}
        for n in TWIN_WEIGHTS:
            delta_w[n], new_m[n], new_v[n] = _adamw(weights[n], grad_w[n], given["m_" + n], given["v_" + n])
    return (loss, grad_x, *[grad_w[n] for n in TWIN_WEIGHTS], *[delta_w[n] for n in TWIN_WEIGHTS],
            *[new_m[n] for n in TWIN_WEIGHTS], *[new_v[n] for n in TWIN_WEIGHTS])
```

```python
import functools

import jax
import jax.numpy as jnp
from jax import lax
from jax.experimental import pallas as pl
from jax.experimental.pallas import tpu as pltpu

F32 = jnp.float32
BF16 = jnp.bfloat16
MESH = pl.DeviceIdType.MESH
ANY = pl.BlockSpec(memory_space=pl.ANY)

D = 1024
HEADS = 8
HD = 128
NPROJ = 4
RMS_EPS = 1e-6
QB = 128
CH = 64
SCALE = HD ** -0.5
EXP_CLAMP = 60.0
WROWS = 2 * D + 2 * (D // 4)
W_IN_SB, W_IN_HG = 0, 1
W_OUT_SB, W_OUT_HG = 8, 9

ADAM_LR = 0.001
ADAM_B1 = 0.9
ADAM_B2 = 0.999
ADAM_EPS = 1e-08
ADAM_WD = 0.01
ADAM_STEP = 10

NT = (((1,), (1,)), ((), ()))
TN = (((0,), (0,)), ((), ()))
MIB = 1024 * 1024


def _cparams(sem=None, vmem_mib=40):
    return pltpu.CompilerParams(dimension_semantics=sem, vmem_limit_bytes=vmem_mib * MIB)


def _dot(a, b, dims=None):
    if dims is None:
        return jnp.dot(a, b, preferred_element_type=F32)
    return lax.dot_general(a, b, dims, preferred_element_type=F32)


def _sigmoid(x):
    return 1.0 / (1.0 + jnp.exp(-x))


def _rms(x):
    return lax.rsqrt(jnp.mean(x * x, axis=-1, keepdims=True) + RMS_EPS)


def _rms_bwd(x, r, gain, dy):
    a = dy * gain
    dx = r * a - x * (r * r * r) * jnp.mean(x * a, axis=-1, keepdims=True)
    return dx, dy * (x * r)


def _split2(v):
    hi = v.astype(BF16)
    lo = (v - hi.astype(F32)).astype(BF16)
    return hi, lo


def _cum2(v, u):
    hi, lo = _split2(v)
    return _dot(hi, u) + _dot(lo, u)


def _dot3(a, b, dims=None):
    ah, al = _split2(a)
    bh, bl = _split2(b)
    return _dot(ah, bh, dims) + _dot(ah, bl, dims) + _dot(al, bh, dims)


def _cum3(u, v):
    h1 = v.astype(BF16)
    r1 = v - h1.astype(F32)
    h2 = r1.astype(BF16)
    h3 = (r1 - h2.astype(F32)).astype(BF16)
    return _dot(u, h1) + _dot(u, h2) + _dot(u, h3)


def _in_proj_fwd(h, gain, wall, wblk, name):
    T = h.shape[0]
    tm = min(512, T)

    def body(h_ref, g_ref, w_ref, o_ref, u_ref):
        @pl.when(pl.program_id(1) == 0)
        def _():
            x = h_ref[...]
            u_ref[...] = (x * _rms(x) * g_ref[...]).astype(BF16)

        o_ref[...] = _dot(u_ref[...], w_ref[...])

    return pl.pallas_call(
        body, name=name, grid=(T // tm, NPROJ),
        in_specs=[pl.BlockSpec((tm, D), lambda i, n: (i, 0)),
                  pl.BlockSpec((1, D), lambda i, n: (0, 0)),
                  pl.BlockSpec((None, D, D), lambda i, n: (n, wblk, 0))],
        out_specs=pl.BlockSpec((None, tm, D), lambda i, n: (n, i, 0)),
        out_shape=jax.ShapeDtypeStruct((NPROJ, T, D), F32),
        scratch_shapes=[pltpu.VMEM((tm, D), BF16)],
        compiler_params=_cparams(("parallel", "arbitrary")),
    )(h, gain, wall)


def _head_norm2(x, gain):
    outs = []
    for hh in range(2):
        xs = x[:, hh * HD:(hh + 1) * HD]
        r = _rms(xs)
        outs.append((xs, r))
    return outs


def _out_proj_fwd(o, proj, resid, wall, wblk, name, o_gain=None, target=None):
    T = o.shape[0]
    tm = min(512, T)
    kb = D // NPROJ
    with_loss = target is not None

    def body(*refs):
        if with_loss:
            o_ref, g_ref, r_ref, w_ref, og_ref, t_ref, dh_ref, ls_ref, acc = refs
        else:
            o_ref, g_ref, r_ref, w_ref, h_ref, acc = refs
        i, j = pl.program_id(0), pl.program_id(1)

        @pl.when(j == 0)
        def _():
            acc[...] = jnp.zeros_like(acc)

        x = o_ref[...]
        if with_loss:
            x = jnp.concatenate([xs * r * og_ref[...] for xs, r in _head_norm2(x, None)], axis=1)
        g = g_ref[...]
        a = x * (g * _sigmoid(g))
        acc[...] += _dot(a.astype(BF16), w_ref[...])

        @pl.when(j == NPROJ - 1)
        def _():
            hnew = r_ref[...] + acc[...]
            if with_loss:
                err = hnew - t_ref[...]
                dh_ref[...] = err * (1.0 / D)
                part = jnp.sum(err * err, axis=0, keepdims=True)

                @pl.when(i == 0)
                def _():
                    ls_ref[...] = part

                @pl.when(i != 0)
                def _():
                    ls_ref[...] += part
            else:
                h_ref[...] = hnew

    in_specs = [pl.BlockSpec((tm, kb), lambda i, j: (i, j)),
                pl.BlockSpec((None, tm, kb), lambda i, j: (3, i, j)),
                pl.BlockSpec((tm, D), lambda i, j: (i, 0)),
                pl.BlockSpec((None, kb, D), lambda i, j: (j, wblk, 0))]
    args = [o, proj, resid, wall]
    out_specs = pl.BlockSpec((tm, D), lambda i, j: (i, 0))
    out_shape = jax.ShapeDtypeStruct((T, D), F32)
    if with_loss:
        in_specs += [pl.BlockSpec((1, HD), lambda i, j: (0, 0)),
                     pl.BlockSpec((tm, D), lambda i, j: (i, 0))]
        args += [o_gain, target]
        out_specs = [out_specs, pl.BlockSpec((1, D), lambda i, j: (0, 0))]
        out_shape = [out_shape, jax.ShapeDtypeStruct((1, D), F32)]
    return pl.pallas_call(
        body, name=name, grid=(T // tm, NPROJ), in_specs=in_specs, out_specs=out_specs,
        out_shape=out_shape, scratch_shapes=[pltpu.VMEM((tm, D), F32)],
        compiler_params=_cparams(("arbitrary", "arbitrary")),
    )(*args)


def _out_proj_bwd(dy, o, proj, wall, wblk, name, o_gain=None):
    T = o.shape[0]
    tm = min(512, T)
    kb = D // NPROJ
    normed = o_gain is not None

    def body(*refs):
        if normed:
            dy_ref, o_ref, g_ref, w_ref, og_ref, do_ref, dg_ref, dw_ref, dgain_ref = refs
        else:
            dy_ref, o_ref, g_ref, w_ref, do_ref, dg_ref, dw_ref = refs
        j, i = pl.program_id(0), pl.program_id(1)
        g = g_ref[...]
        s = _sigmoid(g)
        sl = g * s
        x = o_ref[...]
        if normed:
            heads = _head_norm2(x, None)
            on = jnp.concatenate([xs * r * og_ref[...] for xs, r in heads], axis=1)
        else:
            on = x
        dyb = dy_ref[...].astype(BF16)
        part = _dot((on * sl).astype(BF16), dyb, TN)

        @pl.when(i == 0)
        def _():
            dw_ref[...] = part

        @pl.when(i != 0)
        def _():
            dw_ref[...] += part

        da = _dot(dyb, w_ref[...], NT)
        d_on = da * sl
        dg_ref[...] = da * on * (s * (1.0 + g * (1.0 - s)))
        if normed:
            dxs, gsum = [], None
            for hh, (xs, r) in enumerate(heads):
                dx, gt = _rms_bwd(xs, r, og_ref[...], d_on[:, hh * HD:(hh + 1) * HD])
                dxs.append(dx)
                gt = jnp.sum(gt, axis=0, keepdims=True)
                gsum = gt if gsum is None else gsum + gt
            do_ref[...] = jnp.concatenate(dxs, axis=1)
            first = jnp.logical_and(i == 0, j == 0)

            @pl.when(first)
            def _():
                dgain_ref[...] = gsum

            @pl.when(jnp.logical_not(first))
            def _():
                dgain_ref[...] += gsum
        else:
            do_ref[...] = d_on

    in_specs = [pl.BlockSpec((tm, D), lambda j, i: (i, 0)),
                pl.BlockSpec((tm, kb), lambda j, i: (i, j)),
                pl.BlockSpec((None, tm, kb), lambda j, i: (3, i, j)),
                pl.BlockSpec((None, kb, D), lambda j, i: (j, wblk, 0))]
    args = [dy, o, proj, wall]
    out_specs = [pl.BlockSpec((tm, kb), lambda j, i: (i, j)),
                 pl.BlockSpec((None, tm, kb), lambda j, i: (3, i, j)),
                 pl.BlockSpec((None, kb, D), lambda j, i: (j, 0, 0))]
    out_shape = [jax.ShapeDtypeStruct((T, D), F32),
                 jax.ShapeDtypeStruct((NPROJ, T, D), F32),
                 jax.ShapeDtypeStruct((NPROJ, kb, D), F32)]
    if normed:
        in_specs.append(pl.BlockSpec((1, HD), lambda j, i: (0, 0)))
        args.append(o_gain)
        out_specs.append(pl.BlockSpec((1, HD), lambda j, i: (0, 0)))
        out_shape.append(jax.ShapeDtypeStruct((1, HD), F32))
    return pl.pallas_call(
        body, name=name, grid=(NPROJ, T // tm), in_specs=in_specs, out_specs=out_specs,
        out_shape=out_shape, compiler_params=_cparams(("arbitrary", "arbitrary")),
    )(*args)


def _in_proj_bwd_x(dproj, wall, wblk, h, gain, dres, name):
    T = h.shape[0]
    tm = min(512, T)

    def body(d_ref, w_ref, h_ref, g_ref, r_ref, dh_ref, dgain_ref, du):
        i, n = pl.program_id(0), pl.program_id(1)
        part = _dot(d_ref[...].astype(BF16), w_ref[...], NT)

        @pl.when(n == 0)
        def _():
            du[...] = part

        @pl.when(n != 0)
        def _():
            du[...] += part

        @pl.when(n == NPROJ - 1)
        def _():
            x = h_ref[...]
            dx, gt = _rms_bwd(x, _rms(x), g_ref[...], du[...])
            dh_ref[...] = r_ref[...] + dx
            gt = jnp.sum(gt, axis=0, keepdims=True)

            @pl.when(i == 0)
            def _():
                dgain_ref[...] = gt

            @pl.when(i != 0)
            def _():
                dgain_ref[...] += gt

    return pl.pallas_call(
        body, name=name, grid=(T // tm, NPROJ),
        in_specs=[pl.BlockSpec((None, tm, D), lambda i, n: (n, i, 0)),
                  pl.BlockSpec((None, D, D), lambda i, n: (n, wblk, 0)),
                  pl.BlockSpec((tm, D), lambda i, n: (i, 0)),
                  pl.BlockSpec((1, D), lambda i, n: (0, 0)),
                  pl.BlockSpec((tm, D), lambda i, n: (i, 0))],
        out_specs=[pl.BlockSpec((tm, D), lambda i, n: (i, 0)),
                   pl.BlockSpec((1, D), lambda i, n: (0, 0))],
        out_shape=[jax.ShapeDtypeStruct((T, D), F32), jax.ShapeDtypeStruct((1, D), F32)],
        scratch_shapes=[pltpu.VMEM((tm, D), F32)],
        compiler_params=_cparams(("arbitrary", "arbitrary")),
    )(dproj, wall, h, gain, dres)


def _in_proj_bwd_w(dproj, h, gain, name):
    T = h.shape[0]
    tk = min(512, T)

    def body(d_ref, h_ref, g_ref, dw_ref):
        k = pl.program_id(1)
        x = h_ref[...]
        u = (x * _rms(x) * g_ref[...]).astype(BF16)
        part = _dot(u, d_ref[...].astype(BF16), TN)

        @pl.when(k == 0)
        def _():
            dw_ref[...] = part

        @pl.when(k != 0)
        def _():
            dw_ref[...] += part

    return pl.pallas_call(
        body, name=name, grid=(NPROJ, T // tk),
        in_specs=[pl.BlockSpec((None, tk, D), lambda n, k: (n, k, 0)),
                  pl.BlockSpec((tk, D), lambda n, k: (k, 0)),
                  pl.BlockSpec((1, D), lambda n, k: (0, 0))],
        out_specs=pl.BlockSpec((None, D, D), lambda n, k: (n, 0, 0)),
        out_shape=jax.ShapeDtypeStruct((NPROJ, D, D), F32),
        compiler_params=_cparams(("parallel", "arbitrary")),
    )(dproj, h, gain)


def _log_sigmoid_pair(z):
    lb = jnp.minimum(z, 0.0) - jnp.log(1.0 + jnp.exp(-jnp.abs(z)))
    return lb, lb - z


def _tile_consts():
    row = lax.broadcasted_iota(jnp.int32, (QB, QB), 0)
    col = lax.broadcasted_iota(jnp.int32, (QB, QB), 1)
    return col < row, (row > col).astype(BF16), (row < col).astype(BF16)


def _sb_fwd(proj, q_gain, k_gain):
    _, Bl, S, _ = proj.shape

    def body(q_ref, k_ref, v_ref, qg_ref, kg_ref, o_ref, ct_ref, qn, kn, vb):
        q = q_ref[...]
        qn[...] = (q * _rms(q) * qg_ref[...]).astype(BF16)
        k = k_ref[...]
        kn[...] = (k * _rms(k) * kg_ref[...]).astype(BF16)
        vb[...] = v_ref[...].astype(BF16)
        tri, u_gt, _ = _tile_consts()

        def q_block(qi, _):
            qs = pl.ds(pl.multiple_of(qi * QB, QB), QB)
            qb = qn[qs, :]

            def tile(kj, c, acc, diag):
                ks = pl.ds(pl.multiple_of(kj * QB, QB), QB)
                z = _dot(qb, kn[ks, :], NT) * SCALE
                lb, ls = _log_sigmoid_pair(z)
                if diag:
                    ls = jnp.where(tri, ls, 0.0)
                w = jnp.exp(lb + _cum2(ls, u_gt) + c)
                if diag:
                    w = jnp.where(tri, w, 0.0)
                acc = acc + _dot(w.astype(BF16), vb[ks, :])
                return c + jnp.sum(ls, axis=1, keepdims=True), acc

            c, acc = tile(qi, jnp.zeros((QB, 1), F32), jnp.zeros((QB, HD), F32), True)
            c, acc = lax.fori_loop(
                0, qi, lambda n, ca: tile(qi - 1 - n, ca[0], ca[1], False), (c, acc))
            o_ref[qs, :] = acc
            ct_ref[qs, :] = c
            return 0

        lax.fori_loop(0, S // QB, q_block, 0)

    def slot(n):
        return pl.BlockSpec((None, None, S, HD), lambda b, h: (n, b, 0, h))

    return pl.pallas_call(
        body, name="sb_fwd", grid=(Bl, HEADS),
        in_specs=[slot(0), slot(1), slot(2),
                  pl.BlockSpec((1, HD), lambda b, h: (0, 0)),
                  pl.BlockSpec((1, HD), lambda b, h: (0, 0))],
        out_specs=[pl.BlockSpec((None, S, HD), lambda b, h: (b, 0, h)),
                   pl.BlockSpec((None, None, S, 1), lambda b, h: (b, h, 0, 0))],
        out_shape=[jax.ShapeDtypeStruct((Bl, S, D), F32),
                   jax.ShapeDtypeStruct((Bl, HEADS, S, 1), F32)],
        scratch_shapes=[pltpu.VMEM((S, HD), BF16)] * 3,
        compiler_params=_cparams(("parallel", "parallel")),
    )(proj, proj, proj, q_gain, k_gain)


def _sb_bwd(proj, ctot, do, dproj, q_gain, k_gain):
    _, Bl, S, _ = proj.shape

    def body(q_ref, k_ref, v_ref, ct_ref, do_ref, qg_ref, kg_ref, _, dqkv_ref, dqg_ref, dkg_ref,
             qn, kn, vb, dob, dkn, dvn):
        first = jnp.logical_and(pl.program_id(0) == 0, pl.program_id(1) == 0)

        @pl.when(first)
        def _():
            dqg_ref[...] = jnp.zeros_like(dqg_ref)
            dkg_ref[...] = jnp.zeros_like(dkg_ref)

        q = q_ref[...]
        qn[...] = (q * _rms(q) * qg_ref[...]).astype(BF16)
        k = k_ref[...]
        rk = _rms(k)
        kn[...] = (k * rk * kg_ref[...]).astype(BF16)
        vb[...] = v_ref[...].astype(BF16)
        dob[...] = do_ref[...].astype(BF16)
        dkn[...] = jnp.zeros_like(dkn)
        dvn[...] = jnp.zeros_like(dvn)
        tri, u_gt, u_lt = _tile_consts()

        def q_block(qi, dqg):
            qs = pl.ds(pl.multiple_of(qi * QB, QB), QB)
            qb = qn[qs, :]
            dobb = dob[qs, :]
            ctot = ct_ref[qs, :]

            def tile(kj, passed, e, dq, diag):
                ks = pl.ds(pl.multiple_of(kj * QB, QB), QB)
                kb = kn[ks, :]
                z = _dot(qb, kb, NT) * SCALE
                lb, ls = _log_sigmoid_pair(z)
                if diag:
                    ls = jnp.where(tri, ls, 0.0)
                passed = passed + jnp.sum(ls, axis=1, keepdims=True)
                w = jnp.exp(lb + _cum2(ls, u_gt) + (ctot - passed))
                if diag:
                    w = jnp.where(tri, w, 0.0)
                de = w * _dot(dobb, vb[ks, :], NT)
                dvn[ks, :] += _dot(w.astype(BF16), dobb, TN)
                dls = e + _cum2(de, u_lt)
                sg = jnp.exp(lb)
                dz = de * (1.0 - sg) - dls * sg
                if diag:
                    dz = jnp.where(tri, dz, 0.0)
                dzb = (dz * SCALE).astype(BF16)
                dq = dq + _dot(dzb, kb)
                dkn[ks, :] += _dot(dzb, qb, TN)
                return passed, e + jnp.sum(de, axis=1, keepdims=True), dq

            zc = jnp.zeros((QB, 1), F32)
            passed, e, dq = lax.fori_loop(
                0, qi, lambda kj, s: tile(kj, s[0], s[1], s[2], False),
                (zc, zc, jnp.zeros((QB, HD), F32)))
            _, _, dq = tile(qi, passed, e, dq, True)
            xq = q_ref[qs, :]
            dx, gt = _rms_bwd(xq, _rms(xq), qg_ref[...], dq)
            dqkv_ref[0, qs, :] = dx
            return dqg + jnp.sum(gt, axis=0, keepdims=True)

        dqg = lax.fori_loop(0, S // QB, q_block, jnp.zeros((1, HD), F32))
        dqg_ref[...] += dqg
        dx, gt = _rms_bwd(k, rk, kg_ref[...], dkn[...])
        dqkv_ref[1] = dx
        dkg_ref[...] += jnp.sum(gt, axis=0, keepdims=True)
        dqkv_ref[2] = dvn[...]

    def slot(n):
        return pl.BlockSpec((None, None, S, HD), lambda b, h: (n, b, 0, h))

    head = pl.BlockSpec((None, S, HD), lambda b, h: (b, 0, h))
    gain = pl.BlockSpec((1, HD), lambda b, h: (0, 0))
    return pl.pallas_call(
        body, name="sb_bwd", grid=(Bl, HEADS),
        in_specs=[slot(0), slot(1), slot(2),
                  pl.BlockSpec((None, None, S, 1), lambda b, h: (b, h, 0, 0)), head, gain, gain, ANY],
        out_specs=[pl.BlockSpec((3, None, S, HD), lambda b, h: (0, b, 0, h)), gain, gain],
        out_shape=[jax.ShapeDtypeStruct(dproj.shape, F32),
                   jax.ShapeDtypeStruct((1, HD), F32), jax.ShapeDtypeStruct((1, HD), F32)],
        scratch_shapes=[pltpu.VMEM((S, HD), BF16)] * 4 + [pltpu.VMEM((S, HD), F32)] * 2,
        input_output_aliases={7: 0},
        compiler_params=_cparams(("arbitrary", "arbitrary")),
    )(proj, proj, proj, ctot, do, q_gain, k_gain, dproj)


def _lower_bound(logits):
    l0, l1 = logits[0:1, :], logits[1:2, :]
    m = jnp.maximum(l0, l1)
    e0, e1 = jnp.exp(l0 - m), jnp.exp(l1 - m)
    p0, p1 = e0 / (e0 + e1), e1 / (e0 + e1)
    return (p0 + p1) - p0, p0 * p1


def _hg_gates(qr, fp, lbv):
    sq = _sigmoid(qr)
    sp = _sigmoid(fp)
    sn = 1.0 / (1.0 + jnp.exp(fp))
    f = lbv + (1.0 - lbv) * sp
    return qr * sq, sq, sp, sn, f, (1.0 - lbv) * sn


def _chunk_consts():
    row = lax.broadcasted_iota(jnp.int32, (CH, CH), 0)
    col = lax.broadcasted_iota(jnp.int32, (CH, CH), 1)
    return col <= row, (col <= row).astype(BF16), (col >= row).astype(BF16)


def _hg_fwd(proj, lb_logits):
    _, Bl, S, _ = proj.shape
    nc = S // CH

    def body(q_ref, f_ref, i_ref, lg_ref, o_ref, st_ref):
        lbv, _ = _lower_bound(lg_ref[...])
        tril, t_inc, _ = _chunk_consts()

        def chunk(c, st):
            rs = pl.ds(pl.multiple_of(c * CH, CH), CH)
            qa, _, _, _, f, k = _hg_gates(q_ref[rs, :], f_ref[rs, :], lbv)
            g = jnp.log(f)
            gc = _cum3(t_inc, g)
            gl = jnp.sum(g, axis=0, keepdims=True)
            gm = gc - 0.5 * gl
            qt = (qa * jnp.exp(jnp.minimum(gm, EXP_CLAMP))).astype(BF16)
            kt = (k * jnp.exp(jnp.minimum(-gm, EXP_CLAMP))).astype(BF16)
            a = jnp.where(tril, _dot(qt, kt, NT), 0.0)
            ib = i_ref[rs, :].astype(BF16)
            st_ref[c] = st
            o_ref[rs, :] = (_dot(a.astype(BF16), ib)
                            + _dot((qa * jnp.exp(gc)).astype(BF16), st.astype(BF16), NT))
            kd = (k * jnp.exp(gl - gc)).astype(BF16)
            return st * jnp.exp(gl) + _dot(ib, kd, TN)

        lax.fori_loop(0, nc, chunk, jnp.zeros((HD, HD), F32))

    def slot(n):
        return pl.BlockSpec((None, None, S, HD), lambda b, h: (n, b, 0, h))

    return pl.pallas_call(
        body, name="hg_fwd", grid=(Bl, HEADS),
        in_specs=[slot(0), slot(1), slot(2), pl.BlockSpec((2, HD), lambda b, h: (0, h))],
        out_specs=[pl.BlockSpec((None, S, HD), lambda b, h: (b, 0, h)),
                   pl.BlockSpec((None, None, nc, HD, HD), lambda b, h: (b, h, 0, 0, 0))],
        out_shape=[jax.ShapeDtypeStruct((Bl, S, D), F32),
                   jax.ShapeDtypeStruct((Bl, HEADS, nc, HD, HD), F32)],
        compiler_params=_cparams(("parallel", "parallel")),
    )(proj, proj, proj, lb_logits)


def _hg_bwd(proj, states, do, dproj, lb_logits):
    _, Bl, S, _ = proj.shape
    nc = S // CH

    def body(q_ref, f_ref, i_ref, st_ref, do_ref, lg_ref, _, dqfi_ref, dlb_ref):
        lbv, _ = _lower_bound(lg_ref[...])
        tril, t_inc, t_dec = _chunk_consts()
        last_row = lax.broadcasted_iota(jnp.int32, (CH, HD), 0) == CH - 1

        def chunk(n, carry):
            dst, dlb = carry
            c = nc - 1 - n
            rs = pl.ds(pl.multiple_of(c * CH, CH), CH)
            qr, fp = q_ref[rs, :], f_ref[rs, :]
            qa, sq, sp, sn, f, k = _hg_gates(qr, fp, lbv)
            g = jnp.log(f)
            gc = _cum3(t_inc, g)
            gl = jnp.sum(g, axis=0, keepdims=True)
            gm = gc - 0.5 * gl
            e_q = jnp.exp(jnp.minimum(gm, EXP_CLAMP))
            e_k = jnp.exp(jnp.minimum(-gm, EXP_CLAMP))
            e_g = jnp.exp(gc)
            e_l = jnp.exp(gl - gc)
            e_gl = jnp.exp(gl)
            qt, kt, qg, kd = qa * e_q, k * e_k, qa * e_g, k * e_l
            qtb, ktb, qgb, kdb = (t.astype(BF16) for t in (qt, kt, qg, kd))
            ib = i_ref[rs, :].astype(BF16)
            dob = do_ref[rs, :].astype(BF16)
            st = st_ref[c]
            dstb = dst.astype(BF16)
            ab = jnp.where(tril, _dot(qtb, ktb, NT), 0.0).astype(BF16)
            da = jnp.where(tril, _dot(dob, ib, NT), 0.0)
            dqt = _dot3(da, kt)
            dkt = _dot3(da, qt, TN)
            dqfi_ref[2, rs, :] = _dot(ab, dob, TN) + _dot(kdb, dstb, NT)
            dkd = _dot(ib, dstb)
            dqg = _dot(dob, st.astype(BF16))
            dgl = (jnp.sum(dkd * kd, axis=0, keepdims=True)
                   + jnp.sum(dst * st, axis=0, keepdims=True) * e_gl)
            dgc = dqt * qt - dkt * kt + dqg * qg - dkd * kd
            dgc = dgc + jnp.where(last_row, dgl, 0.0)
            df = _cum3(t_dec, dgc) / f
            t1 = df - (dkt * e_k + dkd * e_l)
            dqfi_ref[1, rs, :] = (1.0 - lbv) * t1 * sp * sn
            dqfi_ref[0, rs, :] = (dqt * e_q + dqg * e_g) * (sq * (1.0 + qr * (1.0 - sq)))
            return (dst * e_gl + _dot(dob, qgb, TN),
                    dlb + jnp.sum(sn * t1, axis=0, keepdims=True))

        _, dlb = lax.fori_loop(0, nc, chunk, (jnp.zeros((HD, HD), F32), jnp.zeros((1, HD), F32)))

        @pl.when(pl.program_id(1) == 0)
        def _():
            dlb_ref[...] = dlb

        @pl.when(pl.program_id(1) != 0)
        def _():
            dlb_ref[...] += dlb

    def slot(n):
        return pl.BlockSpec((None, None, S, HD), lambda h, b: (n, b, 0, h))

    return pl.pallas_call(
        body, name="hg_bwd", grid=(HEADS, Bl),
        in_specs=[slot(0), slot(1), slot(2),
                  pl.BlockSpec((None, None, nc, HD, HD), lambda h, b: (b, h, 0, 0, 0)),
                  pl.BlockSpec((None, S, HD), lambda h, b: (b, 0, h)),
                  pl.BlockSpec((2, HD), lambda h, b: (0, h)), ANY],
        out_specs=[pl.BlockSpec((3, None, S, HD), lambda h, b: (0, b, 0, h)),
                   pl.BlockSpec((1, HD), lambda h, b: (0, h))],
        out_shape=[jax.ShapeDtypeStruct(dproj.shape, F32), jax.ShapeDtypeStruct((1, D), F32)],
        input_output_aliases={6: 0},
        compiler_params=_cparams(("parallel", "arbitrary")),
    )(proj, proj, proj, states, do, lb_logits, dproj)


def _place():
    x, y, c = lax.axis_index("x"), lax.axis_index("y"), lax.axis_index("c")
    return x, y, c, [(1 - x, y), (x, 1 - y), (1 - x, 1 - y)]


def _remote(src, dst, ssem, rsem, dev):
    return pltpu.make_async_remote_copy(src_ref=src, dst_ref=dst, send_sem=ssem, recv_sem=rsem,
                                        device_id=dev, device_id_type=MESH)


def _gather_weights(wp, hn):
    half = WROWS // 2

    def body(wp_ref, hn_ref, wall_ref, hnall_ref, ssem, rsem, lsem):
        x, y, c, chips = _place()
        b = 2 * x + y
        mine = pl.ds(c * half, half)
        other = pl.ds((1 - c) * half, half)
        local = [pltpu.make_async_copy(wp_ref, wall_ref.at[b], lsem.at[0]),
                 pltpu.make_async_copy(hn_ref, hnall_ref.at[b], lsem.at[1])]
        for cp in local:
            cp.start()
        sends = []
        for j, chip in enumerate(chips):
            sends.append(_remote(wp_ref.at[mine], wall_ref.at[b, mine], ssem.at[j], rsem.at[j], (*chip, c)))
            sends.append(_remote(hn_ref, hnall_ref.at[b], ssem.at[6 + j], rsem.at[6 + j], (*chip, c)))
        for cp in sends:
            cp.start()
        for j, (cx, cy) in enumerate(chips):
            landed = wall_ref.at[2 * cx + cy, mine]
            _remote(landed, landed, ssem.at[j], rsem.at[j], (cx, cy, c)).wait_recv()
            fwd = _remote(landed, landed, ssem.at[3 + j], rsem.at[3 + j], (x, y, 1 - c))
            fwd.start()
            sends.append(fwd)
        for j, (cx, cy) in enumerate(chips):
            passed = wall_ref.at[2 * cx + cy, other]
            _remote(passed, passed, ssem.at[3 + j], rsem.at[3 + j], (x, y, 1 - c)).wait_recv()
            row = hnall_ref.at[2 * cx + cy]
            _remote(row, row, ssem.at[6 + j], rsem.at[6 + j], (cx, cy, c)).wait_recv()
        for cp in sends:
            cp.wait_send()
        for cp in local:
            cp.wait()

    return pl.pallas_call(
        body, name="gather_weights", in_specs=[ANY, ANY], out_specs=[ANY, ANY],
        out_shape=[jax.ShapeDtypeStruct((NPROJ, WROWS, D), BF16),
                   jax.ShapeDtypeStruct((NPROJ, 1, D // NPROJ), F32)],
        scratch_shapes=[pltpu.SemaphoreType.DMA((9,)), pltpu.SemaphoreType.DMA((9,)),
                        pltpu.SemaphoreType.DMA((2,))],
    )(wp, hn)


def _pair_exchange(grads, pack):
    ng = len(grads)

    def body(*refs):
        g_refs, pack_ref = refs[:ng], refs[ng]
        r_refs, allp_ref = refs[ng + 1:2 * ng + 1], refs[2 * ng + 1]
        ssem, rsem, psend, precv, lsem = refs[2 * ng + 2:]
        x, y, c, _ = _place()
        me = 4 * x + 2 * y + c
        local = pltpu.make_async_copy(pack_ref, allp_ref.at[me], lsem)
        local.start()
        sends = []
        for t, (g, r) in enumerate(zip(g_refs, r_refs)):
            for n in range(NPROJ):
                k = t * NPROJ + n
                sends.append(_remote(g.at[n, 1 - c], r.at[n], ssem.at[k], rsem.at[k], (x, y, 1 - c)))
        flips = [(fx, fy, fc) for fx in (0, 1) for fy in (0, 1) for fc in (0, 1)][1:]
        peers = [(fx + x - 2 * fx * x, fy + y - 2 * fy * y, fc + c - 2 * fc * c) for fx, fy, fc in flips]
        for m, peer in enumerate(peers):
            sends.append(_remote(pack_ref, allp_ref.at[me], psend.at[m], precv.at[m], peer))
        for cp in sends:
            cp.start()
        for t, r in enumerate(r_refs):
            for n in range(NPROJ):
                k = t * NPROJ + n
                _remote(r.at[n], r.at[n], ssem.at[k], rsem.at[k], (x, y, 1 - c)).wait_recv()
        for m, (px, py, pc) in enumerate(peers):
            row = allp_ref.at[4 * px + 2 * py + pc]
            _remote(row, row, psend.at[m], precv.at[m], (px, py, pc)).wait_recv()
        for cp in sends:
            cp.wait_send()
        local.wait()

    out_shape = [jax.ShapeDtypeStruct((NPROJ,) + g.shape[2:], F32) for g in grads]
    out_shape.append(jax.ShapeDtypeStruct((8,) + pack.shape, F32))
    return pl.pallas_call(
        body, name="pair_exchange", in_specs=[ANY] * (ng + 1), out_specs=[ANY] * (ng + 1),
        out_shape=out_shape,
        scratch_shapes=[pltpu.SemaphoreType.DMA((ng * NPROJ,)), pltpu.SemaphoreType.DMA((ng * NPROJ,)),
                        pltpu.SemaphoreType.DMA((7,)), pltpu.SemaphoreType.DMA((7,)),
                        pltpu.SemaphoreType.DMA],
    )(*grads, pack)


def _chip_exchange(sums):
    ng = len(sums)

    def body(*refs):
        s_refs, r_refs = refs[:ng], refs[ng:2 * ng]
        ssem, rsem, lsem = refs[2 * ng:]
        x, y, c, chips = _place()
        b = 2 * x + y
        local = [pltpu.make_async_copy(s.at[b], r.at[b], lsem.at[t])
                 for t, (s, r) in enumerate(zip(s_refs, r_refs))]
        sends = [_remote(s.at[2 * cx + cy], r.at[b], ssem.at[3 * t + j], rsem.at[3 * t + j], (cx, cy, c))
                 for t, (s, r) in enumerate(zip(s_refs, r_refs)) for j, (cx, cy) in enumerate(chips)]
        for cp in local + sends:
            cp.start()
        for t, r in enumerate(r_refs):
            for j, (cx, cy) in enumerate(chips):
                slot = r.at[2 * cx + cy]
                _remote(slot, slot, ssem.at[3 * t + j], rsem.at[3 * t + j], (cx, cy, c)).wait_recv()
        for cp in sends:
            cp.wait_send()
        for cp in local:
            cp.wait()

    return pl.pallas_call(
        body, name="chip_exchange", in_specs=[ANY] * ng, out_specs=[ANY] * ng,
        out_shape=[jax.ShapeDtypeStruct(s.shape, F32) for s in sums],
        scratch_shapes=[pltpu.SemaphoreType.DMA((3 * ng,)), pltpu.SemaphoreType.DMA((3 * ng,)),
                        pltpu.SemaphoreType.DMA((ng,))],
    )(*sums)


def _sibling_share(halves):
    ng = len(halves)

    def body(*refs):
        h_refs, f_refs = refs[:ng], refs[ng:2 * ng]
        ssem, rsem, lsem = refs[2 * ng:]
        x, y, c, _ = _place()
        local = [pltpu.make_async_copy(h, f.at[c], lsem.at[t]) for t, (h, f) in enumerate(zip(h_refs, f_refs))]
        sends = [_remote(h, f.at[c], ssem.at[t], rsem.at[t], (x, y, 1 - c))
                 for t, (h, f) in enumerate(zip(h_refs, f_refs))]
        for cp in local + sends:
            cp.start()
        for t, f in enumerate(f_refs):
            slot = f.at[1 - c]
            _remote(slot, slot, ssem.at[t], rsem.at[t], (x, y, 1 - c)).wait_recv()
        for cp in sends:
            cp.wait_send()
        for cp in local:
            cp.wait()

    return pl.pallas_call(
        body, name="sibling_share", in_specs=[ANY] * ng, out_specs=[ANY] * ng,
        out_shape=[jax.ShapeDtypeStruct((2,) + h.shape, F32) for h in halves],
        scratch_shapes=[pltpu.SemaphoreType.DMA((ng,)), pltpu.SemaphoreType.DMA((ng,)),
                        pltpu.SemaphoreType.DMA((ng,))],
    )(*halves)


def _pair_add(own, recv, cidx, name):
    R = own.shape[2]
    tr = min(256, R)

    def body(c_ref, a_ref, b_ref, o_ref):
        o_ref[...] = a_ref[...] + b_ref[...]

    return pl.pallas_call(
        body, name=name,
        grid_spec=pltpu.PrefetchScalarGridSpec(
            num_scalar_prefetch=1, grid=(NPROJ, R // tr),
            in_specs=[pl.BlockSpec((None, None, tr, D), lambda n, r, c: (n, c[0], r, 0)),
                      pl.BlockSpec((None, tr, D), lambda n, r, c: (n, r, 0))],
            out_specs=pl.BlockSpec((None, tr, D), lambda n, r, c: (n, r, 0))),
        out_shape=jax.ShapeDtypeStruct(recv.shape, F32),
        compiler_params=_cparams(("parallel", "parallel")),
    )(cidx, own, recv)


def _sum4(parts, name):
    R = parts.shape[1]
    tr = min(256, R)

    def body(p_ref, o_ref):
        o_ref[...] = ((p_ref[0] + p_ref[1]) + p_ref[2]) + p_ref[3]

    return pl.pallas_call(
        body, name=name, grid=(R // tr,),
        in_specs=[pl.BlockSpec((NPROJ, tr, D), lambda r: (0, r, 0))],
        out_specs=pl.BlockSpec((tr, D), lambda r: (r, 0)),
        out_shape=jax.ShapeDtypeStruct((R, D), F32),
        compiler_params=_cparams(("parallel",)),
    )(parts)


def _adamw_math(w, g, m, v):
    m = ADAM_B1 * m + (1.0 - ADAM_B1) * g
    v = ADAM_B2 * v + (1.0 - ADAM_B2) * (g * g)
    m_hat = m / (1.0 - ADAM_B1 ** ADAM_STEP)
    v_hat = v / (1.0 - ADAM_B2 ** ADAM_STEP)
    delta = -ADAM_LR * (m_hat / (jnp.sqrt(v_hat) + ADAM_EPS) + ADAM_WD * w)
    return delta, m, v


def _adamw(w, g, m, v, name):
    R = w.shape[0]
    tr = min(256, R)

    def body(w_ref, g_ref, m_ref, v_ref, d_ref, nm_ref, nv_ref):
        d_ref[...], nm_ref[...], nv_ref[...] = _adamw_math(w_ref[...], g_ref[...], m_ref[...], v_ref[...])

    spec = pl.BlockSpec((tr, D), lambda r: (r, 0))
    return pl.pallas_call(
        body, name=name, grid=(R // tr,), in_specs=[spec] * 4, out_specs=[spec] * 3,
        out_shape=[jax.ShapeDtypeStruct((R, D), F32)] * 3,
        compiler_params=_cparams(("parallel",)),
    )(w, g, m, v)


PACK_ROWS = 8


def _small_update(allp, bidx, logits, weights, moments_m, moments_v):
    shapes = [w.shape for w in weights]
    q4 = D // NPROJ

    def body(b_ref, allp_ref, hgp_ref, lg_ref, *refs):
        w_refs, m_refs, v_refs = refs[0:6], refs[6:12], refs[12:18]
        loss_ref = refs[18]
        g_out, d_out, m_out, v_out = refs[19:25], refs[25:31], refs[31:37], refs[37:43]

        def total(ref, row, lo, hi):
            acc = ref[0, row:row + 1, lo:hi]
            for dev in range(1, 8):
                acc = acc + ref[dev, row:row + 1, lo:hi]
            return acc

        _, pp = _lower_bound(lg_ref[...])
        dlb = total(allp_ref, 2, 0, D)
        grads = [total(allp_ref, 0, 0, D), total(allp_ref, 4, 0, HD), total(allp_ref, 4, HD, 2 * HD),
                 total(hgp_ref, 1, 0, q4), total(allp_ref, 4, 2 * HD, 3 * HD), None]
        loss_ref[...] = (0.5 / D) * jnp.sum(total(allp_ref, 3, 0, D), axis=1, keepdims=True)
        for t in range(6):
            if t < 5:
                rows = [(slice(None), grads[t])]
            else:
                rows = [(slice(0, 1), -pp * dlb), (slice(1, 2), pp * dlb)]
            for rs, g in rows:
                g_out[t][rs, :] = g
                d_out[t][rs, :], m_out[t][rs, :], v_out[t][rs, :] = _adamw_math(
                    w_refs[t][rs, :], g, m_refs[t][rs, :], v_refs[t][rs, :])

    whole = [pl.BlockSpec(s, lambda i, b: (0, 0)) for s in shapes]
    return pl.pallas_call(
        body, name="small_update",
        grid_spec=pltpu.PrefetchScalarGridSpec(
            num_scalar_prefetch=1, grid=(1,),
            in_specs=[pl.BlockSpec((8, PACK_ROWS, D), lambda i, b: (0, 0, 0)),
                      pl.BlockSpec((8, PACK_ROWS, q4), lambda i, b: (0, 0, b[0])),
                      pl.BlockSpec((2, D), lambda i, b: (0, 0))] + whole * 3,
            out_specs=[pl.BlockSpec((1, 1), lambda i, b: (0, 0))] + whole * 4),
        out_shape=[jax.ShapeDtypeStruct((1, 1), F32)] + [jax.ShapeDtypeStruct(s, F32) for s in shapes] * 4,
        compiler_params=_cparams(("arbitrary",)),
    )(bidx, allp, allp, logits, *weights, *moments_m, *moments_v)


def kernel(x, sb_norm, sb_w_in, sb_q_gain, sb_k_gain, sb_w_out, hg_norm, hg_w_in, hg_o_gain, hg_w_out, hg_lb_logits, loss_target, m_sb_norm, m_sb_w_in, m_sb_q_gain, m_sb_k_gain, m_sb_w_out, m_hg_norm, m_hg_w_in, m_hg_o_gain, m_hg_w_out, m_hg_lb_logits, v_sb_norm, v_sb_w_in, v_sb_q_gain, v_sb_k_gain, v_sb_w_out, v_hg_norm, v_hg_w_in, v_hg_o_gain, v_hg_w_out, v_hg_lb_logits):
    Bl, S, _ = x.shape
    T = Bl * S
    q4 = D // NPROJ
    cidx = lax.axis_index("c").astype(jnp.int32).reshape(1)
    bidx = (2 * lax.axis_index("x") + lax.axis_index("y")).astype(jnp.int32).reshape(1)

    wp = jnp.concatenate([sb_w_in[0], hg_w_in[0], sb_w_out[0], hg_w_out[0]], axis=0).astype(BF16)
    wall, hnall = _gather_weights(wp, hg_norm)
    hgn = hnall.reshape(1, D)
    x2 = x.reshape(T, D)
    tgt = loss_target.reshape(T, D)

    def heads(a):
        return a.reshape(a.shape[:-2] + (Bl, S, D))

    def flat(a):
        return a.reshape(a.shape[:-3] + (T, D))

    proj0 = _in_proj_fwd(x2, sb_norm, wall, W_IN_SB, "sb_in_fwd")
    o0, ctot = _sb_fwd(heads(proj0), sb_q_gain, sb_k_gain)
    h1 = _out_proj_fwd(flat(o0), proj0, x2, wall, W_OUT_SB, "sb_out_fwd")
    proj1 = _in_proj_fwd(h1, hgn, wall, W_IN_HG, "hg_in_fwd")
    o1, states = _hg_fwd(heads(proj1), hg_lb_logits)
    dh2, loss_terms = _out_proj_fwd(flat(o1), proj1, h1, wall, W_OUT_HG, "hg_out_fwd",
                                    o_gain=hg_o_gain, target=tgt)

    do1, dproj1, gout_hg, d_ogain = _out_proj_bwd(dh2, flat(o1), proj1, wall, W_OUT_HG, "hg_out_bwd",
                                                  o_gain=hg_o_gain)
    dproj1, dlb = _hg_bwd(heads(proj1), states, heads(do1), heads(dproj1), hg_lb_logits)
    dproj1 = flat(dproj1)
    dh1, d_hgn = _in_proj_bwd_x(dproj1, wall, W_IN_HG, h1, hgn, dh2, "hg_in_bwd_x")
    gin_hg = _in_proj_bwd_w(dproj1, h1, hgn, "hg_in_bwd_w")
    do0, dproj0, gout_sb = _out_proj_bwd(dh1, flat(o0), proj0, wall, W_OUT_SB, "sb_out_bwd")
    dproj0, d_qg, d_kg = _sb_bwd(heads(proj0), ctot, heads(do0), heads(dproj0), sb_q_gain, sb_k_gain)
    dproj0 = flat(dproj0)
    grad_x, d_sbn = _in_proj_bwd_x(dproj0, wall, W_IN_SB, x2, sb_norm, dh1, "sb_in_bwd_x")
    gin_sb = _in_proj_bwd_w(dproj0, x2, sb_norm, "sb_in_bwd_w")

    gains = jnp.concatenate([d_qg, d_kg, d_ogain, jnp.zeros((1, D - 3 * HD), F32)], axis=1)
    pack = jnp.concatenate([d_sbn, d_hgn, dlb, loss_terms, gains, jnp.zeros((3, D), F32)], axis=0)
    big = [gin_sb.reshape(NPROJ, 2, D // 2, D), gin_hg.reshape(NPROJ, 2, D // 2, D),
           gout_sb.reshape(NPROJ, 2, q4 // 2, D), gout_hg.reshape(NPROJ, 2, q4 // 2, D)]
    *recv, allp = _pair_exchange(big, pack)
    names = ["sb_in", "hg_in", "sb_out", "hg_out"]
    sums = [_pair_add(g, r, cidx, "pair_add_" + nm) for g, r, nm in zip(big, recv, names)]
    parts = _chip_exchange(sums)
    halves = [_sum4(p, "chip_sum_" + nm) for p, nm in zip(parts, names)]
    full = [f.reshape(1, -1, D) for f in _sibling_share(halves)]
    g_sb_in, g_hg_in, g_sb_out, g_hg_out = full

    big_w = [sb_w_in, hg_w_in, sb_w_out, hg_w_out]
    big_m = [m_sb_w_in, m_hg_w_in, m_sb_w_out, m_hg_w_out]
    big_v = [v_sb_w_in, v_hg_w_in, v_sb_w_out, v_hg_w_out]
    upd = [_adamw(w[0], g[0], m[0], v[0], "adamw_" + nm)
           for w, g, m, v, nm in zip(big_w, full, big_m, big_v, names)]
    (d_sb_in, nm_sb_in, nv_sb_in), (d_hg_in, nm_hg_in, nv_hg_in), \
        (d_sb_out, nm_sb_out, nv_sb_out), (d_hg_out, nm_hg_out, nv_hg_out) = [
            tuple(a[None] for a in u) for u in upd]

    small = _small_update(
        allp, bidx, hg_lb_logits,
        [sb_norm, sb_q_gain, sb_k_gain, hg_norm, hg_o_gain, hg_lb_logits],
        [m_sb_norm, m_sb_q_gain, m_sb_k_gain, m_hg_norm, m_hg_o_gain, m_hg_lb_logits],
        [v_sb_norm, v_sb_q_gain, v_sb_k_gain, v_hg_norm, v_hg_o_gain, v_hg_lb_logits])
    loss = small[0].reshape(())
    (g_sbn, g_qg, g_kg, g_hgn, g_og, g_lb) = small[1:7]
    (d_sbn2, d_qg2, d_kg2, d_hgn2, d_og2, d_lb2) = small[7:13]
    (nm_sbn, nm_qg, nm_kg, nm_hgn, nm_og, nm_lb) = small[13:19]
    (nv_sbn, nv_qg, nv_kg, nv_hgn, nv_og, nv_lb) = small[19:25]

    return (loss, grad_x.reshape(Bl, S, D),
            g_sbn, g_sb_in, g_qg, g_kg, g_sb_out, g_hgn, g_hg_in, g_og, g_hg_out, g_lb,
            d_sbn2, d_sb_in, d_qg2, d_kg2, d_sb_out, d_hgn2, d_hg_in, d_og2, d_hg_out, d_lb2,
            nm_sbn, nm_sb_in, nm_qg, nm_kg, nm_sb_out, nm_hgn, nm_hg_in, nm_og, nm_hg_out, nm_lb,
            nv_sbn, nv_sb_in, nv_qg, nv_kg, nv_sb_out, nv_hgn, nv_hg_in, nv_og, nv_hg_out, nv_lb)
```

```python
import functools

import jax
import jax.numpy as jnp
from jax import lax
from jax.experimental import pallas as pl
from jax.experimental.pallas import tpu as pltpu

F32 = jnp.float32
BF16 = jnp.bfloat16
MESH = pl.DeviceIdType.MESH
ANY = pl.BlockSpec(memory_space=pl.ANY)

D = 1024
HEADS = 8
HD = 128
NPROJ = 4
RMS_EPS = 1e-6
TQ, TK = 512, 256
CH = 64
SCALE = HD ** -0.5
EXP_CLAMP = 60.0
WROWS = 2 * D + 2 * (D // 4)
W_IN_SB, W_IN_HG = 0, 1
W_OUT_SB, W_OUT_HG = 8, 9

ADAM_LR = 0.001
ADAM_B1 = 0.9
ADAM_B2 = 0.999
ADAM_EPS = 1e-08
ADAM_WD = 0.01
ADAM_STEP = 10

NT = (((1,), (1,)), ((), ()))
TN = (((0,), (0,)), ((), ()))
MIB = 1024 * 1024


def _cparams(sem=None, vmem_mib=40):
    return pltpu.CompilerParams(dimension_semantics=sem, vmem_limit_bytes=vmem_mib * MIB)


def _dot(a, b, dims=None):
    if dims is None:
        return jnp.dot(a, b, preferred_element_type=F32)
    return lax.dot_general(a, b, dims, preferred_element_type=F32)


def _sigmoid(x):
    return 1.0 / (1.0 + jnp.exp(-x))


def _rms(x):
    return lax.rsqrt(jnp.mean(x * x, axis=-1, keepdims=True) + RMS_EPS)


def _rms_bwd(x, r, gain, dy):
    a = dy * gain
    dx = r * a - x * (r * r * r) * jnp.mean(x * a, axis=-1, keepdims=True)
    return dx, dy * (x * r)


def _split2(v):
    hi = v.astype(BF16)
    lo = (v - hi.astype(F32)).astype(BF16)
    return hi, lo


def _cum2(v, u):
    hi, lo = _split2(v)
    return _dot(hi, u) + _dot(lo, u)


def _dot3(a, b, dims=None):
    ah, al = _split2(a)
    bh, bl = _split2(b)
    return _dot(ah, bh, dims) + _dot(ah, bl, dims) + _dot(al, bh, dims)


def _cum3(u, v):
    h1 = v.astype(BF16)
    r1 = v - h1.astype(F32)
    h2 = r1.astype(BF16)
    h3 = (r1 - h2.astype(F32)).astype(BF16)
    return _dot(u, h1) + _dot(u, h2) + _dot(u, h3)


def _in_proj_fwd(h, gain, wall, wblk, name):
    T = h.shape[0]
    tm = min(512, T)

    def body(h_ref, g_ref, w_ref, o_ref, u_ref):
        @pl.when(pl.program_id(1) == 0)
        def _():
            x = h_ref[...]
            u_ref[...] = (x * _rms(x) * g_ref[...]).astype(BF16)

        o_ref[...] = _dot(u_ref[...], w_ref[...])

    return pl.pallas_call(
        body, name=name, grid=(T // tm, NPROJ),
        in_specs=[pl.BlockSpec((tm, D), lambda i, n: (i, 0)),
                  pl.BlockSpec((1, D), lambda i, n: (0, 0)),
                  pl.BlockSpec((None, D, D), lambda i, n: (n, wblk, 0))],
        out_specs=pl.BlockSpec((None, tm, D), lambda i, n: (n, i, 0)),
        out_shape=jax.ShapeDtypeStruct((NPROJ, T, D), F32),
        scratch_shapes=[pltpu.VMEM((tm, D), BF16)],
        compiler_params=_cparams(("parallel", "arbitrary")),
    )(h, gain, wall)


def _head_norm2(x, gain):
    outs = []
    for hh in range(2):
        xs = x[:, hh * HD:(hh + 1) * HD]
        r = _rms(xs)
        outs.append((xs, r))
    return outs


def _out_proj_fwd(o, proj, resid, wall, wblk, name, o_gain=None, target=None):
    T = o.shape[0]
    tm = min(512, T)
    kb = D // NPROJ
    with_loss = target is not None

    def body(*refs):
        if with_loss:
            o_ref, g_ref, r_ref, w_ref, og_ref, t_ref, dh_ref, ls_ref, acc = refs
        else:
            o_ref, g_ref, r_ref, w_ref, h_ref, acc = refs
        i, j = pl.program_id(0), pl.program_id(1)

        @pl.when(j == 0)
        def _():
            acc[...] = jnp.zeros_like(acc)

        x = o_ref[...]
        if with_loss:
            x = jnp.concatenate([xs * r * og_ref[...] for xs, r in _head_norm2(x, None)], axis=1)
        g = g_ref[...]
        a = x * (g * _sigmoid(g))
        acc[...] += _dot(a.astype(BF16), w_ref[...])

        @pl.when(j == NPROJ - 1)
        def _():
            hnew = r_ref[...] + acc[...]
            if with_loss:
                err = hnew - t_ref[...]
                dh_ref[...] = err * (1.0 / D)
                part = jnp.sum(err * err, axis=0, keepdims=True)

                @pl.when(i == 0)
                def _():
                    ls_ref[...] = part

                @pl.when(i != 0)
                def _():
                    ls_ref[...] += part
            else:
                h_ref[...] = hnew

    in_specs = [pl.BlockSpec((tm, kb), lambda i, j: (i, j)),
                pl.BlockSpec((None, tm, kb), lambda i, j: (3, i, j)),
                pl.BlockSpec((tm, D), lambda i, j: (i, 0)),
                pl.BlockSpec((None, kb, D), lambda i, j: (j, wblk, 0))]
    args = [o, proj, resid, wall]
    out_specs = pl.BlockSpec((tm, D), lambda i, j: (i, 0))
    out_shape = jax.ShapeDtypeStruct((T, D), F32)
    if with_loss:
        in_specs += [pl.BlockSpec((1, HD), lambda i, j: (0, 0)),
                     pl.BlockSpec((tm, D), lambda i, j: (i, 0))]
        args += [o_gain, target]
        out_specs = [out_specs, pl.BlockSpec((1, D), lambda i, j: (0, 0))]
        out_shape = [out_shape, jax.ShapeDtypeStruct((1, D), F32)]
    return pl.pallas_call(
        body, name=name, grid=(T // tm, NPROJ), in_specs=in_specs, out_specs=out_specs,
        out_shape=out_shape, scratch_shapes=[pltpu.VMEM((tm, D), F32)],
        compiler_params=_cparams(("arbitrary", "arbitrary")),
    )(*args)


def _out_proj_bwd(dy, o, proj, wall, wblk, name, o_gain=None):
    T = o.shape[0]
    tm = min(512, T)
    kb = D // NPROJ
    normed = o_gain is not None

    def body(*refs):
        if normed:
            dy_ref, o_ref, g_ref, w_ref, og_ref, do_ref, dg_ref, dw_ref, dgain_ref = refs
        else:
            dy_ref, o_ref, g_ref, w_ref, do_ref, dg_ref, dw_ref = refs
        j, i = pl.program_id(0), pl.program_id(1)
        g = g_ref[...]
        s = _sigmoid(g)
        sl = g * s
        x = o_ref[...]
        if normed:
            heads = _head_norm2(x, None)
            on = jnp.concatenate([xs * r * og_ref[...] for xs, r in heads], axis=1)
        else:
            on = x
        dyb = dy_ref[...].astype(BF16)
        part = _dot((on * sl).astype(BF16), dyb, TN)

        @pl.when(i == 0)
        def _():
            dw_ref[...] = part

        @pl.when(i != 0)
        def _():
            dw_ref[...] += part

        da = _dot(dyb, w_ref[...], NT)
        d_on = da * sl
        dg_ref[...] = da * on * (s * (1.0 + g * (1.0 - s)))
        if normed:
            dxs, gsum = [], None
            for hh, (xs, r) in enumerate(heads):
                dx, gt = _rms_bwd(xs, r, og_ref[...], d_on[:, hh * HD:(hh + 1) * HD])
                dxs.append(dx)
                gt = jnp.sum(gt, axis=0, keepdims=True)
                gsum = gt if gsum is None else gsum + gt
            do_ref[...] = jnp.concatenate(dxs, axis=1)
            first = jnp.logical_and(i == 0, j == 0)

            @pl.when(first)
            def _():
                dgain_ref[...] = gsum

            @pl.when(jnp.logical_not(first))
            def _():
                dgain_ref[...] += gsum
        else:
            do_ref[...] = d_on

    in_specs = [pl.BlockSpec((tm, D), lambda j, i: (i, 0)),
                pl.BlockSpec((tm, kb), lambda j, i: (i, j)),
                pl.BlockSpec((None, tm, kb), lambda j, i: (3, i, j)),
                pl.BlockSpec((None, kb, D), lambda j, i: (j, wblk, 0))]
    args = [dy, o, proj, wall]
    out_specs = [pl.BlockSpec((tm, kb), lambda j, i: (i, j)),
                 pl.BlockSpec((None, tm, kb), lambda j, i: (3, i, j)),
                 pl.BlockSpec((None, kb, D), lambda j, i: (j, 0, 0))]
    out_shape = [jax.ShapeDtypeStruct((T, D), F32),
                 jax.ShapeDtypeStruct((NPROJ, T, D), F32),
                 jax.ShapeDtypeStruct((NPROJ, kb, D), F32)]
    if normed:
        in_specs.append(pl.BlockSpec((1, HD), lambda j, i: (0, 0)))
        args.append(o_gain)
        out_specs.append(pl.BlockSpec((1, HD), lambda j, i: (0, 0)))
        out_shape.append(jax.ShapeDtypeStruct((1, HD), F32))
    return pl.pallas_call(
        body, name=name, grid=(NPROJ, T // tm), in_specs=in_specs, out_specs=out_specs,
        out_shape=out_shape, compiler_params=_cparams(("arbitrary", "arbitrary")),
    )(*args)


def _in_proj_bwd_x(dproj, wall, wblk, h, gain, dres, name):
    T = h.shape[0]
    tm = min(512, T)

    def body(d_ref, w_ref, h_ref, g_ref, r_ref, dh_ref, dgain_ref, du):
        i, n = pl.program_id(0), pl.program_id(1)
        part = _dot(d_ref[...].astype(BF16), w_ref[...], NT)

        @pl.when(n == 0)
        def _():
            du[...] = part

        @pl.when(n != 0)
        def _():
            du[...] += part

        @pl.when(n == NPROJ - 1)
        def _():
            x = h_ref[...]
            dx, gt = _rms_bwd(x, _rms(x), g_ref[...], du[...])
            dh_ref[...] = r_ref[...] + dx
            gt = jnp.sum(gt, axis=0, keepdims=True)

            @pl.when(i == 0)
            def _():
                dgain_ref[...] = gt

            @pl.when(i != 0)
            def _():
                dgain_ref[...] += gt

    return pl.pallas_call(
        body, name=name, grid=(T // tm, NPROJ),
        in_specs=[pl.BlockSpec((None, tm, D), lambda i, n: (n, i, 0)),
                  pl.BlockSpec((None, D, D), lambda i, n: (n, wblk, 0)),
                  pl.BlockSpec((tm, D), lambda i, n: (i, 0)),
                  pl.BlockSpec((1, D), lambda i, n: (0, 0)),
                  pl.BlockSpec((tm, D), lambda i, n: (i, 0))],
        out_specs=[pl.BlockSpec((tm, D), lambda i, n: (i, 0)),
                   pl.BlockSpec((1, D), lambda i, n: (0, 0))],
        out_shape=[jax.ShapeDtypeStruct((T, D), F32), jax.ShapeDtypeStruct((1, D), F32)],
        scratch_shapes=[pltpu.VMEM((tm, D), F32)],
        compiler_params=_cparams(("arbitrary", "arbitrary")),
    )(dproj, wall, h, gain, dres)


def _in_proj_bwd_w(dproj, h, gain, name):
    T = h.shape[0]
    tk = min(512, T)

    def body(d_ref, h_ref, g_ref, dw_ref):
        k = pl.program_id(1)
        x = h_ref[...]
        u = (x * _rms(x) * g_ref[...]).astype(BF16)
        part = _dot(u, d_ref[...].astype(BF16), TN)

        @pl.when(k == 0)
        def _():
            dw_ref[...] = part

        @pl.when(k != 0)
        def _():
            dw_ref[...] += part

    return pl.pallas_call(
        body, name=name, grid=(NPROJ, T // tk),
        in_specs=[pl.BlockSpec((None, tk, D), lambda n, k: (n, k, 0)),
                  pl.BlockSpec((tk, D), lambda n, k: (k, 0)),
                  pl.BlockSpec((1, D), lambda n, k: (0, 0))],
        out_specs=pl.BlockSpec((None, D, D), lambda n, k: (n, 0, 0)),
        out_shape=jax.ShapeDtypeStruct((NPROJ, D, D), F32),
        compiler_params=_cparams(("parallel", "arbitrary")),
    )(dproj, h, gain)


def _log_sigmoid_pair(z):
    lb = jnp.minimum(z, 0.0) - jnp.log(1.0 + jnp.exp(-jnp.abs(z)))
    return lb, lb - z


def _slab_consts():
    rel = (lax.broadcasted_iota(jnp.int32, (TQ, TK), 1) - lax.broadcasted_iota(jnp.int32, (TQ, TK), 0))
    j = lax.broadcasted_iota(jnp.int32, (TK, TK), 0)
    s = lax.broadcasted_iota(jnp.int32, (TK, TK), 1)
    return rel, (j > s).astype(BF16), (j < s).astype(BF16)


def _sb_fwd(proj, q_gain, k_gain):
    _, Bl, S, _ = proj.shape

    def body(q_ref, k_ref, v_ref, qg_ref, kg_ref, o_ref, ct_ref, qn, kn, vb):
        q = q_ref[...]
        qn[...] = (q * _rms(q) * qg_ref[...]).astype(BF16)
        k = k_ref[...]
        kn[...] = (k * _rms(k) * kg_ref[...]).astype(BF16)
        vb[...] = v_ref[...].astype(BF16)
        rel, u_gt, _ = _slab_consts()

        def q_group(g, _):
            q0 = pl.multiple_of(g * TQ, TQ)
            qs = pl.ds(q0, TQ)
            qb = qn[qs, :]

            def slab(k0, c, acc, shift):
                ks = pl.ds(k0, TK)
                z = _dot(qb, kn[ks, :], NT) * SCALE
                lb, ls = _log_sigmoid_pair(z)
                if shift is not None:
                    ls = jnp.where(rel < -shift, ls, 0.0)
                w = jnp.exp(lb + _cum2(ls, u_gt) + c)
                if shift is not None:
                    w = jnp.where(rel < -shift, w, 0.0)
                acc = acc + _dot(w.astype(BF16), vb[ks, :])
                return c + jnp.sum(ls, axis=1, keepdims=True), acc

            c, acc = jnp.zeros((TQ, 1), F32), jnp.zeros((TQ, HD), F32)
            for shift in reversed(range(0, TQ, TK)):
                c, acc = slab(pl.multiple_of(q0 + shift, TK), c, acc, shift)
            c, acc = lax.fori_loop(
                0, g * (TQ // TK),
                lambda n, ca: slab(pl.multiple_of(q0 - (n + 1) * TK, TK), ca[0], ca[1], None), (c, acc))
            o_ref[qs, :] = acc
            ct_ref[qs, :] = c
            return 0

        lax.fori_loop(0, S // TQ, q_group, 0)

    def slot(n):
        return pl.BlockSpec((None, None, S, HD), lambda b, h: (n, b, 0, h))

    return pl.pallas_call(
        body, name="sb_fwd", grid=(Bl, HEADS),
        in_specs=[slot(0), slot(1), slot(2),
                  pl.BlockSpec((1, HD), lambda b, h: (0, 0)),
                  pl.BlockSpec((1, HD), lambda b, h: (0, 0))],
        out_specs=[pl.BlockSpec((None, S, HD), lambda b, h: (b, 0, h)),
                   pl.BlockSpec((None, None, S, 1), lambda b, h: (b, h, 0, 0))],
        out_shape=[jax.ShapeDtypeStruct((Bl, S, D), F32),
                   jax.ShapeDtypeStruct((Bl, HEADS, S, 1), F32)],
        scratch_shapes=[pltpu.VMEM((S, HD), BF16)] * 3,
        compiler_params=_cparams(("parallel", "parallel")),
    )(proj, proj, proj, q_gain, k_gain)


def _sb_bwd(proj, ctot, do, dproj, q_gain, k_gain):
    _, Bl, S, _ = proj.shape

    def body(q_ref, k_ref, v_ref, ct_ref, do_ref, qg_ref, kg_ref, _, dqkv_ref, dqg_ref, dkg_ref,
             qn, kn, vb, dob, dkn, dvn):
        first = jnp.logical_and(pl.program_id(0) == 0, pl.program_id(1) == 0)

        @pl.when(first)
        def _():
            dqg_ref[...] = jnp.zeros_like(dqg_ref)
            dkg_ref[...] = jnp.zeros_like(dkg_ref)

        q = q_ref[...]
        qn[...] = (q * _rms(q) * qg_ref[...]).astype(BF16)
        k = k_ref[...]
        rk = _rms(k)
        kn[...] = (k * rk * kg_ref[...]).astype(BF16)
        vb[...] = v_ref[...].astype(BF16)
        dob[...] = do_ref[...].astype(BF16)
        dkn[...] = jnp.zeros_like(dkn)
        dvn[...] = jnp.zeros_like(dvn)
        rel, u_gt, u_lt = _slab_consts()

        def q_group(g, dqg):
            q0 = pl.multiple_of(g * TQ, TQ)
            qs = pl.ds(q0, TQ)
            qb = qn[qs, :]
            dobb = dob[qs, :]
            ctot = ct_ref[qs, :]

            def slab(k0, passed, e, dq, shift):
                ks = pl.ds(k0, TK)
                kb = kn[ks, :]
                z = _dot(qb, kb, NT) * SCALE
                lb, ls = _log_sigmoid_pair(z)
                if shift is not None:
                    ls = jnp.where(rel < -shift, ls, 0.0)
                passed = passed + jnp.sum(ls, axis=1, keepdims=True)
                w = jnp.exp(lb + _cum2(ls, u_gt) + (ctot - passed))
                if shift is not None:
                    w = jnp.where(rel < -shift, w, 0.0)
                de = w * _dot(dobb, vb[ks, :], NT)
                dvn[ks, :] += _dot(w.astype(BF16), dobb, TN)
                dls = e + _cum2(de, u_lt)
                sg = jnp.exp(lb)
                dz = de * (1.0 - sg) - dls * sg
                if shift is not None:
                    dz = jnp.where(rel < -shift, dz, 0.0)
                dzb = (dz * SCALE).astype(BF16)
                dq = dq + _dot(dzb, kb)
                dkn[ks, :] += _dot(dzb, qb, TN)
                return passed, e + jnp.sum(de, axis=1, keepdims=True), dq

            zc = jnp.zeros((TQ, 1), F32)
            carry = lax.fori_loop(
                0, g * (TQ // TK), lambda n, s: slab(pl.multiple_of(n * TK, TK), s[0], s[1], s[2], None),
                (zc, zc, jnp.zeros((TQ, HD), F32)))
            for shift in range(0, TQ, TK):
                carry = slab(pl.multiple_of(q0 + shift, TK), carry[0], carry[1], carry[2], shift)
            xq = q_ref[qs, :]
            dx, gt = _rms_bwd(xq, _rms(xq), qg_ref[...], carry[2])
            dqkv_ref[0, qs, :] = dx
            return dqg + jnp.sum(gt, axis=0, keepdims=True)

        dqg = lax.fori_loop(0, S // TQ, q_group, jnp.zeros((1, HD), F32))
        dqg_ref[...] += dqg
        dx, gt = _rms_bwd(k, rk, kg_ref[...], dkn[...])
        dqkv_ref[1] = dx
        dkg_ref[...] += jnp.sum(gt, axis=0, keepdims=True)
        dqkv_ref[2] = dvn[...]

    def slot(n):
        return pl.BlockSpec((None, None, S, HD), lambda b, h: (n, b, 0, h))

    head = pl.BlockSpec((None, S, HD), lambda b, h: (b, 0, h))
    gain = pl.BlockSpec((1, HD), lambda b, h: (0, 0))
    return pl.pallas_call(
        body, name="sb_bwd", grid=(Bl, HEADS),
        in_specs=[slot(0), slot(1), slot(2),
                  pl.BlockSpec((None, None, S, 1), lambda b, h: (b, h, 0, 0)), head, gain, gain, ANY],
        out_specs=[pl.BlockSpec((3, None, S, HD), lambda b, h: (0, b, 0, h)), gain, gain],
        out_shape=[jax.ShapeDtypeStruct(dproj.shape, F32),
                   jax.ShapeDtypeStruct((1, HD), F32), jax.ShapeDtypeStruct((1, HD), F32)],
        scratch_shapes=[pltpu.VMEM((S, HD), BF16)] * 4 + [pltpu.VMEM((S, HD), F32)] * 2,
        input_output_aliases={7: 0},
        compiler_params=_cparams(("arbitrary", "arbitrary")),
    )(proj, proj, proj, ctot, do, q_gain, k_gain, dproj)


def _lower_bound(logits):
    l0, l1 = logits[0:1, :], logits[1:2, :]
    m = jnp.maximum(l0, l1)
    e0, e1 = jnp.exp(l0 - m), jnp.exp(l1 - m)
    p0, p1 = e0 / (e0 + e1), e1 / (e0 + e1)
    return (p0 + p1) - p0, p0 * p1


def _hg_gates(qr, fp, lbv):
    sq = _sigmoid(qr)
    sp = _sigmoid(fp)
    sn = 1.0 / (1.0 + jnp.exp(fp))
    f = lbv + (1.0 - lbv) * sp
    return qr * sq, sq, sp, sn, f, (1.0 - lbv) * sn


def _chunk_consts():
    row = lax.broadcasted_iota(jnp.int32, (CH, CH), 0)
    col = lax.broadcasted_iota(jnp.int32, (CH, CH), 1)
    return col <= row, (col <= row).astype(BF16), (col >= row).astype(BF16)


def _hg_fwd(proj, lb_logits):
    _, Bl, S, _ = proj.shape
    nc = S // CH

    def body(q_ref, f_ref, i_ref, lg_ref, o_ref, st_ref):
        lbv, _ = _lower_bound(lg_ref[...])
        tril, t_inc, _ = _chunk_consts()

        def chunk(c, st):
            rs = pl.ds(pl.multiple_of(c * CH, CH), CH)
            qa, _, _, _, f, k = _hg_gates(q_ref[rs, :], f_ref[rs, :], lbv)
            g = jnp.log(f)
            gc = _cum3(t_inc, g)
            gl = jnp.sum(g, axis=0, keepdims=True)
            gm = gc - 0.5 * gl
            qt = (qa * jnp.exp(jnp.minimum(gm, EXP_CLAMP))).astype(BF16)
            kt = (k * jnp.exp(jnp.minimum(-gm, EXP_CLAMP))).astype(BF16)
            a = jnp.where(tril, _dot(qt, kt, NT), 0.0)
            ib = i_ref[rs, :].astype(BF16)
            st_ref[c] = st
            o_ref[rs, :] = (_dot(a.astype(BF16), ib)
                            + _dot((qa * jnp.exp(gc)).astype(BF16), st.astype(BF16), NT))
            kd = (k * jnp.exp(gl - gc)).astype(BF16)
            return st * jnp.exp(gl) + _dot(ib, kd, TN)

        lax.fori_loop(0, nc, chunk, jnp.zeros((HD, HD), F32))

    def slot(n):
        return pl.BlockSpec((None, None, S, HD), lambda b, h: (n, b, 0, h))

    return pl.pallas_call(
        body, name="hg_fwd", grid=(Bl, HEADS),
        in_specs=[slot(0), slot(1), slot(2), pl.BlockSpec((2, HD), lambda b, h: (0, h))],
        out_specs=[pl.BlockSpec((None, S, HD), lambda b, h: (b, 0, h)),
                   pl.BlockSpec((None, None, nc, HD, HD), lambda b, h: (b, h, 0, 0, 0))],
        out_shape=[jax.ShapeDtypeStruct((Bl, S, D), F32),
                   jax.ShapeDtypeStruct((Bl, HEADS, nc, HD, HD), F32)],
        compiler_params=_cparams(("parallel", "parallel")),
    )(proj, proj, proj, lb_logits)


def _hg_bwd(proj, states, do, dproj, lb_logits):
    _, Bl, S, _ = proj.shape
    nc = S // CH

    def body(q_ref, f_ref, i_ref, st_ref, do_ref, lg_ref, _, dqfi_ref, dlb_ref):
        lbv, _ = _lower_bound(lg_ref[...])
        tril, t_inc, t_dec = _chunk_consts()
        last_row = lax.broadcasted_iota(jnp.int32, (CH, HD), 0) == CH - 1

        def chunk(n, carry):
            dst, dlb = carry
            c = nc - 1 - n
            rs = pl.ds(pl.multiple_of(c * CH, CH), CH)
            qr, fp = q_ref[rs, :], f_ref[rs, :]
            qa, sq, sp, sn, f, k = _hg_gates(qr, fp, lbv)
            g = jnp.log(f)
            gc = _cum3(t_inc, g)
            gl = jnp.sum(g, axis=0, keepdims=True)
            gm = gc - 0.5 * gl
            e_q = jnp.exp(jnp.minimum(gm, EXP_CLAMP))
            e_k = jnp.exp(jnp.minimum(-gm, EXP_CLAMP))
            e_g = jnp.exp(gc)
            e_l = jnp.exp(gl - gc)
            e_gl = jnp.exp(gl)
            qt, kt, qg, kd = qa * e_q, k * e_k, qa * e_g, k * e_l
            qtb, ktb, qgb, kdb = (t.astype(BF16) for t in (qt, kt, qg, kd))
            ib = i_ref[rs, :].astype(BF16)
            dob = do_ref[rs, :].astype(BF16)
            st = st_ref[c]
            dstb = dst.astype(BF16)
            ab = jnp.where(tril, _dot(qtb, ktb, NT), 0.0).astype(BF16)
            da = jnp.where(tril, _dot(dob, ib, NT), 0.0)
            dqt = _dot3(da, kt)
            dkt = _dot3(da, qt, TN)
            dqfi_ref[2, rs, :] = _dot(ab, dob, TN) + _dot(kdb, dstb, NT)
            dkd = _dot(ib, dstb)
            dqg = _dot(dob, st.astype(BF16))
            dgl = (jnp.sum(dkd * kd, axis=0, keepdims=True)
                   + jnp.sum(dst * st, axis=0, keepdims=True) * e_gl)
            dgc = dqt * qt - dkt * kt + dqg * qg - dkd * kd
            dgc = dgc + jnp.where(last_row, dgl, 0.0)
            df = _cum3(t_dec, dgc) / f
            t1 = df - (dkt * e_k + dkd * e_l)
            dqfi_ref[1, rs, :] = (1.0 - lbv) * t1 * sp * sn
            dqfi_ref[0, rs, :] = (dqt * e_q + dqg * e_g) * (sq * (1.0 + qr * (1.0 - sq)))
            return (dst * e_gl + _dot(dob, qgb, TN),
                    dlb + jnp.sum(sn * t1, axis=0, keepdims=True))

        _, dlb = lax.fori_loop(0, nc, chunk, (jnp.zeros((HD, HD), F32), jnp.zeros((1, HD), F32)))

        @pl.when(pl.program_id(1) == 0)
        def _():
            dlb_ref[...] = dlb

        @pl.when(pl.program_id(1) != 0)
        def _():
            dlb_ref[...] += dlb

    def slot(n):
        return pl.BlockSpec((None, None, S, HD), lambda h, b: (n, b, 0, h))

    return pl.pallas_call(
        body, name="hg_bwd", grid=(HEADS, Bl),
        in_specs=[slot(0), slot(1), slot(2),
                  pl.BlockSpec((None, None, nc, HD, HD), lambda h, b: (b, h, 0, 0, 0)),
                  pl.BlockSpec((None, S, HD), lambda h, b: (b, 0, h)),
                  pl.BlockSpec((2, HD), lambda h, b: (0, h)), ANY],
        out_specs=[pl.BlockSpec((3, None, S, HD), lambda h, b: (0, b, 0, h)),
                   pl.BlockSpec((1, HD), lambda h, b: (0, h))],
        out_shape=[jax.ShapeDtypeStruct(dproj.shape, F32), jax.ShapeDtypeStruct((1, D), F32)],
        input_output_aliases={6: 0},
        compiler_params=_cparams(("parallel", "arbitrary")),
    )(proj, proj, proj, states, do, lb_logits, dproj)


def _place():
    x, y, c = lax.axis_index("x"), lax.axis_index("y"), lax.axis_index("c")
    return x, y, c, [(1 - x, y), (x, 1 - y), (1 - x, 1 - y)]


def _remote(src, dst, ssem, rsem, dev):
    return pltpu.make_async_remote_copy(src_ref=src, dst_ref=dst, send_sem=ssem, recv_sem=rsem,
                                        device_id=dev, device_id_type=MESH)


def _gather_weights(wp, hn):
    half = WROWS // 2

    def body(wp_ref, hn_ref, wall_ref, hnall_ref, ssem, rsem, lsem):
        x, y, c, chips = _place()
        b = 2 * x + y
        mine = pl.ds(c * half, half)
        other = pl.ds((1 - c) * half, half)
        local = [pltpu.make_async_copy(wp_ref, wall_ref.at[b], lsem.at[0]),
                 pltpu.make_async_copy(hn_ref, hnall_ref.at[b], lsem.at[1])]
        for cp in local:
            cp.start()
        sends = []
        for j, chip in enumerate(chips):
            sends.append(_remote(wp_ref.at[mine], wall_ref.at[b, mine], ssem.at[j], rsem.at[j], (*chip, c)))
            sends.append(_remote(hn_ref, hnall_ref.at[b], ssem.at[6 + j], rsem.at[6 + j], (*chip, c)))
        for cp in sends:
            cp.start()
        for j, (cx, cy) in enumerate(chips):
            landed = wall_ref.at[2 * cx + cy, mine]
            _remote(landed, landed, ssem.at[j], rsem.at[j], (cx, cy, c)).wait_recv()
            fwd = _remote(landed, landed, ssem.at[3 + j], rsem.at[3 + j], (x, y, 1 - c))
            fwd.start()
            sends.append(fwd)
        for j, (cx, cy) in enumerate(chips):
            passed = wall_ref.at[2 * cx + cy, other]
            _remote(passed, passed, ssem.at[3 + j], rsem.at[3 + j], (x, y, 1 - c)).wait_recv()
            row = hnall_ref.at[2 * cx + cy]
            _remote(row, row, ssem.at[6 + j], rsem.at[6 + j], (cx, cy, c)).wait_recv()
        for cp in sends:
            cp.wait_send()
        for cp in local:
            cp.wait()

    return pl.pallas_call(
        body, name="gather_weights", in_specs=[ANY, ANY], out_specs=[ANY, ANY],
        out_shape=[jax.ShapeDtypeStruct((NPROJ, WROWS, D), BF16),
                   jax.ShapeDtypeStruct((NPROJ, 1, D // NPROJ), F32)],
        scratch_shapes=[pltpu.SemaphoreType.DMA((9,)), pltpu.SemaphoreType.DMA((9,)),
                        pltpu.SemaphoreType.DMA((2,))],
    )(wp, hn)


def _pair_exchange(grads, pack):
    ng = len(grads)

    def body(*refs):
        g_refs, pack_ref = refs[:ng], refs[ng]
        r_refs, allp_ref = refs[ng + 1:2 * ng + 1], refs[2 * ng + 1]
        ssem, rsem, psend, precv, lsem = refs[2 * ng + 2:]
        x, y, c, _ = _place()
        me = 4 * x + 2 * y + c
        local = pltpu.make_async_copy(pack_ref, allp_ref.at[me], lsem)
        local.start()
        sends = []
        for t, (g, r) in enumerate(zip(g_refs, r_refs)):
            for n in range(NPROJ):
                k = t * NPROJ + n
                sends.append(_remote(g.at[n, 1 - c], r.at[n], ssem.at[k], rsem.at[k], (x, y, 1 - c)))
        flips = [(fx, fy, fc) for fx in (0, 1) for fy in (0, 1) for fc in (0, 1)][1:]
        peers = [(fx + x - 2 * fx * x, fy + y - 2 * fy * y, fc + c - 2 * fc * c) for fx, fy, fc in flips]
        for m, peer in enumerate(peers):
            sends.append(_remote(pack_ref, allp_ref.at[me], psend.at[m], precv.at[m], peer))
        for cp in sends:
            cp.start()
        for t, r in enumerate(r_refs):
            for n in range(NPROJ):
                k = t * NPROJ + n
                _remote(r.at[n], r.at[n], ssem.at[k], rsem.at[k], (x, y, 1 - c)).wait_recv()
        for m, (px, py, pc) in enumerate(peers):
            row = allp_ref.at[4 * px + 2 * py + pc]
            _remote(row, row, psend.at[m], precv.at[m], (px, py, pc)).wait_recv()
        for cp in sends:
            cp.wait_send()
        local.wait()

    out_shape = [jax.ShapeDtypeStruct((NPROJ,) + g.shape[2:], F32) for g in grads]
    out_shape.append(jax.ShapeDtypeStruct((8,) + pack.shape, F32))
    return pl.pallas_call(
        body, name="pair_exchange", in_specs=[ANY] * (ng + 1), out_specs=[ANY] * (ng + 1),
        out_shape=out_shape,
        scratch_shapes=[pltpu.SemaphoreType.DMA((ng * NPROJ,)), pltpu.SemaphoreType.DMA((ng * NPROJ,)),
                        pltpu.SemaphoreType.DMA((7,)), pltpu.SemaphoreType.DMA((7,)),
                        pltpu.SemaphoreType.DMA],
    )(*grads, pack)


def _chip_exchange(sums):
    ng = len(sums)

    def body(*refs):
        s_refs, r_refs = refs[:ng], refs[ng:2 * ng]
        ssem, rsem, lsem = refs[2 * ng:]
        x, y, c, chips = _place()
        b = 2 * x + y
        local = [pltpu.make_async_copy(s.at[b], r.at[b], lsem.at[t])
                 for t, (s, r) in enumerate(zip(s_refs, r_refs))]
        sends = [_remote(s.at[2 * cx + cy], r.at[b], ssem.at[3 * t + j], rsem.at[3 * t + j], (cx, cy, c))
                 for t, (s, r) in enumerate(zip(s_refs, r_refs)) for j, (cx, cy) in enumerate(chips)]
        for cp in local + sends:
            cp.start()
        for t, r in enumerate(r_refs):
            for j, (cx, cy) in enumerate(chips):
                slot = r.at[2 * cx + cy]
                _remote(slot, slot, ssem.at[3 * t + j], rsem.at[3 * t + j], (cx, cy, c)).wait_recv()
        for cp in sends:
            cp.wait_send()
        for cp in local:
            cp.wait()

    return pl.pallas_call(
        body, name="chip_exchange", in_specs=[ANY] * ng, out_specs=[ANY] * ng,
        out_shape=[jax.ShapeDtypeStruct(s.shape, F32) for s in sums],
        scratch_shapes=[pltpu.SemaphoreType.DMA((3 * ng,)), pltpu.SemaphoreType.DMA((3 * ng,)),
                        pltpu.SemaphoreType.DMA((ng,))],
    )(*sums)


def _sibling_share(halves):
    ng = len(halves)

    def body(*refs):
        h_refs, f_refs = refs[:ng], refs[ng:2 * ng]
        ssem, rsem, lsem = refs[2 * ng:]
        x, y, c, _ = _place()
        local = [pltpu.make_async_copy(h, f.at[c], lsem.at[t]) for t, (h, f) in enumerate(zip(h_refs, f_refs))]
        sends = [_remote(h, f.at[c], ssem.at[t], rsem.at[t], (x, y, 1 - c))
                 for t, (h, f) in enumerate(zip(h_refs, f_refs))]
        for cp in local + sends:
            cp.start()
        for t, f in enumerate(f_refs):
            slot = f.at[1 - c]
            _remote(slot, slot, ssem.at[t], rsem.at[t], (x, y, 1 - c)).wait_recv()
        for cp in sends:
            cp.wait_send()
        for cp in local:
            cp.wait()

    return pl.pallas_call(
        body, name="sibling_share", in_specs=[ANY] * ng, out_specs=[ANY] * ng,
        out_shape=[jax.ShapeDtypeStruct((2,) + h.shape, F32) for h in halves],
        scratch_shapes=[pltpu.SemaphoreType.DMA((ng,)), pltpu.SemaphoreType.DMA((ng,)),
                        pltpu.SemaphoreType.DMA((ng,))],
    )(*halves)


def _pair_add(own, recv, cidx, name):
    R = own.shape[2]
    tr = min(256, R)

    def body(c_ref, a_ref, b_ref, o_ref):
        o_ref[...] = a_ref[...] + b_ref[...]

    return pl.pallas_call(
        body, name=name,
        grid_spec=pltpu.PrefetchScalarGridSpec(
            num_scalar_prefetch=1, grid=(NPROJ, R // tr),
            in_specs=[pl.BlockSpec((None, None, tr, D), lambda n, r, c: (n, c[0], r, 0)),
                      pl.BlockSpec((None, tr, D), lambda n, r, c: (n, r, 0))],
            out_specs=pl.BlockSpec((None, tr, D), lambda n, r, c: (n, r, 0))),
        out_shape=jax.ShapeDtypeStruct(recv.shape, F32),
        compiler_params=_cparams(("parallel", "parallel")),
    )(cidx, own, recv)


def _sum4(parts, name):
    R = parts.shape[1]
    tr = min(256, R)

    def body(p_ref, o_ref):
        o_ref[...] = ((p_ref[0] + p_ref[1]) + p_ref[2]) + p_ref[3]

    return pl.pallas_call(
        body, name=name, grid=(R // tr,),
        in_specs=[pl.BlockSpec((NPROJ, tr, D), lambda r: (0, r, 0))],
        out_specs=pl.BlockSpec((tr, D), lambda r: (r, 0)),
        out_shape=jax.ShapeDtypeStruct((R, D), F32),
        compiler_params=_cparams(("parallel",)),
    )(parts)


def _adamw_math(w, g, m, v):
    m = ADAM_B1 * m + (1.0 - ADAM_B1) * g
    v = ADAM_B2 * v + (1.0 - ADAM_B2) * (g * g)
    m_hat = m / (1.0 - ADAM_B1 ** ADAM_STEP)
    v_hat = v / (1.0 - ADAM_B2 ** ADAM_STEP)
    delta = -ADAM_LR * (m_hat / (jnp.sqrt(v_hat) + ADAM_EPS) + ADAM_WD * w)
    return delta, m, v


def _adamw(w, g, m, v, name):
    R = w.shape[0]
    tr = min(256, R)

    def body(w_ref, g_ref, m_ref, v_ref, d_ref, nm_ref, nv_ref):
        d_ref[...], nm_ref[...], nv_ref[...] = _adamw_math(w_ref[...], g_ref[...], m_ref[...], v_ref[...])

    spec = pl.BlockSpec((tr, D), lambda r: (r, 0))
    return pl.pallas_call(
        body, name=name, grid=(R // tr,), in_specs=[spec] * 4, out_specs=[spec] * 3,
        out_shape=[jax.ShapeDtypeStruct((R, D), F32)] * 3,
        compiler_params=_cparams(("parallel",)),
    )(w, g, m, v)


PACK_ROWS = 8


def _small_update(allp, bidx, logits, weights, moments_m, moments_v):
    shapes = [w.shape for w in weights]
    q4 = D // NPROJ

    def body(b_ref, allp_ref, hgp_ref, lg_ref, *refs):
        w_refs, m_refs, v_refs = refs[0:6], refs[6:12], refs[12:18]
        loss_ref = refs[18]
        g_out, d_out, m_out, v_out = refs[19:25], refs[25:31], refs[31:37], refs[37:43]

        def total(ref, row, lo, hi):
            acc = ref[0, row:row + 1, lo:hi]
            for dev in range(1, 8):
                acc = acc + ref[dev, row:row + 1, lo:hi]
            return acc

        _, pp = _lower_bound(lg_ref[...])
        dlb = total(allp_ref, 2, 0, D)
        grads = [total(allp_ref, 0, 0, D), total(allp_ref, 4, 0, HD), total(allp_ref, 4, HD, 2 * HD),
                 total(hgp_ref, 1, 0, q4), total(allp_ref, 4, 2 * HD, 3 * HD), None]
        loss_ref[...] = (0.5 / D) * jnp.sum(total(allp_ref, 3, 0, D), axis=1, keepdims=True)
        for t in range(6):
            if t < 5:
                rows = [(slice(None), grads[t])]
            else:
                rows = [(slice(0, 1), -pp * dlb), (slice(1, 2), pp * dlb)]
            for rs, g in rows:
                g_out[t][rs, :] = g
                d_out[t][rs, :], m_out[t][rs, :], v_out[t][rs, :] = _adamw_math(
                    w_refs[t][rs, :], g, m_refs[t][rs, :], v_refs[t][rs, :])

    whole = [pl.BlockSpec(s, lambda i, b: (0, 0)) for s in shapes]
    return pl.pallas_call(
        body, name="small_update",
        grid_spec=pltpu.PrefetchScalarGridSpec(
            num_scalar_prefetch=1, grid=(1,),
            in_specs=[pl.BlockSpec((8, PACK_ROWS, D), lambda i, b: (0, 0, 0)),
                      pl.BlockSpec((8, PACK_ROWS, q4), lambda i, b: (0, 0, b[0])),
                      pl.BlockSpec((2, D), lambda i, b: (0, 0))] + whole * 3,
            out_specs=[pl.BlockSpec((1, 1), lambda i, b: (0, 0))] + whole * 4),
        out_shape=[jax.ShapeDtypeStruct((1, 1), F32)] + [jax.ShapeDtypeStruct(s, F32) for s in shapes] * 4,
        compiler_params=_cparams(("arbitrary",)),
    )(bidx, allp, allp, logits, *weights, *moments_m, *moments_v)


def kernel(x, sb_norm, sb_w_in, sb_q_gain, sb_k_gain, sb_w_out, hg_norm, hg_w_in, hg_o_gain, hg_w_out, hg_lb_logits, loss_target, m_sb_norm, m_sb_w_in, m_sb_q_gain, m_sb_k_gain, m_sb_w_out, m_hg_norm, m_hg_w_in, m_hg_o_gain, m_hg_w_out, m_hg_lb_logits, v_sb_norm, v_sb_w_in, v_sb_q_gain, v_sb_k_gain, v_sb_w_out, v_hg_norm, v_hg_w_in, v_hg_o_gain, v_hg_w_out, v_hg_lb_logits):
    Bl, S, _ = x.shape
    T = Bl * S
    q4 = D // NPROJ
    cidx = lax.axis_index("c").astype(jnp.int32).reshape(1)
    bidx = (2 * lax.axis_index("x") + lax.axis_index("y")).astype(jnp.int32).reshape(1)

    wp = jnp.concatenate([sb_w_in[0], hg_w_in[0], sb_w_out[0], hg_w_out[0]], axis=0).astype(BF16)
    wall, hnall = _gather_weights(wp, hg_norm)
    hgn = hnall.reshape(1, D)
    x2 = x.reshape(T, D)
    tgt = loss_target.reshape(T, D)

    def heads(a):
        return a.reshape(a.shape[:-2] + (Bl, S, D))

    def flat(a):
        return a.reshape(a.shape[:-3] + (T, D))

    proj0 = _in_proj_fwd(x2, sb_norm, wall, W_IN_SB, "sb_in_fwd")
    o0, ctot = _sb_fwd(heads(proj0), sb_q_gain, sb_k_gain)
    h1 = _out_proj_fwd(flat(o0), proj0, x2, wall, W_OUT_SB, "sb_out_fwd")
    proj1 = _in_proj_fwd(h1, hgn, wall, W_IN_HG, "hg_in_fwd")
    o1, states = _hg_fwd(heads(proj1), hg_lb_logits)
    dh2, loss_terms = _out_proj_fwd(flat(o1), proj1, h1, wall, W_OUT_HG, "hg_out_fwd",
                                    o_gain=hg_o_gain, target=tgt)

    do1, dproj1, gout_hg, d_ogain = _out_proj_bwd(dh2, flat(o1), proj1, wall, W_OUT_HG, "hg_out_bwd",
                                                  o_gain=hg_o_gain)
    dproj1, dlb = _hg_bwd(heads(proj1), states, heads(do1), heads(dproj1), hg_lb_logits)
    dproj1 = flat(dproj1)
    dh1, d_hgn = _in_proj_bwd_x(dproj1, wall, W_IN_HG, h1, hgn, dh2, "hg_in_bwd_x")
    gin_hg = _in_proj_bwd_w(dproj1, h1, hgn, "hg_in_bwd_w")
    do0, dproj0, gout_sb = _out_proj_bwd(dh1, flat(o0), proj0, wall, W_OUT_SB, "sb_out_bwd")
    dproj0, d_qg, d_kg = _sb_bwd(heads(proj0), ctot, heads(do0), heads(dproj0), sb_q_gain, sb_k_gain)
    dproj0 = flat(dproj0)
    grad_x, d_sbn = _in_proj_bwd_x(dproj0, wall, W_IN_SB, x2, sb_norm, dh1, "sb_in_bwd_x")
    gin_sb = _in_proj_bwd_w(dproj0, x2, sb_norm, "sb_in_bwd_w")

    gains = jnp.concatenate([d_qg, d_kg, d_ogain, jnp.zeros((1, D - 3 * HD), F32)], axis=1)
    pack = jnp.concatenate([d_sbn, d_hgn, dlb, loss_terms, gains, jnp.zeros((3, D), F32)], axis=0)
    big = [gin_sb.reshape(NPROJ, 2, D // 2, D), gin_hg.reshape(NPROJ, 2, D // 2, D),
           gout_sb.reshape(NPROJ, 2, q4 // 2, D), gout_hg.reshape(NPROJ, 2, q4 // 2, D)]
    *recv, allp = _pair_exchange(big, pack)
    names = ["sb_in", "hg_in", "sb_out", "hg_out"]
    sums = [_pair_add(g, r, cidx, "pair_add_" + nm) for g, r, nm in zip(big, recv, names)]
    parts = _chip_exchange(sums)
    halves = [_sum4(p, "chip_sum_" + nm) for p, nm in zip(parts, names)]
    full = [f.reshape(1, -1, D) for f in _sibling_share(halves)]
    g_sb_in, g_hg_in, g_sb_out, g_hg_out = full

    big_w = [sb_w_in, hg_w_in, sb_w_out, hg_w_out]
    big_m = [m_sb_w_in, m_hg_w_in, m_sb_w_out, m_hg_w_out]
    big_v = [v_sb_w_in, v_hg_w_in, v_sb_w_out, v_hg_w_out]
    upd = [_adamw(w[0], g[0], m[0], v[0], "adamw_" + nm)
           for w, g, m, v, nm in zip(big_w, full, big_m, big_v, names)]
    (d_sb_in, nm_sb_in, nv_sb_in), (d_hg_in, nm_hg_in, nv_hg_in), \
        (d_sb_out, nm_sb_out, nv_sb_out), (d_hg_out, nm_hg_out, nv_hg_out) = [
            tuple(a[None] for a in u) for u in upd]

    small = _small_update(
        allp, bidx, hg_lb_logits,
        [sb_norm, sb_q_gain, sb_k_gain, hg_norm, hg_o_gain, hg_lb_logits],
        [m_sb_norm, m_sb_q_gain, m_sb_k_gain, m_hg_norm, m_hg_o_gain, m_hg_lb_logits],
        [v_sb_norm, v_sb_q_gain, v_sb_k_gain, v_hg_norm, v_hg_o_gain, v_hg_lb_logits])
    loss = small[0].reshape(())
    (g_sbn, g_qg, g_kg, g_hgn, g_og, g_lb) = small[1:7]
    (d_sbn2, d_qg2, d_kg2, d_hgn2, d_og2, d_lb2) = small[7:13]
    (nm_sbn, nm_qg, nm_kg, nm_hgn, nm_og, nm_lb) = small[13:19]
    (nv_sbn, nv_qg, nv_kg, nv_hgn, nv_og, nv_lb) = small[19:25]

    return (loss, grad_x.reshape(Bl, S, D),
            g_sbn, g_sb_in, g_qg, g_kg, g_sb_out, g_hgn, g_hg_in, g_og, g_hg_out, g_lb,
            d_sbn2, d_sb_in, d_qg2, d_kg2, d_sb_out, d_hgn2, d_hg_in, d_og2, d_hg_out, d_lb2,
            nm_sbn, nm_sb_in, nm_qg, nm_kg, nm_sb_out, nm_hgn, nm_hg_in, nm_og, nm_hg_out, nm_lb,
            nv_sbn, nv_sb_in, nv_qg, nv_kg, nv_sb_out, nv_hgn, nv_hg_in, nv_og, nv_hg_out, nv_lb)
```

```python
import functools

import jax
import jax.numpy as jnp
from jax import lax
from jax.experimental import pallas as pl
from jax.experimental.pallas import tpu as pltpu

F32 = jnp.float32
BF16 = jnp.bfloat16
MESH = pl.DeviceIdType.MESH
ANY = pl.BlockSpec(memory_space=pl.ANY)

D = 1024
HEADS = 8
HD = 128
NPROJ = 4
RMS_EPS = 1e-6
TQ, TK = 512, 256
CH = 64
CH_LOG2 = 6
GR = 256
SCALE = HD ** -0.5
EXP_CLAMP = 60.0
WROWS = 2 * D + 2 * (D // 4)
W_IN_SB, W_IN_HG = 0, 1
W_OUT_SB, W_OUT_HG = 8, 9

ADAM_LR = 0.001
ADAM_B1 = 0.9
ADAM_B2 = 0.999
ADAM_EPS = 1e-08
ADAM_WD = 0.01
ADAM_STEP = 10

NT = (((1,), (1,)), ((), ()))
TN = (((0,), (0,)), ((), ()))
MIB = 1024 * 1024


def _cparams(sem=None, vmem_mib=40):
    return pltpu.CompilerParams(dimension_semantics=sem, vmem_limit_bytes=vmem_mib * MIB)


def _dot(a, b, dims=None):
    if dims is None:
        return jnp.dot(a, b, preferred_element_type=F32)
    return lax.dot_general(a, b, dims, preferred_element_type=F32)


def _sigmoid(x):
    return 1.0 / (1.0 + jnp.exp(-x))


def _rms(x):
    return lax.rsqrt(jnp.mean(x * x, axis=-1, keepdims=True) + RMS_EPS)


def _rms_bwd(x, r, gain, dy):
    a = dy * gain
    dx = r * a - x * (r * r * r) * jnp.mean(x * a, axis=-1, keepdims=True)
    return dx, dy * (x * r)


def _split2(v):
    hi = v.astype(BF16)
    lo = (v - hi.astype(F32)).astype(BF16)
    return hi, lo


def _cum2(v, u):
    hi, lo = _split2(v)
    return _dot(hi, u) + _dot(lo, u)


def _dot3(a, b, dims=None):
    ah, al = _split2(a)
    bh, bl = _split2(b)
    return _dot(ah, bh, dims) + _dot(ah, bl, dims) + _dot(al, bh, dims)


def _cum3(u, v):
    h1 = v.astype(BF16)
    r1 = v - h1.astype(F32)
    h2 = r1.astype(BF16)
    h3 = (r1 - h2.astype(F32)).astype(BF16)
    return _dot(u, h1) + _dot(u, h2) + _dot(u, h3)


def _in_proj_fwd(h, gain, wall, wblk, name):
    T = h.shape[0]
    tm = min(512, T)

    def body(h_ref, g_ref, w_ref, o_ref, u_ref):
        @pl.when(pl.program_id(1) == 0)
        def _():
            x = h_ref[...]
            u_ref[...] = (x * _rms(x) * g_ref[...]).astype(BF16)

        o_ref[...] = _dot(u_ref[...], w_ref[...])

    return pl.pallas_call(
        body, name=name, grid=(T // tm, NPROJ),
        in_specs=[pl.BlockSpec((tm, D), lambda i, n: (i, 0)),
                  pl.BlockSpec((1, D), lambda i, n: (0, 0)),
                  pl.BlockSpec((None, D, D), lambda i, n: (n, wblk, 0))],
        out_specs=pl.BlockSpec((None, tm, D), lambda i, n: (n, i, 0)),
        out_shape=jax.ShapeDtypeStruct((NPROJ, T, D), F32),
        scratch_shapes=[pltpu.VMEM((tm, D), BF16)],
        compiler_params=_cparams(("parallel", "arbitrary")),
    )(h, gain, wall)


def _head_norm2(x, gain):
    outs = []
    for hh in range(2):
        xs = x[:, hh * HD:(hh + 1) * HD]
        r = _rms(xs)
        outs.append((xs, r))
    return outs


def _out_proj_fwd(o, proj, resid, wall, wblk, name, o_gain=None, target=None):
    T = o.shape[0]
    tm = min(512, T)
    kb = D // NPROJ
    with_loss = target is not None

    def body(*refs):
        if with_loss:
            o_ref, g_ref, r_ref, w_ref, og_ref, t_ref, dh_ref, ls_ref, acc = refs
        else:
            o_ref, g_ref, r_ref, w_ref, h_ref, acc = refs
        i, j = pl.program_id(0), pl.program_id(1)

        @pl.when(j == 0)
        def _():
            acc[...] = jnp.zeros_like(acc)

        x = o_ref[...]
        if with_loss:
            x = jnp.concatenate([xs * r * og_ref[...] for xs, r in _head_norm2(x, None)], axis=1)
        g = g_ref[...]
        a = x * (g * _sigmoid(g))
        acc[...] += _dot(a.astype(BF16), w_ref[...])

        @pl.when(j == NPROJ - 1)
        def _():
            hnew = r_ref[...] + acc[...]
            if with_loss:
                err = hnew - t_ref[...]
                dh_ref[...] = err * (1.0 / D)
                part = jnp.sum(err * err, axis=0, keepdims=True)

                @pl.when(i == 0)
                def _():
                    ls_ref[...] = part

                @pl.when(i != 0)
                def _():
                    ls_ref[...] += part
            else:
                h_ref[...] = hnew

    in_specs = [pl.BlockSpec((tm, kb), lambda i, j: (i, j)),
                pl.BlockSpec((None, tm, kb), lambda i, j: (3, i, j)),
                pl.BlockSpec((tm, D), lambda i, j: (i, 0)),
                pl.BlockSpec((None, kb, D), lambda i, j: (j, wblk, 0))]
    args = [o, proj, resid, wall]
    out_specs = pl.BlockSpec((tm, D), lambda i, j: (i, 0))
    out_shape = jax.ShapeDtypeStruct((T, D), F32)
    if with_loss:
        in_specs += [pl.BlockSpec((1, HD), lambda i, j: (0, 0)),
                     pl.BlockSpec((tm, D), lambda i, j: (i, 0))]
        args += [o_gain, target]
        out_specs = [out_specs, pl.BlockSpec((1, D), lambda i, j: (0, 0))]
        out_shape = [out_shape, jax.ShapeDtypeStruct((1, D), F32)]
    return pl.pallas_call(
        body, name=name, grid=(T // tm, NPROJ), in_specs=in_specs, out_specs=out_specs,
        out_shape=out_shape, scratch_shapes=[pltpu.VMEM((tm, D), F32)],
        compiler_params=_cparams(("arbitrary", "arbitrary")),
    )(*args)


def _out_proj_bwd(dy, o, proj, wall, wblk, name, o_gain=None):
    T = o.shape[0]
    tm = min(512, T)
    kb = D // NPROJ
    normed = o_gain is not None

    def body(*refs):
        if normed:
            dy_ref, o_ref, g_ref, w_ref, og_ref, do_ref, dg_ref, dw_ref, dgain_ref = refs
        else:
            dy_ref, o_ref, g_ref, w_ref, do_ref, dg_ref, dw_ref = refs
        j, i = pl.program_id(0), pl.program_id(1)
        g = g_ref[...]
        s = _sigmoid(g)
        sl = g * s
        x = o_ref[...]
        if normed:
            heads = _head_norm2(x, None)
            on = jnp.concatenate([xs * r * og_ref[...] for xs, r in heads], axis=1)
        else:
            on = x
        dyb = dy_ref[...].astype(BF16)
        part = _dot((on * sl).astype(BF16), dyb, TN)

        @pl.when(i == 0)
        def _():
            dw_ref[...] = part

        @pl.when(i != 0)
        def _():
            dw_ref[...] += part

        da = _dot(dyb, w_ref[...], NT)
        d_on = da * sl
        dg_ref[...] = da * on * (s * (1.0 + g * (1.0 - s)))
        if normed:
            dxs, gsum = [], None
            for hh, (xs, r) in enumerate(heads):
                dx, gt = _rms_bwd(xs, r, og_ref[...], d_on[:, hh * HD:(hh + 1) * HD])
                dxs.append(dx)
                gt = jnp.sum(gt, axis=0, keepdims=True)
                gsum = gt if gsum is None else gsum + gt
            do_ref[...] = jnp.concatenate(dxs, axis=1)
            first = jnp.logical_and(i == 0, j == 0)

            @pl.when(first)
            def _():
                dgain_ref[...] = gsum

            @pl.when(jnp.logical_not(first))
            def _():
                dgain_ref[...] += gsum
        else:
            do_ref[...] = d_on

    in_specs = [pl.BlockSpec((tm, D), lambda j, i: (i, 0)),
                pl.BlockSpec((tm, kb), lambda j, i: (i, j)),
                pl.BlockSpec((None, tm, kb), lambda j, i: (3, i, j)),
                pl.BlockSpec((None, kb, D), lambda j, i: (j, wblk, 0))]
    args = [dy, o, proj, wall]
    out_specs = [pl.BlockSpec((tm, kb), lambda j, i: (i, j)),
                 pl.BlockSpec((None, tm, kb), lambda j, i: (3, i, j)),
                 pl.BlockSpec((None, kb, D), lambda j, i: (j, 0, 0))]
    out_shape = [jax.ShapeDtypeStruct((T, D), F32),
                 jax.ShapeDtypeStruct((NPROJ, T, D), F32),
                 jax.ShapeDtypeStruct((NPROJ, kb, D), F32)]
    if normed:
        in_specs.append(pl.BlockSpec((1, HD), lambda j, i: (0, 0)))
        args.append(o_gain)
        out_specs.append(pl.BlockSpec((1, HD), lambda j, i: (0, 0)))
        out_shape.append(jax.ShapeDtypeStruct((1, HD), F32))
    return pl.pallas_call(
        body, name=name, grid=(NPROJ, T // tm), in_specs=in_specs, out_specs=out_specs,
        out_shape=out_shape, compiler_params=_cparams(("arbitrary", "arbitrary")),
    )(*args)


def _in_proj_bwd_x(dproj, wall, wblk, h, gain, dres, name):
    T = h.shape[0]
    tm = min(512, T)

    def body(d_ref, w_ref, h_ref, g_ref, r_ref, dh_ref, dgain_ref, du):
        i, n = pl.program_id(0), pl.program_id(1)
        part = _dot(d_ref[...].astype(BF16), w_ref[...], NT)

        @pl.when(n == 0)
        def _():
            du[...] = part

        @pl.when(n != 0)
        def _():
            du[...] += part

        @pl.when(n == NPROJ - 1)
        def _():
            x = h_ref[...]
            dx, gt = _rms_bwd(x, _rms(x), g_ref[...], du[...])
            dh_ref[...] = r_ref[...] + dx
            gt = jnp.sum(gt, axis=0, keepdims=True)

            @pl.when(i == 0)
            def _():
                dgain_ref[...] = gt

            @pl.when(i != 0)
            def _():
                dgain_ref[...] += gt

    return pl.pallas_call(
        body, name=name, grid=(T // tm, NPROJ),
        in_specs=[pl.BlockSpec((None, tm, D), lambda i, n: (n, i, 0)),
                  pl.BlockSpec((None, D, D), lambda i, n: (n, wblk, 0)),
                  pl.BlockSpec((tm, D), lambda i, n: (i, 0)),
                  pl.BlockSpec((1, D), lambda i, n: (0, 0)),
                  pl.BlockSpec((tm, D), lambda i, n: (i, 0))],
        out_specs=[pl.BlockSpec((tm, D), lambda i, n: (i, 0)),
                   pl.BlockSpec((1, D), lambda i, n: (0, 0))],
        out_shape=[jax.ShapeDtypeStruct((T, D), F32), jax.ShapeDtypeStruct((1, D), F32)],
        scratch_shapes=[pltpu.VMEM((tm, D), F32)],
        compiler_params=_cparams(("arbitrary", "arbitrary")),
    )(dproj, wall, h, gain, dres)


def _in_proj_bwd_w(dproj, h, gain, name):
    T = h.shape[0]
    tk = min(512, T)

    def body(d_ref, h_ref, g_ref, dw_ref):
        k = pl.program_id(1)
        x = h_ref[...]
        u = (x * _rms(x) * g_ref[...]).astype(BF16)
        part = _dot(u, d_ref[...].astype(BF16), TN)

        @pl.when(k == 0)
        def _():
            dw_ref[...] = part

        @pl.when(k != 0)
        def _():
            dw_ref[...] += part

    return pl.pallas_call(
        body, name=name, grid=(NPROJ, T // tk),
        in_specs=[pl.BlockSpec((None, tk, D), lambda n, k: (n, k, 0)),
                  pl.BlockSpec((tk, D), lambda n, k: (k, 0)),
                  pl.BlockSpec((1, D), lambda n, k: (0, 0))],
        out_specs=pl.BlockSpec((None, D, D), lambda n, k: (n, 0, 0)),
        out_shape=jax.ShapeDtypeStruct((NPROJ, D, D), F32),
        compiler_params=_cparams(("parallel", "arbitrary")),
    )(dproj, h, gain)


def _log_sigmoid_pair(z):
    lb = jnp.minimum(z, 0.0) - jnp.log(1.0 + jnp.exp(-jnp.abs(z)))
    return lb, lb - z


def _slab_consts():
    rel = (lax.broadcasted_iota(jnp.int32, (TQ, TK), 1) - lax.broadcasted_iota(jnp.int32, (TQ, TK), 0))
    j = lax.broadcasted_iota(jnp.int32, (TK, TK), 0)
    s = lax.broadcasted_iota(jnp.int32, (TK, TK), 1)
    return rel, (j > s).astype(BF16), (j < s).astype(BF16)


def _sb_fwd(proj, q_gain, k_gain):
    _, Bl, S, _ = proj.shape

    def body(q_ref, k_ref, v_ref, qg_ref, kg_ref, o_ref, ct_ref, qn, kn, vb):
        q = q_ref[...]
        qn[...] = (q * _rms(q) * qg_ref[...]).astype(BF16)
        k = k_ref[...]
        kn[...] = (k * _rms(k) * kg_ref[...]).astype(BF16)
        vb[...] = v_ref[...].astype(BF16)
        rel, u_gt, _ = _slab_consts()

        def q_group(g, _):
            q0 = pl.multiple_of(g * TQ, TQ)
            qs = pl.ds(q0, TQ)
            qb = qn[qs, :]

            def slab(k0, c, acc, shift):
                ks = pl.ds(k0, TK)
                z = _dot(qb, kn[ks, :], NT) * SCALE
                lb, ls = _log_sigmoid_pair(z)
                if shift is not None:
                    ls = jnp.where(rel < -shift, ls, 0.0)
                w = jnp.exp(lb + _cum2(ls, u_gt) + c)
                if shift is not None:
                    w = jnp.where(rel < -shift, w, 0.0)
                acc = acc + _dot(w.astype(BF16), vb[ks, :])
                return c + jnp.sum(ls, axis=1, keepdims=True), acc

            c, acc = jnp.zeros((TQ, 1), F32), jnp.zeros((TQ, HD), F32)
            for shift in reversed(range(0, TQ, TK)):
                c, acc = slab(pl.multiple_of(q0 + shift, TK), c, acc, shift)
            c, acc = lax.fori_loop(
                0, g * (TQ // TK),
                lambda n, ca: slab(pl.multiple_of(q0 - (n + 1) * TK, TK), ca[0], ca[1], None), (c, acc))
            o_ref[qs, :] = acc
            ct_ref[qs, :] = c
            return 0

        lax.fori_loop(0, S // TQ, q_group, 0)

    def slot(n):
        return pl.BlockSpec((None, None, S, HD), lambda b, h: (n, b, 0, h))

    return pl.pallas_call(
        body, name="sb_fwd", grid=(Bl, HEADS),
        in_specs=[slot(0), slot(1), slot(2),
                  pl.BlockSpec((1, HD), lambda b, h: (0, 0)),
                  pl.BlockSpec((1, HD), lambda b, h: (0, 0))],
        out_specs=[pl.BlockSpec((None, S, HD), lambda b, h: (b, 0, h)),
                   pl.BlockSpec((None, None, S, 1), lambda b, h: (b, h, 0, 0))],
        out_shape=[jax.ShapeDtypeStruct((Bl, S, D), F32),
                   jax.ShapeDtypeStruct((Bl, HEADS, S, 1), F32)],
        scratch_shapes=[pltpu.VMEM((S, HD), BF16)] * 3,
        compiler_params=_cparams(("parallel", "parallel")),
    )(proj, proj, proj, q_gain, k_gain)


def _sb_bwd(proj, ctot, do, dproj, q_gain, k_gain):
    _, Bl, S, _ = proj.shape

    def body(q_ref, k_ref, v_ref, ct_ref, do_ref, qg_ref, kg_ref, _, dqkv_ref, dqg_ref, dkg_ref,
             qn, kn, vb, dob, dkn, dvn):
        first = jnp.logical_and(pl.program_id(0) == 0, pl.program_id(1) == 0)

        @pl.when(first)
        def _():
            dqg_ref[...] = jnp.zeros_like(dqg_ref)
            dkg_ref[...] = jnp.zeros_like(dkg_ref)

        q = q_ref[...]
        qn[...] = (q * _rms(q) * qg_ref[...]).astype(BF16)
        k = k_ref[...]
        rk = _rms(k)
        kn[...] = (k * rk * kg_ref[...]).astype(BF16)
        vb[...] = v_ref[...].astype(BF16)
        dob[...] = do_ref[...].astype(BF16)
        dkn[...] = jnp.zeros_like(dkn)
        dvn[...] = jnp.zeros_like(dvn)
        rel, u_gt, u_lt = _slab_consts()

        def q_group(g, dqg):
            q0 = pl.multiple_of(g * TQ, TQ)
            qs = pl.ds(q0, TQ)
            qb = qn[qs, :]
            dobb = dob[qs, :]
            ctot = ct_ref[qs, :]

            def slab(k0, passed, e, dq, shift):
                ks = pl.ds(k0, TK)
                kb = kn[ks, :]
                z = _dot(qb, kb, NT) * SCALE
                lb, ls = _log_sigmoid_pair(z)
                if shift is not None:
                    ls = jnp.where(rel < -shift, ls, 0.0)
                passed = passed + jnp.sum(ls, axis=1, keepdims=True)
                w = jnp.exp(lb + _cum2(ls, u_gt) + (ctot - passed))
                if shift is not None:
                    w = jnp.where(rel < -shift, w, 0.0)
                de = w * _dot(dobb, vb[ks, :], NT)
                dvn[ks, :] += _dot(w.astype(BF16), dobb, TN)
                dls = e + _cum2(de, u_lt)
                sg = jnp.exp(lb)
                dz = de * (1.0 - sg) - dls * sg
                if shift is not None:
                    dz = jnp.where(rel < -shift, dz, 0.0)
                dzb = (dz * SCALE).astype(BF16)
                dq = dq + _dot(dzb, kb)
                dkn[ks, :] += _dot(dzb, qb, TN)
                return passed, e + jnp.sum(de, axis=1, keepdims=True), dq

            zc = jnp.zeros((TQ, 1), F32)
            carry = lax.fori_loop(
                0, g * (TQ // TK), lambda n, s: slab(pl.multiple_of(n * TK, TK), s[0], s[1], s[2], None),
                (zc, zc, jnp.zeros((TQ, HD), F32)))
            for shift in range(0, TQ, TK):
                carry = slab(pl.multiple_of(q0 + shift, TK), carry[0], carry[1], carry[2], shift)
            xq = q_ref[qs, :]
            dx, gt = _rms_bwd(xq, _rms(xq), qg_ref[...], carry[2])
            dqkv_ref[0, qs, :] = dx
            return dqg + jnp.sum(gt, axis=0, keepdims=True)

        dqg = lax.fori_loop(0, S // TQ, q_group, jnp.zeros((1, HD), F32))
        dqg_ref[...] += dqg
        dx, gt = _rms_bwd(k, rk, kg_ref[...], dkn[...])
        dqkv_ref[1] = dx
        dkg_ref[...] += jnp.sum(gt, axis=0, keepdims=True)
        dqkv_ref[2] = dvn[...]

    def slot(n):
        return pl.BlockSpec((None, None, S, HD), lambda b, h: (n, b, 0, h))

    head = pl.BlockSpec((None, S, HD), lambda b, h: (b, 0, h))
    gain = pl.BlockSpec((1, HD), lambda b, h: (0, 0))
    return pl.pallas_call(
        body, name="sb_bwd", grid=(Bl, HEADS),
        in_specs=[slot(0), slot(1), slot(2),
                  pl.BlockSpec((None, None, S, 1), lambda b, h: (b, h, 0, 0)), head, gain, gain, ANY],
        out_specs=[pl.BlockSpec((3, None, S, HD), lambda b, h: (0, b, 0, h)), gain, gain],
        out_shape=[jax.ShapeDtypeStruct(dproj.shape, F32),
                   jax.ShapeDtypeStruct((1, HD), F32), jax.ShapeDtypeStruct((1, HD), F32)],
        scratch_shapes=[pltpu.VMEM((S, HD), BF16)] * 4 + [pltpu.VMEM((S, HD), F32)] * 2,
        input_output_aliases={7: 0},
        compiler_params=_cparams(("arbitrary", "arbitrary")),
    )(proj, proj, proj, ctot, do, q_gain, k_gain, dproj)


def _lower_bound(logits):
    l0, l1 = logits[0:1, :], logits[1:2, :]
    m = jnp.maximum(l0, l1)
    e0, e1 = jnp.exp(l0 - m), jnp.exp(l1 - m)
    p0, p1 = e0 / (e0 + e1), e1 / (e0 + e1)
    return (p0 + p1) - p0, p0 * p1


def _hg_gates(qr, fp, lbv):
    sq = _sigmoid(qr)
    sp = _sigmoid(fp)
    sn = 1.0 / (1.0 + jnp.exp(fp))
    f = lbv + (1.0 - lbv) * sp
    return qr * sq, sq, sp, sn, f, (1.0 - lbv) * sn


def _group_consts():
    t = lax.broadcasted_iota(jnp.int32, (GR, GR), 0)
    j = lax.broadcasted_iota(jnp.int32, (GR, GR), 1)
    same = lax.shift_right_logical(t, CH_LOG2) == lax.shift_right_logical(j, CH_LOG2)
    tril = jnp.logical_and(same, j <= t)
    return (tril, tril.astype(BF16), jnp.logical_and(same, j >= t).astype(BF16), same.astype(BF16))


def _hg_decays(qa, k, f, t_inc, t_same):
    g = jnp.log(f)
    gc = _cum3(t_inc, g)
    gl = _cum3(t_same, g)
    gm = gc - 0.5 * gl
    e_q = jnp.exp(jnp.minimum(gm, EXP_CLAMP))
    e_k = jnp.exp(jnp.minimum(-gm, EXP_CLAMP))
    e_g = jnp.exp(gc)
    e_l = jnp.exp(gl - gc)
    return qa * e_q, k * e_k, qa * e_g, k * e_l, e_q, e_k, e_g, e_l, jnp.exp(gl)


def _hg_fwd(proj, lb_logits):
    _, Bl, S, _ = proj.shape
    nc = S // CH
    per = GR // CH

    def body(q_ref, f_ref, i_ref, lg_ref, o_ref, st_ref, qg_s, kd_s, egl_s):
        lbv, _ = _lower_bound(lg_ref[...])
        tril, t_inc, _, t_same = _group_consts()

        def intra(n, _):
            for u in range(2):
                rs = pl.ds(pl.multiple_of((2 * n + u) * GR, GR), GR)
                qa, _, _, _, f, k = _hg_gates(q_ref[rs, :], f_ref[rs, :], lbv)
                qt, kt, qg, kd, _, _, _, _, e_gl = _hg_decays(qa, k, f, t_inc, t_same)
                a = jnp.where(tril, _dot(qt.astype(BF16), kt.astype(BF16), NT), 0.0)
                o_ref[rs, :] = _dot(a.astype(BF16), i_ref[rs, :].astype(BF16))
                qg_s[rs, :] = qg.astype(BF16)
                kd_s[rs, :] = kd.astype(BF16)
                egl_s[rs, :] = e_gl
            return 0

        lax.fori_loop(0, S // (2 * GR), intra, 0)

        def scan(n, st):
            for u in range(per):
                c = n * per + u
                rs = pl.ds(pl.multiple_of(c * CH, CH), CH)
                st_ref[c] = st
                o_ref[rs, :] += _dot(qg_s[rs, :], st.astype(BF16), NT)
                st = (st * egl_s[pl.ds(pl.multiple_of(c * CH, CH), 1), :]
                      + _dot(i_ref[rs, :].astype(BF16), kd_s[rs, :], TN))
            return st

        lax.fori_loop(0, nc // per, scan, jnp.zeros((HD, HD), F32))

    def slot(n):
        return pl.BlockSpec((None, None, S, HD), lambda b, h: (n, b, 0, h))

    return pl.pallas_call(
        body, name="hg_fwd", grid=(Bl, HEADS),
        in_specs=[slot(0), slot(1), slot(2), pl.BlockSpec((2, HD), lambda b, h: (0, h))],
        out_specs=[pl.BlockSpec((None, S, HD), lambda b, h: (b, 0, h)),
                   pl.BlockSpec((None, None, nc, HD, HD), lambda b, h: (b, h, 0, 0, 0))],
        out_shape=[jax.ShapeDtypeStruct((Bl, S, D), F32),
                   jax.ShapeDtypeStruct((Bl, HEADS, nc, HD, HD), F32)],
        scratch_shapes=[pltpu.VMEM((S, HD), BF16)] * 2 + [pltpu.VMEM((S, HD), F32)],
        compiler_params=_cparams(("parallel", "parallel")),
    )(proj, proj, proj, lb_logits)


def _hg_bwd(proj, states, do, dproj, lb_logits):
    _, Bl, S, _ = proj.shape
    nc = S // CH
    per = GR // CH

    def body(q_ref, f_ref, i_ref, st_ref, do_ref, lg_ref, _, dqfi_ref, dlb_ref,
             qg_s, kd_s, egl_s, dqg_s, dkd_s, dse_s):
        lbv, _ = _lower_bound(lg_ref[...])
        tril, t_inc, t_dec, t_same = _group_consts()

        def decays(n, _):
            for u in range(2):
                rs = pl.ds(pl.multiple_of((2 * n + u) * GR, GR), GR)
                qa, _, _, _, f, k = _hg_gates(q_ref[rs, :], f_ref[rs, :], lbv)
                _, _, qg, kd, _, _, _, _, e_gl = _hg_decays(qa, k, f, t_inc, t_same)
                qg_s[rs, :] = qg.astype(BF16)
                kd_s[rs, :] = kd.astype(BF16)
                egl_s[rs, :] = e_gl
            return 0

        lax.fori_loop(0, S // (2 * GR), decays, 0)

        def scan(n, dst):
            for u in range(per):
                c = nc - 1 - (n * per + u)
                rs = pl.ds(pl.multiple_of(c * CH, CH), CH)
                st = st_ref[c]
                dstb = dst.astype(BF16)
                dob = do_ref[rs, :].astype(BF16)
                dqg_s[rs, :] = _dot(dob, st.astype(BF16))
                dkd_s[rs, :] = _dot(i_ref[rs, :].astype(BF16), dstb)
                dqfi_ref[2, rs, :] = _dot(kd_s[rs, :], dstb, NT)
                dse_s[rs, :] = jnp.broadcast_to(jnp.sum(dst * st, axis=0, keepdims=True), (CH, HD))
                dst = (dst * egl_s[pl.ds(pl.multiple_of(c * CH, CH), 1), :]
                       + _dot(dob, qg_s[rs, :], TN))
            return dst

        lax.fori_loop(0, nc // per, scan, jnp.zeros((HD, HD), F32))

        def intra(n, dlb):
            for u in range(2):
                rs = pl.ds(pl.multiple_of((2 * n + u) * GR, GR), GR)
                qr, fp = q_ref[rs, :], f_ref[rs, :]
                qa, sq, sp, sn, f, k = _hg_gates(qr, fp, lbv)
                qt, kt, qg, kd, e_q, e_k, e_g, e_l, e_gl = _hg_decays(qa, k, f, t_inc, t_same)
                ib = i_ref[rs, :].astype(BF16)
                dob = do_ref[rs, :].astype(BF16)
                ab = jnp.where(tril, _dot(qt.astype(BF16), kt.astype(BF16), NT), 0.0).astype(BF16)
                da = jnp.where(tril, _dot(dob, ib, NT), 0.0)
                dqt = _dot3(da, kt)
                dkt = _dot3(da, qt, TN)
                dqfi_ref[2, rs, :] += _dot(ab, dob, TN)
                dqg, dkd = dqg_s[rs, :], dkd_s[rs, :]
                dgc = dqt * qt - dkt * kt + dqg * qg - dkd * kd
                dg = _cum3(t_dec, dgc) + _cum3(t_same, dkd * kd) + dse_s[rs, :] * e_gl
                t1 = dg / f - (dkt * e_k + dkd * e_l)
                dqfi_ref[1, rs, :] = (1.0 - lbv) * t1 * sp * sn
                dqfi_ref[0, rs, :] = (dqt * e_q + dqg * e_g) * (sq * (1.0 + qr * (1.0 - sq)))
                dlb = dlb + jnp.sum(sn * t1, axis=0, keepdims=True)
            return dlb

        dlb = lax.fori_loop(0, S // (2 * GR), intra, jnp.zeros((1, HD), F32))

        @pl.when(pl.program_id(1) == 0)
        def _():
            dlb_ref[...] = dlb

        @pl.when(pl.program_id(1) != 0)
        def _():
            dlb_ref[...] += dlb

    def slot(n):
        return pl.BlockSpec((None, None, S, HD), lambda h, b: (n, b, 0, h))

    return pl.pallas_call(
        body, name="hg_bwd", grid=(HEADS, Bl),
        in_specs=[slot(0), slot(1), slot(2),
                  pl.BlockSpec((None, None, nc, HD, HD), lambda h, b: (b, h, 0, 0, 0)),
                  pl.BlockSpec((None, S, HD), lambda h, b: (b, 0, h)),
                  pl.BlockSpec((2, HD), lambda h, b: (0, h)), ANY],
        out_specs=[pl.BlockSpec((3, None, S, HD), lambda h, b: (0, b, 0, h)),
                   pl.BlockSpec((1, HD), lambda h, b: (0, h))],
        out_shape=[jax.ShapeDtypeStruct(dproj.shape, F32), jax.ShapeDtypeStruct((1, D), F32)],
        scratch_shapes=[pltpu.VMEM((S, HD), BF16)] * 2 + [pltpu.VMEM((S, HD), F32)] * 4,
        input_output_aliases={6: 0},
        compiler_params=_cparams(("parallel", "arbitrary")),
    )(proj, proj, proj, states, do, lb_logits, dproj)


def _place():
    x, y, c = lax.axis_index("x"), lax.axis_index("y"), lax.axis_index("c")
    return x, y, c, [(1 - x, y), (x, 1 - y), (1 - x, 1 - y)]


def _remote(src, dst, ssem, rsem, dev):
    return pltpu.make_async_remote_copy(src_ref=src, dst_ref=dst, send_sem=ssem, recv_sem=rsem,
                                        device_id=dev, device_id_type=MESH)


def _gather_weights(wp, hn):
    half = WROWS // 2

    def body(wp_ref, hn_ref, wall_ref, hnall_ref, ssem, rsem):
        x, y, c, chips = _place()
        b = 2 * x + y
        mine = pl.ds(c * half, half)
        other = pl.ds((1 - c) * half, half)
        sends = []
        for j, chip in enumerate(chips):
            sends.append(_remote(wp_ref.at[mine], wall_ref.at[b, mine], ssem.at[j], rsem.at[j], (*chip, c)))
            sends.append(_remote(hn_ref, hnall_ref.at[b], ssem.at[6 + j], rsem.at[6 + j], (*chip, c)))
        for cp in sends:
            cp.start()
        for j, (cx, cy) in enumerate(chips):
            landed = wall_ref.at[2 * cx + cy, mine]
            _remote(landed, landed, ssem.at[j], rsem.at[j], (cx, cy, c)).wait_recv()
            fwd = _remote(landed, landed, ssem.at[3 + j], rsem.at[3 + j], (x, y, 1 - c))
            fwd.start()
            sends.append(fwd)
        for j, (cx, cy) in enumerate(chips):
            passed = wall_ref.at[2 * cx + cy, other]
            _remote(passed, passed, ssem.at[3 + j], rsem.at[3 + j], (x, y, 1 - c)).wait_recv()
            row = hnall_ref.at[2 * cx + cy]
            _remote(row, row, ssem.at[6 + j], rsem.at[6 + j], (cx, cy, c)).wait_recv()
        for cp in sends:
            cp.wait_send()

    return pl.pallas_call(
        body, name="gather_weights", in_specs=[ANY, ANY], out_specs=[ANY, ANY],
        out_shape=[jax.ShapeDtypeStruct((NPROJ, WROWS, D), BF16),
                   jax.ShapeDtypeStruct((NPROJ, 1, D // NPROJ), F32)],
        scratch_shapes=[pltpu.SemaphoreType.DMA((9,)), pltpu.SemaphoreType.DMA((9,))],
    )(wp, hn)


def _pair_exchange(grads, pack):
    ng = len(grads)

    def body(*refs):
        g_refs, pack_ref = refs[:ng], refs[ng]
        r_refs, allp_ref = refs[ng + 1:2 * ng + 1], refs[2 * ng + 1]
        ssem, rsem, psend, precv, lsem = refs[2 * ng + 2:]
        x, y, c, _ = _place()
        me = 4 * x + 2 * y + c
        local = pltpu.make_async_copy(pack_ref, allp_ref.at[me], lsem)
        local.start()
        sends = []
        for t, (g, r) in enumerate(zip(g_refs, r_refs)):
            for n in range(NPROJ):
                k = t * NPROJ + n
                sends.append(_remote(g.at[n, 1 - c], r.at[n], ssem.at[k], rsem.at[k], (x, y, 1 - c)))
        flips = [(fx, fy, fc) for fx in (0, 1) for fy in (0, 1) for fc in (0, 1)][1:]
        peers = [(fx + x - 2 * fx * x, fy + y - 2 * fy * y, fc + c - 2 * fc * c) for fx, fy, fc in flips]
        for m, peer in enumerate(peers):
            sends.append(_remote(pack_ref, allp_ref.at[me], psend.at[m], precv.at[m], peer))
        for cp in sends:
            cp.start()
        for t, r in enumerate(r_refs):
            for n in range(NPROJ):
                k = t * NPROJ + n
                _remote(r.at[n], r.at[n], ssem.at[k], rsem.at[k], (x, y, 1 - c)).wait_recv()
        for m, (px, py, pc) in enumerate(peers):
            row = allp_ref.at[4 * px + 2 * py + pc]
            _remote(row, row, psend.at[m], precv.at[m], (px, py, pc)).wait_recv()
        for cp in sends:
            cp.wait_send()
        local.wait()

    out_shape = [jax.ShapeDtypeStruct((NPROJ,) + g.shape[2:], F32) for g in grads]
    out_shape.append(jax.ShapeDtypeStruct((8,) + pack.shape, F32))
    return pl.pallas_call(
        body, name="pair_exchange", in_specs=[ANY] * (ng + 1), out_specs=[ANY] * (ng + 1),
        out_shape=out_shape,
        scratch_shapes=[pltpu.SemaphoreType.DMA((ng * NPROJ,)), pltpu.SemaphoreType.DMA((ng * NPROJ,)),
                        pltpu.SemaphoreType.DMA((7,)), pltpu.SemaphoreType.DMA((7,)),
                        pltpu.SemaphoreType.DMA],
    )(*grads, pack)


def _chip_exchange(sums):
    ng = len(sums)

    def body(*refs):
        s_refs, r_refs = refs[:ng], refs[ng:2 * ng]
        ssem, rsem = refs[2 * ng:]
        x, y, c, chips = _place()
        b = 2 * x + y
        sends = [_remote(s.at[2 * cx + cy], r.at[b], ssem.at[3 * t + j], rsem.at[3 * t + j], (cx, cy, c))
                 for t, (s, r) in enumerate(zip(s_refs, r_refs)) for j, (cx, cy) in enumerate(chips)]
        for cp in sends:
            cp.start()
        for t, r in enumerate(r_refs):
            for j, (cx, cy) in enumerate(chips):
                slot = r.at[2 * cx + cy]
                _remote(slot, slot, ssem.at[3 * t + j], rsem.at[3 * t + j], (cx, cy, c)).wait_recv()
        for cp in sends:
            cp.wait_send()

    return pl.pallas_call(
        body, name="chip_exchange", in_specs=[ANY] * ng, out_specs=[ANY] * ng,
        out_shape=[jax.ShapeDtypeStruct(s.shape, s.dtype) for s in sums],
        scratch_shapes=[pltpu.SemaphoreType.DMA((3 * ng,)), pltpu.SemaphoreType.DMA((3 * ng,))],
    )(*sums)


def _sibling_share(halves):
    ng = len(halves)

    def body(*refs):
        h_refs, f_refs = refs[:ng], refs[ng:2 * ng]
        ssem, rsem = refs[2 * ng:]
        x, y, c, _ = _place()
        sends = [_remote(h, f, ssem.at[t], rsem.at[t], (x, y, 1 - c))
                 for t, (h, f) in enumerate(zip(h_refs, f_refs))]
        for cp in sends:
            cp.start()
        for t, f in enumerate(f_refs):
            _remote(f, f, ssem.at[t], rsem.at[t], (x, y, 1 - c)).wait_recv()
        for cp in sends:
            cp.wait_send()

    return pl.pallas_call(
        body, name="sibling_share", in_specs=[ANY] * ng, out_specs=[ANY] * ng,
        out_shape=[jax.ShapeDtypeStruct(h.shape, F32) for h in halves],
        scratch_shapes=[pltpu.SemaphoreType.DMA((ng,)), pltpu.SemaphoreType.DMA((ng,))],
    )(*halves)


def _pair_add(own, recv, cidx, name):
    R = own.shape[2]
    tr = min(256, R)

    def body(c_ref, a_ref, b_ref, o_ref):
        o_ref[...] = (a_ref[...] + b_ref[...]).astype(BF16)

    return pl.pallas_call(
        body, name=name,
        grid_spec=pltpu.PrefetchScalarGridSpec(
            num_scalar_prefetch=1, grid=(NPROJ, R // tr),
            in_specs=[pl.BlockSpec((None, None, tr, D), lambda n, r, c: (n, c[0], r, 0)),
                      pl.BlockSpec((None, tr, D), lambda n, r, c: (n, r, 0))],
            out_specs=pl.BlockSpec((None, tr, D), lambda n, r, c: (n, r, 0))),
        out_shape=jax.ShapeDtypeStruct(recv.shape, BF16),
        compiler_params=_cparams(("parallel", "parallel")),
    )(cidx, own, recv)


def _chip_sum(sums, parts, bidx, name):
    R = parts.shape[1]
    tr = min(256, R)

    def body(b_ref, s_ref, p_ref, o_ref):
        acc = None
        for j in range(NPROJ):
            term = jnp.where(b_ref[0] == j, s_ref[...], p_ref[j]).astype(F32)
            acc = term if acc is None else acc + term
        o_ref[...] = acc

    return pl.pallas_call(
        body, name=name,
        grid_spec=pltpu.PrefetchScalarGridSpec(
            num_scalar_prefetch=1, grid=(R // tr,),
            in_specs=[pl.BlockSpec((None, tr, D), lambda r, b: (b[0], r, 0)),
                      pl.BlockSpec((NPROJ, tr, D), lambda r, b: (0, r, 0))],
            out_specs=pl.BlockSpec((tr, D), lambda r, b: (r, 0))),
        out_shape=jax.ShapeDtypeStruct((R, D), F32),
        compiler_params=_cparams(("parallel",)),
    )(bidx, sums, parts)


def _adamw_math(w, g, m, v):
    m = ADAM_B1 * m + (1.0 - ADAM_B1) * g
    v = ADAM_B2 * v + (1.0 - ADAM_B2) * (g * g)
    m_hat = m / (1.0 - ADAM_B1 ** ADAM_STEP)
    v_hat = v / (1.0 - ADAM_B2 ** ADAM_STEP)
    delta = -ADAM_LR * (m_hat / (jnp.sqrt(v_hat) + ADAM_EPS) + ADAM_WD * w)
    return delta, m, v


def _adamw(w, mine, theirs, m, v, cidx, name):
    R = mine.shape[0]
    tr = min(256, R)
    nr = R // tr

    def body(c_ref, w_ref, a_ref, b_ref, m_ref, v_ref, g_ref, d_ref, nm_ref, nv_ref):
        g = jnp.where(pl.program_id(0) == c_ref[0], a_ref[...], b_ref[...])
        g_ref[...] = g
        d_ref[...], nm_ref[...], nv_ref[...] = _adamw_math(w_ref[...], g, m_ref[...], v_ref[...])

    full = pl.BlockSpec((tr, D), lambda h, r, c: (h * nr + r, 0))
    half = pl.BlockSpec((tr, D), lambda h, r, c: (r, 0))
    return pl.pallas_call(
        body, name=name,
        grid_spec=pltpu.PrefetchScalarGridSpec(
            num_scalar_prefetch=1, grid=(2, nr),
            in_specs=[full, half, half, full, full], out_specs=[full] * 4),
        out_shape=[jax.ShapeDtypeStruct(w.shape, F32)] * 4,
        compiler_params=_cparams(("parallel", "parallel")),
    )(cidx, w, mine, theirs, m, v)


PACK_ROWS = 8


def _small_update(allp, bidx, logits, weights, moments_m, moments_v):
    shapes = [w.shape for w in weights]
    q4 = D // NPROJ

    def body(b_ref, allp_ref, hgp_ref, lg_ref, *refs):
        w_refs, m_refs, v_refs = refs[0:6], refs[6:12], refs[12:18]
        loss_ref = refs[18]
        g_out, d_out, m_out, v_out = refs[19:25], refs[25:31], refs[31:37], refs[37:43]

        def total(ref, row, lo, hi):
            acc = ref[0, row:row + 1, lo:hi]
            for dev in range(1, 8):
                acc = acc + ref[dev, row:row + 1, lo:hi]
            return acc

        _, pp = _lower_bound(lg_ref[...])
        dlb = total(allp_ref, 2, 0, D)
        grads = [total(allp_ref, 0, 0, D), total(allp_ref, 4, 0, HD), total(allp_ref, 4, HD, 2 * HD),
                 total(hgp_ref, 1, 0, q4), total(allp_ref, 4, 2 * HD, 3 * HD), None]
        loss_ref[...] = (0.5 / D) * jnp.sum(total(allp_ref, 3, 0, D), axis=1, keepdims=True)
        for t in range(6):
            if t < 5:
                rows = [(slice(None), grads[t])]
            else:
                rows = [(slice(0, 1), -pp * dlb), (slice(1, 2), pp * dlb)]
            for rs, g in rows:
                g_out[t][rs, :] = g
                d_out[t][rs, :], m_out[t][rs, :], v_out[t][rs, :] = _adamw_math(
                    w_refs[t][rs, :], g, m_refs[t][rs, :], v_refs[t][rs, :])

    whole = [pl.BlockSpec(s, lambda i, b: (0, 0)) for s in shapes]
    return pl.pallas_call(
        body, name="small_update",
        grid_spec=pltpu.PrefetchScalarGridSpec(
            num_scalar_prefetch=1, grid=(1,),
            in_specs=[pl.BlockSpec((8, PACK_ROWS, D), lambda i, b: (0, 0, 0)),
                      pl.BlockSpec((8, PACK_ROWS, q4), lambda i, b: (0, 0, b[0])),
                      pl.BlockSpec((2, D), lambda i, b: (0, 0))] + whole * 3,
            out_specs=[pl.BlockSpec((1, 1), lambda i, b: (0, 0))] + whole * 4),
        out_shape=[jax.ShapeDtypeStruct((1, 1), F32)] + [jax.ShapeDtypeStruct(s, F32) for s in shapes] * 4,
        compiler_params=_cparams(("arbitrary",)),
    )(bidx, allp, allp, logits, *weights, *moments_m, *moments_v)


def kernel(x, sb_norm, sb_w_in, sb_q_gain, sb_k_gain, sb_w_out, hg_norm, hg_w_in, hg_o_gain, hg_w_out, hg_lb_logits, loss_target, m_sb_norm, m_sb_w_in, m_sb_q_gain, m_sb_k_gain, m_sb_w_out, m_hg_norm, m_hg_w_in, m_hg_o_gain, m_hg_w_out, m_hg_lb_logits, v_sb_norm, v_sb_w_in, v_sb_q_gain, v_sb_k_gain, v_sb_w_out, v_hg_norm, v_hg_w_in, v_hg_o_gain, v_hg_w_out, v_hg_lb_logits):
    Bl, S, _ = x.shape
    T = Bl * S
    q4 = D // NPROJ
    cidx = lax.axis_index("c").astype(jnp.int32).reshape(1)
    bidx = (2 * lax.axis_index("x") + lax.axis_index("y")).astype(jnp.int32).reshape(1)

    wp = jnp.concatenate([sb_w_in[0], hg_w_in[0], sb_w_out[0], hg_w_out[0]], axis=0).astype(BF16)
    wall, hnall = _gather_weights(wp, hg_norm)
    wall = lax.dynamic_update_slice(wall, wp[None], (bidx[0], 0, 0))
    hnall = lax.dynamic_update_slice(hnall, hg_norm[None], (bidx[0], 0, 0))
    hgn = hnall.reshape(1, D)
    x2 = x.reshape(T, D)
    tgt = loss_target.reshape(T, D)

    def heads(a):
        return a.reshape(a.shape[:-2] + (Bl, S, D))

    def flat(a):
        return a.reshape(a.shape[:-3] + (T, D))

    proj0 = _in_proj_fwd(x2, sb_norm, wall, W_IN_SB, "sb_in_fwd")
    o0, ctot = _sb_fwd(heads(proj0), sb_q_gain, sb_k_gain)
    h1 = _out_proj_fwd(flat(o0), proj0, x2, wall, W_OUT_SB, "sb_out_fwd")
    proj1 = _in_proj_fwd(h1, hgn, wall, W_IN_HG, "hg_in_fwd")
    o1, states = _hg_fwd(heads(proj1), hg_lb_logits)
    dh2, loss_terms = _out_proj_fwd(flat(o1), proj1, h1, wall, W_OUT_HG, "hg_out_fwd",
                                    o_gain=hg_o_gain, target=tgt)

    do1, dproj1, gout_hg, d_ogain = _out_proj_bwd(dh2, flat(o1), proj1, wall, W_OUT_HG, "hg_out_bwd",
                                                  o_gain=hg_o_gain)
    dproj1, dlb = _hg_bwd(heads(proj1), states, heads(do1), heads(dproj1), hg_lb_logits)
    dproj1 = flat(dproj1)
    dh1, d_hgn = _in_proj_bwd_x(dproj1, wall, W_IN_HG, h1, hgn, dh2, "hg_in_bwd_x")
    gin_hg = _in_proj_bwd_w(dproj1, h1, hgn, "hg_in_bwd_w")
    do0, dproj0, gout_sb = _out_proj_bwd(dh1, flat(o0), proj0, wall, W_OUT_SB, "sb_out_bwd")
    dproj0, d_qg, d_kg = _sb_bwd(heads(proj0), ctot, heads(do0), heads(dproj0), sb_q_gain, sb_k_gain)
    dproj0 = flat(dproj0)
    grad_x, d_sbn = _in_proj_bwd_x(dproj0, wall, W_IN_SB, x2, sb_norm, dh1, "sb_in_bwd_x")
    gin_sb = _in_proj_bwd_w(dproj0, x2, sb_norm, "sb_in_bwd_w")

    gains = jnp.concatenate([d_qg, d_kg, d_ogain, jnp.zeros((1, D - 3 * HD), F32)], axis=1)
    pack = jnp.concatenate([d_sbn, d_hgn, dlb, loss_terms, gains, jnp.zeros((3, D), F32)], axis=0)
    big = [gin_sb.reshape(NPROJ, 2, D // 2, D), gin_hg.reshape(NPROJ, 2, D // 2, D),
           gout_sb.reshape(NPROJ, 2, q4 // 2, D), gout_hg.reshape(NPROJ, 2, q4 // 2, D)]
    *recv, allp = _pair_exchange(big, pack)
    names = ["sb_in", "hg_in", "sb_out", "hg_out"]
    sums = [_pair_add(g, r, cidx, "pair_add_" + nm) for g, r, nm in zip(big, recv, names)]
    parts = _chip_exchange(sums)
    halves = [_chip_sum(sm, p, bidx, "chip_sum_" + nm) for sm, p, nm in zip(sums, parts, names)]
    theirs = _sibling_share(halves)

    big_w = [sb_w_in, hg_w_in, sb_w_out, hg_w_out]
    big_m = [m_sb_w_in, m_hg_w_in, m_sb_w_out, m_hg_w_out]
    big_v = [v_sb_w_in, v_hg_w_in, v_sb_w_out, v_hg_w_out]
    upd = [_adamw(w[0], a, b, m[0], v[0], cidx, "adamw_" + nm)
           for w, a, b, m, v, nm in zip(big_w, halves, theirs, big_m, big_v, names)]
    (g_sb_in, d_sb_in, nm_sb_in, nv_sb_in), (g_hg_in, d_hg_in, nm_hg_in, nv_hg_in), \
        (g_sb_out, d_sb_out, nm_sb_out, nv_sb_out), (g_hg_out, d_hg_out, nm_hg_out, nv_hg_out) = [
            tuple(a[None] for a in u) for u in upd]

    small = _small_update(
        allp, bidx, hg_lb_logits,
        [sb_norm, sb_q_gain, sb_k_gain, hg_norm, hg_o_gain, hg_lb_logits],
        [m_sb_norm, m_sb_q_gain, m_sb_k_gain, m_hg_norm, m_hg_o_gain, m_hg_lb_logits],
        [v_sb_norm, v_sb_q_gain, v_sb_k_gain, v_hg_norm, v_hg_o_gain, v_hg_lb_logits])
    loss = small[0].reshape(())
    (g_sbn, g_qg, g_kg, g_hgn, g_og, g_lb) = small[1:7]
    (d_sbn2, d_qg2, d_kg2, d_hgn2, d_og2, d_lb2) = small[7:13]
    (nm_sbn, nm_qg, nm_kg, nm_hgn, nm_og, nm_lb) = small[13:19]
    (nv_sbn, nv_qg, nv_kg, nv_hgn, nv_og, nv_lb) = small[19:25]

    return (loss, grad_x.reshape(Bl, S, D),
            g_sbn, g_sb_in, g_qg, g_kg, g_sb_out, g_hgn, g_hg_in, g_og, g_hg_out, g_lb,
            d_sbn2, d_sb_in, d_qg2, d_kg2, d_sb_out, d_hgn2, d_hg_in, d_og2, d_hg_out, d_lb2,
            nm_sbn, nm_sb_in, nm_qg, nm_kg, nm_sb_out, nm_hgn, nm_hg_in, nm_og, nm_hg_out, nm_lb,
            nv_sbn, nv_sb_in, nv_qg, nv_kg, nv_sb_out, nv_hgn, nv_hg_in, nv_og, nv_hg_out, nv_lb)
```

```python
import functools

import jax
import jax.numpy as jnp
from jax import lax
from jax.experimental import pallas as pl
from jax.experimental.pallas import tpu as pltpu

F32 = jnp.float32
BF16 = jnp.bfloat16
MESH = pl.DeviceIdType.MESH
ANY = pl.BlockSpec(memory_space=pl.ANY)

D = 1024
HEADS = 8
HD = 128
NPROJ = 4
RMS_EPS = 1e-6
TK = 256
CH = 64
CH_LOG2 = 6
GR = 256
SCALE = HD ** -0.5
EXP_CLAMP = 60.0
WROWS = 2 * D + 2 * (D // 4)
W_IN_SB, W_IN_HG = 0, 1
W_OUT_SB, W_OUT_HG = 8, 9

ADAM_LR = 0.001
ADAM_B1 = 0.9
ADAM_B2 = 0.999
ADAM_EPS = 1e-08
ADAM_WD = 0.01
ADAM_STEP = 10

NT = (((1,), (1,)), ((), ()))
TN = (((0,), (0,)), ((), ()))
MIB = 1024 * 1024


def _cparams(sem=None, vmem_mib=40):
    return pltpu.CompilerParams(dimension_semantics=sem, vmem_limit_bytes=vmem_mib * MIB)


def _dot(a, b, dims=None):
    if dims is None:
        return jnp.dot(a, b, preferred_element_type=F32)
    return lax.dot_general(a, b, dims, preferred_element_type=F32)


def _sigmoid(x):
    return 1.0 / (1.0 + jnp.exp(-x))


def _rms(x):
    return lax.rsqrt(jnp.mean(x * x, axis=-1, keepdims=True) + RMS_EPS)


def _rms_bwd(x, r, gain, dy):
    a = dy * gain
    dx = r * a - x * (r * r * r) * jnp.mean(x * a, axis=-1, keepdims=True)
    return dx, dy * (x * r)


def _split2(v):
    hi = v.astype(BF16)
    lo = (v - hi.astype(F32)).astype(BF16)
    return hi, lo


def _cum2(v, u):
    hi, lo = _split2(v)
    return _dot(hi, u) + _dot(lo, u)


def _dot3(a, b, dims=None):
    ah, al = _split2(a)
    bh, bl = _split2(b)
    return _dot(ah, bh, dims) + _dot(ah, bl, dims) + _dot(al, bh, dims)


def _cum3(u, v):
    h1 = v.astype(BF16)
    r1 = v - h1.astype(F32)
    h2 = r1.astype(BF16)
    h3 = (r1 - h2.astype(F32)).astype(BF16)
    return _dot(u, h1) + _dot(u, h2) + _dot(u, h3)


def _in_proj_fwd(h, gain, wall, wblk, name):
    T = h.shape[0]
    tm = min(512, T)

    def body(h_ref, g_ref, w_ref, o_ref, u_ref):
        @pl.when(pl.program_id(1) == 0)
        def _():
            x = h_ref[...]
            u_ref[...] = (x * _rms(x) * g_ref[...]).astype(BF16)

        o_ref[...] = _dot(u_ref[...], w_ref[...])

    return pl.pallas_call(
        body, name=name, grid=(T // tm, NPROJ),
        in_specs=[pl.BlockSpec((tm, D), lambda i, n: (i, 0)),
                  pl.BlockSpec((1, D), lambda i, n: (0, 0)),
                  pl.BlockSpec((None, D, D), lambda i, n: (n, wblk, 0))],
        out_specs=pl.BlockSpec((None, tm, D), lambda i, n: (n, i, 0)),
        out_shape=jax.ShapeDtypeStruct((NPROJ, T, D), F32),
        scratch_shapes=[pltpu.VMEM((tm, D), BF16)],
        compiler_params=_cparams(("parallel", "arbitrary")),
    )(h, gain, wall)


def _head_norm2(x, gain):
    outs = []
    for hh in range(2):
        xs = x[:, hh * HD:(hh + 1) * HD]
        r = _rms(xs)
        outs.append((xs, r))
    return outs


def _out_proj_fwd(o, proj, resid, wall, wblk, name, o_gain=None, target=None):
    T = o.shape[0]
    tm = min(512, T)
    kb = D // NPROJ
    with_loss = target is not None

    def body(*refs):
        if with_loss:
            o_ref, g_ref, r_ref, w_ref, og_ref, t_ref, dh_ref, ls_ref, acc = refs
        else:
            o_ref, g_ref, r_ref, w_ref, h_ref, acc = refs
        i, j = pl.program_id(0), pl.program_id(1)

        @pl.when(j == 0)
        def _():
            acc[...] = jnp.zeros_like(acc)

        x = o_ref[...]
        if with_loss:
            x = jnp.concatenate([xs * r * og_ref[...] for xs, r in _head_norm2(x, None)], axis=1)
        g = g_ref[...]
        a = x * (g * _sigmoid(g))
        acc[...] += _dot(a.astype(BF16), w_ref[...])

        @pl.when(j == NPROJ - 1)
        def _():
            hnew = r_ref[...] + acc[...]
            if with_loss:
                err = hnew - t_ref[...]
                dh_ref[...] = err * (1.0 / D)
                part = jnp.sum(err * err, axis=0, keepdims=True)

                @pl.when(i == 0)
                def _():
                    ls_ref[...] = part

                @pl.when(i != 0)
                def _():
                    ls_ref[...] += part
            else:
                h_ref[...] = hnew

    in_specs = [pl.BlockSpec((tm, kb), lambda i, j: (i, j)),
                pl.BlockSpec((None, tm, kb), lambda i, j: (3, i, j)),
                pl.BlockSpec((tm, D), lambda i, j: (i, 0)),
                pl.BlockSpec((None, kb, D), lambda i, j: (j, wblk, 0))]
    args = [o, proj, resid, wall]
    out_specs = pl.BlockSpec((tm, D), lambda i, j: (i, 0))
    out_shape = jax.ShapeDtypeStruct((T, D), F32)
    if with_loss:
        in_specs += [pl.BlockSpec((1, HD), lambda i, j: (0, 0)),
                     pl.BlockSpec((tm, D), lambda i, j: (i, 0))]
        args += [o_gain, target]
        out_specs = [out_specs, pl.BlockSpec((1, D), lambda i, j: (0, 0))]
        out_shape = [out_shape, jax.ShapeDtypeStruct((1, D), F32)]
    return pl.pallas_call(
        body, name=name, grid=(T // tm, NPROJ), in_specs=in_specs, out_specs=out_specs,
        out_shape=out_shape, scratch_shapes=[pltpu.VMEM((tm, D), F32)],
        compiler_params=_cparams(("arbitrary", "arbitrary")),
    )(*args)


def _out_proj_bwd(dy, o, proj, wall, wblk, name, o_gain=None):
    T = o.shape[0]
    tm = min(512, T)
    kb = D // NPROJ
    normed = o_gain is not None

    def body(*refs):
        if normed:
            dy_ref, o_ref, g_ref, w_ref, og_ref, do_ref, dg_ref, dw_ref, dgain_ref = refs
        else:
            dy_ref, o_ref, g_ref, w_ref, do_ref, dg_ref, dw_ref = refs
        j, i = pl.program_id(0), pl.program_id(1)
        g = g_ref[...]
        s = _sigmoid(g)
        sl = g * s
        x = o_ref[...]
        if normed:
            heads = _head_norm2(x, None)
            on = jnp.concatenate([xs * r * og_ref[...] for xs, r in heads], axis=1)
        else:
            on = x
        dyb = dy_ref[...].astype(BF16)
        part = _dot((on * sl).astype(BF16), dyb, TN)

        @pl.when(i == 0)
        def _():
            dw_ref[...] = part

        @pl.when(i != 0)
        def _():
            dw_ref[...] += part

        da = _dot(dyb, w_ref[...], NT)
        d_on = da * sl
        dg_ref[...] = da * on * (s * (1.0 + g * (1.0 - s)))
        if normed:
            dxs, gsum = [], None
            for hh, (xs, r) in enumerate(heads):
                dx, gt = _rms_bwd(xs, r, og_ref[...], d_on[:, hh * HD:(hh + 1) * HD])
                dxs.append(dx)
                gt = jnp.sum(gt, axis=0, keepdims=True)
                gsum = gt if gsum is None else gsum + gt
            do_ref[...] = jnp.concatenate(dxs, axis=1)
            first = jnp.logical_and(i == 0, j == 0)

            @pl.when(first)
            def _():
                dgain_ref[...] = gsum

            @pl.when(jnp.logical_not(first))
            def _():
                dgain_ref[...] += gsum
        else:
            do_ref[...] = d_on

    in_specs = [pl.BlockSpec((tm, D), lambda j, i: (i, 0)),
                pl.BlockSpec((tm, kb), lambda j, i: (i, j)),
                pl.BlockSpec((None, tm, kb), lambda j, i: (3, i, j)),
                pl.BlockSpec((None, kb, D), lambda j, i: (j, wblk, 0))]
    args = [dy, o, proj, wall]
    out_specs = [pl.BlockSpec((tm, kb), lambda j, i: (i, j)),
                 pl.BlockSpec((None, tm, kb), lambda j, i: (3, i, j)),
                 pl.BlockSpec((None, kb, D), lambda j, i: (j, 0, 0))]
    out_shape = [jax.ShapeDtypeStruct((T, D), F32),
                 jax.ShapeDtypeStruct((NPROJ, T, D), F32),
                 jax.ShapeDtypeStruct((NPROJ, kb, D), F32)]
    if normed:
        in_specs.append(pl.BlockSpec((1, HD), lambda j, i: (0, 0)))
        args.append(o_gain)
        out_specs.append(pl.BlockSpec((1, HD), lambda j, i: (0, 0)))
        out_shape.append(jax.ShapeDtypeStruct((1, HD), F32))
    return pl.pallas_call(
        body, name=name, grid=(NPROJ, T // tm), in_specs=in_specs, out_specs=out_specs,
        out_shape=out_shape, compiler_params=_cparams(("arbitrary", "arbitrary")),
    )(*args)


def _in_proj_bwd_x(dproj, wall, wblk, h, gain, dres, name):
    T = h.shape[0]
    tm = min(512, T)

    def body(d_ref, w_ref, h_ref, g_ref, r_ref, dh_ref, dgain_ref, du):
        i, n = pl.program_id(0), pl.program_id(1)
        part = _dot(d_ref[...].astype(BF16), w_ref[...], NT)

        @pl.when(n == 0)
        def _():
            du[...] = part

        @pl.when(n != 0)
        def _():
            du[...] += part

        @pl.when(n == NPROJ - 1)
        def _():
            x = h_ref[...]
            dx, gt = _rms_bwd(x, _rms(x), g_ref[...], du[...])
            dh_ref[...] = r_ref[...] + dx
            gt = jnp.sum(gt, axis=0, keepdims=True)

            @pl.when(i == 0)
            def _():
                dgain_ref[...] = gt

            @pl.when(i != 0)
            def _():
                dgain_ref[...] += gt

    return pl.pallas_call(
        body, name=name, grid=(T // tm, NPROJ),
        in_specs=[pl.BlockSpec((None, tm, D), lambda i, n: (n, i, 0)),
                  pl.BlockSpec((None, D, D), lambda i, n: (n, wblk, 0)),
                  pl.BlockSpec((tm, D), lambda i, n: (i, 0)),
                  pl.BlockSpec((1, D), lambda i, n: (0, 0)),
                  pl.BlockSpec((tm, D), lambda i, n: (i, 0))],
        out_specs=[pl.BlockSpec((tm, D), lambda i, n: (i, 0)),
                   pl.BlockSpec((1, D), lambda i, n: (0, 0))],
        out_shape=[jax.ShapeDtypeStruct((T, D), F32), jax.ShapeDtypeStruct((1, D), F32)],
        scratch_shapes=[pltpu.VMEM((tm, D), F32)],
        compiler_params=_cparams(("arbitrary", "arbitrary")),
    )(dproj, wall, h, gain, dres)


def _in_proj_bwd_w(dproj, h, gain, name):
    T = h.shape[0]
    tk = min(512, T)

    def body(d_ref, h_ref, g_ref, dw_ref):
        k = pl.program_id(1)
        x = h_ref[...]
        u = (x * _rms(x) * g_ref[...]).astype(BF16)
        part = _dot(u, d_ref[...].astype(BF16), TN)

        @pl.when(k == 0)
        def _():
            dw_ref[...] = part

        @pl.when(k != 0)
        def _():
            dw_ref[...] += part

    return pl.pallas_call(
        body, name=name, grid=(NPROJ, T // tk),
        in_specs=[pl.BlockSpec((None, tk, D), lambda n, k: (n, k, 0)),
                  pl.BlockSpec((tk, D), lambda n, k: (k, 0)),
                  pl.BlockSpec((1, D), lambda n, k: (0, 0))],
        out_specs=pl.BlockSpec((None, D, D), lambda n, k: (n, 0, 0)),
        out_shape=jax.ShapeDtypeStruct((NPROJ, D, D), F32),
        compiler_params=_cparams(("parallel", "arbitrary")),
    )(dproj, h, gain)


def _log_sigmoid_pair(z):
    lb = jnp.minimum(z, 0.0) - jnp.log(1.0 + jnp.exp(-jnp.abs(z)))
    return lb, lb - z


def _slab_consts():
    t = lax.broadcasted_iota(jnp.int32, (TK, TK), 0)
    s = lax.broadcasted_iota(jnp.int32, (TK, TK), 1)
    return s < t, (t > s).astype(BF16), (t < s).astype(BF16)


def _slab_rows(k0, S):
    return [(r0, r1, masked) for r0, r1, masked in ((k0, k0 + TK, True), (k0 + TK, S, False)) if r0 < r1]


def _sb_fwd(proj, q_gain, k_gain):
    _, Bl, S, _ = proj.shape

    def body(q_ref, k_ref, v_ref, qg_ref, kg_ref, o_ref, ct_ref, qn, kn, vb):
        q = q_ref[...]
        qn[...] = (q * _rms(q) * qg_ref[...]).astype(BF16)
        k = k_ref[...]
        kn[...] = (k * _rms(k) * kg_ref[...]).astype(BF16)
        vb[...] = v_ref[...].astype(BF16)
        o_ref[...] = jnp.zeros_like(o_ref)
        ct_ref[...] = jnp.zeros_like(ct_ref)
        tri, u_gt, _ = _slab_consts()
        for k0 in reversed(range(0, S, TK)):
            kb, vbb = kn[k0:k0 + TK, :], vb[k0:k0 + TK, :]
            for r0, r1, masked in _slab_rows(k0, S):
                z = _dot(qn[r0:r1, :], kb, NT) * SCALE
                lb, ls = _log_sigmoid_pair(z)
                if masked:
                    ls = jnp.where(tri, ls, 0.0)
                c = ct_ref[r0:r1, :]
                w = jnp.exp(lb + _cum2(ls, u_gt) + c)
                if masked:
                    w = jnp.where(tri, w, 0.0)
                o_ref[r0:r1, :] += _dot(w.astype(BF16), vbb)
                ct_ref[r0:r1, :] = c + jnp.sum(ls, axis=1, keepdims=True)

    def slot(n):
        return pl.BlockSpec((None, None, S, HD), lambda b, h: (n, b, 0, h))

    return pl.pallas_call(
        body, name="sb_fwd", grid=(Bl, HEADS),
        in_specs=[slot(0), slot(1), slot(2),
                  pl.BlockSpec((1, HD), lambda b, h: (0, 0)),
                  pl.BlockSpec((1, HD), lambda b, h: (0, 0))],
        out_specs=[pl.BlockSpec((None, S, HD), lambda b, h: (b, 0, h)),
                   pl.BlockSpec((None, None, S, 1), lambda b, h: (b, h, 0, 0))],
        out_shape=[jax.ShapeDtypeStruct((Bl, S, D), F32),
                   jax.ShapeDtypeStruct((Bl, HEADS, S, 1), F32)],
        scratch_shapes=[pltpu.VMEM((S, HD), BF16)] * 3,
        compiler_params=_cparams(("parallel", "parallel"), vmem_mib=56),
    )(proj, proj, proj, q_gain, k_gain)


def _sb_bwd(proj, ctot, do, dproj, q_gain, k_gain):
    _, Bl, S, _ = proj.shape

    def body(q_ref, k_ref, v_ref, ct_ref, do_ref, qg_ref, kg_ref, _, dqkv_ref, dqg_ref, dkg_ref,
             qn, kn, vb, dob, dqn, dkn, dvn, passed_s, e_s):
        first = jnp.logical_and(pl.program_id(0) == 0, pl.program_id(1) == 0)

        @pl.when(first)
        def _():
            dqg_ref[...] = jnp.zeros_like(dqg_ref)
            dkg_ref[...] = jnp.zeros_like(dkg_ref)

        q = q_ref[...]
        rq = _rms(q)
        qn[...] = (q * rq * qg_ref[...]).astype(BF16)
        k = k_ref[...]
        rk = _rms(k)
        kn[...] = (k * rk * kg_ref[...]).astype(BF16)
        vb[...] = v_ref[...].astype(BF16)
        dob[...] = do_ref[...].astype(BF16)
        for acc in (dqn, dkn, dvn, passed_s, e_s):
            acc[...] = jnp.zeros_like(acc)
        tri, u_gt, u_lt = _slab_consts()
        for k0 in range(0, S, TK):
            keys = slice(k0, k0 + TK)
            kb, vbb = kn[keys, :], vb[keys, :]
            for r0, r1, masked in _slab_rows(k0, S):
                rows = slice(r0, r1)
                qb, dobb = qn[rows, :], dob[rows, :]
                z = _dot(qb, kb, NT) * SCALE
                lb, ls = _log_sigmoid_pair(z)
                if masked:
                    ls = jnp.where(tri, ls, 0.0)
                passed = passed_s[rows, :] + jnp.sum(ls, axis=1, keepdims=True)
                passed_s[rows, :] = passed
                w = jnp.exp(lb + _cum2(ls, u_gt) + (ct_ref[rows, :] - passed))
                if masked:
                    w = jnp.where(tri, w, 0.0)
                de = w * _dot(dobb, vbb, NT)
                dvn[keys, :] += _dot(w.astype(BF16), dobb, TN)
                e = e_s[rows, :]
                dls = e + _cum2(de, u_lt)
                e_s[rows, :] = e + jnp.sum(de, axis=1, keepdims=True)
                sg = jnp.exp(lb)
                dz = de * (1.0 - sg) - dls * sg
                if masked:
                    dz = jnp.where(tri, dz, 0.0)
                dzb = (dz * SCALE).astype(BF16)
                dqn[rows, :] += _dot(dzb, kb)
                dkn[keys, :] += _dot(dzb, qb, TN)

        dx, gt = _rms_bwd(q, rq, qg_ref[...], dqn[...])
        dqkv_ref[0] = dx
        dqg_ref[...] += jnp.sum(gt, axis=0, keepdims=True)
        dx, gt = _rms_bwd(k, rk, kg_ref[...], dkn[...])
        dqkv_ref[1] = dx
        dkg_ref[...] += jnp.sum(gt, axis=0, keepdims=True)
        dqkv_ref[2] = dvn[...]

    def slot(n):
        return pl.BlockSpec((None, None, S, HD), lambda b, h: (n, b, 0, h))

    head = pl.BlockSpec((None, S, HD), lambda b, h: (b, 0, h))
    gain = pl.BlockSpec((1, HD), lambda b, h: (0, 0))
    return pl.pallas_call(
        body, name="sb_bwd", grid=(Bl, HEADS),
        in_specs=[slot(0), slot(1), slot(2),
                  pl.BlockSpec((None, None, S, 1), lambda b, h: (b, h, 0, 0)), head, gain, gain, ANY],
        out_specs=[pl.BlockSpec((3, None, S, HD), lambda b, h: (0, b, 0, h)), gain, gain],
        out_shape=[jax.ShapeDtypeStruct(dproj.shape, F32),
                   jax.ShapeDtypeStruct((1, HD), F32), jax.ShapeDtypeStruct((1, HD), F32)],
        scratch_shapes=([pltpu.VMEM((S, HD), BF16)] * 4 + [pltpu.VMEM((S, HD), F32)] * 3
                        + [pltpu.VMEM((S, 1), F32)] * 2),
        input_output_aliases={7: 0},
        compiler_params=_cparams(("arbitrary", "arbitrary"), vmem_mib=56),
    )(proj, proj, proj, ctot, do, q_gain, k_gain, dproj)


def _lower_bound(logits):
    l0, l1 = logits[0:1, :], logits[1:2, :]
    m = jnp.maximum(l0, l1)
    e0, e1 = jnp.exp(l0 - m), jnp.exp(l1 - m)
    p0, p1 = e0 / (e0 + e1), e1 / (e0 + e1)
    return (p0 + p1) - p0, p0 * p1


def _hg_gates(qr, fp, lbv):
    sq = _sigmoid(qr)
    sp = _sigmoid(fp)
    sn = 1.0 / (1.0 + jnp.exp(fp))
    f = lbv + (1.0 - lbv) * sp
    return qr * sq, sq, sp, sn, f, (1.0 - lbv) * sn


def _group_consts():
    t = lax.broadcasted_iota(jnp.int32, (GR, GR), 0)
    j = lax.broadcasted_iota(jnp.int32, (GR, GR), 1)
    same = lax.shift_right_logical(t, CH_LOG2) == lax.shift_right_logical(j, CH_LOG2)
    tril = jnp.logical_and(same, j <= t)
    return (tril, tril.astype(BF16), jnp.logical_and(same, j >= t).astype(BF16), same.astype(BF16))


def _hg_decays(qa, k, f, t_inc, t_same):
    g = jnp.log(f)
    gc = _cum3(t_inc, g)
    gl = _cum3(t_same, g)
    gm = gc - 0.5 * gl
    e_q = jnp.exp(jnp.minimum(gm, EXP_CLAMP))
    e_k = jnp.exp(jnp.minimum(-gm, EXP_CLAMP))
    e_g = jnp.exp(gc)
    e_l = jnp.exp(gl - gc)
    return qa * e_q, k * e_k, qa * e_g, k * e_l, e_q, e_k, e_g, e_l, jnp.exp(gl)


def _hg_fwd(proj, lb_logits):
    _, Bl, S, _ = proj.shape
    nc = S // CH
    per = GR // CH

    def body(q_ref, f_ref, i_ref, lg_ref, o_ref, st_ref, qg_s, kd_s, egl_s):
        lbv, _ = _lower_bound(lg_ref[...])
        tril, t_inc, _, t_same = _group_consts()

        def intra(n, _):
            for u in range(2):
                rs = pl.ds(pl.multiple_of((2 * n + u) * GR, GR), GR)
                qa, _, _, _, f, k = _hg_gates(q_ref[rs, :], f_ref[rs, :], lbv)
                qt, kt, qg, kd, _, _, _, _, e_gl = _hg_decays(qa, k, f, t_inc, t_same)
                a = jnp.where(tril, _dot(qt.astype(BF16), kt.astype(BF16), NT), 0.0)
                o_ref[rs, :] = _dot(a.astype(BF16), i_ref[rs, :].astype(BF16))
                qg_s[rs, :] = qg.astype(BF16)
                kd_s[rs, :] = kd.astype(BF16)
                egl_s[rs, :] = e_gl
            return 0

        lax.fori_loop(0, S // (2 * GR), intra, 0)

        def scan(n, st):
            for u in range(per):
                c = n * per + u
                rs = pl.ds(pl.multiple_of(c * CH, CH), CH)
                st_ref[c] = st
                o_ref[rs, :] += _dot(qg_s[rs, :], st.astype(BF16), NT)
                st = (st * egl_s[pl.ds(pl.multiple_of(c * CH, CH), 1), :]
                      + _dot(i_ref[rs, :].astype(BF16), kd_s[rs, :], TN))
            return st

        lax.fori_loop(0, nc // per, scan, jnp.zeros((HD, HD), F32))

    def slot(n):
        return pl.BlockSpec((None, None, S, HD), lambda b, h: (n, b, 0, h))

    return pl.pallas_call(
        body, name="hg_fwd", grid=(Bl, HEADS),
        in_specs=[slot(0), slot(1), slot(2), pl.BlockSpec((2, HD), lambda b, h: (0, h))],
        out_specs=[pl.BlockSpec((None, S, HD), lambda b, h: (b, 0, h)),
                   pl.BlockSpec((None, None, nc, HD, HD), lambda b, h: (b, h, 0, 0, 0))],
        out_shape=[jax.ShapeDtypeStruct((Bl, S, D), F32),
                   jax.ShapeDtypeStruct((Bl, HEADS, nc, HD, HD), F32)],
        scratch_shapes=[pltpu.VMEM((S, HD), BF16)] * 2 + [pltpu.VMEM((S, HD), F32)],
        compiler_params=_cparams(("parallel", "parallel")),
    )(proj, proj, proj, lb_logits)


def _hg_bwd(proj, states, do, dproj, lb_logits):
    _, Bl, S, _ = proj.shape
    nc = S // CH
    per = GR // CH

    def body(q_ref, f_ref, i_ref, st_ref, do_ref, lg_ref, _, dqfi_ref, dlb_ref,
             qg_s, kd_s, egl_s, dqg_s, dkd_s, dse_s):
        lbv, _ = _lower_bound(lg_ref[...])
        tril, t_inc, t_dec, t_same = _group_consts()

        def decays(n, _):
            for u in range(2):
                rs = pl.ds(pl.multiple_of((2 * n + u) * GR, GR), GR)
                qa, _, _, _, f, k = _hg_gates(q_ref[rs, :], f_ref[rs, :], lbv)
                _, _, qg, kd, _, _, _, _, e_gl = _hg_decays(qa, k, f, t_inc, t_same)
                qg_s[rs, :] = qg.astype(BF16)
                kd_s[rs, :] = kd.astype(BF16)
                egl_s[rs, :] = e_gl
            return 0

        lax.fori_loop(0, S // (2 * GR), decays, 0)

        def scan(n, dst):
            for u in range(per):
                c = nc - 1 - (n * per + u)
                rs = pl.ds(pl.multiple_of(c * CH, CH), CH)
                st = st_ref[c]
                dstb = dst.astype(BF16)
                dob = do_ref[rs, :].astype(BF16)
                dqg_s[rs, :] = _dot(dob, st.astype(BF16))
                dkd_s[rs, :] = _dot(i_ref[rs, :].astype(BF16), dstb)
                dqfi_ref[2, rs, :] = _dot(kd_s[rs, :], dstb, NT)
                dse_s[rs, :] = jnp.broadcast_to(jnp.sum(dst * st, axis=0, keepdims=True), (CH, HD))
                dst = (dst * egl_s[pl.ds(pl.multiple_of(c * CH, CH), 1), :]
                       + _dot(dob, qg_s[rs, :], TN))
            return dst

        lax.fori_loop(0, nc // per, scan, jnp.zeros((HD, HD), F32))

        def intra(n, dlb):
            for u in range(2):
                rs = pl.ds(pl.multiple_of((2 * n + u) * GR, GR), GR)
                qr, fp = q_ref[rs, :], f_ref[rs, :]
                qa, sq, sp, sn, f, k = _hg_gates(qr, fp, lbv)
                qt, kt, qg, kd, e_q, e_k, e_g, e_l, e_gl = _hg_decays(qa, k, f, t_inc, t_same)
                ib = i_ref[rs, :].astype(BF16)
                dob = do_ref[rs, :].astype(BF16)
                ab = jnp.where(tril, _dot(qt.astype(BF16), kt.astype(BF16), NT), 0.0).astype(BF16)
                da = jnp.where(tril, _dot(dob, ib, NT), 0.0)
                dqt = _dot3(da, kt)
                dkt = _dot3(da, qt, TN)
                dqfi_ref[2, rs, :] += _dot(ab, dob, TN)
                dqg, dkd = dqg_s[rs, :], dkd_s[rs, :]
                dgc = dqt * qt - dkt * kt + dqg * qg - dkd * kd
                dg = _cum3(t_dec, dgc) + _cum3(t_same, dkd * kd) + dse_s[rs, :] * e_gl
                t1 = dg / f - (dkt * e_k + dkd * e_l)
                dqfi_ref[1, rs, :] = (1.0 - lbv) * t1 * sp * sn
                dqfi_ref[0, rs, :] = (dqt * e_q + dqg * e_g) * (sq * (1.0 + qr * (1.0 - sq)))
                dlb = dlb + jnp.sum(sn * t1, axis=0, keepdims=True)
            return dlb

        dlb = lax.fori_loop(0, S // (2 * GR), intra, jnp.zeros((1, HD), F32))

        @pl.when(pl.program_id(1) == 0)
        def _():
            dlb_ref[...] = dlb

        @pl.when(pl.program_id(1) != 0)
        def _():
            dlb_ref[...] += dlb

    def slot(n):
        return pl.BlockSpec((None, None, S, HD), lambda h, b: (n, b, 0, h))

    return pl.pallas_call(
        body, name="hg_bwd", grid=(HEADS, Bl),
        in_specs=[slot(0), slot(1), slot(2),
                  pl.BlockSpec((None, None, nc, HD, HD), lambda h, b: (b, h, 0, 0, 0)),
                  pl.BlockSpec((None, S, HD), lambda h, b: (b, 0, h)),
                  pl.BlockSpec((2, HD), lambda h, b: (0, h)), ANY],
        out_specs=[pl.BlockSpec((3, None, S, HD), lambda h, b: (0, b, 0, h)),
                   pl.BlockSpec((1, HD), lambda h, b: (0, h))],
        out_shape=[jax.ShapeDtypeStruct(dproj.shape, F32), jax.ShapeDtypeStruct((1, D), F32)],
        scratch_shapes=[pltpu.VMEM((S, HD), BF16)] * 2 + [pltpu.VMEM((S, HD), F32)] * 4,
        input_output_aliases={6: 0},
        compiler_params=_cparams(("parallel", "arbitrary")),
    )(proj, proj, proj, states, do, lb_logits, dproj)


def _place():
    x, y, c = lax.axis_index("x"), lax.axis_index("y"), lax.axis_index("c")
    return x, y, c, [(1 - x, y), (x, 1 - y), (1 - x, 1 - y)]


def _remote(src, dst, ssem, rsem, dev):
    return pltpu.make_async_remote_copy(src_ref=src, dst_ref=dst, send_sem=ssem, recv_sem=rsem,
                                        device_id=dev, device_id_type=MESH)


def _gather_weights(wp, hn):
    half = WROWS // 2

    def body(wp_ref, hn_ref, wall_ref, hnall_ref, ssem, rsem):
        x, y, c, chips = _place()
        b = 2 * x + y
        mine = pl.ds(c * half, half)
        other = pl.ds((1 - c) * half, half)
        sends = []
        for j, chip in enumerate(chips):
            sends.append(_remote(wp_ref.at[mine], wall_ref.at[b, mine], ssem.at[j], rsem.at[j], (*chip, c)))
            sends.append(_remote(hn_ref, hnall_ref.at[b], ssem.at[6 + j], rsem.at[6 + j], (*chip, c)))
        for cp in sends:
            cp.start()
        for j, (cx, cy) in enumerate(chips):
            landed = wall_ref.at[2 * cx + cy, mine]
            _remote(landed, landed, ssem.at[j], rsem.at[j], (cx, cy, c)).wait_recv()
            fwd = _remote(landed, landed, ssem.at[3 + j], rsem.at[3 + j], (x, y, 1 - c))
            fwd.start()
            sends.append(fwd)
        for j, (cx, cy) in enumerate(chips):
            passed = wall_ref.at[2 * cx + cy, other]
            _remote(passed, passed, ssem.at[3 + j], rsem.at[3 + j], (x, y, 1 - c)).wait_recv()
            row = hnall_ref.at[2 * cx + cy]
            _remote(row, row, ssem.at[6 + j], rsem.at[6 + j], (cx, cy, c)).wait_recv()
        for cp in sends:
            cp.wait_send()

    return pl.pallas_call(
        body, name="gather_weights", in_specs=[ANY, ANY], out_specs=[ANY, ANY],
        out_shape=[jax.ShapeDtypeStruct((NPROJ, WROWS, D), BF16),
                   jax.ShapeDtypeStruct((NPROJ, 1, D // NPROJ), F32)],
        scratch_shapes=[pltpu.SemaphoreType.DMA((9,)), pltpu.SemaphoreType.DMA((9,))],
    )(wp, hn)


def _pair_exchange(grads, pack):
    ng = len(grads)

    def body(*refs):
        g_refs, pack_ref = refs[:ng], refs[ng]
        r_refs, allp_ref = refs[ng + 1:2 * ng + 1], refs[2 * ng + 1]
        ssem, rsem, psend, precv, lsem = refs[2 * ng + 2:]
        x, y, c, _ = _place()
        me = 4 * x + 2 * y + c
        local = pltpu.make_async_copy(pack_ref, allp_ref.at[me], lsem)
        local.start()
        sends = []
        for t, (g, r) in enumerate(zip(g_refs, r_refs)):
            for n in range(NPROJ):
                k = t * NPROJ + n
                sends.append(_remote(g.at[n, 1 - c], r.at[n], ssem.at[k], rsem.at[k], (x, y, 1 - c)))
        flips = [(fx, fy, fc) for fx in (0, 1) for fy in (0, 1) for fc in (0, 1)][1:]
        peers = [(fx + x - 2 * fx * x, fy + y - 2 * fy * y, fc + c - 2 * fc * c) for fx, fy, fc in flips]
        for m, peer in enumerate(peers):
            sends.append(_remote(pack_ref, allp_ref.at[me], psend.at[m], precv.at[m], peer))
        for cp in sends:
            cp.start()
        for t, r in enumerate(r_refs):
            for n in range(NPROJ):
                k = t * NPROJ + n
                _remote(r.at[n], r.at[n], ssem.at[k], rsem.at[k], (x, y, 1 - c)).wait_recv()
        for m, (px, py, pc) in enumerate(peers):
            row = allp_ref.at[4 * px + 2 * py + pc]
            _remote(row, row, psend.at[m], precv.at[m], (px, py, pc)).wait_recv()
        for cp in sends:
            cp.wait_send()
        local.wait()

    out_shape = [jax.ShapeDtypeStruct((NPROJ,) + g.shape[2:], F32) for g in grads]
    out_shape.append(jax.ShapeDtypeStruct((8,) + pack.shape, F32))
    return pl.pallas_call(
        body, name="pair_exchange", in_specs=[ANY] * (ng + 1), out_specs=[ANY] * (ng + 1),
        out_shape=out_shape,
        scratch_shapes=[pltpu.SemaphoreType.DMA((ng * NPROJ,)), pltpu.SemaphoreType.DMA((ng * NPROJ,)),
                        pltpu.SemaphoreType.DMA((7,)), pltpu.SemaphoreType.DMA((7,)),
                        pltpu.SemaphoreType.DMA],
    )(*grads, pack)


def _chip_exchange(sums):
    ng = len(sums)

    def body(*refs):
        s_refs, r_refs = refs[:ng], refs[ng:2 * ng]
        ssem, rsem = refs[2 * ng:]
        x, y, c, chips = _place()
        b = 2 * x + y
        sends = [_remote(s.at[2 * cx + cy], r.at[b], ssem.at[3 * t + j], rsem.at[3 * t + j], (cx, cy, c))
                 for t, (s, r) in enumerate(zip(s_refs, r_refs)) for j, (cx, cy) in enumerate(chips)]
        for cp in sends:
            cp.start()
        for t, r in enumerate(r_refs):
            for j, (cx, cy) in enumerate(chips):
                slot = r.at[2 * cx + cy]
                _remote(slot, slot, ssem.at[3 * t + j], rsem.at[3 * t + j], (cx, cy, c)).wait_recv()
        for cp in sends:
            cp.wait_send()

    return pl.pallas_call(
        body, name="chip_exchange", in_specs=[ANY] * ng, out_specs=[ANY] * ng,
        out_shape=[jax.ShapeDtypeStruct(s.shape, s.dtype) for s in sums],
        scratch_shapes=[pltpu.SemaphoreType.DMA((3 * ng,)), pltpu.SemaphoreType.DMA((3 * ng,))],
    )(*sums)


def _sibling_share(halves):
    ng = len(halves)

    def body(*refs):
        h_refs, f_refs = refs[:ng], refs[ng:2 * ng]
        ssem, rsem = refs[2 * ng:]
        x, y, c, _ = _place()
        sends = [_remote(h, f, ssem.at[t], rsem.at[t], (x, y, 1 - c))
                 for t, (h, f) in enumerate(zip(h_refs, f_refs))]
        for cp in sends:
            cp.start()
        for t, f in enumerate(f_refs):
            _remote(f, f, ssem.at[t], rsem.at[t], (x, y, 1 - c)).wait_recv()
        for cp in sends:
            cp.wait_send()

    return pl.pallas_call(
        body, name="sibling_share", in_specs=[ANY] * ng, out_specs=[ANY] * ng,
        out_shape=[jax.ShapeDtypeStruct(h.shape, F32) for h in halves],
        scratch_shapes=[pltpu.SemaphoreType.DMA((ng,)), pltpu.SemaphoreType.DMA((ng,))],
    )(*halves)


def _pair_add(own, recv, cidx, name):
    R = own.shape[2]
    tr = min(256, R)

    def body(c_ref, a_ref, b_ref, o_ref):
        o_ref[...] = (a_ref[...] + b_ref[...]).astype(BF16)

    return pl.pallas_call(
        body, name=name,
        grid_spec=pltpu.PrefetchScalarGridSpec(
            num_scalar_prefetch=1, grid=(NPROJ, R // tr),
            in_specs=[pl.BlockSpec((None, None, tr, D), lambda n, r, c: (n, c[0], r, 0)),
                      pl.BlockSpec((None, tr, D), lambda n, r, c: (n, r, 0))],
            out_specs=pl.BlockSpec((None, tr, D), lambda n, r, c: (n, r, 0))),
        out_shape=jax.ShapeDtypeStruct(recv.shape, BF16),
        compiler_params=_cparams(("parallel", "parallel")),
    )(cidx, own, recv)


def _chip_sum(sums, parts, bidx, name):
    R = parts.shape[1]
    tr = min(256, R)

    def body(b_ref, s_ref, p_ref, o_ref):
        acc = None
        for j in range(NPROJ):
            term = jnp.where(b_ref[0] == j, s_ref[...], p_ref[j]).astype(F32)
            acc = term if acc is None else acc + term
        o_ref[...] = acc

    return pl.pallas_call(
        body, name=name,
        grid_spec=pltpu.PrefetchScalarGridSpec(
            num_scalar_prefetch=1, grid=(R // tr,),
            in_specs=[pl.BlockSpec((None, tr, D), lambda r, b: (b[0], r, 0)),
                      pl.BlockSpec((NPROJ, tr, D), lambda r, b: (0, r, 0))],
            out_specs=pl.BlockSpec((tr, D), lambda r, b: (r, 0))),
        out_shape=jax.ShapeDtypeStruct((R, D), F32),
        compiler_params=_cparams(("parallel",)),
    )(bidx, sums, parts)


def _adamw_math(w, g, m, v):
    m = ADAM_B1 * m + (1.0 - ADAM_B1) * g
    v = ADAM_B2 * v + (1.0 - ADAM_B2) * (g * g)
    m_hat = m / (1.0 - ADAM_B1 ** ADAM_STEP)
    v_hat = v / (1.0 - ADAM_B2 ** ADAM_STEP)
    delta = -ADAM_LR * (m_hat / (jnp.sqrt(v_hat) + ADAM_EPS) + ADAM_WD * w)
    return delta, m, v


def _adamw(w, mine, theirs, m, v, cidx, name):
    R = mine.shape[0]
    tr = min(256, R)
    nr = R // tr

    def body(c_ref, w_ref, a_ref, b_ref, m_ref, v_ref, g_ref, d_ref, nm_ref, nv_ref):
        g = jnp.where(pl.program_id(0) == c_ref[0], a_ref[...], b_ref[...])
        g_ref[...] = g
        d_ref[...], nm_ref[...], nv_ref[...] = _adamw_math(w_ref[...], g, m_ref[...], v_ref[...])

    full = pl.BlockSpec((tr, D), lambda h, r, c: (h * nr + r, 0))
    half = pl.BlockSpec((tr, D), lambda h, r, c: (r, 0))
    return pl.pallas_call(
        body, name=name,
        grid_spec=pltpu.PrefetchScalarGridSpec(
            num_scalar_prefetch=1, grid=(2, nr),
            in_specs=[full, half, half, full, full], out_specs=[full] * 4),
        out_shape=[jax.ShapeDtypeStruct(w.shape, F32)] * 4,
        compiler_params=_cparams(("parallel", "parallel")),
    )(cidx, w, mine, theirs, m, v)


PACK_ROWS = 8


def _small_update(allp, bidx, logits, weights, moments_m, moments_v):
    shapes = [w.shape for w in weights]
    q4 = D // NPROJ

    def body(b_ref, allp_ref, hgp_ref, lg_ref, *refs):
        w_refs, m_refs, v_refs = refs[0:6], refs[6:12], refs[12:18]
        loss_ref = refs[18]
        g_out, d_out, m_out, v_out = refs[19:25], refs[25:31], refs[31:37], refs[37:43]

        def total(ref, row, lo, hi):
            acc = ref[0, row:row + 1, lo:hi]
            for dev in range(1, 8):
                acc = acc + ref[dev, row:row + 1, lo:hi]
            return acc

        _, pp = _lower_bound(lg_ref[...])
        dlb = total(allp_ref, 2, 0, D)
        grads = [total(allp_ref, 0, 0, D), total(allp_ref, 4, 0, HD), total(allp_ref, 4, HD, 2 * HD),
                 total(hgp_ref, 1, 0, q4), total(allp_ref, 4, 2 * HD, 3 * HD), None]
        loss_ref[...] = (0.5 / D) * jnp.sum(total(allp_ref, 3, 0, D), axis=1, keepdims=True)
        for t in range(6):
            if t < 5:
                rows = [(slice(None), grads[t])]
            else:
                rows = [(slice(0, 1), -pp * dlb), (slice(1, 2), pp * dlb)]
            for rs, g in rows:
                g_out[t][rs, :] = g
                d_out[t][rs, :], m_out[t][rs, :], v_out[t][rs, :] = _adamw_math(
                    w_refs[t][rs, :], g, m_refs[t][rs, :], v_refs[t][rs, :])

    whole = [pl.BlockSpec(s, lambda i, b: (0, 0)) for s in shapes]
    return pl.pallas_call(
        body, name="small_update",
        grid_spec=pltpu.PrefetchScalarGridSpec(
            num_scalar_prefetch=1, grid=(1,),
            in_specs=[pl.BlockSpec((8, PACK_ROWS, D), lambda i, b: (0, 0, 0)),
                      pl.BlockSpec((8, PACK_ROWS, q4), lambda i, b: (0, 0, b[0])),
                      pl.BlockSpec((2, D), lambda i, b: (0, 0))] + whole * 3,
            out_specs=[pl.BlockSpec((1, 1), lambda i, b: (0, 0))] + whole * 4),
        out_shape=[jax.ShapeDtypeStruct((1, 1), F32)] + [jax.ShapeDtypeStruct(s, F32) for s in shapes] * 4,
        compiler_params=_cparams(("arbitrary",)),
    )(bidx, allp, allp, logits, *weights, *moments_m, *moments_v)


def kernel(x, sb_norm, sb_w_in, sb_q_gain, sb_k_gain, sb_w_out, hg_norm, hg_w_in, hg_o_gain, hg_w_out, hg_lb_logits, loss_target, m_sb_norm, m_sb_w_in, m_sb_q_gain, m_sb_k_gain, m_sb_w_out, m_hg_norm, m_hg_w_in, m_hg_o_gain, m_hg_w_out, m_hg_lb_logits, v_sb_norm, v_sb_w_in, v_sb_q_gain, v_sb_k_gain, v_sb_w_out, v_hg_norm, v_hg_w_in, v_hg_o_gain, v_hg_w_out, v_hg_lb_logits):
    Bl, S, _ = x.shape
    T = Bl * S
    q4 = D // NPROJ
    cidx = lax.axis_index("c").astype(jnp.int32).reshape(1)
    bidx = (2 * lax.axis_index("x") + lax.axis_index("y")).astype(jnp.int32).reshape(1)

    wp = jnp.concatenate([sb_w_in[0], hg_w_in[0], sb_w_out[0], hg_w_out[0]], axis=0).astype(BF16)
    wall, hnall = _gather_weights(wp, hg_norm)
    wall = lax.dynamic_update_slice(wall, wp[None], (bidx[0], 0, 0))
    hnall = lax.dynamic_update_slice(hnall, hg_norm[None], (bidx[0], 0, 0))
    hgn = hnall.reshape(1, D)
    x2 = x.reshape(T, D)
    tgt = loss_target.reshape(T, D)

    def heads(a):
        return a.reshape(a.shape[:-2] + (Bl, S, D))

    def flat(a):
        return a.reshape(a.shape[:-3] + (T, D))

    proj0 = _in_proj_fwd(x2, sb_norm, wall, W_IN_SB, "sb_in_fwd")
    o0, ctot = _sb_fwd(heads(proj0), sb_q_gain, sb_k_gain)
    h1 = _out_proj_fwd(flat(o0), proj0, x2, wall, W_OUT_SB, "sb_out_fwd")
    proj1 = _in_proj_fwd(h1, hgn, wall, W_IN_HG, "hg_in_fwd")
    o1, states = _hg_fwd(heads(proj1), hg_lb_logits)
    dh2, loss_terms = _out_proj_fwd(flat(o1), proj1, h1, wall, W_OUT_HG, "hg_out_fwd",
                                    o_gain=hg_o_gain, target=tgt)

    do1, dproj1, gout_hg, d_ogain = _out_proj_bwd(dh2, flat(o1), proj1, wall, W_OUT_HG, "hg_out_bwd",
                                                  o_gain=hg_o_gain)
    dproj1, dlb = _hg_bwd(heads(proj1), states, heads(do1), heads(dproj1), hg_lb_logits)
    dproj1 = flat(dproj1)
    dh1, d_hgn = _in_proj_bwd_x(dproj1, wall, W_IN_HG, h1, hgn, dh2, "hg_in_bwd_x")
    gin_hg = _in_proj_bwd_w(dproj1, h1, hgn, "hg_in_bwd_w")
    do0, dproj0, gout_sb = _out_proj_bwd(dh1, flat(o0), proj0, wall, W_OUT_SB, "sb_out_bwd")
    dproj0, d_qg, d_kg = _sb_bwd(heads(proj0), ctot, heads(do0), heads(dproj0), sb_q_gain, sb_k_gain)
    dproj0 = flat(dproj0)
    grad_x, d_sbn = _in_proj_bwd_x(dproj0, wall, W_IN_SB, x2, sb_norm, dh1, "sb_in_bwd_x")
    gin_sb = _in_proj_bwd_w(dproj0, x2, sb_norm, "sb_in_bwd_w")

    gains = jnp.concatenate([d_qg, d_kg, d_ogain, jnp.zeros((1, D - 3 * HD), F32)], axis=1)
    pack = jnp.concatenate([d_sbn, d_hgn, dlb, loss_terms, gains, jnp.zeros((3, D), F32)], axis=0)
    big = [gin_sb.reshape(NPROJ, 2, D // 2, D), gin_hg.reshape(NPROJ, 2, D // 2, D),
           gout_sb.reshape(NPROJ, 2, q4 // 2, D), gout_hg.reshape(NPROJ, 2, q4 // 2, D)]
    *recv, allp = _pair_exchange(big, pack)
    names = ["sb_in", "hg_in", "sb_out", "hg_out"]
    sums = [_pair_add(g, r, cidx, "pair_add_" + nm) for g, r, nm in zip(big, recv, names)]
    parts = _chip_exchange(sums)
    halves = [_chip_sum(sm, p, bidx, "chip_sum_" + nm) for sm, p, nm in zip(sums, parts, names)]
    theirs = _sibling_share(halves)

    big_w = [sb_w_in, hg_w_in, sb_w_out, hg_w_out]
    big_m = [m_sb_w_in, m_hg_w_in, m_sb_w_out, m_hg_w_out]
    big_v = [v_sb_w_in, v_hg_w_in, v_sb_w_out, v_hg_w_out]
    upd = [_adamw(w[0], a, b, m[0], v[0], cidx, "adamw_" + nm)
           for w, a, b, m, v, nm in zip(big_w, halves, theirs, big_m, big_v, names)]
    (g_sb_in, d_sb_in, nm_sb_in, nv_sb_in), (g_hg_in, d_hg_in, nm_hg_in, nv_hg_in), \
        (g_sb_out, d_sb_out, nm_sb_out, nv_sb_out), (g_hg_out, d_hg_out, nm_hg_out, nv_hg_out) = [
            tuple(a[None] for a in u) for u in upd]

    small = _small_update(
        allp, bidx, hg_lb_logits,
        [sb_norm, sb_q_gain, sb_k_gain, hg_norm, hg_o_gain, hg_lb_logits],
        [m_sb_norm, m_sb_q_gain, m_sb_k_gain, m_hg_norm, m_hg_o_gain, m_hg_lb_logits],
        [v_sb_norm, v_sb_q_gain, v_sb_k_gain, v_hg_norm, v_hg_o_gain, v_hg_lb_logits])
    loss = small[0].reshape(())
    (g_sbn, g_qg, g_kg, g_hgn, g_og, g_lb) = small[1:7]
    (d_sbn2, d_qg2, d_kg2, d_hgn2, d_og2, d_lb2) = small[7:13]
    (nm_sbn, nm_qg, nm_kg, nm_hgn, nm_og, nm_lb) = small[13:19]
    (nv_sbn, nv_qg, nv_kg, nv_hgn, nv_og, nv_lb) = small[19:25]

    return (loss, grad_x.reshape(Bl, S, D),
            g_sbn, g_sb_in, g_qg, g_kg, g_sb_out, g_hgn, g_hg_in, g_og, g_hg_out, g_lb,
            d_sbn2, d_sb_in, d_qg2, d_kg2, d_sb_out, d_hgn2, d_hg_in, d_og2, d_hg_out, d_lb2,
            nm_sbn, nm_sb_in, nm_qg, nm_kg, nm_sb_out, nm_hgn, nm_hg_in, nm_og, nm_hg_out, nm_lb,
            nv_sbn, nv_sb_in, nv_qg, nv_kg, nv_sb_out, nv_hgn, nv_hg_in, nv_og, nv_hg_out, nv_lb)
```

```python
import functools

import jax
import jax.numpy as jnp
from jax import lax
from jax.experimental import pallas as pl
from jax.experimental.pallas import tpu as pltpu

F32 = jnp.float32
BF16 = jnp.bfloat16
MESH = pl.DeviceIdType.MESH
ANY = pl.BlockSpec(memory_space=pl.ANY)

D = 1024
HEADS = 8
HD = 128
NPROJ = 4
RMS_EPS = 1e-6
TK = 256
CH = 64
CH_LOG2 = 6
GR = 256
SCALE = HD ** -0.5
EXP_CLAMP = 60.0
WROWS = 2 * D + 2 * (D // 4)
W_IN_SB, W_IN_HG = 0, 1
W_OUT_SB, W_OUT_HG = 8, 9

ADAM_LR = 0.001
ADAM_B1 = 0.9
ADAM_B2 = 0.999
ADAM_EPS = 1e-08
ADAM_WD = 0.01
ADAM_STEP = 10

NT = (((1,), (1,)), ((), ()))
TN = (((0,), (0,)), ((), ()))
MIB = 1024 * 1024


def _cparams(sem=None, vmem_mib=40):
    return pltpu.CompilerParams(dimension_semantics=sem, vmem_limit_bytes=vmem_mib * MIB)


def _dot(a, b, dims=None):
    if dims is None:
        return jnp.dot(a, b, preferred_element_type=F32)
    return lax.dot_general(a, b, dims, preferred_element_type=F32)


def _sigmoid(x):
    return 1.0 / (1.0 + jnp.exp(-x))


def _rms(x):
    return lax.rsqrt(jnp.mean(x * x, axis=-1, keepdims=True) + RMS_EPS)


def _rms_bwd(x, r, gain, dy):
    a = dy * gain
    dx = r * a - x * (r * r * r) * jnp.mean(x * a, axis=-1, keepdims=True)
    return dx, dy * (x * r)


def _split2(v):
    hi = v.astype(BF16)
    lo = (v - hi.astype(F32)).astype(BF16)
    return hi, lo


def _cum2(v, u):
    hi, lo = _split2(v)
    return _dot(hi, u) + _dot(lo, u)


def _dot3(a, b, dims=None):
    ah, al = _split2(a)
    bh, bl = _split2(b)
    return _dot(ah, bh, dims) + _dot(ah, bl, dims) + _dot(al, bh, dims)


def _cum2l(u, v):
    hi, lo = _split2(v)
    return _dot(u, hi) + _dot(u, lo)


def _in_proj_fwd(h, gain, wall, wblk, name):
    T = h.shape[0]
    tm = min(512, T)

    def body(h_ref, g_ref, w_ref, o_ref, u_s):
        rows = pl.ds(pl.multiple_of(pl.program_id(1) * tm, tm), tm)

        @pl.when(pl.program_id(0) == 0)
        def _():
            x = h_ref[...]
            u_s[rows, :] = (x * _rms(x) * g_ref[...]).astype(BF16)

        o_ref[...] = _dot(u_s[rows, :], w_ref[...])

    return pl.pallas_call(
        body, name=name, grid=(NPROJ, T // tm),
        in_specs=[pl.BlockSpec((tm, D), lambda n, i: (jnp.where(n == 0, i, 0), 0)),
                  pl.BlockSpec((1, D), lambda n, i: (0, 0)),
                  pl.BlockSpec((None, D, D), lambda n, i: (n, wblk, 0))],
        out_specs=pl.BlockSpec((None, tm, D), lambda n, i: (n, i, 0)),
        out_shape=jax.ShapeDtypeStruct((NPROJ, T, D), F32),
        scratch_shapes=[pltpu.VMEM((T, D), BF16)],
        compiler_params=_cparams(("arbitrary", "arbitrary")),
    )(h, gain, wall)


def _head_norm(x):
    outs = []
    for hh in range(x.shape[1] // HD):
        xs = x[:, hh * HD:(hh + 1) * HD]
        outs.append((xs, _rms(xs)))
    return outs


def _w_out_specs(wblk):
    kb = D // NPROJ
    return [pl.BlockSpec((None, kb, D), functools.partial(lambda j, i: (j, wblk, 0), j)) for j in range(NPROJ)]


def _out_proj_fwd(o, proj, resid, wall, wblk, name, o_gain=None, target=None):
    T = o.shape[0]
    tm = min(512, T)
    kb = D // NPROJ
    with_loss = target is not None

    def body(*refs):
        o_ref, g_ref, r_ref = refs[:3]
        w_refs = refs[3:3 + NPROJ]
        if with_loss:
            og_ref, t_ref, dh_ref, ls_ref = refs[3 + NPROJ:]
        else:
            h_ref, = refs[3 + NPROJ:]
        x = o_ref[...]
        if with_loss:
            x = jnp.concatenate([xs * r * og_ref[...] for xs, r in _head_norm(x)], axis=1)
        g = g_ref[...]
        a = (x * (g * _sigmoid(g))).astype(BF16)
        hnew = r_ref[...]
        for j in range(NPROJ):
            hnew = hnew + _dot(a[:, j * kb:(j + 1) * kb], w_refs[j][...])
        if with_loss:
            err = hnew - t_ref[...]
            dh_ref[...] = err * (1.0 / D)
            part = jnp.sum(err * err, axis=0, keepdims=True)

            @pl.when(pl.program_id(0) == 0)
            def _():
                ls_ref[...] = part

            @pl.when(pl.program_id(0) != 0)
            def _():
                ls_ref[...] += part
        else:
            h_ref[...] = hnew

    tile = pl.BlockSpec((tm, D), lambda i: (i, 0))
    in_specs = [tile, pl.BlockSpec((None, tm, D), lambda i: (3, i, 0)), tile] + _w_out_specs(wblk)
    args = [o, proj, resid] + [wall] * NPROJ
    out_specs = tile
    out_shape = jax.ShapeDtypeStruct((T, D), F32)
    if with_loss:
        in_specs += [pl.BlockSpec((1, HD), lambda i: (0, 0)), tile]
        args += [o_gain, target]
        out_specs = [tile, pl.BlockSpec((1, D), lambda i: (0, 0))]
        out_shape = [out_shape, jax.ShapeDtypeStruct((1, D), F32)]
    return pl.pallas_call(
        body, name=name, grid=(T // tm,), in_specs=in_specs, out_specs=out_specs,
        out_shape=out_shape, compiler_params=_cparams(("arbitrary",)),
    )(*args)


def _out_proj_bwd(dy, o, proj, wall, wblk, name, o_gain=None):
    T = o.shape[0]
    tm = min(512, T)
    kb = D // NPROJ
    normed = o_gain is not None

    def body(*refs):
        dy_ref, o_ref, g_ref = refs[:3]
        w_refs = refs[3:3 + NPROJ]
        if normed:
            og_ref, do_ref, dg_ref, dw_ref, dgain_ref = refs[3 + NPROJ:]
        else:
            do_ref, dg_ref, dw_ref = refs[3 + NPROJ:]
        first = pl.program_id(0) == 0
        g = g_ref[...]
        s = _sigmoid(g)
        sl = g * s
        x = o_ref[...]
        if normed:
            heads = _head_norm(x)
            on = jnp.concatenate([xs * r * og_ref[...] for xs, r in heads], axis=1)
        else:
            on = x
        dyb = dy_ref[...].astype(BF16)
        a = (on * sl).astype(BF16)
        das = []
        for j in range(NPROJ):
            part = _dot(a[:, j * kb:(j + 1) * kb], dyb, TN)

            @pl.when(first)
            def _():
                dw_ref[j] = part

            @pl.when(jnp.logical_not(first))
            def _():
                dw_ref[j] += part

            das.append(_dot(dyb, w_refs[j][...], NT))
        da = jnp.concatenate(das, axis=1)
        d_on = da * sl
        dg_ref[...] = (da * on * (s * (1.0 + g * (1.0 - s)))).astype(BF16)
        if normed:
            dxs, gsum = [], None
            for hh, (xs, r) in enumerate(heads):
                dx, gt = _rms_bwd(xs, r, og_ref[...], d_on[:, hh * HD:(hh + 1) * HD])
                dxs.append(dx)
                gt = jnp.sum(gt, axis=0, keepdims=True)
                gsum = gt if gsum is None else gsum + gt
            do_ref[...] = jnp.concatenate(dxs, axis=1).astype(BF16)

            @pl.when(first)
            def _():
                dgain_ref[...] = gsum

            @pl.when(jnp.logical_not(first))
            def _():
                dgain_ref[...] += gsum
        else:
            do_ref[...] = d_on.astype(BF16)

    tile = pl.BlockSpec((tm, D), lambda i: (i, 0))
    gate = pl.BlockSpec((None, tm, D), lambda i: (3, i, 0))
    in_specs = [tile, tile, gate] + _w_out_specs(wblk)
    args = [dy, o, proj] + [wall] * NPROJ
    out_specs = [tile, gate, pl.BlockSpec((NPROJ, kb, D), lambda i: (0, 0, 0))]
    out_shape = [jax.ShapeDtypeStruct((T, D), BF16),
                 jax.ShapeDtypeStruct((NPROJ, T, D), BF16),
                 jax.ShapeDtypeStruct((NPROJ, kb, D), F32)]
    if normed:
        in_specs.append(pl.BlockSpec((1, HD), lambda i: (0, 0)))
        args.append(o_gain)
        out_specs.append(pl.BlockSpec((1, HD), lambda i: (0, 0)))
        out_shape.append(jax.ShapeDtypeStruct((1, HD), F32))
    return pl.pallas_call(
        body, name=name, grid=(T // tm,), in_specs=in_specs, out_specs=out_specs,
        out_shape=out_shape, compiler_params=_cparams(("arbitrary",), vmem_mib=48),
    )(*args)


def _in_proj_bwd_x(dproj, wall, wblk, h, gain, dres, name):
    T = h.shape[0]
    tm = min(512, T)

    def body(d_ref, w_ref, h_ref, g_ref, r_ref, dh_ref, dgain_ref, du):
        i, n = pl.program_id(0), pl.program_id(1)
        part = _dot(d_ref[...], w_ref[...], NT)

        @pl.when(n == 0)
        def _():
            du[...] = part

        @pl.when(n != 0)
        def _():
            du[...] += part

        @pl.when(n == NPROJ - 1)
        def _():
            x = h_ref[...]
            dx, gt = _rms_bwd(x, _rms(x), g_ref[...], du[...])
            dh_ref[...] = r_ref[...] + dx
            gt = jnp.sum(gt, axis=0, keepdims=True)

            @pl.when(i == 0)
            def _():
                dgain_ref[...] = gt

            @pl.when(i != 0)
            def _():
                dgain_ref[...] += gt

    return pl.pallas_call(
        body, name=name, grid=(T // tm, NPROJ),
        in_specs=[pl.BlockSpec((None, tm, D), lambda i, n: (n, i, 0)),
                  pl.BlockSpec((None, D, D), lambda i, n: (n, wblk, 0)),
                  pl.BlockSpec((tm, D), lambda i, n: (i, 0)),
                  pl.BlockSpec((1, D), lambda i, n: (0, 0)),
                  pl.BlockSpec((tm, D), lambda i, n: (i, 0))],
        out_specs=[pl.BlockSpec((tm, D), lambda i, n: (i, 0)),
                   pl.BlockSpec((1, D), lambda i, n: (0, 0))],
        out_shape=[jax.ShapeDtypeStruct((T, D), F32), jax.ShapeDtypeStruct((1, D), F32)],
        scratch_shapes=[pltpu.VMEM((tm, D), F32)],
        compiler_params=_cparams(("arbitrary", "arbitrary")),
    )(dproj, wall, h, gain, dres)


def _in_proj_bwd_w(dproj, h, gain, name):
    T = h.shape[0]
    tk = min(512, T)

    def body(d_ref, h_ref, g_ref, dw_ref, u_s):
        k = pl.program_id(1)
        rows = pl.ds(pl.multiple_of(k * tk, tk), tk)

        @pl.when(pl.program_id(0) == 0)
        def _():
            x = h_ref[...]
            u_s[rows, :] = (x * _rms(x) * g_ref[...]).astype(BF16)

        part = _dot(u_s[rows, :], d_ref[...], TN)

        @pl.when(k == 0)
        def _():
            dw_ref[...] = part

        @pl.when(k != 0)
        def _():
            dw_ref[...] += part

    return pl.pallas_call(
        body, name=name, grid=(NPROJ, T // tk),
        in_specs=[pl.BlockSpec((None, tk, D), lambda n, k: (n, k, 0)),
                  pl.BlockSpec((tk, D), lambda n, k: (jnp.where(n == 0, k, 0), 0)),
                  pl.BlockSpec((1, D), lambda n, k: (0, 0))],
        out_specs=pl.BlockSpec((None, D, D), lambda n, k: (n, 0, 0)),
        out_shape=jax.ShapeDtypeStruct((NPROJ, D, D), F32),
        scratch_shapes=[pltpu.VMEM((T, D), BF16)],
        compiler_params=_cparams(("arbitrary", "arbitrary")),
    )(dproj, h, gain)


def _log_sigmoid_pair(z):
    lb = jnp.minimum(z, 0.0) - jnp.log(1.0 + jnp.exp(-jnp.abs(z)))
    return lb, lb - z


def _slab_consts():
    t = lax.broadcasted_iota(jnp.int32, (TK, TK), 0)
    s = lax.broadcasted_iota(jnp.int32, (TK, TK), 1)
    return s < t, (t > s).astype(BF16), (t < s).astype(BF16)


def _slab_rows(k0, S):
    return [(r0, r1, masked) for r0, r1, masked in ((k0, k0 + TK, True), (k0 + TK, S, False)) if r0 < r1]


def _sb_fwd(proj, q_gain, k_gain):
    _, Bl, S, _ = proj.shape

    def body(q_ref, k_ref, v_ref, qg_ref, kg_ref, o_ref, ct_ref, qn, kn, vb):
        q = q_ref[...]
        qn[...] = (q * _rms(q) * qg_ref[...]).astype(BF16)
        k = k_ref[...]
        kn[...] = (k * _rms(k) * kg_ref[...]).astype(BF16)
        vb[...] = v_ref[...].astype(BF16)
        o_ref[...] = jnp.zeros_like(o_ref)
        ct_ref[...] = jnp.zeros_like(ct_ref)
        tri, u_gt, _ = _slab_consts()
        for k0 in reversed(range(0, S, TK)):
            kb, vbb = kn[k0:k0 + TK, :], vb[k0:k0 + TK, :]
            for r0, r1, masked in _slab_rows(k0, S):
                z = _dot(qn[r0:r1, :], kb, NT) * SCALE
                lb, ls = _log_sigmoid_pair(z)
                if masked:
                    ls = jnp.where(tri, ls, 0.0)
                c = ct_ref[r0:r1, :]
                w = jnp.exp(lb + _cum2(ls, u_gt) + c)
                if masked:
                    w = jnp.where(tri, w, 0.0)
                o_ref[r0:r1, :] += _dot(w.astype(BF16), vbb)
                ct_ref[r0:r1, :] = c + jnp.sum(ls, axis=1, keepdims=True)

    def slot(n):
        return pl.BlockSpec((None, None, S, HD), lambda b, h: (n, b, 0, h))

    return pl.pallas_call(
        body, name="sb_fwd", grid=(Bl, HEADS),
        in_specs=[slot(0), slot(1), slot(2),
                  pl.BlockSpec((1, HD), lambda b, h: (0, 0)),
                  pl.BlockSpec((1, HD), lambda b, h: (0, 0))],
        out_specs=[pl.BlockSpec((None, S, HD), lambda b, h: (b, 0, h)),
                   pl.BlockSpec((None, None, S, 1), lambda b, h: (b, h, 0, 0))],
        out_shape=[jax.ShapeDtypeStruct((Bl, S, D), F32),
                   jax.ShapeDtypeStruct((Bl, HEADS, S, 1), F32)],
        scratch_shapes=[pltpu.VMEM((S, HD), BF16)] * 3,
        compiler_params=_cparams(("parallel", "parallel"), vmem_mib=56),
    )(proj, proj, proj, q_gain, k_gain)


def _sb_bwd(proj, ctot, do, dproj, q_gain, k_gain):
    _, Bl, S, _ = proj.shape

    def body(q_ref, k_ref, v_ref, ct_ref, do_ref, qg_ref, kg_ref, _, dqkv_ref, dqg_ref, dkg_ref,
             qn, kn, vb, dqn, dkn, dvn, passed_s, e_s):
        first = jnp.logical_and(pl.program_id(0) == 0, pl.program_id(1) == 0)

        @pl.when(first)
        def _():
            dqg_ref[...] = jnp.zeros_like(dqg_ref)
            dkg_ref[...] = jnp.zeros_like(dkg_ref)

        q = q_ref[...]
        rq = _rms(q)
        qn[...] = (q * rq * qg_ref[...]).astype(BF16)
        k = k_ref[...]
        rk = _rms(k)
        kn[...] = (k * rk * kg_ref[...]).astype(BF16)
        vb[...] = v_ref[...].astype(BF16)
        for acc in (dqn, dkn, dvn, passed_s, e_s):
            acc[...] = jnp.zeros_like(acc)
        tri, u_gt, u_lt = _slab_consts()
        for k0 in range(0, S, TK):
            keys = slice(k0, k0 + TK)
            kb, vbb = kn[keys, :], vb[keys, :]
            for r0, r1, masked in _slab_rows(k0, S):
                rows = slice(r0, r1)
                qb, dobb = qn[rows, :], do_ref[rows, :]
                z = _dot(qb, kb, NT) * SCALE
                lb, ls = _log_sigmoid_pair(z)
                if masked:
                    ls = jnp.where(tri, ls, 0.0)
                passed = passed_s[rows, :] + jnp.sum(ls, axis=1, keepdims=True)
                passed_s[rows, :] = passed
                w = jnp.exp(lb + _cum2(ls, u_gt) + (ct_ref[rows, :] - passed))
                if masked:
                    w = jnp.where(tri, w, 0.0)
                de = w * _dot(dobb, vbb, NT)
                dvn[keys, :] += _dot(w.astype(BF16), dobb, TN)
                e = e_s[rows, :]
                dls = e + _cum2(de, u_lt)
                e_s[rows, :] = e + jnp.sum(de, axis=1, keepdims=True)
                sg = jnp.exp(lb)
                dz = de * (1.0 - sg) - dls * sg
                if masked:
                    dz = jnp.where(tri, dz, 0.0)
                dzb = (dz * SCALE).astype(BF16)
                dqn[rows, :] += _dot(dzb, kb)
                dkn[keys, :] += _dot(dzb, qb, TN)

        dx, gt = _rms_bwd(q, rq, qg_ref[...], dqn[...])
        dqkv_ref[0] = dx.astype(BF16)
        dqg_ref[...] += jnp.sum(gt, axis=0, keepdims=True)
        dx, gt = _rms_bwd(k, rk, kg_ref[...], dkn[...])
        dqkv_ref[1] = dx.astype(BF16)
        dkg_ref[...] += jnp.sum(gt, axis=0, keepdims=True)
        dqkv_ref[2] = dvn[...].astype(BF16)

    def slot(n):
        return pl.BlockSpec((None, None, S, HD), lambda b, h: (n, b, 0, h))

    head = pl.BlockSpec((None, S, HD), lambda b, h: (b, 0, h))
    gain = pl.BlockSpec((1, HD), lambda b, h: (0, 0))
    return pl.pallas_call(
        body, name="sb_bwd", grid=(Bl, HEADS),
        in_specs=[slot(0), slot(1), slot(2),
                  pl.BlockSpec((None, None, S, 1), lambda b, h: (b, h, 0, 0)), head, gain, gain, ANY],
        out_specs=[pl.BlockSpec((3, None, S, HD), lambda b, h: (0, b, 0, h)), gain, gain],
        out_shape=[jax.ShapeDtypeStruct(dproj.shape, dproj.dtype),
                   jax.ShapeDtypeStruct((1, HD), F32), jax.ShapeDtypeStruct((1, HD), F32)],
        scratch_shapes=([pltpu.VMEM((S, HD), BF16)] * 3 + [pltpu.VMEM((S, HD), F32)] * 3
                        + [pltpu.VMEM((S, 1), F32)] * 2),
        input_output_aliases={7: 0},
        compiler_params=_cparams(("arbitrary", "arbitrary"), vmem_mib=56),
    )(proj, proj, proj, ctot, do, q_gain, k_gain, dproj)


def _lower_bound(logits):
    l0, l1 = logits[0:1, :], logits[1:2, :]
    m = jnp.maximum(l0, l1)
    e0, e1 = jnp.exp(l0 - m), jnp.exp(l1 - m)
    p0, p1 = e0 / (e0 + e1), e1 / (e0 + e1)
    return (p0 + p1) - p0, p0 * p1


def _hg_gates(qr, fp, lbv):
    sq = _sigmoid(qr)
    sp = _sigmoid(fp)
    sn = 1.0 / (1.0 + jnp.exp(fp))
    f = lbv + (1.0 - lbv) * sp
    return qr * sq, sq, sp, sn, f, (1.0 - lbv) * sn


def _group_consts():
    t = lax.broadcasted_iota(jnp.int32, (GR, GR), 0)
    j = lax.broadcasted_iota(jnp.int32, (GR, GR), 1)
    same = lax.shift_right_logical(t, CH_LOG2) == lax.shift_right_logical(j, CH_LOG2)
    tril = jnp.logical_and(same, j <= t)
    return (tril, tril.astype(BF16), jnp.logical_and(same, j >= t).astype(BF16), same.astype(BF16))


def _hg_decays(qa, k, f, t_inc, t_same):
    g = jnp.log(f)
    gc = _cum2l(t_inc, g)
    gl = _cum2l(t_same, g)
    gm = gc - 0.5 * gl
    e_q = jnp.exp(jnp.minimum(gm, EXP_CLAMP))
    e_k = jnp.exp(jnp.minimum(-gm, EXP_CLAMP))
    e_g = jnp.exp(gc)
    e_l = jnp.exp(gl - gc)
    return qa * e_q, k * e_k, qa * e_g, k * e_l, e_q, e_k, e_g, e_l, jnp.exp(gl)


def _hg_fwd(proj, lb_logits):
    _, Bl, S, _ = proj.shape
    nc = S // CH

    def body(q_ref, f_ref, i_ref, lg_ref, o_ref, st_ref, qg_s, kd_s, egl_s):
        lbv, _ = _lower_bound(lg_ref[...])
        tril, t_inc, _, t_same = _group_consts()

        def intra(n, _):
            for u in range(2):
                rs = pl.ds(pl.multiple_of((2 * n + u) * GR, GR), GR)
                qa, _, _, _, f, k = _hg_gates(q_ref[rs, :], f_ref[rs, :], lbv)
                qt, kt, qg, kd, _, _, _, _, e_gl = _hg_decays(qa, k, f, t_inc, t_same)
                a = jnp.where(tril, _dot(qt.astype(BF16), kt.astype(BF16), NT), 0.0)
                o_ref[rs, :] = _dot(a.astype(BF16), i_ref[rs, :].astype(BF16))
                qg_s[rs, :] = qg.astype(BF16)
                kd_s[rs, :] = kd.astype(BF16)
                egl_s[rs, :] = e_gl
            return 0

        lax.fori_loop(0, S // (2 * GR), intra, 0)

        st = jnp.zeros((HD, HD), F32)
        for c in range(nc):
            rs = slice(c * CH, (c + 1) * CH)
            st_ref[c] = st
            o_ref[rs, :] += _dot(qg_s[rs, :], st.astype(BF16), NT)
            st = st * egl_s[c * CH:c * CH + 1, :] + _dot(i_ref[rs, :].astype(BF16), kd_s[rs, :], TN)

    def slot(n):
        return pl.BlockSpec((None, None, S, HD), lambda b, h: (n, b, 0, h))

    return pl.pallas_call(
        body, name="hg_fwd", grid=(Bl, HEADS),
        in_specs=[slot(0), slot(1), slot(2), pl.BlockSpec((2, HD), lambda b, h: (0, h))],
        out_specs=[pl.BlockSpec((None, S, HD), lambda b, h: (b, 0, h)),
                   pl.BlockSpec((None, None, nc, HD, HD), lambda b, h: (b, h, 0, 0, 0))],
        out_shape=[jax.ShapeDtypeStruct((Bl, S, D), F32),
                   jax.ShapeDtypeStruct((Bl, HEADS, nc, HD, HD), F32)],
        scratch_shapes=[pltpu.VMEM((S, HD), BF16)] * 2 + [pltpu.VMEM((S, HD), F32)],
        compiler_params=_cparams(("parallel", "parallel")),
    )(proj, proj, proj, lb_logits)


def _hg_bwd(proj, states, do, dproj, lb_logits):
    _, Bl, S, _ = proj.shape
    nc = S // CH

    def body(q_ref, f_ref, i_ref, st_ref, do_ref, lg_ref, _, dqfi_ref, dlb_ref,
             qg_s, kd_s, egl_s, dqg_s, dkd_s, dse_s, di_s):
        lbv, _ = _lower_bound(lg_ref[...])
        tril, t_inc, t_dec, t_same = _group_consts()

        def decays(n, _):
            for u in range(2):
                rs = pl.ds(pl.multiple_of((2 * n + u) * GR, GR), GR)
                qa, _, _, _, f, k = _hg_gates(q_ref[rs, :], f_ref[rs, :], lbv)
                _, _, qg, kd, _, _, _, _, e_gl = _hg_decays(qa, k, f, t_inc, t_same)
                qg_s[rs, :] = qg.astype(BF16)
                kd_s[rs, :] = kd.astype(BF16)
                egl_s[rs, :] = e_gl
            return 0

        lax.fori_loop(0, S // (2 * GR), decays, 0)

        dst = jnp.zeros((HD, HD), F32)
        for c in reversed(range(nc)):
            rs = slice(c * CH, (c + 1) * CH)
            st = st_ref[c]
            dstb = dst.astype(BF16)
            dob = do_ref[rs, :]
            dqg_s[rs, :] = _dot(dob, st.astype(BF16))
            dkd_s[rs, :] = _dot(i_ref[rs, :].astype(BF16), dstb)
            di_s[rs, :] = _dot(kd_s[rs, :], dstb, NT)
            dse_s[rs, :] = jnp.broadcast_to(jnp.sum(dst * st, axis=0, keepdims=True), (CH, HD))
            dst = dst * egl_s[c * CH:c * CH + 1, :] + _dot(dob, qg_s[rs, :], TN)

        def intra(n, dlb):
            for u in range(2):
                rs = pl.ds(pl.multiple_of((2 * n + u) * GR, GR), GR)
                qr, fp = q_ref[rs, :], f_ref[rs, :]
                qa, sq, sp, sn, f, k = _hg_gates(qr, fp, lbv)
                qt, kt, qg, kd, e_q, e_k, e_g, e_l, e_gl = _hg_decays(qa, k, f, t_inc, t_same)
                ib = i_ref[rs, :].astype(BF16)
                dob = do_ref[rs, :]
                ab = jnp.where(tril, _dot(qt.astype(BF16), kt.astype(BF16), NT), 0.0).astype(BF16)
                da = jnp.where(tril, _dot(dob, ib, NT), 0.0)
                dqt = _dot3(da, kt)
                dkt = _dot3(da, qt, TN)
                dqfi_ref[2, rs, :] = (di_s[rs, :] + _dot(ab, dob, TN)).astype(BF16)
                dqg, dkd = dqg_s[rs, :], dkd_s[rs, :]
                dgc = dqt * qt - dkt * kt + dqg * qg - dkd * kd
                dg = _cum2l(t_dec, dgc) + _cum2l(t_same, dkd * kd) + dse_s[rs, :] * e_gl
                t1 = dg / f - (dkt * e_k + dkd * e_l)
                dqfi_ref[1, rs, :] = ((1.0 - lbv) * t1 * sp * sn).astype(BF16)
                dqfi_ref[0, rs, :] = ((dqt * e_q + dqg * e_g) * (sq * (1.0 + qr * (1.0 - sq)))).astype(BF16)
                dlb = dlb + jnp.sum(sn * t1, axis=0, keepdims=True)
            return dlb

        dlb = lax.fori_loop(0, S // (2 * GR), intra, jnp.zeros((1, HD), F32))

        @pl.when(pl.program_id(1) == 0)
        def _():
            dlb_ref[...] = dlb

        @pl.when(pl.program_id(1) != 0)
        def _():
            dlb_ref[...] += dlb

    def slot(n):
        return pl.BlockSpec((None, None, S, HD), lambda h, b: (n, b, 0, h))

    return pl.pallas_call(
        body, name="hg_bwd", grid=(HEADS, Bl),
        in_specs=[slot(0), slot(1), slot(2),
                  pl.BlockSpec((None, None, nc, HD, HD), lambda h, b: (b, h, 0, 0, 0)),
                  pl.BlockSpec((None, S, HD), lambda h, b: (b, 0, h)),
                  pl.BlockSpec((2, HD), lambda h, b: (0, h)), ANY],
        out_specs=[pl.BlockSpec((3, None, S, HD), lambda h, b: (0, b, 0, h)),
                   pl.BlockSpec((1, HD), lambda h, b: (0, h))],
        out_shape=[jax.ShapeDtypeStruct(dproj.shape, dproj.dtype), jax.ShapeDtypeStruct((1, D), F32)],
        scratch_shapes=[pltpu.VMEM((S, HD), BF16)] * 2 + [pltpu.VMEM((S, HD), F32)] * 5,
        input_output_aliases={6: 0},
        compiler_params=_cparams(("parallel", "arbitrary")),
    )(proj, proj, proj, states, do, lb_logits, dproj)


def _place():
    x, y, c = lax.axis_index("x"), lax.axis_index("y"), lax.axis_index("c")
    return x, y, c, [(1 - x, y), (x, 1 - y), (1 - x, 1 - y)]


def _remote(src, dst, ssem, rsem, dev):
    return pltpu.make_async_remote_copy(src_ref=src, dst_ref=dst, send_sem=ssem, recv_sem=rsem,
                                        device_id=dev, device_id_type=MESH)


def _gather_weights(wp, hn):
    half = WROWS // 2

    def body(wp_ref, hn_ref, wall_ref, hnall_ref, ssem, rsem):
        x, y, c, chips = _place()
        b = 2 * x + y
        mine = pl.ds(c * half, half)
        other = pl.ds((1 - c) * half, half)
        sends = []
        for j, chip in enumerate(chips):
            sends.append(_remote(wp_ref.at[mine], wall_ref.at[b, mine], ssem.at[j], rsem.at[j], (*chip, c)))
            sends.append(_remote(hn_ref, hnall_ref.at[b], ssem.at[6 + j], rsem.at[6 + j], (*chip, c)))
        for cp in sends:
            cp.start()
        for j, (cx, cy) in enumerate(chips):
            landed = wall_ref.at[2 * cx + cy, mine]
            _remote(landed, landed, ssem.at[j], rsem.at[j], (cx, cy, c)).wait_recv()
            fwd = _remote(landed, landed, ssem.at[3 + j], rsem.at[3 + j], (x, y, 1 - c))
            fwd.start()
            sends.append(fwd)
        for j, (cx, cy) in enumerate(chips):
            passed = wall_ref.at[2 * cx + cy, other]
            _remote(passed, passed, ssem.at[3 + j], rsem.at[3 + j], (x, y, 1 - c)).wait_recv()
            row = hnall_ref.at[2 * cx + cy]
            _remote(row, row, ssem.at[6 + j], rsem.at[6 + j], (cx, cy, c)).wait_recv()
        for cp in sends:
            cp.wait_send()

    return pl.pallas_call(
        body, name="gather_weights", in_specs=[ANY, ANY], out_specs=[ANY, ANY],
        out_shape=[jax.ShapeDtypeStruct((NPROJ, WROWS, D), BF16),
                   jax.ShapeDtypeStruct((NPROJ, 1, D // NPROJ), F32)],
        scratch_shapes=[pltpu.SemaphoreType.DMA((9,)), pltpu.SemaphoreType.DMA((9,))],
    )(wp, hn)


def _pair_exchange(grads, pack):
    ng = len(grads)

    def body(*refs):
        g_refs, pack_ref = refs[:ng], refs[ng]
        r_refs, allp_ref = refs[ng + 1:2 * ng + 1], refs[2 * ng + 1]
        ssem, rsem, psend, precv, lsem = refs[2 * ng + 2:]
        x, y, c, _ = _place()
        me = 4 * x + 2 * y + c
        local = pltpu.make_async_copy(pack_ref, allp_ref.at[me], lsem)
        local.start()
        sends = []
        for t, (g, r) in enumerate(zip(g_refs, r_refs)):
            for n in range(NPROJ):
                k = t * NPROJ + n
                sends.append(_remote(g.at[n, 1 - c], r.at[n], ssem.at[k], rsem.at[k], (x, y, 1 - c)))
        flips = [(fx, fy, fc) for fx in (0, 1) for fy in (0, 1) for fc in (0, 1)][1:]
        peers = [(fx + x - 2 * fx * x, fy + y - 2 * fy * y, fc + c - 2 * fc * c) for fx, fy, fc in flips]
        for m, peer in enumerate(peers):
            sends.append(_remote(pack_ref, allp_ref.at[me], psend.at[m], precv.at[m], peer))
        for cp in sends:
            cp.start()
        for t, r in enumerate(r_refs):
            for n in range(NPROJ):
                k = t * NPROJ + n
                _remote(r.at[n], r.at[n], ssem.at[k], rsem.at[k], (x, y, 1 - c)).wait_recv()
        for m, (px, py, pc) in enumerate(peers):
            row = allp_ref.at[4 * px + 2 * py + pc]
            _remote(row, row, psend.at[m], precv.at[m], (px, py, pc)).wait_recv()
        for cp in sends:
            cp.wait_send()
        local.wait()

    out_shape = [jax.ShapeDtypeStruct((NPROJ,) + g.shape[2:], F32) for g in grads]
    out_shape.append(jax.ShapeDtypeStruct((8,) + pack.shape, F32))
    return pl.pallas_call(
        body, name="pair_exchange", in_specs=[ANY] * (ng + 1), out_specs=[ANY] * (ng + 1),
        out_shape=out_shape,
        scratch_shapes=[pltpu.SemaphoreType.DMA((ng * NPROJ,)), pltpu.SemaphoreType.DMA((ng * NPROJ,)),
                        pltpu.SemaphoreType.DMA((7,)), pltpu.SemaphoreType.DMA((7,)),
                        pltpu.SemaphoreType.DMA],
    )(*grads, pack)


def _chip_exchange(sums):
    ng = len(sums)

    def body(*refs):
        s_refs, r_refs = refs[:ng], refs[ng:2 * ng]
        ssem, rsem = refs[2 * ng:]
        x, y, c, chips = _place()
        b = 2 * x + y
        sends = [_remote(s.at[2 * cx + cy], r.at[b], ssem.at[3 * t + j], rsem.at[3 * t + j], (cx, cy, c))
                 for t, (s, r) in enumerate(zip(s_refs, r_refs)) for j, (cx, cy) in enumerate(chips)]
        for cp in sends:
            cp.start()
        for t, r in enumerate(r_refs):
            for j, (cx, cy) in enumerate(chips):
                slot = r.at[2 * cx + cy]
                _remote(slot, slot, ssem.at[3 * t + j], rsem.at[3 * t + j], (cx, cy, c)).wait_recv()
        for cp in sends:
            cp.wait_send()

    return pl.pallas_call(
        body, name="chip_exchange", in_specs=[ANY] * ng, out_specs=[ANY] * ng,
        out_shape=[jax.ShapeDtypeStruct(s.shape, s.dtype) for s in sums],
        scratch_shapes=[pltpu.SemaphoreType.DMA((3 * ng,)), pltpu.SemaphoreType.DMA((3 * ng,))],
    )(*sums)


def _sibling_share(halves):
    ng = len(halves)

    def body(*refs):
        h_refs, f_refs = refs[:ng], refs[ng:2 * ng]
        ssem, rsem = refs[2 * ng:]
        x, y, c, _ = _place()
        sends = [_remote(h, f, ssem.at[t], rsem.at[t], (x, y, 1 - c))
                 for t, (h, f) in enumerate(zip(h_refs, f_refs))]
        for cp in sends:
            cp.start()
        for t, f in enumerate(f_refs):
            _remote(f, f, ssem.at[t], rsem.at[t], (x, y, 1 - c)).wait_recv()
        for cp in sends:
            cp.wait_send()

    return pl.pallas_call(
        body, name="sibling_share", in_specs=[ANY] * ng, out_specs=[ANY] * ng,
        out_shape=[jax.ShapeDtypeStruct(h.shape, F32) for h in halves],
        scratch_shapes=[pltpu.SemaphoreType.DMA((ng,)), pltpu.SemaphoreType.DMA((ng,))],
    )(*halves)


def _pair_add(own, recv, cidx, name):
    R = own.shape[2]
    tr = min(256, R)

    def body(c_ref, a_ref, b_ref, o_ref):
        o_ref[...] = (a_ref[...] + b_ref[...]).astype(BF16)

    return pl.pallas_call(
        body, name=name,
        grid_spec=pltpu.PrefetchScalarGridSpec(
            num_scalar_prefetch=1, grid=(NPROJ, R // tr),
            in_specs=[pl.BlockSpec((None, None, tr, D), lambda n, r, c: (n, c[0], r, 0)),
                      pl.BlockSpec((None, tr, D), lambda n, r, c: (n, r, 0))],
            out_specs=pl.BlockSpec((None, tr, D), lambda n, r, c: (n, r, 0))),
        out_shape=jax.ShapeDtypeStruct(recv.shape, BF16),
        compiler_params=_cparams(("parallel", "parallel")),
    )(cidx, own, recv)


def _chip_sum(sums, parts, bidx, name):
    R = parts.shape[1]
    tr = min(256, R)

    def body(b_ref, s_ref, p_ref, o_ref):
        acc = None
        for j in range(NPROJ):
            term = jnp.where(b_ref[0] == j, s_ref[...], p_ref[j]).astype(F32)
            acc = term if acc is None else acc + term
        o_ref[...] = acc

    return pl.pallas_call(
        body, name=name,
        grid_spec=pltpu.PrefetchScalarGridSpec(
            num_scalar_prefetch=1, grid=(R // tr,),
            in_specs=[pl.BlockSpec((None, tr, D), lambda r, b: (b[0], r, 0)),
                      pl.BlockSpec((NPROJ, tr, D), lambda r, b: (0, r, 0))],
            out_specs=pl.BlockSpec((tr, D), lambda r, b: (r, 0))),
        out_shape=jax.ShapeDtypeStruct((R, D), F32),
        compiler_params=_cparams(("parallel",)),
    )(bidx, sums, parts)


def _adamw_math(w, g, m, v):
    m = ADAM_B1 * m + (1.0 - ADAM_B1) * g
    v = ADAM_B2 * v + (1.0 - ADAM_B2) * (g * g)
    m_hat = m / (1.0 - ADAM_B1 ** ADAM_STEP)
    v_hat = v / (1.0 - ADAM_B2 ** ADAM_STEP)
    delta = -ADAM_LR * (m_hat / (jnp.sqrt(v_hat) + ADAM_EPS) + ADAM_WD * w)
    return delta, m, v


def _adamw(w, mine, theirs, m, v, cidx, name):
    R = mine.shape[0]
    tr = min(256, R)
    nr = R // tr

    def body(c_ref, w_ref, a_ref, b_ref, m_ref, v_ref, g_ref, d_ref, nm_ref, nv_ref):
        g = jnp.where(pl.program_id(0) == c_ref[0], a_ref[...], b_ref[...])
        g_ref[...] = g
        d_ref[...], nm_ref[...], nv_ref[...] = _adamw_math(w_ref[...], g, m_ref[...], v_ref[...])

    full = pl.BlockSpec((tr, D), lambda h, r, c: (h * nr + r, 0))
    half = pl.BlockSpec((tr, D), lambda h, r, c: (r, 0))
    return pl.pallas_call(
        body, name=name,
        grid_spec=pltpu.PrefetchScalarGridSpec(
            num_scalar_prefetch=1, grid=(2, nr),
            in_specs=[full, half, half, full, full], out_specs=[full] * 4),
        out_shape=[jax.ShapeDtypeStruct(w.shape, F32)] * 4,
        compiler_params=_cparams(("parallel", "parallel")),
    )(cidx, w, mine, theirs, m, v)


PACK_ROWS = 8


def _small_update(allp, bidx, logits, weights, moments_m, moments_v):
    shapes = [w.shape for w in weights]
    q4 = D // NPROJ

    def body(b_ref, allp_ref, hgp_ref, lg_ref, *refs):
        w_refs, m_refs, v_refs = refs[0:6], refs[6:12], refs[12:18]
        loss_ref = refs[18]
        g_out, d_out, m_out, v_out = refs[19:25], refs[25:31], refs[31:37], refs[37:43]

        def total(ref, row, lo, hi):
            acc = ref[0, row:row + 1, lo:hi]
            for dev in range(1, 8):
                acc = acc + ref[dev, row:row + 1, lo:hi]
            return acc

        _, pp = _lower_bound(lg_ref[...])
        dlb = total(allp_ref, 2, 0, D)
        grads = [total(allp_ref, 0, 0, D), total(allp_ref, 4, 0, HD), total(allp_ref, 4, HD, 2 * HD),
                 total(hgp_ref, 1, 0, q4), total(allp_ref, 4, 2 * HD, 3 * HD), None]
        loss_ref[...] = (0.5 / D) * jnp.sum(total(allp_ref, 3, 0, D), axis=1, keepdims=True)
        for t in range(6):
            if t < 5:
                rows = [(slice(None), grads[t])]
            else:
                rows = [(slice(0, 1), -pp * dlb), (slice(1, 2), pp * dlb)]
            for rs, g in rows:
                g_out[t][rs, :] = g
                d_out[t][rs, :], m_out[t][rs, :], v_out[t][rs, :] = _adamw_math(
                    w_refs[t][rs, :], g, m_refs[t][rs, :], v_refs[t][rs, :])

    whole = [pl.BlockSpec(s, lambda i, b: (0, 0)) for s in shapes]
    return pl.pallas_call(
        body, name="small_update",
        grid_spec=pltpu.PrefetchScalarGridSpec(
            num_scalar_prefetch=1, grid=(1,),
            in_specs=[pl.BlockSpec((8, PACK_ROWS, D), lambda i, b: (0, 0, 0)),
                      pl.BlockSpec((8, PACK_ROWS, q4), lambda i, b: (0, 0, b[0])),
                      pl.BlockSpec((2, D), lambda i, b: (0, 0))] + whole * 3,
            out_specs=[pl.BlockSpec((1, 1), lambda i, b: (0, 0))] + whole * 4),
        out_shape=[jax.ShapeDtypeStruct((1, 1), F32)] + [jax.ShapeDtypeStruct(s, F32) for s in shapes] * 4,
        compiler_params=_cparams(("arbitrary",)),
    )(bidx, allp, allp, logits, *weights, *moments_m, *moments_v)


def kernel(x, sb_norm, sb_w_in, sb_q_gain, sb_k_gain, sb_w_out, hg_norm, hg_w_in, hg_o_gain, hg_w_out, hg_lb_logits, loss_target, m_sb_norm, m_sb_w_in, m_sb_q_gain, m_sb_k_gain, m_sb_w_out, m_hg_norm, m_hg_w_in, m_hg_o_gain, m_hg_w_out, m_hg_lb_logits, v_sb_norm, v_sb_w_in, v_sb_q_gain, v_sb_k_gain, v_sb_w_out, v_hg_norm, v_hg_w_in, v_hg_o_gain, v_hg_w_out, v_hg_lb_logits):
    Bl, S, _ = x.shape
    T = Bl * S
    q4 = D // NPROJ
    cidx = lax.axis_index("c").astype(jnp.int32).reshape(1)
    bidx = (2 * lax.axis_index("x") + lax.axis_index("y")).astype(jnp.int32).reshape(1)

    wp = jnp.concatenate([sb_w_in[0], hg_w_in[0], sb_w_out[0], hg_w_out[0]], axis=0).astype(BF16)
    wall, hnall = _gather_weights(wp, hg_norm)
    wall = lax.dynamic_update_slice(wall, wp[None], (bidx[0], 0, 0))
    hnall = lax.dynamic_update_slice(hnall, hg_norm[None], (bidx[0], 0, 0))
    hgn = hnall.reshape(1, D)
    x2 = x.reshape(T, D)
    tgt = loss_target.reshape(T, D)

    def heads(a):
        return a.reshape(a.shape[:-2] + (Bl, S, D))

    def flat(a):
        return a.reshape(a.shape[:-3] + (T, D))

    proj0 = _in_proj_fwd(x2, sb_norm, wall, W_IN_SB, "sb_in_fwd")
    o0, ctot = _sb_fwd(heads(proj0), sb_q_gain, sb_k_gain)
    h1 = _out_proj_fwd(flat(o0), proj0, x2, wall, W_OUT_SB, "sb_out_fwd")
    proj1 = _in_proj_fwd(h1, hgn, wall, W_IN_HG, "hg_in_fwd")
    o1, states = _hg_fwd(heads(proj1), hg_lb_logits)
    dh2, loss_terms = _out_proj_fwd(flat(o1), proj1, h1, wall, W_OUT_HG, "hg_out_fwd",
                                    o_gain=hg_o_gain, target=tgt)

    do1, dproj1, gout_hg, d_ogain = _out_proj_bwd(dh2, flat(o1), proj1, wall, W_OUT_HG, "hg_out_bwd",
                                                  o_gain=hg_o_gain)
    dproj1, dlb = _hg_bwd(heads(proj1), states, heads(do1), heads(dproj1), hg_lb_logits)
    dproj1 = flat(dproj1)
    dh1, d_hgn = _in_proj_bwd_x(dproj1, wall, W_IN_HG, h1, hgn, dh2, "hg_in_bwd_x")
    gin_hg = _in_proj_bwd_w(dproj1, h1, hgn, "hg_in_bwd_w")
    do0, dproj0, gout_sb = _out_proj_bwd(dh1, flat(o0), proj0, wall, W_OUT_SB, "sb_out_bwd")
    dproj0, d_qg, d_kg = _sb_bwd(heads(proj0), ctot, heads(do0), heads(dproj0), sb_q_gain, sb_k_gain)
    dproj0 = flat(dproj0)
    grad_x, d_sbn = _in_proj_bwd_x(dproj0, wall, W_IN_SB, x2, sb_norm, dh1, "sb_in_bwd_x")
    gin_sb = _in_proj_bwd_w(dproj0, x2, sb_norm, "sb_in_bwd_w")

    gains = jnp.concatenate([d_qg, d_kg, d_ogain, jnp.zeros((1, D - 3 * HD), F32)], axis=1)
    pack = jnp.concatenate([d_sbn, d_hgn, dlb, loss_terms, gains, jnp.zeros((3, D), F32)], axis=0)
    big = [gin_sb.reshape(NPROJ, 2, D // 2, D), gin_hg.reshape(NPROJ, 2, D // 2, D),
           gout_sb.reshape(NPROJ, 2, q4 // 2, D), gout_hg.reshape(NPROJ, 2, q4 // 2, D)]
    *recv, allp = _pair_exchange(big, pack)
    names = ["sb_in", "hg_in", "sb_out", "hg_out"]
    sums = [_pair_add(g, r, cidx, "pair_add_" + nm) for g, r, nm in zip(big, recv, names)]
    parts = _chip_exchange(sums)
    halves = [_chip_sum(sm, p, bidx, "chip_sum_" + nm) for sm, p, nm in zip(sums, parts, names)]
    theirs = _sibling_share(halves)

    big_w = [sb_w_in, hg_w_in, sb_w_out, hg_w_out]
    big_m = [m_sb_w_in, m_hg_w_in, m_sb_w_out, m_hg_w_out]
    big_v = [v_sb_w_in, v_hg_w_in, v_sb_w_out, v_hg_w_out]
    upd = [_adamw(w[0], a, b, m[0], v[0], cidx, "adamw_" + nm)
           for w, a, b, m, v, nm in zip(big_w, halves, theirs, big_m, big_v, names)]
    (g_sb_in, d_sb_in, nm_sb_in, nv_sb_in), (g_hg_in, d_hg_in, nm_hg_in, nv_hg_in), \
        (g_sb_out, d_sb_out, nm_sb_out, nv_sb_out), (g_hg_out, d_hg_out, nm_hg_out, nv_hg_out) = [
            tuple(a[None] for a in u) for u in upd]

    small = _small_update(
        allp, bidx, hg_lb_logits,
        [sb_norm, sb_q_gain, sb_k_gain, hg_norm, hg_o_gain, hg_lb_logits],
        [m_sb_norm, m_sb_q_gain, m_sb_k_gain, m_hg_norm, m_hg_o_gain, m_hg_lb_logits],
        [v_sb_norm, v_sb_q_gain, v_sb_k_gain, v_hg_norm, v_hg_o_gain, v_hg_lb_logits])
    loss = small[0].reshape(())
    (g_sbn, g_qg, g_kg, g_hgn, g_og, g_lb) = small[1:7]
    (d_sbn2, d_qg2, d_kg2, d_hgn2, d_og2, d_lb2) = small[7:13]
    (nm_sbn, nm_qg, nm_kg, nm_hgn, nm_og, nm_lb) = small[13:19]
    (nv_sbn, nv_qg, nv_kg, nv_hgn, nv_og, nv_lb) = small[19:25]

    return (loss, grad_x.reshape(Bl, S, D),
            g_sbn, g_sb_in, g_qg, g_kg, g_sb_out, g_hgn, g_hg_in, g_og, g_hg_out, g_lb,
            d_sbn2, d_sb_in, d_qg2, d_kg2, d_sb_out, d_hgn2, d_hg_in, d_og2, d_hg_out, d_lb2,
            nm_sbn, nm_sb_in, nm_qg, nm_kg, nm_sb_out, nm_hgn, nm_hg_in, nm_og, nm_hg_out, nm_lb,
            nv_sbn, nv_sb_in, nv_qg, nv_kg, nv_sb_out, nv_hgn, nv_hg_in, nv_og, nv_hg_out, nv_lb)
```

```python
import functools

import jax
import jax.numpy as jnp
from jax import lax
from jax.experimental import pallas as pl
from jax.experimental.pallas import tpu as pltpu

F32 = jnp.float32
BF16 = jnp.bfloat16
MESH = pl.DeviceIdType.MESH
ANY = pl.BlockSpec(memory_space=pl.ANY)

D = 1024
HEADS = 8
HD = 128
NPROJ = 4
RMS_EPS = 1e-6
TK = 256
CH = 64
CH_LOG2 = 6
GR = 128
GROUPS = 4
SCALE = HD ** -0.5
EXP_CLAMP = 60.0
W_IN, W_OUT = 0, 4

ADAM_LR = 0.001
ADAM_B1 = 0.9
ADAM_B2 = 0.999
ADAM_EPS = 1e-08
ADAM_WD = 0.01
ADAM_STEP = 10

NT = (((1,), (1,)), ((), ()))
TN = (((0,), (0,)), ((), ()))
MIB = 1024 * 1024


def _cparams(sem=None, vmem_mib=40):
    return pltpu.CompilerParams(dimension_semantics=sem, vmem_limit_bytes=vmem_mib * MIB)


def _dot(a, b, dims=None):
    if dims is None:
        return jnp.dot(a, b, preferred_element_type=F32)
    return lax.dot_general(a, b, dims, preferred_element_type=F32)


def _sigmoid(x):
    return 1.0 / (1.0 + jnp.exp(-x))


def _rms(x):
    return lax.rsqrt(jnp.mean(x * x, axis=-1, keepdims=True) + RMS_EPS)


def _rms_bwd(x, r, gain, dy):
    a = dy * gain
    dx = r * a - x * (r * r * r) * jnp.mean(x * a, axis=-1, keepdims=True)
    return dx, dy * (x * r)


def _split2(v):
    hi = v.astype(BF16)
    lo = (v - hi.astype(F32)).astype(BF16)
    return hi, lo


def _cum2(v, u):
    hi, lo = _split2(v)
    return _dot(hi, u) + _dot(lo, u)


def _dot3(a, b, dims=None):
    ah, al = _split2(a)
    bh, bl = _split2(b)
    return _dot(ah, bh, dims) + _dot(ah, bl, dims) + _dot(al, bh, dims)


def _cum2l(u, v):
    hi, lo = _split2(v)
    return _dot(u, hi) + _dot(u, lo)


def _in_proj_fwd(h, gain, wall, wblk, name):
    T = h.shape[0]
    tm = min(512, T)

    def body(h_ref, g_ref, w_ref, o_ref, u_s):
        rows = pl.ds(pl.multiple_of(pl.program_id(1) * tm, tm), tm)

        @pl.when(pl.program_id(0) == 0)
        def _():
            x = h_ref[...]
            u_s[rows, :] = (x * _rms(x) * g_ref[...]).astype(BF16)

        o_ref[...] = _dot(u_s[rows, :], w_ref[...])

    return pl.pallas_call(
        body, name=name, grid=(NPROJ, T // tm),
        in_specs=[pl.BlockSpec((tm, D), lambda n, i: (jnp.where(n == 0, i, 0), 0)),
                  pl.BlockSpec((1, D), lambda n, i: (0, 0)),
                  pl.BlockSpec((None, D, D), lambda n, i: (n, wblk, 0))],
        out_specs=pl.BlockSpec((None, tm, D), lambda n, i: (n, i, 0)),
        out_shape=jax.ShapeDtypeStruct((NPROJ, T, D), F32),
        scratch_shapes=[pltpu.VMEM((T, D), BF16)],
        compiler_params=_cparams(("arbitrary", "arbitrary")),
    )(h, gain, wall)


def _head_norm(x):
    outs = []
    for hh in range(x.shape[1] // HD):
        xs = x[:, hh * HD:(hh + 1) * HD]
        outs.append((xs, _rms(xs)))
    return outs


def _w_out_specs(wblk):
    kb = D // NPROJ
    return [pl.BlockSpec((None, kb, D), functools.partial(lambda j, i: (j, wblk, 0), j)) for j in range(NPROJ)]


def _out_proj_fwd(o, proj, resid, wall, wblk, name, o_gain=None, target=None):
    T = o.shape[0]
    tm = min(512, T)
    kb = D // NPROJ
    with_loss = target is not None

    def body(*refs):
        o_ref, g_ref, r_ref = refs[:3]
        w_refs = refs[3:3 + NPROJ]
        if with_loss:
            og_ref, t_ref, dh_ref, ls_ref = refs[3 + NPROJ:]
        else:
            h_ref, = refs[3 + NPROJ:]
        x = o_ref[...]
        if with_loss:
            x = jnp.concatenate([xs * r * og_ref[...] for xs, r in _head_norm(x)], axis=1)
        g = g_ref[...]
        a = (x * (g * _sigmoid(g))).astype(BF16)
        hnew = r_ref[...]
        for j in range(NPROJ):
            hnew = hnew + _dot(a[:, j * kb:(j + 1) * kb], w_refs[j][...])
        if with_loss:
            err = hnew - t_ref[...]
            dh_ref[...] = err * (1.0 / D)
            part = jnp.sum(err * err, axis=0, keepdims=True)

            @pl.when(pl.program_id(0) == 0)
            def _():
                ls_ref[...] = part

            @pl.when(pl.program_id(0) != 0)
            def _():
                ls_ref[...] += part
        else:
            h_ref[...] = hnew

    tile = pl.BlockSpec((tm, D), lambda i: (i, 0))
    in_specs = [tile, pl.BlockSpec((None, tm, D), lambda i: (3, i, 0)), tile] + _w_out_specs(wblk)
    args = [o, proj, resid] + [wall] * NPROJ
    out_specs = tile
    out_shape = jax.ShapeDtypeStruct((T, D), F32)
    if with_loss:
        in_specs += [pl.BlockSpec((1, HD), lambda i: (0, 0)), tile]
        args += [o_gain, target]
        out_specs = [tile, pl.BlockSpec((1, D), lambda i: (0, 0))]
        out_shape = [out_shape, jax.ShapeDtypeStruct((1, D), F32)]
    return pl.pallas_call(
        body, name=name, grid=(T // tm,), in_specs=in_specs, out_specs=out_specs,
        out_shape=out_shape, compiler_params=_cparams(("arbitrary",)),
    )(*args)


def _out_proj_bwd(dy, o, proj, wall, wblk, name, o_gain=None, exchange=()):
    T = o.shape[0]
    tm = min(512, T)
    kb = D // NPROJ
    normed = o_gain is not None
    ne = len(exchange)

    def body(*refs):
        it = iter(refs)
        dy_ref, o_ref, g_ref = (next(it) for _ in range(3))
        w_refs = [next(it) for _ in range(NPROJ)]
        og_ref = next(it) if normed else None
        xg_refs = [next(it) for _ in range(ne)]
        do_ref, dg_ref, dw_ref = (next(it) for _ in range(3))
        dgain_ref = next(it) if normed else None
        xr_refs = [next(it) for _ in range(ne)]
        first = pl.program_id(0) == 0
        if ne:
            start, finish = _pair_ops(xg_refs, xr_refs, next(it), next(it))
            pl.when(first)(start)
        g = g_ref[...]
        s = _sigmoid(g)
        sl = g * s
        x = o_ref[...]
        if normed:
            heads = _head_norm(x)
            on = jnp.concatenate([xs * r * og_ref[...] for xs, r in heads], axis=1)
        else:
            on = x
        dyb = dy_ref[...].astype(BF16)
        a = (on * sl).astype(BF16)
        das = []
        for j in range(NPROJ):
            part = _dot(a[:, j * kb:(j + 1) * kb], dyb, TN)

            @pl.when(first)
            def _():
                dw_ref[j] = part

            @pl.when(jnp.logical_not(first))
            def _():
                dw_ref[j] += part

            das.append(_dot(dyb, w_refs[j][...], NT))
        da = jnp.concatenate(das, axis=1)
        d_on = da * sl
        dg_ref[...] = (da * on * (s * (1.0 + g * (1.0 - s)))).astype(BF16)
        if normed:
            dxs, gsum = [], None
            for hh, (xs, r) in enumerate(heads):
                dx, gt = _rms_bwd(xs, r, og_ref[...], d_on[:, hh * HD:(hh + 1) * HD])
                dxs.append(dx)
                gt = jnp.sum(gt, axis=0, keepdims=True)
                gsum = gt if gsum is None else gsum + gt
            do_ref[...] = jnp.concatenate(dxs, axis=1).astype(BF16)

            @pl.when(first)
            def _():
                dgain_ref[...] = gsum

            @pl.when(jnp.logical_not(first))
            def _():
                dgain_ref[...] += gsum
        else:
            do_ref[...] = d_on.astype(BF16)
        if ne:
            pl.when(pl.program_id(0) == T // tm - 1)(finish)

    tile = pl.BlockSpec((tm, D), lambda i: (i, 0))
    gate = pl.BlockSpec((None, tm, D), lambda i: (3, i, 0))
    in_specs = [tile, tile, gate] + _w_out_specs(wblk)
    args = [dy, o, proj] + [wall] * NPROJ
    out_specs = [tile, gate, pl.BlockSpec((NPROJ, kb, D), lambda i: (0, 0, 0))]
    out_shape = [jax.ShapeDtypeStruct((T, D), BF16),
                 jax.ShapeDtypeStruct((NPROJ, T, D), BF16),
                 jax.ShapeDtypeStruct((NPROJ, kb, D), F32)]
    if normed:
        in_specs.append(pl.BlockSpec((1, HD), lambda i: (0, 0)))
        args.append(o_gain)
        out_specs.append(pl.BlockSpec((1, HD), lambda i: (0, 0)))
        out_shape.append(jax.ShapeDtypeStruct((1, HD), F32))
    x_shape, x_sems = _pair_shapes(exchange) if ne else ([], [])
    return pl.pallas_call(
        body, name=name, grid=(T // tm,), in_specs=in_specs + [ANY] * ne, out_specs=out_specs + [ANY] * ne,
        out_shape=out_shape + x_shape, scratch_shapes=x_sems,
        compiler_params=_cparams(("arbitrary",), vmem_mib=48),
    )(*args, *exchange)


def _in_proj_bwd_x(dproj, wall, wblk, h, gain, dres, name):
    T = h.shape[0]
    tm = min(512, T)

    def body(d_ref, w_ref, h_ref, g_ref, r_ref, dh_ref, dgain_ref, du):
        i, n = pl.program_id(0), pl.program_id(1)
        part = _dot(d_ref[...], w_ref[...], NT)

        @pl.when(n == 0)
        def _():
            du[...] = part

        @pl.when(n != 0)
        def _():
            du[...] += part

        @pl.when(n == NPROJ - 1)
        def _():
            x = h_ref[...]
            dx, gt = _rms_bwd(x, _rms(x), g_ref[...], du[...])
            dh_ref[...] = r_ref[...] + dx
            gt = jnp.sum(gt, axis=0, keepdims=True)

            @pl.when(i == 0)
            def _():
                dgain_ref[...] = gt

            @pl.when(i != 0)
            def _():
                dgain_ref[...] += gt

    return pl.pallas_call(
        body, name=name, grid=(T // tm, NPROJ),
        in_specs=[pl.BlockSpec((None, tm, D), lambda i, n: (n, i, 0)),
                  pl.BlockSpec((None, D, D), lambda i, n: (n, wblk, 0)),
                  pl.BlockSpec((tm, D), lambda i, n: (i, 0)),
                  pl.BlockSpec((1, D), lambda i, n: (0, 0)),
                  pl.BlockSpec((tm, D), lambda i, n: (i, 0))],
        out_specs=[pl.BlockSpec((tm, D), lambda i, n: (i, 0)),
                   pl.BlockSpec((1, D), lambda i, n: (0, 0))],
        out_shape=[jax.ShapeDtypeStruct((T, D), F32), jax.ShapeDtypeStruct((1, D), F32)],
        scratch_shapes=[pltpu.VMEM((tm, D), F32)],
        compiler_params=_cparams(("arbitrary", "arbitrary")),
    )(dproj, wall, h, gain, dres)


def _in_proj_bwd_w(dproj, h, gain, name):
    T = h.shape[0]
    tk = min(512, T)

    def body(d_ref, h_ref, g_ref, dw_ref, u_s):
        k = pl.program_id(1)
        rows = pl.ds(pl.multiple_of(k * tk, tk), tk)

        @pl.when(pl.program_id(0) == 0)
        def _():
            x = h_ref[...]
            u_s[rows, :] = (x * _rms(x) * g_ref[...]).astype(BF16)

        part = _dot(u_s[rows, :], d_ref[...], TN)

        @pl.when(k == 0)
        def _():
            dw_ref[...] = part

        @pl.when(k != 0)
        def _():
            dw_ref[...] += part

    return pl.pallas_call(
        body, name=name, grid=(NPROJ, T // tk),
        in_specs=[pl.BlockSpec((None, tk, D), lambda n, k: (n, k, 0)),
                  pl.BlockSpec((tk, D), lambda n, k: (jnp.where(n == 0, k, 0), 0)),
                  pl.BlockSpec((1, D), lambda n, k: (0, 0))],
        out_specs=pl.BlockSpec((None, D, D), lambda n, k: (n, 0, 0)),
        out_shape=jax.ShapeDtypeStruct((NPROJ, D, D), F32),
        scratch_shapes=[pltpu.VMEM((T, D), BF16)],
        compiler_params=_cparams(("arbitrary", "arbitrary")),
    )(dproj, h, gain)


def _log_sigmoid_pair(z):
    lb = jnp.minimum(z, 0.0) - jnp.log(1.0 + jnp.exp(-jnp.abs(z)))
    return lb, lb - z


def _slab_consts():
    t = lax.broadcasted_iota(jnp.int32, (TK, TK), 0)
    s = lax.broadcasted_iota(jnp.int32, (TK, TK), 1)
    return s < t, (t > s).astype(BF16), (t < s).astype(BF16)


def _slab_rows(k0, S):
    return [(r0, r1, masked) for r0, r1, masked in ((k0, k0 + TK, True), (k0 + TK, S, False)) if r0 < r1]


def _sb_fwd(proj, q_gain, k_gain, wp, hn):
    _, Bl, S, _ = proj.shape
    steps = Bl * HEADS

    def body(q_ref, k_ref, v_ref, qg_ref, kg_ref, wp_ref, hn_ref, o_ref, ct_ref, wall_ref, hnall_ref,
             qn, kn, vb, ssem, rsem):
        step = pl.program_id(0) * HEADS + pl.program_id(1)
        start, forward, finish = _gather_ops(wp_ref, wall_ref, ssem, rsem, hn_ref, hnall_ref)
        pl.when(step == 0)(start)
        pl.when(step == steps // 2)(forward)
        q = q_ref[...]
        qn[...] = (q * _rms(q) * qg_ref[...]).astype(BF16)
        k = k_ref[...]
        kn[...] = (k * _rms(k) * kg_ref[...]).astype(BF16)
        vb[...] = v_ref[...].astype(BF16)
        o_ref[...] = jnp.zeros_like(o_ref)
        ct_ref[...] = jnp.zeros_like(ct_ref)
        tri, u_gt, _ = _slab_consts()
        for k0 in reversed(range(0, S, TK)):
            kb, vbb = kn[k0:k0 + TK, :], vb[k0:k0 + TK, :]
            for r0, r1, masked in _slab_rows(k0, S):
                z = _dot(qn[r0:r1, :], kb, NT) * SCALE
                lb, ls = _log_sigmoid_pair(z)
                if masked:
                    ls = jnp.where(tri, ls, 0.0)
                c = ct_ref[r0:r1, :]
                w = jnp.exp(lb + _cum2(ls, u_gt) + c)
                if masked:
                    w = jnp.where(tri, w, 0.0)
                o_ref[r0:r1, :] += _dot(w.astype(BF16), vbb)
                ct_ref[r0:r1, :] = c + jnp.sum(ls, axis=1, keepdims=True)
        pl.when(step == steps - 1)(finish)

    def slot(n):
        return pl.BlockSpec((None, None, S, HD), lambda b, h: (n, b, 0, h))

    return pl.pallas_call(
        body, name="sb_fwd", grid=(Bl, HEADS),
        in_specs=[slot(0), slot(1), slot(2),
                  pl.BlockSpec((1, HD), lambda b, h: (0, 0)),
                  pl.BlockSpec((1, HD), lambda b, h: (0, 0)), ANY, ANY],
        out_specs=[pl.BlockSpec((None, S, HD), lambda b, h: (b, 0, h)),
                   pl.BlockSpec((None, None, S, 1), lambda b, h: (b, h, 0, 0)), ANY, ANY],
        out_shape=[jax.ShapeDtypeStruct((Bl, S, D), F32),
                   jax.ShapeDtypeStruct((Bl, HEADS, S, 1), F32),
                   jax.ShapeDtypeStruct((NPROJ,) + wp.shape, BF16),
                   jax.ShapeDtypeStruct((NPROJ,) + hn.shape, F32)],
        scratch_shapes=[pltpu.VMEM((S, HD), BF16)] * 3 + [pltpu.SemaphoreType.DMA((GATHER_SEMS,))] * 2,
        compiler_params=_cparams(("arbitrary", "arbitrary"), vmem_mib=56),
    )(proj, proj, proj, q_gain, k_gain, wp, hn)


def _sb_bwd(proj, ctot, do, dproj, q_gain, k_gain, exchange):
    _, Bl, S, _ = proj.shape
    ne = len(exchange)

    def body(q_ref, k_ref, v_ref, ct_ref, do_ref, qg_ref, kg_ref, _, *refs):
        xs_refs, (dqkv_ref, dqg_ref, dkg_ref), xr_refs = refs[:ne], refs[ne:ne + 3], refs[ne + 3:2 * ne + 3]
        qn, kn, vb, dqn, dkn, dvn, passed_s, e_s, ssem, rsem = refs[2 * ne + 3:]
        step = pl.program_id(0) * HEADS + pl.program_id(1)
        first = step == 0
        start, finish = _chip_ops(xs_refs, xr_refs, ssem, rsem)

        @pl.when(first)
        def _():
            start()
            dqg_ref[...] = jnp.zeros_like(dqg_ref)
            dkg_ref[...] = jnp.zeros_like(dkg_ref)

        q = q_ref[...]
        rq = _rms(q)
        qn[...] = (q * rq * qg_ref[...]).astype(BF16)
        k = k_ref[...]
        rk = _rms(k)
        kn[...] = (k * rk * kg_ref[...]).astype(BF16)
        vb[...] = v_ref[...].astype(BF16)
        for acc in (dqn, dkn, dvn, passed_s, e_s):
            acc[...] = jnp.zeros_like(acc)
        tri, u_gt, u_lt = _slab_consts()
        for k0 in range(0, S, TK):
            keys = slice(k0, k0 + TK)
            kb, vbb = kn[keys, :], vb[keys, :]
            for r0, r1, masked in _slab_rows(k0, S):
                rows = slice(r0, r1)
                qb, dobb = qn[rows, :], do_ref[rows, :]
                z = _dot(qb, kb, NT) * SCALE
                lb, ls = _log_sigmoid_pair(z)
                if masked:
                    ls = jnp.where(tri, ls, 0.0)
                passed = passed_s[rows, :] + jnp.sum(ls, axis=1, keepdims=True)
                passed_s[rows, :] = passed
                w = jnp.exp(lb + _cum2(ls, u_gt) + (ct_ref[rows, :] - passed))
                if masked:
                    w = jnp.where(tri, w, 0.0)
                de = w * _dot(dobb, vbb, NT)
                dvn[keys, :] += _dot(w.astype(BF16), dobb, TN)
                e = e_s[rows, :]
                dls = e + _cum2(de, u_lt)
                e_s[rows, :] = e + jnp.sum(de, axis=1, keepdims=True)
                sg = jnp.exp(lb)
                dz = de * (1.0 - sg) - dls * sg
                if masked:
                    dz = jnp.where(tri, dz, 0.0)
                dzb = (dz * SCALE).astype(BF16)
                dqn[rows, :] += _dot(dzb, kb)
                dkn[keys, :] += _dot(dzb, qb, TN)

        dx, gt = _rms_bwd(q, rq, qg_ref[...], dqn[...])
        dqkv_ref[0] = dx.astype(BF16)
        dqg_ref[...] += jnp.sum(gt, axis=0, keepdims=True)
        dx, gt = _rms_bwd(k, rk, kg_ref[...], dkn[...])
        dqkv_ref[1] = dx.astype(BF16)
        dkg_ref[...] += jnp.sum(gt, axis=0, keepdims=True)
        dqkv_ref[2] = dvn[...].astype(BF16)
        pl.when(step == Bl * HEADS - 1)(finish)

    def slot(n):
        return pl.BlockSpec((None, None, S, HD), lambda b, h: (n, b, 0, h))

    head = pl.BlockSpec((None, S, HD), lambda b, h: (b, 0, h))
    gain = pl.BlockSpec((1, HD), lambda b, h: (0, 0))
    x_shape, x_sems = _chip_shapes(exchange)
    return pl.pallas_call(
        body, name="sb_bwd", grid=(Bl, HEADS),
        in_specs=[slot(0), slot(1), slot(2),
                  pl.BlockSpec((None, None, S, 1), lambda b, h: (b, h, 0, 0)), head, gain, gain, ANY] + [ANY] * ne,
        out_specs=[pl.BlockSpec((3, None, S, HD), lambda b, h: (0, b, 0, h)), gain, gain] + [ANY] * ne,
        out_shape=[jax.ShapeDtypeStruct(dproj.shape, dproj.dtype),
                   jax.ShapeDtypeStruct((1, HD), F32), jax.ShapeDtypeStruct((1, HD), F32)] + x_shape,
        scratch_shapes=([pltpu.VMEM((S, HD), BF16)] * 3 + [pltpu.VMEM((S, HD), F32)] * 3
                        + [pltpu.VMEM((S, 1), F32)] * 2 + x_sems),
        input_output_aliases={7: 0},
        compiler_params=_cparams(("arbitrary", "arbitrary"), vmem_mib=56),
    )(proj, proj, proj, ctot, do, q_gain, k_gain, dproj, *exchange)


def _lower_bound(logits):
    l0, l1 = logits[0:1, :], logits[1:2, :]
    m = jnp.maximum(l0, l1)
    e0, e1 = jnp.exp(l0 - m), jnp.exp(l1 - m)
    p0, p1 = e0 / (e0 + e1), e1 / (e0 + e1)
    return (p0 + p1) - p0, p0 * p1


def _hg_gates(qr, fp, lbv):
    sq = _sigmoid(qr)
    sp = _sigmoid(fp)
    sn = 1.0 / (1.0 + jnp.exp(fp))
    f = lbv + (1.0 - lbv) * sp
    return qr * sq, sq, sp, sn, f, (1.0 - lbv) * sn


def _group_consts():
    t = lax.broadcasted_iota(jnp.int32, (GR, GR), 0)
    j = lax.broadcasted_iota(jnp.int32, (GR, GR), 1)
    same = lax.shift_right_logical(t, CH_LOG2) == lax.shift_right_logical(j, CH_LOG2)
    tril = jnp.logical_and(same, j <= t)
    return (tril, tril.astype(BF16), jnp.logical_and(same, j >= t).astype(BF16), same.astype(BF16))


def _hg_decays(qa, k, f, t_inc, t_same):
    g = jnp.log(f)
    gc = _cum2l(t_inc, g)
    gl = _cum2l(t_same, g)
    gm = gc - 0.5 * gl
    e_q = jnp.exp(jnp.minimum(gm, EXP_CLAMP))
    e_k = jnp.exp(jnp.minimum(-gm, EXP_CLAMP))
    e_g = jnp.exp(gc)
    e_l = jnp.exp(gl - gc)
    return qa * e_q, k * e_k, qa * e_g, k * e_l, e_q, e_k, e_g, e_l, jnp.exp(gl)


def _hg_fwd(proj, lb_logits):
    _, Bl, S, _ = proj.shape
    nc = S // CH

    def body(q_ref, f_ref, i_ref, lg_ref, o_ref, st_ref, qg_s, kd_s, egl_s):
        lbv, _ = _lower_bound(lg_ref[...])
        tril, t_inc, _, t_same = _group_consts()

        def intra(n, _):
            for u in range(GROUPS):
                rs = pl.ds(pl.multiple_of((GROUPS * n + u) * GR, GR), GR)
                qa, _, _, _, f, k = _hg_gates(q_ref[rs, :], f_ref[rs, :], lbv)
                qt, kt, qg, kd, _, _, _, _, e_gl = _hg_decays(qa, k, f, t_inc, t_same)
                a = jnp.where(tril, _dot(qt.astype(BF16), kt.astype(BF16), NT), 0.0)
                o_ref[rs, :] = _dot(a.astype(BF16), i_ref[rs, :].astype(BF16))
                qg_s[rs, :] = qg.astype(BF16)
                kd_s[rs, :] = kd.astype(BF16)
                egl_s[rs, :] = e_gl
            return 0

        lax.fori_loop(0, S // (GROUPS * GR), intra, 0)

        st = jnp.zeros((HD, HD), F32)
        for c in range(nc):
            rs = slice(c * CH, (c + 1) * CH)
            st_ref[c] = st
            o_ref[rs, :] += _dot(qg_s[rs, :], st.astype(BF16), NT)
            st = st * egl_s[c * CH:c * CH + 1, :] + _dot(i_ref[rs, :].astype(BF16), kd_s[rs, :], TN)

    def slot(n):
        return pl.BlockSpec((None, None, S, HD), lambda b, h: (n, b, 0, h))

    return pl.pallas_call(
        body, name="hg_fwd", grid=(Bl, HEADS),
        in_specs=[slot(0), slot(1), slot(2), pl.BlockSpec((2, HD), lambda b, h: (0, h))],
        out_specs=[pl.BlockSpec((None, S, HD), lambda b, h: (b, 0, h)),
                   pl.BlockSpec((None, None, nc, HD, HD), lambda b, h: (b, h, 0, 0, 0))],
        out_shape=[jax.ShapeDtypeStruct((Bl, S, D), F32),
                   jax.ShapeDtypeStruct((Bl, HEADS, nc, HD, HD), F32)],
        scratch_shapes=[pltpu.VMEM((S, HD), BF16)] * 2 + [pltpu.VMEM((S, HD), F32)],
        compiler_params=_cparams(("parallel", "parallel")),
    )(proj, proj, proj, lb_logits)


def _hg_bwd(proj, states, do, dproj, lb_logits):
    _, Bl, S, _ = proj.shape
    nc = S // CH

    def body(q_ref, f_ref, i_ref, st_ref, do_ref, lg_ref, _, dqfi_ref, dlb_ref,
             qg_s, kd_s, egl_s, dqg_s, dkd_s, dse_s, di_s):
        lbv, _ = _lower_bound(lg_ref[...])
        tril, t_inc, t_dec, t_same = _group_consts()

        def decays(n, _):
            for u in range(GROUPS):
                rs = pl.ds(pl.multiple_of((GROUPS * n + u) * GR, GR), GR)
                qa, _, _, _, f, k = _hg_gates(q_ref[rs, :], f_ref[rs, :], lbv)
                _, _, qg, kd, _, _, _, _, e_gl = _hg_decays(qa, k, f, t_inc, t_same)
                qg_s[rs, :] = qg.astype(BF16)
                kd_s[rs, :] = kd.astype(BF16)
                egl_s[rs, :] = e_gl
            return 0

        lax.fori_loop(0, S // (GROUPS * GR), decays, 0)

        dst = jnp.zeros((HD, HD), F32)
        for c in reversed(range(nc)):
            rs = slice(c * CH, (c + 1) * CH)
            st = st_ref[c]
            dstb = dst.astype(BF16)
            dob = do_ref[rs, :]
            dqg_s[rs, :] = _dot(dob, st.astype(BF16))
            dkd_s[rs, :] = _dot(i_ref[rs, :].astype(BF16), dstb)
            di_s[rs, :] = _dot(kd_s[rs, :], dstb, NT)
            dse_s[rs, :] = jnp.broadcast_to(jnp.sum(dst * st, axis=0, keepdims=True), (CH, HD))
            dst = dst * egl_s[c * CH:c * CH + 1, :] + _dot(dob, qg_s[rs, :], TN)

        def intra(n, dlb):
            for u in range(GROUPS):
                rs = pl.ds(pl.multiple_of((GROUPS * n + u) * GR, GR), GR)
                qr, fp = q_ref[rs, :], f_ref[rs, :]
                qa, sq, sp, sn, f, k = _hg_gates(qr, fp, lbv)
                qt, kt, qg, kd, e_q, e_k, e_g, e_l, e_gl = _hg_decays(qa, k, f, t_inc, t_same)
                ib = i_ref[rs, :].astype(BF16)
                dob = do_ref[rs, :]
                ab = jnp.where(tril, _dot(qt.astype(BF16), kt.astype(BF16), NT), 0.0).astype(BF16)
                da = jnp.where(tril, _dot(dob, ib, NT), 0.0)
                dqt = _dot3(da, kt)
                dkt = _dot3(da, qt, TN)
                dqfi_ref[2, rs, :] = (di_s[rs, :] + _dot(ab, dob, TN)).astype(BF16)
                dqg, dkd = dqg_s[rs, :], dkd_s[rs, :]
                dgc = dqt * qt - dkt * kt + dqg * qg - dkd * kd
                dg = _cum2l(t_dec, dgc) + _cum2l(t_same, dkd * kd) + dse_s[rs, :] * e_gl
                t1 = dg / f - (dkt * e_k + dkd * e_l)
                dqfi_ref[1, rs, :] = ((1.0 - lbv) * t1 * sp * sn).astype(BF16)
                dqfi_ref[0, rs, :] = ((dqt * e_q + dqg * e_g) * (sq * (1.0 + qr * (1.0 - sq)))).astype(BF16)
                dlb = dlb + jnp.sum(sn * t1, axis=0, keepdims=True)
            return dlb

        dlb = lax.fori_loop(0, S // (GROUPS * GR), intra, jnp.zeros((1, HD), F32))

        @pl.when(pl.program_id(1) == 0)
        def _():
            dlb_ref[...] = dlb

        @pl.when(pl.program_id(1) != 0)
        def _():
            dlb_ref[...] += dlb

    def slot(n):
        return pl.BlockSpec((None, None, S, HD), lambda h, b: (n, b, 0, h))

    return pl.pallas_call(
        body, name="hg_bwd", grid=(HEADS, Bl),
        in_specs=[slot(0), slot(1), slot(2),
                  pl.BlockSpec((None, None, nc, HD, HD), lambda h, b: (b, h, 0, 0, 0)),
                  pl.BlockSpec((None, S, HD), lambda h, b: (b, 0, h)),
                  pl.BlockSpec((2, HD), lambda h, b: (0, h)), ANY],
        out_specs=[pl.BlockSpec((3, None, S, HD), lambda h, b: (0, b, 0, h)),
                   pl.BlockSpec((1, HD), lambda h, b: (0, h))],
        out_shape=[jax.ShapeDtypeStruct(dproj.shape, dproj.dtype), jax.ShapeDtypeStruct((1, D), F32)],
        scratch_shapes=[pltpu.VMEM((S, HD), BF16)] * 2 + [pltpu.VMEM((S, HD), F32)] * 5,
        input_output_aliases={6: 0},
        compiler_params=_cparams(("parallel", "arbitrary")),
    )(proj, proj, proj, states, do, lb_logits, dproj)


def _place():
    x, y, c = lax.axis_index("x"), lax.axis_index("y"), lax.axis_index("c")
    return x, y, c, [(1 - x, y), (x, 1 - y), (1 - x, 1 - y)]


def _remote(src, dst, ssem, rsem, dev):
    return pltpu.make_async_remote_copy(src_ref=src, dst_ref=dst, send_sem=ssem, recv_sem=rsem,
                                        device_id=dev, device_id_type=MESH)


GATHER_SEMS = 9


def _gather_ops(wp_ref, wall_ref, ssem, rsem, hn_ref=None, hnall_ref=None):
    half = wp_ref.shape[0] // 2

    def place():
        x, y, c, chips = _place()
        return x, y, c, chips, 2 * x + y, pl.ds(c * half, half), pl.ds((1 - c) * half, half)

    def first_sends():
        x, y, c, chips, b, mine, _ = place()
        cps = [_remote(wp_ref.at[mine], wall_ref.at[b, mine], ssem.at[j], rsem.at[j], (*chip, c))
               for j, chip in enumerate(chips)]
        if hn_ref is not None:
            cps += [_remote(hn_ref, hnall_ref.at[b], ssem.at[6 + j], rsem.at[6 + j], (*chip, c))
                    for j, chip in enumerate(chips)]
        return cps

    def forwards():
        x, y, c, chips, _, mine, _ = place()
        return [_remote(wall_ref.at[2 * cx + cy, mine], wall_ref.at[2 * cx + cy, mine],
                        ssem.at[3 + j], rsem.at[3 + j], (x, y, 1 - c)) for j, (cx, cy) in enumerate(chips)]

    def start():
        for cp in first_sends():
            cp.start()

    def forward():
        x, y, c, chips, _, mine, _ = place()
        for j, (cx, cy) in enumerate(chips):
            landed = wall_ref.at[2 * cx + cy, mine]
            _remote(landed, landed, ssem.at[j], rsem.at[j], (cx, cy, c)).wait_recv()
        for cp in forwards():
            cp.start()

    def finish():
        x, y, c, chips, _, _, other = place()
        for j, (cx, cy) in enumerate(chips):
            passed = wall_ref.at[2 * cx + cy, other]
            _remote(passed, passed, ssem.at[3 + j], rsem.at[3 + j], (x, y, 1 - c)).wait_recv()
            if hn_ref is not None:
                row = hnall_ref.at[2 * cx + cy]
                _remote(row, row, ssem.at[6 + j], rsem.at[6 + j], (cx, cy, c)).wait_recv()
        for cp in first_sends() + forwards():
            cp.wait_send()

    return start, forward, finish


def _gather_weights(wp):
    def body(wp_ref, wall_ref, ssem, rsem):
        for step in _gather_ops(wp_ref, wall_ref, ssem, rsem):
            step()

    return pl.pallas_call(
        body, name="gather_weights", in_specs=[ANY], out_specs=ANY,
        out_shape=jax.ShapeDtypeStruct((NPROJ,) + wp.shape, BF16),
        scratch_shapes=[pltpu.SemaphoreType.DMA((GATHER_SEMS,)), pltpu.SemaphoreType.DMA((GATHER_SEMS,))],
    )(wp)


def _pair_ops(g_refs, r_refs, ssem, rsem):
    def copies():
        x, y, c, _ = _place()
        return [_remote(g.at[n, 1 - c], r.at[n], ssem.at[t * NPROJ + n], rsem.at[t * NPROJ + n], (x, y, 1 - c))
                for t, (g, r) in enumerate(zip(g_refs, r_refs)) for n in range(NPROJ)]

    def start():
        for cp in copies():
            cp.start()

    def finish():
        x, y, c, _ = _place()
        for t, r in enumerate(r_refs):
            for n in range(NPROJ):
                k = t * NPROJ + n
                _remote(r.at[n], r.at[n], ssem.at[k], rsem.at[k], (x, y, 1 - c)).wait_recv()
        for cp in copies():
            cp.wait_send()

    return start, finish


def _pair_shapes(grads):
    return ([jax.ShapeDtypeStruct((NPROJ,) + g.shape[2:], F32) for g in grads],
            [pltpu.SemaphoreType.DMA((len(grads) * NPROJ,))] * 2)


def _pair_exchange(grads, pack):
    ng = len(grads)

    def body(*refs):
        g_refs, pack_ref = refs[:ng], refs[ng]
        r_refs, allp_ref = refs[ng + 1:2 * ng + 1], refs[2 * ng + 1]
        ssem, rsem, psend, precv, lsem = refs[2 * ng + 2:]
        x, y, c, _ = _place()
        me = 4 * x + 2 * y + c
        local = pltpu.make_async_copy(pack_ref, allp_ref.at[me], lsem)
        local.start()
        start, finish = _pair_ops(g_refs, r_refs, ssem, rsem)
        start()
        flips = [(fx, fy, fc) for fx in (0, 1) for fy in (0, 1) for fc in (0, 1)][1:]
        peers = [(fx + x - 2 * fx * x, fy + y - 2 * fy * y, fc + c - 2 * fc * c) for fx, fy, fc in flips]
        sends = [_remote(pack_ref, allp_ref.at[me], psend.at[m], precv.at[m], peer) for m, peer in enumerate(peers)]
        for cp in sends:
            cp.start()
        finish()
        for m, (px, py, pc) in enumerate(peers):
            row = allp_ref.at[4 * px + 2 * py + pc]
            _remote(row, row, psend.at[m], precv.at[m], (px, py, pc)).wait_recv()
        for cp in sends:
            cp.wait_send()
        local.wait()

    out_shape, sems = _pair_shapes(grads)
    return pl.pallas_call(
        body, name="pair_exchange", in_specs=[ANY] * (ng + 1), out_specs=[ANY] * (ng + 1),
        out_shape=out_shape + [jax.ShapeDtypeStruct((8,) + pack.shape, F32)],
        scratch_shapes=sems + [pltpu.SemaphoreType.DMA((7,)), pltpu.SemaphoreType.DMA((7,)),
                               pltpu.SemaphoreType.DMA],
    )(*grads, pack)


def _chip_ops(s_refs, r_refs, ssem, rsem):
    def copies():
        x, y, c, chips = _place()
        return [_remote(s.at[2 * cx + cy], r.at[2 * x + y], ssem.at[3 * t + j], rsem.at[3 * t + j], (cx, cy, c))
                for t, (s, r) in enumerate(zip(s_refs, r_refs)) for j, (cx, cy) in enumerate(chips)]

    def start():
        for cp in copies():
            cp.start()

    def finish():
        x, y, c, chips = _place()
        for t, r in enumerate(r_refs):
            for j, (cx, cy) in enumerate(chips):
                slot = r.at[2 * cx + cy]
                _remote(slot, slot, ssem.at[3 * t + j], rsem.at[3 * t + j], (cx, cy, c)).wait_recv()
        for cp in copies():
            cp.wait_send()

    return start, finish


def _chip_shapes(sums):
    return ([jax.ShapeDtypeStruct(s.shape, s.dtype) for s in sums],
            [pltpu.SemaphoreType.DMA((3 * len(sums),))] * 2)


def _chip_exchange(sums):
    ng = len(sums)

    def body(*refs):
        start, finish = _chip_ops(refs[:ng], refs[ng:2 * ng], *refs[2 * ng:])
        start()
        finish()

    out_shape, sems = _chip_shapes(sums)
    return pl.pallas_call(
        body, name="chip_exchange", in_specs=[ANY] * ng, out_specs=[ANY] * ng,
        out_shape=out_shape, scratch_shapes=sems,
    )(*sums)


def _sibling_share(halves):
    ng = len(halves)

    def body(*refs):
        h_refs, f_refs = refs[:ng], refs[ng:2 * ng]
        ssem, rsem = refs[2 * ng:]
        x, y, c, _ = _place()
        sends = [_remote(h, f, ssem.at[t], rsem.at[t], (x, y, 1 - c))
                 for t, (h, f) in enumerate(zip(h_refs, f_refs))]
        for cp in sends:
            cp.start()
        for t, f in enumerate(f_refs):
            _remote(f, f, ssem.at[t], rsem.at[t], (x, y, 1 - c)).wait_recv()
        for cp in sends:
            cp.wait_send()

    return pl.pallas_call(
        body, name="sibling_share", in_specs=[ANY] * ng, out_specs=[ANY] * ng,
        out_shape=[jax.ShapeDtypeStruct(h.shape, F32) for h in halves],
        scratch_shapes=[pltpu.SemaphoreType.DMA((ng,)), pltpu.SemaphoreType.DMA((ng,))],
    )(*halves)


def _pair_add(own, recv, cidx, name):
    R = own.shape[2]
    tr = min(256, R)

    def body(c_ref, a_ref, b_ref, o_ref):
        o_ref[...] = (a_ref[...] + b_ref[...]).astype(BF16)

    return pl.pallas_call(
        body, name=name,
        grid_spec=pltpu.PrefetchScalarGridSpec(
            num_scalar_prefetch=1, grid=(NPROJ, R // tr),
            in_specs=[pl.BlockSpec((None, None, tr, D), lambda n, r, c: (n, c[0], r, 0)),
                      pl.BlockSpec((None, tr, D), lambda n, r, c: (n, r, 0))],
            out_specs=pl.BlockSpec((None, tr, D), lambda n, r, c: (n, r, 0))),
        out_shape=jax.ShapeDtypeStruct(recv.shape, BF16),
        compiler_params=_cparams(("parallel", "parallel")),
    )(cidx, own, recv)


def _chip_sum(sums, parts, bidx, name):
    R = parts.shape[1]
    tr = min(256, R)

    def body(b_ref, s_ref, p_ref, o_ref):
        acc = None
        for j in range(NPROJ):
            term = jnp.where(b_ref[0] == j, s_ref[...], p_ref[j]).astype(F32)
            acc = term if acc is None else acc + term
        o_ref[...] = acc

    return pl.pallas_call(
        body, name=name,
        grid_spec=pltpu.PrefetchScalarGridSpec(
            num_scalar_prefetch=1, grid=(R // tr,),
            in_specs=[pl.BlockSpec((None, tr, D), lambda r, b: (b[0], r, 0)),
                      pl.BlockSpec((NPROJ, tr, D), lambda r, b: (0, r, 0))],
            out_specs=pl.BlockSpec((tr, D), lambda r, b: (r, 0))),
        out_shape=jax.ShapeDtypeStruct((R, D), F32),
        compiler_params=_cparams(("parallel",)),
    )(bidx, sums, parts)


def _adamw_math(w, g, m, v):
    m = ADAM_B1 * m + (1.0 - ADAM_B1) * g
    v = ADAM_B2 * v + (1.0 - ADAM_B2) * (g * g)
    m_hat = m / (1.0 - ADAM_B1 ** ADAM_STEP)
    v_hat = v / (1.0 - ADAM_B2 ** ADAM_STEP)
    delta = -ADAM_LR * (m_hat / (jnp.sqrt(v_hat) + ADAM_EPS) + ADAM_WD * w)
    return delta, m, v


def _adamw(w, mine, theirs, m, v, cidx, name):
    R = mine.shape[0]
    tr = min(256, R)
    nr = R // tr

    def body(c_ref, w_ref, a_ref, b_ref, m_ref, v_ref, g_ref, d_ref, nm_ref, nv_ref):
        g = jnp.where(pl.program_id(0) == c_ref[0], a_ref[...], b_ref[...])
        g_ref[...] = g
        d_ref[...], nm_ref[...], nv_ref[...] = _adamw_math(w_ref[...], g, m_ref[...], v_ref[...])

    full = pl.BlockSpec((tr, D), lambda h, r, c: (h * nr + r, 0))
    half = pl.BlockSpec((tr, D), lambda h, r, c: (r, 0))
    return pl.pallas_call(
        body, name=name,
        grid_spec=pltpu.PrefetchScalarGridSpec(
            num_scalar_prefetch=1, grid=(2, nr),
            in_specs=[full, half, half, full, full], out_specs=[full] * 4),
        out_shape=[jax.ShapeDtypeStruct(w.shape, F32)] * 4,
        compiler_params=_cparams(("parallel", "parallel")),
    )(cidx, w, mine, theirs, m, v)


PACK_ROWS = 8


def _small_update(allp, bidx, logits, weights, moments_m, moments_v):
    shapes = [w.shape for w in weights]
    q4 = D // NPROJ

    def body(b_ref, allp_ref, hgp_ref, lg_ref, *refs):
        w_refs, m_refs, v_refs = refs[0:6], refs[6:12], refs[12:18]
        loss_ref = refs[18]
        g_out, d_out, m_out, v_out = refs[19:25], refs[25:31], refs[31:37], refs[37:43]

        def total(ref, row, lo, hi):
            acc = ref[0, row:row + 1, lo:hi]
            for dev in range(1, 8):
                acc = acc + ref[dev, row:row + 1, lo:hi]
            return acc

        _, pp = _lower_bound(lg_ref[...])
        dlb = total(allp_ref, 2, 0, D)
        grads = [total(allp_ref, 0, 0, D), total(allp_ref, 4, 0, HD), total(allp_ref, 4, HD, 2 * HD),
                 total(hgp_ref, 1, 0, q4), total(allp_ref, 4, 2 * HD, 3 * HD), None]
        loss_ref[...] = (0.5 / D) * jnp.sum(total(allp_ref, 3, 0, D), axis=1, keepdims=True)
        for t in range(6):
            if t < 5:
                rows = [(slice(None), grads[t])]
            else:
                rows = [(slice(0, 1), -pp * dlb), (slice(1, 2), pp * dlb)]
            for rs, g in rows:
                g_out[t][rs, :] = g
                d_out[t][rs, :], m_out[t][rs, :], v_out[t][rs, :] = _adamw_math(
                    w_refs[t][rs, :], g, m_refs[t][rs, :], v_refs[t][rs, :])

    whole = [pl.BlockSpec(s, lambda i, b: (0, 0)) for s in shapes]
    return pl.pallas_call(
        body, name="small_update",
        grid_spec=pltpu.PrefetchScalarGridSpec(
            num_scalar_prefetch=1, grid=(1,),
            in_specs=[pl.BlockSpec((8, PACK_ROWS, D), lambda i, b: (0, 0, 0)),
                      pl.BlockSpec((8, PACK_ROWS, q4), lambda i, b: (0, 0, b[0])),
                      pl.BlockSpec((2, D), lambda i, b: (0, 0))] + whole * 3,
            out_specs=[pl.BlockSpec((1, 1), lambda i, b: (0, 0))] + whole * 4),
        out_shape=[jax.ShapeDtypeStruct((1, 1), F32)] + [jax.ShapeDtypeStruct(s, F32) for s in shapes] * 4,
        compiler_params=_cparams(("arbitrary",)),
    )(bidx, allp, allp, logits, *weights, *moments_m, *moments_v)


def kernel(x, sb_norm, sb_w_in, sb_q_gain, sb_k_gain, sb_w_out, hg_norm, hg_w_in, hg_o_gain, hg_w_out, hg_lb_logits, loss_target, m_sb_norm, m_sb_w_in, m_sb_q_gain, m_sb_k_gain, m_sb_w_out, m_hg_norm, m_hg_w_in, m_hg_o_gain, m_hg_w_out, m_hg_lb_logits, v_sb_norm, v_sb_w_in, v_sb_q_gain, v_sb_k_gain, v_sb_w_out, v_hg_norm, v_hg_w_in, v_hg_o_gain, v_hg_w_out, v_hg_lb_logits):
    Bl, S, _ = x.shape
    T = Bl * S
    cidx = lax.axis_index("c").astype(jnp.int32).reshape(1)
    bidx = (2 * lax.axis_index("x") + lax.axis_index("y")).astype(jnp.int32).reshape(1)

    def own_slot(gathered, mine):
        return lax.dynamic_update_slice(gathered, mine[None], (bidx[0],) + (0,) * mine.ndim)

    def halved(g):
        return g.reshape(NPROJ, 2, g.shape[-2] * g.shape[0] // (2 * NPROJ), D)

    def heads(a):
        return a.reshape(a.shape[:-2] + (Bl, S, D))

    def flat(a):
        return a.reshape(a.shape[:-3] + (T, D))

    wp_sb = jnp.concatenate([sb_w_in[0], sb_w_out[0]], axis=0).astype(BF16)
    wp_hg = jnp.concatenate([hg_w_in[0], hg_w_out[0]], axis=0).astype(BF16)
    wall_sb = own_slot(_gather_weights(wp_sb), wp_sb)
    x2 = x.reshape(T, D)
    tgt = loss_target.reshape(T, D)

    proj0 = _in_proj_fwd(x2, sb_norm, wall_sb, W_IN, "sb_in_fwd")
    o0, ctot, wall_hg, hnall = _sb_fwd(heads(proj0), sb_q_gain, sb_k_gain, wp_hg, hg_norm)
    wall_hg = own_slot(wall_hg, wp_hg)
    hgn = own_slot(hnall, hg_norm).reshape(1, D)
    h1 = _out_proj_fwd(flat(o0), proj0, x2, wall_sb, W_OUT, "sb_out_fwd")
    proj1 = _in_proj_fwd(h1, hgn, wall_hg, W_IN, "hg_in_fwd")
    o1, states = _hg_fwd(heads(proj1), hg_lb_logits)
    dh2, loss_terms = _out_proj_fwd(flat(o1), proj1, h1, wall_hg, W_OUT, "hg_out_fwd",
                                    o_gain=hg_o_gain, target=tgt)

    do1, dproj1, gout_hg, d_ogain = _out_proj_bwd(dh2, flat(o1), proj1, wall_hg, W_OUT, "hg_out_bwd",
                                                  o_gain=hg_o_gain)
    dproj1, dlb = _hg_bwd(heads(proj1), states, heads(do1), heads(dproj1), hg_lb_logits)
    dproj1 = flat(dproj1)
    dh1, d_hgn = _in_proj_bwd_x(dproj1, wall_hg, W_IN, h1, hgn, dh2, "hg_in_bwd_x")
    big_hg = [halved(_in_proj_bwd_w(dproj1, h1, hgn, "hg_in_bwd_w")), halved(gout_hg)]
    do0, dproj0, gout_sb, *recv_hg = _out_proj_bwd(dh1, flat(o0), proj0, wall_sb, W_OUT, "sb_out_bwd",
                                                   exchange=big_hg)
    sums_hg = [_pair_add(g, r, cidx, "pair_add_" + nm) for g, r, nm in zip(big_hg, recv_hg, ("hg_in", "hg_out"))]
    dproj0, d_qg, d_kg, *parts_hg = _sb_bwd(heads(proj0), ctot, heads(do0), heads(dproj0),
                                            sb_q_gain, sb_k_gain, sums_hg)
    dproj0 = flat(dproj0)
    grad_x, d_sbn = _in_proj_bwd_x(dproj0, wall_sb, W_IN, x2, sb_norm, dh1, "sb_in_bwd_x")
    big_sb = [halved(_in_proj_bwd_w(dproj0, x2, sb_norm, "sb_in_bwd_w")), halved(gout_sb)]

    gains = jnp.concatenate([d_qg, d_kg, d_ogain, jnp.zeros((1, D - 3 * HD), F32)], axis=1)
    pack = jnp.concatenate([d_sbn, d_hgn, dlb, loss_terms, gains, jnp.zeros((3, D), F32)], axis=0)
    *recv_sb, allp = _pair_exchange(big_sb, pack)
    sums_sb = [_pair_add(g, r, cidx, "pair_add_" + nm) for g, r, nm in zip(big_sb, recv_sb, ("sb_in", "sb_out"))]
    parts_sb = _chip_exchange(sums_sb)
    names = ["sb_in", "hg_in", "sb_out", "hg_out"]
    sums = [sums_sb[0], sums_hg[0], sums_sb[1], sums_hg[1]]
    parts = [parts_sb[0], parts_hg[0], parts_sb[1], parts_hg[1]]
    halves = [_chip_sum(sm, p, bidx, "chip_sum_" + nm) for sm, p, nm in zip(sums, parts, names)]
    theirs = _sibling_share(halves)

    big_w = [sb_w_in, hg_w_in, sb_w_out, hg_w_out]
    big_m = [m_sb_w_in, m_hg_w_in, m_sb_w_out, m_hg_w_out]
    big_v = [v_sb_w_in, v_hg_w_in, v_sb_w_out, v_hg_w_out]
    upd = [_adamw(w[0], a, b, m[0], v[0], cidx, "adamw_" + nm)
           for w, a, b, m, v, nm in zip(big_w, halves, theirs, big_m, big_v, names)]
    (g_sb_in, d_sb_in, nm_sb_in, nv_sb_in), (g_hg_in, d_hg_in, nm_hg_in, nv_hg_in), \
        (g_sb_out, d_sb_out, nm_sb_out, nv_sb_out), (g_hg_out, d_hg_out, nm_hg_out, nv_hg_out) = [
            tuple(a[None] for a in u) for u in upd]

    small = _small_update(
        allp, bidx, hg_lb_logits,
        [sb_norm, sb_q_gain, sb_k_gain, hg_norm, hg_o_gain, hg_lb_logits],
        [m_sb_norm, m_sb_q_gain, m_sb_k_gain, m_hg_norm, m_hg_o_gain, m_hg_lb_logits],
        [v_sb_norm, v_sb_q_gain, v_sb_k_gain, v_hg_norm, v_hg_o_gain, v_hg_lb_logits])
    loss = small[0].reshape(())
    (g_sbn, g_qg, g_kg, g_hgn, g_og, g_lb) = small[1:7]
    (d_sbn2, d_qg2, d_kg2, d_hgn2, d_og2, d_lb2) = small[7:13]
    (nm_sbn, nm_qg, nm_kg, nm_hgn, nm_og, nm_lb) = small[13:19]
    (nv_sbn, nv_qg, nv_kg, nv_hgn, nv_og, nv_lb) = small[19:25]

    return (loss, grad_x.reshape(Bl, S, D),
            g_sbn, g_sb_in, g_qg, g_kg, g_sb_out, g_hgn, g_hg_in, g_og, g_hg_out, g_lb,
            d_sbn2, d_sb_in, d_qg2, d_kg2, d_sb_out, d_hgn2, d_hg_in, d_og2, d_hg_out, d_lb2,
            nm_sbn, nm_sb_in, nm_qg, nm_kg, nm_sb_out, nm_hgn, nm_hg_in, nm_og, nm_hg_out, nm_lb,
            nv_sbn, nv_sb_in, nv_qg, nv_kg, nv_sb_out, nv_hgn, nv_hg_in, nv_og, nv_hg_out, nv_lb)
```

```python
import functools

import jax
import jax.numpy as jnp
from jax import lax
from jax.experimental import pallas as pl
from jax.experimental.pallas import tpu as pltpu

F32 = jnp.float32
BF16 = jnp.bfloat16
MESH = pl.DeviceIdType.MESH
ANY = pl.BlockSpec(memory_space=pl.ANY)

D = 1024
HEADS = 8
HD = 128
NPROJ = 4
RMS_EPS = 1e-6
TK = 256
CH = 64
CH_LOG2 = 6
GR = 256
GROUPS = 2
SCALE = HD ** -0.5
EXP_CLAMP = 60.0
W_IN, W_OUT = 0, 4

ADAM_LR = 0.001
ADAM_B1 = 0.9
ADAM_B2 = 0.999
ADAM_EPS = 1e-08
ADAM_WD = 0.01
ADAM_STEP = 10

NT = (((1,), (1,)), ((), ()))
TN = (((0,), (0,)), ((), ()))
MIB = 1024 * 1024


def _cparams(sem=None, vmem_mib=40):
    return pltpu.CompilerParams(dimension_semantics=sem, vmem_limit_bytes=vmem_mib * MIB)


def _dot(a, b, dims=None):
    if dims is None:
        return jnp.dot(a, b, preferred_element_type=F32)
    return lax.dot_general(a, b, dims, preferred_element_type=F32)


def _sigmoid(x):
    return 1.0 / (1.0 + jnp.exp(-x))


def _rms(x):
    return lax.rsqrt(jnp.mean(x * x, axis=-1, keepdims=True) + RMS_EPS)


def _rms_bwd(x, r, gain, dy):
    a = dy * gain
    dx = r * a - x * (r * r * r) * jnp.mean(x * a, axis=-1, keepdims=True)
    return dx, dy * (x * r)


def _split2(v):
    hi = v.astype(BF16)
    lo = (v - hi.astype(F32)).astype(BF16)
    return hi, lo


def _cum2(v, u):
    hi, lo = _split2(v)
    return _dot(hi, u) + _dot(lo, u)


def _dot3(a, b, dims=None):
    ah, al = _split2(a)
    bh, bl = _split2(b)
    return _dot(ah, bh, dims) + _dot(ah, bl, dims) + _dot(al, bh, dims)


def _cum2l(u, v):
    hi, lo = _split2(v)
    return _dot(u, hi) + _dot(u, lo)


def _in_proj_fwd(h, gain, wall, wblk, name):
    T = h.shape[0]
    tm = min(512, T)

    def body(h_ref, g_ref, w_ref, o_ref, u_s):
        rows = pl.ds(pl.multiple_of(pl.program_id(1) * tm, tm), tm)

        @pl.when(pl.program_id(0) == 0)
        def _():
            x = h_ref[...]
            u_s[rows, :] = (x * _rms(x) * g_ref[...]).astype(BF16)

        o_ref[...] = _dot(u_s[rows, :], w_ref[...])

    return pl.pallas_call(
        body, name=name, grid=(NPROJ, T // tm),
        in_specs=[pl.BlockSpec((tm, D), lambda n, i: (jnp.where(n == 0, i, 0), 0)),
                  pl.BlockSpec((1, D), lambda n, i: (0, 0)),
                  pl.BlockSpec((None, D, D), lambda n, i: (n, wblk, 0))],
        out_specs=pl.BlockSpec((None, tm, D), lambda n, i: (n, i, 0)),
        out_shape=jax.ShapeDtypeStruct((NPROJ, T, D), F32),
        scratch_shapes=[pltpu.VMEM((T, D), BF16)],
        compiler_params=_cparams(("arbitrary", "arbitrary")),
    )(h, gain, wall)


def _head_norm(x):
    outs = []
    for hh in range(x.shape[1] // HD):
        xs = x[:, hh * HD:(hh + 1) * HD]
        outs.append((xs, _rms(xs)))
    return outs


def _w_out_specs(wblk):
    kb = D // NPROJ
    return [pl.BlockSpec((None, kb, D), functools.partial(lambda j, i: (j, wblk, 0), j)) for j in range(NPROJ)]


def _out_proj_fwd(o, proj, resid, wall, wblk, name, o_gain=None, target=None):
    T = o.shape[0]
    tm = min(512, T)
    kb = D // NPROJ
    with_loss = target is not None

    def body(*refs):
        o_ref, g_ref, r_ref = refs[:3]
        w_refs = refs[3:3 + NPROJ]
        if with_loss:
            og_ref, t_ref, dh_ref, ls_ref = refs[3 + NPROJ:]
        else:
            h_ref, = refs[3 + NPROJ:]
        x = o_ref[...]
        if with_loss:
            x = jnp.concatenate([xs * r * og_ref[...] for xs, r in _head_norm(x)], axis=1)
        g = g_ref[...]
        a = (x * (g * _sigmoid(g))).astype(BF16)
        hnew = r_ref[...]
        for j in range(NPROJ):
            hnew = hnew + _dot(a[:, j * kb:(j + 1) * kb], w_refs[j][...])
        if with_loss:
            err = hnew - t_ref[...]
            dh_ref[...] = err * (1.0 / D)
            part = jnp.sum(err * err, axis=0, keepdims=True)

            @pl.when(pl.program_id(0) == 0)
            def _():
                ls_ref[...] = part

            @pl.when(pl.program_id(0) != 0)
            def _():
                ls_ref[...] += part
        else:
            h_ref[...] = hnew

    tile = pl.BlockSpec((tm, D), lambda i: (i, 0))
    in_specs = [tile, pl.BlockSpec((None, tm, D), lambda i: (3, i, 0)), tile] + _w_out_specs(wblk)
    args = [o, proj, resid] + [wall] * NPROJ
    out_specs = tile
    out_shape = jax.ShapeDtypeStruct((T, D), F32)
    if with_loss:
        in_specs += [pl.BlockSpec((1, HD), lambda i: (0, 0)), tile]
        args += [o_gain, target]
        out_specs = [tile, pl.BlockSpec((1, D), lambda i: (0, 0))]
        out_shape = [out_shape, jax.ShapeDtypeStruct((1, D), F32)]
    return pl.pallas_call(
        body, name=name, grid=(T // tm,), in_specs=in_specs, out_specs=out_specs,
        out_shape=out_shape, compiler_params=_cparams(("arbitrary",)),
    )(*args)


def _out_proj_bwd(dy, o, proj, wall, wblk, name, o_gain=None, exchange=()):
    T = o.shape[0]
    tm = min(512, T)
    kb = D // NPROJ
    normed = o_gain is not None
    ne = len(exchange)

    def body(*refs):
        it = iter(refs)
        dy_ref, o_ref, g_ref = (next(it) for _ in range(3))
        w_refs = [next(it) for _ in range(NPROJ)]
        og_ref = next(it) if normed else None
        xg_refs = [next(it) for _ in range(ne)]
        do_ref, dg_ref, dw_ref = (next(it) for _ in range(3))
        dgain_ref = next(it) if normed else None
        xr_refs = [next(it) for _ in range(ne)]
        wt_s = next(it)
        first = pl.program_id(0) == 0
        if ne:
            start, finish = _pair_ops(xg_refs, xr_refs, next(it), next(it))
            pl.when(first)(start)

        @pl.when(first)
        def _():
            for j in range(NPROJ):
                wt_s[:, j * kb:(j + 1) * kb] = w_refs[j][...].T
        g = g_ref[...]
        s = _sigmoid(g)
        sl = g * s
        x = o_ref[...]
        if normed:
            heads = _head_norm(x)
            on = jnp.concatenate([xs * r * og_ref[...] for xs, r in heads], axis=1)
        else:
            on = x
        dyb = dy_ref[...].astype(BF16)
        a = (on * sl).astype(BF16)
        for j in range(NPROJ):
            part = _dot(a[:, j * kb:(j + 1) * kb], dyb, TN)

            @pl.when(first)
            def _():
                dw_ref[j] = part

            @pl.when(jnp.logical_not(first))
            def _():
                dw_ref[j] += part

        da = _dot(dyb, wt_s[...])
        d_on = da * sl
        dg_ref[...] = (da * on * (s * (1.0 + g * (1.0 - s)))).astype(BF16)
        if normed:
            dxs, gsum = [], None
            for hh, (xs, r) in enumerate(heads):
                dx, gt = _rms_bwd(xs, r, og_ref[...], d_on[:, hh * HD:(hh + 1) * HD])
                dxs.append(dx)
                gt = jnp.sum(gt, axis=0, keepdims=True)
                gsum = gt if gsum is None else gsum + gt
            do_ref[...] = jnp.concatenate(dxs, axis=1).astype(BF16)

            @pl.when(first)
            def _():
                dgain_ref[...] = gsum

            @pl.when(jnp.logical_not(first))
            def _():
                dgain_ref[...] += gsum
        else:
            do_ref[...] = d_on.astype(BF16)
        if ne:
            pl.when(pl.program_id(0) == T // tm - 1)(finish)

    tile = pl.BlockSpec((tm, D), lambda i: (i, 0))
    gate = pl.BlockSpec((None, tm, D), lambda i: (3, i, 0))
    in_specs = [tile, tile, gate] + _w_out_specs(wblk)
    args = [dy, o, proj] + [wall] * NPROJ
    out_specs = [tile, gate, pl.BlockSpec((NPROJ, kb, D), lambda i: (0, 0, 0))]
    out_shape = [jax.ShapeDtypeStruct((T, D), BF16),
                 jax.ShapeDtypeStruct((NPROJ, T, D), BF16),
                 jax.ShapeDtypeStruct((NPROJ, kb, D), F32)]
    if normed:
        in_specs.append(pl.BlockSpec((1, HD), lambda i: (0, 0)))
        args.append(o_gain)
        out_specs.append(pl.BlockSpec((1, HD), lambda i: (0, 0)))
        out_shape.append(jax.ShapeDtypeStruct((1, HD), F32))
    x_shape, x_sems = _pair_shapes(exchange) if ne else ([], [])
    return pl.pallas_call(
        body, name=name, grid=(T // tm,), in_specs=in_specs + [ANY] * ne, out_specs=out_specs + [ANY] * ne,
        out_shape=out_shape + x_shape, scratch_shapes=[pltpu.VMEM((D, D), BF16)] + x_sems,
        compiler_params=_cparams(("arbitrary",), vmem_mib=48),
    )(*args, *exchange)


def _in_proj_bwd_x(dproj, wall, wblk, h, gain, dres, name):
    T = h.shape[0]
    tm = min(512, T)

    def body(d_ref, w_ref, h_ref, g_ref, r_ref, dh_ref, dgain_ref, du, wt_s):
        i, n = pl.program_id(0), pl.program_id(1)

        @pl.when(i == 0)
        def _():
            wt_s[n] = w_ref[...].T

        part = _dot(d_ref[...], wt_s[n])

        @pl.when(n == 0)
        def _():
            du[...] = part

        @pl.when(n != 0)
        def _():
            du[...] += part

        @pl.when(n == NPROJ - 1)
        def _():
            x = h_ref[...]
            dx, gt = _rms_bwd(x, _rms(x), g_ref[...], du[...])
            dh_ref[...] = r_ref[...] + dx
            gt = jnp.sum(gt, axis=0, keepdims=True)

            @pl.when(i == 0)
            def _():
                dgain_ref[...] = gt

            @pl.when(i != 0)
            def _():
                dgain_ref[...] += gt

    return pl.pallas_call(
        body, name=name, grid=(T // tm, NPROJ),
        in_specs=[pl.BlockSpec((None, tm, D), lambda i, n: (n, i, 0)),
                  pl.BlockSpec((None, D, D), lambda i, n: (jnp.where(i == 0, n, NPROJ - 1), wblk, 0)),
                  pl.BlockSpec((tm, D), lambda i, n: (i, 0)),
                  pl.BlockSpec((1, D), lambda i, n: (0, 0)),
                  pl.BlockSpec((tm, D), lambda i, n: (i, 0))],
        out_specs=[pl.BlockSpec((tm, D), lambda i, n: (i, 0)),
                   pl.BlockSpec((1, D), lambda i, n: (0, 0))],
        out_shape=[jax.ShapeDtypeStruct((T, D), F32), jax.ShapeDtypeStruct((1, D), F32)],
        scratch_shapes=[pltpu.VMEM((tm, D), F32), pltpu.VMEM((NPROJ, D, D), BF16)],
        compiler_params=_cparams(("arbitrary", "arbitrary")),
    )(dproj, wall, h, gain, dres)


def _in_proj_bwd_w(dproj, h, gain, name):
    T = h.shape[0]
    tk = min(512, T)

    def body(d_ref, h_ref, g_ref, dw_ref, ut_s):
        k = pl.program_id(1)

        @pl.when(pl.program_id(0) == 0)
        def _():
            x = h_ref[...]
            ut_s[k] = (x * _rms(x) * g_ref[...]).astype(BF16).T

        part = _dot(ut_s[k], d_ref[...])

        @pl.when(k == 0)
        def _():
            dw_ref[...] = part

        @pl.when(k != 0)
        def _():
            dw_ref[...] += part

    return pl.pallas_call(
        body, name=name, grid=(NPROJ, T // tk),
        in_specs=[pl.BlockSpec((None, tk, D), lambda n, k: (n, k, 0)),
                  pl.BlockSpec((tk, D), lambda n, k: (jnp.where(n == 0, k, 0), 0)),
                  pl.BlockSpec((1, D), lambda n, k: (0, 0))],
        out_specs=pl.BlockSpec((None, D, D), lambda n, k: (n, 0, 0)),
        out_shape=jax.ShapeDtypeStruct((NPROJ, D, D), F32),
        scratch_shapes=[pltpu.VMEM((T // tk, D, tk), BF16)],
        compiler_params=_cparams(("arbitrary", "arbitrary")),
    )(dproj, h, gain)


def _log_sigmoid_pair(z):
    lb = jnp.minimum(z, 0.0) - jnp.log(1.0 + jnp.exp(-jnp.abs(z)))
    return lb, lb - z


def _slab_consts():
    t = lax.broadcasted_iota(jnp.int32, (TK, TK), 0)
    s = lax.broadcasted_iota(jnp.int32, (TK, TK), 1)
    return s < t, (t > s).astype(BF16), (t < s).astype(BF16)


def _slab_rows(k0, S):
    return [(r0, r1, masked) for r0, r1, masked in ((k0, k0 + TK, True), (k0 + TK, S, False)) if r0 < r1]


def _sb_fwd(proj, q_gain, k_gain, wp, hn):
    _, Bl, S, _ = proj.shape
    steps = Bl * HEADS

    def body(q_ref, k_ref, v_ref, qg_ref, kg_ref, wp_ref, hn_ref, o_ref, ct_ref, wall_ref, hnall_ref,
             qn, kn, vb, ssem, rsem):
        step = pl.program_id(0) * HEADS + pl.program_id(1)
        start, forward, finish = _gather_ops(wp_ref, wall_ref, ssem, rsem, hn_ref, hnall_ref)
        pl.when(step == 0)(start)
        pl.when(step == steps // 2)(forward)
        q = q_ref[...]
        qn[...] = (q * _rms(q) * qg_ref[...]).astype(BF16)
        k = k_ref[...]
        kn[...] = (k * _rms(k) * kg_ref[...]).astype(BF16)
        vb[...] = v_ref[...].astype(BF16)
        o_ref[...] = jnp.zeros_like(o_ref)
        ct_ref[...] = jnp.zeros_like(ct_ref)
        tri, u_gt, _ = _slab_consts()
        for k0 in reversed(range(0, S, TK)):
            kb, vbb = kn[k0:k0 + TK, :], vb[k0:k0 + TK, :]
            for r0, r1, masked in _slab_rows(k0, S):
                z = _dot(qn[r0:r1, :], kb, NT) * SCALE
                lb, ls = _log_sigmoid_pair(z)
                if masked:
                    ls = jnp.where(tri, ls, 0.0)
                c = ct_ref[r0:r1, :]
                w = jnp.exp(lb + _cum2(ls, u_gt) + c)
                if masked:
                    w = jnp.where(tri, w, 0.0)
                o_ref[r0:r1, :] += _dot(w.astype(BF16), vbb)
                ct_ref[r0:r1, :] = c + jnp.sum(ls, axis=1, keepdims=True)
        pl.when(step == steps - 1)(finish)

    def slot(n):
        return pl.BlockSpec((None, None, S, HD), lambda b, h: (n, b, 0, h))

    return pl.pallas_call(
        body, name="sb_fwd", grid=(Bl, HEADS),
        in_specs=[slot(0), slot(1), slot(2),
                  pl.BlockSpec((1, HD), lambda b, h: (0, 0)),
                  pl.BlockSpec((1, HD), lambda b, h: (0, 0)), ANY, ANY],
        out_specs=[pl.BlockSpec((None, S, HD), lambda b, h: (b, 0, h)),
                   pl.BlockSpec((None, None, S, 1), lambda b, h: (b, h, 0, 0)), ANY, ANY],
        out_shape=[jax.ShapeDtypeStruct((Bl, S, D), F32),
                   jax.ShapeDtypeStruct((Bl, HEADS, S, 1), F32),
                   jax.ShapeDtypeStruct((NPROJ,) + wp.shape, BF16),
                   jax.ShapeDtypeStruct((NPROJ,) + hn.shape, F32)],
        scratch_shapes=[pltpu.VMEM((S, HD), BF16)] * 3 + [pltpu.SemaphoreType.DMA((GATHER_SEMS,))] * 2,
        compiler_params=_cparams(("arbitrary", "arbitrary"), vmem_mib=56),
    )(proj, proj, proj, q_gain, k_gain, wp, hn)


def _sb_bwd(proj, ctot, do, dproj, q_gain, k_gain, exchange):
    _, Bl, S, _ = proj.shape
    ne = len(exchange)

    def body(q_ref, k_ref, v_ref, ct_ref, do_ref, qg_ref, kg_ref, _, *refs):
        xs_refs, (dqkv_ref, dqg_ref, dkg_ref), xr_refs = refs[:ne], refs[ne:ne + 3], refs[ne + 3:2 * ne + 3]
        qn, kn, vb, dqn, dkn, dvn, passed_s, e_s, ssem, rsem = refs[2 * ne + 3:]
        step = pl.program_id(0) * HEADS + pl.program_id(1)
        first = step == 0
        start, finish = _chip_ops(xs_refs, xr_refs, ssem, rsem)

        @pl.when(first)
        def _():
            start()
            dqg_ref[...] = jnp.zeros_like(dqg_ref)
            dkg_ref[...] = jnp.zeros_like(dkg_ref)

        q = q_ref[...]
        rq = _rms(q)
        qn[...] = (q * rq * qg_ref[...]).astype(BF16)
        k = k_ref[...]
        rk = _rms(k)
        kn[...] = (k * rk * kg_ref[...]).astype(BF16)
        vb[...] = v_ref[...].astype(BF16)
        for acc in (dqn, dkn, dvn, passed_s, e_s):
            acc[...] = jnp.zeros_like(acc)
        tri, u_gt, u_lt = _slab_consts()
        for k0 in range(0, S, TK):
            keys = slice(k0, k0 + TK)
            kb, vbb = kn[keys, :], vb[keys, :]
            for r0, r1, masked in _slab_rows(k0, S):
                rows = slice(r0, r1)
                qb, dobb = qn[rows, :], do_ref[rows, :]
                z = _dot(qb, kb, NT) * SCALE
                lb, ls = _log_sigmoid_pair(z)
                if masked:
                    ls = jnp.where(tri, ls, 0.0)
                passed = passed_s[rows, :] + jnp.sum(ls, axis=1, keepdims=True)
                passed_s[rows, :] = passed
                w = jnp.exp(lb + _cum2(ls, u_gt) + (ct_ref[rows, :] - passed))
                if masked:
                    w = jnp.where(tri, w, 0.0)
                de = w * _dot(dobb, vbb, NT)
                dvn[keys, :] += _dot(w.astype(BF16), dobb, TN)
                e = e_s[rows, :]
                dls = e + _cum2(de, u_lt)
                e_s[rows, :] = e + jnp.sum(de, axis=1, keepdims=True)
                sg = jnp.exp(lb)
                dz = de * (1.0 - sg) - dls * sg
                if masked:
                    dz = jnp.where(tri, dz, 0.0)
                dzb = (dz * SCALE).astype(BF16)
                dqn[rows, :] += _dot(dzb, kb)
                dkn[keys, :] += _dot(dzb, qb, TN)

        dx, gt = _rms_bwd(q, rq, qg_ref[...], dqn[...])
        dqkv_ref[0] = dx.astype(BF16)
        dqg_ref[...] += jnp.sum(gt, axis=0, keepdims=True)
        dx, gt = _rms_bwd(k, rk, kg_ref[...], dkn[...])
        dqkv_ref[1] = dx.astype(BF16)
        dkg_ref[...] += jnp.sum(gt, axis=0, keepdims=True)
        dqkv_ref[2] = dvn[...].astype(BF16)
        pl.when(step == Bl * HEADS - 1)(finish)

    def slot(n):
        return pl.BlockSpec((None, None, S, HD), lambda b, h: (n, b, 0, h))

    head = pl.BlockSpec((None, S, HD), lambda b, h: (b, 0, h))
    gain = pl.BlockSpec((1, HD), lambda b, h: (0, 0))
    x_shape, x_sems = _chip_shapes(exchange)
    return pl.pallas_call(
        body, name="sb_bwd", grid=(Bl, HEADS),
        in_specs=[slot(0), slot(1), slot(2),
                  pl.BlockSpec((None, None, S, 1), lambda b, h: (b, h, 0, 0)), head, gain, gain, ANY] + [ANY] * ne,
        out_specs=[pl.BlockSpec((3, None, S, HD), lambda b, h: (0, b, 0, h)), gain, gain] + [ANY] * ne,
        out_shape=[jax.ShapeDtypeStruct(dproj.shape, dproj.dtype),
                   jax.ShapeDtypeStruct((1, HD), F32), jax.ShapeDtypeStruct((1, HD), F32)] + x_shape,
        scratch_shapes=([pltpu.VMEM((S, HD), BF16)] * 3 + [pltpu.VMEM((S, HD), F32)] * 3
                        + [pltpu.VMEM((S, 1), F32)] * 2 + x_sems),
        input_output_aliases={7: 0},
        compiler_params=_cparams(("arbitrary", "arbitrary"), vmem_mib=56),
    )(proj, proj, proj, ctot, do, q_gain, k_gain, dproj, *exchange)


def _lower_bound(logits):
    l0, l1 = logits[0:1, :], logits[1:2, :]
    m = jnp.maximum(l0, l1)
    e0, e1 = jnp.exp(l0 - m), jnp.exp(l1 - m)
    p0, p1 = e0 / (e0 + e1), e1 / (e0 + e1)
    return (p0 + p1) - p0, p0 * p1


def _hg_gates(qr, fp, lbv):
    sq = _sigmoid(qr)
    sp = _sigmoid(fp)
    sn = 1.0 / (1.0 + jnp.exp(fp))
    f = lbv + (1.0 - lbv) * sp
    return qr * sq, sq, sp, sn, f, (1.0 - lbv) * sn


def _group_consts():
    t = lax.broadcasted_iota(jnp.int32, (GR, GR), 0)
    j = lax.broadcasted_iota(jnp.int32, (GR, GR), 1)
    same = lax.shift_right_logical(t, CH_LOG2) == lax.shift_right_logical(j, CH_LOG2)
    tril = jnp.logical_and(same, j <= t)
    return (tril, tril.astype(BF16), jnp.logical_and(same, j >= t).astype(BF16), same.astype(BF16))


def _hg_decays(qa, k, f, t_inc, t_same):
    g = jnp.log(f)
    gc = _cum2l(t_inc, g)
    gl = _cum2l(t_same, g)
    gm = gc - 0.5 * gl
    e_q = jnp.exp(jnp.minimum(gm, EXP_CLAMP))
    e_k = jnp.exp(jnp.minimum(-gm, EXP_CLAMP))
    e_g = jnp.exp(gc)
    e_l = jnp.exp(gl - gc)
    return qa * e_q, k * e_k, qa * e_g, k * e_l, e_q, e_k, e_g, e_l, jnp.exp(gl)


def _hg_fwd(proj, lb_logits):
    _, Bl, S, _ = proj.shape
    nc = S // CH

    def body(q_ref, f_ref, i_ref, lg_ref, o_ref, st_ref, qg_s, kd_s, egl_s):
        lbv, _ = _lower_bound(lg_ref[...])
        tril, t_inc, _, t_same = _group_consts()

        def intra(n, _):
            for u in range(GROUPS):
                rs = pl.ds(pl.multiple_of((GROUPS * n + u) * GR, GR), GR)
                qa, _, _, _, f, k = _hg_gates(q_ref[rs, :], f_ref[rs, :], lbv)
                qt, kt, qg, kd, _, _, _, _, e_gl = _hg_decays(qa, k, f, t_inc, t_same)
                a = jnp.where(tril, _dot(qt.astype(BF16), kt.astype(BF16), NT), 0.0)
                o_ref[rs, :] = _dot(a.astype(BF16), i_ref[rs, :].astype(BF16))
                qg_s[rs, :] = qg.astype(BF16)
                kd_s[rs, :] = kd.astype(BF16)
                egl_s[rs, :] = e_gl
            return 0

        lax.fori_loop(0, S // (GROUPS * GR), intra, 0)

        st = jnp.zeros((HD, HD), F32)
        for c in range(nc):
            rs = slice(c * CH, (c + 1) * CH)
            st_ref[c] = st
            o_ref[rs, :] += _dot(qg_s[rs, :], st.astype(BF16), NT)
            st = st * egl_s[c * CH:c * CH + 1, :] + _dot(i_ref[rs, :].astype(BF16), kd_s[rs, :], TN)

    def slot(n):
        return pl.BlockSpec((None, None, S, HD), lambda b, h: (n, b, 0, h))

    return pl.pallas_call(
        body, name="hg_fwd", grid=(Bl, HEADS),
        in_specs=[slot(0), slot(1), slot(2), pl.BlockSpec((2, HD), lambda b, h: (0, h))],
        out_specs=[pl.BlockSpec((None, S, HD), lambda b, h: (b, 0, h)),
                   pl.BlockSpec((None, None, nc, HD, HD), lambda b, h: (b, h, 0, 0, 0))],
        out_shape=[jax.ShapeDtypeStruct((Bl, S, D), F32),
                   jax.ShapeDtypeStruct((Bl, HEADS, nc, HD, HD), F32)],
        scratch_shapes=[pltpu.VMEM((S, HD), BF16)] * 2 + [pltpu.VMEM((S, HD), F32)],
        compiler_params=_cparams(("parallel", "parallel")),
    )(proj, proj, proj, lb_logits)


def _hg_bwd(proj, states, do, dproj, lb_logits):
    _, Bl, S, _ = proj.shape
    nc = S // CH

    def body(q_ref, f_ref, i_ref, st_ref, do_ref, lg_ref, _, dqfi_ref, dlb_ref,
             qg_s, kd_s, egl_s, dqg_s, dkd_s, dse_s, di_s):
        lbv, _ = _lower_bound(lg_ref[...])
        tril, t_inc, t_dec, t_same = _group_consts()

        def decays(n, _):
            for u in range(GROUPS):
                rs = pl.ds(pl.multiple_of((GROUPS * n + u) * GR, GR), GR)
                qa, _, _, _, f, k = _hg_gates(q_ref[rs, :], f_ref[rs, :], lbv)
                _, _, qg, kd, _, _, _, _, e_gl = _hg_decays(qa, k, f, t_inc, t_same)
                qg_s[rs, :] = qg.astype(BF16)
                kd_s[rs, :] = kd.astype(BF16)
                egl_s[rs, :] = e_gl
            return 0

        lax.fori_loop(0, S // (GROUPS * GR), decays, 0)

        dst = jnp.zeros((HD, HD), F32)
        for c in reversed(range(nc)):
            rs = slice(c * CH, (c + 1) * CH)
            st = st_ref[c]
            dstb = dst.astype(BF16)
            dob = do_ref[rs, :]
            dqg_s[rs, :] = _dot(dob, st.astype(BF16))
            dkd_s[rs, :] = _dot(i_ref[rs, :].astype(BF16), dstb)
            di_s[rs, :] = _dot(kd_s[rs, :], dstb, NT)
            dse_s[rs, :] = jnp.broadcast_to(jnp.sum(dst * st, axis=0, keepdims=True), (CH, HD))
            dst = dst * egl_s[c * CH:c * CH + 1, :] + _dot(dob, qg_s[rs, :], TN)

        def intra(n, dlb):
            for u in range(GROUPS):
                rs = pl.ds(pl.multiple_of((GROUPS * n + u) * GR, GR), GR)
                qr, fp = q_ref[rs, :], f_ref[rs, :]
                qa, sq, sp, sn, f, k = _hg_gates(qr, fp, lbv)
                qt, kt, qg, kd, e_q, e_k, e_g, e_l, e_gl = _hg_decays(qa, k, f, t_inc, t_same)
                ib = i_ref[rs, :].astype(BF16)
                dob = do_ref[rs, :]
                ab = jnp.where(tril, _dot(qt.astype(BF16), kt.astype(BF16), NT), 0.0).astype(BF16)
                da = jnp.where(tril, _dot(dob, ib, NT), 0.0)
                dqt = _dot3(da, kt)
                dkt = _dot3(da, qt, TN)
                dqfi_ref[2, rs, :] = (di_s[rs, :] + _dot(ab, dob, TN)).astype(BF16)
                dqg, dkd = dqg_s[rs, :], dkd_s[rs, :]
                dgc = dqt * qt - dkt * kt + dqg * qg - dkd * kd
                dg = _cum2l(t_dec, dgc) + _cum2l(t_same, dkd * kd) + dse_s[rs, :] * e_gl
                t1 = dg / f - (dkt * e_k + dkd * e_l)
                dqfi_ref[1, rs, :] = ((1.0 - lbv) * t1 * sp * sn).astype(BF16)
                dqfi_ref[0, rs, :] = ((dqt * e_q + dqg * e_g) * (sq * (1.0 + qr * (1.0 - sq)))).astype(BF16)
                dlb = dlb + jnp.sum(sn * t1, axis=0, keepdims=True)
            return dlb

        dlb = lax.fori_loop(0, S // (GROUPS * GR), intra, jnp.zeros((1, HD), F32))

        @pl.when(pl.program_id(1) == 0)
        def _():
            dlb_ref[...] = dlb

        @pl.when(pl.program_id(1) != 0)
        def _():
            dlb_ref[...] += dlb

    def slot(n):
        return pl.BlockSpec((None, None, S, HD), lambda h, b: (n, b, 0, h))

    return pl.pallas_call(
        body, name="hg_bwd", grid=(HEADS, Bl),
        in_specs=[slot(0), slot(1), slot(2),
                  pl.BlockSpec((None, None, nc, HD, HD), lambda h, b: (b, h, 0, 0, 0)),
                  pl.BlockSpec((None, S, HD), lambda h, b: (b, 0, h)),
                  pl.BlockSpec((2, HD), lambda h, b: (0, h)), ANY],
        out_specs=[pl.BlockSpec((3, None, S, HD), lambda h, b: (0, b, 0, h)),
                   pl.BlockSpec((1, HD), lambda h, b: (0, h))],
        out_shape=[jax.ShapeDtypeStruct(dproj.shape, dproj.dtype), jax.ShapeDtypeStruct((1, D), F32)],
        scratch_shapes=[pltpu.VMEM((S, HD), BF16)] * 2 + [pltpu.VMEM((S, HD), F32)] * 5,
        input_output_aliases={6: 0},
        compiler_params=_cparams(("parallel", "arbitrary")),
    )(proj, proj, proj, states, do, lb_logits, dproj)


def _place():
    x, y, c = lax.axis_index("x"), lax.axis_index("y"), lax.axis_index("c")
    return x, y, c, [(1 - x, y), (x, 1 - y), (1 - x, 1 - y)]


def _remote(src, dst, ssem, rsem, dev):
    return pltpu.make_async_remote_copy(src_ref=src, dst_ref=dst, send_sem=ssem, recv_sem=rsem,
                                        device_id=dev, device_id_type=MESH)


GATHER_SEMS = 9


def _gather_ops(wp_ref, wall_ref, ssem, rsem, hn_ref=None, hnall_ref=None):
    half = wp_ref.shape[0] // 2

    def place():
        x, y, c, chips = _place()
        return x, y, c, chips, 2 * x + y, pl.ds(c * half, half), pl.ds((1 - c) * half, half)

    def first_sends():
        x, y, c, chips, b, mine, _ = place()
        cps = [_remote(wp_ref.at[mine], wall_ref.at[b, mine], ssem.at[j], rsem.at[j], (*chip, c))
               for j, chip in enumerate(chips)]
        if hn_ref is not None:
            cps += [_remote(hn_ref, hnall_ref.at[b], ssem.at[6 + j], rsem.at[6 + j], (*chip, c))
                    for j, chip in enumerate(chips)]
        return cps

    def forwards():
        x, y, c, chips, _, mine, _ = place()
        return [_remote(wall_ref.at[2 * cx + cy, mine], wall_ref.at[2 * cx + cy, mine],
                        ssem.at[3 + j], rsem.at[3 + j], (x, y, 1 - c)) for j, (cx, cy) in enumerate(chips)]

    def start():
        for cp in first_sends():
            cp.start()

    def forward():
        x, y, c, chips, _, mine, _ = place()
        for j, (cx, cy) in enumerate(chips):
            landed = wall_ref.at[2 * cx + cy, mine]
            _remote(landed, landed, ssem.at[j], rsem.at[j], (cx, cy, c)).wait_recv()
        for cp in forwards():
            cp.start()

    def finish():
        x, y, c, chips, _, _, other = place()
        for j, (cx, cy) in enumerate(chips):
            passed = wall_ref.at[2 * cx + cy, other]
            _remote(passed, passed, ssem.at[3 + j], rsem.at[3 + j], (x, y, 1 - c)).wait_recv()
            if hn_ref is not None:
                row = hnall_ref.at[2 * cx + cy]
                _remote(row, row, ssem.at[6 + j], rsem.at[6 + j], (cx, cy, c)).wait_recv()
        for cp in first_sends() + forwards():
            cp.wait_send()

    return start, forward, finish


def _gather_weights(wp):
    def body(wp_ref, wall_ref, ssem, rsem):
        for step in _gather_ops(wp_ref, wall_ref, ssem, rsem):
            step()

    return pl.pallas_call(
        body, name="gather_weights", in_specs=[ANY], out_specs=ANY,
        out_shape=jax.ShapeDtypeStruct((NPROJ,) + wp.shape, BF16),
        scratch_shapes=[pltpu.SemaphoreType.DMA((GATHER_SEMS,)), pltpu.SemaphoreType.DMA((GATHER_SEMS,))],
    )(wp)


def _pair_ops(g_refs, r_refs, ssem, rsem):
    def copies():
        x, y, c, _ = _place()
        return [_remote(g.at[n, 1 - c], r.at[n], ssem.at[t * NPROJ + n], rsem.at[t * NPROJ + n], (x, y, 1 - c))
                for t, (g, r) in enumerate(zip(g_refs, r_refs)) for n in range(NPROJ)]

    def start():
        for cp in copies():
            cp.start()

    def finish():
        x, y, c, _ = _place()
        for t, r in enumerate(r_refs):
            for n in range(NPROJ):
                k = t * NPROJ + n
                _remote(r.at[n], r.at[n], ssem.at[k], rsem.at[k], (x, y, 1 - c)).wait_recv()
        for cp in copies():
            cp.wait_send()

    return start, finish


def _pair_shapes(grads):
    return ([jax.ShapeDtypeStruct((NPROJ,) + g.shape[2:], F32) for g in grads],
            [pltpu.SemaphoreType.DMA((len(grads) * NPROJ,))] * 2)


def _pair_exchange(grads, pack):
    ng = len(grads)

    def body(*refs):
        g_refs, pack_ref = refs[:ng], refs[ng]
        r_refs, allp_ref = refs[ng + 1:2 * ng + 1], refs[2 * ng + 1]
        ssem, rsem, psend, precv, lsem = refs[2 * ng + 2:]
        x, y, c, _ = _place()
        me = 4 * x + 2 * y + c
        local = pltpu.make_async_copy(pack_ref, allp_ref.at[me], lsem)
        local.start()
        start, finish = _pair_ops(g_refs, r_refs, ssem, rsem)
        start()
        flips = [(fx, fy, fc) for fx in (0, 1) for fy in (0, 1) for fc in (0, 1)][1:]
        peers = [(fx + x - 2 * fx * x, fy + y - 2 * fy * y, fc + c - 2 * fc * c) for fx, fy, fc in flips]
        sends = [_remote(pack_ref, allp_ref.at[me], psend.at[m], precv.at[m], peer) for m, peer in enumerate(peers)]
        for cp in sends:
            cp.start()
        finish()
        for m, (px, py, pc) in enumerate(peers):
            row = allp_ref.at[4 * px + 2 * py + pc]
            _remote(row, row, psend.at[m], precv.at[m], (px, py, pc)).wait_recv()
        for cp in sends:
            cp.wait_send()
        local.wait()

    out_shape, sems = _pair_shapes(grads)
    return pl.pallas_call(
        body, name="pair_exchange", in_specs=[ANY] * (ng + 1), out_specs=[ANY] * (ng + 1),
        out_shape=out_shape + [jax.ShapeDtypeStruct((8,) + pack.shape, F32)],
        scratch_shapes=sems + [pltpu.SemaphoreType.DMA((7,)), pltpu.SemaphoreType.DMA((7,)),
                               pltpu.SemaphoreType.DMA],
    )(*grads, pack)


def _chip_ops(s_refs, r_refs, ssem, rsem):
    def copies():
        x, y, c, chips = _place()
        return [_remote(s.at[2 * cx + cy], r.at[2 * x + y], ssem.at[3 * t + j], rsem.at[3 * t + j], (cx, cy, c))
                for t, (s, r) in enumerate(zip(s_refs, r_refs)) for j, (cx, cy) in enumerate(chips)]

    def start():
        for cp in copies():
            cp.start()

    def finish():
        x, y, c, chips = _place()
        for t, r in enumerate(r_refs):
            for j, (cx, cy) in enumerate(chips):
                slot = r.at[2 * cx + cy]
                _remote(slot, slot, ssem.at[3 * t + j], rsem.at[3 * t + j], (cx, cy, c)).wait_recv()
        for cp in copies():
            cp.wait_send()

    return start, finish


def _chip_shapes(sums):
    return ([jax.ShapeDtypeStruct(s.shape, s.dtype) for s in sums],
            [pltpu.SemaphoreType.DMA((3 * len(sums),))] * 2)


def _chip_exchange(sums):
    ng = len(sums)

    def body(*refs):
        start, finish = _chip_ops(refs[:ng], refs[ng:2 * ng], *refs[2 * ng:])
        start()
        finish()

    out_shape, sems = _chip_shapes(sums)
    return pl.pallas_call(
        body, name="chip_exchange", in_specs=[ANY] * ng, out_specs=[ANY] * ng,
        out_shape=out_shape, scratch_shapes=sems,
    )(*sums)


def _sibling_share(halves):
    ng = len(halves)

    def body(*refs):
        h_refs, f_refs = refs[:ng], refs[ng:2 * ng]
        ssem, rsem = refs[2 * ng:]
        x, y, c, _ = _place()
        sends = [_remote(h, f, ssem.at[t], rsem.at[t], (x, y, 1 - c))
                 for t, (h, f) in enumerate(zip(h_refs, f_refs))]
        for cp in sends:
            cp.start()
        for t, f in enumerate(f_refs):
            _remote(f, f, ssem.at[t], rsem.at[t], (x, y, 1 - c)).wait_recv()
        for cp in sends:
            cp.wait_send()

    return pl.pallas_call(
        body, name="sibling_share", in_specs=[ANY] * ng, out_specs=[ANY] * ng,
        out_shape=[jax.ShapeDtypeStruct(h.shape, F32) for h in halves],
        scratch_shapes=[pltpu.SemaphoreType.DMA((ng,)), pltpu.SemaphoreType.DMA((ng,))],
    )(*halves)


def _pair_add(own, recv, cidx, name):
    R = own.shape[2]
    tr = min(256, R)

    def body(c_ref, a_ref, b_ref, o_ref):
        o_ref[...] = (a_ref[...] + b_ref[...]).astype(BF16)

    return pl.pallas_call(
        body, name=name,
        grid_spec=pltpu.PrefetchScalarGridSpec(
            num_scalar_prefetch=1, grid=(NPROJ, R // tr),
            in_specs=[pl.BlockSpec((None, None, tr, D), lambda n, r, c: (n, c[0], r, 0)),
                      pl.BlockSpec((None, tr, D), lambda n, r, c: (n, r, 0))],
            out_specs=pl.BlockSpec((None, tr, D), lambda n, r, c: (n, r, 0))),
        out_shape=jax.ShapeDtypeStruct(recv.shape, BF16),
        compiler_params=_cparams(("parallel", "parallel")),
    )(cidx, own, recv)


def _chip_sum(sums, parts, bidx, name):
    R = parts.shape[1]
    tr = min(256, R)

    def body(b_ref, s_ref, p_ref, o_ref):
        acc = None
        for j in range(NPROJ):
            term = jnp.where(b_ref[0] == j, s_ref[...], p_ref[j]).astype(F32)
            acc = term if acc is None else acc + term
        o_ref[...] = acc

    return pl.pallas_call(
        body, name=name,
        grid_spec=pltpu.PrefetchScalarGridSpec(
            num_scalar_prefetch=1, grid=(R // tr,),
            in_specs=[pl.BlockSpec((None, tr, D), lambda r, b: (b[0], r, 0)),
                      pl.BlockSpec((NPROJ, tr, D), lambda r, b: (0, r, 0))],
            out_specs=pl.BlockSpec((tr, D), lambda r, b: (r, 0))),
        out_shape=jax.ShapeDtypeStruct((R, D), F32),
        compiler_params=_cparams(("parallel",)),
    )(bidx, sums, parts)


def _adamw_math(w, g, m, v):
    m = ADAM_B1 * m + (1.0 - ADAM_B1) * g
    v = ADAM_B2 * v + (1.0 - ADAM_B2) * (g * g)
    m_hat = m / (1.0 - ADAM_B1 ** ADAM_STEP)
    v_hat = v / (1.0 - ADAM_B2 ** ADAM_STEP)
    delta = -ADAM_LR * (m_hat / (jnp.sqrt(v_hat) + ADAM_EPS) + ADAM_WD * w)
    return delta, m, v


def _adamw(w, mine, theirs, m, v, cidx, name):
    R = mine.shape[0]
    tr = min(256, R)
    nr = R // tr

    def body(c_ref, w_ref, a_ref, b_ref, m_ref, v_ref, g_ref, d_ref, nm_ref, nv_ref):
        g = jnp.where(pl.program_id(0) == c_ref[0], a_ref[...], b_ref[...])
        g_ref[...] = g
        d_ref[...], nm_ref[...], nv_ref[...] = _adamw_math(w_ref[...], g, m_ref[...], v_ref[...])

    full = pl.BlockSpec((tr, D), lambda h, r, c: (h * nr + r, 0))
    half = pl.BlockSpec((tr, D), lambda h, r, c: (r, 0))
    return pl.pallas_call(
        body, name=name,
        grid_spec=pltpu.PrefetchScalarGridSpec(
            num_scalar_prefetch=1, grid=(2, nr),
            in_specs=[full, half, half, full, full], out_specs=[full] * 4),
        out_shape=[jax.ShapeDtypeStruct(w.shape, F32)] * 4,
        compiler_params=_cparams(("parallel", "parallel")),
    )(cidx, w, mine, theirs, m, v)


PACK_ROWS = 8


def _small_update(allp, bidx, logits, weights, moments_m, moments_v):
    shapes = [w.shape for w in weights]
    q4 = D // NPROJ

    def body(b_ref, allp_ref, hgp_ref, lg_ref, *refs):
        w_refs, m_refs, v_refs = refs[0:6], refs[6:12], refs[12:18]
        loss_ref = refs[18]
        g_out, d_out, m_out, v_out = refs[19:25], refs[25:31], refs[31:37], refs[37:43]

        def total(ref, row, lo, hi):
            acc = ref[0, row:row + 1, lo:hi]
            for dev in range(1, 8):
                acc = acc + ref[dev, row:row + 1, lo:hi]
            return acc

        _, pp = _lower_bound(lg_ref[...])
        dlb = total(allp_ref, 2, 0, D)
        grads = [total(allp_ref, 0, 0, D), total(allp_ref, 4, 0, HD), total(allp_ref, 4, HD, 2 * HD),
                 total(hgp_ref, 1, 0, q4), total(allp_ref, 4, 2 * HD, 3 * HD), None]
        loss_ref[...] = (0.5 / D) * jnp.sum(total(allp_ref, 3, 0, D), axis=1, keepdims=True)
        for t in range(6):
            if t < 5:
                rows = [(slice(None), grads[t])]
            else:
                rows = [(slice(0, 1), -pp * dlb), (slice(1, 2), pp * dlb)]
            for rs, g in rows:
                g_out[t][rs, :] = g
                d_out[t][rs, :], m_out[t][rs, :], v_out[t][rs, :] = _adamw_math(
                    w_refs[t][rs, :], g, m_refs[t][rs, :], v_refs[t][rs, :])

    whole = [pl.BlockSpec(s, lambda i, b: (0, 0)) for s in shapes]
    return pl.pallas_call(
        body, name="small_update",
        grid_spec=pltpu.PrefetchScalarGridSpec(
            num_scalar_prefetch=1, grid=(1,),
            in_specs=[pl.BlockSpec((8, PACK_ROWS, D), lambda i, b: (0, 0, 0)),
                      pl.BlockSpec((8, PACK_ROWS, q4), lambda i, b: (0, 0, b[0])),
                      pl.BlockSpec((2, D), lambda i, b: (0, 0))] + whole * 3,
            out_specs=[pl.BlockSpec((1, 1), lambda i, b: (0, 0))] + whole * 4),
        out_shape=[jax.ShapeDtypeStruct((1, 1), F32)] + [jax.ShapeDtypeStruct(s, F32) for s in shapes] * 4,
        compiler_params=_cparams(("arbitrary",)),
    )(bidx, allp, allp, logits, *weights, *moments_m, *moments_v)


def kernel(x, sb_norm, sb_w_in, sb_q_gain, sb_k_gain, sb_w_out, hg_norm, hg_w_in, hg_o_gain, hg_w_out, hg_lb_logits, loss_target, m_sb_norm, m_sb_w_in, m_sb_q_gain, m_sb_k_gain, m_sb_w_out, m_hg_norm, m_hg_w_in, m_hg_o_gain, m_hg_w_out, m_hg_lb_logits, v_sb_norm, v_sb_w_in, v_sb_q_gain, v_sb_k_gain, v_sb_w_out, v_hg_norm, v_hg_w_in, v_hg_o_gain, v_hg_w_out, v_hg_lb_logits):
    Bl, S, _ = x.shape
    T = Bl * S
    cidx = lax.axis_index("c").astype(jnp.int32).reshape(1)
    bidx = (2 * lax.axis_index("x") + lax.axis_index("y")).astype(jnp.int32).reshape(1)

    def own_slot(gathered, mine):
        return lax.dynamic_update_slice(gathered, mine[None], (bidx[0],) + (0,) * mine.ndim)

    def halved(g):
        return g.reshape(NPROJ, 2, g.shape[-2] * g.shape[0] // (2 * NPROJ), D)

    def heads(a):
        return a.reshape(a.shape[:-2] + (Bl, S, D))

    def flat(a):
        return a.reshape(a.shape[:-3] + (T, D))

    wp_sb = jnp.concatenate([sb_w_in[0], sb_w_out[0]], axis=0).astype(BF16)
    wp_hg = jnp.concatenate([hg_w_in[0], hg_w_out[0]], axis=0).astype(BF16)
    wall_sb = own_slot(_gather_weights(wp_sb), wp_sb)
    x2 = x.reshape(T, D)
    tgt = loss_target.reshape(T, D)

    proj0 = _in_proj_fwd(x2, sb_norm, wall_sb, W_IN, "sb_in_fwd")
    o0, ctot, wall_hg, hnall = _sb_fwd(heads(proj0), sb_q_gain, sb_k_gain, wp_hg, hg_norm)
    wall_hg = own_slot(wall_hg, wp_hg)
    hgn = own_slot(hnall, hg_norm).reshape(1, D)
    h1 = _out_proj_fwd(flat(o0), proj0, x2, wall_sb, W_OUT, "sb_out_fwd")
    proj1 = _in_proj_fwd(h1, hgn, wall_hg, W_IN, "hg_in_fwd")
    o1, states = _hg_fwd(heads(proj1), hg_lb_logits)
    dh2, loss_terms = _out_proj_fwd(flat(o1), proj1, h1, wall_hg, W_OUT, "hg_out_fwd",
                                    o_gain=hg_o_gain, target=tgt)

    do1, dproj1, gout_hg, d_ogain = _out_proj_bwd(dh2, flat(o1), proj1, wall_hg, W_OUT, "hg_out_bwd",
                                                  o_gain=hg_o_gain)
    dproj1, dlb = _hg_bwd(heads(proj1), states, heads(do1), heads(dproj1), hg_lb_logits)
    dproj1 = flat(dproj1)
    dh1, d_hgn = _in_proj_bwd_x(dproj1, wall_hg, W_IN, h1, hgn, dh2, "hg_in_bwd_x")
    big_hg = [halved(_in_proj_bwd_w(dproj1, h1, hgn, "hg_in_bwd_w")), halved(gout_hg)]
    do0, dproj0, gout_sb, *recv_hg = _out_proj_bwd(dh1, flat(o0), proj0, wall_sb, W_OUT, "sb_out_bwd",
                                                   exchange=big_hg)
    sums_hg = [_pair_add(g, r, cidx, "pair_add_" + nm) for g, r, nm in zip(big_hg, recv_hg, ("hg_in", "hg_out"))]
    dproj0, d_qg, d_kg, *parts_hg = _sb_bwd(heads(proj0), ctot, heads(do0), heads(dproj0),
                                            sb_q_gain, sb_k_gain, sums_hg)
    dproj0 = flat(dproj0)
    grad_x, d_sbn = _in_proj_bwd_x(dproj0, wall_sb, W_IN, x2, sb_norm, dh1, "sb_in_bwd_x")
    big_sb = [halved(_in_proj_bwd_w(dproj0, x2, sb_norm, "sb_in_bwd_w")), halved(gout_sb)]

    gains = jnp.concatenate([d_qg, d_kg, d_ogain, jnp.zeros((1, D - 3 * HD), F32)], axis=1)
    pack = jnp.concatenate([d_sbn, d_hgn, dlb, loss_terms, gains, jnp.zeros((3, D), F32)], axis=0)
    *recv_sb, allp = _pair_exchange(big_sb, pack)
    sums_sb = [_pair_add(g, r, cidx, "pair_add_" + nm) for g, r, nm in zip(big_sb, recv_sb, ("sb_in", "sb_out"))]
    parts_sb = _chip_exchange(sums_sb)
    names = ["sb_in", "hg_in", "sb_out", "hg_out"]
    sums = [sums_sb[0], sums_hg[0], sums_sb[1], sums_hg[1]]
    parts = [parts_sb[0], parts_hg[0], parts_sb[1], parts_hg[1]]
    halves = [_chip_sum(sm, p, bidx, "chip_sum_" + nm) for sm, p, nm in zip(sums, parts, names)]
    theirs = _sibling_share(halves)

    big_w = [sb_w_in, hg_w_in, sb_w_out, hg_w_out]
    big_m = [m_sb_w_in, m_hg_w_in, m_sb_w_out, m_hg_w_out]
    big_v = [v_sb_w_in, v_hg_w_in, v_sb_w_out, v_hg_w_out]
    upd = [_adamw(w[0], a, b, m[0], v[0], cidx, "adamw_" + nm)
           for w, a, b, m, v, nm in zip(big_w, halves, theirs, big_m, big_v, names)]
    (g_sb_in, d_sb_in, nm_sb_in, nv_sb_in), (g_hg_in, d_hg_in, nm_hg_in, nv_hg_in), \
        (g_sb_out, d_sb_out, nm_sb_out, nv_sb_out), (g_hg_out, d_hg_out, nm_hg_out, nv_hg_out) = [
            tuple(a[None] for a in u) for u in upd]

    small = _small_update(
        allp, bidx, hg_lb_logits,
        [sb_norm, sb_q_gain, sb_k_gain, hg_norm, hg_o_gain, hg_lb_logits],
        [m_sb_norm, m_sb_q_gain, m_sb_k_gain, m_hg_norm, m_hg_o_gain, m_hg_lb_logits],
        [v_sb_norm, v_sb_q_gain, v_sb_k_gain, v_hg_norm, v_hg_o_gain, v_hg_lb_logits])
    loss = small[0].reshape(())
    (g_sbn, g_qg, g_kg, g_hgn, g_og, g_lb) = small[1:7]
    (d_sbn2, d_qg2, d_kg2, d_hgn2, d_og2, d_lb2) = small[7:13]
    (nm_sbn, nm_qg, nm_kg, nm_hgn, nm_og, nm_lb) = small[13:19]
    (nv_sbn, nv_qg, nv_kg, nv_hgn, nv_og, nv_lb) = small[19:25]

    return (loss, grad_x.reshape(Bl, S, D),
            g_sbn, g_sb_in, g_qg, g_kg, g_sb_out, g_hgn, g_hg_in, g_og, g_hg_out, g_lb,
            d_sbn2, d_sb_in, d_qg2, d_kg2, d_sb_out, d_hgn2, d_hg_in, d_og2, d_hg_out, d_lb2,
            nm_sbn, nm_sb_in, nm_qg, nm_kg, nm_sb_out, nm_hgn, nm_hg_in, nm_og, nm_hg_out, nm_lb,
            nv_sbn, nv_sb_in, nv_qg, nv_kg, nv_sb_out, nv_hgn, nv_hg_in, nv_og, nv_hg_out, nv_lb)
```

```python
import functools

import jax
import jax.numpy as jnp
from jax import lax
from jax.experimental import pallas as pl
from jax.experimental.pallas import tpu as pltpu

F32 = jnp.float32
BF16 = jnp.bfloat16
MESH = pl.DeviceIdType.MESH
ANY = pl.BlockSpec(memory_space=pl.ANY)

D = 1024
HEADS = 8
HD = 128
NPROJ = 4
RMS_EPS = 1e-6
TK = 256
CH = 64
CH_LOG2 = 6
GR = 256
SCALE = HD ** -0.5
EXP_CLAMP = 60.0
W_IN, W_OUT = 0, 4

ADAM_LR = 0.001
ADAM_B1 = 0.9
ADAM_B2 = 0.999
ADAM_EPS = 1e-08
ADAM_WD = 0.01
ADAM_STEP = 10

NT = (((1,), (1,)), ((), ()))
TN = (((0,), (0,)), ((), ()))
MIB = 1024 * 1024


def _cparams(sem=None, vmem_mib=40):
    return pltpu.CompilerParams(dimension_semantics=sem, vmem_limit_bytes=vmem_mib * MIB)


def _dot(a, b, dims=None):
    if dims is None:
        return jnp.dot(a, b, preferred_element_type=F32)
    return lax.dot_general(a, b, dims, preferred_element_type=F32)


def _sigmoid(x):
    return 1.0 / (1.0 + jnp.exp(-x))


def _rms(x):
    return lax.rsqrt(jnp.mean(x * x, axis=-1, keepdims=True) + RMS_EPS)


def _rms_bwd(x, r, gain, dy):
    a = dy * gain
    dx = r * a - x * (r * r * r) * jnp.mean(x * a, axis=-1, keepdims=True)
    return dx, dy * (x * r)


def _split2(v):
    hi = v.astype(BF16)
    lo = (v - hi.astype(F32)).astype(BF16)
    return hi, lo


def _cum2(v, u):
    hi, lo = _split2(v)
    return _dot(hi, u) + _dot(lo, u)


def _dot3(a, b, dims=None):
    ah, al = _split2(a)
    bh, bl = _split2(b)
    return _dot(ah, bh, dims) + _dot(ah, bl, dims) + _dot(al, bh, dims)


def _cum2l(u, v):
    hi, lo = _split2(v)
    return _dot(u, hi) + _dot(u, lo)


def _in_proj_fwd(h, gain, wall, wblk, name):
    T = h.shape[0]
    tm = min(512, T)

    def body(h_ref, g_ref, w_ref, o_ref, u_s):
        rows = pl.ds(pl.multiple_of(pl.program_id(1) * tm, tm), tm)

        @pl.when(pl.program_id(0) == 0)
        def _():
            x = h_ref[...]
            u_s[rows, :] = (x * _rms(x) * g_ref[...]).astype(BF16)

        o_ref[...] = _dot(u_s[rows, :], w_ref[...])

    return pl.pallas_call(
        body, name=name, grid=(NPROJ, T // tm),
        in_specs=[pl.BlockSpec((tm, D), lambda n, i: (jnp.where(n == 0, i, 0), 0)),
                  pl.BlockSpec((1, D), lambda n, i: (0, 0)),
                  pl.BlockSpec((None, D, D), lambda n, i: (n, wblk, 0))],
        out_specs=pl.BlockSpec((None, tm, D), lambda n, i: (n, i, 0)),
        out_shape=jax.ShapeDtypeStruct((NPROJ, T, D), F32),
        scratch_shapes=[pltpu.VMEM((T, D), BF16)],
        compiler_params=_cparams(("arbitrary", "arbitrary")),
    )(h, gain, wall)


def _head_norm(x):
    outs = []
    for hh in range(x.shape[1] // HD):
        xs = x[:, hh * HD:(hh + 1) * HD]
        outs.append((xs, _rms(xs)))
    return outs


def _w_out_specs(wblk):
    kb = D // NPROJ
    return [pl.BlockSpec((None, kb, D), functools.partial(lambda j, i: (j, wblk, 0), j)) for j in range(NPROJ)]


def _out_proj_fwd(o, proj, resid, wall, wblk, name, o_gain=None, target=None):
    T = o.shape[0]
    tm = min(512, T)
    kb = D // NPROJ
    with_loss = target is not None

    def body(*refs):
        o_ref, g_ref, r_ref = refs[:3]
        w_refs = refs[3:3 + NPROJ]
        if with_loss:
            og_ref, t_ref, dh_ref, ls_ref = refs[3 + NPROJ:]
        else:
            h_ref, = refs[3 + NPROJ:]
        x = o_ref[...]
        if with_loss:
            x = jnp.concatenate([xs * r * og_ref[...] for xs, r in _head_norm(x)], axis=1)
        g = g_ref[...]
        a = (x * (g * _sigmoid(g))).astype(BF16)
        hnew = r_ref[...]
        for j in range(NPROJ):
            hnew = hnew + _dot(a[:, j * kb:(j + 1) * kb], w_refs[j][...])
        if with_loss:
            err = hnew - t_ref[...]
            dh_ref[...] = err * (1.0 / D)
            part = jnp.sum(err * err, axis=0, keepdims=True)

            @pl.when(pl.program_id(0) == 0)
            def _():
                ls_ref[...] = part

            @pl.when(pl.program_id(0) != 0)
            def _():
                ls_ref[...] += part
        else:
            h_ref[...] = hnew

    tile = pl.BlockSpec((tm, D), lambda i: (i, 0))
    in_specs = [tile, pl.BlockSpec((None, tm, D), lambda i: (3, i, 0)), tile] + _w_out_specs(wblk)
    args = [o, proj, resid] + [wall] * NPROJ
    out_specs = tile
    out_shape = jax.ShapeDtypeStruct((T, D), F32)
    if with_loss:
        in_specs += [pl.BlockSpec((1, HD), lambda i: (0, 0)), tile]
        args += [o_gain, target]
        out_specs = [tile, pl.BlockSpec((1, D), lambda i: (0, 0))]
        out_shape = [out_shape, jax.ShapeDtypeStruct((1, D), F32)]
    return pl.pallas_call(
        body, name=name, grid=(T // tm,), in_specs=in_specs, out_specs=out_specs,
        out_shape=out_shape, compiler_params=_cparams(("arbitrary",)),
    )(*args)


def _out_proj_bwd(dy, o, proj, wall, wblk, name, o_gain=None, exchange=()):
    T = o.shape[0]
    tm = min(512, T)
    kb = D // NPROJ
    normed = o_gain is not None
    ne = len(exchange)

    def body(*refs):
        it = iter(refs)
        dy_ref, o_ref, g_ref = (next(it) for _ in range(3))
        w_refs = [next(it) for _ in range(NPROJ)]
        og_ref = next(it) if normed else None
        xg_refs = [next(it) for _ in range(ne)]
        do_ref, dg_ref, dw_ref = (next(it) for _ in range(3))
        dgain_ref = next(it) if normed else None
        xr_refs = [next(it) for _ in range(ne)]
        wt_s = next(it)
        first = pl.program_id(0) == 0
        if ne:
            start, finish = _pair_ops(xg_refs, xr_refs, next(it), next(it))
            pl.when(first)(start)

        @pl.when(first)
        def _():
            for j in range(NPROJ):
                wt_s[:, j * kb:(j + 1) * kb] = w_refs[j][...].T
        g = g_ref[...]
        s = _sigmoid(g)
        sl = g * s
        x = o_ref[...]
        if normed:
            heads = _head_norm(x)
            on = jnp.concatenate([xs * r * og_ref[...] for xs, r in heads], axis=1)
        else:
            on = x
        dyb = dy_ref[...].astype(BF16)
        a = (on * sl).astype(BF16)
        for j in range(NPROJ):
            part = _dot(a[:, j * kb:(j + 1) * kb], dyb, TN)

            @pl.when(first)
            def _():
                dw_ref[j] = part

            @pl.when(jnp.logical_not(first))
            def _():
                dw_ref[j] += part

        da = _dot(dyb, wt_s[...])
        d_on = da * sl
        dg_ref[...] = (da * on * (s * (1.0 + g * (1.0 - s)))).astype(BF16)
        if normed:
            dxs, gsum = [], None
            for hh, (xs, r) in enumerate(heads):
                dx, gt = _rms_bwd(xs, r, og_ref[...], d_on[:, hh * HD:(hh + 1) * HD])
                dxs.append(dx)
                gt = jnp.sum(gt, axis=0, keepdims=True)
                gsum = gt if gsum is None else gsum + gt
            do_ref[...] = jnp.concatenate(dxs, axis=1).astype(BF16)

            @pl.when(first)
            def _():
                dgain_ref[...] = gsum

            @pl.when(jnp.logical_not(first))
            def _():
                dgain_ref[...] += gsum
        else:
            do_ref[...] = d_on.astype(BF16)
        if ne:
            pl.when(pl.program_id(0) == T // tm - 1)(finish)

    tile = pl.BlockSpec((tm, D), lambda i: (i, 0))
    gate = pl.BlockSpec((None, tm, D), lambda i: (3, i, 0))
    in_specs = [tile, tile, gate] + _w_out_specs(wblk)
    args = [dy, o, proj] + [wall] * NPROJ
    out_specs = [tile, gate, pl.BlockSpec((NPROJ, kb, D), lambda i: (0, 0, 0))]
    out_shape = [jax.ShapeDtypeStruct((T, D), BF16),
                 jax.ShapeDtypeStruct((NPROJ, T, D), BF16),
                 jax.ShapeDtypeStruct((NPROJ, kb, D), F32)]
    if normed:
        in_specs.append(pl.BlockSpec((1, HD), lambda i: (0, 0)))
        args.append(o_gain)
        out_specs.append(pl.BlockSpec((1, HD), lambda i: (0, 0)))
        out_shape.append(jax.ShapeDtypeStruct((1, HD), F32))
    x_shape, x_sems = _pair_shapes(exchange) if ne else ([], [])
    return pl.pallas_call(
        body, name=name, grid=(T // tm,), in_specs=in_specs + [ANY] * ne, out_specs=out_specs + [ANY] * ne,
        out_shape=out_shape + x_shape, scratch_shapes=[pltpu.VMEM((D, D), BF16)] + x_sems,
        compiler_params=_cparams(("arbitrary",), vmem_mib=48),
    )(*args, *exchange)


def _in_proj_bwd_x(dproj, wall, wblk, h, gain, dres, name):
    T = h.shape[0]
    tm = min(512, T)

    def body(d_ref, w_ref, h_ref, g_ref, r_ref, dh_ref, dgain_ref, du, wt_s):
        i, n = pl.program_id(0), pl.program_id(1)

        @pl.when(i == 0)
        def _():
            wt_s[n] = w_ref[...].T

        part = _dot(d_ref[...], wt_s[n])

        @pl.when(n == 0)
        def _():
            du[...] = part

        @pl.when(n != 0)
        def _():
            du[...] += part

        @pl.when(n == NPROJ - 1)
        def _():
            x = h_ref[...]
            dx, gt = _rms_bwd(x, _rms(x), g_ref[...], du[...])
            dh_ref[...] = r_ref[...] + dx
            gt = jnp.sum(gt, axis=0, keepdims=True)

            @pl.when(i == 0)
            def _():
                dgain_ref[...] = gt

            @pl.when(i != 0)
            def _():
                dgain_ref[...] += gt

    return pl.pallas_call(
        body, name=name, grid=(T // tm, NPROJ),
        in_specs=[pl.BlockSpec((None, tm, D), lambda i, n: (n, i, 0)),
                  pl.BlockSpec((None, D, D), lambda i, n: (jnp.where(i == 0, n, NPROJ - 1), wblk, 0)),
                  pl.BlockSpec((tm, D), lambda i, n: (i, 0)),
                  pl.BlockSpec((1, D), lambda i, n: (0, 0)),
                  pl.BlockSpec((tm, D), lambda i, n: (i, 0))],
        out_specs=[pl.BlockSpec((tm, D), lambda i, n: (i, 0)),
                   pl.BlockSpec((1, D), lambda i, n: (0, 0))],
        out_shape=[jax.ShapeDtypeStruct((T, D), F32), jax.ShapeDtypeStruct((1, D), F32)],
        scratch_shapes=[pltpu.VMEM((tm, D), F32), pltpu.VMEM((NPROJ, D, D), BF16)],
        compiler_params=_cparams(("arbitrary", "arbitrary")),
    )(dproj, wall, h, gain, dres)


def _in_proj_bwd_w(dproj, h, gain, name):
    T = h.shape[0]
    tk = min(512, T)

    def body(d_ref, h_ref, g_ref, dw_ref, ut_s):
        k = pl.program_id(1)

        @pl.when(pl.program_id(0) == 0)
        def _():
            x = h_ref[...]
            ut_s[k] = (x * _rms(x) * g_ref[...]).astype(BF16).T

        part = _dot(ut_s[k], d_ref[...])

        @pl.when(k == 0)
        def _():
            dw_ref[...] = part

        @pl.when(k != 0)
        def _():
            dw_ref[...] += part

    return pl.pallas_call(
        body, name=name, grid=(NPROJ, T // tk),
        in_specs=[pl.BlockSpec((None, tk, D), lambda n, k: (n, k, 0)),
                  pl.BlockSpec((tk, D), lambda n, k: (jnp.where(n == 0, k, 0), 0)),
                  pl.BlockSpec((1, D), lambda n, k: (0, 0))],
        out_specs=pl.BlockSpec((None, D, D), lambda n, k: (n, 0, 0)),
        out_shape=jax.ShapeDtypeStruct((NPROJ, D, D), F32),
        scratch_shapes=[pltpu.VMEM((T // tk, D, tk), BF16)],
        compiler_params=_cparams(("arbitrary", "arbitrary")),
    )(dproj, h, gain)


def _log_sigmoid_pair(z):
    lb = jnp.minimum(z, 0.0) - jnp.log(1.0 + jnp.exp(-jnp.abs(z)))
    return lb, lb - z


def _slab_consts():
    t = lax.broadcasted_iota(jnp.int32, (TK, TK), 0)
    s = lax.broadcasted_iota(jnp.int32, (TK, TK), 1)
    return s < t, (t > s).astype(BF16), (t < s).astype(BF16)


def _slab_rows(k0, S):
    return [(r0, r1, masked) for r0, r1, masked in ((k0, k0 + TK, True), (k0 + TK, S, False)) if r0 < r1]


def _sb_fwd(proj, q_gain, k_gain, wp, hn):
    _, Bl, S, _ = proj.shape
    steps = Bl * HEADS

    def body(q_ref, k_ref, v_ref, qg_ref, kg_ref, wp_ref, hn_ref, o_ref, ct_ref, wall_ref, hnall_ref,
             qn, kn, vb, ssem, rsem):
        step = pl.program_id(0) * HEADS + pl.program_id(1)
        start, forward, finish = _gather_ops(wp_ref, wall_ref, ssem, rsem, hn_ref, hnall_ref)
        pl.when(step == 0)(start)
        pl.when(step == steps // 2)(forward)
        q = q_ref[...]
        qn[...] = (q * _rms(q) * qg_ref[...]).astype(BF16)
        k = k_ref[...]
        kn[...] = (k * _rms(k) * kg_ref[...]).astype(BF16)
        vb[...] = v_ref[...].astype(BF16)
        o_ref[...] = jnp.zeros_like(o_ref)
        ct_ref[...] = jnp.zeros_like(ct_ref)
        tri, u_gt, _ = _slab_consts()
        for k0 in reversed(range(0, S, TK)):
            kb, vbb = kn[k0:k0 + TK, :], vb[k0:k0 + TK, :]
            for r0, r1, masked in _slab_rows(k0, S):
                z = _dot(qn[r0:r1, :], kb, NT) * SCALE
                lb, ls = _log_sigmoid_pair(z)
                if masked:
                    ls = jnp.where(tri, ls, 0.0)
                c = ct_ref[r0:r1, :]
                w = jnp.exp(lb + _cum2(ls, u_gt) + c)
                if masked:
                    w = jnp.where(tri, w, 0.0)
                o_ref[r0:r1, :] += _dot(w.astype(BF16), vbb)
                ct_ref[r0:r1, :] = c + jnp.sum(ls, axis=1, keepdims=True)
        pl.when(step == steps - 1)(finish)

    def slot(n):
        return pl.BlockSpec((None, None, S, HD), lambda b, h: (n, b, 0, h))

    return pl.pallas_call(
        body, name="sb_fwd", grid=(Bl, HEADS),
        in_specs=[slot(0), slot(1), slot(2),
                  pl.BlockSpec((1, HD), lambda b, h: (0, 0)),
                  pl.BlockSpec((1, HD), lambda b, h: (0, 0)), ANY, ANY],
        out_specs=[pl.BlockSpec((None, S, HD), lambda b, h: (b, 0, h)),
                   pl.BlockSpec((None, None, S, 1), lambda b, h: (b, h, 0, 0)), ANY, ANY],
        out_shape=[jax.ShapeDtypeStruct((Bl, S, D), F32),
                   jax.ShapeDtypeStruct((Bl, HEADS, S, 1), F32),
                   jax.ShapeDtypeStruct((NPROJ,) + wp.shape, BF16),
                   jax.ShapeDtypeStruct((NPROJ,) + hn.shape, F32)],
        scratch_shapes=[pltpu.VMEM((S, HD), BF16)] * 3 + [pltpu.SemaphoreType.DMA((GATHER_SEMS,))] * 2,
        compiler_params=_cparams(("arbitrary", "arbitrary"), vmem_mib=56),
    )(proj, proj, proj, q_gain, k_gain, wp, hn)


def _sb_bwd(proj, ctot, do, dproj, q_gain, k_gain, exchange):
    _, Bl, S, _ = proj.shape
    ne = len(exchange)

    def body(q_ref, k_ref, v_ref, ct_ref, do_ref, qg_ref, kg_ref, _, *refs):
        xs_refs, (dqkv_ref, dqg_ref, dkg_ref), xr_refs = refs[:ne], refs[ne:ne + 3], refs[ne + 3:2 * ne + 3]
        qn, kn, vb, dqn, dkn, dvn, passed_s, e_s, ssem, rsem = refs[2 * ne + 3:]
        step = pl.program_id(0) * HEADS + pl.program_id(1)
        first = step == 0
        start, finish = _chip_ops(xs_refs, xr_refs, ssem, rsem)

        @pl.when(first)
        def _():
            start()
            dqg_ref[...] = jnp.zeros_like(dqg_ref)
            dkg_ref[...] = jnp.zeros_like(dkg_ref)

        q = q_ref[...]
        rq = _rms(q)
        qn[...] = (q * rq * qg_ref[...]).astype(BF16)
        k = k_ref[...]
        rk = _rms(k)
        kn[...] = (k * rk * kg_ref[...]).astype(BF16)
        vb[...] = v_ref[...].astype(BF16)
        for acc in (dqn, dkn, dvn, passed_s, e_s):
            acc[...] = jnp.zeros_like(acc)
        tri, u_gt, u_lt = _slab_consts()
        for k0 in range(0, S, TK):
            keys = slice(k0, k0 + TK)
            kb, vbb = kn[keys, :], vb[keys, :]
            for r0, r1, masked in _slab_rows(k0, S):
                rows = slice(r0, r1)
                qb, dobb = qn[rows, :], do_ref[rows, :]
                z = _dot(qb, kb, NT) * SCALE
                lb, ls = _log_sigmoid_pair(z)
                if masked:
                    ls = jnp.where(tri, ls, 0.0)
                passed = passed_s[rows, :] + jnp.sum(ls, axis=1, keepdims=True)
                passed_s[rows, :] = passed
                w = jnp.exp(lb + _cum2(ls, u_gt) + (ct_ref[rows, :] - passed))
                if masked:
                    w = jnp.where(tri, w, 0.0)
                de = w * _dot(dobb, vbb, NT)
                dvn[keys, :] += _dot(w.astype(BF16), dobb, TN)
                e = e_s[rows, :]
                dls = e + _cum2(de, u_lt)
                e_s[rows, :] = e + jnp.sum(de, axis=1, keepdims=True)
                sg = jnp.exp(lb)
                dz = de * (1.0 - sg) - dls * sg
                if masked:
                    dz = jnp.where(tri, dz, 0.0)
                dzb = (dz * SCALE).astype(BF16)
                dqn[rows, :] += _dot(dzb, kb)
                dkn[keys, :] += _dot(dzb, qb, TN)

        dx, gt = _rms_bwd(q, rq, qg_ref[...], dqn[...])
        dqkv_ref[0] = dx.astype(BF16)
        dqg_ref[...] += jnp.sum(gt, axis=0, keepdims=True)
        dx, gt = _rms_bwd(k, rk, kg_ref[...], dkn[...])
        dqkv_ref[1] = dx.astype(BF16)
        dkg_ref[...] += jnp.sum(gt, axis=0, keepdims=True)
        dqkv_ref[2] = dvn[...].astype(BF16)
        pl.when(step == Bl * HEADS - 1)(finish)

    def slot(n):
        return pl.BlockSpec((None, None, S, HD), lambda b, h: (n, b, 0, h))

    head = pl.BlockSpec((None, S, HD), lambda b, h: (b, 0, h))
    gain = pl.BlockSpec((1, HD), lambda b, h: (0, 0))
    x_shape, x_sems = _chip_shapes(exchange)
    return pl.pallas_call(
        body, name="sb_bwd", grid=(Bl, HEADS),
        in_specs=[slot(0), slot(1), slot(2),
                  pl.BlockSpec((None, None, S, 1), lambda b, h: (b, h, 0, 0)), head, gain, gain, ANY] + [ANY] * ne,
        out_specs=[pl.BlockSpec((3, None, S, HD), lambda b, h: (0, b, 0, h)), gain, gain] + [ANY] * ne,
        out_shape=[jax.ShapeDtypeStruct(dproj.shape, dproj.dtype),
                   jax.ShapeDtypeStruct((1, HD), F32), jax.ShapeDtypeStruct((1, HD), F32)] + x_shape,
        scratch_shapes=([pltpu.VMEM((S, HD), BF16)] * 3 + [pltpu.VMEM((S, HD), F32)] * 3
                        + [pltpu.VMEM((S, 1), F32)] * 2 + x_sems),
        input_output_aliases={7: 0},
        compiler_params=_cparams(("arbitrary", "arbitrary"), vmem_mib=56),
    )(proj, proj, proj, ctot, do, q_gain, k_gain, dproj, *exchange)


def _lower_bound(logits):
    l0, l1 = logits[0:1, :], logits[1:2, :]
    m = jnp.maximum(l0, l1)
    e0, e1 = jnp.exp(l0 - m), jnp.exp(l1 - m)
    p0, p1 = e0 / (e0 + e1), e1 / (e0 + e1)
    return (p0 + p1) - p0, p0 * p1


def _hg_gates(qr, fp, lbv):
    sq = _sigmoid(qr)
    sp = _sigmoid(fp)
    sn = 1.0 / (1.0 + jnp.exp(fp))
    f = lbv + (1.0 - lbv) * sp
    return qr * sq, sq, sp, sn, f, (1.0 - lbv) * sn


def _group_consts():
    t = lax.broadcasted_iota(jnp.int32, (GR, GR), 0)
    j = lax.broadcasted_iota(jnp.int32, (GR, GR), 1)
    same = lax.shift_right_logical(t, CH_LOG2) == lax.shift_right_logical(j, CH_LOG2)
    tril = jnp.logical_and(same, j <= t)
    return (tril, tril.astype(BF16), jnp.logical_and(same, j >= t).astype(BF16), same.astype(BF16))


def _hg_decays(qa, k, f, t_inc, t_same):
    g = jnp.log(f)
    gc = _cum2l(t_inc, g)
    gl = _cum2l(t_same, g)
    gm = gc - 0.5 * gl
    e_q = jnp.exp(jnp.minimum(gm, EXP_CLAMP))
    e_k = jnp.exp(jnp.minimum(-gm, EXP_CLAMP))
    e_g = jnp.exp(gc)
    e_l = jnp.exp(gl - gc)
    return qa * e_q, k * e_k, qa * e_g, k * e_l, e_q, e_k, e_g, e_l, jnp.exp(gl)


def _hg_fwd(proj, lb_logits):
    _, Bl, S, _ = proj.shape
    nc = S // CH

    def body(q_ref, f_ref, i_ref, lg_ref, o_ref, st_ref, egl_s):
        lbv, _ = _lower_bound(lg_ref[...])
        tril, t_inc, _, t_same = _group_consts()
        st = jnp.zeros((HD, HD), F32)
        for g0 in range(0, S, GR):
            rs = slice(g0, g0 + GR)
            qa, _, _, _, f, k = _hg_gates(q_ref[rs, :], f_ref[rs, :], lbv)
            qt, kt, qg, kd, _, _, _, _, e_gl = _hg_decays(qa, k, f, t_inc, t_same)
            a = jnp.where(tril, _dot(qt.astype(BF16), kt.astype(BF16), NT), 0.0)
            ib, qgb, kdb = i_ref[rs, :].astype(BF16), qg.astype(BF16), kd.astype(BF16)
            within = _dot(a.astype(BF16), ib)
            egl_s[rs, :] = e_gl
            outs = []
            for l0 in range(0, GR, CH):
                ls = slice(l0, l0 + CH)
                st_ref[(g0 + l0) // CH] = st
                outs.append(within[ls] + _dot(qgb[ls], st.astype(BF16), NT))
                st = st * egl_s[g0 + l0:g0 + l0 + 1, :] + _dot(ib[ls], kdb[ls], TN)
            o_ref[rs, :] = jnp.concatenate(outs, axis=0)

    def slot(n):
        return pl.BlockSpec((None, None, S, HD), lambda b, h: (n, b, 0, h))

    return pl.pallas_call(
        body, name="hg_fwd", grid=(Bl, HEADS),
        in_specs=[slot(0), slot(1), slot(2), pl.BlockSpec((2, HD), lambda b, h: (0, h))],
        out_specs=[pl.BlockSpec((None, S, HD), lambda b, h: (b, 0, h)),
                   pl.BlockSpec((None, None, nc, HD, HD), lambda b, h: (b, h, 0, 0, 0))],
        out_shape=[jax.ShapeDtypeStruct((Bl, S, D), F32),
                   jax.ShapeDtypeStruct((Bl, HEADS, nc, HD, HD), F32)],
        scratch_shapes=[pltpu.VMEM((S, HD), F32)],
        compiler_params=_cparams(("parallel", "parallel")),
    )(proj, proj, proj, lb_logits)


def _hg_bwd(proj, states, do, dproj, lb_logits):
    _, Bl, S, _ = proj.shape

    def body(q_ref, f_ref, i_ref, st_ref, do_ref, lg_ref, _, dqfi_ref, dlb_ref, egl_s):
        lbv, _ = _lower_bound(lg_ref[...])
        tril, t_inc, t_dec, t_same = _group_consts()
        dst = jnp.zeros((HD, HD), F32)
        dlb = jnp.zeros((1, HD), F32)
        for g0 in reversed(range(0, S, GR)):
            rs = slice(g0, g0 + GR)
            qr, fp = q_ref[rs, :], f_ref[rs, :]
            qa, sq, sp, sn, f, k = _hg_gates(qr, fp, lbv)
            qt, kt, qg, kd, e_q, e_k, e_g, e_l, e_gl = _hg_decays(qa, k, f, t_inc, t_same)
            ib, dob, qgb, kdb = i_ref[rs, :].astype(BF16), do_ref[rs, :], qg.astype(BF16), kd.astype(BF16)
            egl_s[rs, :] = e_gl
            dqg, dkd, di, dse = [], [], [], []
            for l0 in reversed(range(0, GR, CH)):
                ls = slice(l0, l0 + CH)
                st = st_ref[(g0 + l0) // CH]
                dstb = dst.astype(BF16)
                dqg.insert(0, _dot(dob[ls], st.astype(BF16)))
                dkd.insert(0, _dot(ib[ls], dstb))
                di.insert(0, _dot(kdb[ls], dstb, NT))
                dse.insert(0, jnp.broadcast_to(jnp.sum(dst * st, axis=0, keepdims=True), (CH, HD)))
                dst = dst * egl_s[g0 + l0:g0 + l0 + 1, :] + _dot(dob[ls], qgb[ls], TN)
            dqg, dkd, di, dse = (jnp.concatenate(p, axis=0) for p in (dqg, dkd, di, dse))
            ab = jnp.where(tril, _dot(qt.astype(BF16), kt.astype(BF16), NT), 0.0).astype(BF16)
            da = jnp.where(tril, _dot(dob, ib, NT), 0.0)
            dqt = _dot3(da, kt)
            dkt = _dot3(da, qt, TN)
            dqfi_ref[2, rs, :] = (di + _dot(ab, dob, TN)).astype(BF16)
            dgc = dqt * qt - dkt * kt + dqg * qg - dkd * kd
            dg = _cum2l(t_dec, dgc) + _cum2l(t_same, dkd * kd) + dse * e_gl
            t1 = dg / f - (dkt * e_k + dkd * e_l)
            dqfi_ref[1, rs, :] = ((1.0 - lbv) * t1 * sp * sn).astype(BF16)
            dqfi_ref[0, rs, :] = ((dqt * e_q + dqg * e_g) * (sq * (1.0 + qr * (1.0 - sq)))).astype(BF16)
            dlb = dlb + jnp.sum(sn * t1, axis=0, keepdims=True)

        @pl.when(pl.program_id(1) == 0)
        def _():
            dlb_ref[...] = dlb

        @pl.when(pl.program_id(1) != 0)
        def _():
            dlb_ref[...] += dlb

    def slot(n):
        return pl.BlockSpec((None, None, S, HD), lambda h, b: (n, b, 0, h))

    return pl.pallas_call(
        body, name="hg_bwd", grid=(HEADS, Bl),
        in_specs=[slot(0), slot(1), slot(2),
                  pl.BlockSpec((None, None, S // CH, HD, HD), lambda h, b: (b, h, 0, 0, 0)),
                  pl.BlockSpec((None, S, HD), lambda h, b: (b, 0, h)),
                  pl.BlockSpec((2, HD), lambda h, b: (0, h)), ANY],
        out_specs=[pl.BlockSpec((3, None, S, HD), lambda h, b: (0, b, 0, h)),
                   pl.BlockSpec((1, HD), lambda h, b: (0, h))],
        out_shape=[jax.ShapeDtypeStruct(dproj.shape, dproj.dtype), jax.ShapeDtypeStruct((1, D), F32)],
        scratch_shapes=[pltpu.VMEM((S, HD), F32)],
        input_output_aliases={6: 0},
        compiler_params=_cparams(("parallel", "arbitrary")),
    )(proj, proj, proj, states, do, lb_logits, dproj)


def _place():
    x, y, c = lax.axis_index("x"), lax.axis_index("y"), lax.axis_index("c")
    return x, y, c, [(1 - x, y), (x, 1 - y), (1 - x, 1 - y)]


def _remote(src, dst, ssem, rsem, dev):
    return pltpu.make_async_remote_copy(src_ref=src, dst_ref=dst, send_sem=ssem, recv_sem=rsem,
                                        device_id=dev, device_id_type=MESH)


GATHER_SEMS = 9


def _gather_ops(wp_ref, wall_ref, ssem, rsem, hn_ref=None, hnall_ref=None):
    half = wp_ref.shape[0] // 2

    def place():
        x, y, c, chips = _place()
        return x, y, c, chips, 2 * x + y, pl.ds(c * half, half), pl.ds((1 - c) * half, half)

    def first_sends():
        x, y, c, chips, b, mine, _ = place()
        cps = [_remote(wp_ref.at[mine], wall_ref.at[b, mine], ssem.at[j], rsem.at[j], (*chip, c))
               for j, chip in enumerate(chips)]
        if hn_ref is not None:
            cps += [_remote(hn_ref, hnall_ref.at[b], ssem.at[6 + j], rsem.at[6 + j], (*chip, c))
                    for j, chip in enumerate(chips)]
        return cps

    def forwards():
        x, y, c, chips, _, mine, _ = place()
        return [_remote(wall_ref.at[2 * cx + cy, mine], wall_ref.at[2 * cx + cy, mine],
                        ssem.at[3 + j], rsem.at[3 + j], (x, y, 1 - c)) for j, (cx, cy) in enumerate(chips)]

    def start():
        for cp in first_sends():
            cp.start()

    def forward():
        x, y, c, chips, _, mine, _ = place()
        for j, (cx, cy) in enumerate(chips):
            landed = wall_ref.at[2 * cx + cy, mine]
            _remote(landed, landed, ssem.at[j], rsem.at[j], (cx, cy, c)).wait_recv()
        for cp in forwards():
            cp.start()

    def finish():
        x, y, c, chips, _, _, other = place()
        for j, (cx, cy) in enumerate(chips):
            passed = wall_ref.at[2 * cx + cy, other]
            _remote(passed, passed, ssem.at[3 + j], rsem.at[3 + j], (x, y, 1 - c)).wait_recv()
            if hn_ref is not None:
                row = hnall_ref.at[2 * cx + cy]
                _remote(row, row, ssem.at[6 + j], rsem.at[6 + j], (cx, cy, c)).wait_recv()
        for cp in first_sends() + forwards():
            cp.wait_send()

    return start, forward, finish


def _gather_weights(wp):
    def body(wp_ref, wall_ref, ssem, rsem):
        for step in _gather_ops(wp_ref, wall_ref, ssem, rsem):
            step()

    return pl.pallas_call(
        body, name="gather_weights", in_specs=[ANY], out_specs=ANY,
        out_shape=jax.ShapeDtypeStruct((NPROJ,) + wp.shape, BF16),
        scratch_shapes=[pltpu.SemaphoreType.DMA((GATHER_SEMS,)), pltpu.SemaphoreType.DMA((GATHER_SEMS,))],
    )(wp)


def _pair_ops(g_refs, r_refs, ssem, rsem):
    def copies():
        x, y, c, _ = _place()
        return [_remote(g.at[n, 1 - c], r.at[n], ssem.at[t * NPROJ + n], rsem.at[t * NPROJ + n], (x, y, 1 - c))
                for t, (g, r) in enumerate(zip(g_refs, r_refs)) for n in range(NPROJ)]

    def start():
        for cp in copies():
            cp.start()

    def finish():
        x, y, c, _ = _place()
        for t, r in enumerate(r_refs):
            for n in range(NPROJ):
                k = t * NPROJ + n
                _remote(r.at[n], r.at[n], ssem.at[k], rsem.at[k], (x, y, 1 - c)).wait_recv()
        for cp in copies():
            cp.wait_send()

    return start, finish


def _pair_shapes(grads):
    return ([jax.ShapeDtypeStruct((NPROJ,) + g.shape[2:], F32) for g in grads],
            [pltpu.SemaphoreType.DMA((len(grads) * NPROJ,))] * 2)


def _pair_exchange(grads, pack):
    ng = len(grads)

    def body(*refs):
        g_refs, pack_ref = refs[:ng], refs[ng]
        r_refs, allp_ref = refs[ng + 1:2 * ng + 1], refs[2 * ng + 1]
        ssem, rsem, psend, precv, lsem = refs[2 * ng + 2:]
        x, y, c, _ = _place()
        me = 4 * x + 2 * y + c
        local = pltpu.make_async_copy(pack_ref, allp_ref.at[me], lsem)
        local.start()
        start, finish = _pair_ops(g_refs, r_refs, ssem, rsem)
        start()
        flips = [(fx, fy, fc) for fx in (0, 1) for fy in (0, 1) for fc in (0, 1)][1:]
        peers = [(fx + x - 2 * fx * x, fy + y - 2 * fy * y, fc + c - 2 * fc * c) for fx, fy, fc in flips]
        sends = [_remote(pack_ref, allp_ref.at[me], psend.at[m], precv.at[m], peer) for m, peer in enumerate(peers)]
        for cp in sends:
            cp.start()
        finish()
        for m, (px, py, pc) in enumerate(peers):
            row = allp_ref.at[4 * px + 2 * py + pc]
            _remote(row, row, psend.at[m], precv.at[m], (px, py, pc)).wait_recv()
        for cp in sends:
            cp.wait_send()
        local.wait()

    out_shape, sems = _pair_shapes(grads)
    return pl.pallas_call(
        body, name="pair_exchange", in_specs=[ANY] * (ng + 1), out_specs=[ANY] * (ng + 1),
        out_shape=out_shape + [jax.ShapeDtypeStruct((8,) + pack.shape, F32)],
        scratch_shapes=sems + [pltpu.SemaphoreType.DMA((7,)), pltpu.SemaphoreType.DMA((7,)),
                               pltpu.SemaphoreType.DMA],
    )(*grads, pack)


def _chip_ops(s_refs, r_refs, ssem, rsem):
    def copies():
        x, y, c, chips = _place()
        return [_remote(s.at[2 * cx + cy], r.at[2 * x + y], ssem.at[3 * t + j], rsem.at[3 * t + j], (cx, cy, c))
                for t, (s, r) in enumerate(zip(s_refs, r_refs)) for j, (cx, cy) in enumerate(chips)]

    def start():
        for cp in copies():
            cp.start()

    def finish():
        x, y, c, chips = _place()
        for t, r in enumerate(r_refs):
            for j, (cx, cy) in enumerate(chips):
                slot = r.at[2 * cx + cy]
                _remote(slot, slot, ssem.at[3 * t + j], rsem.at[3 * t + j], (cx, cy, c)).wait_recv()
        for cp in copies():
            cp.wait_send()

    return start, finish


def _chip_shapes(sums):
    return ([jax.ShapeDtypeStruct(s.shape, s.dtype) for s in sums],
            [pltpu.SemaphoreType.DMA((3 * len(sums),))] * 2)


def _chip_exchange(sums):
    ng = len(sums)

    def body(*refs):
        start, finish = _chip_ops(refs[:ng], refs[ng:2 * ng], *refs[2 * ng:])
        start()
        finish()

    out_shape, sems = _chip_shapes(sums)
    return pl.pallas_call(
        body, name="chip_exchange", in_specs=[ANY] * ng, out_specs=[ANY] * ng,
        out_shape=out_shape, scratch_shapes=sems,
    )(*sums)


def _sibling_share(halves):
    ng = len(halves)

    def body(*refs):
        h_refs, f_refs = refs[:ng], refs[ng:2 * ng]
        ssem, rsem = refs[2 * ng:]
        x, y, c, _ = _place()
        sends = [_remote(h, f, ssem.at[t], rsem.at[t], (x, y, 1 - c))
                 for t, (h, f) in enumerate(zip(h_refs, f_refs))]
        for cp in sends:
            cp.start()
        for t, f in enumerate(f_refs):
            _remote(f, f, ssem.at[t], rsem.at[t], (x, y, 1 - c)).wait_recv()
        for cp in sends:
            cp.wait_send()

    return pl.pallas_call(
        body, name="sibling_share", in_specs=[ANY] * ng, out_specs=[ANY] * ng,
        out_shape=[jax.ShapeDtypeStruct(h.shape, F32) for h in halves],
        scratch_shapes=[pltpu.SemaphoreType.DMA((ng,)), pltpu.SemaphoreType.DMA((ng,))],
    )(*halves)


def _pair_add(own, recv, cidx, name):
    R = own.shape[2]
    tr = min(256, R)

    def body(c_ref, a_ref, b_ref, o_ref):
        o_ref[...] = (a_ref[...] + b_ref[...]).astype(BF16)

    return pl.pallas_call(
        body, name=name,
        grid_spec=pltpu.PrefetchScalarGridSpec(
            num_scalar_prefetch=1, grid=(NPROJ, R // tr),
            in_specs=[pl.BlockSpec((None, None, tr, D), lambda n, r, c: (n, c[0], r, 0)),
                      pl.BlockSpec((None, tr, D), lambda n, r, c: (n, r, 0))],
            out_specs=pl.BlockSpec((None, tr, D), lambda n, r, c: (n, r, 0))),
        out_shape=jax.ShapeDtypeStruct(recv.shape, BF16),
        compiler_params=_cparams(("parallel", "parallel")),
    )(cidx, own, recv)


def _chip_sum(sums, parts, bidx, name):
    R = parts.shape[1]
    tr = min(256, R)

    def body(b_ref, s_ref, p_ref, o_ref):
        acc = None
        for j in range(NPROJ):
            term = jnp.where(b_ref[0] == j, s_ref[...], p_ref[j]).astype(F32)
            acc = term if acc is None else acc + term
        o_ref[...] = acc

    return pl.pallas_call(
        body, name=name,
        grid_spec=pltpu.PrefetchScalarGridSpec(
            num_scalar_prefetch=1, grid=(R // tr,),
            in_specs=[pl.BlockSpec((None, tr, D), lambda r, b: (b[0], r, 0)),
                      pl.BlockSpec((NPROJ, tr, D), lambda r, b: (0, r, 0))],
            out_specs=pl.BlockSpec((tr, D), lambda r, b: (r, 0))),
        out_shape=jax.ShapeDtypeStruct((R, D), F32),
        compiler_params=_cparams(("parallel",)),
    )(bidx, sums, parts)


def _adamw_math(w, g, m, v):
    m = ADAM_B1 * m + (1.0 - ADAM_B1) * g
    v = ADAM_B2 * v + (1.0 - ADAM_B2) * (g * g)
    m_hat = m / (1.0 - ADAM_B1 ** ADAM_STEP)
    v_hat = v / (1.0 - ADAM_B2 ** ADAM_STEP)
    delta = -ADAM_LR * (m_hat / (jnp.sqrt(v_hat) + ADAM_EPS) + ADAM_WD * w)
    return delta, m, v


def _adamw(w, mine, theirs, m, v, cidx, name):
    R = mine.shape[0]
    tr = min(256, R)
    nr = R // tr

    def body(c_ref, w_ref, a_ref, b_ref, m_ref, v_ref, g_ref, d_ref, nm_ref, nv_ref):
        g = jnp.where(pl.program_id(0) == c_ref[0], a_ref[...], b_ref[...])
        g_ref[...] = g
        d_ref[...], nm_ref[...], nv_ref[...] = _adamw_math(w_ref[...], g, m_ref[...], v_ref[...])

    full = pl.BlockSpec((tr, D), lambda h, r, c: (h * nr + r, 0))
    half = pl.BlockSpec((tr, D), lambda h, r, c: (r, 0))
    return pl.pallas_call(
        body, name=name,
        grid_spec=pltpu.PrefetchScalarGridSpec(
            num_scalar_prefetch=1, grid=(2, nr),
            in_specs=[full, half, half, full, full], out_specs=[full] * 4),
        out_shape=[jax.ShapeDtypeStruct(w.shape, F32)] * 4,
        compiler_params=_cparams(("parallel", "parallel")),
    )(cidx, w, mine, theirs, m, v)


PACK_ROWS = 8


def _small_update(allp, bidx, logits, weights, moments_m, moments_v):
    shapes = [w.shape for w in weights]
    q4 = D // NPROJ

    def body(b_ref, allp_ref, hgp_ref, lg_ref, *refs):
        w_refs, m_refs, v_refs = refs[0:6], refs[6:12], refs[12:18]
        loss_ref = refs[18]
        g_out, d_out, m_out, v_out = refs[19:25], refs[25:31], refs[31:37], refs[37:43]

        def total(ref, row, lo, hi):
            acc = ref[0, row:row + 1, lo:hi]
            for dev in range(1, 8):
                acc = acc + ref[dev, row:row + 1, lo:hi]
            return acc

        _, pp = _lower_bound(lg_ref[...])
        dlb = total(allp_ref, 2, 0, D)
        grads = [total(allp_ref, 0, 0, D), total(allp_ref, 4, 0, HD), total(allp_ref, 4, HD, 2 * HD),
                 total(hgp_ref, 1, 0, q4), total(allp_ref, 4, 2 * HD, 3 * HD), None]
        loss_ref[...] = (0.5 / D) * jnp.sum(total(allp_ref, 3, 0, D), axis=1, keepdims=True)
        for t in range(6):
            if t < 5:
                rows = [(slice(None), grads[t])]
            else:
                rows = [(slice(0, 1), -pp * dlb), (slice(1, 2), pp * dlb)]
            for rs, g in rows:
                g_out[t][rs, :] = g
                d_out[t][rs, :], m_out[t][rs, :], v_out[t][rs, :] = _adamw_math(
                    w_refs[t][rs, :], g, m_refs[t][rs, :], v_refs[t][rs, :])

    whole = [pl.BlockSpec(s, lambda i, b: (0, 0)) for s in shapes]
    return pl.pallas_call(
        body, name="small_update",
        grid_spec=pltpu.PrefetchScalarGridSpec(
            num_scalar_prefetch=1, grid=(1,),
            in_specs=[pl.BlockSpec((8, PACK_ROWS, D), lambda i, b: (0, 0, 0)),
                      pl.BlockSpec((8, PACK_ROWS, q4), lambda i, b: (0, 0, b[0])),
                      pl.BlockSpec((2, D), lambda i, b: (0, 0))] + whole * 3,
            out_specs=[pl.BlockSpec((1, 1), lambda i, b: (0, 0))] + whole * 4),
        out_shape=[jax.ShapeDtypeStruct((1, 1), F32)] + [jax.ShapeDtypeStruct(s, F32) for s in shapes] * 4,
        compiler_params=_cparams(("arbitrary",)),
    )(bidx, allp, allp, logits, *weights, *moments_m, *moments_v)


def kernel(x, sb_norm, sb_w_in, sb_q_gain, sb_k_gain, sb_w_out, hg_norm, hg_w_in, hg_o_gain, hg_w_out, hg_lb_logits, loss_target, m_sb_norm, m_sb_w_in, m_sb_q_gain, m_sb_k_gain, m_sb_w_out, m_hg_norm, m_hg_w_in, m_hg_o_gain, m_hg_w_out, m_hg_lb_logits, v_sb_norm, v_sb_w_in, v_sb_q_gain, v_sb_k_gain, v_sb_w_out, v_hg_norm, v_hg_w_in, v_hg_o_gain, v_hg_w_out, v_hg_lb_logits):
    Bl, S, _ = x.shape
    T = Bl * S
    cidx = lax.axis_index("c").astype(jnp.int32).reshape(1)
    bidx = (2 * lax.axis_index("x") + lax.axis_index("y")).astype(jnp.int32).reshape(1)

    def in_hbm(arrays):
        return [pltpu.with_memory_space_constraint(a, pltpu.HBM) for a in arrays]

    def own_slot(gathered, mine):
        return lax.dynamic_update_slice(gathered, mine[None], (bidx[0],) + (0,) * mine.ndim)

    def halved(g):
        return g.reshape(NPROJ, 2, g.shape[-2] * g.shape[0] // (2 * NPROJ), D)

    def heads(a):
        return a.reshape(a.shape[:-2] + (Bl, S, D))

    def flat(a):
        return a.reshape(a.shape[:-3] + (T, D))

    wp_sb = jnp.concatenate([sb_w_in[0], sb_w_out[0]], axis=0).astype(BF16)
    wp_hg = jnp.concatenate([hg_w_in[0], hg_w_out[0]], axis=0).astype(BF16)
    wall_sb, = in_hbm([own_slot(_gather_weights(wp_sb), wp_sb)])
    x2 = x.reshape(T, D)
    tgt = loss_target.reshape(T, D)

    proj0 = _in_proj_fwd(x2, sb_norm, wall_sb, W_IN, "sb_in_fwd")
    o0, ctot, wall_hg, hnall = _sb_fwd(heads(proj0), sb_q_gain, sb_k_gain, wp_hg, hg_norm)
    wall_hg, = in_hbm([own_slot(wall_hg, wp_hg)])
    hgn = own_slot(hnall, hg_norm).reshape(1, D)
    h1 = _out_proj_fwd(flat(o0), proj0, x2, wall_sb, W_OUT, "sb_out_fwd")
    proj1 = _in_proj_fwd(h1, hgn, wall_hg, W_IN, "hg_in_fwd")
    o1, states = _hg_fwd(heads(proj1), hg_lb_logits)
    dh2, loss_terms = _out_proj_fwd(flat(o1), proj1, h1, wall_hg, W_OUT, "hg_out_fwd",
                                    o_gain=hg_o_gain, target=tgt)

    do1, dproj1, gout_hg, d_ogain = _out_proj_bwd(dh2, flat(o1), proj1, wall_hg, W_OUT, "hg_out_bwd",
                                                  o_gain=hg_o_gain)
    dproj1, dlb = _hg_bwd(heads(proj1), states, heads(do1), heads(dproj1), hg_lb_logits)
    dproj1 = flat(dproj1)
    dh1, d_hgn = _in_proj_bwd_x(dproj1, wall_hg, W_IN, h1, hgn, dh2, "hg_in_bwd_x")
    big_hg = [halved(_in_proj_bwd_w(dproj1, h1, hgn, "hg_in_bwd_w")), halved(gout_hg)]
    do0, dproj0, gout_sb, *recv_hg = _out_proj_bwd(dh1, flat(o0), proj0, wall_sb, W_OUT, "sb_out_bwd",
                                                   exchange=big_hg)
    sums_hg = in_hbm(_pair_add(g, r, cidx, "pair_add_" + nm)
                     for g, r, nm in zip(big_hg, in_hbm(recv_hg), ("hg_in", "hg_out")))
    dproj0, d_qg, d_kg, *parts_hg = _sb_bwd(heads(proj0), ctot, heads(do0), heads(dproj0),
                                            sb_q_gain, sb_k_gain, sums_hg)
    dproj0 = flat(dproj0)
    grad_x, d_sbn = _in_proj_bwd_x(dproj0, wall_sb, W_IN, x2, sb_norm, dh1, "sb_in_bwd_x")
    big_sb = [halved(_in_proj_bwd_w(dproj0, x2, sb_norm, "sb_in_bwd_w")), halved(gout_sb)]

    gains = jnp.concatenate([d_qg, d_kg, d_ogain, jnp.zeros((1, D - 3 * HD), F32)], axis=1)
    pack = jnp.concatenate([d_sbn, d_hgn, dlb, loss_terms, gains, jnp.zeros((3, D), F32)], axis=0)
    *recv_sb, allp = _pair_exchange(big_sb, pack)
    sums_sb = in_hbm(_pair_add(g, r, cidx, "pair_add_" + nm)
                     for g, r, nm in zip(big_sb, in_hbm(recv_sb), ("sb_in", "sb_out")))
    parts_sb = _chip_exchange(sums_sb)
    names = ["sb_in", "hg_in", "sb_out", "hg_out"]
    sums = [sums_sb[0], sums_hg[0], sums_sb[1], sums_hg[1]]
    parts = [parts_sb[0], parts_hg[0], parts_sb[1], parts_hg[1]]
    halves = in_hbm(_chip_sum(sm, p, bidx, "chip_sum_" + nm) for sm, p, nm in zip(sums, in_hbm(parts), names))
    theirs = in_hbm(_sibling_share(halves))

    big_w = [sb_w_in, hg_w_in, sb_w_out, hg_w_out]
    big_m = [m_sb_w_in, m_hg_w_in, m_sb_w_out, m_hg_w_out]
    big_v = [v_sb_w_in, v_hg_w_in, v_sb_w_out, v_hg_w_out]
    upd = [_adamw(w[0], a, b, m[0], v[0], cidx, "adamw_" + nm)
           for w, a, b, m, v, nm in zip(big_w, halves, theirs, big_m, big_v, names)]
    (g_sb_in, d_sb_in, nm_sb_in, nv_sb_in), (g_hg_in, d_hg_in, nm_hg_in, nv_hg_in), \
        (g_sb_out, d_sb_out, nm_sb_out, nv_sb_out), (g_hg_out, d_hg_out, nm_hg_out, nv_hg_out) = [
            tuple(a[None] for a in u) for u in upd]

    small = _small_update(
        allp, bidx, hg_lb_logits,
        [sb_norm, sb_q_gain, sb_k_gain, hg_norm, hg_o_gain, hg_lb_logits],
        [m_sb_norm, m_sb_q_gain, m_sb_k_gain, m_hg_norm, m_hg_o_gain, m_hg_lb_logits],
        [v_sb_norm, v_sb_q_gain, v_sb_k_gain, v_hg_norm, v_hg_o_gain, v_hg_lb_logits])
    loss = small[0].reshape(())
    (g_sbn, g_qg, g_kg, g_hgn, g_og, g_lb) = small[1:7]
    (d_sbn2, d_qg2, d_kg2, d_hgn2, d_og2, d_lb2) = small[7:13]
    (nm_sbn, nm_qg, nm_kg, nm_hgn, nm_og, nm_lb) = small[13:19]
    (nv_sbn, nv_qg, nv_kg, nv_hgn, nv_og, nv_lb) = small[19:25]

    return (loss, grad_x.reshape(Bl, S, D),
            g_sbn, g_sb_in, g_qg, g_kg, g_sb_out, g_hgn, g_hg_in, g_og, g_hg_out, g_lb,
            d_sbn2, d_sb_in, d_qg2, d_kg2, d_sb_out, d_hgn2, d_hg_in, d_og2, d_hg_out, d_lb2,
            nm_sbn, nm_sb_in, nm_qg, nm_kg, nm_sb_out, nm_hgn, nm_hg_in, nm_og, nm_hg_out, nm_lb,
            nv_sbn, nv_sb_in, nv_qg, nv_kg, nv_sb_out, nv_hgn, nv_hg_in, nv_og, nv_hg_out, nv_lb)
```

```python
import functools

import jax
import jax.numpy as jnp
from jax import lax
from jax.experimental import pallas as pl
from jax.experimental.pallas import tpu as pltpu

F32 = jnp.float32
BF16 = jnp.bfloat16
MESH = pl.DeviceIdType.MESH
ANY = pl.BlockSpec(memory_space=pl.ANY)

D = 1024
HEADS = 8
HD = 128
NPROJ = 4
RMS_EPS = 1e-6
TK = 256
CH = 64
CH_LOG2 = 6
GR = 256
SCALE = HD ** -0.5
EXP_CLAMP = 60.0
W_IN, W_OUT = 0, 4

ADAM_LR = 0.001
ADAM_B1 = 0.9
ADAM_B2 = 0.999
ADAM_EPS = 1e-08
ADAM_WD = 0.01
ADAM_STEP = 10

NT = (((1,), (1,)), ((), ()))
TN = (((0,), (0,)), ((), ()))
MIB = 1024 * 1024


def _cparams(sem=None, vmem_mib=40):
    return pltpu.CompilerParams(dimension_semantics=sem, vmem_limit_bytes=vmem_mib * MIB)


def _dot(a, b, dims=None):
    if dims is None:
        return jnp.dot(a, b, preferred_element_type=F32)
    return lax.dot_general(a, b, dims, preferred_element_type=F32)


def _sigmoid(x):
    return 1.0 / (1.0 + jnp.exp(-x))


def _rms(x):
    return lax.rsqrt(jnp.mean(x * x, axis=-1, keepdims=True) + RMS_EPS)


def _rms_bwd(x, r, gain, dy):
    a = dy * gain
    dx = r * a - x * (r * r * r) * jnp.mean(x * a, axis=-1, keepdims=True)
    return dx, dy * (x * r)


def _split2(v):
    hi = v.astype(BF16)
    lo = (v - hi.astype(F32)).astype(BF16)
    return hi, lo


def _cum2(v, u):
    hi, lo = _split2(v)
    return _dot(hi, u) + _dot(lo, u)


def _dot3(a, b, dims=None):
    ah, al = _split2(a)
    bh, bl = _split2(b)
    return _dot(ah, bh, dims) + _dot(ah, bl, dims) + _dot(al, bh, dims)


def _cum2l(u, v):
    hi, lo = _split2(v)
    return _dot(u, hi) + _dot(u, lo)


def _in_proj_fwd(h, gain, wall, wblk, name):
    T = h.shape[0]
    tm = min(1024, T)

    def body(h_ref, g_ref, w_ref, o_ref, u_s):
        rows = pl.ds(pl.multiple_of(pl.program_id(1) * tm, tm), tm)

        @pl.when(pl.program_id(0) == 0)
        def _():
            x = h_ref[...]
            u_s[rows, :] = (x * _rms(x) * g_ref[...]).astype(BF16)

        o_ref[...] = _dot(u_s[rows, :], w_ref[...])

    return pl.pallas_call(
        body, name=name, grid=(NPROJ, T // tm),
        in_specs=[pl.BlockSpec((tm, D), lambda n, i: (jnp.where(n == 0, i, 0), 0)),
                  pl.BlockSpec((1, D), lambda n, i: (0, 0)),
                  pl.BlockSpec((None, D, D), lambda n, i: (n, wblk, 0))],
        out_specs=pl.BlockSpec((None, tm, D), lambda n, i: (n, i, 0)),
        out_shape=jax.ShapeDtypeStruct((NPROJ, T, D), F32),
        scratch_shapes=[pltpu.VMEM((T, D), BF16)],
        compiler_params=_cparams(("arbitrary", "arbitrary")),
    )(h, gain, wall)


def _head_norm(x):
    outs = []
    for hh in range(x.shape[1] // HD):
        xs = x[:, hh * HD:(hh + 1) * HD]
        outs.append((xs, _rms(xs)))
    return outs


def _w_out_specs(wblk):
    kb = D // NPROJ
    return [pl.BlockSpec((None, kb, D), functools.partial(lambda j, i: (j, wblk, 0), j)) for j in range(NPROJ)]


def _out_proj_fwd(o, proj, resid, wall, wblk, name, o_gain=None, target=None):
    T = o.shape[0]
    tm = min(512, T)
    kb = D // NPROJ
    with_loss = target is not None

    def body(*refs):
        o_ref, g_ref, r_ref = refs[:3]
        w_refs = refs[3:3 + NPROJ]
        if with_loss:
            og_ref, t_ref, dh_ref, ls_ref = refs[3 + NPROJ:]
        else:
            h_ref, = refs[3 + NPROJ:]
        x = o_ref[...]
        if with_loss:
            x = jnp.concatenate([xs * r * og_ref[...] for xs, r in _head_norm(x)], axis=1)
        g = g_ref[...]
        a = (x * (g * _sigmoid(g))).astype(BF16)
        hnew = r_ref[...]
        for j in range(NPROJ):
            hnew = hnew + _dot(a[:, j * kb:(j + 1) * kb], w_refs[j][...])
        if with_loss:
            err = hnew - t_ref[...]
            dh_ref[...] = err * (1.0 / D)
            part = jnp.sum(err * err, axis=0, keepdims=True)

            @pl.when(pl.program_id(0) == 0)
            def _():
                ls_ref[...] = part

            @pl.when(pl.program_id(0) != 0)
            def _():
                ls_ref[...] += part
        else:
            h_ref[...] = hnew

    tile = pl.BlockSpec((tm, D), lambda i: (i, 0))
    in_specs = [tile, pl.BlockSpec((None, tm, D), lambda i: (3, i, 0)), tile] + _w_out_specs(wblk)
    args = [o, proj, resid] + [wall] * NPROJ
    out_specs = tile
    out_shape = jax.ShapeDtypeStruct((T, D), F32)
    if with_loss:
        in_specs += [pl.BlockSpec((1, HD), lambda i: (0, 0)), tile]
        args += [o_gain, target]
        out_specs = [tile, pl.BlockSpec((1, D), lambda i: (0, 0))]
        out_shape = [out_shape, jax.ShapeDtypeStruct((1, D), F32)]
    return pl.pallas_call(
        body, name=name, grid=(T // tm,), in_specs=in_specs, out_specs=out_specs,
        out_shape=out_shape, compiler_params=_cparams(("arbitrary",)),
    )(*args)


def _out_proj_bwd(dy, o, proj, wall, wblk, name, o_gain=None, exchange=()):
    T = o.shape[0]
    tm = min(512, T)
    kb = D // NPROJ
    normed = o_gain is not None
    ne = len(exchange)

    def body(*refs):
        it = iter(refs)
        dy_ref, o_ref, g_ref = (next(it) for _ in range(3))
        w_refs = [next(it) for _ in range(NPROJ)]
        og_ref = next(it) if normed else None
        xg_refs = [next(it) for _ in range(ne)]
        do_ref, dg_ref, dw_ref = (next(it) for _ in range(3))
        dgain_ref = next(it) if normed else None
        xr_refs = [next(it) for _ in range(ne)]
        wt_s = next(it)
        first = pl.program_id(0) == 0
        if ne:
            start, finish = _pair_ops(xg_refs, xr_refs, next(it), next(it))
            pl.when(first)(start)

        @pl.when(first)
        def _():
            for j in range(NPROJ):
                wt_s[:, j * kb:(j + 1) * kb] = w_refs[j][...].T
        g = g_ref[...]
        s = _sigmoid(g)
        sl = g * s
        x = o_ref[...]
        if normed:
            heads = _head_norm(x)
            on = jnp.concatenate([xs * r * og_ref[...] for xs, r in heads], axis=1)
        else:
            on = x
        dyb = dy_ref[...].astype(BF16)
        a = (on * sl).astype(BF16)
        for j in range(NPROJ):
            part = _dot(a[:, j * kb:(j + 1) * kb], dyb, TN)

            @pl.when(first)
            def _():
                dw_ref[j] = part

            @pl.when(jnp.logical_not(first))
            def _():
                dw_ref[j] += part

        da = _dot(dyb, wt_s[...])
        d_on = da * sl
        dg_ref[...] = (da * on * (s * (1.0 + g * (1.0 - s)))).astype(BF16)
        if normed:
            dxs, gsum = [], None
            for hh, (xs, r) in enumerate(heads):
                dx, gt = _rms_bwd(xs, r, og_ref[...], d_on[:, hh * HD:(hh + 1) * HD])
                dxs.append(dx)
                gt = jnp.sum(gt, axis=0, keepdims=True)
                gsum = gt if gsum is None else gsum + gt
            do_ref[...] = jnp.concatenate(dxs, axis=1).astype(BF16)

            @pl.when(first)
            def _():
                dgain_ref[...] = gsum

            @pl.when(jnp.logical_not(first))
            def _():
                dgain_ref[...] += gsum
        else:
            do_ref[...] = d_on.astype(BF16)
        if ne:
            pl.when(pl.program_id(0) == T // tm - 1)(finish)

    tile = pl.BlockSpec((tm, D), lambda i: (i, 0))
    gate = pl.BlockSpec((None, tm, D), lambda i: (3, i, 0))
    in_specs = [tile, tile, gate] + _w_out_specs(wblk)
    args = [dy, o, proj] + [wall] * NPROJ
    out_specs = [tile, gate, pl.BlockSpec((NPROJ, kb, D), lambda i: (0, 0, 0))]
    out_shape = [jax.ShapeDtypeStruct((T, D), BF16),
                 jax.ShapeDtypeStruct((NPROJ, T, D), BF16),
                 jax.ShapeDtypeStruct((NPROJ, kb, D), F32)]
    if normed:
        in_specs.append(pl.BlockSpec((1, HD), lambda i: (0, 0)))
        args.append(o_gain)
        out_specs.append(pl.BlockSpec((1, HD), lambda i: (0, 0)))
        out_shape.append(jax.ShapeDtypeStruct((1, HD), F32))
    x_shape, x_sems = _pair_shapes(exchange) if ne else ([], [])
    return pl.pallas_call(
        body, name=name, grid=(T // tm,), in_specs=in_specs + [ANY] * ne, out_specs=out_specs + [ANY] * ne,
        out_shape=out_shape + x_shape, scratch_shapes=[pltpu.VMEM((D, D), BF16)] + x_sems,
        compiler_params=_cparams(("arbitrary",), vmem_mib=48),
    )(*args, *exchange)


def _in_proj_bwd_x(dproj, wall, wblk, h, gain, dres, name, exchange=()):
    T = h.shape[0]
    tm = min(512, T)
    ne = len(exchange)

    def body(d_ref, w_ref, h_ref, g_ref, r_ref, *refs):
        xs_refs, (dh_ref, dgain_ref), xr_refs = refs[:ne], refs[ne:ne + 2], refs[ne + 2:2 * ne + 2]
        du, wt_s = refs[2 * ne + 2:2 * ne + 4]
        i, n = pl.program_id(0), pl.program_id(1)
        if ne:
            start, finish = _chip_ops(xs_refs, xr_refs, *refs[2 * ne + 4:])
            pl.when(jnp.logical_and(i == 0, n == 0))(start)

        @pl.when(i == 0)
        def _():
            wt_s[n] = w_ref[...].T

        part = _dot(d_ref[...], wt_s[n])

        @pl.when(n == 0)
        def _():
            du[...] = part

        @pl.when(n != 0)
        def _():
            du[...] += part

        @pl.when(n == NPROJ - 1)
        def _():
            x = h_ref[...]
            dx, gt = _rms_bwd(x, _rms(x), g_ref[...], du[...])
            dh_ref[...] = r_ref[...] + dx
            gt = jnp.sum(gt, axis=0, keepdims=True)

            @pl.when(i == 0)
            def _():
                dgain_ref[...] = gt

            @pl.when(i != 0)
            def _():
                dgain_ref[...] += gt

        if ne:
            pl.when(jnp.logical_and(i == T // tm - 1, n == NPROJ - 1))(finish)

    x_shape, x_sems = _chip_shapes(exchange) if ne else ([], [])
    return pl.pallas_call(
        body, name=name, grid=(T // tm, NPROJ),
        in_specs=[pl.BlockSpec((None, tm, D), lambda i, n: (n, i, 0)),
                  pl.BlockSpec((None, D, D), lambda i, n: (jnp.where(i == 0, n, NPROJ - 1), wblk, 0)),
                  pl.BlockSpec((tm, D), lambda i, n: (i, 0)),
                  pl.BlockSpec((1, D), lambda i, n: (0, 0)),
                  pl.BlockSpec((tm, D), lambda i, n: (i, 0))] + [ANY] * ne,
        out_specs=[pl.BlockSpec((tm, D), lambda i, n: (i, 0)),
                   pl.BlockSpec((1, D), lambda i, n: (0, 0))] + [ANY] * ne,
        out_shape=[jax.ShapeDtypeStruct((T, D), F32), jax.ShapeDtypeStruct((1, D), F32)] + x_shape,
        scratch_shapes=[pltpu.VMEM((tm, D), F32), pltpu.VMEM((NPROJ, D, D), BF16)] + x_sems,
        compiler_params=_cparams(("arbitrary", "arbitrary")),
    )(dproj, wall, h, gain, dres, *exchange)


def _in_proj_bwd_w(dproj, h, gain, name):
    T = h.shape[0]
    tk = min(1024, T)

    def body(d_ref, h_ref, g_ref, dw_ref, ut_s):
        k = pl.program_id(1)

        @pl.when(pl.program_id(0) == 0)
        def _():
            x = h_ref[...]
            ut_s[k] = (x * _rms(x) * g_ref[...]).astype(BF16).T

        part = _dot(ut_s[k], d_ref[...])

        @pl.when(k == 0)
        def _():
            dw_ref[...] = part

        @pl.when(k != 0)
        def _():
            dw_ref[...] += part

    return pl.pallas_call(
        body, name=name, grid=(NPROJ, T // tk),
        in_specs=[pl.BlockSpec((None, tk, D), lambda n, k: (n, k, 0)),
                  pl.BlockSpec((tk, D), lambda n, k: (jnp.where(n == 0, k, 0), 0)),
                  pl.BlockSpec((1, D), lambda n, k: (0, 0))],
        out_specs=pl.BlockSpec((None, D, D), lambda n, k: (n, 0, 0)),
        out_shape=jax.ShapeDtypeStruct((NPROJ, D, D), F32),
        scratch_shapes=[pltpu.VMEM((T // tk, D, tk), BF16)],
        compiler_params=_cparams(("arbitrary", "arbitrary")),
    )(dproj, h, gain)


def _log_sigmoid_pair(z):
    lb = jnp.minimum(z, 0.0) - jnp.log(1.0 + jnp.exp(-jnp.abs(z)))
    return lb, lb - z


def _slab_consts():
    t = lax.broadcasted_iota(jnp.int32, (TK, TK), 0)
    s = lax.broadcasted_iota(jnp.int32, (TK, TK), 1)
    return s < t, (t > s).astype(BF16), (t < s).astype(BF16)


def _slab_rows(k0, S):
    return [(r0, r1, masked) for r0, r1, masked in ((k0, k0 + TK, True), (k0 + TK, S, False)) if r0 < r1]


def _sb_fwd(proj, q_gain, k_gain, wp, hn):
    _, Bl, S, _ = proj.shape
    steps = Bl * HEADS

    def body(q_ref, k_ref, v_ref, qg_ref, kg_ref, wp_ref, hn_ref, o_ref, ct_ref, wall_ref, hnall_ref,
             qn, kn, vb, ssem, rsem):
        step = pl.program_id(0) * HEADS + pl.program_id(1)
        start, forward, finish = _gather_ops(wp_ref, wall_ref, ssem, rsem, hn_ref, hnall_ref)
        pl.when(step == 0)(start)
        pl.when(step == steps // 2)(forward)
        q = q_ref[...]
        qn[...] = (q * _rms(q) * qg_ref[...]).astype(BF16)
        k = k_ref[...]
        kn[...] = (k * _rms(k) * kg_ref[...]).astype(BF16)
        vb[...] = v_ref[...].astype(BF16)
        o_ref[...] = jnp.zeros_like(o_ref)
        ct_ref[...] = jnp.zeros_like(ct_ref)
        tri, u_gt, _ = _slab_consts()
        for k0 in reversed(range(0, S, TK)):
            kb, vbb = kn[k0:k0 + TK, :], vb[k0:k0 + TK, :]
            for r0, r1, masked in _slab_rows(k0, S):
                z = _dot(qn[r0:r1, :], kb, NT) * SCALE
                lb, ls = _log_sigmoid_pair(z)
                if masked:
                    ls = jnp.where(tri, ls, 0.0)
                c = ct_ref[r0:r1, :]
                w = jnp.exp(lb + _cum2(ls, u_gt) + c)
                if masked:
                    w = jnp.where(tri, w, 0.0)
                o_ref[r0:r1, :] += _dot(w.astype(BF16), vbb)
                ct_ref[r0:r1, :] = c + jnp.sum(ls, axis=1, keepdims=True)
        pl.when(step == steps - 1)(finish)

    def slot(n):
        return pl.BlockSpec((None, None, S, HD), lambda b, h: (n, b, 0, h))

    return pl.pallas_call(
        body, name="sb_fwd", grid=(Bl, HEADS),
        in_specs=[slot(0), slot(1), slot(2),
                  pl.BlockSpec((1, HD), lambda b, h: (0, 0)),
                  pl.BlockSpec((1, HD), lambda b, h: (0, 0)), ANY, ANY],
        out_specs=[pl.BlockSpec((None, S, HD), lambda b, h: (b, 0, h)),
                   pl.BlockSpec((None, None, S, 1), lambda b, h: (b, h, 0, 0)), ANY, ANY],
        out_shape=[jax.ShapeDtypeStruct((Bl, S, D), F32),
                   jax.ShapeDtypeStruct((Bl, HEADS, S, 1), F32),
                   jax.ShapeDtypeStruct((NPROJ,) + wp.shape, BF16),
                   jax.ShapeDtypeStruct((NPROJ,) + hn.shape, F32)],
        scratch_shapes=[pltpu.VMEM((S, HD), BF16)] * 3 + [pltpu.SemaphoreType.DMA((GATHER_SEMS,))] * 2,
        compiler_params=_cparams(("arbitrary", "arbitrary"), vmem_mib=56),
    )(proj, proj, proj, q_gain, k_gain, wp, hn)


def _sb_bwd(proj, ctot, do, dproj, q_gain, k_gain, exchange):
    _, Bl, S, _ = proj.shape
    ne = len(exchange)

    def body(q_ref, k_ref, v_ref, ct_ref, do_ref, qg_ref, kg_ref, _, *refs):
        xs_refs, (dqkv_ref, dqg_ref, dkg_ref), xr_refs = refs[:ne], refs[ne:ne + 3], refs[ne + 3:2 * ne + 3]
        qn, kn, vb, dqn, dkn, dvn, passed_s, e_s, ssem, rsem = refs[2 * ne + 3:]
        step = pl.program_id(0) * HEADS + pl.program_id(1)
        first = step == 0
        start, finish = _chip_ops(xs_refs, xr_refs, ssem, rsem)

        @pl.when(first)
        def _():
            start()
            dqg_ref[...] = jnp.zeros_like(dqg_ref)
            dkg_ref[...] = jnp.zeros_like(dkg_ref)

        q = q_ref[...]
        rq = _rms(q)
        qn[...] = (q * rq * qg_ref[...]).astype(BF16)
        k = k_ref[...]
        rk = _rms(k)
        kn[...] = (k * rk * kg_ref[...]).astype(BF16)
        vb[...] = v_ref[...].astype(BF16)
        for acc in (dqn, dkn, dvn, passed_s, e_s):
            acc[...] = jnp.zeros_like(acc)
        tri, u_gt, u_lt = _slab_consts()
        for k0 in range(0, S, TK):
            keys = slice(k0, k0 + TK)
            kb, vbb = kn[keys, :], vb[keys, :]
            for r0, r1, masked in _slab_rows(k0, S):
                rows = slice(r0, r1)
                qb, dobb = qn[rows, :], do_ref[rows, :]
                z = _dot(qb, kb, NT) * SCALE
                lb, ls = _log_sigmoid_pair(z)
                if masked:
                    ls = jnp.where(tri, ls, 0.0)
                passed = passed_s[rows, :] + jnp.sum(ls, axis=1, keepdims=True)
                passed_s[rows, :] = passed
                w = jnp.exp(lb + _cum2(ls, u_gt) + (ct_ref[rows, :] - passed))
                if masked:
                    w = jnp.where(tri, w, 0.0)
                de = w * _dot(dobb, vbb, NT)
                dvn[keys, :] += _dot(w.astype(BF16), dobb, TN)
                e = e_s[rows, :]
                dls = e + _cum2(de, u_lt)
                e_s[rows, :] = e + jnp.sum(de, axis=1, keepdims=True)
                sg = jnp.exp(lb)
                dz = de * (1.0 - sg) - dls * sg
                if masked:
                    dz = jnp.where(tri, dz, 0.0)
                dzb = (dz * SCALE).astype(BF16)
                dqn[rows, :] += _dot(dzb, kb)
                dkn[keys, :] += _dot(dzb, qb, TN)

        dx, gt = _rms_bwd(q, rq, qg_ref[...], dqn[...])
        dqkv_ref[0] = dx.astype(BF16)
        dqg_ref[...] += jnp.sum(gt, axis=0, keepdims=True)
        dx, gt = _rms_bwd(k, rk, kg_ref[...], dkn[...])
        dqkv_ref[1] = dx.astype(BF16)
        dkg_ref[...] += jnp.sum(gt, axis=0, keepdims=True)
        dqkv_ref[2] = dvn[...].astype(BF16)
        pl.when(step == Bl * HEADS - 1)(finish)

    def slot(n):
        return pl.BlockSpec((None, None, S, HD), lambda b, h: (n, b, 0, h))

    head = pl.BlockSpec((None, S, HD), lambda b, h: (b, 0, h))
    gain = pl.BlockSpec((1, HD), lambda b, h: (0, 0))
    x_shape, x_sems = _chip_shapes(exchange)
    return pl.pallas_call(
        body, name="sb_bwd", grid=(Bl, HEADS),
        in_specs=[slot(0), slot(1), slot(2),
                  pl.BlockSpec((None, None, S, 1), lambda b, h: (b, h, 0, 0)), head, gain, gain, ANY] + [ANY] * ne,
        out_specs=[pl.BlockSpec((3, None, S, HD), lambda b, h: (0, b, 0, h)), gain, gain] + [ANY] * ne,
        out_shape=[jax.ShapeDtypeStruct(dproj.shape, dproj.dtype),
                   jax.ShapeDtypeStruct((1, HD), F32), jax.ShapeDtypeStruct((1, HD), F32)] + x_shape,
        scratch_shapes=([pltpu.VMEM((S, HD), BF16)] * 3 + [pltpu.VMEM((S, HD), F32)] * 3
                        + [pltpu.VMEM((S, 1), F32)] * 2 + x_sems),
        input_output_aliases={7: 0},
        compiler_params=_cparams(("arbitrary", "arbitrary"), vmem_mib=56),
    )(proj, proj, proj, ctot, do, q_gain, k_gain, dproj, *exchange)


def _lower_bound(logits):
    l0, l1 = logits[0:1, :], logits[1:2, :]
    m = jnp.maximum(l0, l1)
    e0, e1 = jnp.exp(l0 - m), jnp.exp(l1 - m)
    p0, p1 = e0 / (e0 + e1), e1 / (e0 + e1)
    return (p0 + p1) - p0, p0 * p1


def _hg_gates(qr, fp, lbv):
    sq = _sigmoid(qr)
    sp = _sigmoid(fp)
    sn = 1.0 / (1.0 + jnp.exp(fp))
    f = lbv + (1.0 - lbv) * sp
    return qr * sq, sq, sp, sn, f, (1.0 - lbv) * sn


def _group_consts():
    t = lax.broadcasted_iota(jnp.int32, (GR, GR), 0)
    j = lax.broadcasted_iota(jnp.int32, (GR, GR), 1)
    same = lax.shift_right_logical(t, CH_LOG2) == lax.shift_right_logical(j, CH_LOG2)
    tril = jnp.logical_and(same, j <= t)
    return (tril, tril.astype(BF16), jnp.logical_and(same, j >= t).astype(BF16), same.astype(BF16))


def _hg_decays(qa, k, f, t_inc, t_same):
    g = jnp.log(f)
    gc = _cum2l(t_inc, g)
    gl = _cum2l(t_same, g)
    gm = gc - 0.5 * gl
    e_q = jnp.exp(jnp.minimum(gm, EXP_CLAMP))
    e_k = jnp.exp(jnp.minimum(-gm, EXP_CLAMP))
    e_g = jnp.exp(gc)
    e_l = jnp.exp(gl - gc)
    return qa * e_q, k * e_k, qa * e_g, k * e_l, e_q, e_k, e_g, e_l, jnp.exp(gl)


def _hg_fwd(proj, lb_logits):
    _, Bl, S, _ = proj.shape
    nc = S // CH

    def body(q_ref, f_ref, i_ref, lg_ref, o_ref, st_ref, egl_s):
        lbv, _ = _lower_bound(lg_ref[...])
        tril, t_inc, _, t_same = _group_consts()
        st = jnp.zeros((HD, HD), F32)
        for g0 in range(0, S, GR):
            rs = slice(g0, g0 + GR)
            qa, _, _, _, f, k = _hg_gates(q_ref[rs, :], f_ref[rs, :], lbv)
            qt, kt, qg, kd, _, _, _, _, e_gl = _hg_decays(qa, k, f, t_inc, t_same)
            a = jnp.where(tril, _dot(qt.astype(BF16), kt.astype(BF16), NT), 0.0)
            ib, qgb, kdb = i_ref[rs, :].astype(BF16), qg.astype(BF16), kd.astype(BF16)
            within = _dot(a.astype(BF16), ib)
            egl_s[rs, :] = e_gl
            outs = []
            for l0 in range(0, GR, CH):
                ls = slice(l0, l0 + CH)
                st_ref[(g0 + l0) // CH] = st
                outs.append(within[ls] + _dot(qgb[ls], st.astype(BF16), NT))
                st = st * egl_s[g0 + l0:g0 + l0 + 1, :] + _dot(ib[ls], kdb[ls], TN)
            o_ref[rs, :] = jnp.concatenate(outs, axis=0)

    def slot(n):
        return pl.BlockSpec((None, None, S, HD), lambda b, h: (n, b, 0, h))

    return pl.pallas_call(
        body, name="hg_fwd", grid=(Bl, HEADS),
        in_specs=[slot(0), slot(1), slot(2), pl.BlockSpec((2, HD), lambda b, h: (0, h))],
        out_specs=[pl.BlockSpec((None, S, HD), lambda b, h: (b, 0, h)),
                   pl.BlockSpec((None, None, nc, HD, HD), lambda b, h: (b, h, 0, 0, 0))],
        out_shape=[jax.ShapeDtypeStruct((Bl, S, D), F32),
                   jax.ShapeDtypeStruct((Bl, HEADS, nc, HD, HD), F32)],
        scratch_shapes=[pltpu.VMEM((S, HD), F32)],
        compiler_params=_cparams(("parallel", "parallel")),
    )(proj, proj, proj, lb_logits)


def _hg_bwd(proj, states, do, dproj, lb_logits):
    _, Bl, S, _ = proj.shape

    def body(q_ref, f_ref, i_ref, st_ref, do_ref, lg_ref, _, dqfi_ref, dlb_ref, egl_s):
        lbv, _ = _lower_bound(lg_ref[...])
        tril, t_inc, t_dec, t_same = _group_consts()
        dst = jnp.zeros((HD, HD), F32)
        dlb = jnp.zeros((1, HD), F32)
        for g0 in reversed(range(0, S, GR)):
            rs = slice(g0, g0 + GR)
            qr, fp = q_ref[rs, :], f_ref[rs, :]
            qa, sq, sp, sn, f, k = _hg_gates(qr, fp, lbv)
            qt, kt, qg, kd, e_q, e_k, e_g, e_l, e_gl = _hg_decays(qa, k, f, t_inc, t_same)
            ib, dob, qgb, kdb = i_ref[rs, :].astype(BF16), do_ref[rs, :], qg.astype(BF16), kd.astype(BF16)
            egl_s[rs, :] = e_gl
            dqg, dkd, di, dse = [], [], [], []
            for l0 in reversed(range(0, GR, CH)):
                ls = slice(l0, l0 + CH)
                st = st_ref[(g0 + l0) // CH]
                dstb = dst.astype(BF16)
                dqg.insert(0, _dot(dob[ls], st.astype(BF16)))
                dkd.insert(0, _dot(ib[ls], dstb))
                di.insert(0, _dot(kdb[ls], dstb, NT))
                dse.insert(0, jnp.broadcast_to(jnp.sum(dst * st, axis=0, keepdims=True), (CH, HD)))
                dst = dst * egl_s[g0 + l0:g0 + l0 + 1, :] + _dot(dob[ls], qgb[ls], TN)
            dqg, dkd, di, dse = (jnp.concatenate(p, axis=0) for p in (dqg, dkd, di, dse))
            ab = jnp.where(tril, _dot(qt.astype(BF16), kt.astype(BF16), NT), 0.0).astype(BF16)
            da = jnp.where(tril, _dot(dob, ib, NT), 0.0)
            dqt = _dot3(da, kt)
            dkt = _dot3(da, qt, TN)
            dqfi_ref[2, rs, :] = (di + _dot(ab, dob, TN)).astype(BF16)
            dgc = dqt * qt - dkt * kt + dqg * qg - dkd * kd
            dg = _cum2l(t_dec, dgc) + _cum2l(t_same, dkd * kd) + dse * e_gl
            t1 = dg / f - (dkt * e_k + dkd * e_l)
            dqfi_ref[1, rs, :] = ((1.0 - lbv) * t1 * sp * sn).astype(BF16)
            dqfi_ref[0, rs, :] = ((dqt * e_q + dqg * e_g) * (sq * (1.0 + qr * (1.0 - sq)))).astype(BF16)
            dlb = dlb + jnp.sum(sn * t1, axis=0, keepdims=True)

        @pl.when(pl.program_id(1) == 0)
        def _():
            dlb_ref[...] = dlb

        @pl.when(pl.program_id(1) != 0)
        def _():
            dlb_ref[...] += dlb

    def slot(n):
        return pl.BlockSpec((None, None, S, HD), lambda h, b: (n, b, 0, h))

    return pl.pallas_call(
        body, name="hg_bwd", grid=(HEADS, Bl),
        in_specs=[slot(0), slot(1), slot(2),
                  pl.BlockSpec((None, None, S // CH, HD, HD), lambda h, b: (b, h, 0, 0, 0)),
                  pl.BlockSpec((None, S, HD), lambda h, b: (b, 0, h)),
                  pl.BlockSpec((2, HD), lambda h, b: (0, h)), ANY],
        out_specs=[pl.BlockSpec((3, None, S, HD), lambda h, b: (0, b, 0, h)),
                   pl.BlockSpec((1, HD), lambda h, b: (0, h))],
        out_shape=[jax.ShapeDtypeStruct(dproj.shape, dproj.dtype), jax.ShapeDtypeStruct((1, D), F32)],
        scratch_shapes=[pltpu.VMEM((S, HD), F32)],
        input_output_aliases={6: 0},
        compiler_params=_cparams(("parallel", "arbitrary")),
    )(proj, proj, proj, states, do, lb_logits, dproj)


def _place():
    x, y, c = lax.axis_index("x"), lax.axis_index("y"), lax.axis_index("c")
    return x, y, c, [(1 - x, y), (x, 1 - y), (1 - x, 1 - y)]


def _remote(src, dst, ssem, rsem, dev):
    return pltpu.make_async_remote_copy(src_ref=src, dst_ref=dst, send_sem=ssem, recv_sem=rsem,
                                        device_id=dev, device_id_type=MESH)


GATHER_SEMS = 9


def _gather_ops(wp_ref, wall_ref, ssem, rsem, hn_ref=None, hnall_ref=None):
    half = wp_ref.shape[0] // 2

    def place():
        x, y, c, chips = _place()
        return x, y, c, chips, 2 * x + y, pl.ds(c * half, half), pl.ds((1 - c) * half, half)

    def first_sends():
        x, y, c, chips, b, mine, _ = place()
        cps = [_remote(wp_ref.at[mine], wall_ref.at[b, mine], ssem.at[j], rsem.at[j], (*chip, c))
               for j, chip in enumerate(chips)]
        if hn_ref is not None:
            cps += [_remote(hn_ref, hnall_ref.at[b], ssem.at[6 + j], rsem.at[6 + j], (*chip, c))
                    for j, chip in enumerate(chips)]
        return cps

    def forwards():
        x, y, c, chips, _, mine, _ = place()
        return [_remote(wall_ref.at[2 * cx + cy, mine], wall_ref.at[2 * cx + cy, mine],
                        ssem.at[3 + j], rsem.at[3 + j], (x, y, 1 - c)) for j, (cx, cy) in enumerate(chips)]

    def start():
        for cp in first_sends():
            cp.start()

    def forward():
        x, y, c, chips, _, mine, _ = place()
        for j, (cx, cy) in enumerate(chips):
            landed = wall_ref.at[2 * cx + cy, mine]
            _remote(landed, landed, ssem.at[j], rsem.at[j], (cx, cy, c)).wait_recv()
        for cp in forwards():
            cp.start()

    def finish():
        x, y, c, chips, _, _, other = place()
        for j, (cx, cy) in enumerate(chips):
            passed = wall_ref.at[2 * cx + cy, other]
            _remote(passed, passed, ssem.at[3 + j], rsem.at[3 + j], (x, y, 1 - c)).wait_recv()
            if hn_ref is not None:
                row = hnall_ref.at[2 * cx + cy]
                _remote(row, row, ssem.at[6 + j], rsem.at[6 + j], (cx, cy, c)).wait_recv()
        for cp in first_sends() + forwards():
            cp.wait_send()

    return start, forward, finish


def _gather_weights(wp):
    def body(wp_ref, wall_ref, ssem, rsem):
        for step in _gather_ops(wp_ref, wall_ref, ssem, rsem):
            step()

    return pl.pallas_call(
        body, name="gather_weights", in_specs=[ANY], out_specs=ANY,
        out_shape=jax.ShapeDtypeStruct((NPROJ,) + wp.shape, BF16),
        scratch_shapes=[pltpu.SemaphoreType.DMA((GATHER_SEMS,)), pltpu.SemaphoreType.DMA((GATHER_SEMS,))],
    )(wp)


def _pair_ops(g_refs, r_refs, ssem, rsem):
    def copies():
        x, y, c, _ = _place()
        return [_remote(g.at[n, 1 - c], r.at[n], ssem.at[t * NPROJ + n], rsem.at[t * NPROJ + n], (x, y, 1 - c))
                for t, (g, r) in enumerate(zip(g_refs, r_refs)) for n in range(NPROJ)]

    def start():
        for cp in copies():
            cp.start()

    def finish():
        x, y, c, _ = _place()
        for t, r in enumerate(r_refs):
            for n in range(NPROJ):
                k = t * NPROJ + n
                _remote(r.at[n], r.at[n], ssem.at[k], rsem.at[k], (x, y, 1 - c)).wait_recv()
        for cp in copies():
            cp.wait_send()

    return start, finish


def _pair_shapes(grads):
    return ([jax.ShapeDtypeStruct((NPROJ,) + g.shape[2:], F32) for g in grads],
            [pltpu.SemaphoreType.DMA((len(grads) * NPROJ,))] * 2)


def _pair_exchange(grads):
    ng = len(grads)

    def body(*refs):
        start, finish = _pair_ops(refs[:ng], refs[ng:2 * ng], *refs[2 * ng:])
        start()
        finish()

    out_shape, sems = _pair_shapes(grads)
    return pl.pallas_call(
        body, name="pair_exchange", in_specs=[ANY] * ng, out_specs=[ANY] * ng,
        out_shape=out_shape, scratch_shapes=sems,
    )(*grads)


def _chip_ops(s_refs, r_refs, ssem, rsem):
    def copies():
        x, y, c, chips = _place()
        return [_remote(s.at[2 * cx + cy], r.at[2 * x + y], ssem.at[3 * t + j], rsem.at[3 * t + j], (cx, cy, c))
                for t, (s, r) in enumerate(zip(s_refs, r_refs)) for j, (cx, cy) in enumerate(chips)]

    def start():
        for cp in copies():
            cp.start()

    def finish():
        x, y, c, chips = _place()
        for t, r in enumerate(r_refs):
            for j, (cx, cy) in enumerate(chips):
                slot = r.at[2 * cx + cy]
                _remote(slot, slot, ssem.at[3 * t + j], rsem.at[3 * t + j], (cx, cy, c)).wait_recv()
        for cp in copies():
            cp.wait_send()

    return start, finish


def _chip_shapes(sums):
    return ([jax.ShapeDtypeStruct(s.shape, s.dtype) for s in sums],
            [pltpu.SemaphoreType.DMA((3 * len(sums),))] * 2)


def _sibling_share(halves, pack):
    ng = len(halves)

    def body(*refs):
        h_refs, pack_ref = refs[:ng], refs[ng]
        f_refs, allp_ref = refs[ng + 1:2 * ng + 1], refs[2 * ng + 1]
        ssem, rsem, psend, precv, lsem = refs[2 * ng + 2:]
        x, y, c, _ = _place()
        me = 4 * x + 2 * y + c
        local = pltpu.make_async_copy(pack_ref, allp_ref.at[me], lsem)
        local.start()
        flips = [(fx, fy, fc) for fx in (0, 1) for fy in (0, 1) for fc in (0, 1)][1:]
        peers = [(fx + x - 2 * fx * x, fy + y - 2 * fy * y, fc + c - 2 * fc * c) for fx, fy, fc in flips]
        sends = [_remote(pack_ref, allp_ref.at[me], psend.at[m], precv.at[m], peer) for m, peer in enumerate(peers)]
        sends += [_remote(h, f, ssem.at[t], rsem.at[t], (x, y, 1 - c))
                  for t, (h, f) in enumerate(zip(h_refs, f_refs))]
        for cp in sends:
            cp.start()
        for t, f in enumerate(f_refs):
            _remote(f, f, ssem.at[t], rsem.at[t], (x, y, 1 - c)).wait_recv()
        for m, (px, py, pc) in enumerate(peers):
            row = allp_ref.at[4 * px + 2 * py + pc]
            _remote(row, row, psend.at[m], precv.at[m], (px, py, pc)).wait_recv()
        for cp in sends:
            cp.wait_send()
        local.wait()

    return pl.pallas_call(
        body, name="sibling_share", in_specs=[ANY] * (ng + 1), out_specs=[ANY] * (ng + 1),
        out_shape=[jax.ShapeDtypeStruct(h.shape, F32) for h in halves]
        + [jax.ShapeDtypeStruct((8,) + pack.shape, F32)],
        scratch_shapes=[pltpu.SemaphoreType.DMA((ng,)), pltpu.SemaphoreType.DMA((ng,)),
                        pltpu.SemaphoreType.DMA((7,)), pltpu.SemaphoreType.DMA((7,)), pltpu.SemaphoreType.DMA],
    )(*halves, pack)


def _pair_add(own, recv, cidx, name):
    R = own.shape[2]
    tr = min(256, R)

    def body(c_ref, a_ref, b_ref, o_ref):
        o_ref[...] = (a_ref[...] + b_ref[...]).astype(BF16)

    return pl.pallas_call(
        body, name=name,
        grid_spec=pltpu.PrefetchScalarGridSpec(
            num_scalar_prefetch=1, grid=(NPROJ, R // tr),
            in_specs=[pl.BlockSpec((None, None, tr, D), lambda n, r, c: (n, c[0], r, 0)),
                      pl.BlockSpec((None, tr, D), lambda n, r, c: (n, r, 0))],
            out_specs=pl.BlockSpec((None, tr, D), lambda n, r, c: (n, r, 0))),
        out_shape=jax.ShapeDtypeStruct(recv.shape, BF16),
        compiler_params=_cparams(("parallel", "parallel")),
    )(cidx, own, recv)


def _chip_sum(sums, parts, bidx, name):
    R = parts.shape[1]
    tr = min(256, R)

    def body(b_ref, s_ref, p_ref, o_ref):
        acc = None
        for j in range(NPROJ):
            term = jnp.where(b_ref[0] == j, s_ref[...], p_ref[j]).astype(F32)
            acc = term if acc is None else acc + term
        o_ref[...] = acc

    return pl.pallas_call(
        body, name=name,
        grid_spec=pltpu.PrefetchScalarGridSpec(
            num_scalar_prefetch=1, grid=(R // tr,),
            in_specs=[pl.BlockSpec((None, tr, D), lambda r, b: (b[0], r, 0)),
                      pl.BlockSpec((NPROJ, tr, D), lambda r, b: (0, r, 0))],
            out_specs=pl.BlockSpec((tr, D), lambda r, b: (r, 0))),
        out_shape=jax.ShapeDtypeStruct((R, D), F32),
        compiler_params=_cparams(("parallel",)),
    )(bidx, sums, parts)


def _adamw_math(w, g, m, v):
    m = ADAM_B1 * m + (1.0 - ADAM_B1) * g
    v = ADAM_B2 * v + (1.0 - ADAM_B2) * (g * g)
    m_hat = m / (1.0 - ADAM_B1 ** ADAM_STEP)
    v_hat = v / (1.0 - ADAM_B2 ** ADAM_STEP)
    delta = -ADAM_LR * (m_hat / (jnp.sqrt(v_hat) + ADAM_EPS) + ADAM_WD * w)
    return delta, m, v


def _adamw(w, mine, theirs, m, v, cidx, name):
    R = mine.shape[0]
    tr = min(256, R)
    nr = R // tr

    def body(c_ref, w_ref, a_ref, b_ref, m_ref, v_ref, g_ref, d_ref, nm_ref, nv_ref):
        g = jnp.where(pl.program_id(0) == c_ref[0], a_ref[...], b_ref[...])
        g_ref[...] = g
        d_ref[...], nm_ref[...], nv_ref[...] = _adamw_math(w_ref[...], g, m_ref[...], v_ref[...])

    full = pl.BlockSpec((tr, D), lambda h, r, c: (h * nr + r, 0))
    half = pl.BlockSpec((tr, D), lambda h, r, c: (r, 0))
    return pl.pallas_call(
        body, name=name,
        grid_spec=pltpu.PrefetchScalarGridSpec(
            num_scalar_prefetch=1, grid=(2, nr),
            in_specs=[full, half, half, full, full], out_specs=[full] * 4),
        out_shape=[jax.ShapeDtypeStruct(w.shape, F32)] * 4,
        compiler_params=_cparams(("parallel", "parallel")),
    )(cidx, w, mine, theirs, m, v)


PACK_ROWS = 8


def _small_update(allp, bidx, logits, weights, moments_m, moments_v):
    shapes = [w.shape for w in weights]
    q4 = D // NPROJ

    def body(b_ref, allp_ref, hgp_ref, lg_ref, *refs):
        w_refs, m_refs, v_refs = refs[0:6], refs[6:12], refs[12:18]
        loss_ref = refs[18]
        g_out, d_out, m_out, v_out = refs[19:25], refs[25:31], refs[31:37], refs[37:43]

        def total(ref, row, lo, hi):
            acc = ref[0, row:row + 1, lo:hi]
            for dev in range(1, 8):
                acc = acc + ref[dev, row:row + 1, lo:hi]
            return acc

        _, pp = _lower_bound(lg_ref[...])
        dlb = total(allp_ref, 2, 0, D)
        grads = [total(allp_ref, 0, 0, D), total(allp_ref, 4, 0, HD), total(allp_ref, 4, HD, 2 * HD),
                 total(hgp_ref, 1, 0, q4), total(allp_ref, 4, 2 * HD, 3 * HD), None]
        loss_ref[...] = (0.5 / D) * jnp.sum(total(allp_ref, 3, 0, D), axis=1, keepdims=True)
        for t in range(6):
            if t < 5:
                rows = [(slice(None), grads[t])]
            else:
                rows = [(slice(0, 1), -pp * dlb), (slice(1, 2), pp * dlb)]
            for rs, g in rows:
                g_out[t][rs, :] = g
                d_out[t][rs, :], m_out[t][rs, :], v_out[t][rs, :] = _adamw_math(
                    w_refs[t][rs, :], g, m_refs[t][rs, :], v_refs[t][rs, :])

    whole = [pl.BlockSpec(s, lambda i, b: (0, 0)) for s in shapes]
    return pl.pallas_call(
        body, name="small_update",
        grid_spec=pltpu.PrefetchScalarGridSpec(
            num_scalar_prefetch=1, grid=(1,),
            in_specs=[pl.BlockSpec((8, PACK_ROWS, D), lambda i, b: (0, 0, 0)),
                      pl.BlockSpec((8, PACK_ROWS, q4), lambda i, b: (0, 0, b[0])),
                      pl.BlockSpec((2, D), lambda i, b: (0, 0))] + whole * 3,
            out_specs=[pl.BlockSpec((1, 1), lambda i, b: (0, 0))] + whole * 4),
        out_shape=[jax.ShapeDtypeStruct((1, 1), F32)] + [jax.ShapeDtypeStruct(s, F32) for s in shapes] * 4,
        compiler_params=_cparams(("arbitrary",)),
    )(bidx, allp, allp, logits, *weights, *moments_m, *moments_v)


def kernel(x, sb_norm, sb_w_in, sb_q_gain, sb_k_gain, sb_w_out, hg_norm, hg_w_in, hg_o_gain, hg_w_out, hg_lb_logits, loss_target, m_sb_norm, m_sb_w_in, m_sb_q_gain, m_sb_k_gain, m_sb_w_out, m_hg_norm, m_hg_w_in, m_hg_o_gain, m_hg_w_out, m_hg_lb_logits, v_sb_norm, v_sb_w_in, v_sb_q_gain, v_sb_k_gain, v_sb_w_out, v_hg_norm, v_hg_w_in, v_hg_o_gain, v_hg_w_out, v_hg_lb_logits):
    Bl, S, _ = x.shape
    T = Bl * S
    cidx = lax.axis_index("c").astype(jnp.int32).reshape(1)
    bidx = (2 * lax.axis_index("x") + lax.axis_index("y")).astype(jnp.int32).reshape(1)

    def in_hbm(arrays):
        return [pltpu.with_memory_space_constraint(a, pltpu.HBM) for a in arrays]

    def own_slot(gathered, mine):
        return lax.dynamic_update_slice(gathered, mine[None], (bidx[0],) + (0,) * mine.ndim)

    def halved(g):
        return g.reshape(NPROJ, 2, g.shape[-2] * g.shape[0] // (2 * NPROJ), D)

    def heads(a):
        return a.reshape(a.shape[:-2] + (Bl, S, D))

    def flat(a):
        return a.reshape(a.shape[:-3] + (T, D))

    wp_sb = jnp.concatenate([sb_w_in[0], sb_w_out[0]], axis=0).astype(BF16)
    wp_hg = jnp.concatenate([hg_w_in[0], hg_w_out[0]], axis=0).astype(BF16)
    wall_sb, = in_hbm([own_slot(_gather_weights(wp_sb), wp_sb)])
    x2 = x.reshape(T, D)
    tgt = loss_target.reshape(T, D)

    proj0 = _in_proj_fwd(x2, sb_norm, wall_sb, W_IN, "sb_in_fwd")
    o0, ctot, wall_hg, hnall = _sb_fwd(heads(proj0), sb_q_gain, sb_k_gain, wp_hg, hg_norm)
    wall_hg, = in_hbm([own_slot(wall_hg, wp_hg)])
    hgn = own_slot(hnall, hg_norm).reshape(1, D)
    h1 = _out_proj_fwd(flat(o0), proj0, x2, wall_sb, W_OUT, "sb_out_fwd")
    proj1 = _in_proj_fwd(h1, hgn, wall_hg, W_IN, "hg_in_fwd")
    o1, states = _hg_fwd(heads(proj1), hg_lb_logits)
    dh2, loss_terms = _out_proj_fwd(flat(o1), proj1, h1, wall_hg, W_OUT, "hg_out_fwd",
                                    o_gain=hg_o_gain, target=tgt)

    do1, dproj1, gout_hg, d_ogain = _out_proj_bwd(dh2, flat(o1), proj1, wall_hg, W_OUT, "hg_out_bwd",
                                                  o_gain=hg_o_gain)
    dproj1, dlb = _hg_bwd(heads(proj1), states, heads(do1), heads(dproj1), hg_lb_logits)
    dproj1 = flat(dproj1)
    dh1, d_hgn = _in_proj_bwd_x(dproj1, wall_hg, W_IN, h1, hgn, dh2, "hg_in_bwd_x")
    big_hg = [halved(_in_proj_bwd_w(dproj1, h1, hgn, "hg_in_bwd_w")), halved(gout_hg)]
    do0, dproj0, gout_sb, *recv_hg = _out_proj_bwd(dh1, flat(o0), proj0, wall_sb, W_OUT, "sb_out_bwd",
                                                   exchange=big_hg)
    sums_hg = in_hbm(_pair_add(g, r, cidx, "pair_add_" + nm)
                     for g, r, nm in zip(big_hg, in_hbm(recv_hg), ("hg_in", "hg_out")))
    dproj0, d_qg, d_kg, *parts_hg = _sb_bwd(heads(proj0), ctot, heads(do0), heads(dproj0),
                                            sb_q_gain, sb_k_gain, sums_hg)
    dproj0 = flat(dproj0)

    big_sb = [halved(_in_proj_bwd_w(dproj0, x2, sb_norm, "sb_in_bwd_w")), halved(gout_sb)]
    sums_sb = in_hbm(_pair_add(g, r, cidx, "pair_add_" + nm)
                     for g, r, nm in zip(big_sb, in_hbm(_pair_exchange(big_sb)), ("sb_in", "sb_out")))
    grad_x, d_sbn, *parts_sb = _in_proj_bwd_x(dproj0, wall_sb, W_IN, x2, sb_norm, dh1, "sb_in_bwd_x",
                                              exchange=sums_sb)

    names = ["sb_in", "hg_in", "sb_out", "hg_out"]
    sums = [sums_sb[0], sums_hg[0], sums_sb[1], sums_hg[1]]
    parts = [parts_sb[0], parts_hg[0], parts_sb[1], parts_hg[1]]
    halves = in_hbm(_chip_sum(sm, p, bidx, "chip_sum_" + nm) for sm, p, nm in zip(sums, in_hbm(parts), names))
    gains = jnp.concatenate([d_qg, d_kg, d_ogain, jnp.zeros((1, D - 3 * HD), F32)], axis=1)
    pack = jnp.concatenate([d_sbn, d_hgn, dlb, loss_terms, gains, jnp.zeros((3, D), F32)], axis=0)
    *theirs, allp = _sibling_share(halves, pack)
    theirs = in_hbm(theirs)

    big_w = [sb_w_in, hg_w_in, sb_w_out, hg_w_out]
    big_m = [m_sb_w_in, m_hg_w_in, m_sb_w_out, m_hg_w_out]
    big_v = [v_sb_w_in, v_hg_w_in, v_sb_w_out, v_hg_w_out]
    upd = [_adamw(w[0], a, b, m[0], v[0], cidx, "adamw_" + nm)
           for w, a, b, m, v, nm in zip(big_w, halves, theirs, big_m, big_v, names)]
    (g_sb_in, d_sb_in, nm_sb_in, nv_sb_in), (g_hg_in, d_hg_in, nm_hg_in, nv_hg_in), \
        (g_sb_out, d_sb_out, nm_sb_out, nv_sb_out), (g_hg_out, d_hg_out, nm_hg_out, nv_hg_out) = [
            tuple(a[None] for a in u) for u in upd]

    small = _small_update(
        allp, bidx, hg_lb_logits,
        [sb_norm, sb_q_gain, sb_k_gain, hg_norm, hg_o_gain, hg_lb_logits],
        [m_sb_norm, m_sb_q_gain, m_sb_k_gain, m_hg_norm, m_hg_o_gain, m_hg_lb_logits],
        [v_sb_norm, v_sb_q_gain, v_sb_k_gain, v_hg_norm, v_hg_o_gain, v_hg_lb_logits])
    loss = small[0].reshape(())
    (g_sbn, g_qg, g_kg, g_hgn, g_og, g_lb) = small[1:7]
    (d_sbn2, d_qg2, d_kg2, d_hgn2, d_og2, d_lb2) = small[7:13]
    (nm_sbn, nm_qg, nm_kg, nm_hgn, nm_og, nm_lb) = small[13:19]
    (nv_sbn, nv_qg, nv_kg, nv_hgn, nv_og, nv_lb) = small[19:25]

    return (loss, grad_x.reshape(Bl, S, D),
            g_sbn, g_sb_in, g_qg, g_kg, g_sb_out, g_hgn, g_hg_in, g_og, g_hg_out, g_lb,
            d_sbn2, d_sb_in, d_qg2, d_kg2, d_sb_out, d_hgn2, d_hg_in, d_og2, d_hg_out, d_lb2,
            nm_sbn, nm_sb_in, nm_qg, nm_kg, nm_sb_out, nm_hgn, nm_hg_in, nm_og, nm_hg_out, nm_lb,
            nv_sbn, nv_sb_in, nv_qg, nv_kg, nv_sb_out, nv_hgn, nv_hg_in, nv_og, nv_hg_out, nv_lb)
```

```python
import functools

import jax
import jax.numpy as jnp
from jax import lax
from jax.experimental import pallas as pl
from jax.experimental.pallas import tpu as pltpu

F32 = jnp.float32
BF16 = jnp.bfloat16
MESH = pl.DeviceIdType.MESH
ANY = pl.BlockSpec(memory_space=pl.ANY)

D = 1024
HEADS = 8
HD = 128
NPROJ = 4
RMS_EPS = 1e-6
TK = 256
CH = 64
CH_LOG2 = 6
GR = 256
SCALE = HD ** -0.5
EXP_CLAMP = 60.0
W_IN, W_OUT = 0, 4

ADAM_LR = 0.001
ADAM_B1 = 0.9
ADAM_B2 = 0.999
ADAM_EPS = 1e-08
ADAM_WD = 0.01
ADAM_STEP = 10

NT = (((1,), (1,)), ((), ()))
TN = (((0,), (0,)), ((), ()))
MIB = 1024 * 1024


def _cparams(sem=None, vmem_mib=40):
    return pltpu.CompilerParams(dimension_semantics=sem, vmem_limit_bytes=vmem_mib * MIB)


def _dot(a, b, dims=None):
    if dims is None:
        return jnp.dot(a, b, preferred_element_type=F32)
    return lax.dot_general(a, b, dims, preferred_element_type=F32)


def _sigmoid(x):
    return 1.0 / (1.0 + jnp.exp(-x))


def _rms(x):
    return lax.rsqrt(jnp.mean(x * x, axis=-1, keepdims=True) + RMS_EPS)


def _rms_bwd(x, r, gain, dy):
    a = dy * gain
    dx = r * a - x * (r * r * r) * jnp.mean(x * a, axis=-1, keepdims=True)
    return dx, dy * (x * r)


def _split2(v):
    hi = v.astype(BF16)
    lo = (v - hi.astype(F32)).astype(BF16)
    return hi, lo


def _cum2(v, u):
    hi, lo = _split2(v)
    return _dot(hi, u) + _dot(lo, u)


def _dot3(a, b, dims=None):
    ah, al = _split2(a)
    bh, bl = _split2(b)
    return _dot(ah, bh, dims) + _dot(ah, bl, dims) + _dot(al, bh, dims)


def _cum2l(u, v):
    hi, lo = _split2(v)
    return _dot(u, hi) + _dot(u, lo)


def _in_proj_fwd(h, gain, wall, wblk, name):
    T = h.shape[0]
    tm = min(1024, T)

    def body(h_ref, g_ref, w_ref, o_ref, u_s):
        rows = pl.ds(pl.multiple_of(pl.program_id(1) * tm, tm), tm)

        @pl.when(pl.program_id(0) == 0)
        def _():
            x = h_ref[...]
            u_s[rows, :] = (x * _rms(x) * g_ref[...]).astype(BF16)

        o_ref[...] = _dot(u_s[rows, :], w_ref[...])

    return pl.pallas_call(
        body, name=name, grid=(NPROJ, T // tm),
        in_specs=[pl.BlockSpec((tm, D), lambda n, i: (jnp.where(n == 0, i, 0), 0)),
                  pl.BlockSpec((1, D), lambda n, i: (0, 0)),
                  pl.BlockSpec((None, D, D), lambda n, i: (n, wblk, 0))],
        out_specs=pl.BlockSpec((None, tm, D), lambda n, i: (n, i, 0)),
        out_shape=jax.ShapeDtypeStruct((NPROJ, T, D), F32),
        scratch_shapes=[pltpu.VMEM((T, D), BF16)],
        compiler_params=_cparams(("arbitrary", "arbitrary")),
    )(h, gain, wall)


def _head_norm(x):
    outs = []
    for hh in range(x.shape[1] // HD):
        xs = x[:, hh * HD:(hh + 1) * HD]
        outs.append((xs, _rms(xs)))
    return outs


def _w_out_specs(wblk):
    kb = D // NPROJ
    return [pl.BlockSpec((None, kb, D), functools.partial(lambda j, i: (j, wblk, 0), j)) for j in range(NPROJ)]


def _out_proj_fwd(o, proj, resid, wall, wblk, name, o_gain=None, target=None):
    T = o.shape[0]
    tm = min(512, T)
    kb = D // NPROJ
    with_loss = target is not None

    def body(*refs):
        o_ref, g_ref, r_ref = refs[:3]
        w_refs = refs[3:3 + NPROJ]
        if with_loss:
            og_ref, t_ref, dh_ref, ls_ref = refs[3 + NPROJ:]
        else:
            h_ref, = refs[3 + NPROJ:]
        x = o_ref[...]
        if with_loss:
            x = jnp.concatenate([xs * r * og_ref[...] for xs, r in _head_norm(x)], axis=1)
        g = g_ref[...]
        a = (x * (g * _sigmoid(g))).astype(BF16)
        hnew = r_ref[...]
        for j in range(NPROJ):
            hnew = hnew + _dot(a[:, j * kb:(j + 1) * kb], w_refs[j][...])
        if with_loss:
            err = hnew - t_ref[...]
            dh_ref[...] = err * (1.0 / D)
            part = jnp.sum(err * err, axis=0, keepdims=True)

            @pl.when(pl.program_id(0) == 0)
            def _():
                ls_ref[...] = part

            @pl.when(pl.program_id(0) != 0)
            def _():
                ls_ref[...] += part
        else:
            h_ref[...] = hnew

    tile = pl.BlockSpec((tm, D), lambda i: (i, 0))
    in_specs = [tile, pl.BlockSpec((None, tm, D), lambda i: (3, i, 0)), tile] + _w_out_specs(wblk)
    args = [o, proj, resid] + [wall] * NPROJ
    out_specs = tile
    out_shape = jax.ShapeDtypeStruct((T, D), F32)
    if with_loss:
        in_specs += [pl.BlockSpec((1, HD), lambda i: (0, 0)), tile]
        args += [o_gain, target]
        out_specs = [tile, pl.BlockSpec((1, D), lambda i: (0, 0))]
        out_shape = [out_shape, jax.ShapeDtypeStruct((1, D), F32)]
    return pl.pallas_call(
        body, name=name, grid=(T // tm,), in_specs=in_specs, out_specs=out_specs,
        out_shape=out_shape, compiler_params=_cparams(("arbitrary",)),
    )(*args)


def _out_proj_bwd(dy, o, proj, wall, wblk, name, o_gain=None, exchange=()):
    T = o.shape[0]
    tm = min(512, T)
    kb = D // NPROJ
    normed = o_gain is not None
    ne = len(exchange)

    def body(*refs):
        it = iter(refs)
        dy_ref, o_ref, g_ref = (next(it) for _ in range(3))
        w_refs = [next(it) for _ in range(NPROJ)]
        og_ref = next(it) if normed else None
        xg_refs = [next(it) for _ in range(ne)]
        do_ref, dg_ref, dw_ref = (next(it) for _ in range(3))
        dgain_ref = next(it) if normed else None
        xr_refs = [next(it) for _ in range(ne)]
        wt_s = next(it)
        first = pl.program_id(0) == 0
        if ne:
            start, finish = _pair_ops(xg_refs, xr_refs, next(it), next(it))
            pl.when(first)(start)

        @pl.when(first)
        def _():
            for j in range(NPROJ):
                wt_s[:, j * kb:(j + 1) * kb] = w_refs[j][...].T
        g = g_ref[...]
        s = _sigmoid(g)
        sl = g * s
        x = o_ref[...]
        if normed:
            heads = _head_norm(x)
            on = jnp.concatenate([xs * r * og_ref[...] for xs, r in heads], axis=1)
        else:
            on = x
        dyb = dy_ref[...].astype(BF16)
        a = (on * sl).astype(BF16)
        for j in range(NPROJ):
            part = _dot(a[:, j * kb:(j + 1) * kb], dyb, TN)

            @pl.when(first)
            def _():
                dw_ref[j] = part

            @pl.when(jnp.logical_not(first))
            def _():
                dw_ref[j] += part

        da = _dot(dyb, wt_s[...])
        d_on = da * sl
        dg_ref[...] = (da * on * (s * (1.0 + g * (1.0 - s)))).astype(BF16)
        if normed:
            dxs, gsum = [], None
            for hh, (xs, r) in enumerate(heads):
                dx, gt = _rms_bwd(xs, r, og_ref[...], d_on[:, hh * HD:(hh + 1) * HD])
                dxs.append(dx)
                gt = jnp.sum(gt, axis=0, keepdims=True)
                gsum = gt if gsum is None else gsum + gt
            do_ref[...] = jnp.concatenate(dxs, axis=1).astype(BF16)

            @pl.when(first)
            def _():
                dgain_ref[...] = gsum

            @pl.when(jnp.logical_not(first))
            def _():
                dgain_ref[...] += gsum
        else:
            do_ref[...] = d_on.astype(BF16)
        if ne:
            pl.when(pl.program_id(0) == T // tm - 1)(finish)

    tile = pl.BlockSpec((tm, D), lambda i: (i, 0))
    gate = pl.BlockSpec((None, tm, D), lambda i: (3, i, 0))
    in_specs = [tile, tile, gate] + _w_out_specs(wblk)
    args = [dy, o, proj] + [wall] * NPROJ
    out_specs = [tile, gate, pl.BlockSpec((NPROJ, kb, D), lambda i: (0, 0, 0))]
    out_shape = [jax.ShapeDtypeStruct((T, D), BF16),
                 jax.ShapeDtypeStruct((NPROJ, T, D), BF16),
                 jax.ShapeDtypeStruct((NPROJ, kb, D), F32)]
    if normed:
        in_specs.append(pl.BlockSpec((1, HD), lambda i: (0, 0)))
        args.append(o_gain)
        out_specs.append(pl.BlockSpec((1, HD), lambda i: (0, 0)))
        out_shape.append(jax.ShapeDtypeStruct((1, HD), F32))
    x_shape, x_sems = _pair_shapes(exchange) if ne else ([], [])
    return pl.pallas_call(
        body, name=name, grid=(T // tm,), in_specs=in_specs + [ANY] * ne, out_specs=out_specs + [ANY] * ne,
        out_shape=out_shape + x_shape, scratch_shapes=[pltpu.VMEM((D, D), BF16)] + x_sems,
        compiler_params=_cparams(("arbitrary",), vmem_mib=48),
    )(*args, *exchange)


def _in_proj_bwd_x(dproj, wall, wblk, h, gain, dres, name, exchange=()):
    T = h.shape[0]
    tm = min(512, T)
    ne = len(exchange)

    def body(d_ref, w_ref, h_ref, g_ref, r_ref, *refs):
        xs_refs, (dh_ref, dgain_ref), xr_refs = refs[:ne], refs[ne:ne + 2], refs[ne + 2:2 * ne + 2]
        du, wt_s = refs[2 * ne + 2:2 * ne + 4]
        i, n = pl.program_id(0), pl.program_id(1)
        if ne:
            start, finish = _chip_ops(xs_refs, xr_refs, *refs[2 * ne + 4:])
            pl.when(jnp.logical_and(i == 0, n == 0))(start)

        @pl.when(i == 0)
        def _():
            wt_s[n] = w_ref[...].T

        part = _dot(d_ref[...], wt_s[n])

        @pl.when(n == 0)
        def _():
            du[...] = part

        @pl.when(n != 0)
        def _():
            du[...] += part

        @pl.when(n == NPROJ - 1)
        def _():
            x = h_ref[...]
            dx, gt = _rms_bwd(x, _rms(x), g_ref[...], du[...])
            dh_ref[...] = r_ref[...] + dx
            gt = jnp.sum(gt, axis=0, keepdims=True)

            @pl.when(i == 0)
            def _():
                dgain_ref[...] = gt

            @pl.when(i != 0)
            def _():
                dgain_ref[...] += gt

        if ne:
            pl.when(jnp.logical_and(i == T // tm - 1, n == NPROJ - 1))(finish)

    x_shape, x_sems = _chip_shapes(exchange) if ne else ([], [])
    return pl.pallas_call(
        body, name=name, grid=(T // tm, NPROJ),
        in_specs=[pl.BlockSpec((None, tm, D), lambda i, n: (n, i, 0)),
                  pl.BlockSpec((None, D, D), lambda i, n: (jnp.where(i == 0, n, NPROJ - 1), wblk, 0)),
                  pl.BlockSpec((tm, D), lambda i, n: (i, 0)),
                  pl.BlockSpec((1, D), lambda i, n: (0, 0)),
                  pl.BlockSpec((tm, D), lambda i, n: (i, 0))] + [ANY] * ne,
        out_specs=[pl.BlockSpec((tm, D), lambda i, n: (i, 0)),
                   pl.BlockSpec((1, D), lambda i, n: (0, 0))] + [ANY] * ne,
        out_shape=[jax.ShapeDtypeStruct((T, D), F32), jax.ShapeDtypeStruct((1, D), F32)] + x_shape,
        scratch_shapes=[pltpu.VMEM((tm, D), F32), pltpu.VMEM((NPROJ, D, D), BF16)] + x_sems,
        compiler_params=_cparams(("arbitrary", "arbitrary")),
    )(dproj, wall, h, gain, dres, *exchange)


def _in_proj_bwd_w(dproj, h, gain, name):
    T = h.shape[0]
    tk = min(1024, T)

    def body(d_ref, h_ref, g_ref, dw_ref, ut_s):
        k = pl.program_id(1)

        @pl.when(pl.program_id(0) == 0)
        def _():
            x = h_ref[...]
            ut_s[k] = (x * _rms(x) * g_ref[...]).astype(BF16).T

        part = _dot(ut_s[k], d_ref[...])

        @pl.when(k == 0)
        def _():
            dw_ref[...] = part

        @pl.when(k != 0)
        def _():
            dw_ref[...] += part

    return pl.pallas_call(
        body, name=name, grid=(NPROJ, T // tk),
        in_specs=[pl.BlockSpec((None, tk, D), lambda n, k: (n, k, 0)),
                  pl.BlockSpec((tk, D), lambda n, k: (jnp.where(n == 0, k, 0), 0)),
                  pl.BlockSpec((1, D), lambda n, k: (0, 0))],
        out_specs=pl.BlockSpec((None, D, D), lambda n, k: (n, 0, 0)),
        out_shape=jax.ShapeDtypeStruct((NPROJ, D, D), F32),
        scratch_shapes=[pltpu.VMEM((T // tk, D, tk), BF16)],
        compiler_params=_cparams(("arbitrary", "arbitrary")),
    )(dproj, h, gain)


def _log_sigmoid_pair(z):
    lb = jnp.minimum(z, 0.0) - jnp.log(1.0 + jnp.exp(-jnp.abs(z)))
    return lb, lb - z


def _slab_consts():
    t = lax.broadcasted_iota(jnp.int32, (TK, TK), 0)
    s = lax.broadcasted_iota(jnp.int32, (TK, TK), 1)
    return s < t, (t > s).astype(BF16), (t < s).astype(BF16)


def _slab_rows(k0, S):
    return [(r0, r1, masked) for r0, r1, masked in ((k0, k0 + TK, True), (k0 + TK, S, False)) if r0 < r1]


def _sb_fwd(proj, q_gain, k_gain, wp, hn):
    _, Bl, S, _ = proj.shape
    steps = Bl * HEADS

    def body(q_ref, k_ref, v_ref, qg_ref, kg_ref, wp_ref, hn_ref, o_ref, ct_ref, wall_ref, hnall_ref,
             qn, kn, vb, ssem, rsem):
        step = pl.program_id(0) * HEADS + pl.program_id(1)
        start, forward, finish = _gather_ops(wp_ref, wall_ref, ssem, rsem, hn_ref, hnall_ref)
        pl.when(step == 0)(start)
        pl.when(step == steps // 2)(forward)
        q = q_ref[...]
        qn[...] = (q * _rms(q) * (qg_ref[...] * SCALE)).astype(BF16)
        k = k_ref[...]
        kn[...] = (k * _rms(k) * kg_ref[...]).astype(BF16)
        vb[...] = v_ref[...].astype(BF16)
        tri, u_gt, _ = _slab_consts()
        nb = S // TK
        c_blk = [jnp.zeros((TK, 1), F32)] * nb
        o_blk = [jnp.zeros((TK, HD), F32)] * nb
        for k0 in reversed(range(0, S, TK)):
            kb, vbb = kn[k0:k0 + TK, :], vb[k0:k0 + TK, :]
            for r0, r1, masked in _slab_rows(k0, S):
                blocks = range(r0 // TK, r1 // TK)
                z = _dot(qn[r0:r1, :], kb, NT)
                lb, ls = _log_sigmoid_pair(z)
                if masked:
                    ls = jnp.where(tri, ls, 0.0)
                c = jnp.concatenate([c_blk[b] for b in blocks], axis=0)
                w = jnp.exp(lb + _cum2(ls, u_gt) + c)
                if masked:
                    w = jnp.where(tri, w, 0.0)
                o_new = _dot(w.astype(BF16), vbb)
                c_new = jnp.sum(ls, axis=1, keepdims=True)
                for i, b in enumerate(blocks):
                    o_blk[b] = o_blk[b] + o_new[i * TK:(i + 1) * TK]
                    c_blk[b] = c_blk[b] + c_new[i * TK:(i + 1) * TK]
        o_ref[...] = jnp.concatenate(o_blk, axis=0)
        ct_ref[...] = jnp.concatenate(c_blk, axis=0)
        pl.when(step == steps - 1)(finish)

    def slot(n):
        return pl.BlockSpec((None, None, S, HD), lambda b, h: (n, b, 0, h))

    return pl.pallas_call(
        body, name="sb_fwd", grid=(Bl, HEADS),
        in_specs=[slot(0), slot(1), slot(2),
                  pl.BlockSpec((1, HD), lambda b, h: (0, 0)),
                  pl.BlockSpec((1, HD), lambda b, h: (0, 0)), ANY, ANY],
        out_specs=[pl.BlockSpec((None, S, HD), lambda b, h: (b, 0, h)),
                   pl.BlockSpec((None, None, S, 1), lambda b, h: (b, h, 0, 0)), ANY, ANY],
        out_shape=[jax.ShapeDtypeStruct((Bl, S, D), F32),
                   jax.ShapeDtypeStruct((Bl, HEADS, S, 1), F32),
                   jax.ShapeDtypeStruct((NPROJ,) + wp.shape, BF16),
                   jax.ShapeDtypeStruct((NPROJ,) + hn.shape, F32)],
        scratch_shapes=[pltpu.VMEM((S, HD), BF16)] * 3 + [pltpu.SemaphoreType.DMA((GATHER_SEMS,))] * 2,
        compiler_params=_cparams(("arbitrary", "arbitrary"), vmem_mib=56),
    )(proj, proj, proj, q_gain, k_gain, wp, hn)


def _sb_bwd(proj, ctot, do, dproj, q_gain, k_gain, exchange):
    _, Bl, S, _ = proj.shape
    ne = len(exchange)

    def body(q_ref, k_ref, v_ref, ct_ref, do_ref, qg_ref, kg_ref, _, *refs):
        xs_refs, (dqkv_ref, dqg_ref, dkg_ref), xr_refs = refs[:ne], refs[ne:ne + 3], refs[ne + 3:2 * ne + 3]
        qn, kn, vb, dqn, dkn, dvn, passed_s, e_s, ssem, rsem = refs[2 * ne + 3:]
        step = pl.program_id(0) * HEADS + pl.program_id(1)
        first = step == 0
        start, finish = _chip_ops(xs_refs, xr_refs, ssem, rsem)

        @pl.when(first)
        def _():
            start()
            dqg_ref[...] = jnp.zeros_like(dqg_ref)
            dkg_ref[...] = jnp.zeros_like(dkg_ref)

        q = q_ref[...]
        rq = _rms(q)
        qn[...] = (q * rq * (qg_ref[...] * SCALE)).astype(BF16)
        k = k_ref[...]
        rk = _rms(k)
        kn[...] = (k * rk * kg_ref[...]).astype(BF16)
        vb[...] = v_ref[...].astype(BF16)
        for acc in (dqn, dkn, dvn, passed_s, e_s):
            acc[...] = jnp.zeros_like(acc)
        tri, u_gt, u_lt = _slab_consts()
        for k0 in range(0, S, TK):
            keys = slice(k0, k0 + TK)
            kb, vbb = kn[keys, :], vb[keys, :]
            for r0, r1, masked in _slab_rows(k0, S):
                rows = slice(r0, r1)
                qb, dobb = qn[rows, :], do_ref[rows, :]
                z = _dot(qb, kb, NT)
                lb, ls = _log_sigmoid_pair(z)
                if masked:
                    ls = jnp.where(tri, ls, 0.0)
                passed = passed_s[rows, :] + jnp.sum(ls, axis=1, keepdims=True)
                passed_s[rows, :] = passed
                w = jnp.exp(lb + _cum2(ls, u_gt) + (ct_ref[rows, :] - passed))
                if masked:
                    w = jnp.where(tri, w, 0.0)
                de = w * _dot(dobb, vbb, NT)
                dvn[keys, :] += _dot(w.astype(BF16), dobb, TN)
                e = e_s[rows, :]
                dls = e + _cum2(de, u_lt)
                e_s[rows, :] = e + jnp.sum(de, axis=1, keepdims=True)
                sg = jnp.exp(lb)
                dz = de - sg * (de + dls)
                if masked:
                    dz = jnp.where(tri, dz, 0.0)
                dzb = dz.astype(BF16)
                dqn[rows, :] += _dot(dzb, kb)
                dkn[keys, :] += _dot(dzb, qb, TN)

        dx, gt = _rms_bwd(q, rq, qg_ref[...], dqn[...] * SCALE)
        dqkv_ref[0] = dx.astype(BF16)
        dqg_ref[...] += jnp.sum(gt, axis=0, keepdims=True)
        dx, gt = _rms_bwd(k, rk, kg_ref[...], dkn[...])
        dqkv_ref[1] = dx.astype(BF16)
        dkg_ref[...] += jnp.sum(gt, axis=0, keepdims=True)
        dqkv_ref[2] = dvn[...].astype(BF16)
        pl.when(step == Bl * HEADS - 1)(finish)

    def slot(n):
        return pl.BlockSpec((None, None, S, HD), lambda b, h: (n, b, 0, h))

    head = pl.BlockSpec((None, S, HD), lambda b, h: (b, 0, h))
    gain = pl.BlockSpec((1, HD), lambda b, h: (0, 0))
    x_shape, x_sems = _chip_shapes(exchange)
    return pl.pallas_call(
        body, name="sb_bwd", grid=(Bl, HEADS),
        in_specs=[slot(0), slot(1), slot(2),
                  pl.BlockSpec((None, None, S, 1), lambda b, h: (b, h, 0, 0)), head, gain, gain, ANY] + [ANY] * ne,
        out_specs=[pl.BlockSpec((3, None, S, HD), lambda b, h: (0, b, 0, h)), gain, gain] + [ANY] * ne,
        out_shape=[jax.ShapeDtypeStruct(dproj.shape, dproj.dtype),
                   jax.ShapeDtypeStruct((1, HD), F32), jax.ShapeDtypeStruct((1, HD), F32)] + x_shape,
        scratch_shapes=([pltpu.VMEM((S, HD), BF16)] * 3 + [pltpu.VMEM((S, HD), F32)] * 3
                        + [pltpu.VMEM((S, 1), F32)] * 2 + x_sems),
        input_output_aliases={7: 0},
        compiler_params=_cparams(("arbitrary", "arbitrary"), vmem_mib=56),
    )(proj, proj, proj, ctot, do, q_gain, k_gain, dproj, *exchange)


def _lower_bound(logits):
    l0, l1 = logits[0:1, :], logits[1:2, :]
    m = jnp.maximum(l0, l1)
    e0, e1 = jnp.exp(l0 - m), jnp.exp(l1 - m)
    p0, p1 = e0 / (e0 + e1), e1 / (e0 + e1)
    return (p0 + p1) - p0, p0 * p1


def _hg_gates(qr, fp, lbv):
    sq = _sigmoid(qr)
    sp = _sigmoid(fp)
    sn = 1.0 / (1.0 + jnp.exp(fp))
    f = lbv + (1.0 - lbv) * sp
    return qr * sq, sq, sp, sn, f, (1.0 - lbv) * sn


def _group_consts():
    t = lax.broadcasted_iota(jnp.int32, (GR, GR), 0)
    j = lax.broadcasted_iota(jnp.int32, (GR, GR), 1)
    same = lax.shift_right_logical(t, CH_LOG2) == lax.shift_right_logical(j, CH_LOG2)
    tril = jnp.logical_and(same, j <= t)
    return (tril, tril.astype(BF16), jnp.logical_and(same, j >= t).astype(BF16), same.astype(BF16))


def _hg_decays(qa, k, f, t_inc, t_same):
    g = jnp.log(f)
    gc = _cum2l(t_inc, g)
    gl = _cum2l(t_same, g)
    gm = gc - 0.5 * gl
    e_q = jnp.exp(jnp.minimum(gm, EXP_CLAMP))
    e_k = jnp.exp(jnp.minimum(-gm, EXP_CLAMP))
    e_g = jnp.exp(gc)
    e_l = jnp.exp(gl - gc)
    return qa * e_q, k * e_k, qa * e_g, k * e_l, e_q, e_k, e_g, e_l, jnp.exp(gl)


def _hg_fwd(proj, lb_logits):
    _, Bl, S, _ = proj.shape
    nc = S // CH

    def body(q_ref, f_ref, i_ref, lg_ref, o_ref, st_ref, egl_s):
        lbv, _ = _lower_bound(lg_ref[...])
        tril, t_inc, _, t_same = _group_consts()
        st = jnp.zeros((HD, HD), F32)
        for g0 in range(0, S, GR):
            rs = slice(g0, g0 + GR)
            qa, _, _, _, f, k = _hg_gates(q_ref[rs, :], f_ref[rs, :], lbv)
            qt, kt, qg, kd, _, _, _, _, e_gl = _hg_decays(qa, k, f, t_inc, t_same)
            a = jnp.where(tril, _dot(qt.astype(BF16), kt.astype(BF16), NT), 0.0)
            ib, qgb, kdb = i_ref[rs, :].astype(BF16), qg.astype(BF16), kd.astype(BF16)
            within = _dot(a.astype(BF16), ib)
            egl_s[rs, :] = e_gl
            outs = []
            for l0 in range(0, GR, CH):
                ls = slice(l0, l0 + CH)
                st_ref[(g0 + l0) // CH] = st
                outs.append(within[ls] + _dot(qgb[ls], st.astype(BF16), NT))
                st = st * egl_s[g0 + l0:g0 + l0 + 1, :] + _dot(ib[ls], kdb[ls], TN)
            o_ref[rs, :] = jnp.concatenate(outs, axis=0)

    def slot(n):
        return pl.BlockSpec((None, None, S, HD), lambda b, h: (n, b, 0, h))

    return pl.pallas_call(
        body, name="hg_fwd", grid=(Bl, HEADS),
        in_specs=[slot(0), slot(1), slot(2), pl.BlockSpec((2, HD), lambda b, h: (0, h))],
        out_specs=[pl.BlockSpec((None, S, HD), lambda b, h: (b, 0, h)),
                   pl.BlockSpec((None, None, nc, HD, HD), lambda b, h: (b, h, 0, 0, 0))],
        out_shape=[jax.ShapeDtypeStruct((Bl, S, D), F32),
                   jax.ShapeDtypeStruct((Bl, HEADS, nc, HD, HD), F32)],
        scratch_shapes=[pltpu.VMEM((S, HD), F32)],
        compiler_params=_cparams(("parallel", "parallel")),
    )(proj, proj, proj, lb_logits)


def _hg_bwd(proj, states, do, dproj, lb_logits):
    _, Bl, S, _ = proj.shape

    def body(q_ref, f_ref, i_ref, st_ref, do_ref, lg_ref, _, dqfi_ref, dlb_ref, egl_s):
        lbv, _ = _lower_bound(lg_ref[...])
        tril, t_inc, t_dec, t_same = _group_consts()
        dst = jnp.zeros((HD, HD), F32)
        dlb = jnp.zeros((1, HD), F32)
        for g0 in reversed(range(0, S, GR)):
            rs = slice(g0, g0 + GR)
            qr, fp = q_ref[rs, :], f_ref[rs, :]
            qa, sq, sp, sn, f, k = _hg_gates(qr, fp, lbv)
            qt, kt, qg, kd, e_q, e_k, e_g, e_l, e_gl = _hg_decays(qa, k, f, t_inc, t_same)
            ib, dob, qgb, kdb = i_ref[rs, :].astype(BF16), do_ref[rs, :], qg.astype(BF16), kd.astype(BF16)
            egl_s[rs, :] = e_gl
            dqg, dkd, di, dse = [], [], [], []
            for l0 in reversed(range(0, GR, CH)):
                ls = slice(l0, l0 + CH)
                st = st_ref[(g0 + l0) // CH]
                dstb = dst.astype(BF16)
                dqg.insert(0, _dot(dob[ls], st.astype(BF16)))
                dkd.insert(0, _dot(ib[ls], dstb))
                di.insert(0, _dot(kdb[ls], dstb, NT))
                dse.insert(0, jnp.broadcast_to(jnp.sum(dst * st, axis=0, keepdims=True), (CH, HD)))
                dst = dst * egl_s[g0 + l0:g0 + l0 + 1, :] + _dot(dob[ls], qgb[ls], TN)
            dqg, dkd, di, dse = (jnp.concatenate(p, axis=0) for p in (dqg, dkd, di, dse))
            ab = jnp.where(tril, _dot(qt.astype(BF16), kt.astype(BF16), NT), 0.0).astype(BF16)
            da = jnp.where(tril, _dot(dob, ib, NT), 0.0)
            dqt = _dot3(da, kt)
            dkt = _dot3(da, qt, TN)
            dqfi_ref[2, rs, :] = (di + _dot(ab, dob, TN)).astype(BF16)
            dgc = dqt * qt - dkt * kt + dqg * qg - dkd * kd
            dg = _cum2l(t_dec, dgc) + _cum2l(t_same, dkd * kd) + dse * e_gl
            t1 = dg / f - (dkt * e_k + dkd * e_l)
            dqfi_ref[1, rs, :] = ((1.0 - lbv) * t1 * sp * sn).astype(BF16)
            dqfi_ref[0, rs, :] = ((dqt * e_q + dqg * e_g) * (sq * (1.0 + qr * (1.0 - sq)))).astype(BF16)
            dlb = dlb + jnp.sum(sn * t1, axis=0, keepdims=True)

        @pl.when(pl.program_id(1) == 0)
        def _():
            dlb_ref[...] = dlb

        @pl.when(pl.program_id(1) != 0)
        def _():
            dlb_ref[...] += dlb

    def slot(n):
        return pl.BlockSpec((None, None, S, HD), lambda h, b: (n, b, 0, h))

    return pl.pallas_call(
        body, name="hg_bwd", grid=(HEADS, Bl),
        in_specs=[slot(0), slot(1), slot(2),
                  pl.BlockSpec((None, None, S // CH, HD, HD), lambda h, b: (b, h, 0, 0, 0)),
                  pl.BlockSpec((None, S, HD), lambda h, b: (b, 0, h)),
                  pl.BlockSpec((2, HD), lambda h, b: (0, h)), ANY],
        out_specs=[pl.BlockSpec((3, None, S, HD), lambda h, b: (0, b, 0, h)),
                   pl.BlockSpec((1, HD), lambda h, b: (0, h))],
        out_shape=[jax.ShapeDtypeStruct(dproj.shape, dproj.dtype), jax.ShapeDtypeStruct((1, D), F32)],
        scratch_shapes=[pltpu.VMEM((S, HD), F32)],
        input_output_aliases={6: 0},
        compiler_params=_cparams(("parallel", "arbitrary")),
    )(proj, proj, proj, states, do, lb_logits, dproj)


def _place():
    x, y, c = lax.axis_index("x"), lax.axis_index("y"), lax.axis_index("c")
    return x, y, c, [(1 - x, y), (x, 1 - y), (1 - x, 1 - y)]


def _remote(src, dst, ssem, rsem, dev):
    return pltpu.make_async_remote_copy(src_ref=src, dst_ref=dst, send_sem=ssem, recv_sem=rsem,
                                        device_id=dev, device_id_type=MESH)


GATHER_SEMS = 9


def _gather_ops(wp_ref, wall_ref, ssem, rsem, hn_ref=None, hnall_ref=None):
    half = wp_ref.shape[0] // 2

    def place():
        x, y, c, chips = _place()
        return x, y, c, chips, 2 * x + y, pl.ds(c * half, half), pl.ds((1 - c) * half, half)

    def first_sends():
        x, y, c, chips, b, mine, _ = place()
        cps = [_remote(wp_ref.at[mine], wall_ref.at[b, mine], ssem.at[j], rsem.at[j], (*chip, c))
               for j, chip in enumerate(chips)]
        if hn_ref is not None:
            cps += [_remote(hn_ref, hnall_ref.at[b], ssem.at[6 + j], rsem.at[6 + j], (*chip, c))
                    for j, chip in enumerate(chips)]
        return cps

    def forwards():
        x, y, c, chips, _, mine, _ = place()
        return [_remote(wall_ref.at[2 * cx + cy, mine], wall_ref.at[2 * cx + cy, mine],
                        ssem.at[3 + j], rsem.at[3 + j], (x, y, 1 - c)) for j, (cx, cy) in enumerate(chips)]

    def start():
        for cp in first_sends():
            cp.start()

    def forward():
        x, y, c, chips, _, mine, _ = place()
        for j, (cx, cy) in enumerate(chips):
            landed = wall_ref.at[2 * cx + cy, mine]
            _remote(landed, landed, ssem.at[j], rsem.at[j], (cx, cy, c)).wait_recv()
        for cp in forwards():
            cp.start()

    def finish():
        x, y, c, chips, _, _, other = place()
        for j, (cx, cy) in enumerate(chips):
            passed = wall_ref.at[2 * cx + cy, other]
            _remote(passed, passed, ssem.at[3 + j], rsem.at[3 + j], (x, y, 1 - c)).wait_recv()
            if hn_ref is not None:
                row = hnall_ref.at[2 * cx + cy]
                _remote(row, row, ssem.at[6 + j], rsem.at[6 + j], (cx, cy, c)).wait_recv()
        for cp in first_sends() + forwards():
            cp.wait_send()

    return start, forward, finish


def _gather_weights(wp):
    def body(wp_ref, wall_ref, ssem, rsem):
        for step in _gather_ops(wp_ref, wall_ref, ssem, rsem):
            step()

    return pl.pallas_call(
        body, name="gather_weights", in_specs=[ANY], out_specs=ANY,
        out_shape=jax.ShapeDtypeStruct((NPROJ,) + wp.shape, BF16),
        scratch_shapes=[pltpu.SemaphoreType.DMA((GATHER_SEMS,)), pltpu.SemaphoreType.DMA((GATHER_SEMS,))],
    )(wp)


def _pair_ops(g_refs, r_refs, ssem, rsem):
    def copies():
        x, y, c, _ = _place()
        return [_remote(g.at[n, 1 - c], r.at[n], ssem.at[t * NPROJ + n], rsem.at[t * NPROJ + n], (x, y, 1 - c))
                for t, (g, r) in enumerate(zip(g_refs, r_refs)) for n in range(NPROJ)]

    def start():
        for cp in copies():
            cp.start()

    def finish():
        x, y, c, _ = _place()
        for t, r in enumerate(r_refs):
            for n in range(NPROJ):
                k = t * NPROJ + n
                _remote(r.at[n], r.at[n], ssem.at[k], rsem.at[k], (x, y, 1 - c)).wait_recv()
        for cp in copies():
            cp.wait_send()

    return start, finish


def _pair_shapes(grads):
    return ([jax.ShapeDtypeStruct((NPROJ,) + g.shape[2:], F32) for g in grads],
            [pltpu.SemaphoreType.DMA((len(grads) * NPROJ,))] * 2)


def _pair_exchange(grads):
    ng = len(grads)

    def body(*refs):
        start, finish = _pair_ops(refs[:ng], refs[ng:2 * ng], *refs[2 * ng:])
        start()
        finish()

    out_shape, sems = _pair_shapes(grads)
    return pl.pallas_call(
        body, name="pair_exchange", in_specs=[ANY] * ng, out_specs=[ANY] * ng,
        out_shape=out_shape, scratch_shapes=sems,
    )(*grads)


def _chip_ops(s_refs, r_refs, ssem, rsem):
    def copies():
        x, y, c, chips = _place()
        return [_remote(s.at[2 * cx + cy], r.at[2 * x + y], ssem.at[3 * t + j], rsem.at[3 * t + j], (cx, cy, c))
                for t, (s, r) in enumerate(zip(s_refs, r_refs)) for j, (cx, cy) in enumerate(chips)]

    def start():
        for cp in copies():
            cp.start()

    def finish():
        x, y, c, chips = _place()
        for t, r in enumerate(r_refs):
            for j, (cx, cy) in enumerate(chips):
                slot = r.at[2 * cx + cy]
                _remote(slot, slot, ssem.at[3 * t + j], rsem.at[3 * t + j], (cx, cy, c)).wait_recv()
        for cp in copies():
            cp.wait_send()

    return start, finish


def _chip_shapes(sums):
    return ([jax.ShapeDtypeStruct(s.shape, s.dtype) for s in sums],
            [pltpu.SemaphoreType.DMA((3 * len(sums),))] * 2)


def _sibling_share(halves, pack):
    ng = len(halves)

    def body(*refs):
        h_refs, pack_ref = refs[:ng], refs[ng]
        f_refs, allp_ref = refs[ng + 1:2 * ng + 1], refs[2 * ng + 1]
        ssem, rsem, psend, precv, lsem = refs[2 * ng + 2:]
        x, y, c, _ = _place()
        me = 4 * x + 2 * y + c
        local = pltpu.make_async_copy(pack_ref, allp_ref.at[me], lsem)
        local.start()
        flips = [(fx, fy, fc) for fx in (0, 1) for fy in (0, 1) for fc in (0, 1)][1:]
        peers = [(fx + x - 2 * fx * x, fy + y - 2 * fy * y, fc + c - 2 * fc * c) for fx, fy, fc in flips]
        sends = [_remote(pack_ref, allp_ref.at[me], psend.at[m], precv.at[m], peer) for m, peer in enumerate(peers)]
        sends += [_remote(h, f, ssem.at[t], rsem.at[t], (x, y, 1 - c))
                  for t, (h, f) in enumerate(zip(h_refs, f_refs))]
        for cp in sends:
            cp.start()
        for t, f in enumerate(f_refs):
            _remote(f, f, ssem.at[t], rsem.at[t], (x, y, 1 - c)).wait_recv()
        for m, (px, py, pc) in enumerate(peers):
            row = allp_ref.at[4 * px + 2 * py + pc]
            _remote(row, row, psend.at[m], precv.at[m], (px, py, pc)).wait_recv()
        for cp in sends:
            cp.wait_send()
        local.wait()

    return pl.pallas_call(
        body, name="sibling_share", in_specs=[ANY] * (ng + 1), out_specs=[ANY] * (ng + 1),
        out_shape=[jax.ShapeDtypeStruct(h.shape, F32) for h in halves]
        + [jax.ShapeDtypeStruct((8,) + pack.shape, F32)],
        scratch_shapes=[pltpu.SemaphoreType.DMA((ng,)), pltpu.SemaphoreType.DMA((ng,)),
                        pltpu.SemaphoreType.DMA((7,)), pltpu.SemaphoreType.DMA((7,)), pltpu.SemaphoreType.DMA],
    )(*halves, pack)


def _pair_add(own, recv, cidx, name):
    R = own.shape[2]
    tr = min(256, R)

    def body(c_ref, a_ref, b_ref, o_ref):
        o_ref[...] = (a_ref[...] + b_ref[...]).astype(BF16)

    return pl.pallas_call(
        body, name=name,
        grid_spec=pltpu.PrefetchScalarGridSpec(
            num_scalar_prefetch=1, grid=(NPROJ, R // tr),
            in_specs=[pl.BlockSpec((None, None, tr, D), lambda n, r, c: (n, c[0], r, 0)),
                      pl.BlockSpec((None, tr, D), lambda n, r, c: (n, r, 0))],
            out_specs=pl.BlockSpec((None, tr, D), lambda n, r, c: (n, r, 0))),
        out_shape=jax.ShapeDtypeStruct(recv.shape, BF16),
        compiler_params=_cparams(("parallel", "parallel")),
    )(cidx, own, recv)


def _chip_sum(sums, parts, bidx, name):
    R = parts.shape[1]
    tr = min(256, R)

    def body(b_ref, s_ref, p_ref, o_ref):
        acc = None
        for j in range(NPROJ):
            term = jnp.where(b_ref[0] == j, s_ref[...], p_ref[j]).astype(F32)
            acc = term if acc is None else acc + term
        o_ref[...] = acc

    return pl.pallas_call(
        body, name=name,
        grid_spec=pltpu.PrefetchScalarGridSpec(
            num_scalar_prefetch=1, grid=(R // tr,),
            in_specs=[pl.BlockSpec((None, tr, D), lambda r, b: (b[0], r, 0)),
                      pl.BlockSpec((NPROJ, tr, D), lambda r, b: (0, r, 0))],
            out_specs=pl.BlockSpec((tr, D), lambda r, b: (r, 0))),
        out_shape=jax.ShapeDtypeStruct((R, D), F32),
        compiler_params=_cparams(("parallel",)),
    )(bidx, sums, parts)


def _adamw_math(w, g, m, v):
    m = ADAM_B1 * m + (1.0 - ADAM_B1) * g
    v = ADAM_B2 * v + (1.0 - ADAM_B2) * (g * g)
    m_hat = m / (1.0 - ADAM_B1 ** ADAM_STEP)
    v_hat = v / (1.0 - ADAM_B2 ** ADAM_STEP)
    delta = -ADAM_LR * (m_hat / (jnp.sqrt(v_hat) + ADAM_EPS) + ADAM_WD * w)
    return delta, m, v


def _adamw(w, mine, theirs, m, v, cidx, name):
    R = mine.shape[0]
    tr = min(256, R)
    nr = R // tr

    def body(c_ref, w_ref, a_ref, b_ref, m_ref, v_ref, g_ref, d_ref, nm_ref, nv_ref):
        g = jnp.where(pl.program_id(0) == c_ref[0], a_ref[...], b_ref[...])
        g_ref[...] = g
        d_ref[...], nm_ref[...], nv_ref[...] = _adamw_math(w_ref[...], g, m_ref[...], v_ref[...])

    full = pl.BlockSpec((tr, D), lambda h, r, c: (h * nr + r, 0))
    half = pl.BlockSpec((tr, D), lambda h, r, c: (r, 0))
    return pl.pallas_call(
        body, name=name,
        grid_spec=pltpu.PrefetchScalarGridSpec(
            num_scalar_prefetch=1, grid=(2, nr),
            in_specs=[full, half, half, full, full], out_specs=[full] * 4),
        out_shape=[jax.ShapeDtypeStruct(w.shape, F32)] * 4,
        compiler_params=_cparams(("parallel", "parallel")),
    )(cidx, w, mine, theirs, m, v)


PACK_ROWS = 8


def _small_update(allp, bidx, logits, weights, moments_m, moments_v):
    shapes = [w.shape for w in weights]
    q4 = D // NPROJ

    def body(b_ref, allp_ref, hgp_ref, lg_ref, *refs):
        w_refs, m_refs, v_refs = refs[0:6], refs[6:12], refs[12:18]
        loss_ref = refs[18]
        g_out, d_out, m_out, v_out = refs[19:25], refs[25:31], refs[31:37], refs[37:43]

        def total(ref, row, lo, hi):
            acc = ref[0, row:row + 1, lo:hi]
            for dev in range(1, 8):
                acc = acc + ref[dev, row:row + 1, lo:hi]
            return acc

        _, pp = _lower_bound(lg_ref[...])
        dlb = total(allp_ref, 2, 0, D)
        grads = [total(allp_ref, 0, 0, D), total(allp_ref, 4, 0, HD), total(allp_ref, 4, HD, 2 * HD),
                 total(hgp_ref, 1, 0, q4), total(allp_ref, 4, 2 * HD, 3 * HD), None]
        loss_ref[...] = (0.5 / D) * jnp.sum(total(allp_ref, 3, 0, D), axis=1, keepdims=True)
        for t in range(6):
            if t < 5:
                rows = [(slice(None), grads[t])]
            else:
                rows = [(slice(0, 1), -pp * dlb), (slice(1, 2), pp * dlb)]
            for rs, g in rows:
                g_out[t][rs, :] = g
                d_out[t][rs, :], m_out[t][rs, :], v_out[t][rs, :] = _adamw_math(
                    w_refs[t][rs, :], g, m_refs[t][rs, :], v_refs[t][rs, :])

    whole = [pl.BlockSpec(s, lambda i, b: (0, 0)) for s in shapes]
    return pl.pallas_call(
        body, name="small_update",
        grid_spec=pltpu.PrefetchScalarGridSpec(
            num_scalar_prefetch=1, grid=(1,),
            in_specs=[pl.BlockSpec((8, PACK_ROWS, D), lambda i, b: (0, 0, 0)),
                      pl.BlockSpec((8, PACK_ROWS, q4), lambda i, b: (0, 0, b[0])),
                      pl.BlockSpec((2, D), lambda i, b: (0, 0))] + whole * 3,
            out_specs=[pl.BlockSpec((1, 1), lambda i, b: (0, 0))] + whole * 4),
        out_shape=[jax.ShapeDtypeStruct((1, 1), F32)] + [jax.ShapeDtypeStruct(s, F32) for s in shapes] * 4,
        compiler_params=_cparams(("arbitrary",)),
    )(bidx, allp, allp, logits, *weights, *moments_m, *moments_v)


def kernel(x, sb_norm, sb_w_in, sb_q_gain, sb_k_gain, sb_w_out, hg_norm, hg_w_in, hg_o_gain, hg_w_out, hg_lb_logits, loss_target, m_sb_norm, m_sb_w_in, m_sb_q_gain, m_sb_k_gain, m_sb_w_out, m_hg_norm, m_hg_w_in, m_hg_o_gain, m_hg_w_out, m_hg_lb_logits, v_sb_norm, v_sb_w_in, v_sb_q_gain, v_sb_k_gain, v_sb_w_out, v_hg_norm, v_hg_w_in, v_hg_o_gain, v_hg_w_out, v_hg_lb_logits):
    Bl, S, _ = x.shape
    T = Bl * S
    cidx = lax.axis_index("c").astype(jnp.int32).reshape(1)
    bidx = (2 * lax.axis_index("x") + lax.axis_index("y")).astype(jnp.int32).reshape(1)

    def in_hbm(arrays):
        return [pltpu.with_memory_space_constraint(a, pltpu.HBM) for a in arrays]

    def own_slot(gathered, mine):
        return lax.dynamic_update_slice(gathered, mine[None], (bidx[0],) + (0,) * mine.ndim)

    def halved(g):
        return g.reshape(NPROJ, 2, g.shape[-2] * g.shape[0] // (2 * NPROJ), D)

    def heads(a):
        return a.reshape(a.shape[:-2] + (Bl, S, D))

    def flat(a):
        return a.reshape(a.shape[:-3] + (T, D))

    wp_sb = jnp.concatenate([sb_w_in[0], sb_w_out[0]], axis=0).astype(BF16)
    wp_hg = jnp.concatenate([hg_w_in[0], hg_w_out[0]], axis=0).astype(BF16)
    wall_sb, = in_hbm([own_slot(_gather_weights(wp_sb), wp_sb)])
    x2 = x.reshape(T, D)
    tgt = loss_target.reshape(T, D)

    proj0 = _in_proj_fwd(x2, sb_norm, wall_sb, W_IN, "sb_in_fwd")
    o0, ctot, wall_hg, hnall = _sb_fwd(heads(proj0), sb_q_gain, sb_k_gain, wp_hg, hg_norm)
    wall_hg, = in_hbm([own_slot(wall_hg, wp_hg)])
    hgn = own_slot(hnall, hg_norm).reshape(1, D)
    h1 = _out_proj_fwd(flat(o0), proj0, x2, wall_sb, W_OUT, "sb_out_fwd")
    proj1 = _in_proj_fwd(h1, hgn, wall_hg, W_IN, "hg_in_fwd")
    o1, states = _hg_fwd(heads(proj1), hg_lb_logits)
    dh2, loss_terms = _out_proj_fwd(flat(o1), proj1, h1, wall_hg, W_OUT, "hg_out_fwd",
                                    o_gain=hg_o_gain, target=tgt)

    do1, dproj1, gout_hg, d_ogain = _out_proj_bwd(dh2, flat(o1), proj1, wall_hg, W_OUT, "hg_out_bwd",
                                                  o_gain=hg_o_gain)
    dproj1, dlb = _hg_bwd(heads(proj1), states, heads(do1), heads(dproj1), hg_lb_logits)
    dproj1 = flat(dproj1)
    dh1, d_hgn = _in_proj_bwd_x(dproj1, wall_hg, W_IN, h1, hgn, dh2, "hg_in_bwd_x")
    big_hg = [halved(_in_proj_bwd_w(dproj1, h1, hgn, "hg_in_bwd_w")), halved(gout_hg)]
    do0, dproj0, gout_sb, *recv_hg = _out_proj_bwd(dh1, flat(o0), proj0, wall_sb, W_OUT, "sb_out_bwd",
                                                   exchange=big_hg)
    sums_hg = in_hbm(_pair_add(g, r, cidx, "pair_add_" + nm)
                     for g, r, nm in zip(big_hg, in_hbm(recv_hg), ("hg_in", "hg_out")))
    dproj0, d_qg, d_kg, *parts_hg = _sb_bwd(heads(proj0), ctot, heads(do0), heads(dproj0),
                                            sb_q_gain, sb_k_gain, sums_hg)
    dproj0 = flat(dproj0)

    big_sb = [halved(_in_proj_bwd_w(dproj0, x2, sb_norm, "sb_in_bwd_w")), halved(gout_sb)]
    sums_sb = in_hbm(_pair_add(g, r, cidx, "pair_add_" + nm)
                     for g, r, nm in zip(big_sb, in_hbm(_pair_exchange(big_sb)), ("sb_in", "sb_out")))
    grad_x, d_sbn, *parts_sb = _in_proj_bwd_x(dproj0, wall_sb, W_IN, x2, sb_norm, dh1, "sb_in_bwd_x",
                                              exchange=sums_sb)

    names = ["sb_in", "hg_in", "sb_out", "hg_out"]
    sums = [sums_sb[0], sums_hg[0], sums_sb[1], sums_hg[1]]
    parts = [parts_sb[0], parts_hg[0], parts_sb[1], parts_hg[1]]
    halves = in_hbm(_chip_sum(sm, p, bidx, "chip_sum_" + nm) for sm, p, nm in zip(sums, in_hbm(parts), names))
    gains = jnp.concatenate([d_qg, d_kg, d_ogain, jnp.zeros((1, D - 3 * HD), F32)], axis=1)
    pack = jnp.concatenate([d_sbn, d_hgn, dlb, loss_terms, gains, jnp.zeros((3, D), F32)], axis=0)
    *theirs, allp = _sibling_share(halves, pack)
    theirs = in_hbm(theirs)

    big_w = [sb_w_in, hg_w_in, sb_w_out, hg_w_out]
    big_m = [m_sb_w_in, m_hg_w_in, m_sb_w_out, m_hg_w_out]
    big_v = [v_sb_w_in, v_hg_w_in, v_sb_w_out, v_hg_w_out]
    upd = [_adamw(w[0], a, b, m[0], v[0], cidx, "adamw_" + nm)
           for w, a, b, m, v, nm in zip(big_w, halves, theirs, big_m, big_v, names)]
    (g_sb_in, d_sb_in, nm_sb_in, nv_sb_in), (g_hg_in, d_hg_in, nm_hg_in, nv_hg_in), \
        (g_sb_out, d_sb_out, nm_sb_out, nv_sb_out), (g_hg_out, d_hg_out, nm_hg_out, nv_hg_out) = [
            tuple(a[None] for a in u) for u in upd]

    small = _small_update(
        allp, bidx, hg_lb_logits,
        [sb_norm, sb_q_gain, sb_k_gain, hg_norm, hg_o_gain, hg_lb_logits],
        [m_sb_norm, m_sb_q_gain, m_sb_k_gain, m_hg_norm, m_hg_o_gain, m_hg_lb_logits],
        [v_sb_norm, v_sb_q_gain, v_sb_k_gain, v_hg_norm, v_hg_o_gain, v_hg_lb_logits])
    loss = small[0].reshape(())
    (g_sbn, g_qg, g_kg, g_hgn, g_og, g_lb) = small[1:7]
    (d_sbn2, d_qg2, d_kg2, d_hgn2, d_og2, d_lb2) = small[7:13]
    (nm_sbn, nm_qg, nm_kg, nm_hgn, nm_og, nm_lb) = small[13:19]
    (nv_sbn, nv_qg, nv_kg, nv_hgn, nv_og, nv_lb) = small[19:25]

    return (loss, grad_x.reshape(Bl, S, D),
            g_sbn, g_sb_in, g_qg, g_kg, g_sb_out, g_hgn, g_hg_in, g_og, g_hg_out, g_lb,
            d_sbn2, d_sb_in, d_qg2, d_kg2, d_sb_out, d_hgn2, d_hg_in, d_og2, d_hg_out, d_lb2,
            nm_sbn, nm_sb_in, nm_qg, nm_kg, nm_sb_out, nm_hgn, nm_hg_in, nm_og, nm_hg_out, nm_lb,
            nv_sbn, nv_sb_in, nv_qg, nv_kg, nv_sb_out, nv_hgn, nv_hg_in, nv_og, nv_hg_out, nv_lb)
```

```python
import functools

import jax
import jax.numpy as jnp
from jax import lax
from jax.experimental import pallas as pl
from jax.experimental.pallas import tpu as pltpu

F32 = jnp.float32
BF16 = jnp.bfloat16
MESH = pl.DeviceIdType.MESH
ANY = pl.BlockSpec(memory_space=pl.ANY)

D = 1024
HEADS = 8
HD = 128
NPROJ = 4
RMS_EPS = 1e-6
TK = 256
CH = 64
CH_LOG2 = 6
GR = 128
SCALE = HD ** -0.5
EXP_CLAMP = 60.0
W_IN, W_OUT = 0, 4

ADAM_LR = 0.001
ADAM_B1 = 0.9
ADAM_B2 = 0.999
ADAM_EPS = 1e-08
ADAM_WD = 0.01
ADAM_STEP = 10

NT = (((1,), (1,)), ((), ()))
TN = (((0,), (0,)), ((), ()))
MIB = 1024 * 1024


def _cparams(sem=None, vmem_mib=40):
    return pltpu.CompilerParams(dimension_semantics=sem, vmem_limit_bytes=vmem_mib * MIB)


def _dot(a, b, dims=None):
    if dims is None:
        return jnp.dot(a, b, preferred_element_type=F32)
    return lax.dot_general(a, b, dims, preferred_element_type=F32)


def _sigmoid(x):
    return 1.0 / (1.0 + jnp.exp(-x))


def _rms(x):
    return lax.rsqrt(jnp.mean(x * x, axis=-1, keepdims=True) + RMS_EPS)


def _rms_bwd(x, r, gain, dy):
    a = dy * gain
    dx = r * a - x * (r * r * r) * jnp.mean(x * a, axis=-1, keepdims=True)
    return dx, dy * (x * r)


def _split2(v):
    hi = v.astype(BF16)
    lo = (v - hi.astype(F32)).astype(BF16)
    return hi, lo


def _cum2(v, u):
    hi, lo = _split2(v)
    return _dot(hi, u) + _dot(lo, u)


def _dot3(a, b, dims=None):
    ah, al = _split2(a)
    bh, bl = _split2(b)
    return _dot(ah, bh, dims) + _dot(ah, bl, dims) + _dot(al, bh, dims)


def _cum2l(u, v):
    hi, lo = _split2(v)
    return _dot(u, hi) + _dot(u, lo)


def _in_proj_fwd(h, gain, wall, wblk, name):
    T = h.shape[0]
    tm = min(1024, T)

    def body(h_ref, g_ref, w_ref, o_ref, u_s):
        rows = pl.ds(pl.multiple_of(pl.program_id(1) * tm, tm), tm)

        @pl.when(pl.program_id(0) == 0)
        def _():
            x = h_ref[...]
            u_s[rows, :] = (x * _rms(x) * g_ref[...]).astype(BF16)

        o_ref[...] = _dot(u_s[rows, :], w_ref[...])

    return pl.pallas_call(
        body, name=name, grid=(NPROJ, T // tm),
        in_specs=[pl.BlockSpec((tm, D), lambda n, i: (jnp.where(n == 0, i, 0), 0)),
                  pl.BlockSpec((1, D), lambda n, i: (0, 0)),
                  pl.BlockSpec((None, D, D), lambda n, i: (n, wblk, 0))],
        out_specs=pl.BlockSpec((None, tm, D), lambda n, i: (n, i, 0)),
        out_shape=jax.ShapeDtypeStruct((NPROJ, T, D), F32),
        scratch_shapes=[pltpu.VMEM((T, D), BF16)],
        compiler_params=_cparams(("arbitrary", "arbitrary")),
    )(h, gain, wall)


def _head_norm(x):
    outs = []
    for hh in range(x.shape[1] // HD):
        xs = x[:, hh * HD:(hh + 1) * HD]
        outs.append((xs, _rms(xs)))
    return outs


def _w_out_specs(wblk):
    kb = D // NPROJ
    return [pl.BlockSpec((None, kb, D), functools.partial(lambda j, i: (j, wblk, 0), j)) for j in range(NPROJ)]


def _out_proj_fwd(o, proj, resid, wall, wblk, name, o_gain=None, target=None):
    T = o.shape[0]
    tm = min(512, T)
    kb = D // NPROJ
    with_loss = target is not None

    def body(*refs):
        o_ref, g_ref, r_ref = refs[:3]
        w_refs = refs[3:3 + NPROJ]
        if with_loss:
            og_ref, t_ref, dh_ref, ls_ref = refs[3 + NPROJ:]
        else:
            h_ref, = refs[3 + NPROJ:]
        x = o_ref[...]
        if with_loss:
            x = jnp.concatenate([xs * r * og_ref[...] for xs, r in _head_norm(x)], axis=1)
        g = g_ref[...]
        a = (x * (g * _sigmoid(g))).astype(BF16)
        hnew = r_ref[...]
        for j in range(NPROJ):
            hnew = hnew + _dot(a[:, j * kb:(j + 1) * kb], w_refs[j][...])
        if with_loss:
            err = hnew - t_ref[...]
            dh_ref[...] = err * (1.0 / D)
            part = jnp.sum(err * err, axis=0, keepdims=True)

            @pl.when(pl.program_id(0) == 0)
            def _():
                ls_ref[...] = part

            @pl.when(pl.program_id(0) != 0)
            def _():
                ls_ref[...] += part
        else:
            h_ref[...] = hnew

    tile = pl.BlockSpec((tm, D), lambda i: (i, 0))
    in_specs = [tile, pl.BlockSpec((None, tm, D), lambda i: (3, i, 0)), tile] + _w_out_specs(wblk)
    args = [o, proj, resid] + [wall] * NPROJ
    out_specs = tile
    out_shape = jax.ShapeDtypeStruct((T, D), F32)
    if with_loss:
        in_specs += [pl.BlockSpec((1, HD), lambda i: (0, 0)), tile]
        args += [o_gain, target]
        out_specs = [tile, pl.BlockSpec((1, D), lambda i: (0, 0))]
        out_shape = [out_shape, jax.ShapeDtypeStruct((1, D), F32)]
    return pl.pallas_call(
        body, name=name, grid=(T // tm,), in_specs=in_specs, out_specs=out_specs,
        out_shape=out_shape, compiler_params=_cparams(("arbitrary",)),
    )(*args)


def _out_proj_bwd(dy, o, proj, wall, wblk, name, o_gain=None, exchange=()):
    T = o.shape[0]
    tm = min(512, T)
    kb = D // NPROJ
    normed = o_gain is not None
    ne = len(exchange)

    def body(*refs):
        it = iter(refs)
        dy_ref, o_ref, g_ref = (next(it) for _ in range(3))
        w_refs = [next(it) for _ in range(NPROJ)]
        og_ref = next(it) if normed else None
        xg_refs = [next(it) for _ in range(ne)]
        do_ref, dg_ref, dw_ref = (next(it) for _ in range(3))
        dgain_ref = next(it) if normed else None
        xr_refs = [next(it) for _ in range(ne)]
        wt_s = next(it)
        first = pl.program_id(0) == 0
        if ne:
            start, finish = _pair_ops(xg_refs, xr_refs, next(it), next(it))
            pl.when(first)(start)

        @pl.when(first)
        def _():
            for j in range(NPROJ):
                wt_s[:, j * kb:(j + 1) * kb] = w_refs[j][...].T
        g = g_ref[...]
        s = _sigmoid(g)
        sl = g * s
        x = o_ref[...]
        if normed:
            heads = _head_norm(x)
            on = jnp.concatenate([xs * r * og_ref[...] for xs, r in heads], axis=1)
        else:
            on = x
        dyb = dy_ref[...].astype(BF16)
        a = (on * sl).astype(BF16)
        for j in range(NPROJ):
            part = _dot(a[:, j * kb:(j + 1) * kb], dyb, TN)

            @pl.when(first)
            def _():
                dw_ref[j] = part

            @pl.when(jnp.logical_not(first))
            def _():
                dw_ref[j] += part

        da = _dot(dyb, wt_s[...])
        d_on = da * sl
        dg_ref[...] = (da * on * (s * (1.0 + g * (1.0 - s)))).astype(BF16)
        if normed:
            dxs, gsum = [], None
            for hh, (xs, r) in enumerate(heads):
                dx, gt = _rms_bwd(xs, r, og_ref[...], d_on[:, hh * HD:(hh + 1) * HD])
                dxs.append(dx)
                gt = jnp.sum(gt, axis=0, keepdims=True)
                gsum = gt if gsum is None else gsum + gt
            do_ref[...] = jnp.concatenate(dxs, axis=1).astype(BF16)

            @pl.when(first)
            def _():
                dgain_ref[...] = gsum

            @pl.when(jnp.logical_not(first))
            def _():
                dgain_ref[...] += gsum
        else:
            do_ref[...] = d_on.astype(BF16)
        if ne:
            pl.when(pl.program_id(0) == T // tm - 1)(finish)

    tile = pl.BlockSpec((tm, D), lambda i: (i, 0))
    gate = pl.BlockSpec((None, tm, D), lambda i: (3, i, 0))
    in_specs = [tile, tile, gate] + _w_out_specs(wblk)
    args = [dy, o, proj] + [wall] * NPROJ
    out_specs = [tile, gate, pl.BlockSpec((NPROJ, kb, D), lambda i: (0, 0, 0))]
    out_shape = [jax.ShapeDtypeStruct((T, D), BF16),
                 jax.ShapeDtypeStruct((NPROJ, T, D), BF16),
                 jax.ShapeDtypeStruct((NPROJ, kb, D), F32)]
    if normed:
        in_specs.append(pl.BlockSpec((1, HD), lambda i: (0, 0)))
        args.append(o_gain)
        out_specs.append(pl.BlockSpec((1, HD), lambda i: (0, 0)))
        out_shape.append(jax.ShapeDtypeStruct((1, HD), F32))
    x_shape, x_sems = _pair_shapes(exchange) if ne else ([], [])
    return pl.pallas_call(
        body, name=name, grid=(T // tm,), in_specs=in_specs + [ANY] * ne, out_specs=out_specs + [ANY] * ne,
        out_shape=out_shape + x_shape, scratch_shapes=[pltpu.VMEM((D, D), BF16)] + x_sems,
        compiler_params=_cparams(("arbitrary",), vmem_mib=48),
    )(*args, *exchange)


def _in_proj_bwd_x(dproj, wall, wblk, h, gain, dres, name, exchange=()):
    T = h.shape[0]
    tm = min(512, T)
    ne = len(exchange)

    def body(d_ref, w_ref, h_ref, g_ref, r_ref, *refs):
        xs_refs, (dh_ref, dgain_ref), xr_refs = refs[:ne], refs[ne:ne + 2], refs[ne + 2:2 * ne + 2]
        du, wt_s = refs[2 * ne + 2:2 * ne + 4]
        i, n = pl.program_id(0), pl.program_id(1)
        if ne:
            start, finish = _chip_ops(xs_refs, xr_refs, *refs[2 * ne + 4:])
            pl.when(jnp.logical_and(i == 0, n == 0))(start)

        @pl.when(i == 0)
        def _():
            wt_s[n] = w_ref[...].T

        part = _dot(d_ref[...], wt_s[n])

        @pl.when(n == 0)
        def _():
            du[...] = part

        @pl.when(n != 0)
        def _():
            du[...] += part

        @pl.when(n == NPROJ - 1)
        def _():
            x = h_ref[...]
            dx, gt = _rms_bwd(x, _rms(x), g_ref[...], du[...])
            dh_ref[...] = r_ref[...] + dx
            gt = jnp.sum(gt, axis=0, keepdims=True)

            @pl.when(i == 0)
            def _():
                dgain_ref[...] = gt

            @pl.when(i != 0)
            def _():
                dgain_ref[...] += gt

        if ne:
            pl.when(jnp.logical_and(i == T // tm - 1, n == NPROJ - 1))(finish)

    x_shape, x_sems = _chip_shapes(exchange) if ne else ([], [])
    return pl.pallas_call(
        body, name=name, grid=(T // tm, NPROJ),
        in_specs=[pl.BlockSpec((None, tm, D), lambda i, n: (n, i, 0)),
                  pl.BlockSpec((None, D, D), lambda i, n: (jnp.where(i == 0, n, NPROJ - 1), wblk, 0)),
                  pl.BlockSpec((tm, D), lambda i, n: (i, 0)),
                  pl.BlockSpec((1, D), lambda i, n: (0, 0)),
                  pl.BlockSpec((tm, D), lambda i, n: (i, 0))] + [ANY] * ne,
        out_specs=[pl.BlockSpec((tm, D), lambda i, n: (i, 0)),
                   pl.BlockSpec((1, D), lambda i, n: (0, 0))] + [ANY] * ne,
        out_shape=[jax.ShapeDtypeStruct((T, D), F32), jax.ShapeDtypeStruct((1, D), F32)] + x_shape,
        scratch_shapes=[pltpu.VMEM((tm, D), F32), pltpu.VMEM((NPROJ, D, D), BF16)] + x_sems,
        compiler_params=_cparams(("arbitrary", "arbitrary")),
    )(dproj, wall, h, gain, dres, *exchange)


def _in_proj_bwd_w(dproj, h, gain, name):
    T = h.shape[0]
    tk = min(1024, T)

    def body(d_ref, h_ref, g_ref, dw_ref, ut_s):
        k = pl.program_id(1)

        @pl.when(pl.program_id(0) == 0)
        def _():
            x = h_ref[...]
            ut_s[k] = (x * _rms(x) * g_ref[...]).astype(BF16).T

        part = _dot(ut_s[k], d_ref[...])

        @pl.when(k == 0)
        def _():
            dw_ref[...] = part

        @pl.when(k != 0)
        def _():
            dw_ref[...] += part

    return pl.pallas_call(
        body, name=name, grid=(NPROJ, T // tk),
        in_specs=[pl.BlockSpec((None, tk, D), lambda n, k: (n, k, 0)),
                  pl.BlockSpec((tk, D), lambda n, k: (jnp.where(n == 0, k, 0), 0)),
                  pl.BlockSpec((1, D), lambda n, k: (0, 0))],
        out_specs=pl.BlockSpec((None, D, D), lambda n, k: (n, 0, 0)),
        out_shape=jax.ShapeDtypeStruct((NPROJ, D, D), F32),
        scratch_shapes=[pltpu.VMEM((T // tk, D, tk), BF16)],
        compiler_params=_cparams(("arbitrary", "arbitrary")),
    )(dproj, h, gain)


def _log_sigmoid_pair(z):
    lb = jnp.minimum(z, 0.0) - jnp.log(1.0 + jnp.exp(-jnp.abs(z)))
    return lb, lb - z


def _slab_consts():
    t = lax.broadcasted_iota(jnp.int32, (TK, TK), 0)
    s = lax.broadcasted_iota(jnp.int32, (TK, TK), 1)
    return s < t, (t > s).astype(BF16), (t < s).astype(BF16)


def _slab_rows(k0, S):
    return [(r0, r1, masked) for r0, r1, masked in ((k0, k0 + TK, True), (k0 + TK, S, False)) if r0 < r1]


def _sb_fwd(proj, q_gain, k_gain, wp, hn):
    _, Bl, S, _ = proj.shape
    steps = Bl * HEADS

    def body(q_ref, k_ref, v_ref, qg_ref, kg_ref, wp_ref, hn_ref, o_ref, ct_ref, wall_ref, hnall_ref,
             qn, kn, vb, ssem, rsem):
        step = pl.program_id(0) * HEADS + pl.program_id(1)
        start, forward, finish = _gather_ops(wp_ref, wall_ref, ssem, rsem, hn_ref, hnall_ref)
        pl.when(step == 0)(start)
        pl.when(step == steps // 2)(forward)
        q = q_ref[...]
        qn[...] = (q * _rms(q) * (qg_ref[...] * SCALE)).astype(BF16)
        k = k_ref[...]
        kn[...] = (k * _rms(k) * kg_ref[...]).astype(BF16)
        vb[...] = v_ref[...].astype(BF16)
        tri, u_gt, _ = _slab_consts()
        nb = S // TK
        c_blk = [jnp.zeros((TK, 1), F32)] * nb
        o_blk = [jnp.zeros((TK, HD), F32)] * nb
        for k0 in reversed(range(0, S, TK)):
            kb, vbb = kn[k0:k0 + TK, :], vb[k0:k0 + TK, :]
            for r0, r1, masked in _slab_rows(k0, S):
                blocks = range(r0 // TK, r1 // TK)
                z = _dot(qn[r0:r1, :], kb, NT)
                lb, ls = _log_sigmoid_pair(z)
                if masked:
                    ls = jnp.where(tri, ls, 0.0)
                c = jnp.concatenate([c_blk[b] for b in blocks], axis=0)
                w = jnp.exp(lb + _cum2(ls, u_gt) + c)
                if masked:
                    w = jnp.where(tri, w, 0.0)
                o_new = _dot(w.astype(BF16), vbb)
                c_new = jnp.sum(ls, axis=1, keepdims=True)
                for i, b in enumerate(blocks):
                    o_blk[b] = o_blk[b] + o_new[i * TK:(i + 1) * TK]
                    c_blk[b] = c_blk[b] + c_new[i * TK:(i + 1) * TK]
        o_ref[...] = jnp.concatenate(o_blk, axis=0)
        ct_ref[...] = jnp.concatenate(c_blk, axis=0)
        pl.when(step == steps - 1)(finish)

    def slot(n):
        return pl.BlockSpec((None, None, S, HD), lambda b, h: (n, b, 0, h))

    return pl.pallas_call(
        body, name="sb_fwd", grid=(Bl, HEADS),
        in_specs=[slot(0), slot(1), slot(2),
                  pl.BlockSpec((1, HD), lambda b, h: (0, 0)),
                  pl.BlockSpec((1, HD), lambda b, h: (0, 0)), ANY, ANY],
        out_specs=[pl.BlockSpec((None, S, HD), lambda b, h: (b, 0, h)),
                   pl.BlockSpec((None, None, S, 1), lambda b, h: (b, h, 0, 0)), ANY, ANY],
        out_shape=[jax.ShapeDtypeStruct((Bl, S, D), F32),
                   jax.ShapeDtypeStruct((Bl, HEADS, S, 1), F32),
                   jax.ShapeDtypeStruct((NPROJ,) + wp.shape, BF16),
                   jax.ShapeDtypeStruct((NPROJ,) + hn.shape, F32)],
        scratch_shapes=[pltpu.VMEM((S, HD), BF16)] * 3 + [pltpu.SemaphoreType.DMA((GATHER_SEMS,))] * 2,
        compiler_params=_cparams(("arbitrary", "arbitrary"), vmem_mib=56),
    )(proj, proj, proj, q_gain, k_gain, wp, hn)


def _sb_bwd(proj, ctot, do, dproj, q_gain, k_gain, exchange):
    _, Bl, S, _ = proj.shape
    ne = len(exchange)

    def body(q_ref, k_ref, v_ref, ct_ref, do_ref, qg_ref, kg_ref, _, *refs):
        xs_refs, (dqkv_ref, dqg_ref, dkg_ref), xr_refs = refs[:ne], refs[ne:ne + 3], refs[ne + 3:2 * ne + 3]
        qn, kn, vb, dqn, dkn, dvn, ssem, rsem = refs[2 * ne + 3:]
        step = pl.program_id(0) * HEADS + pl.program_id(1)
        first = step == 0
        start, finish = _chip_ops(xs_refs, xr_refs, ssem, rsem)

        @pl.when(first)
        def _():
            start()
            dqg_ref[...] = jnp.zeros_like(dqg_ref)
            dkg_ref[...] = jnp.zeros_like(dkg_ref)

        q = q_ref[...]
        rq = _rms(q)
        qn[...] = (q * rq * (qg_ref[...] * SCALE)).astype(BF16)
        k = k_ref[...]
        rk = _rms(k)
        kn[...] = (k * rk * kg_ref[...]).astype(BF16)
        vb[...] = v_ref[...].astype(BF16)
        for acc in (dqn, dkn, dvn):
            acc[...] = jnp.zeros_like(acc)
        tri, u_gt, u_lt = _slab_consts()
        nb = S // TK
        passed, e = [jnp.zeros((TK, 1), F32)] * nb, [jnp.zeros((TK, 1), F32)] * nb
        for k0 in range(0, S, TK):
            keys = slice(k0, k0 + TK)
            kb, vbb = kn[keys, :], vb[keys, :]
            for r0, r1, masked in _slab_rows(k0, S):
                rows, blocks = slice(r0, r1), range(r0 // TK, r1 // TK)
                qb, dobb = qn[rows, :], do_ref[rows, :]
                z = _dot(qb, kb, NT)
                lb, ls = _log_sigmoid_pair(z)
                if masked:
                    ls = jnp.where(tri, ls, 0.0)
                p_new = jnp.concatenate([passed[b] for b in blocks], axis=0) + jnp.sum(ls, axis=1, keepdims=True)
                w = jnp.exp(lb + _cum2(ls, u_gt) + (ct_ref[rows, :] - p_new))
                if masked:
                    w = jnp.where(tri, w, 0.0)
                de = w * _dot(dobb, vbb, NT)
                e_old = jnp.concatenate([e[b] for b in blocks], axis=0)
                dls = e_old + _cum2(de, u_lt)
                e_new = e_old + jnp.sum(de, axis=1, keepdims=True)
                for i, b in enumerate(blocks):
                    passed[b], e[b] = p_new[i * TK:(i + 1) * TK], e_new[i * TK:(i + 1) * TK]
                sg = jnp.exp(lb)
                dz = de - sg * (de + dls)
                if masked:
                    dz = jnp.where(tri, dz, 0.0)
                dzb = dz.astype(BF16)
                dqn[rows, :] += _dot(dzb, kb)
                dkn[keys, :] += _dot(dzb, qb, TN)
                dvn[keys, :] += _dot(w.astype(BF16), dobb, TN)

        dx, gt = _rms_bwd(q, rq, qg_ref[...], dqn[...] * SCALE)
        dqkv_ref[0] = dx.astype(BF16)
        dqg_ref[...] += jnp.sum(gt, axis=0, keepdims=True)
        dx, gt = _rms_bwd(k, rk, kg_ref[...], dkn[...])
        dqkv_ref[1] = dx.astype(BF16)
        dkg_ref[...] += jnp.sum(gt, axis=0, keepdims=True)
        dqkv_ref[2] = dvn[...].astype(BF16)
        pl.when(step == Bl * HEADS - 1)(finish)

    def slot(n):
        return pl.BlockSpec((None, None, S, HD), lambda b, h: (n, b, 0, h))

    head = pl.BlockSpec((None, S, HD), lambda b, h: (b, 0, h))
    gain = pl.BlockSpec((1, HD), lambda b, h: (0, 0))
    x_shape, x_sems = _chip_shapes(exchange)
    return pl.pallas_call(
        body, name="sb_bwd", grid=(Bl, HEADS),
        in_specs=[slot(0), slot(1), slot(2),
                  pl.BlockSpec((None, None, S, 1), lambda b, h: (b, h, 0, 0)), head, gain, gain, ANY] + [ANY] * ne,
        out_specs=[pl.BlockSpec((3, None, S, HD), lambda b, h: (0, b, 0, h)), gain, gain] + [ANY] * ne,
        out_shape=[jax.ShapeDtypeStruct(dproj.shape, dproj.dtype),
                   jax.ShapeDtypeStruct((1, HD), F32), jax.ShapeDtypeStruct((1, HD), F32)] + x_shape,
        scratch_shapes=[pltpu.VMEM((S, HD), BF16)] * 3 + [pltpu.VMEM((S, HD), F32)] * 3 + x_sems,
        input_output_aliases={7: 0},
        compiler_params=_cparams(("arbitrary", "arbitrary"), vmem_mib=56),
    )(proj, proj, proj, ctot, do, q_gain, k_gain, dproj, *exchange)


def _lower_bound(logits):
    l0, l1 = logits[0:1, :], logits[1:2, :]
    m = jnp.maximum(l0, l1)
    e0, e1 = jnp.exp(l0 - m), jnp.exp(l1 - m)
    p0, p1 = e0 / (e0 + e1), e1 / (e0 + e1)
    return (p0 + p1) - p0, p0 * p1


def _hg_gates(qr, fp, lbv):
    sq = _sigmoid(qr)
    sp = _sigmoid(fp)
    sn = 1.0 / (1.0 + jnp.exp(fp))
    f = lbv + (1.0 - lbv) * sp
    return qr * sq, sq, sp, sn, f, (1.0 - lbv) * sn


def _group_consts():
    t = lax.broadcasted_iota(jnp.int32, (GR, GR), 0)
    j = lax.broadcasted_iota(jnp.int32, (GR, GR), 1)
    same = lax.shift_right_logical(t, CH_LOG2) == lax.shift_right_logical(j, CH_LOG2)
    tril = jnp.logical_and(same, j <= t)
    return (tril, tril.astype(BF16), jnp.logical_and(same, j >= t).astype(BF16), same.astype(BF16))


def _hg_decays(qa, k, f, t_inc, t_same):
    g = jnp.log(f)
    gc = _cum2l(t_inc, g)
    gl = _cum2l(t_same, g)
    gm = gc - 0.5 * gl
    e_q = jnp.exp(jnp.minimum(gm, EXP_CLAMP))
    e_k = jnp.exp(jnp.minimum(-gm, EXP_CLAMP))
    e_g = jnp.exp(gc)
    e_l = jnp.exp(gl - gc)
    return qa * e_q, k * e_k, qa * e_g, k * e_l, e_q, e_k, e_g, e_l, jnp.exp(gl)


def _hg_fwd(proj, lb_logits):
    _, Bl, S, _ = proj.shape
    nc = S // CH

    def body(q_ref, f_ref, i_ref, lg_ref, o_ref, st_ref, egl_s):
        lbv, _ = _lower_bound(lg_ref[...])
        tril, t_inc, _, t_same = _group_consts()
        st = jnp.zeros((HD, HD), F32)
        for g0 in range(0, S, GR):
            rs = slice(g0, g0 + GR)
            qa, _, _, _, f, k = _hg_gates(q_ref[rs, :], f_ref[rs, :], lbv)
            qt, kt, qg, kd, _, _, _, _, e_gl = _hg_decays(qa, k, f, t_inc, t_same)
            a = jnp.where(tril, _dot(qt.astype(BF16), kt.astype(BF16), NT), 0.0)
            ib, qgb, kdb = i_ref[rs, :].astype(BF16), qg.astype(BF16), kd.astype(BF16)
            within = _dot(a.astype(BF16), ib)
            egl_s[rs, :] = e_gl
            outs = []
            for l0 in range(0, GR, CH):
                ls = slice(l0, l0 + CH)
                st_ref[(g0 + l0) // CH] = st
                outs.append(within[ls] + _dot(qgb[ls], st.astype(BF16), NT))
                st = st * egl_s[g0 + l0:g0 + l0 + 1, :] + _dot(ib[ls], kdb[ls], TN)
            o_ref[rs, :] = jnp.concatenate(outs, axis=0)

    def slot(n):
        return pl.BlockSpec((None, None, S, HD), lambda b, h: (n, b, 0, h))

    return pl.pallas_call(
        body, name="hg_fwd", grid=(Bl, HEADS),
        in_specs=[slot(0), slot(1), slot(2), pl.BlockSpec((2, HD), lambda b, h: (0, h))],
        out_specs=[pl.BlockSpec((None, S, HD), lambda b, h: (b, 0, h)),
                   pl.BlockSpec((None, None, nc, HD, HD), lambda b, h: (b, h, 0, 0, 0))],
        out_shape=[jax.ShapeDtypeStruct((Bl, S, D), F32),
                   jax.ShapeDtypeStruct((Bl, HEADS, nc, HD, HD), F32)],
        scratch_shapes=[pltpu.VMEM((S, HD), F32)],
        compiler_params=_cparams(("parallel", "parallel")),
    )(proj, proj, proj, lb_logits)


def _hg_bwd(proj, states, do, dproj, lb_logits):
    _, Bl, S, _ = proj.shape

    def body(q_ref, f_ref, i_ref, st_ref, do_ref, lg_ref, _, dqfi_ref, dlb_ref, egl_s):
        lbv, _ = _lower_bound(lg_ref[...])
        tril, t_inc, t_dec, t_same = _group_consts()
        dst = jnp.zeros((HD, HD), F32)
        dlb = jnp.zeros((1, HD), F32)
        for g0 in reversed(range(0, S, GR)):
            rs = slice(g0, g0 + GR)
            qr, fp = q_ref[rs, :], f_ref[rs, :]
            qa, sq, sp, sn, f, k = _hg_gates(qr, fp, lbv)
            qt, kt, qg, kd, e_q, e_k, e_g, e_l, e_gl = _hg_decays(qa, k, f, t_inc, t_same)
            ib, dob, qgb, kdb = i_ref[rs, :].astype(BF16), do_ref[rs, :], qg.astype(BF16), kd.astype(BF16)
            egl_s[rs, :] = e_gl
            dqg, dkd, di, dse = [], [], [], []
            for l0 in reversed(range(0, GR, CH)):
                ls = slice(l0, l0 + CH)
                st = st_ref[(g0 + l0) // CH]
                dstb = dst.astype(BF16)
                dqg.insert(0, _dot(dob[ls], st.astype(BF16)))
                dkd.insert(0, _dot(ib[ls], dstb))
                di.insert(0, _dot(kdb[ls], dstb, NT))
                dse.insert(0, jnp.broadcast_to(jnp.sum(dst * st, axis=0, keepdims=True), (CH, HD)))
                dst = dst * egl_s[g0 + l0:g0 + l0 + 1, :] + _dot(dob[ls], qgb[ls], TN)
            dqg, dkd, di, dse = (jnp.concatenate(p, axis=0) for p in (dqg, dkd, di, dse))
            ab = jnp.where(tril, _dot(qt.astype(BF16), kt.astype(BF16), NT), 0.0).astype(BF16)
            da = jnp.where(tril, _dot(dob, ib, NT), 0.0)
            dqt = _dot3(da, kt)
            dkt = _dot3(da, qt, TN)
            dqfi_ref[2, rs, :] = (di + _dot(ab, dob, TN)).astype(BF16)
            dgc = dqt * qt - dkt * kt + dqg * qg - dkd * kd
            dg = _cum2l(t_dec, dgc) + _cum2l(t_same, dkd * kd) + dse * e_gl
            t1 = dg / f - (dkt * e_k + dkd * e_l)
            dqfi_ref[1, rs, :] = ((1.0 - lbv) * t1 * sp * sn).astype(BF16)
            dqfi_ref[0, rs, :] = ((dqt * e_q + dqg * e_g) * (sq * (1.0 + qr * (1.0 - sq)))).astype(BF16)
            dlb = dlb + jnp.sum(sn * t1, axis=0, keepdims=True)

        @pl.when(pl.program_id(1) == 0)
        def _():
            dlb_ref[...] = dlb

        @pl.when(pl.program_id(1) != 0)
        def _():
            dlb_ref[...] += dlb

    def slot(n):
        return pl.BlockSpec((None, None, S, HD), lambda h, b: (n, b, 0, h))

    return pl.pallas_call(
        body, name="hg_bwd", grid=(HEADS, Bl),
        in_specs=[slot(0), slot(1), slot(2),
                  pl.BlockSpec((None, None, S // CH, HD, HD), lambda h, b: (b, h, 0, 0, 0)),
                  pl.BlockSpec((None, S, HD), lambda h, b: (b, 0, h)),
                  pl.BlockSpec((2, HD), lambda h, b: (0, h)), ANY],
        out_specs=[pl.BlockSpec((3, None, S, HD), lambda h, b: (0, b, 0, h)),
                   pl.BlockSpec((1, HD), lambda h, b: (0, h))],
        out_shape=[jax.ShapeDtypeStruct(dproj.shape, dproj.dtype), jax.ShapeDtypeStruct((1, D), F32)],
        scratch_shapes=[pltpu.VMEM((S, HD), F32)],
        input_output_aliases={6: 0},
        compiler_params=_cparams(("parallel", "arbitrary")),
    )(proj, proj, proj, states, do, lb_logits, dproj)


def _place():
    x, y, c = lax.axis_index("x"), lax.axis_index("y"), lax.axis_index("c")
    return x, y, c, [(1 - x, y), (x, 1 - y), (1 - x, 1 - y)]


def _remote(src, dst, ssem, rsem, dev):
    return pltpu.make_async_remote_copy(src_ref=src, dst_ref=dst, send_sem=ssem, recv_sem=rsem,
                                        device_id=dev, device_id_type=MESH)


GATHER_SEMS = 9


def _gather_ops(wp_ref, wall_ref, ssem, rsem, hn_ref=None, hnall_ref=None):
    half = wp_ref.shape[0] // 2

    def place():
        x, y, c, chips = _place()
        return x, y, c, chips, 2 * x + y, pl.ds(c * half, half), pl.ds((1 - c) * half, half)

    def first_sends():
        x, y, c, chips, b, mine, _ = place()
        cps = [_remote(wp_ref.at[mine], wall_ref.at[b, mine], ssem.at[j], rsem.at[j], (*chip, c))
               for j, chip in enumerate(chips)]
        if hn_ref is not None:
            cps += [_remote(hn_ref, hnall_ref.at[b], ssem.at[6 + j], rsem.at[6 + j], (*chip, c))
                    for j, chip in enumerate(chips)]
        return cps

    def forwards():
        x, y, c, chips, _, mine, _ = place()
        return [_remote(wall_ref.at[2 * cx + cy, mine], wall_ref.at[2 * cx + cy, mine],
                        ssem.at[3 + j], rsem.at[3 + j], (x, y, 1 - c)) for j, (cx, cy) in enumerate(chips)]

    def start():
        for cp in first_sends():
            cp.start()

    def forward():
        x, y, c, chips, _, mine, _ = place()
        for j, (cx, cy) in enumerate(chips):
            landed = wall_ref.at[2 * cx + cy, mine]
            _remote(landed, landed, ssem.at[j], rsem.at[j], (cx, cy, c)).wait_recv()
        for cp in forwards():
            cp.start()

    def finish():
        x, y, c, chips, _, _, other = place()
        for j, (cx, cy) in enumerate(chips):
            passed = wall_ref.at[2 * cx + cy, other]
            _remote(passed, passed, ssem.at[3 + j], rsem.at[3 + j], (x, y, 1 - c)).wait_recv()
            if hn_ref is not None:
                row = hnall_ref.at[2 * cx + cy]
                _remote(row, row, ssem.at[6 + j], rsem.at[6 + j], (cx, cy, c)).wait_recv()
        for cp in first_sends() + forwards():
            cp.wait_send()

    return start, forward, finish


def _gather_weights(wp):
    def body(wp_ref, wall_ref, ssem, rsem):
        for step in _gather_ops(wp_ref, wall_ref, ssem, rsem):
            step()

    return pl.pallas_call(
        body, name="gather_weights", in_specs=[ANY], out_specs=ANY,
        out_shape=jax.ShapeDtypeStruct((NPROJ,) + wp.shape, BF16),
        scratch_shapes=[pltpu.SemaphoreType.DMA((GATHER_SEMS,)), pltpu.SemaphoreType.DMA((GATHER_SEMS,))],
    )(wp)


def _pair_ops(g_refs, r_refs, ssem, rsem):
    def copies():
        x, y, c, _ = _place()
        return [_remote(g.at[n, 1 - c], r.at[n], ssem.at[t * NPROJ + n], rsem.at[t * NPROJ + n], (x, y, 1 - c))
                for t, (g, r) in enumerate(zip(g_refs, r_refs)) for n in range(NPROJ)]

    def start():
        for cp in copies():
            cp.start()

    def finish():
        x, y, c, _ = _place()
        for t, r in enumerate(r_refs):
            for n in range(NPROJ):
                k = t * NPROJ + n
                _remote(r.at[n], r.at[n], ssem.at[k], rsem.at[k], (x, y, 1 - c)).wait_recv()
        for cp in copies():
            cp.wait_send()

    return start, finish


def _pair_shapes(grads):
    return ([jax.ShapeDtypeStruct((NPROJ,) + g.shape[2:], F32) for g in grads],
            [pltpu.SemaphoreType.DMA((len(grads) * NPROJ,))] * 2)


def _pair_exchange(grads):
    ng = len(grads)

    def body(*refs):
        start, finish = _pair_ops(refs[:ng], refs[ng:2 * ng], *refs[2 * ng:])
        start()
        finish()

    out_shape, sems = _pair_shapes(grads)
    return pl.pallas_call(
        body, name="pair_exchange", in_specs=[ANY] * ng, out_specs=[ANY] * ng,
        out_shape=out_shape, scratch_shapes=sems,
    )(*grads)


def _chip_ops(s_refs, r_refs, ssem, rsem):
    def copies():
        x, y, c, chips = _place()
        return [_remote(s.at[2 * cx + cy], r.at[2 * x + y], ssem.at[3 * t + j], rsem.at[3 * t + j], (cx, cy, c))
                for t, (s, r) in enumerate(zip(s_refs, r_refs)) for j, (cx, cy) in enumerate(chips)]

    def start():
        for cp in copies():
            cp.start()

    def finish():
        x, y, c, chips = _place()
        for t, r in enumerate(r_refs):
            for j, (cx, cy) in enumerate(chips):
                slot = r.at[2 * cx + cy]
                _remote(slot, slot, ssem.at[3 * t + j], rsem.at[3 * t + j], (cx, cy, c)).wait_recv()
        for cp in copies():
            cp.wait_send()

    return start, finish


def _chip_shapes(sums):
    return ([jax.ShapeDtypeStruct(s.shape, s.dtype) for s in sums],
            [pltpu.SemaphoreType.DMA((3 * len(sums),))] * 2)


def _sibling_share(halves, pack):
    ng = len(halves)

    def body(*refs):
        h_refs, pack_ref = refs[:ng], refs[ng]
        f_refs, allp_ref = refs[ng + 1:2 * ng + 1], refs[2 * ng + 1]
        ssem, rsem, psend, precv, lsem = refs[2 * ng + 2:]
        x, y, c, _ = _place()
        me = 4 * x + 2 * y + c
        local = pltpu.make_async_copy(pack_ref, allp_ref.at[me], lsem)
        local.start()
        flips = [(fx, fy, fc) for fx in (0, 1) for fy in (0, 1) for fc in (0, 1)][1:]
        peers = [(fx + x - 2 * fx * x, fy + y - 2 * fy * y, fc + c - 2 * fc * c) for fx, fy, fc in flips]
        sends = [_remote(pack_ref, allp_ref.at[me], psend.at[m], precv.at[m], peer) for m, peer in enumerate(peers)]
        sends += [_remote(h, f, ssem.at[t], rsem.at[t], (x, y, 1 - c))
                  for t, (h, f) in enumerate(zip(h_refs, f_refs))]
        for cp in sends:
            cp.start()
        for t, f in enumerate(f_refs):
            _remote(f, f, ssem.at[t], rsem.at[t], (x, y, 1 - c)).wait_recv()
        for m, (px, py, pc) in enumerate(peers):
            row = allp_ref.at[4 * px + 2 * py + pc]
            _remote(row, row, psend.at[m], precv.at[m], (px, py, pc)).wait_recv()
        for cp in sends:
            cp.wait_send()
        local.wait()

    return pl.pallas_call(
        body, name="sibling_share", in_specs=[ANY] * (ng + 1), out_specs=[ANY] * (ng + 1),
        out_shape=[jax.ShapeDtypeStruct(h.shape, F32) for h in halves]
        + [jax.ShapeDtypeStruct((8,) + pack.shape, F32)],
        scratch_shapes=[pltpu.SemaphoreType.DMA((ng,)), pltpu.SemaphoreType.DMA((ng,)),
                        pltpu.SemaphoreType.DMA((7,)), pltpu.SemaphoreType.DMA((7,)), pltpu.SemaphoreType.DMA],
    )(*halves, pack)


def _pair_add(own, recv, cidx, name):
    R = own.shape[2]
    tr = min(256, R)

    def body(c_ref, a_ref, b_ref, o_ref):
        o_ref[...] = (a_ref[...] + b_ref[...]).astype(BF16)

    return pl.pallas_call(
        body, name=name,
        grid_spec=pltpu.PrefetchScalarGridSpec(
            num_scalar_prefetch=1, grid=(NPROJ, R // tr),
            in_specs=[pl.BlockSpec((None, None, tr, D), lambda n, r, c: (n, c[0], r, 0)),
                      pl.BlockSpec((None, tr, D), lambda n, r, c: (n, r, 0))],
            out_specs=pl.BlockSpec((None, tr, D), lambda n, r, c: (n, r, 0))),
        out_shape=jax.ShapeDtypeStruct(recv.shape, BF16),
        compiler_params=_cparams(("parallel", "parallel")),
    )(cidx, own, recv)


def _chip_sum(sums, parts, bidx, name):
    R = parts.shape[1]
    tr = min(256, R)

    def body(b_ref, s_ref, p_ref, o_ref):
        acc = None
        for j in range(NPROJ):
            term = jnp.where(b_ref[0] == j, s_ref[...], p_ref[j]).astype(F32)
            acc = term if acc is None else acc + term
        o_ref[...] = acc

    return pl.pallas_call(
        body, name=name,
        grid_spec=pltpu.PrefetchScalarGridSpec(
            num_scalar_prefetch=1, grid=(R // tr,),
            in_specs=[pl.BlockSpec((None, tr, D), lambda r, b: (b[0], r, 0)),
                      pl.BlockSpec((NPROJ, tr, D), lambda r, b: (0, r, 0))],
            out_specs=pl.BlockSpec((tr, D), lambda r, b: (r, 0))),
        out_shape=jax.ShapeDtypeStruct((R, D), F32),
        compiler_params=_cparams(("parallel",)),
    )(bidx, sums, parts)


def _adamw_math(w, g, m, v):
    m = ADAM_B1 * m + (1.0 - ADAM_B1) * g
    v = ADAM_B2 * v + (1.0 - ADAM_B2) * (g * g)
    m_hat = m / (1.0 - ADAM_B1 ** ADAM_STEP)
    v_hat = v / (1.0 - ADAM_B2 ** ADAM_STEP)
    delta = -ADAM_LR * (m_hat / (jnp.sqrt(v_hat) + ADAM_EPS) + ADAM_WD * w)
    return delta, m, v


def _adamw(w, mine, theirs, m, v, cidx, name):
    R = mine.shape[0]
    tr = min(256, R)
    nr = R // tr

    def body(c_ref, w_ref, a_ref, b_ref, m_ref, v_ref, g_ref, d_ref, nm_ref, nv_ref):
        g = jnp.where(pl.program_id(0) == c_ref[0], a_ref[...], b_ref[...])
        g_ref[...] = g
        d_ref[...], nm_ref[...], nv_ref[...] = _adamw_math(w_ref[...], g, m_ref[...], v_ref[...])

    full = pl.BlockSpec((tr, D), lambda h, r, c: (h * nr + r, 0))
    half = pl.BlockSpec((tr, D), lambda h, r, c: (r, 0))
    return pl.pallas_call(
        body, name=name,
        grid_spec=pltpu.PrefetchScalarGridSpec(
            num_scalar_prefetch=1, grid=(2, nr),
            in_specs=[full, half, half, full, full], out_specs=[full] * 4),
        out_shape=[jax.ShapeDtypeStruct(w.shape, F32)] * 4,
        compiler_params=_cparams(("parallel", "parallel")),
    )(cidx, w, mine, theirs, m, v)


PACK_ROWS = 8


def _small_update(allp, bidx, logits, weights, moments_m, moments_v):
    shapes = [w.shape for w in weights]
    q4 = D // NPROJ

    def body(b_ref, allp_ref, hgp_ref, lg_ref, *refs):
        w_refs, m_refs, v_refs = refs[0:6], refs[6:12], refs[12:18]
        loss_ref = refs[18]
        g_out, d_out, m_out, v_out = refs[19:25], refs[25:31], refs[31:37], refs[37:43]

        def total(ref, row, lo, hi):
            acc = ref[0, row:row + 1, lo:hi]
            for dev in range(1, 8):
                acc = acc + ref[dev, row:row + 1, lo:hi]
            return acc

        _, pp = _lower_bound(lg_ref[...])
        dlb = total(allp_ref, 2, 0, D)
        grads = [total(allp_ref, 0, 0, D), total(allp_ref, 4, 0, HD), total(allp_ref, 4, HD, 2 * HD),
                 total(hgp_ref, 1, 0, q4), total(allp_ref, 4, 2 * HD, 3 * HD), None]
        loss_ref[...] = (0.5 / D) * jnp.sum(total(allp_ref, 3, 0, D), axis=1, keepdims=True)
        for t in range(6):
            if t < 5:
                rows = [(slice(None), grads[t])]
            else:
                rows = [(slice(0, 1), -pp * dlb), (slice(1, 2), pp * dlb)]
            for rs, g in rows:
                g_out[t][rs, :] = g
                d_out[t][rs, :], m_out[t][rs, :], v_out[t][rs, :] = _adamw_math(
                    w_refs[t][rs, :], g, m_refs[t][rs, :], v_refs[t][rs, :])

    whole = [pl.BlockSpec(s, lambda i, b: (0, 0)) for s in shapes]
    return pl.pallas_call(
        body, name="small_update",
        grid_spec=pltpu.PrefetchScalarGridSpec(
            num_scalar_prefetch=1, grid=(1,),
            in_specs=[pl.BlockSpec((8, PACK_ROWS, D), lambda i, b: (0, 0, 0)),
                      pl.BlockSpec((8, PACK_ROWS, q4), lambda i, b: (0, 0, b[0])),
                      pl.BlockSpec((2, D), lambda i, b: (0, 0))] + whole * 3,
            out_specs=[pl.BlockSpec((1, 1), lambda i, b: (0, 0))] + whole * 4),
        out_shape=[jax.ShapeDtypeStruct((1, 1), F32)] + [jax.ShapeDtypeStruct(s, F32) for s in shapes] * 4,
        compiler_params=_cparams(("arbitrary",)),
    )(bidx, allp, allp, logits, *weights, *moments_m, *moments_v)


def kernel(x, sb_norm, sb_w_in, sb_q_gain, sb_k_gain, sb_w_out, hg_norm, hg_w_in, hg_o_gain, hg_w_out, hg_lb_logits, loss_target, m_sb_norm, m_sb_w_in, m_sb_q_gain, m_sb_k_gain, m_sb_w_out, m_hg_norm, m_hg_w_in, m_hg_o_gain, m_hg_w_out, m_hg_lb_logits, v_sb_norm, v_sb_w_in, v_sb_q_gain, v_sb_k_gain, v_sb_w_out, v_hg_norm, v_hg_w_in, v_hg_o_gain, v_hg_w_out, v_hg_lb_logits):
    Bl, S, _ = x.shape
    T = Bl * S
    cidx = lax.axis_index("c").astype(jnp.int32).reshape(1)
    bidx = (2 * lax.axis_index("x") + lax.axis_index("y")).astype(jnp.int32).reshape(1)

    def in_hbm(arrays):
        return [pltpu.with_memory_space_constraint(a, pltpu.HBM) for a in arrays]

    def own_slot(gathered, mine):
        return lax.dynamic_update_slice(gathered, mine[None], (bidx[0],) + (0,) * mine.ndim)

    def halved(g):
        return g.reshape(NPROJ, 2, g.shape[-2] * g.shape[0] // (2 * NPROJ), D)

    def heads(a):
        return a.reshape(a.shape[:-2] + (Bl, S, D))

    def flat(a):
        return a.reshape(a.shape[:-3] + (T, D))

    wp_sb = jnp.concatenate([sb_w_in[0], sb_w_out[0]], axis=0).astype(BF16)
    wp_hg = jnp.concatenate([hg_w_in[0], hg_w_out[0]], axis=0).astype(BF16)
    wall_sb, = in_hbm([own_slot(_gather_weights(wp_sb), wp_sb)])
    x2 = x.reshape(T, D)
    tgt = loss_target.reshape(T, D)

    proj0 = _in_proj_fwd(x2, sb_norm, wall_sb, W_IN, "sb_in_fwd")
    o0, ctot, wall_hg, hnall = _sb_fwd(heads(proj0), sb_q_gain, sb_k_gain, wp_hg, hg_norm)
    wall_hg, = in_hbm([own_slot(wall_hg, wp_hg)])
    hgn = own_slot(hnall, hg_norm).reshape(1, D)
    h1 = _out_proj_fwd(flat(o0), proj0, x2, wall_sb, W_OUT, "sb_out_fwd")
    proj1 = _in_proj_fwd(h1, hgn, wall_hg, W_IN, "hg_in_fwd")
    o1, states = _hg_fwd(heads(proj1), hg_lb_logits)
    dh2, loss_terms = _out_proj_fwd(flat(o1), proj1, h1, wall_hg, W_OUT, "hg_out_fwd",
                                    o_gain=hg_o_gain, target=tgt)

    do1, dproj1, gout_hg, d_ogain = _out_proj_bwd(dh2, flat(o1), proj1, wall_hg, W_OUT, "hg_out_bwd",
                                                  o_gain=hg_o_gain)
    dproj1, dlb = _hg_bwd(heads(proj1), states, heads(do1), heads(dproj1), hg_lb_logits)
    dproj1 = flat(dproj1)
    dh1, d_hgn = _in_proj_bwd_x(dproj1, wall_hg, W_IN, h1, hgn, dh2, "hg_in_bwd_x")
    big_hg = [halved(_in_proj_bwd_w(dproj1, h1, hgn, "hg_in_bwd_w")), halved(gout_hg)]
    do0, dproj0, gout_sb, *recv_hg = _out_proj_bwd(dh1, flat(o0), proj0, wall_sb, W_OUT, "sb_out_bwd",
                                                   exchange=big_hg)
    sums_hg = in_hbm(_pair_add(g, r, cidx, "pair_add_" + nm)
                     for g, r, nm in zip(big_hg, in_hbm(recv_hg), ("hg_in", "hg_out")))
    dproj0, d_qg, d_kg, *parts_hg = _sb_bwd(heads(proj0), ctot, heads(do0), heads(dproj0),
                                            sb_q_gain, sb_k_gain, sums_hg)
    dproj0 = flat(dproj0)

    big_sb = [halved(_in_proj_bwd_w(dproj0, x2, sb_norm, "sb_in_bwd_w")), halved(gout_sb)]
    sums_sb = in_hbm(_pair_add(g, r, cidx, "pair_add_" + nm)
                     for g, r, nm in zip(big_sb, in_hbm(_pair_exchange(big_sb)), ("sb_in", "sb_out")))
    grad_x, d_sbn, *parts_sb = _in_proj_bwd_x(dproj0, wall_sb, W_IN, x2, sb_norm, dh1, "sb_in_bwd_x",
                                              exchange=sums_sb)

    names = ["sb_in", "hg_in", "sb_out", "hg_out"]
    sums = [sums_sb[0], sums_hg[0], sums_sb[1], sums_hg[1]]
    parts = [parts_sb[0], parts_hg[0], parts_sb[1], parts_hg[1]]
    halves = in_hbm(_chip_sum(sm, p, bidx, "chip_sum_" + nm) for sm, p, nm in zip(sums, in_hbm(parts), names))
    gains = jnp.concatenate([d_qg, d_kg, d_ogain, jnp.zeros((1, D - 3 * HD), F32)], axis=1)
    pack = jnp.concatenate([d_sbn, d_hgn, dlb, loss_terms, gains, jnp.zeros((3, D), F32)], axis=0)
    *theirs, allp = _sibling_share(halves, pack)
    theirs = in_hbm(theirs)

    big_w = [sb_w_in, hg_w_in, sb_w_out, hg_w_out]
    big_m = [m_sb_w_in, m_hg_w_in, m_sb_w_out, m_hg_w_out]
    big_v = [v_sb_w_in, v_hg_w_in, v_sb_w_out, v_hg_w_out]
    upd = [_adamw(w[0], a, b, m[0], v[0], cidx, "adamw_" + nm)
           for w, a, b, m, v, nm in zip(big_w, halves, theirs, big_m, big_v, names)]
    (g_sb_in, d_sb_in, nm_sb_in, nv_sb_in), (g_hg_in, d_hg_in, nm_hg_in, nv_hg_in), \
        (g_sb_out, d_sb_out, nm_sb_out, nv_sb_out), (g_hg_out, d_hg_out, nm_hg_out, nv_hg_out) = [
            tuple(a[None] for a in u) for u in upd]

    small = _small_update(
        allp, bidx, hg_lb_logits,
        [sb_norm, sb_q_gain, sb_k_gain, hg_norm, hg_o_gain, hg_lb_logits],
        [m_sb_norm, m_sb_q_gain, m_sb_k_gain, m_hg_norm, m_hg_o_gain, m_hg_lb_logits],
        [v_sb_norm, v_sb_q_gain, v_sb_k_gain, v_hg_norm, v_hg_o_gain, v_hg_lb_logits])
    loss = small[0].reshape(())
    (g_sbn, g_qg, g_kg, g_hgn, g_og, g_lb) = small[1:7]
    (d_sbn2, d_qg2, d_kg2, d_hgn2, d_og2, d_lb2) = small[7:13]
    (nm_sbn, nm_qg, nm_kg, nm_hgn, nm_og, nm_lb) = small[13:19]
    (nv_sbn, nv_qg, nv_kg, nv_hgn, nv_og, nv_lb) = small[19:25]

    return (loss, grad_x.reshape(Bl, S, D),
            g_sbn, g_sb_in, g_qg, g_kg, g_sb_out, g_hgn, g_hg_in, g_og, g_hg_out, g_lb,
            d_sbn2, d_sb_in, d_qg2, d_kg2, d_sb_out, d_hgn2, d_hg_in, d_og2, d_hg_out, d_lb2,
            nm_sbn, nm_sb_in, nm_qg, nm_kg, nm_sb_out, nm_hgn, nm_hg_in, nm_og, nm_hg_out, nm_lb,
            nv_sbn, nv_sb_in, nv_qg, nv_kg, nv_sb_out, nv_hgn, nv_hg_in, nv_og, nv_hg_out, nv_lb)
```

```python
import functools

import jax
import jax.numpy as jnp
from jax import lax
from jax.experimental import pallas as pl
from jax.experimental.pallas import tpu as pltpu

F32 = jnp.float32
BF16 = jnp.bfloat16
MESH = pl.DeviceIdType.MESH
ANY = pl.BlockSpec(memory_space=pl.ANY)

D = 1024
HEADS = 8
HD = 128
NPROJ = 4
RMS_EPS = 1e-6
TK = 256
CH = 64
CH_LOG2 = 6
GR = 128
SCALE = HD ** -0.5
EXP_CLAMP = 60.0
W_IN, W_OUT_SB, W_OUT_HG = 0, 4, 5

ADAM_LR = 0.001
ADAM_B1 = 0.9
ADAM_B2 = 0.999
ADAM_EPS = 1e-08
ADAM_WD = 0.01
ADAM_STEP = 10

NT = (((1,), (1,)), ((), ()))
TN = (((0,), (0,)), ((), ()))
MIB = 1024 * 1024


def _cparams(sem=None, vmem_mib=40):
    return pltpu.CompilerParams(dimension_semantics=sem, vmem_limit_bytes=vmem_mib * MIB)


def _dot(a, b, dims=None):
    if dims is None:
        return jnp.dot(a, b, preferred_element_type=F32)
    return lax.dot_general(a, b, dims, preferred_element_type=F32)


def _sigmoid(x):
    return 1.0 / (1.0 + jnp.exp(-x))


def _rms(x):
    return lax.rsqrt(jnp.mean(x * x, axis=-1, keepdims=True) + RMS_EPS)


def _rms_bwd(x, r, gain, dy):
    a = dy * gain
    dx = r * a - x * (r * r * r) * jnp.mean(x * a, axis=-1, keepdims=True)
    return dx, dy * (x * r)


def _split2(v):
    hi = v.astype(BF16)
    lo = (v - hi.astype(F32)).astype(BF16)
    return hi, lo


def _cum2(v, u):
    hi, lo = _split2(v)
    return _dot(hi, u) + _dot(lo, u)


def _dot3(a, b, dims=None):
    ah, al = _split2(a)
    bh, bl = _split2(b)
    return _dot(ah, bh, dims) + _dot(ah, bl, dims) + _dot(al, bh, dims)


def _cum2l(u, v):
    hi, lo = _split2(v)
    return _dot(u, hi) + _dot(u, lo)


def _in_proj_fwd(h, gain, wall, wblk, name):
    T = h.shape[0]
    tm = min(1024, T)

    def body(h_ref, g_ref, w_ref, o_ref, u_s):
        rows = pl.ds(pl.multiple_of(pl.program_id(1) * tm, tm), tm)

        @pl.when(pl.program_id(0) == 0)
        def _():
            x = h_ref[...]
            u_s[rows, :] = (x * _rms(x) * g_ref[...]).astype(BF16)

        o_ref[...] = _dot(u_s[rows, :], w_ref[...])

    return pl.pallas_call(
        body, name=name, grid=(NPROJ, T // tm),
        in_specs=[pl.BlockSpec((tm, D), lambda n, i: (jnp.where(n == 0, i, 0), 0)),
                  pl.BlockSpec((1, D), lambda n, i: (0, 0)),
                  pl.BlockSpec((None, D, D), lambda n, i: (n, wblk, 0))],
        out_specs=pl.BlockSpec((None, tm, D), lambda n, i: (n, i, 0)),
        out_shape=jax.ShapeDtypeStruct((NPROJ, T, D), F32),
        scratch_shapes=[pltpu.VMEM((T, D), BF16)],
        compiler_params=_cparams(("arbitrary", "arbitrary")),
    )(h, gain, wall)


def _head_norm(x):
    outs = []
    for hh in range(x.shape[1] // HD):
        xs = x[:, hh * HD:(hh + 1) * HD]
        outs.append((xs, _rms(xs)))
    return outs


def _w_out_specs(wblk):
    kb = D // NPROJ
    return [pl.BlockSpec((None, kb, D), functools.partial(lambda j, i: (j, wblk, 0), j)) for j in range(NPROJ)]


def _out_proj_fwd(o, proj, resid, wall, wblk, name, o_gain=None, target=None):
    T = o.shape[0]
    tm = min(512, T)
    kb = D // NPROJ
    with_loss = target is not None

    def body(*refs):
        o_ref, g_ref, r_ref = refs[:3]
        w_refs = refs[3:3 + NPROJ]
        if with_loss:
            og_ref, t_ref, dh_ref, ls_ref = refs[3 + NPROJ:]
        else:
            h_ref, = refs[3 + NPROJ:]
        x = o_ref[...]
        if with_loss:
            x = jnp.concatenate([xs * r * og_ref[...] for xs, r in _head_norm(x)], axis=1)
        g = g_ref[...]
        a = (x * (g * _sigmoid(g))).astype(BF16)
        hnew = r_ref[...]
        for j in range(NPROJ):
            hnew = hnew + _dot(a[:, j * kb:(j + 1) * kb], w_refs[j][...])
        if with_loss:
            err = hnew - t_ref[...]
            dh_ref[...] = err * (1.0 / D)
            part = jnp.sum(err * err, axis=0, keepdims=True)

            @pl.when(pl.program_id(0) == 0)
            def _():
                ls_ref[...] = part

            @pl.when(pl.program_id(0) != 0)
            def _():
                ls_ref[...] += part
        else:
            h_ref[...] = hnew

    tile = pl.BlockSpec((tm, D), lambda i: (i, 0))
    in_specs = [tile, pl.BlockSpec((None, tm, D), lambda i: (3, i, 0)), tile] + _w_out_specs(wblk)
    args = [o, proj, resid] + [wall] * NPROJ
    out_specs = tile
    out_shape = jax.ShapeDtypeStruct((T, D), F32)
    if with_loss:
        in_specs += [pl.BlockSpec((1, HD), lambda i: (0, 0)), tile]
        args += [o_gain, target]
        out_specs = [tile, pl.BlockSpec((1, D), lambda i: (0, 0))]
        out_shape = [out_shape, jax.ShapeDtypeStruct((1, D), F32)]
    return pl.pallas_call(
        body, name=name, grid=(T // tm,), in_specs=in_specs, out_specs=out_specs,
        out_shape=out_shape, compiler_params=_cparams(("arbitrary",)),
    )(*args)


def _out_proj_bwd(dy, o, proj, wall, wblk, name, o_gain=None, exchange=()):
    T = o.shape[0]
    tm = min(512, T)
    kb = D // NPROJ
    normed = o_gain is not None
    ne = len(exchange)

    def body(*refs):
        it = iter(refs)
        dy_ref, o_ref, g_ref = (next(it) for _ in range(3))
        w_refs = [next(it) for _ in range(NPROJ)]
        og_ref = next(it) if normed else None
        xg_refs = [next(it) for _ in range(ne)]
        do_ref, dg_ref, dw_ref = (next(it) for _ in range(3))
        dgain_ref = next(it) if normed else None
        xr_refs = [next(it) for _ in range(ne)]
        wt_s = next(it)
        first = pl.program_id(0) == 0
        if ne:
            start, finish = _pair_ops(xg_refs, xr_refs, next(it), next(it))
            pl.when(first)(start)

        @pl.when(first)
        def _():
            for j in range(NPROJ):
                wt_s[:, j * kb:(j + 1) * kb] = w_refs[j][...].T
        g = g_ref[...]
        s = _sigmoid(g)
        sl = g * s
        x = o_ref[...]
        if normed:
            heads = _head_norm(x)
            on = jnp.concatenate([xs * r * og_ref[...] for xs, r in heads], axis=1)
        else:
            on = x
        dyb = dy_ref[...].astype(BF16)
        a = (on * sl).astype(BF16)
        for j in range(NPROJ):
            part = _dot(a[:, j * kb:(j + 1) * kb], dyb, TN)

            @pl.when(first)
            def _():
                dw_ref[j] = part

            @pl.when(jnp.logical_not(first))
            def _():
                dw_ref[j] += part

        da = _dot(dyb, wt_s[...])
        d_on = da * sl
        dg_ref[...] = (da * on * (s * (1.0 + g * (1.0 - s)))).astype(BF16)
        if normed:
            dxs, gsum = [], None
            for hh, (xs, r) in enumerate(heads):
                dx, gt = _rms_bwd(xs, r, og_ref[...], d_on[:, hh * HD:(hh + 1) * HD])
                dxs.append(dx)
                gt = jnp.sum(gt, axis=0, keepdims=True)
                gsum = gt if gsum is None else gsum + gt
            do_ref[...] = jnp.concatenate(dxs, axis=1).astype(BF16)

            @pl.when(first)
            def _():
                dgain_ref[...] = gsum

            @pl.when(jnp.logical_not(first))
            def _():
                dgain_ref[...] += gsum
        else:
            do_ref[...] = d_on.astype(BF16)
        if ne:
            pl.when(pl.program_id(0) == T // tm - 1)(finish)

    tile = pl.BlockSpec((tm, D), lambda i: (i, 0))
    gate = pl.BlockSpec((None, tm, D), lambda i: (3, i, 0))
    in_specs = [tile, tile, gate] + _w_out_specs(wblk)
    args = [dy, o, proj] + [wall] * NPROJ
    out_specs = [tile, gate, pl.BlockSpec((NPROJ, kb, D), lambda i: (0, 0, 0))]
    out_shape = [jax.ShapeDtypeStruct((T, D), BF16),
                 jax.ShapeDtypeStruct((NPROJ, T, D), BF16),
                 jax.ShapeDtypeStruct((NPROJ, kb, D), F32)]
    if normed:
        in_specs.append(pl.BlockSpec((1, HD), lambda i: (0, 0)))
        args.append(o_gain)
        out_specs.append(pl.BlockSpec((1, HD), lambda i: (0, 0)))
        out_shape.append(jax.ShapeDtypeStruct((1, HD), F32))
    x_shape, x_sems = _pair_shapes(exchange) if ne else ([], [])
    return pl.pallas_call(
        body, name=name, grid=(T // tm,), in_specs=in_specs + [ANY] * ne, out_specs=out_specs + [ANY] * ne,
        out_shape=out_shape + x_shape, scratch_shapes=[pltpu.VMEM((D, D), BF16)] + x_sems,
        compiler_params=_cparams(("arbitrary",), vmem_mib=48),
    )(*args, *exchange)


def _in_proj_bwd_x(dproj, wall, wblk, h, gain, dres, name, exchange=()):
    T = h.shape[0]
    tm = min(512, T)
    ne = len(exchange)

    def body(d_ref, w_ref, h_ref, g_ref, r_ref, *refs):
        xs_refs, (dh_ref, dgain_ref), xr_refs = refs[:ne], refs[ne:ne + 2], refs[ne + 2:2 * ne + 2]
        du, wt_s = refs[2 * ne + 2:2 * ne + 4]
        i, n = pl.program_id(0), pl.program_id(1)
        if ne:
            start, finish = _chip_ops(xs_refs, xr_refs, *refs[2 * ne + 4:])
            pl.when(jnp.logical_and(i == 0, n == 0))(start)

        @pl.when(i == 0)
        def _():
            wt_s[n] = w_ref[...].T

        part = _dot(d_ref[...], wt_s[n])

        @pl.when(n == 0)
        def _():
            du[...] = part

        @pl.when(n != 0)
        def _():
            du[...] += part

        @pl.when(n == NPROJ - 1)
        def _():
            x = h_ref[...]
            dx, gt = _rms_bwd(x, _rms(x), g_ref[...], du[...])
            dh_ref[...] = r_ref[...] + dx
            gt = jnp.sum(gt, axis=0, keepdims=True)

            @pl.when(i == 0)
            def _():
                dgain_ref[...] = gt

            @pl.when(i != 0)
            def _():
                dgain_ref[...] += gt

        if ne:
            pl.when(jnp.logical_and(i == T // tm - 1, n == NPROJ - 1))(finish)

    x_shape, x_sems = _chip_shapes(exchange) if ne else ([], [])
    return pl.pallas_call(
        body, name=name, grid=(T // tm, NPROJ),
        in_specs=[pl.BlockSpec((None, tm, D), lambda i, n: (n, i, 0)),
                  pl.BlockSpec((None, D, D), lambda i, n: (jnp.where(i == 0, n, NPROJ - 1), wblk, 0)),
                  pl.BlockSpec((tm, D), lambda i, n: (i, 0)),
                  pl.BlockSpec((1, D), lambda i, n: (0, 0)),
                  pl.BlockSpec((tm, D), lambda i, n: (i, 0))] + [ANY] * ne,
        out_specs=[pl.BlockSpec((tm, D), lambda i, n: (i, 0)),
                   pl.BlockSpec((1, D), lambda i, n: (0, 0))] + [ANY] * ne,
        out_shape=[jax.ShapeDtypeStruct((T, D), F32), jax.ShapeDtypeStruct((1, D), F32)] + x_shape,
        scratch_shapes=[pltpu.VMEM((tm, D), F32), pltpu.VMEM((NPROJ, D, D), BF16)] + x_sems,
        compiler_params=_cparams(("arbitrary", "arbitrary")),
    )(dproj, wall, h, gain, dres, *exchange)


def _in_proj_bwd_w(dproj, h, gain, name):
    T = h.shape[0]
    tk = min(1024, T)

    def body(d_ref, h_ref, g_ref, dw_ref, ut_s):
        k = pl.program_id(1)

        @pl.when(pl.program_id(0) == 0)
        def _():
            x = h_ref[...]
            ut_s[k] = (x * _rms(x) * g_ref[...]).astype(BF16).T

        part = _dot(ut_s[k], d_ref[...])

        @pl.when(k == 0)
        def _():
            dw_ref[...] = part

        @pl.when(k != 0)
        def _():
            dw_ref[...] += part

    return pl.pallas_call(
        body, name=name, grid=(NPROJ, T // tk),
        in_specs=[pl.BlockSpec((None, tk, D), lambda n, k: (n, k, 0)),
                  pl.BlockSpec((tk, D), lambda n, k: (jnp.where(n == 0, k, 0), 0)),
                  pl.BlockSpec((1, D), lambda n, k: (0, 0))],
        out_specs=pl.BlockSpec((None, D, D), lambda n, k: (n, 0, 0)),
        out_shape=jax.ShapeDtypeStruct((NPROJ, D, D), F32),
        scratch_shapes=[pltpu.VMEM((T // tk, D, tk), BF16)],
        compiler_params=_cparams(("arbitrary", "arbitrary")),
    )(dproj, h, gain)


def _log_sigmoid_pair(z):
    lb = jnp.minimum(z, 0.0) - jnp.log(1.0 + jnp.exp(-jnp.abs(z)))
    return lb, lb - z


def _slab_consts():
    t = lax.broadcasted_iota(jnp.int32, (TK, TK), 0)
    s = lax.broadcasted_iota(jnp.int32, (TK, TK), 1)
    return s < t, (t > s).astype(BF16), (t < s).astype(BF16)


def _slab_rows(k0, S):
    return [(r0, r1, masked) for r0, r1, masked in ((k0, k0 + TK, True), (k0 + TK, S, False)) if r0 < r1]


def _sb_fwd(proj, q_gain, k_gain, wp, hn):
    _, Bl, S, _ = proj.shape
    steps = Bl * HEADS

    def body(q_ref, k_ref, v_ref, qg_ref, kg_ref, wp_ref, hn_ref, o_ref, ct_ref, wall_ref, hnall_ref,
             qn, kn, vb, ssem, rsem):
        step = pl.program_id(0) * HEADS + pl.program_id(1)
        start, forward, finish = _gather_ops(wp_ref, wall_ref, ssem, rsem, hn_ref, hnall_ref)
        pl.when(step == 0)(start)
        pl.when(step == steps // 2)(forward)
        q = q_ref[...]
        qn[...] = (q * _rms(q) * (qg_ref[...] * SCALE)).astype(BF16)
        k = k_ref[...]
        kn[...] = (k * _rms(k) * kg_ref[...]).astype(BF16)
        vb[...] = v_ref[...].astype(BF16)
        tri, u_gt, _ = _slab_consts()
        nb = S // TK
        c_blk = [jnp.zeros((TK, 1), F32)] * nb
        o_blk = [jnp.zeros((TK, HD), F32)] * nb
        for k0 in reversed(range(0, S, TK)):
            kb, vbb = kn[k0:k0 + TK, :], vb[k0:k0 + TK, :]
            for r0, r1, masked in _slab_rows(k0, S):
                blocks = range(r0 // TK, r1 // TK)
                z = _dot(qn[r0:r1, :], kb, NT)
                lb, ls = _log_sigmoid_pair(z)
                if masked:
                    ls = jnp.where(tri, ls, 0.0)
                c = jnp.concatenate([c_blk[b] for b in blocks], axis=0)
                w = jnp.exp(lb + _cum2(ls, u_gt) + c)
                if masked:
                    w = jnp.where(tri, w, 0.0)
                o_new = _dot(w.astype(BF16), vbb)
                c_new = jnp.sum(ls, axis=1, keepdims=True)
                for i, b in enumerate(blocks):
                    o_blk[b] = o_blk[b] + o_new[i * TK:(i + 1) * TK]
                    c_blk[b] = c_blk[b] + c_new[i * TK:(i + 1) * TK]
        o_ref[...] = jnp.concatenate(o_blk, axis=0)
        ct_ref[...] = jnp.concatenate(c_blk, axis=0)
        pl.when(step == steps - 1)(finish)

    def slot(n):
        return pl.BlockSpec((None, None, S, HD), lambda b, h: (n, b, 0, h))

    return pl.pallas_call(
        body, name="sb_fwd", grid=(Bl, HEADS),
        in_specs=[slot(0), slot(1), slot(2),
                  pl.BlockSpec((1, HD), lambda b, h: (0, 0)),
                  pl.BlockSpec((1, HD), lambda b, h: (0, 0)), ANY, ANY],
        out_specs=[pl.BlockSpec((None, S, HD), lambda b, h: (b, 0, h)),
                   pl.BlockSpec((None, None, S, 1), lambda b, h: (b, h, 0, 0)), ANY, ANY],
        out_shape=[jax.ShapeDtypeStruct((Bl, S, D), F32),
                   jax.ShapeDtypeStruct((Bl, HEADS, S, 1), F32),
                   jax.ShapeDtypeStruct((NPROJ,) + wp.shape, BF16),
                   jax.ShapeDtypeStruct((NPROJ,) + hn.shape, F32)],
        scratch_shapes=[pltpu.VMEM((S, HD), BF16)] * 3 + [pltpu.SemaphoreType.DMA((GATHER_SEMS,))] * 2,
        compiler_params=_cparams(("arbitrary", "arbitrary"), vmem_mib=56),
    )(proj, proj, proj, q_gain, k_gain, wp, hn)


def _sb_bwd(proj, ctot, do, dproj, q_gain, k_gain, exchange):
    _, Bl, S, _ = proj.shape
    ne = len(exchange)

    def body(q_ref, k_ref, v_ref, ct_ref, do_ref, qg_ref, kg_ref, _, *refs):
        xs_refs, (dqkv_ref, dqg_ref, dkg_ref), xr_refs = refs[:ne], refs[ne:ne + 3], refs[ne + 3:2 * ne + 3]
        qn, kn, vb, dqn, dkn, dvn, ssem, rsem = refs[2 * ne + 3:]
        step = pl.program_id(0) * HEADS + pl.program_id(1)
        first = step == 0
        start, finish = _chip_ops(xs_refs, xr_refs, ssem, rsem)

        @pl.when(first)
        def _():
            start()
            dqg_ref[...] = jnp.zeros_like(dqg_ref)
            dkg_ref[...] = jnp.zeros_like(dkg_ref)

        q = q_ref[...]
        rq = _rms(q)
        qn[...] = (q * rq * (qg_ref[...] * SCALE)).astype(BF16)
        k = k_ref[...]
        rk = _rms(k)
        kn[...] = (k * rk * kg_ref[...]).astype(BF16)
        vb[...] = v_ref[...].astype(BF16)
        for acc in (dqn, dkn, dvn):
            acc[...] = jnp.zeros_like(acc)
        tri, u_gt, u_lt = _slab_consts()
        nb = S // TK
        passed, e = [jnp.zeros((TK, 1), F32)] * nb, [jnp.zeros((TK, 1), F32)] * nb
        for k0 in range(0, S, TK):
            keys = slice(k0, k0 + TK)
            kb, vbb = kn[keys, :], vb[keys, :]
            for r0, r1, masked in _slab_rows(k0, S):
                rows, blocks = slice(r0, r1), range(r0 // TK, r1 // TK)
                qb, dobb = qn[rows, :], do_ref[rows, :]
                z = _dot(qb, kb, NT)
                lb, ls = _log_sigmoid_pair(z)
                if masked:
                    ls = jnp.where(tri, ls, 0.0)
                p_new = jnp.concatenate([passed[b] for b in blocks], axis=0) + jnp.sum(ls, axis=1, keepdims=True)
                w = jnp.exp(lb + _cum2(ls, u_gt) + (ct_ref[rows, :] - p_new))
                if masked:
                    w = jnp.where(tri, w, 0.0)
                de = w * _dot(dobb, vbb, NT)
                e_old = jnp.concatenate([e[b] for b in blocks], axis=0)
                dls = e_old + _cum2(de, u_lt)
                e_new = e_old + jnp.sum(de, axis=1, keepdims=True)
                for i, b in enumerate(blocks):
                    passed[b], e[b] = p_new[i * TK:(i + 1) * TK], e_new[i * TK:(i + 1) * TK]
                sg = jnp.exp(lb)
                dz = de - sg * (de + dls)
                if masked:
                    dz = jnp.where(tri, dz, 0.0)
                dzb = dz.astype(BF16)
                dqn[rows, :] += _dot(dzb, kb)
                dkn[keys, :] += _dot(dzb, qb, TN)
                dvn[keys, :] += _dot(w.astype(BF16), dobb, TN)

        dx, gt = _rms_bwd(q, rq, qg_ref[...], dqn[...] * SCALE)
        dqkv_ref[0] = dx.astype(BF16)
        dqg_ref[...] += jnp.sum(gt, axis=0, keepdims=True)
        dx, gt = _rms_bwd(k, rk, kg_ref[...], dkn[...])
        dqkv_ref[1] = dx.astype(BF16)
        dkg_ref[...] += jnp.sum(gt, axis=0, keepdims=True)
        dqkv_ref[2] = dvn[...].astype(BF16)
        pl.when(step == Bl * HEADS - 1)(finish)

    def slot(n):
        return pl.BlockSpec((None, None, S, HD), lambda b, h: (n, b, 0, h))

    head = pl.BlockSpec((None, S, HD), lambda b, h: (b, 0, h))
    gain = pl.BlockSpec((1, HD), lambda b, h: (0, 0))
    x_shape, x_sems = _chip_shapes(exchange)
    return pl.pallas_call(
        body, name="sb_bwd", grid=(Bl, HEADS),
        in_specs=[slot(0), slot(1), slot(2),
                  pl.BlockSpec((None, None, S, 1), lambda b, h: (b, h, 0, 0)), head, gain, gain, ANY] + [ANY] * ne,
        out_specs=[pl.BlockSpec((3, None, S, HD), lambda b, h: (0, b, 0, h)), gain, gain] + [ANY] * ne,
        out_shape=[jax.ShapeDtypeStruct(dproj.shape, dproj.dtype),
                   jax.ShapeDtypeStruct((1, HD), F32), jax.ShapeDtypeStruct((1, HD), F32)] + x_shape,
        scratch_shapes=[pltpu.VMEM((S, HD), BF16)] * 3 + [pltpu.VMEM((S, HD), F32)] * 3 + x_sems,
        input_output_aliases={7: 0},
        compiler_params=_cparams(("arbitrary", "arbitrary"), vmem_mib=56),
    )(proj, proj, proj, ctot, do, q_gain, k_gain, dproj, *exchange)


def _lower_bound(logits):
    l0, l1 = logits[0:1, :], logits[1:2, :]
    m = jnp.maximum(l0, l1)
    e0, e1 = jnp.exp(l0 - m), jnp.exp(l1 - m)
    p0, p1 = e0 / (e0 + e1), e1 / (e0 + e1)
    return (p0 + p1) - p0, p0 * p1


def _hg_gates(qr, fp, lbv):
    sq = _sigmoid(qr)
    sp = _sigmoid(fp)
    sn = 1.0 / (1.0 + jnp.exp(fp))
    f = lbv + (1.0 - lbv) * sp
    return qr * sq, sq, sp, sn, f, (1.0 - lbv) * sn


def _group_consts():
    t = lax.broadcasted_iota(jnp.int32, (GR, GR), 0)
    j = lax.broadcasted_iota(jnp.int32, (GR, GR), 1)
    same = lax.shift_right_logical(t, CH_LOG2) == lax.shift_right_logical(j, CH_LOG2)
    tril = jnp.logical_and(same, j <= t)
    return (tril, tril.astype(BF16), jnp.logical_and(same, j >= t).astype(BF16), same.astype(BF16))


def _hg_decays(qa, k, f, t_inc, t_same):
    g = jnp.log(f)
    gc = _cum2l(t_inc, g)
    gl = _cum2l(t_same, g)
    gm = gc - 0.5 * gl
    e_q = jnp.exp(jnp.minimum(gm, EXP_CLAMP))
    e_k = jnp.exp(jnp.minimum(-gm, EXP_CLAMP))
    e_g = jnp.exp(gc)
    e_l = jnp.exp(gl - gc)
    return qa * e_q, k * e_k, qa * e_g, k * e_l, e_q, e_k, e_g, e_l, jnp.exp(gl)


def _hg_fwd(proj, lb_logits):
    _, Bl, S, _ = proj.shape
    nc = S // CH

    def body(q_ref, f_ref, i_ref, lg_ref, o_ref, st_ref, egl_s):
        lbv, _ = _lower_bound(lg_ref[...])
        tril, t_inc, _, t_same = _group_consts()
        st = jnp.zeros((HD, HD), F32)
        for g0 in range(0, S, GR):
            rs = slice(g0, g0 + GR)
            qa, _, _, _, f, k = _hg_gates(q_ref[rs, :], f_ref[rs, :], lbv)
            qt, kt, qg, kd, _, _, _, _, e_gl = _hg_decays(qa, k, f, t_inc, t_same)
            a = jnp.where(tril, _dot(qt.astype(BF16), kt.astype(BF16), NT), 0.0)
            ib, qgb, kdb = i_ref[rs, :].astype(BF16), qg.astype(BF16), kd.astype(BF16)
            within = _dot(a.astype(BF16), ib)
            egl_s[rs, :] = e_gl
            outs = []
            for l0 in range(0, GR, CH):
                ls = slice(l0, l0 + CH)
                st_ref[(g0 + l0) // CH] = st
                outs.append(within[ls] + _dot(qgb[ls], st.astype(BF16), NT))
                st = st * egl_s[g0 + l0:g0 + l0 + 1, :] + _dot(ib[ls], kdb[ls], TN)
            o_ref[rs, :] = jnp.concatenate(outs, axis=0)

    def slot(n):
        return pl.BlockSpec((None, None, S, HD), lambda b, h: (n, b, 0, h))

    return pl.pallas_call(
        body, name="hg_fwd", grid=(Bl, HEADS),
        in_specs=[slot(0), slot(1), slot(2), pl.BlockSpec((2, HD), lambda b, h: (0, h))],
        out_specs=[pl.BlockSpec((None, S, HD), lambda b, h: (b, 0, h)),
                   pl.BlockSpec((None, None, nc, HD, HD), lambda b, h: (b, h, 0, 0, 0))],
        out_shape=[jax.ShapeDtypeStruct((Bl, S, D), F32),
                   jax.ShapeDtypeStruct((Bl, HEADS, nc, HD, HD), F32)],
        scratch_shapes=[pltpu.VMEM((S, HD), F32)],
        compiler_params=_cparams(("parallel", "parallel")),
    )(proj, proj, proj, lb_logits)


def _hg_bwd(proj, states, do, dproj, lb_logits):
    _, Bl, S, _ = proj.shape

    def body(q_ref, f_ref, i_ref, st_ref, do_ref, lg_ref, _, dqfi_ref, dlb_ref, egl_s):
        lbv, _ = _lower_bound(lg_ref[...])
        tril, t_inc, t_dec, t_same = _group_consts()
        dst = jnp.zeros((HD, HD), F32)
        dlb = jnp.zeros((1, HD), F32)
        for g0 in reversed(range(0, S, GR)):
            rs = slice(g0, g0 + GR)
            qr, fp = q_ref[rs, :], f_ref[rs, :]
            qa, sq, sp, sn, f, k = _hg_gates(qr, fp, lbv)
            qt, kt, qg, kd, e_q, e_k, e_g, e_l, e_gl = _hg_decays(qa, k, f, t_inc, t_same)
            ib, dob, qgb, kdb = i_ref[rs, :].astype(BF16), do_ref[rs, :], qg.astype(BF16), kd.astype(BF16)
            egl_s[rs, :] = e_gl
            ab = jnp.where(tril, _dot(qt.astype(BF16), kt.astype(BF16), NT), 0.0).astype(BF16)
            da = jnp.where(tril, _dot(dob, ib, NT), 0.0)
            dqt = _dot3(da, kt)
            dkt = _dot3(da, qt, TN)
            di_within = _dot(ab, dob, TN)
            dqg, dkd, di, dse = [], [], [], []
            for l0 in reversed(range(0, GR, CH)):
                ls = slice(l0, l0 + CH)
                st = st_ref[(g0 + l0) // CH]
                dstb = dst.astype(BF16)
                dqg.insert(0, _dot(dob[ls], st.astype(BF16)))
                dkd.insert(0, _dot(ib[ls], dstb))
                di.insert(0, _dot(kdb[ls], dstb, NT))
                dse.insert(0, jnp.broadcast_to(jnp.sum(dst * st, axis=0, keepdims=True), (CH, HD)))
                dst = dst * egl_s[g0 + l0:g0 + l0 + 1, :] + _dot(dob[ls], qgb[ls], TN)
            dqg, dkd, di, dse = (jnp.concatenate(p, axis=0) for p in (dqg, dkd, di, dse))
            dqfi_ref[2, rs, :] = (di + di_within).astype(BF16)
            dgc = dqt * qt - dkt * kt + dqg * qg - dkd * kd
            dg = _cum2l(t_dec, dgc) + _cum2l(t_same, dkd * kd) + dse * e_gl
            t1 = dg / f - (dkt * e_k + dkd * e_l)
            dqfi_ref[1, rs, :] = ((1.0 - lbv) * t1 * sp * sn).astype(BF16)
            dqfi_ref[0, rs, :] = ((dqt * e_q + dqg * e_g) * (sq * (1.0 + qr * (1.0 - sq)))).astype(BF16)
            dlb = dlb + jnp.sum(sn * t1, axis=0, keepdims=True)

        @pl.when(pl.program_id(1) == 0)
        def _():
            dlb_ref[...] = dlb

        @pl.when(pl.program_id(1) != 0)
        def _():
            dlb_ref[...] += dlb

    def slot(n):
        return pl.BlockSpec((None, None, S, HD), lambda h, b: (n, b, 0, h))

    return pl.pallas_call(
        body, name="hg_bwd", grid=(HEADS, Bl),
        in_specs=[slot(0), slot(1), slot(2),
                  pl.BlockSpec((None, None, S // CH, HD, HD), lambda h, b: (b, h, 0, 0, 0)),
                  pl.BlockSpec((None, S, HD), lambda h, b: (b, 0, h)),
                  pl.BlockSpec((2, HD), lambda h, b: (0, h)), ANY],
        out_specs=[pl.BlockSpec((3, None, S, HD), lambda h, b: (0, b, 0, h)),
                   pl.BlockSpec((1, HD), lambda h, b: (0, h))],
        out_shape=[jax.ShapeDtypeStruct(dproj.shape, dproj.dtype), jax.ShapeDtypeStruct((1, D), F32)],
        scratch_shapes=[pltpu.VMEM((S, HD), F32)],
        input_output_aliases={6: 0},
        compiler_params=_cparams(("parallel", "arbitrary")),
    )(proj, proj, proj, states, do, lb_logits, dproj)


def _place():
    x, y, c = lax.axis_index("x"), lax.axis_index("y"), lax.axis_index("c")
    return x, y, c, [(1 - x, y), (x, 1 - y), (1 - x, 1 - y)]


def _remote(src, dst, ssem, rsem, dev):
    return pltpu.make_async_remote_copy(src_ref=src, dst_ref=dst, send_sem=ssem, recv_sem=rsem,
                                        device_id=dev, device_id_type=MESH)


GATHER_SEMS = 9


def _gather_ops(wp_ref, wall_ref, ssem, rsem, hn_ref=None, hnall_ref=None):
    half = wp_ref.shape[0] // 2

    def place():
        x, y, c, chips = _place()
        return x, y, c, chips, 2 * x + y, pl.ds(c * half, half), pl.ds((1 - c) * half, half)

    def first_sends():
        x, y, c, chips, b, mine, _ = place()
        cps = [_remote(wp_ref.at[mine], wall_ref.at[b, mine], ssem.at[j], rsem.at[j], (*chip, c))
               for j, chip in enumerate(chips)]
        if hn_ref is not None:
            cps += [_remote(hn_ref, hnall_ref.at[b], ssem.at[6 + j], rsem.at[6 + j], (*chip, c))
                    for j, chip in enumerate(chips)]
        return cps

    def forwards():
        x, y, c, chips, _, mine, _ = place()
        return [_remote(wall_ref.at[2 * cx + cy, mine], wall_ref.at[2 * cx + cy, mine],
                        ssem.at[3 + j], rsem.at[3 + j], (x, y, 1 - c)) for j, (cx, cy) in enumerate(chips)]

    def start():
        for cp in first_sends():
            cp.start()

    def forward():
        x, y, c, chips, _, mine, _ = place()
        for j, (cx, cy) in enumerate(chips):
            landed = wall_ref.at[2 * cx + cy, mine]
            _remote(landed, landed, ssem.at[j], rsem.at[j], (cx, cy, c)).wait_recv()
        for cp in forwards():
            cp.start()

    def finish():
        x, y, c, chips, _, _, other = place()
        for j, (cx, cy) in enumerate(chips):
            passed = wall_ref.at[2 * cx + cy, other]
            _remote(passed, passed, ssem.at[3 + j], rsem.at[3 + j], (x, y, 1 - c)).wait_recv()
            if hn_ref is not None:
                row = hnall_ref.at[2 * cx + cy]
                _remote(row, row, ssem.at[6 + j], rsem.at[6 + j], (cx, cy, c)).wait_recv()
        for cp in first_sends() + forwards():
            cp.wait_send()

    return start, forward, finish


def _gather_weights(wp):
    def body(wp_ref, wall_ref, ssem, rsem):
        for step in _gather_ops(wp_ref, wall_ref, ssem, rsem):
            step()

    return pl.pallas_call(
        body, name="gather_weights", in_specs=[ANY], out_specs=ANY,
        out_shape=jax.ShapeDtypeStruct((NPROJ,) + wp.shape, BF16),
        scratch_shapes=[pltpu.SemaphoreType.DMA((GATHER_SEMS,)), pltpu.SemaphoreType.DMA((GATHER_SEMS,))],
    )(wp)


def _pair_ops(g_refs, r_refs, ssem, rsem):
    def copies():
        x, y, c, _ = _place()
        return [_remote(g.at[n, 1 - c], r.at[n], ssem.at[t * NPROJ + n], rsem.at[t * NPROJ + n], (x, y, 1 - c))
                for t, (g, r) in enumerate(zip(g_refs, r_refs)) for n in range(NPROJ)]

    def start():
        for cp in copies():
            cp.start()

    def finish():
        x, y, c, _ = _place()
        for t, r in enumerate(r_refs):
            for n in range(NPROJ):
                k = t * NPROJ + n
                _remote(r.at[n], r.at[n], ssem.at[k], rsem.at[k], (x, y, 1 - c)).wait_recv()
        for cp in copies():
            cp.wait_send()

    return start, finish


def _pair_shapes(grads):
    return ([jax.ShapeDtypeStruct((NPROJ,) + g.shape[2:], F32) for g in grads],
            [pltpu.SemaphoreType.DMA((len(grads) * NPROJ,))] * 2)


def _pair_exchange(grads):
    ng = len(grads)

    def body(*refs):
        start, finish = _pair_ops(refs[:ng], refs[ng:2 * ng], *refs[2 * ng:])
        start()
        finish()

    out_shape, sems = _pair_shapes(grads)
    return pl.pallas_call(
        body, name="pair_exchange", in_specs=[ANY] * ng, out_specs=[ANY] * ng,
        out_shape=out_shape, scratch_shapes=sems,
    )(*grads)


def _chip_ops(s_refs, r_refs, ssem, rsem):
    def copies():
        x, y, c, chips = _place()
        return [_remote(s.at[2 * cx + cy], r.at[2 * x + y], ssem.at[3 * t + j], rsem.at[3 * t + j], (cx, cy, c))
                for t, (s, r) in enumerate(zip(s_refs, r_refs)) for j, (cx, cy) in enumerate(chips)]

    def start():
        for cp in copies():
            cp.start()

    def finish():
        x, y, c, chips = _place()
        for t, r in enumerate(r_refs):
            for j, (cx, cy) in enumerate(chips):
                slot = r.at[2 * cx + cy]
                _remote(slot, slot, ssem.at[3 * t + j], rsem.at[3 * t + j], (cx, cy, c)).wait_recv()
        for cp in copies():
            cp.wait_send()

    return start, finish


def _chip_shapes(sums):
    return ([jax.ShapeDtypeStruct(s.shape, s.dtype) for s in sums],
            [pltpu.SemaphoreType.DMA((3 * len(sums),))] * 2)


def _sibling_share(halves, pack):
    ng = len(halves)

    def body(*refs):
        h_refs, pack_ref = refs[:ng], refs[ng]
        f_refs, allp_ref = refs[ng + 1:2 * ng + 1], refs[2 * ng + 1]
        ssem, rsem, psend, precv, lsem = refs[2 * ng + 2:]
        x, y, c, _ = _place()
        me = 4 * x + 2 * y + c
        local = pltpu.make_async_copy(pack_ref, allp_ref.at[me], lsem)
        local.start()
        flips = [(fx, fy, fc) for fx in (0, 1) for fy in (0, 1) for fc in (0, 1)][1:]
        peers = [(fx + x - 2 * fx * x, fy + y - 2 * fy * y, fc + c - 2 * fc * c) for fx, fy, fc in flips]
        sends = [_remote(pack_ref, allp_ref.at[me], psend.at[m], precv.at[m], peer) for m, peer in enumerate(peers)]
        sends += [_remote(h, f, ssem.at[t], rsem.at[t], (x, y, 1 - c))
                  for t, (h, f) in enumerate(zip(h_refs, f_refs))]
        for cp in sends:
            cp.start()
        for t, f in enumerate(f_refs):
            _remote(f, f, ssem.at[t], rsem.at[t], (x, y, 1 - c)).wait_recv()
        for m, (px, py, pc) in enumerate(peers):
            row = allp_ref.at[4 * px + 2 * py + pc]
            _remote(row, row, psend.at[m], precv.at[m], (px, py, pc)).wait_recv()
        for cp in sends:
            cp.wait_send()
        local.wait()

    return pl.pallas_call(
        body, name="sibling_share", in_specs=[ANY] * (ng + 1), out_specs=[ANY] * (ng + 1),
        out_shape=[jax.ShapeDtypeStruct(h.shape, F32) for h in halves]
        + [jax.ShapeDtypeStruct((8,) + pack.shape, F32)],
        scratch_shapes=[pltpu.SemaphoreType.DMA((ng,)), pltpu.SemaphoreType.DMA((ng,)),
                        pltpu.SemaphoreType.DMA((7,)), pltpu.SemaphoreType.DMA((7,)), pltpu.SemaphoreType.DMA],
    )(*halves, pack)


def _pair_add(own, recv, cidx, name):
    R = own.shape[2]
    tr = min(256, R)

    def body(c_ref, a_ref, b_ref, o_ref):
        o_ref[...] = (a_ref[...] + b_ref[...]).astype(BF16)

    return pl.pallas_call(
        body, name=name,
        grid_spec=pltpu.PrefetchScalarGridSpec(
            num_scalar_prefetch=1, grid=(NPROJ, R // tr),
            in_specs=[pl.BlockSpec((None, None, tr, D), lambda n, r, c: (n, c[0], r, 0)),
                      pl.BlockSpec((None, tr, D), lambda n, r, c: (n, r, 0))],
            out_specs=pl.BlockSpec((None, tr, D), lambda n, r, c: (n, r, 0))),
        out_shape=jax.ShapeDtypeStruct(recv.shape, BF16),
        compiler_params=_cparams(("parallel", "parallel")),
    )(cidx, own, recv)


def _chip_sum(sums, parts, bidx, name):
    R = parts.shape[1]
    tr = min(256, R)

    def body(b_ref, s_ref, p_ref, o_ref):
        acc = None
        for j in range(NPROJ):
            term = jnp.where(b_ref[0] == j, s_ref[...], p_ref[j]).astype(F32)
            acc = term if acc is None else acc + term
        o_ref[...] = acc

    return pl.pallas_call(
        body, name=name,
        grid_spec=pltpu.PrefetchScalarGridSpec(
            num_scalar_prefetch=1, grid=(R // tr,),
            in_specs=[pl.BlockSpec((None, tr, D), lambda r, b: (b[0], r, 0)),
                      pl.BlockSpec((NPROJ, tr, D), lambda r, b: (0, r, 0))],
            out_specs=pl.BlockSpec((tr, D), lambda r, b: (r, 0))),
        out_shape=jax.ShapeDtypeStruct((R, D), F32),
        compiler_params=_cparams(("parallel",)),
    )(bidx, sums, parts)


def _adamw_math(w, g, m, v):
    m = ADAM_B1 * m + (1.0 - ADAM_B1) * g
    v = ADAM_B2 * v + (1.0 - ADAM_B2) * (g * g)
    m_hat = m / (1.0 - ADAM_B1 ** ADAM_STEP)
    v_hat = v / (1.0 - ADAM_B2 ** ADAM_STEP)
    delta = -ADAM_LR * (m_hat / (jnp.sqrt(v_hat) + ADAM_EPS) + ADAM_WD * w)
    return delta, m, v


def _adamw(w, mine, theirs, m, v, cidx, name):
    R = mine.shape[0]
    tr = min(256, R)
    nr = R // tr

    def body(c_ref, w_ref, a_ref, b_ref, m_ref, v_ref, g_ref, d_ref, nm_ref, nv_ref):
        g = jnp.where(pl.program_id(0) == c_ref[0], a_ref[...], b_ref[...])
        g_ref[...] = g
        d_ref[...], nm_ref[...], nv_ref[...] = _adamw_math(w_ref[...], g, m_ref[...], v_ref[...])

    full = pl.BlockSpec((tr, D), lambda h, r, c: (h * nr + r, 0))
    half = pl.BlockSpec((tr, D), lambda h, r, c: (r, 0))
    return pl.pallas_call(
        body, name=name,
        grid_spec=pltpu.PrefetchScalarGridSpec(
            num_scalar_prefetch=1, grid=(2, nr),
            in_specs=[full, half, half, full, full], out_specs=[full] * 4),
        out_shape=[jax.ShapeDtypeStruct(w.shape, F32)] * 4,
        compiler_params=_cparams(("parallel", "parallel")),
    )(cidx, w, mine, theirs, m, v)


PACK_ROWS = 8


def _small_update(allp, bidx, logits, weights, moments_m, moments_v):
    shapes = [w.shape for w in weights]
    q4 = D // NPROJ

    def body(b_ref, allp_ref, hgp_ref, lg_ref, *refs):
        w_refs, m_refs, v_refs = refs[0:6], refs[6:12], refs[12:18]
        loss_ref = refs[18]
        g_out, d_out, m_out, v_out = refs[19:25], refs[25:31], refs[31:37], refs[37:43]

        def total(ref, row, lo, hi):
            acc = ref[0, row:row + 1, lo:hi]
            for dev in range(1, 8):
                acc = acc + ref[dev, row:row + 1, lo:hi]
            return acc

        _, pp = _lower_bound(lg_ref[...])
        dlb = total(allp_ref, 2, 0, D)
        grads = [total(allp_ref, 0, 0, D), total(allp_ref, 4, 0, HD), total(allp_ref, 4, HD, 2 * HD),
                 total(hgp_ref, 1, 0, q4), total(allp_ref, 4, 2 * HD, 3 * HD), None]
        loss_ref[...] = (0.5 / D) * jnp.sum(total(allp_ref, 3, 0, D), axis=1, keepdims=True)
        for t in range(6):
            if t < 5:
                rows = [(slice(None), grads[t])]
            else:
                rows = [(slice(0, 1), -pp * dlb), (slice(1, 2), pp * dlb)]
            for rs, g in rows:
                g_out[t][rs, :] = g
                d_out[t][rs, :], m_out[t][rs, :], v_out[t][rs, :] = _adamw_math(
                    w_refs[t][rs, :], g, m_refs[t][rs, :], v_refs[t][rs, :])

    whole = [pl.BlockSpec(s, lambda i, b: (0, 0)) for s in shapes]
    return pl.pallas_call(
        body, name="small_update",
        grid_spec=pltpu.PrefetchScalarGridSpec(
            num_scalar_prefetch=1, grid=(1,),
            in_specs=[pl.BlockSpec((8, PACK_ROWS, D), lambda i, b: (0, 0, 0)),
                      pl.BlockSpec((8, PACK_ROWS, q4), lambda i, b: (0, 0, b[0])),
                      pl.BlockSpec((2, D), lambda i, b: (0, 0))] + whole * 3,
            out_specs=[pl.BlockSpec((1, 1), lambda i, b: (0, 0))] + whole * 4),
        out_shape=[jax.ShapeDtypeStruct((1, 1), F32)] + [jax.ShapeDtypeStruct(s, F32) for s in shapes] * 4,
        compiler_params=_cparams(("arbitrary",)),
    )(bidx, allp, allp, logits, *weights, *moments_m, *moments_v)


def kernel(x, sb_norm, sb_w_in, sb_q_gain, sb_k_gain, sb_w_out, hg_norm, hg_w_in, hg_o_gain, hg_w_out, hg_lb_logits, loss_target, m_sb_norm, m_sb_w_in, m_sb_q_gain, m_sb_k_gain, m_sb_w_out, m_hg_norm, m_hg_w_in, m_hg_o_gain, m_hg_w_out, m_hg_lb_logits, v_sb_norm, v_sb_w_in, v_sb_q_gain, v_sb_k_gain, v_sb_w_out, v_hg_norm, v_hg_w_in, v_hg_o_gain, v_hg_w_out, v_hg_lb_logits):
    Bl, S, _ = x.shape
    T = Bl * S
    cidx = lax.axis_index("c").astype(jnp.int32).reshape(1)
    bidx = (2 * lax.axis_index("x") + lax.axis_index("y")).astype(jnp.int32).reshape(1)

    def in_hbm(arrays):
        return [pltpu.with_memory_space_constraint(a, pltpu.HBM) for a in arrays]

    def own_slot(gathered, mine):
        return lax.dynamic_update_slice(gathered, mine[None], (bidx[0],) + (0,) * mine.ndim)

    def halved(g):
        return g.reshape(NPROJ, 2, g.shape[-2] * g.shape[0] // (2 * NPROJ), D)

    def heads(a):
        return a.reshape(a.shape[:-2] + (Bl, S, D))

    def flat(a):
        return a.reshape(a.shape[:-3] + (T, D))

    wp_first = sb_w_in[0].astype(BF16)
    wp_rest = jnp.concatenate([hg_w_in[0], sb_w_out[0], hg_w_out[0]], axis=0).astype(BF16)
    wall_first, = in_hbm([own_slot(_gather_weights(wp_first), wp_first)])
    x2 = x.reshape(T, D)
    tgt = loss_target.reshape(T, D)

    proj0 = _in_proj_fwd(x2, sb_norm, wall_first, W_IN, "sb_in_fwd")
    o0, ctot, wall_rest, hnall = _sb_fwd(heads(proj0), sb_q_gain, sb_k_gain, wp_rest, hg_norm)
    wall_rest, = in_hbm([own_slot(wall_rest, wp_rest)])
    hgn = own_slot(hnall, hg_norm).reshape(1, D)
    h1 = _out_proj_fwd(flat(o0), proj0, x2, wall_rest, W_OUT_SB, "sb_out_fwd")
    proj1 = _in_proj_fwd(h1, hgn, wall_rest, W_IN, "hg_in_fwd")
    o1, states = _hg_fwd(heads(proj1), hg_lb_logits)
    dh2, loss_terms = _out_proj_fwd(flat(o1), proj1, h1, wall_rest, W_OUT_HG, "hg_out_fwd",
                                    o_gain=hg_o_gain, target=tgt)

    do1, dproj1, gout_hg, d_ogain = _out_proj_bwd(dh2, flat(o1), proj1, wall_rest, W_OUT_HG, "hg_out_bwd",
                                                  o_gain=hg_o_gain)
    dproj1, dlb = _hg_bwd(heads(proj1), states, heads(do1), heads(dproj1), hg_lb_logits)
    dproj1 = flat(dproj1)
    dh1, d_hgn = _in_proj_bwd_x(dproj1, wall_rest, W_IN, h1, hgn, dh2, "hg_in_bwd_x")
    big_hg = [halved(_in_proj_bwd_w(dproj1, h1, hgn, "hg_in_bwd_w")), halved(gout_hg)]
    do0, dproj0, gout_sb, *recv_hg = _out_proj_bwd(dh1, flat(o0), proj0, wall_rest, W_OUT_SB, "sb_out_bwd",
                                                   exchange=big_hg)
    sums_hg = in_hbm(_pair_add(g, r, cidx, "pair_add_" + nm)
                     for g, r, nm in zip(big_hg, in_hbm(recv_hg), ("hg_in", "hg_out")))
    dproj0, d_qg, d_kg, *parts_hg = _sb_bwd(heads(proj0), ctot, heads(do0), heads(dproj0),
                                            sb_q_gain, sb_k_gain, sums_hg)
    dproj0 = flat(dproj0)

    big_sb = [halved(_in_proj_bwd_w(dproj0, x2, sb_norm, "sb_in_bwd_w")), halved(gout_sb)]
    sums_sb = in_hbm(_pair_add(g, r, cidx, "pair_add_" + nm)
                     for g, r, nm in zip(big_sb, in_hbm(_pair_exchange(big_sb)), ("sb_in", "sb_out")))
    grad_x, d_sbn, *parts_sb = _in_proj_bwd_x(dproj0, wall_first, W_IN, x2, sb_norm, dh1, "sb_in_bwd_x",
                                              exchange=sums_sb)

    names = ["sb_in", "hg_in", "sb_out", "hg_out"]
    sums = [sums_sb[0], sums_hg[0], sums_sb[1], sums_hg[1]]
    parts = [parts_sb[0], parts_hg[0], parts_sb[1], parts_hg[1]]
    halves = in_hbm(_chip_sum(sm, p, bidx, "chip_sum_" + nm) for sm, p, nm in zip(sums, in_hbm(parts), names))
    gains = jnp.concatenate([d_qg, d_kg, d_ogain, jnp.zeros((1, D - 3 * HD), F32)], axis=1)
    pack = jnp.concatenate([d_sbn, d_hgn, dlb, loss_terms, gains, jnp.zeros((3, D), F32)], axis=0)
    *theirs, allp = _sibling_share(halves, pack)
    theirs = in_hbm(theirs)

    big_w = [sb_w_in, hg_w_in, sb_w_out, hg_w_out]
    big_m = [m_sb_w_in, m_hg_w_in, m_sb_w_out, m_hg_w_out]
    big_v = [v_sb_w_in, v_hg_w_in, v_sb_w_out, v_hg_w_out]
    upd = [_adamw(w[0], a, b, m[0], v[0], cidx, "adamw_" + nm)
           for w, a, b, m, v, nm in zip(big_w, halves, theirs, big_m, big_v, names)]
    (g_sb_in, d_sb_in, nm_sb_in, nv_sb_in), (g_hg_in, d_hg_in, nm_hg_in, nv_hg_in), \
        (g_sb_out, d_sb_out, nm_sb_out, nv_sb_out), (g_hg_out, d_hg_out, nm_hg_out, nv_hg_out) = [
            tuple(a[None] for a in u) for u in upd]

    small = _small_update(
        allp, bidx, hg_lb_logits,
        [sb_norm, sb_q_gain, sb_k_gain, hg_norm, hg_o_gain, hg_lb_logits],
        [m_sb_norm, m_sb_q_gain, m_sb_k_gain, m_hg_norm, m_hg_o_gain, m_hg_lb_logits],
        [v_sb_norm, v_sb_q_gain, v_sb_k_gain, v_hg_norm, v_hg_o_gain, v_hg_lb_logits])
    loss = small[0].reshape(())
    (g_sbn, g_qg, g_kg, g_hgn, g_og, g_lb) = small[1:7]
    (d_sbn2, d_qg2, d_kg2, d_hgn2, d_og2, d_lb2) = small[7:13]
    (nm_sbn, nm_qg, nm_kg, nm_hgn, nm_og, nm_lb) = small[13:19]
    (nv_sbn, nv_qg, nv_kg, nv_hgn, nv_og, nv_lb) = small[19:25]

    return (loss, grad_x.reshape(Bl, S, D),
            g_sbn, g_sb_in, g_qg, g_kg, g_sb_out, g_hgn, g_hg_in, g_og, g_hg_out, g_lb,
            d_sbn2, d_sb_in, d_qg2, d_kg2, d_sb_out, d_hgn2, d_hg_in, d_og2, d_hg_out, d_lb2,
            nm_sbn, nm_sb_in, nm_qg, nm_kg, nm_sb_out, nm_hgn, nm_hg_in, nm_og, nm_hg_out, nm_lb,
            nv_sbn, nv_sb_in, nv_qg, nv_kg, nv_sb_out, nv_hgn, nv_hg_in, nv_og, nv_hg_out, nv_lb)
```

```python
import functools

import jax
import jax.numpy as jnp
from jax import lax
from jax.experimental import pallas as pl
from jax.experimental.pallas import tpu as pltpu

F32 = jnp.float32
BF16 = jnp.bfloat16
MESH = pl.DeviceIdType.MESH
ANY = pl.BlockSpec(memory_space=pl.ANY)

D = 1024
HEADS = 8
HD = 128
NPROJ = 4
RMS_EPS = 1e-6
TK = 256
CH = 64
CH_LOG2 = 6
GR = 128
SCALE = HD ** -0.5
EXP_CLAMP = 60.0
W_IN, W_OUT_SB, W_OUT_HG = 0, 4, 5

ADAM_LR = 0.001
ADAM_B1 = 0.9
ADAM_B2 = 0.999
ADAM_EPS = 1e-08
ADAM_WD = 0.01
ADAM_STEP = 10

NT = (((1,), (1,)), ((), ()))
TN = (((0,), (0,)), ((), ()))
MIB = 1024 * 1024


def _cparams(sem=None, vmem_mib=40):
    return pltpu.CompilerParams(dimension_semantics=sem, vmem_limit_bytes=vmem_mib * MIB)


def _dot(a, b, dims=None):
    if dims is None:
        return jnp.dot(a, b, preferred_element_type=F32)
    return lax.dot_general(a, b, dims, preferred_element_type=F32)


def _sigmoid(x):
    return 1.0 / (1.0 + jnp.exp(-x))


def _rms(x):
    return lax.rsqrt(jnp.mean(x * x, axis=-1, keepdims=True) + RMS_EPS)


def _rms_bwd(x, r, gain, dy):
    a = dy * gain
    dx = r * a - x * (r * r * r) * jnp.mean(x * a, axis=-1, keepdims=True)
    return dx, dy * (x * r)


def _split2(v):
    hi = v.astype(BF16)
    lo = (v - hi.astype(F32)).astype(BF16)
    return hi, lo


def _cum2(v, u):
    hi, lo = _split2(v)
    return _dot(hi, u) + _dot(lo, u)


def _dot3(a, b, dims=None):
    ah, al = _split2(a)
    bh, bl = _split2(b)
    return _dot(ah, bh, dims) + _dot(ah, bl, dims) + _dot(al, bh, dims)


def _cum2l(u, v):
    hi, lo = _split2(v)
    return _dot(u, hi) + _dot(u, lo)


def _in_proj_fwd(h, gain, wall, wblk, name):
    T = h.shape[0]
    tm = min(1024, T)

    def body(h_ref, g_ref, w_ref, o_ref, u_s):
        rows = pl.ds(pl.multiple_of(pl.program_id(1) * tm, tm), tm)

        @pl.when(pl.program_id(0) == 0)
        def _():
            x = h_ref[...]
            u_s[rows, :] = (x * _rms(x) * g_ref[...]).astype(BF16)

        o_ref[...] = _dot(u_s[rows, :], w_ref[...])

    return pl.pallas_call(
        body, name=name, grid=(NPROJ, T // tm),
        in_specs=[pl.BlockSpec((tm, D), lambda n, i: (jnp.where(n == 0, i, 0), 0)),
                  pl.BlockSpec((1, D), lambda n, i: (0, 0)),
                  pl.BlockSpec((None, D, D), lambda n, i: (n, wblk, 0))],
        out_specs=pl.BlockSpec((None, tm, D), lambda n, i: (n, i, 0)),
        out_shape=jax.ShapeDtypeStruct((NPROJ, T, D), F32),
        scratch_shapes=[pltpu.VMEM((T, D), BF16)],
        compiler_params=_cparams(("arbitrary", "arbitrary")),
    )(h, gain, wall)


def _head_norm(x):
    outs = []
    for hh in range(x.shape[1] // HD):
        xs = x[:, hh * HD:(hh + 1) * HD]
        outs.append((xs, _rms(xs)))
    return outs


def _w_out_specs(wblk):
    kb = D // NPROJ
    return [pl.BlockSpec((None, kb, D), functools.partial(lambda j, i: (j, wblk, 0), j)) for j in range(NPROJ)]


def _out_proj_fwd(o, proj, resid, wall, wblk, name, o_gain=None, target=None):
    T = o.shape[0]
    tm = min(512, T)
    kb = D // NPROJ
    with_loss = target is not None

    def body(*refs):
        o_ref, g_ref, r_ref = refs[:3]
        w_refs = refs[3:3 + NPROJ]
        if with_loss:
            og_ref, t_ref, dh_ref, ls_ref = refs[3 + NPROJ:]
        else:
            h_ref, = refs[3 + NPROJ:]
        x = o_ref[...]
        if with_loss:
            x = jnp.concatenate([xs * r * og_ref[...] for xs, r in _head_norm(x)], axis=1)
        g = g_ref[...]
        a = (x * (g * _sigmoid(g))).astype(BF16)
        hnew = r_ref[...]
        for j in range(NPROJ):
            hnew = hnew + _dot(a[:, j * kb:(j + 1) * kb], w_refs[j][...])
        if with_loss:
            err = hnew - t_ref[...]
            dh_ref[...] = err * (1.0 / D)
            part = jnp.sum(err * err, axis=0, keepdims=True)

            @pl.when(pl.program_id(0) == 0)
            def _():
                ls_ref[...] = part

            @pl.when(pl.program_id(0) != 0)
            def _():
                ls_ref[...] += part
        else:
            h_ref[...] = hnew

    tile = pl.BlockSpec((tm, D), lambda i: (i, 0))
    in_specs = [tile, pl.BlockSpec((None, tm, D), lambda i: (3, i, 0)), tile] + _w_out_specs(wblk)
    args = [o, proj, resid] + [wall] * NPROJ
    out_specs = tile
    out_shape = jax.ShapeDtypeStruct((T, D), F32)
    if with_loss:
        in_specs += [pl.BlockSpec((1, HD), lambda i: (0, 0)), tile]
        args += [o_gain, target]
        out_specs = [tile, pl.BlockSpec((1, D), lambda i: (0, 0))]
        out_shape = [out_shape, jax.ShapeDtypeStruct((1, D), F32)]
    return pl.pallas_call(
        body, name=name, grid=(T // tm,), in_specs=in_specs, out_specs=out_specs,
        out_shape=out_shape, compiler_params=_cparams(("arbitrary",)),
    )(*args)


def _out_proj_bwd(dy, o, proj, wall, wblk, name, o_gain=None, exchange=()):
    T = o.shape[0]
    tm = min(512, T)
    kb = D // NPROJ
    normed = o_gain is not None
    ne = len(exchange)

    def body(*refs):
        it = iter(refs)
        dy_ref, o_ref, g_ref = (next(it) for _ in range(3))
        w_refs = [next(it) for _ in range(NPROJ)]
        og_ref = next(it) if normed else None
        xg_refs = [next(it) for _ in range(ne)]
        do_ref, dg_ref, dw_ref = (next(it) for _ in range(3))
        dgain_ref = next(it) if normed else None
        xr_refs = [next(it) for _ in range(ne)]
        wt_s = next(it)
        first = pl.program_id(0) == 0
        if ne:
            start, finish = _pair_ops(xg_refs, xr_refs, next(it), next(it))
            pl.when(first)(start)

        @pl.when(first)
        def _():
            for j in range(NPROJ):
                wt_s[:, j * kb:(j + 1) * kb] = w_refs[j][...].T
        g = g_ref[...]
        s = _sigmoid(g)
        sl = g * s
        x = o_ref[...]
        if normed:
            heads = _head_norm(x)
            on = jnp.concatenate([xs * r * og_ref[...] for xs, r in heads], axis=1)
        else:
            on = x
        dyb = dy_ref[...].astype(BF16)
        a = (on * sl).astype(BF16)
        for j in range(NPROJ):
            part = _dot(a[:, j * kb:(j + 1) * kb], dyb, TN)

            @pl.when(first)
            def _():
                dw_ref[j] = part

            @pl.when(jnp.logical_not(first))
            def _():
                dw_ref[j] += part

        da = _dot(dyb, wt_s[...])
        d_on = da * sl
        dg_ref[...] = (da * on * (s * (1.0 + g * (1.0 - s)))).astype(BF16)
        if normed:
            dxs, gsum = [], None
            for hh, (xs, r) in enumerate(heads):
                dx, gt = _rms_bwd(xs, r, og_ref[...], d_on[:, hh * HD:(hh + 1) * HD])
                dxs.append(dx)
                gt = jnp.sum(gt, axis=0, keepdims=True)
                gsum = gt if gsum is None else gsum + gt
            do_ref[...] = jnp.concatenate(dxs, axis=1).astype(BF16)

            @pl.when(first)
            def _():
                dgain_ref[...] = gsum

            @pl.when(jnp.logical_not(first))
            def _():
                dgain_ref[...] += gsum
        else:
            do_ref[...] = d_on.astype(BF16)
        if ne:
            pl.when(pl.program_id(0) == T // tm - 1)(finish)

    tile = pl.BlockSpec((tm, D), lambda i: (i, 0))
    gate = pl.BlockSpec((None, tm, D), lambda i: (3, i, 0))
    in_specs = [tile, tile, gate] + _w_out_specs(wblk)
    args = [dy, o, proj] + [wall] * NPROJ
    out_specs = [tile, gate, pl.BlockSpec((NPROJ, kb, D), lambda i: (0, 0, 0))]
    out_shape = [jax.ShapeDtypeStruct((T, D), BF16),
                 jax.ShapeDtypeStruct((NPROJ, T, D), BF16),
                 jax.ShapeDtypeStruct((NPROJ, kb, D), F32)]
    if normed:
        in_specs.append(pl.BlockSpec((1, HD), lambda i: (0, 0)))
        args.append(o_gain)
        out_specs.append(pl.BlockSpec((1, HD), lambda i: (0, 0)))
        out_shape.append(jax.ShapeDtypeStruct((1, HD), F32))
    x_shape, x_sems = _pair_shapes(exchange) if ne else ([], [])
    return pl.pallas_call(
        body, name=name, grid=(T // tm,), in_specs=in_specs + [ANY] * ne, out_specs=out_specs + [ANY] * ne,
        out_shape=out_shape + x_shape, scratch_shapes=[pltpu.VMEM((D, D), BF16)] + x_sems,
        compiler_params=_cparams(("arbitrary",), vmem_mib=48),
    )(*args, *exchange)


def _in_proj_bwd_x(dproj, wall, wblk, h, gain, dres, name, exchange=()):
    T = h.shape[0]
    tm = min(512, T)
    ne = len(exchange)

    def body(d_ref, w_ref, h_ref, g_ref, r_ref, *refs):
        xs_refs, (dh_ref, dgain_ref), xr_refs = refs[:ne], refs[ne:ne + 2], refs[ne + 2:2 * ne + 2]
        du, wt_s = refs[2 * ne + 2:2 * ne + 4]
        i, n = pl.program_id(0), pl.program_id(1)
        if ne:
            start, finish = _chip_ops(xs_refs, xr_refs, *refs[2 * ne + 4:])
            pl.when(jnp.logical_and(i == 0, n == 0))(start)

        @pl.when(i == 0)
        def _():
            wt_s[n] = w_ref[...].T

        part = _dot(d_ref[...], wt_s[n])

        @pl.when(n == 0)
        def _():
            du[...] = part

        @pl.when(n != 0)
        def _():
            du[...] += part

        @pl.when(n == NPROJ - 1)
        def _():
            x = h_ref[...]
            dx, gt = _rms_bwd(x, _rms(x), g_ref[...], du[...])
            dh_ref[...] = r_ref[...] + dx
            gt = jnp.sum(gt, axis=0, keepdims=True)

            @pl.when(i == 0)
            def _():
                dgain_ref[...] = gt

            @pl.when(i != 0)
            def _():
                dgain_ref[...] += gt

        if ne:
            pl.when(jnp.logical_and(i == T // tm - 1, n == NPROJ - 1))(finish)

    x_shape, x_sems = _chip_shapes(exchange) if ne else ([], [])
    return pl.pallas_call(
        body, name=name, grid=(T // tm, NPROJ),
        in_specs=[pl.BlockSpec((None, tm, D), lambda i, n: (n, i, 0)),
                  pl.BlockSpec((None, D, D), lambda i, n: (jnp.where(i == 0, n, NPROJ - 1), wblk, 0)),
                  pl.BlockSpec((tm, D), lambda i, n: (i, 0)),
                  pl.BlockSpec((1, D), lambda i, n: (0, 0)),
                  pl.BlockSpec((tm, D), lambda i, n: (i, 0))] + [ANY] * ne,
        out_specs=[pl.BlockSpec((tm, D), lambda i, n: (i, 0)),
                   pl.BlockSpec((1, D), lambda i, n: (0, 0))] + [ANY] * ne,
        out_shape=[jax.ShapeDtypeStruct((T, D), F32), jax.ShapeDtypeStruct((1, D), F32)] + x_shape,
        scratch_shapes=[pltpu.VMEM((tm, D), F32), pltpu.VMEM((NPROJ, D, D), BF16)] + x_sems,
        compiler_params=_cparams(("arbitrary", "arbitrary")),
    )(dproj, wall, h, gain, dres, *exchange)


def _in_proj_bwd_w(dproj, h, gain, name):
    T = h.shape[0]
    tk = min(1024, T)

    def body(d_ref, h_ref, g_ref, dw_ref, ut_s):
        k = pl.program_id(1)

        @pl.when(pl.program_id(0) == 0)
        def _():
            x = h_ref[...]
            ut_s[k] = (x * _rms(x) * g_ref[...]).astype(BF16).T

        part = _dot(ut_s[k], d_ref[...])

        @pl.when(k == 0)
        def _():
            dw_ref[...] = part

        @pl.when(k != 0)
        def _():
            dw_ref[...] += part

    return pl.pallas_call(
        body, name=name, grid=(NPROJ, T // tk),
        in_specs=[pl.BlockSpec((None, tk, D), lambda n, k: (n, k, 0)),
                  pl.BlockSpec((tk, D), lambda n, k: (jnp.where(n == 0, k, 0), 0)),
                  pl.BlockSpec((1, D), lambda n, k: (0, 0))],
        out_specs=pl.BlockSpec((None, D, D), lambda n, k: (n, 0, 0)),
        out_shape=jax.ShapeDtypeStruct((NPROJ, D, D), F32),
        scratch_shapes=[pltpu.VMEM((T // tk, D, tk), BF16)],
        compiler_params=_cparams(("arbitrary", "arbitrary")),
    )(dproj, h, gain)


def _log_sigmoid_pair(z):
    lb = jnp.minimum(z, 0.0) - jnp.log(1.0 + jnp.exp(-jnp.abs(z)))
    return lb, lb - z


def _slab_consts():
    t = lax.broadcasted_iota(jnp.int32, (TK, TK), 0)
    s = lax.broadcasted_iota(jnp.int32, (TK, TK), 1)
    return s < t, (t > s).astype(BF16), (t < s).astype(BF16)


def _slab_rows(k0, S):
    return [(r0, r1, masked) for r0, r1, masked in ((k0, k0 + TK, True), (k0 + TK, S, False)) if r0 < r1]


def _sb_fwd(proj, q_gain, k_gain, wp, hn):
    _, Bl, S, _ = proj.shape
    steps = Bl * HEADS

    def body(q_ref, k_ref, v_ref, qg_ref, kg_ref, wp_ref, hn_ref, o_ref, ct_ref, wall_ref, hnall_ref,
             qn, kn, vb, ssem, rsem):
        step = pl.program_id(0) * HEADS + pl.program_id(1)
        start, forward, finish = _gather_ops(wp_ref, wall_ref, ssem, rsem, hn_ref, hnall_ref)
        pl.when(step == 0)(start)
        pl.when(step == steps // 2)(forward)
        q = q_ref[...]
        qn[...] = (q * _rms(q) * (qg_ref[...] * SCALE)).astype(BF16)
        k = k_ref[...]
        kn[...] = (k * _rms(k) * kg_ref[...]).astype(BF16)
        vb[...] = v_ref[...].astype(BF16)
        tri, u_gt, _ = _slab_consts()
        nb = S // TK
        c_blk = [jnp.zeros((TK, 1), F32)] * nb
        o_blk = [jnp.zeros((TK, HD), F32)] * nb
        for k0 in reversed(range(0, S, TK)):
            kb, vbb = kn[k0:k0 + TK, :], vb[k0:k0 + TK, :]
            for r0, r1, masked in _slab_rows(k0, S):
                blocks = range(r0 // TK, r1 // TK)
                z = _dot(qn[r0:r1, :], kb, NT)
                lb, ls = _log_sigmoid_pair(z)
                if masked:
                    ls = jnp.where(tri, ls, 0.0)
                c = jnp.concatenate([c_blk[b] for b in blocks], axis=0)
                w = jnp.exp(lb + _cum2(ls, u_gt) + c)
                if masked:
                    w = jnp.where(tri, w, 0.0)
                o_new = _dot(w.astype(BF16), vbb)
                c_new = jnp.sum(ls, axis=1, keepdims=True)
                for i, b in enumerate(blocks):
                    o_blk[b] = o_blk[b] + o_new[i * TK:(i + 1) * TK]
                    c_blk[b] = c_blk[b] + c_new[i * TK:(i + 1) * TK]
        o_ref[...] = jnp.concatenate(o_blk, axis=0)
        ct_ref[...] = jnp.concatenate(c_blk, axis=0)
        pl.when(step == steps - 1)(finish)

    def slot(n):
        return pl.BlockSpec((None, None, S, HD), lambda b, h: (n, b, 0, h))

    return pl.pallas_call(
        body, name="sb_fwd", grid=(Bl, HEADS),
        in_specs=[slot(0), slot(1), slot(2),
                  pl.BlockSpec((1, HD), lambda b, h: (0, 0)),
                  pl.BlockSpec((1, HD), lambda b, h: (0, 0)), ANY, ANY],
        out_specs=[pl.BlockSpec((None, S, HD), lambda b, h: (b, 0, h)),
                   pl.BlockSpec((None, None, S, 1), lambda b, h: (b, h, 0, 0)), ANY, ANY],
        out_shape=[jax.ShapeDtypeStruct((Bl, S, D), F32),
                   jax.ShapeDtypeStruct((Bl, HEADS, S, 1), F32),
                   jax.ShapeDtypeStruct((NPROJ,) + wp.shape, BF16),
                   jax.ShapeDtypeStruct((NPROJ,) + hn.shape, F32)],
        scratch_shapes=[pltpu.VMEM((S, HD), BF16)] * 3 + [pltpu.SemaphoreType.DMA((GATHER_SEMS,))] * 2,
        compiler_params=_cparams(("arbitrary", "arbitrary"), vmem_mib=56),
    )(proj, proj, proj, q_gain, k_gain, wp, hn)


def _sb_bwd(proj, ctot, do, dproj, q_gain, k_gain, exchange):
    _, Bl, S, _ = proj.shape
    ne = len(exchange)

    def body(q_ref, k_ref, v_ref, ct_ref, do_ref, qg_ref, kg_ref, _, *refs):
        xs_refs, (dqkv_ref, dqg_ref, dkg_ref), xr_refs = refs[:ne], refs[ne:ne + 3], refs[ne + 3:2 * ne + 3]
        qn, kn, vb, dqn, dkn, dvn, ssem, rsem = refs[2 * ne + 3:]
        step = pl.program_id(0) * HEADS + pl.program_id(1)
        first = step == 0
        start, finish = _chip_ops(xs_refs, xr_refs, ssem, rsem)

        @pl.when(first)
        def _():
            start()
            dqg_ref[...] = jnp.zeros_like(dqg_ref)
            dkg_ref[...] = jnp.zeros_like(dkg_ref)

        q = q_ref[...]
        rq = _rms(q)
        qn[...] = (q * rq * (qg_ref[...] * SCALE)).astype(BF16)
        k = k_ref[...]
        rk = _rms(k)
        kn[...] = (k * rk * kg_ref[...]).astype(BF16)
        vb[...] = v_ref[...].astype(BF16)
        for acc in (dqn, dkn, dvn):
            acc[...] = jnp.zeros_like(acc)
        tri, u_gt, u_lt = _slab_consts()
        nb = S // TK
        passed, e = [jnp.zeros((TK, 1), F32)] * nb, [jnp.zeros((TK, 1), F32)] * nb
        for k0 in range(0, S, TK):
            keys = slice(k0, k0 + TK)
            kb, vbb = kn[keys, :], vb[keys, :]
            for r0, r1, masked in _slab_rows(k0, S):
                rows, blocks = slice(r0, r1), range(r0 // TK, r1 // TK)
                qb, dobb = qn[rows, :], do_ref[rows, :]
                z = _dot(qb, kb, NT)
                lb, ls = _log_sigmoid_pair(z)
                if masked:
                    ls = jnp.where(tri, ls, 0.0)
                p_new = jnp.concatenate([passed[b] for b in blocks], axis=0) + jnp.sum(ls, axis=1, keepdims=True)
                w = jnp.exp(lb + _cum2(ls, u_gt) + (ct_ref[rows, :] - p_new))
                if masked:
                    w = jnp.where(tri, w, 0.0)
                de = w * _dot(dobb, vbb, NT)
                e_old = jnp.concatenate([e[b] for b in blocks], axis=0)
                dls = e_old + _cum2(de, u_lt)
                e_new = e_old + jnp.sum(de, axis=1, keepdims=True)
                for i, b in enumerate(blocks):
                    passed[b], e[b] = p_new[i * TK:(i + 1) * TK], e_new[i * TK:(i + 1) * TK]
                sg = jnp.exp(lb)
                dz = de - sg * (de + dls)
                if masked:
                    dz = jnp.where(tri, dz, 0.0)
                dzb = dz.astype(BF16)
                dqn[rows, :] += _dot(dzb, kb)
                dkn[keys, :] += _dot(dzb, qb, TN)
                dvn[keys, :] += _dot(w.astype(BF16), dobb, TN)

        dx, gt = _rms_bwd(q, rq, qg_ref[...], dqn[...] * SCALE)
        dqkv_ref[0] = dx.astype(BF16)
        dqg_ref[...] += jnp.sum(gt, axis=0, keepdims=True)
        dx, gt = _rms_bwd(k, rk, kg_ref[...], dkn[...])
        dqkv_ref[1] = dx.astype(BF16)
        dkg_ref[...] += jnp.sum(gt, axis=0, keepdims=True)
        dqkv_ref[2] = dvn[...].astype(BF16)
        pl.when(step == Bl * HEADS - 1)(finish)

    def slot(n):
        return pl.BlockSpec((None, None, S, HD), lambda b, h: (n, b, 0, h))

    head = pl.BlockSpec((None, S, HD), lambda b, h: (b, 0, h))
    gain = pl.BlockSpec((1, HD), lambda b, h: (0, 0))
    x_shape, x_sems = _chip_shapes(exchange)
    return pl.pallas_call(
        body, name="sb_bwd", grid=(Bl, HEADS),
        in_specs=[slot(0), slot(1), slot(2),
                  pl.BlockSpec((None, None, S, 1), lambda b, h: (b, h, 0, 0)), head, gain, gain, ANY] + [ANY] * ne,
        out_specs=[pl.BlockSpec((3, None, S, HD), lambda b, h: (0, b, 0, h)), gain, gain] + [ANY] * ne,
        out_shape=[jax.ShapeDtypeStruct(dproj.shape, dproj.dtype),
                   jax.ShapeDtypeStruct((1, HD), F32), jax.ShapeDtypeStruct((1, HD), F32)] + x_shape,
        scratch_shapes=[pltpu.VMEM((S, HD), BF16)] * 3 + [pltpu.VMEM((S, HD), F32)] * 3 + x_sems,
        input_output_aliases={7: 0},
        compiler_params=_cparams(("arbitrary", "arbitrary"), vmem_mib=56),
    )(proj, proj, proj, ctot, do, q_gain, k_gain, dproj, *exchange)


def _lower_bound(logits):
    l0, l1 = logits[0:1, :], logits[1:2, :]
    m = jnp.maximum(l0, l1)
    e0, e1 = jnp.exp(l0 - m), jnp.exp(l1 - m)
    p0, p1 = e0 / (e0 + e1), e1 / (e0 + e1)
    return (p0 + p1) - p0, p0 * p1


def _hg_gates(qr, fp, lbv):
    sq = _sigmoid(qr)
    sp = _sigmoid(fp)
    sn = 1.0 / (1.0 + jnp.exp(fp))
    f = lbv + (1.0 - lbv) * sp
    return qr * sq, sq, sp, sn, f, (1.0 - lbv) * sn


def _group_consts():
    t = lax.broadcasted_iota(jnp.int32, (GR, GR), 0)
    j = lax.broadcasted_iota(jnp.int32, (GR, GR), 1)
    same = lax.shift_right_logical(t, CH_LOG2) == lax.shift_right_logical(j, CH_LOG2)
    tril = jnp.logical_and(same, j <= t)
    return (tril, tril.astype(BF16), jnp.logical_and(same, j >= t).astype(BF16), same.astype(BF16))


def _hg_decays(qa, k, f, t_inc, t_same):
    g = jnp.log(f)
    gc = _cum2l(t_inc, g)
    gl = _cum2l(t_same, g)
    gm = gc - 0.5 * gl
    e_q = jnp.exp(jnp.minimum(gm, EXP_CLAMP))
    e_k = jnp.exp(jnp.minimum(-gm, EXP_CLAMP))
    e_g = jnp.exp(gc)
    e_l = jnp.exp(gl - gc)
    return qa * e_q, k * e_k, qa * e_g, k * e_l, e_q, e_k, e_g, e_l, jnp.exp(gl)


def _hg_fwd(proj, lb_logits):
    _, Bl, S, _ = proj.shape
    nc = S // CH

    def body(q_ref, f_ref, i_ref, lg_ref, o_ref, st_ref, egl_s):
        lbv, _ = _lower_bound(lg_ref[...])
        tril, t_inc, _, t_same = _group_consts()
        st = jnp.zeros((HD, HD), F32)
        for g0 in range(0, S, GR):
            rs = slice(g0, g0 + GR)
            qa, _, _, _, f, k = _hg_gates(q_ref[rs, :], f_ref[rs, :], lbv)
            qt, kt, qg, kd, _, _, _, _, e_gl = _hg_decays(qa, k, f, t_inc, t_same)
            a = jnp.where(tril, _dot(qt.astype(BF16), kt.astype(BF16), NT), 0.0)
            ib, qgb, kdb = i_ref[rs, :].astype(BF16), qg.astype(BF16), kd.astype(BF16)
            within = _dot(a.astype(BF16), ib)
            egl_s[rs, :] = e_gl
            outs = []
            for l0 in range(0, GR, CH):
                ls = slice(l0, l0 + CH)
                st_ref[(g0 + l0) // CH] = st
                outs.append(within[ls] + _dot(qgb[ls], st.astype(BF16), NT))
                st = st * egl_s[g0 + l0:g0 + l0 + 1, :] + _dot(ib[ls], kdb[ls], TN)
            o_ref[rs, :] = jnp.concatenate(outs, axis=0)

    def slot(n):
        return pl.BlockSpec((None, None, S, HD), lambda b, h: (n, b, 0, h))

    return pl.pallas_call(
        body, name="hg_fwd", grid=(Bl, HEADS),
        in_specs=[slot(0), slot(1), slot(2), pl.BlockSpec((2, HD), lambda b, h: (0, h))],
        out_specs=[pl.BlockSpec((None, S, HD), lambda b, h: (b, 0, h)),
                   pl.BlockSpec((None, None, nc, HD, HD), lambda b, h: (b, h, 0, 0, 0))],
        out_shape=[jax.ShapeDtypeStruct((Bl, S, D), F32),
                   jax.ShapeDtypeStruct((Bl, HEADS, nc, HD, HD), F32)],
        scratch_shapes=[pltpu.VMEM((S, HD), F32)],
        compiler_params=_cparams(("parallel", "parallel")),
    )(proj, proj, proj, lb_logits)


def _hg_bwd(proj, states, do, dproj, lb_logits):
    _, Bl, S, _ = proj.shape

    def body(q_ref, f_ref, i_ref, st_ref, do_ref, lg_ref, _, dqfi_ref, dlb_ref, egl_s):
        lbv, _ = _lower_bound(lg_ref[...])
        tril, t_inc, t_dec, t_same = _group_consts()
        dst = jnp.zeros((HD, HD), F32)
        dlb = jnp.zeros((1, HD), F32)
        for g0 in reversed(range(0, S, GR)):
            rs = slice(g0, g0 + GR)
            qr, fp = q_ref[rs, :], f_ref[rs, :]
            qa, sq, sp, sn, f, k = _hg_gates(qr, fp, lbv)
            qt, kt, qg, kd, e_q, e_k, e_g, e_l, e_gl = _hg_decays(qa, k, f, t_inc, t_same)
            ib, dob, qgb, kdb = i_ref[rs, :].astype(BF16), do_ref[rs, :], qg.astype(BF16), kd.astype(BF16)
            egl_s[rs, :] = e_gl
            ab = jnp.where(tril, _dot(qt.astype(BF16), kt.astype(BF16), NT), 0.0).astype(BF16)
            da = jnp.where(tril, _dot(dob, ib, NT), 0.0)
            dqt = _dot3(da, kt)
            dkt = _dot3(da, qt, TN)
            di_within = _dot(ab, dob, TN)
            dqg, dkd, di, dse = [], [], [], []
            for l0 in reversed(range(0, GR, CH)):
                ls = slice(l0, l0 + CH)
                st = st_ref[(g0 + l0) // CH]
                dstb = dst.astype(BF16)
                dqg.insert(0, _dot(dob[ls], st.astype(BF16)))
                dkd.insert(0, _dot(ib[ls], dstb))
                di.insert(0, _dot(kdb[ls], dstb, NT))
                dse.insert(0, jnp.broadcast_to(jnp.sum(dst * st, axis=0, keepdims=True), (CH, HD)))
                dst = dst * egl_s[g0 + l0:g0 + l0 + 1, :] + _dot(dob[ls], qgb[ls], TN)
            dqg, dkd, di, dse = (jnp.concatenate(p, axis=0) for p in (dqg, dkd, di, dse))
            dqfi_ref[2, rs, :] = (di + di_within).astype(BF16)
            dgc = dqt * qt - dkt * kt + dqg * qg - dkd * kd
            dg = _cum2l(t_dec, dgc) + _cum2l(t_same, dkd * kd) + dse * e_gl
            t1 = dg / f - (dkt * e_k + dkd * e_l)
            dqfi_ref[1, rs, :] = ((1.0 - lbv) * t1 * sp * sn).astype(BF16)
            dqfi_ref[0, rs, :] = ((dqt * e_q + dqg * e_g) * (sq * (1.0 + qr * (1.0 - sq)))).astype(BF16)
            dlb = dlb + jnp.sum(sn * t1, axis=0, keepdims=True)

        @pl.when(pl.program_id(1) == 0)
        def _():
            dlb_ref[...] = dlb

        @pl.when(pl.program_id(1) != 0)
        def _():
            dlb_ref[...] += dlb

    def slot(n):
        return pl.BlockSpec((None, None, S, HD), lambda h, b: (n, b, 0, h))

    return pl.pallas_call(
        body, name="hg_bwd", grid=(HEADS, Bl),
        in_specs=[slot(0), slot(1), slot(2),
                  pl.BlockSpec((None, None, S // CH, HD, HD), lambda h, b: (b, h, 0, 0, 0)),
                  pl.BlockSpec((None, S, HD), lambda h, b: (b, 0, h)),
                  pl.BlockSpec((2, HD), lambda h, b: (0, h)), ANY],
        out_specs=[pl.BlockSpec((3, None, S, HD), lambda h, b: (0, b, 0, h)),
                   pl.BlockSpec((1, HD), lambda h, b: (0, h))],
        out_shape=[jax.ShapeDtypeStruct(dproj.shape, dproj.dtype), jax.ShapeDtypeStruct((1, D), F32)],
        scratch_shapes=[pltpu.VMEM((S, HD), F32)],
        input_output_aliases={6: 0},
        compiler_params=_cparams(("parallel", "arbitrary")),
    )(proj, proj, proj, states, do, lb_logits, dproj)


def _place():
    x, y, c = lax.axis_index("x"), lax.axis_index("y"), lax.axis_index("c")
    return x, y, c, [(1 - x, y), (x, 1 - y), (1 - x, 1 - y)]


def _remote(src, dst, ssem, rsem, dev):
    return pltpu.make_async_remote_copy(src_ref=src, dst_ref=dst, send_sem=ssem, recv_sem=rsem,
                                        device_id=dev, device_id_type=MESH)


GATHER_SEMS = 9


def _gather_ops(wp_ref, wall_ref, ssem, rsem, hn_ref=None, hnall_ref=None):
    half = wp_ref.shape[0] // 2

    def place():
        x, y, c, chips = _place()
        return x, y, c, chips, 2 * x + y, pl.ds(c * half, half), pl.ds((1 - c) * half, half)

    def first_sends():
        x, y, c, chips, b, mine, _ = place()
        cps = [_remote(wp_ref.at[mine], wall_ref.at[b, mine], ssem.at[j], rsem.at[j], (*chip, c))
               for j, chip in enumerate(chips)]
        if hn_ref is not None:
            cps += [_remote(hn_ref, hnall_ref.at[b], ssem.at[6 + j], rsem.at[6 + j], (*chip, c))
                    for j, chip in enumerate(chips)]
        return cps

    def forwards():
        x, y, c, chips, _, mine, _ = place()
        return [_remote(wall_ref.at[2 * cx + cy, mine], wall_ref.at[2 * cx + cy, mine],
                        ssem.at[3 + j], rsem.at[3 + j], (x, y, 1 - c)) for j, (cx, cy) in enumerate(chips)]

    def start():
        for cp in first_sends():
            cp.start()

    def forward():
        x, y, c, chips, _, mine, _ = place()
        for j, (cx, cy) in enumerate(chips):
            landed = wall_ref.at[2 * cx + cy, mine]
            _remote(landed, landed, ssem.at[j], rsem.at[j], (cx, cy, c)).wait_recv()
        for cp in forwards():
            cp.start()

    def finish():
        x, y, c, chips, _, _, other = place()
        for j, (cx, cy) in enumerate(chips):
            passed = wall_ref.at[2 * cx + cy, other]
            _remote(passed, passed, ssem.at[3 + j], rsem.at[3 + j], (x, y, 1 - c)).wait_recv()
            if hn_ref is not None:
                row = hnall_ref.at[2 * cx + cy]
                _remote(row, row, ssem.at[6 + j], rsem.at[6 + j], (cx, cy, c)).wait_recv()
        for cp in first_sends() + forwards():
            cp.wait_send()

    return start, forward, finish


def _gather_weights(wp):
    def body(wp_ref, wall_ref, ssem, rsem):
        for step in _gather_ops(wp_ref, wall_ref, ssem, rsem):
            step()

    return pl.pallas_call(
        body, name="gather_weights", in_specs=[ANY], out_specs=ANY,
        out_shape=jax.ShapeDtypeStruct((NPROJ,) + wp.shape, BF16),
        scratch_shapes=[pltpu.SemaphoreType.DMA((GATHER_SEMS,)), pltpu.SemaphoreType.DMA((GATHER_SEMS,))],
    )(wp)


def _pair_ops(g_refs, r_refs, ssem, rsem):
    def copies():
        x, y, c, _ = _place()
        return [_remote(g.at[n, 1 - c], r.at[n], ssem.at[t * NPROJ + n], rsem.at[t * NPROJ + n], (x, y, 1 - c))
                for t, (g, r) in enumerate(zip(g_refs, r_refs)) for n in range(NPROJ)]

    def start():
        for cp in copies():
            cp.start()

    def finish():
        x, y, c, _ = _place()
        for t, r in enumerate(r_refs):
            for n in range(NPROJ):
                k = t * NPROJ + n
                _remote(r.at[n], r.at[n], ssem.at[k], rsem.at[k], (x, y, 1 - c)).wait_recv()
        for cp in copies():
            cp.wait_send()

    return start, finish


def _pair_shapes(grads):
    return ([jax.ShapeDtypeStruct((NPROJ,) + g.shape[2:], F32) for g in grads],
            [pltpu.SemaphoreType.DMA((len(grads) * NPROJ,))] * 2)


def _pair_exchange(grads):
    ng = len(grads)

    def body(*refs):
        start, finish = _pair_ops(refs[:ng], refs[ng:2 * ng], *refs[2 * ng:])
        start()
        finish()

    out_shape, sems = _pair_shapes(grads)
    return pl.pallas_call(
        body, name="pair_exchange", in_specs=[ANY] * ng, out_specs=[ANY] * ng,
        out_shape=out_shape, scratch_shapes=sems,
    )(*grads)


def _chip_ops(s_refs, r_refs, ssem, rsem):
    def copies():
        x, y, c, chips = _place()
        return [_remote(s.at[2 * cx + cy], r.at[2 * x + y], ssem.at[3 * t + j], rsem.at[3 * t + j], (cx, cy, c))
                for t, (s, r) in enumerate(zip(s_refs, r_refs)) for j, (cx, cy) in enumerate(chips)]

    def start():
        for cp in copies():
            cp.start()

    def finish():
        x, y, c, chips = _place()
        for t, r in enumerate(r_refs):
            for j, (cx, cy) in enumerate(chips):
                slot = r.at[2 * cx + cy]
                _remote(slot, slot, ssem.at[3 * t + j], rsem.at[3 * t + j], (cx, cy, c)).wait_recv()
        for cp in copies():
            cp.wait_send()

    return start, finish


def _chip_shapes(sums):
    return ([jax.ShapeDtypeStruct(s.shape, s.dtype) for s in sums],
            [pltpu.SemaphoreType.DMA((3 * len(sums),))] * 2)


def _sibling_share(halves, pack):
    ng = len(halves)

    def body(*refs):
        h_refs, pack_ref = refs[:ng], refs[ng]
        f_refs, allp_ref = refs[ng + 1:2 * ng + 1], refs[2 * ng + 1]
        ssem, rsem, psend, precv, lsem = refs[2 * ng + 2:]
        x, y, c, _ = _place()
        me = 4 * x + 2 * y + c
        local = pltpu.make_async_copy(pack_ref, allp_ref.at[me], lsem)
        local.start()
        flips = [(fx, fy, fc) for fx in (0, 1) for fy in (0, 1) for fc in (0, 1)][1:]
        peers = [(fx + x - 2 * fx * x, fy + y - 2 * fy * y, fc + c - 2 * fc * c) for fx, fy, fc in flips]
        sends = [_remote(pack_ref, allp_ref.at[me], psend.at[m], precv.at[m], peer) for m, peer in enumerate(peers)]
        sends += [_remote(h, f, ssem.at[t], rsem.at[t], (x, y, 1 - c))
                  for t, (h, f) in enumerate(zip(h_refs, f_refs))]
        for cp in sends:
            cp.start()
        for t, f in enumerate(f_refs):
            _remote(f, f, ssem.at[t], rsem.at[t], (x, y, 1 - c)).wait_recv()
        for m, (px, py, pc) in enumerate(peers):
            row = allp_ref.at[4 * px + 2 * py + pc]
            _remote(row, row, psend.at[m], precv.at[m], (px, py, pc)).wait_recv()
        for cp in sends:
            cp.wait_send()
        local.wait()

    return pl.pallas_call(
        body, name="sibling_share", in_specs=[ANY] * (ng + 1), out_specs=[ANY] * (ng + 1),
        out_shape=[jax.ShapeDtypeStruct(h.shape, F32) for h in halves]
        + [jax.ShapeDtypeStruct((8,) + pack.shape, F32)],
        scratch_shapes=[pltpu.SemaphoreType.DMA((ng,)), pltpu.SemaphoreType.DMA((ng,)),
                        pltpu.SemaphoreType.DMA((7,)), pltpu.SemaphoreType.DMA((7,)), pltpu.SemaphoreType.DMA],
    )(*halves, pack)


def _pair_add(own, recv, cidx, name):
    R = own.shape[2]
    tr = min(256, R)

    def body(c_ref, a_ref, b_ref, o_ref):
        o_ref[...] = (a_ref[...] + b_ref[...]).astype(BF16)

    return pl.pallas_call(
        body, name=name,
        grid_spec=pltpu.PrefetchScalarGridSpec(
            num_scalar_prefetch=1, grid=(NPROJ, R // tr),
            in_specs=[pl.BlockSpec((None, None, tr, D), lambda n, r, c: (n, c[0], r, 0)),
                      pl.BlockSpec((None, tr, D), lambda n, r, c: (n, r, 0))],
            out_specs=pl.BlockSpec((None, tr, D), lambda n, r, c: (n, r, 0))),
        out_shape=jax.ShapeDtypeStruct(recv.shape, BF16),
        compiler_params=_cparams(("parallel", "parallel")),
    )(cidx, own, recv)


def _chip_sum(sums, parts, bidx, name):
    R = parts.shape[1]
    tr = min(256, R)

    def body(b_ref, s_ref, p_ref, o_ref):
        acc = None
        for j in range(NPROJ):
            term = jnp.where(b_ref[0] == j, s_ref[...], p_ref[j]).astype(F32)
            acc = term if acc is None else acc + term
        o_ref[...] = acc

    return pl.pallas_call(
        body, name=name,
        grid_spec=pltpu.PrefetchScalarGridSpec(
            num_scalar_prefetch=1, grid=(R // tr,),
            in_specs=[pl.BlockSpec((None, tr, D), lambda r, b: (b[0], r, 0)),
                      pl.BlockSpec((NPROJ, tr, D), lambda r, b: (0, r, 0))],
            out_specs=pl.BlockSpec((tr, D), lambda r, b: (r, 0))),
        out_shape=jax.ShapeDtypeStruct((R, D), F32),
        compiler_params=_cparams(("parallel",)),
    )(bidx, sums, parts)


def _adamw_math(w, g, m, v):
    m = ADAM_B1 * m + (1.0 - ADAM_B1) * g
    v = ADAM_B2 * v + (1.0 - ADAM_B2) * (g * g)
    m_hat = m / (1.0 - ADAM_B1 ** ADAM_STEP)
    v_hat = v / (1.0 - ADAM_B2 ** ADAM_STEP)
    delta = -ADAM_LR * (m_hat / (jnp.sqrt(v_hat) + ADAM_EPS) + ADAM_WD * w)
    return delta, m, v


def _adamw(w, mine, theirs, m, v, cidx, name):
    R = mine.shape[0]
    tr = min(256, R)
    nr = R // tr

    def body(c_ref, w_ref, a_ref, b_ref, m_ref, v_ref, g_ref, d_ref, nm_ref, nv_ref):
        g = jnp.where(pl.program_id(0) == c_ref[0], a_ref[...], b_ref[...])
        g_ref[...] = g
        d_ref[...], nm_ref[...], nv_ref[...] = _adamw_math(w_ref[...], g, m_ref[...], v_ref[...])

    full = pl.BlockSpec((tr, D), lambda h, r, c: (h * nr + r, 0))
    half = pl.BlockSpec((tr, D), lambda h, r, c: (r, 0))
    return pl.pallas_call(
        body, name=name,
        grid_spec=pltpu.PrefetchScalarGridSpec(
            num_scalar_prefetch=1, grid=(2, nr),
            in_specs=[full, half, half, full, full], out_specs=[full] * 4),
        out_shape=[jax.ShapeDtypeStruct(w.shape, F32)] * 4,
        compiler_params=_cparams(("parallel", "parallel")),
    )(cidx, w, mine, theirs, m, v)


PACK_ROWS = 8


def _small_update(allp, bidx, logits, weights, moments_m, moments_v):
    shapes = [w.shape for w in weights]
    q4 = D // NPROJ

    def body(b_ref, allp_ref, hgp_ref, lg_ref, *refs):
        w_refs, m_refs, v_refs = refs[0:6], refs[6:12], refs[12:18]
        loss_ref = refs[18]
        g_out, d_out, m_out, v_out = refs[19:25], refs[25:31], refs[31:37], refs[37:43]

        def total(ref, row, lo, hi):
            acc = ref[0, row:row + 1, lo:hi]
            for dev in range(1, 8):
                acc = acc + ref[dev, row:row + 1, lo:hi]
            return acc

        _, pp = _lower_bound(lg_ref[...])
        dlb = total(allp_ref, 2, 0, D)
        grads = [total(allp_ref, 0, 0, D), total(allp_ref, 4, 0, HD), total(allp_ref, 4, HD, 2 * HD),
                 total(hgp_ref, 1, 0, q4), total(allp_ref, 4, 2 * HD, 3 * HD), None]
        loss_ref[...] = (0.5 / D) * jnp.sum(total(allp_ref, 3, 0, D), axis=1, keepdims=True)
        for t in range(6):
            if t < 5:
                rows = [(slice(None), grads[t])]
            else:
                rows = [(slice(0, 1), -pp * dlb), (slice(1, 2), pp * dlb)]
            for rs, g in rows:
                g_out[t][rs, :] = g
                d_out[t][rs, :], m_out[t][rs, :], v_out[t][rs, :] = _adamw_math(
                    w_refs[t][rs, :], g, m_refs[t][rs, :], v_refs[t][rs, :])

    whole = [pl.BlockSpec(s, lambda i, b: (0, 0)) for s in shapes]
    return pl.pallas_call(
        body, name="small_update",
        grid_spec=pltpu.PrefetchScalarGridSpec(
            num_scalar_prefetch=1, grid=(1,),
            in_specs=[pl.BlockSpec((8, PACK_ROWS, D), lambda i, b: (0, 0, 0)),
                      pl.BlockSpec((8, PACK_ROWS, q4), lambda i, b: (0, 0, b[0])),
                      pl.BlockSpec((2, D), lambda i, b: (0, 0))] + whole * 3,
            out_specs=[pl.BlockSpec((1, 1), lambda i, b: (0, 0))] + whole * 4),
        out_shape=[jax.ShapeDtypeStruct((1, 1), F32)] + [jax.ShapeDtypeStruct(s, F32) for s in shapes] * 4,
        compiler_params=_cparams(("arbitrary",)),
    )(bidx, allp, allp, logits, *weights, *moments_m, *moments_v)


def kernel(x, sb_norm, sb_w_in, sb_q_gain, sb_k_gain, sb_w_out, hg_norm, hg_w_in, hg_o_gain, hg_w_out, hg_lb_logits, loss_target, m_sb_norm, m_sb_w_in, m_sb_q_gain, m_sb_k_gain, m_sb_w_out, m_hg_norm, m_hg_w_in, m_hg_o_gain, m_hg_w_out, m_hg_lb_logits, v_sb_norm, v_sb_w_in, v_sb_q_gain, v_sb_k_gain, v_sb_w_out, v_hg_norm, v_hg_w_in, v_hg_o_gain, v_hg_w_out, v_hg_lb_logits):
    Bl, S, _ = x.shape
    T = Bl * S
    cidx = lax.axis_index("c").astype(jnp.int32).reshape(1)
    bidx = (2 * lax.axis_index("x") + lax.axis_index("y")).astype(jnp.int32).reshape(1)

    def in_hbm(arrays):
        return [pltpu.with_memory_space_constraint(a, pltpu.HBM) for a in arrays]

    def own_slot(gathered, mine):
        return lax.dynamic_update_slice(gathered, mine[None], (bidx[0],) + (0,) * mine.ndim)

    def halved(g):
        return g.reshape(NPROJ, 2, g.shape[-2] * g.shape[0] // (2 * NPROJ), D)

    def heads(a):
        return a.reshape(a.shape[:-2] + (Bl, S, D))

    def flat(a):
        return a.reshape(a.shape[:-3] + (T, D))

    wp_first = sb_w_in[0].astype(BF16)
    wp_rest = jnp.concatenate([hg_w_in[0], sb_w_out[0], hg_w_out[0]], axis=0).astype(BF16)
    wall_first, = in_hbm([own_slot(_gather_weights(wp_first), wp_first)])
    x2 = x.reshape(T, D)
    tgt = loss_target.reshape(T, D)

    proj0 = _in_proj_fwd(x2, sb_norm, wall_first, W_IN, "sb_in_fwd")
    o0, ctot, wall_rest, hnall = _sb_fwd(heads(proj0), sb_q_gain, sb_k_gain, wp_rest, hg_norm)
    wall_rest, = in_hbm([own_slot(wall_rest, wp_rest)])
    hgn = own_slot(hnall, hg_norm).reshape(1, D)
    h1 = _out_proj_fwd(flat(o0), proj0, x2, wall_rest, W_OUT_SB, "sb_out_fwd")
    proj1 = _in_proj_fwd(h1, hgn, wall_rest, W_IN, "hg_in_fwd")
    o1, states = _hg_fwd(heads(proj1), hg_lb_logits)
    dh2, loss_terms = _out_proj_fwd(flat(o1), proj1, h1, wall_rest, W_OUT_HG, "hg_out_fwd",
                                    o_gain=hg_o_gain, target=tgt)

    do1, dproj1, gout_hg, d_ogain = _out_proj_bwd(dh2, flat(o1), proj1, wall_rest, W_OUT_HG, "hg_out_bwd",
                                                  o_gain=hg_o_gain)
    dproj1, dlb = _hg_bwd(heads(proj1), states, heads(do1), heads(dproj1), hg_lb_logits)
    dproj1 = flat(dproj1)
    dh1, d_hgn = _in_proj_bwd_x(dproj1, wall_rest, W_IN, h1, hgn, dh2, "hg_in_bwd_x")
    big_hg = in_hbm([halved(_in_proj_bwd_w(dproj1, h1, hgn, "hg_in_bwd_w")), halved(gout_hg)])
    do0, dproj0, gout_sb, *recv_hg = _out_proj_bwd(dh1, flat(o0), proj0, wall_rest, W_OUT_SB, "sb_out_bwd",
                                                   exchange=big_hg)
    sums_hg = in_hbm(_pair_add(g, r, cidx, "pair_add_" + nm)
                     for g, r, nm in zip(big_hg, in_hbm(recv_hg), ("hg_in", "hg_out")))
    dproj0, d_qg, d_kg, *parts_hg = _sb_bwd(heads(proj0), ctot, heads(do0), heads(dproj0),
                                            sb_q_gain, sb_k_gain, sums_hg)
    dproj0 = flat(dproj0)

    big_sb = in_hbm([halved(_in_proj_bwd_w(dproj0, x2, sb_norm, "sb_in_bwd_w")), halved(gout_sb)])
    sums_sb = in_hbm(_pair_add(g, r, cidx, "pair_add_" + nm)
                     for g, r, nm in zip(big_sb, in_hbm(_pair_exchange(big_sb)), ("sb_in", "sb_out")))
    grad_x, d_sbn, *parts_sb = _in_proj_bwd_x(dproj0, wall_first, W_IN, x2, sb_norm, dh1, "sb_in_bwd_x",
                                              exchange=sums_sb)

    names = ["sb_in", "hg_in", "sb_out", "hg_out"]
    sums = [sums_sb[0], sums_hg[0], sums_sb[1], sums_hg[1]]
    parts = [parts_sb[0], parts_hg[0], parts_sb[1], parts_hg[1]]
    halves = in_hbm(_chip_sum(sm, p, bidx, "chip_sum_" + nm) for sm, p, nm in zip(sums, in_hbm(parts), names))
    gains = jnp.concatenate([d_qg, d_kg, d_ogain, jnp.zeros((1, D - 3 * HD), F32)], axis=1)
    pack = jnp.concatenate([d_sbn, d_hgn, dlb, loss_terms, gains, jnp.zeros((3, D), F32)], axis=0)
    *theirs, allp = _sibling_share(halves, pack)
    theirs = in_hbm(theirs)

    big_w = [sb_w_in, hg_w_in, sb_w_out, hg_w_out]
    big_m = [m_sb_w_in, m_hg_w_in, m_sb_w_out, m_hg_w_out]
    big_v = [v_sb_w_in, v_hg_w_in, v_sb_w_out, v_hg_w_out]
    upd = [_adamw(*in_hbm([w[0], a, b, m[0], v[0]]), cidx, "adamw_" + nm)
           for w, a, b, m, v, nm in zip(big_w, halves, theirs, big_m, big_v, names)]
    (g_sb_in, d_sb_in, nm_sb_in, nv_sb_in), (g_hg_in, d_hg_in, nm_hg_in, nv_hg_in), \
        (g_sb_out, d_sb_out, nm_sb_out, nv_sb_out), (g_hg_out, d_hg_out, nm_hg_out, nv_hg_out) = [
            tuple(a[None] for a in u) for u in upd]

    small = _small_update(
        allp, bidx, hg_lb_logits,
        [sb_norm, sb_q_gain, sb_k_gain, hg_norm, hg_o_gain, hg_lb_logits],
        [m_sb_norm, m_sb_q_gain, m_sb_k_gain, m_hg_norm, m_hg_o_gain, m_hg_lb_logits],
        [v_sb_norm, v_sb_q_gain, v_sb_k_gain, v_hg_norm, v_hg_o_gain, v_hg_lb_logits])
    loss = small[0].reshape(())
    (g_sbn, g_qg, g_kg, g_hgn, g_og, g_lb) = small[1:7]
    (d_sbn2, d_qg2, d_kg2, d_hgn2, d_og2, d_lb2) = small[7:13]
    (nm_sbn, nm_qg, nm_kg, nm_hgn, nm_og, nm_lb) = small[13:19]
    (nv_sbn, nv_qg, nv_kg, nv_hgn, nv_og, nv_lb) = small[19:25]

    return (loss, grad_x.reshape(Bl, S, D),
            g_sbn, g_sb_in, g_qg, g_kg, g_sb_out, g_hgn, g_hg_in, g_og, g_hg_out, g_lb,
            d_sbn2, d_sb_in, d_qg2, d_kg2, d_sb_out, d_hgn2, d_hg_in, d_og2, d_hg_out, d_lb2,
            nm_sbn, nm_sb_in, nm_qg, nm_kg, nm_sb_out, nm_hgn, nm_hg_in, nm_og, nm_hg_out, nm_lb,
            nv_sbn, nv_sb_in, nv_qg, nv_kg, nv_sb_out, nv_hgn, nv_hg_in, nv_og, nv_hg_out, nv_lb)
```

```python
import functools

import jax
import jax.numpy as jnp
from jax import lax
from jax.experimental import pallas as pl
from jax.experimental.pallas import tpu as pltpu

F32 = jnp.float32
BF16 = jnp.bfloat16
MESH = pl.DeviceIdType.MESH
ANY = pl.BlockSpec(memory_space=pl.ANY)

D = 1024
HEADS = 8
HD = 128
NPROJ = 4
RMS_EPS = 1e-6
TK = 256
CH = 64
CH_LOG2 = 6
GR = 128
SCALE = HD ** -0.5
EXP_CLAMP = 60.0
W_IN, W_OUT_SB, W_OUT_HG = 0, 4, 5

ADAM_LR = 0.001
ADAM_B1 = 0.9
ADAM_B2 = 0.999
ADAM_EPS = 1e-08
ADAM_WD = 0.01
ADAM_STEP = 10

NT = (((1,), (1,)), ((), ()))
TN = (((0,), (0,)), ((), ()))
MIB = 1024 * 1024


def _cparams(sem=None, vmem_mib=40):
    return pltpu.CompilerParams(dimension_semantics=sem, vmem_limit_bytes=vmem_mib * MIB)


def _dot(a, b, dims=None):
    if dims is None:
        return jnp.dot(a, b, preferred_element_type=F32)
    return lax.dot_general(a, b, dims, preferred_element_type=F32)


def _sigmoid(x):
    return 1.0 / (1.0 + jnp.exp(-x))


def _rms(x):
    return lax.rsqrt(jnp.mean(x * x, axis=-1, keepdims=True) + RMS_EPS)


def _rms_bwd(x, r, gain, dy):
    a = dy * gain
    dx = r * a - x * (r * r * r) * jnp.mean(x * a, axis=-1, keepdims=True)
    return dx, dy * (x * r)


def _split2(v):
    hi = v.astype(BF16)
    lo = (v - hi.astype(F32)).astype(BF16)
    return hi, lo


def _cum2(v, u):
    hi, lo = _split2(v)
    return _dot(hi, u) + _dot(lo, u)


def _dot3(a, b, dims=None):
    ah, al = _split2(a)
    bh, bl = _split2(b)
    return _dot(ah, bh, dims) + _dot(ah, bl, dims) + _dot(al, bh, dims)


def _cum2l(u, v):
    hi, lo = _split2(v)
    return _dot(u, hi) + _dot(u, lo)


def _in_proj_fwd(h, gain, wall, wblk, name):
    T = h.shape[0]
    tm = min(1024, T)

    def body(h_ref, g_ref, w_ref, o_ref, u_s):
        rows = pl.ds(pl.multiple_of(pl.program_id(1) * tm, tm), tm)

        @pl.when(pl.program_id(0) == 0)
        def _():
            x = h_ref[...]
            u_s[rows, :] = (x * _rms(x) * g_ref[...]).astype(BF16)

        o_ref[...] = _dot(u_s[rows, :], w_ref[...])

    return pl.pallas_call(
        body, name=name, grid=(NPROJ, T // tm),
        in_specs=[pl.BlockSpec((tm, D), lambda n, i: (jnp.where(n == 0, i, 0), 0)),
                  pl.BlockSpec((1, D), lambda n, i: (0, 0)),
                  pl.BlockSpec((None, D, D), lambda n, i: (n, wblk, 0))],
        out_specs=pl.BlockSpec((None, tm, D), lambda n, i: (n, i, 0)),
        out_shape=jax.ShapeDtypeStruct((NPROJ, T, D), F32),
        scratch_shapes=[pltpu.VMEM((T, D), BF16)],
        compiler_params=_cparams(("arbitrary", "arbitrary")),
    )(h, gain, wall)


def _head_norm(x):
    outs = []
    for hh in range(x.shape[1] // HD):
        xs = x[:, hh * HD:(hh + 1) * HD]
        outs.append((xs, _rms(xs)))
    return outs


def _w_out_specs(wblk):
    kb = D // NPROJ
    return [pl.BlockSpec((None, kb, D), functools.partial(lambda j, i: (j, wblk, 0), j)) for j in range(NPROJ)]


def _out_proj_fwd(o, proj, resid, wall, wblk, name, o_gain=None, target=None):
    T = o.shape[0]
    tm = min(512, T)
    kb = D // NPROJ
    with_loss = target is not None

    def body(*refs):
        o_ref, g_ref, r_ref = refs[:3]
        w_refs = refs[3:3 + NPROJ]
        if with_loss:
            og_ref, t_ref, dh_ref, ls_ref = refs[3 + NPROJ:]
        else:
            h_ref, = refs[3 + NPROJ:]
        x = o_ref[...]
        if with_loss:
            x = jnp.concatenate([xs * r * og_ref[...] for xs, r in _head_norm(x)], axis=1)
        g = g_ref[...]
        a = (x * (g * _sigmoid(g))).astype(BF16)
        hnew = r_ref[...]
        for j in range(NPROJ):
            hnew = hnew + _dot(a[:, j * kb:(j + 1) * kb], w_refs[j][...])
        if with_loss:
            err = hnew - t_ref[...]
            dh_ref[...] = err * (1.0 / D)
            part = jnp.sum(err * err, axis=0, keepdims=True)

            @pl.when(pl.program_id(0) == 0)
            def _():
                ls_ref[...] = part

            @pl.when(pl.program_id(0) != 0)
            def _():
                ls_ref[...] += part
        else:
            h_ref[...] = hnew

    tile = pl.BlockSpec((tm, D), lambda i: (i, 0))
    in_specs = [tile, pl.BlockSpec((None, tm, D), lambda i: (3, i, 0)), tile] + _w_out_specs(wblk)
    args = [o, proj, resid] + [wall] * NPROJ
    out_specs = tile
    out_shape = jax.ShapeDtypeStruct((T, D), F32)
    if with_loss:
        in_specs += [pl.BlockSpec((1, HD), lambda i: (0, 0)), tile]
        args += [o_gain, target]
        out_specs = [tile, pl.BlockSpec((1, D), lambda i: (0, 0))]
        out_shape = [out_shape, jax.ShapeDtypeStruct((1, D), F32)]
    return pl.pallas_call(
        body, name=name, grid=(T // tm,), in_specs=in_specs, out_specs=out_specs,
        out_shape=out_shape, compiler_params=_cparams(("arbitrary",)),
    )(*args)


def _out_proj_bwd(dy, o, proj, wall, wblk, name, o_gain=None, exchange=()):
    T = o.shape[0]
    tm = min(512, T)
    kb = D // NPROJ
    normed = o_gain is not None
    ne = len(exchange)

    def body(*refs):
        it = iter(refs)
        dy_ref, o_ref, g_ref = (next(it) for _ in range(3))
        w_refs = [next(it) for _ in range(NPROJ)]
        og_ref = next(it) if normed else None
        xg_refs = [next(it) for _ in range(ne)]
        do_ref, dg_ref, dw_ref = (next(it) for _ in range(3))
        dgain_ref = next(it) if normed else None
        xr_refs = [next(it) for _ in range(ne)]
        wt_s = next(it)
        first = pl.program_id(0) == 0
        if ne:
            start, finish = _pair_ops(xg_refs, xr_refs, next(it), next(it))
            pl.when(first)(start)

        @pl.when(first)
        def _():
            for j in range(NPROJ):
                wt_s[:, j * kb:(j + 1) * kb] = w_refs[j][...].T
        g = g_ref[...]
        s = _sigmoid(g)
        sl = g * s
        x = o_ref[...]
        if normed:
            heads = _head_norm(x)
            on = jnp.concatenate([xs * r * og_ref[...] for xs, r in heads], axis=1)
        else:
            on = x
        dyb = dy_ref[...].astype(BF16)
        a = (on * sl).astype(BF16)
        for j in range(NPROJ):
            part = _dot(a[:, j * kb:(j + 1) * kb], dyb, TN)

            @pl.when(first)
            def _():
                dw_ref[j] = part

            @pl.when(jnp.logical_not(first))
            def _():
                dw_ref[j] += part

        da = _dot(dyb, wt_s[...])
        d_on = da * sl
        dg_ref[...] = (da * on * (s * (1.0 + g * (1.0 - s)))).astype(BF16)
        if normed:
            dxs, gsum = [], None
            for hh, (xs, r) in enumerate(heads):
                dx, gt = _rms_bwd(xs, r, og_ref[...], d_on[:, hh * HD:(hh + 1) * HD])
                dxs.append(dx)
                gt = jnp.sum(gt, axis=0, keepdims=True)
                gsum = gt if gsum is None else gsum + gt
            do_ref[...] = jnp.concatenate(dxs, axis=1).astype(BF16)

            @pl.when(first)
            def _():
                dgain_ref[...] = gsum

            @pl.when(jnp.logical_not(first))
            def _():
                dgain_ref[...] += gsum
        else:
            do_ref[...] = d_on.astype(BF16)
        if ne:
            pl.when(pl.program_id(0) == T // tm - 1)(finish)

    tile = pl.BlockSpec((tm, D), lambda i: (i, 0))
    gate = pl.BlockSpec((None, tm, D), lambda i: (3, i, 0))
    in_specs = [tile, tile, gate] + _w_out_specs(wblk)
    args = [dy, o, proj] + [wall] * NPROJ
    out_specs = [tile, gate, pl.BlockSpec((NPROJ, kb, D), lambda i: (0, 0, 0))]
    out_shape = [jax.ShapeDtypeStruct((T, D), BF16),
                 jax.ShapeDtypeStruct((NPROJ, T, D), BF16),
                 jax.ShapeDtypeStruct((NPROJ, kb, D), F32)]
    if normed:
        in_specs.append(pl.BlockSpec((1, HD), lambda i: (0, 0)))
        args.append(o_gain)
        out_specs.append(pl.BlockSpec((1, HD), lambda i: (0, 0)))
        out_shape.append(jax.ShapeDtypeStruct((1, HD), F32))
    x_shape, x_sems = _pair_shapes(exchange) if ne else ([], [])
    return pl.pallas_call(
        body, name=name, grid=(T // tm,), in_specs=in_specs + [ANY] * ne, out_specs=out_specs + [ANY] * ne,
        out_shape=out_shape + x_shape, scratch_shapes=[pltpu.VMEM((D, D), BF16)] + x_sems,
        compiler_params=_cparams(("arbitrary",), vmem_mib=48),
    )(*args, *exchange)


def _in_proj_bwd_x(dproj, wall, wblk, h, gain, dres, name, exchange=()):
    T = h.shape[0]
    tm = min(512, T)
    ne = len(exchange)

    def body(d_ref, w_ref, h_ref, g_ref, r_ref, *refs):
        xs_refs, (dh_ref, dgain_ref), xr_refs = refs[:ne], refs[ne:ne + 2], refs[ne + 2:2 * ne + 2]
        du, wt_s = refs[2 * ne + 2:2 * ne + 4]
        i, n = pl.program_id(0), pl.program_id(1)
        if ne:
            start, finish = _chip_ops(xs_refs, xr_refs, *refs[2 * ne + 4:])
            pl.when(jnp.logical_and(i == 0, n == 0))(start)

        @pl.when(i == 0)
        def _():
            wt_s[n] = w_ref[...].T

        part = _dot(d_ref[...], wt_s[n])

        @pl.when(n == 0)
        def _():
            du[...] = part

        @pl.when(n != 0)
        def _():
            du[...] += part

        @pl.when(n == NPROJ - 1)
        def _():
            x = h_ref[...]
            dx, gt = _rms_bwd(x, _rms(x), g_ref[...], du[...])
            dh_ref[...] = r_ref[...] + dx
            gt = jnp.sum(gt, axis=0, keepdims=True)

            @pl.when(i == 0)
            def _():
                dgain_ref[...] = gt

            @pl.when(i != 0)
            def _():
                dgain_ref[...] += gt

        if ne:
            pl.when(jnp.logical_and(i == T // tm - 1, n == NPROJ - 1))(finish)

    x_shape, x_sems = _chip_shapes(exchange) if ne else ([], [])
    return pl.pallas_call(
        body, name=name, grid=(T // tm, NPROJ),
        in_specs=[pl.BlockSpec((None, tm, D), lambda i, n: (n, i, 0)),
                  pl.BlockSpec((None, D, D), lambda i, n: (jnp.where(i == 0, n, NPROJ - 1), wblk, 0)),
                  pl.BlockSpec((tm, D), lambda i, n: (i, 0)),
                  pl.BlockSpec((1, D), lambda i, n: (0, 0)),
                  pl.BlockSpec((tm, D), lambda i, n: (i, 0))] + [ANY] * ne,
        out_specs=[pl.BlockSpec((tm, D), lambda i, n: (i, 0)),
                   pl.BlockSpec((1, D), lambda i, n: (0, 0))] + [ANY] * ne,
        out_shape=[jax.ShapeDtypeStruct((T, D), F32), jax.ShapeDtypeStruct((1, D), F32)] + x_shape,
        scratch_shapes=[pltpu.VMEM((tm, D), F32), pltpu.VMEM((NPROJ, D, D), BF16)] + x_sems,
        compiler_params=_cparams(("arbitrary", "arbitrary")),
    )(dproj, wall, h, gain, dres, *exchange)


def _in_proj_bwd_w(dproj, h, gain, name):
    T = h.shape[0]
    tk = min(1024, T)

    def body(d_ref, h_ref, g_ref, dw_ref, ut_s):
        k = pl.program_id(1)

        @pl.when(pl.program_id(0) == 0)
        def _():
            x = h_ref[...]
            ut_s[k] = (x * _rms(x) * g_ref[...]).astype(BF16).T

        part = _dot(ut_s[k], d_ref[...])

        @pl.when(k == 0)
        def _():
            dw_ref[...] = part

        @pl.when(k != 0)
        def _():
            dw_ref[...] += part

    return pl.pallas_call(
        body, name=name, grid=(NPROJ, T // tk),
        in_specs=[pl.BlockSpec((None, tk, D), lambda n, k: (n, k, 0)),
                  pl.BlockSpec((tk, D), lambda n, k: (jnp.where(n == 0, k, 0), 0)),
                  pl.BlockSpec((1, D), lambda n, k: (0, 0))],
        out_specs=pl.BlockSpec((None, D, D), lambda n, k: (n, 0, 0)),
        out_shape=jax.ShapeDtypeStruct((NPROJ, D, D), F32),
        scratch_shapes=[pltpu.VMEM((T // tk, D, tk), BF16)],
        compiler_params=_cparams(("arbitrary", "arbitrary")),
    )(dproj, h, gain)


def _log_sigmoid_pair(z):
    lb = jnp.minimum(z, 0.0) - jnp.log(1.0 + jnp.exp(-jnp.abs(z)))
    return lb, lb - z


def _slab_consts():
    t = lax.broadcasted_iota(jnp.int32, (TK, TK), 0)
    s = lax.broadcasted_iota(jnp.int32, (TK, TK), 1)
    return s < t, (t > s).astype(BF16), (t < s).astype(BF16)


def _slab_rows(k0, S):
    return [(r0, r1, masked) for r0, r1, masked in ((k0, k0 + TK, True), (k0 + TK, S, False)) if r0 < r1]


def _sb_fwd(proj, q_gain, k_gain, wp, hn):
    _, Bl, S, _ = proj.shape
    steps = Bl * HEADS

    def body(q_ref, k_ref, v_ref, qg_ref, kg_ref, wp_ref, hn_ref, o_ref, ct_ref, wall_ref, hnall_ref,
             qn, kn, vb, ssem, rsem):
        step = pl.program_id(0) * HEADS + pl.program_id(1)
        start, forward, finish = _gather_ops(wp_ref, wall_ref, ssem, rsem, hn_ref, hnall_ref)
        pl.when(step == 0)(start)
        pl.when(step == steps // 2)(forward)
        q = q_ref[...]
        qn[...] = (q * _rms(q) * (qg_ref[...] * SCALE)).astype(BF16)
        k = k_ref[...]
        kn[...] = (k * _rms(k) * kg_ref[...]).astype(BF16)
        vb[...] = v_ref[...].astype(BF16)
        tri, u_gt, _ = _slab_consts()
        nb = S // TK
        c_blk = [jnp.zeros((TK, 1), F32)] * nb
        o_blk = [jnp.zeros((TK, HD), F32)] * nb
        for k0 in reversed(range(0, S, TK)):
            kb, vbb = kn[k0:k0 + TK, :], vb[k0:k0 + TK, :]
            for r0, r1, masked in _slab_rows(k0, S):
                blocks = range(r0 // TK, r1 // TK)
                z = _dot(qn[r0:r1, :], kb, NT)
                lb, ls = _log_sigmoid_pair(z)
                if masked:
                    ls = jnp.where(tri, ls, 0.0)
                c = jnp.concatenate([c_blk[b] for b in blocks], axis=0)
                w = jnp.exp(lb + _cum2(ls, u_gt) + c)
                if masked:
                    w = jnp.where(tri, w, 0.0)
                o_new = _dot(w.astype(BF16), vbb)
                c_new = jnp.sum(ls, axis=1, keepdims=True)
                for i, b in enumerate(blocks):
                    o_blk[b] = o_blk[b] + o_new[i * TK:(i + 1) * TK]
                    c_blk[b] = c_blk[b] + c_new[i * TK:(i + 1) * TK]
        o_ref[...] = jnp.concatenate(o_blk, axis=0)
        ct_ref[...] = jnp.concatenate(c_blk, axis=0)
        pl.when(step == steps - 1)(finish)

    def slot(n):
        return pl.BlockSpec((None, None, S, HD), lambda b, h: (n, b, 0, h))

    return pl.pallas_call(
        body, name="sb_fwd", grid=(Bl, HEADS),
        in_specs=[slot(0), slot(1), slot(2),
                  pl.BlockSpec((1, HD), lambda b, h: (0, 0)),
                  pl.BlockSpec((1, HD), lambda b, h: (0, 0)), ANY, ANY],
        out_specs=[pl.BlockSpec((None, S, HD), lambda b, h: (b, 0, h)),
                   pl.BlockSpec((None, None, S, 1), lambda b, h: (b, h, 0, 0)), ANY, ANY],
        out_shape=[jax.ShapeDtypeStruct((Bl, S, D), F32),
                   jax.ShapeDtypeStruct((Bl, HEADS, S, 1), F32),
                   jax.ShapeDtypeStruct((NPROJ,) + wp.shape, BF16),
                   jax.ShapeDtypeStruct((NPROJ,) + hn.shape, F32)],
        scratch_shapes=[pltpu.VMEM((S, HD), BF16)] * 3 + [pltpu.SemaphoreType.DMA((GATHER_SEMS,))] * 2,
        compiler_params=_cparams(("arbitrary", "arbitrary"), vmem_mib=56),
    )(proj, proj, proj, q_gain, k_gain, wp, hn)


def _sb_bwd(proj, ctot, do, dproj, q_gain, k_gain, exchange):
    _, Bl, S, _ = proj.shape
    ne = len(exchange)

    def body(q_ref, k_ref, v_ref, ct_ref, do_ref, qg_ref, kg_ref, _, *refs):
        xs_refs, (dqkv_ref, dqg_ref, dkg_ref), xr_refs = refs[:ne], refs[ne:ne + 3], refs[ne + 3:2 * ne + 3]
        qn, kn, vb, qt_s, dot_s, dqn, dkt_s, dvt_s, ssem, rsem = refs[2 * ne + 3:]
        step = pl.program_id(0) * HEADS + pl.program_id(1)
        first = step == 0
        start, finish = _chip_ops(xs_refs, xr_refs, ssem, rsem)

        @pl.when(first)
        def _():
            start()
            dqg_ref[...] = jnp.zeros_like(dqg_ref)
            dkg_ref[...] = jnp.zeros_like(dkg_ref)

        q = q_ref[...]
        rq = _rms(q)
        qn[...] = (q * rq * (qg_ref[...] * SCALE)).astype(BF16)
        k = k_ref[...]
        rk = _rms(k)
        kn[...] = (k * rk * kg_ref[...]).astype(BF16)
        vb[...] = v_ref[...].astype(BF16)
        qt_s[...] = qn[...].T
        dot_s[...] = do_ref[...].T
        for acc in (dqn, dkt_s, dvt_s):
            acc[...] = jnp.zeros_like(acc)
        tri, u_gt, u_lt = _slab_consts()
        nb = S // TK
        passed, e = [jnp.zeros((TK, 1), F32)] * nb, [jnp.zeros((TK, 1), F32)] * nb
        for k0 in range(0, S, TK):
            keys = slice(k0, k0 + TK)
            kb, vbb = kn[keys, :], vb[keys, :]
            for r0, r1, masked in _slab_rows(k0, S):
                rows, blocks = slice(r0, r1), range(r0 // TK, r1 // TK)
                qb, dobb = qn[rows, :], do_ref[rows, :]
                z = _dot(qb, kb, NT)
                lb, ls = _log_sigmoid_pair(z)
                if masked:
                    ls = jnp.where(tri, ls, 0.0)
                p_new = jnp.concatenate([passed[b] for b in blocks], axis=0) + jnp.sum(ls, axis=1, keepdims=True)
                w = jnp.exp(lb + _cum2(ls, u_gt) + (ct_ref[rows, :] - p_new))
                if masked:
                    w = jnp.where(tri, w, 0.0)
                de = w * _dot(dobb, vbb, NT)
                e_old = jnp.concatenate([e[b] for b in blocks], axis=0)
                dls = e_old + _cum2(de, u_lt)
                e_new = e_old + jnp.sum(de, axis=1, keepdims=True)
                for i, b in enumerate(blocks):
                    passed[b], e[b] = p_new[i * TK:(i + 1) * TK], e_new[i * TK:(i + 1) * TK]
                sg = jnp.exp(lb)
                dz = de - sg * (de + dls)
                if masked:
                    dz = jnp.where(tri, dz, 0.0)
                dzb = dz.astype(BF16)
                dqn[rows, :] += _dot(dzb, kb)
                dkt_s[:, keys] += _dot(qt_s[:, rows], dzb)
                dvt_s[:, keys] += _dot(dot_s[:, rows], w.astype(BF16))

        dx, gt = _rms_bwd(q, rq, qg_ref[...], dqn[...] * SCALE)
        dqkv_ref[0] = dx.astype(BF16)
        dqg_ref[...] += jnp.sum(gt, axis=0, keepdims=True)
        dx, gt = _rms_bwd(k, rk, kg_ref[...], dkt_s[...].T)
        dqkv_ref[1] = dx.astype(BF16)
        dkg_ref[...] += jnp.sum(gt, axis=0, keepdims=True)
        dqkv_ref[2] = dvt_s[...].T.astype(BF16)
        pl.when(step == Bl * HEADS - 1)(finish)

    def slot(n):
        return pl.BlockSpec((None, None, S, HD), lambda b, h: (n, b, 0, h))

    head = pl.BlockSpec((None, S, HD), lambda b, h: (b, 0, h))
    gain = pl.BlockSpec((1, HD), lambda b, h: (0, 0))
    x_shape, x_sems = _chip_shapes(exchange)
    return pl.pallas_call(
        body, name="sb_bwd", grid=(Bl, HEADS),
        in_specs=[slot(0), slot(1), slot(2),
                  pl.BlockSpec((None, None, S, 1), lambda b, h: (b, h, 0, 0)), head, gain, gain, ANY] + [ANY] * ne,
        out_specs=[pl.BlockSpec((3, None, S, HD), lambda b, h: (0, b, 0, h)), gain, gain] + [ANY] * ne,
        out_shape=[jax.ShapeDtypeStruct(dproj.shape, dproj.dtype),
                   jax.ShapeDtypeStruct((1, HD), F32), jax.ShapeDtypeStruct((1, HD), F32)] + x_shape,
        scratch_shapes=([pltpu.VMEM((S, HD), BF16)] * 3 + [pltpu.VMEM((HD, S), BF16)] * 2 + [pltpu.VMEM((S, HD), F32)]
                        + [pltpu.VMEM((HD, S), F32)] * 2 + x_sems),
        input_output_aliases={7: 0},
        compiler_params=_cparams(("arbitrary", "arbitrary"), vmem_mib=56),
    )(proj, proj, proj, ctot, do, q_gain, k_gain, dproj, *exchange)


def _lower_bound(logits):
    l0, l1 = logits[0:1, :], logits[1:2, :]
    m = jnp.maximum(l0, l1)
    e0, e1 = jnp.exp(l0 - m), jnp.exp(l1 - m)
    p0, p1 = e0 / (e0 + e1), e1 / (e0 + e1)
    return (p0 + p1) - p0, p0 * p1


def _hg_gates(qr, fp, lbv):
    sq = _sigmoid(qr)
    sp = _sigmoid(fp)
    sn = 1.0 / (1.0 + jnp.exp(fp))
    f = lbv + (1.0 - lbv) * sp
    return qr * sq, sq, sp, sn, f, (1.0 - lbv) * sn


def _group_consts():
    t = lax.broadcasted_iota(jnp.int32, (GR, GR), 0)
    j = lax.broadcasted_iota(jnp.int32, (GR, GR), 1)
    same = lax.shift_right_logical(t, CH_LOG2) == lax.shift_right_logical(j, CH_LOG2)
    tril = jnp.logical_and(same, j <= t)
    return (tril, tril.astype(BF16), jnp.logical_and(same, j >= t).astype(BF16), same.astype(BF16))


def _hg_decays(qa, k, f, t_inc, t_same):
    g = jnp.log(f)
    gc = _cum2l(t_inc, g)
    gl = _cum2l(t_same, g)
    gm = gc - 0.5 * gl
    e_q = jnp.exp(jnp.minimum(gm, EXP_CLAMP))
    e_k = jnp.exp(jnp.minimum(-gm, EXP_CLAMP))
    e_g = jnp.exp(gc)
    e_l = jnp.exp(gl - gc)
    return qa * e_q, k * e_k, qa * e_g, k * e_l, e_q, e_k, e_g, e_l, jnp.exp(gl)


def _hg_fwd(proj, lb_logits):
    _, Bl, S, _ = proj.shape
    nc = S // CH

    def body(q_ref, f_ref, i_ref, lg_ref, o_ref, st_ref, egl_s):
        lbv, _ = _lower_bound(lg_ref[...])
        tril, t_inc, _, t_same = _group_consts()
        st = jnp.zeros((HD, HD), F32)
        for g0 in range(0, S, GR):
            rs = slice(g0, g0 + GR)
            qa, _, _, _, f, k = _hg_gates(q_ref[rs, :], f_ref[rs, :], lbv)
            qt, kt, qg, kd, _, _, _, _, e_gl = _hg_decays(qa, k, f, t_inc, t_same)
            a = jnp.where(tril, _dot(qt.astype(BF16), kt.astype(BF16), NT), 0.0)
            ib, qgb, kdb = i_ref[rs, :].astype(BF16), qg.astype(BF16), kd.astype(BF16)
            within = _dot(a.astype(BF16), ib)
            egl_s[rs, :] = e_gl
            outs = []
            for l0 in range(0, GR, CH):
                ls = slice(l0, l0 + CH)
                st_ref[(g0 + l0) // CH] = st
                outs.append(within[ls] + _dot(qgb[ls], st.astype(BF16), NT))
                st = st * egl_s[g0 + l0:g0 + l0 + 1, :] + _dot(ib[ls], kdb[ls], TN)
            o_ref[rs, :] = jnp.concatenate(outs, axis=0)

    def slot(n):
        return pl.BlockSpec((None, None, S, HD), lambda b, h: (n, b, 0, h))

    return pl.pallas_call(
        body, name="hg_fwd", grid=(Bl, HEADS),
        in_specs=[slot(0), slot(1), slot(2), pl.BlockSpec((2, HD), lambda b, h: (0, h))],
        out_specs=[pl.BlockSpec((None, S, HD), lambda b, h: (b, 0, h)),
                   pl.BlockSpec((None, None, nc, HD, HD), lambda b, h: (b, h, 0, 0, 0))],
        out_shape=[jax.ShapeDtypeStruct((Bl, S, D), F32),
                   jax.ShapeDtypeStruct((Bl, HEADS, nc, HD, HD), F32)],
        scratch_shapes=[pltpu.VMEM((S, HD), F32)],
        compiler_params=_cparams(("parallel", "parallel")),
    )(proj, proj, proj, lb_logits)


def _hg_bwd(proj, states, do, dproj, lb_logits):
    _, Bl, S, _ = proj.shape

    def body(q_ref, f_ref, i_ref, st_ref, do_ref, lg_ref, _, dqfi_ref, dlb_ref, egl_s):
        lbv, _ = _lower_bound(lg_ref[...])
        tril, t_inc, t_dec, t_same = _group_consts()
        dst = jnp.zeros((HD, HD), F32)
        dlb = jnp.zeros((1, HD), F32)
        for g0 in reversed(range(0, S, GR)):
            rs = slice(g0, g0 + GR)
            qr, fp = q_ref[rs, :], f_ref[rs, :]
            qa, sq, sp, sn, f, k = _hg_gates(qr, fp, lbv)
            qt, kt, qg, kd, e_q, e_k, e_g, e_l, e_gl = _hg_decays(qa, k, f, t_inc, t_same)
            ib, dob, qgb, kdb = i_ref[rs, :].astype(BF16), do_ref[rs, :], qg.astype(BF16), kd.astype(BF16)
            egl_s[rs, :] = e_gl
            ab = jnp.where(tril, _dot(qt.astype(BF16), kt.astype(BF16), NT), 0.0).astype(BF16)
            da = jnp.where(tril, _dot(dob, ib, NT), 0.0)
            dqt = _dot3(da, kt)
            dkt = _dot3(da, qt, TN)
            di_within = _dot(ab, dob, TN)
            dqg, dkd, di, dse = [], [], [], []
            for l0 in reversed(range(0, GR, CH)):
                ls = slice(l0, l0 + CH)
                st = st_ref[(g0 + l0) // CH]
                dstb = dst.astype(BF16)
                dqg.insert(0, _dot(dob[ls], st.astype(BF16)))
                dkd.insert(0, _dot(ib[ls], dstb))
                di.insert(0, _dot(kdb[ls], dstb, NT))
                dse.insert(0, jnp.broadcast_to(jnp.sum(dst * st, axis=0, keepdims=True), (CH, HD)))
                dst = dst * egl_s[g0 + l0:g0 + l0 + 1, :] + _dot(dob[ls], qgb[ls], TN)
            dqg, dkd, di, dse = (jnp.concatenate(p, axis=0) for p in (dqg, dkd, di, dse))
            dqfi_ref[2, rs, :] = (di + di_within).astype(BF16)
            dgc = dqt * qt - dkt * kt + dqg * qg - dkd * kd
            dg = _cum2l(t_dec, dgc) + _cum2l(t_same, dkd * kd) + dse * e_gl
            t1 = dg / f - (dkt * e_k + dkd * e_l)
            dqfi_ref[1, rs, :] = ((1.0 - lbv) * t1 * sp * sn).astype(BF16)
            dqfi_ref[0, rs, :] = ((dqt * e_q + dqg * e_g) * (sq * (1.0 + qr * (1.0 - sq)))).astype(BF16)
            dlb = dlb + jnp.sum(sn * t1, axis=0, keepdims=True)

        @pl.when(pl.program_id(1) == 0)
        def _():
            dlb_ref[...] = dlb

        @pl.when(pl.program_id(1) != 0)
        def _():
            dlb_ref[...] += dlb

    def slot(n):
        return pl.BlockSpec((None, None, S, HD), lambda h, b: (n, b, 0, h))

    return pl.pallas_call(
        body, name="hg_bwd", grid=(HEADS, Bl),
        in_specs=[slot(0), slot(1), slot(2),
                  pl.BlockSpec((None, None, S // CH, HD, HD), lambda h, b: (b, h, 0, 0, 0)),
                  pl.BlockSpec((None, S, HD), lambda h, b: (b, 0, h)),
                  pl.BlockSpec((2, HD), lambda h, b: (0, h)), ANY],
        out_specs=[pl.BlockSpec((3, None, S, HD), lambda h, b: (0, b, 0, h)),
                   pl.BlockSpec((1, HD), lambda h, b: (0, h))],
        out_shape=[jax.ShapeDtypeStruct(dproj.shape, dproj.dtype), jax.ShapeDtypeStruct((1, D), F32)],
        scratch_shapes=[pltpu.VMEM((S, HD), F32)],
        input_output_aliases={6: 0},
        compiler_params=_cparams(("parallel", "arbitrary")),
    )(proj, proj, proj, states, do, lb_logits, dproj)


def _place():
    x, y, c = lax.axis_index("x"), lax.axis_index("y"), lax.axis_index("c")
    return x, y, c, [(1 - x, y), (x, 1 - y), (1 - x, 1 - y)]


def _remote(src, dst, ssem, rsem, dev):
    return pltpu.make_async_remote_copy(src_ref=src, dst_ref=dst, send_sem=ssem, recv_sem=rsem,
                                        device_id=dev, device_id_type=MESH)


GATHER_SEMS = 9


def _gather_ops(wp_ref, wall_ref, ssem, rsem, hn_ref=None, hnall_ref=None):
    half = wp_ref.shape[0] // 2

    def place():
        x, y, c, chips = _place()
        return x, y, c, chips, 2 * x + y, pl.ds(c * half, half), pl.ds((1 - c) * half, half)

    def first_sends():
        x, y, c, chips, b, mine, _ = place()
        cps = [_remote(wp_ref.at[mine], wall_ref.at[b, mine], ssem.at[j], rsem.at[j], (*chip, c))
               for j, chip in enumerate(chips)]
        if hn_ref is not None:
            cps += [_remote(hn_ref, hnall_ref.at[b], ssem.at[6 + j], rsem.at[6 + j], (*chip, c))
                    for j, chip in enumerate(chips)]
        return cps

    def forwards():
        x, y, c, chips, _, mine, _ = place()
        return [_remote(wall_ref.at[2 * cx + cy, mine], wall_ref.at[2 * cx + cy, mine],
                        ssem.at[3 + j], rsem.at[3 + j], (x, y, 1 - c)) for j, (cx, cy) in enumerate(chips)]

    def start():
        for cp in first_sends():
            cp.start()

    def forward():
        x, y, c, chips, _, mine, _ = place()
        for j, (cx, cy) in enumerate(chips):
            landed = wall_ref.at[2 * cx + cy, mine]
            _remote(landed, landed, ssem.at[j], rsem.at[j], (cx, cy, c)).wait_recv()
        for cp in forwards():
            cp.start()

    def finish():
        x, y, c, chips, _, _, other = place()
        for j, (cx, cy) in enumerate(chips):
            passed = wall_ref.at[2 * cx + cy, other]
            _remote(passed, passed, ssem.at[3 + j], rsem.at[3 + j], (x, y, 1 - c)).wait_recv()
            if hn_ref is not None:
                row = hnall_ref.at[2 * cx + cy]
                _remote(row, row, ssem.at[6 + j], rsem.at[6 + j], (cx, cy, c)).wait_recv()
        for cp in first_sends() + forwards():
            cp.wait_send()

    return start, forward, finish


def _gather_weights(wp):
    def body(wp_ref, wall_ref, ssem, rsem):
        for step in _gather_ops(wp_ref, wall_ref, ssem, rsem):
            step()

    return pl.pallas_call(
        body, name="gather_weights", in_specs=[ANY], out_specs=ANY,
        out_shape=jax.ShapeDtypeStruct((NPROJ,) + wp.shape, BF16),
        scratch_shapes=[pltpu.SemaphoreType.DMA((GATHER_SEMS,)), pltpu.SemaphoreType.DMA((GATHER_SEMS,))],
    )(wp)


def _pair_ops(g_refs, r_refs, ssem, rsem):
    def copies():
        x, y, c, _ = _place()
        return [_remote(g.at[n, 1 - c], r.at[n], ssem.at[t * NPROJ + n], rsem.at[t * NPROJ + n], (x, y, 1 - c))
                for t, (g, r) in enumerate(zip(g_refs, r_refs)) for n in range(NPROJ)]

    def start():
        for cp in copies():
            cp.start()

    def finish():
        x, y, c, _ = _place()
        for t, r in enumerate(r_refs):
            for n in range(NPROJ):
                k = t * NPROJ + n
                _remote(r.at[n], r.at[n], ssem.at[k], rsem.at[k], (x, y, 1 - c)).wait_recv()
        for cp in copies():
            cp.wait_send()

    return start, finish


def _pair_shapes(grads):
    return ([jax.ShapeDtypeStruct((NPROJ,) + g.shape[2:], F32) for g in grads],
            [pltpu.SemaphoreType.DMA((len(grads) * NPROJ,))] * 2)


def _pair_exchange(grads):
    ng = len(grads)

    def body(*refs):
        start, finish = _pair_ops(refs[:ng], refs[ng:2 * ng], *refs[2 * ng:])
        start()
        finish()

    out_shape, sems = _pair_shapes(grads)
    return pl.pallas_call(
        body, name="pair_exchange", in_specs=[ANY] * ng, out_specs=[ANY] * ng,
        out_shape=out_shape, scratch_shapes=sems,
    )(*grads)


def _chip_ops(s_refs, r_refs, ssem, rsem):
    def copies():
        x, y, c, chips = _place()
        return [_remote(s.at[2 * cx + cy], r.at[2 * x + y], ssem.at[3 * t + j], rsem.at[3 * t + j], (cx, cy, c))
                for t, (s, r) in enumerate(zip(s_refs, r_refs)) for j, (cx, cy) in enumerate(chips)]

    def start():
        for cp in copies():
            cp.start()

    def finish():
        x, y, c, chips = _place()
        for t, r in enumerate(r_refs):
            for j, (cx, cy) in enumerate(chips):
                slot = r.at[2 * cx + cy]
                _remote(slot, slot, ssem.at[3 * t + j], rsem.at[3 * t + j], (cx, cy, c)).wait_recv()
        for cp in copies():
            cp.wait_send()

    return start, finish


def _chip_shapes(sums):
    return ([jax.ShapeDtypeStruct(s.shape, s.dtype) for s in sums],
            [pltpu.SemaphoreType.DMA((3 * len(sums),))] * 2)


def _sibling_share(halves, pack):
    ng = len(halves)

    def body(*refs):
        h_refs, pack_ref = refs[:ng], refs[ng]
        f_refs, allp_ref = refs[ng + 1:2 * ng + 1], refs[2 * ng + 1]
        ssem, rsem, psend, precv, lsem = refs[2 * ng + 2:]
        x, y, c, _ = _place()
        me = 4 * x + 2 * y + c
        local = pltpu.make_async_copy(pack_ref, allp_ref.at[me], lsem)
        local.start()
        flips = [(fx, fy, fc) for fx in (0, 1) for fy in (0, 1) for fc in (0, 1)][1:]
        peers = [(fx + x - 2 * fx * x, fy + y - 2 * fy * y, fc + c - 2 * fc * c) for fx, fy, fc in flips]
        sends = [_remote(pack_ref, allp_ref.at[me], psend.at[m], precv.at[m], peer) for m, peer in enumerate(peers)]
        sends += [_remote(h, f, ssem.at[t], rsem.at[t], (x, y, 1 - c))
                  for t, (h, f) in enumerate(zip(h_refs, f_refs))]
        for cp in sends:
            cp.start()
        for t, f in enumerate(f_refs):
            _remote(f, f, ssem.at[t], rsem.at[t], (x, y, 1 - c)).wait_recv()
        for m, (px, py, pc) in enumerate(peers):
            row = allp_ref.at[4 * px + 2 * py + pc]
            _remote(row, row, psend.at[m], precv.at[m], (px, py, pc)).wait_recv()
        for cp in sends:
            cp.wait_send()
        local.wait()

    return pl.pallas_call(
        body, name="sibling_share", in_specs=[ANY] * (ng + 1), out_specs=[ANY] * (ng + 1),
        out_shape=[jax.ShapeDtypeStruct(h.shape, F32) for h in halves]
        + [jax.ShapeDtypeStruct((8,) + pack.shape, F32)],
        scratch_shapes=[pltpu.SemaphoreType.DMA((ng,)), pltpu.SemaphoreType.DMA((ng,)),
                        pltpu.SemaphoreType.DMA((7,)), pltpu.SemaphoreType.DMA((7,)), pltpu.SemaphoreType.DMA],
    )(*halves, pack)


def _pair_add(own, recv, cidx, name):
    R = own.shape[2]
    tr = min(256, R)

    def body(c_ref, a_ref, b_ref, o_ref):
        o_ref[...] = (a_ref[...] + b_ref[...]).astype(BF16)

    return pl.pallas_call(
        body, name=name,
        grid_spec=pltpu.PrefetchScalarGridSpec(
            num_scalar_prefetch=1, grid=(NPROJ, R // tr),
            in_specs=[pl.BlockSpec((None, None, tr, D), lambda n, r, c: (n, c[0], r, 0)),
                      pl.BlockSpec((None, tr, D), lambda n, r, c: (n, r, 0))],
            out_specs=pl.BlockSpec((None, tr, D), lambda n, r, c: (n, r, 0))),
        out_shape=jax.ShapeDtypeStruct(recv.shape, BF16),
        compiler_params=_cparams(("parallel", "parallel")),
    )(cidx, own, recv)


def _chip_sum(sums, parts, bidx, name):
    R = parts.shape[1]
    tr = min(256, R)

    def body(b_ref, s_ref, p_ref, o_ref):
        acc = None
        for j in range(NPROJ):
            term = jnp.where(b_ref[0] == j, s_ref[...], p_ref[j]).astype(F32)
            acc = term if acc is None else acc + term
        o_ref[...] = acc

    return pl.pallas_call(
        body, name=name,
        grid_spec=pltpu.PrefetchScalarGridSpec(
            num_scalar_prefetch=1, grid=(R // tr,),
            in_specs=[pl.BlockSpec((None, tr, D), lambda r, b: (b[0], r, 0)),
                      pl.BlockSpec((NPROJ, tr, D), lambda r, b: (0, r, 0))],
            out_specs=pl.BlockSpec((tr, D), lambda r, b: (r, 0))),
        out_shape=jax.ShapeDtypeStruct((R, D), F32),
        compiler_params=_cparams(("parallel",)),
    )(bidx, sums, parts)


def _adamw_math(w, g, m, v):
    m = ADAM_B1 * m + (1.0 - ADAM_B1) * g
    v = ADAM_B2 * v + (1.0 - ADAM_B2) * (g * g)
    m_hat = m / (1.0 - ADAM_B1 ** ADAM_STEP)
    v_hat = v / (1.0 - ADAM_B2 ** ADAM_STEP)
    delta = -ADAM_LR * (m_hat / (jnp.sqrt(v_hat) + ADAM_EPS) + ADAM_WD * w)
    return delta, m, v


def _adamw(w, mine, theirs, m, v, cidx, name):
    R = mine.shape[0]
    tr = min(256, R)
    nr = R // tr

    def body(c_ref, w_ref, a_ref, b_ref, m_ref, v_ref, g_ref, d_ref, nm_ref, nv_ref):
        g = jnp.where(pl.program_id(0) == c_ref[0], a_ref[...], b_ref[...])
        g_ref[...] = g
        d_ref[...], nm_ref[...], nv_ref[...] = _adamw_math(w_ref[...], g, m_ref[...], v_ref[...])

    full = pl.BlockSpec((tr, D), lambda h, r, c: (h * nr + r, 0))
    half = pl.BlockSpec((tr, D), lambda h, r, c: (r, 0))
    return pl.pallas_call(
        body, name=name,
        grid_spec=pltpu.PrefetchScalarGridSpec(
            num_scalar_prefetch=1, grid=(2, nr),
            in_specs=[full, half, half, full, full], out_specs=[full] * 4),
        out_shape=[jax.ShapeDtypeStruct(w.shape, F32)] * 4,
        compiler_params=_cparams(("parallel", "parallel")),
    )(cidx, w, mine, theirs, m, v)


PACK_ROWS = 8


def _small_update(allp, bidx, logits, weights, moments_m, moments_v):
    shapes = [w.shape for w in weights]
    q4 = D // NPROJ

    def body(b_ref, allp_ref, hgp_ref, lg_ref, *refs):
        w_refs, m_refs, v_refs = refs[0:6], refs[6:12], refs[12:18]
        loss_ref = refs[18]
        g_out, d_out, m_out, v_out = refs[19:25], refs[25:31], refs[31:37], refs[37:43]

        def total(ref, row, lo, hi):
            acc = ref[0, row:row + 1, lo:hi]
            for dev in range(1, 8):
                acc = acc + ref[dev, row:row + 1, lo:hi]
            return acc

        _, pp = _lower_bound(lg_ref[...])
        dlb = total(allp_ref, 2, 0, D)
        grads = [total(allp_ref, 0, 0, D), total(allp_ref, 4, 0, HD), total(allp_ref, 4, HD, 2 * HD),
                 total(hgp_ref, 1, 0, q4), total(allp_ref, 4, 2 * HD, 3 * HD), None]
        loss_ref[...] = (0.5 / D) * jnp.sum(total(allp_ref, 3, 0, D), axis=1, keepdims=True)
        for t in range(6):
            if t < 5:
                rows = [(slice(None), grads[t])]
            else:
                rows = [(slice(0, 1), -pp * dlb), (slice(1, 2), pp * dlb)]
            for rs, g in rows:
                g_out[t][rs, :] = g
                d_out[t][rs, :], m_out[t][rs, :], v_out[t][rs, :] = _adamw_math(
                    w_refs[t][rs, :], g, m_refs[t][rs, :], v_refs[t][rs, :])

    whole = [pl.BlockSpec(s, lambda i, b: (0, 0)) for s in shapes]
    return pl.pallas_call(
        body, name="small_update",
        grid_spec=pltpu.PrefetchScalarGridSpec(
            num_scalar_prefetch=1, grid=(1,),
            in_specs=[pl.BlockSpec((8, PACK_ROWS, D), lambda i, b: (0, 0, 0)),
                      pl.BlockSpec((8, PACK_ROWS, q4), lambda i, b: (0, 0, b[0])),
                      pl.BlockSpec((2, D), lambda i, b: (0, 0))] + whole * 3,
            out_specs=[pl.BlockSpec((1, 1), lambda i, b: (0, 0))] + whole * 4),
        out_shape=[jax.ShapeDtypeStruct((1, 1), F32)] + [jax.ShapeDtypeStruct(s, F32) for s in shapes] * 4,
        compiler_params=_cparams(("arbitrary",)),
    )(bidx, allp, allp, logits, *weights, *moments_m, *moments_v)


def kernel(x, sb_norm, sb_w_in, sb_q_gain, sb_k_gain, sb_w_out, hg_norm, hg_w_in, hg_o_gain, hg_w_out, hg_lb_logits, loss_target, m_sb_norm, m_sb_w_in, m_sb_q_gain, m_sb_k_gain, m_sb_w_out, m_hg_norm, m_hg_w_in, m_hg_o_gain, m_hg_w_out, m_hg_lb_logits, v_sb_norm, v_sb_w_in, v_sb_q_gain, v_sb_k_gain, v_sb_w_out, v_hg_norm, v_hg_w_in, v_hg_o_gain, v_hg_w_out, v_hg_lb_logits):
    Bl, S, _ = x.shape
    T = Bl * S
    cidx = lax.axis_index("c").astype(jnp.int32).reshape(1)
    bidx = (2 * lax.axis_index("x") + lax.axis_index("y")).astype(jnp.int32).reshape(1)

    def in_hbm(arrays):
        return [pltpu.with_memory_space_constraint(a, pltpu.HBM) for a in arrays]

    def own_slot(gathered, mine):
        return lax.dynamic_update_slice(gathered, mine[None], (bidx[0],) + (0,) * mine.ndim)

    def halved(g):
        return g.reshape(NPROJ, 2, g.shape[-2] * g.shape[0] // (2 * NPROJ), D)

    def heads(a):
        return a.reshape(a.shape[:-2] + (Bl, S, D))

    def flat(a):
        return a.reshape(a.shape[:-3] + (T, D))

    wp_first = sb_w_in[0].astype(BF16)
    wp_rest = jnp.concatenate([hg_w_in[0], sb_w_out[0], hg_w_out[0]], axis=0).astype(BF16)
    wall_first, = in_hbm([own_slot(_gather_weights(wp_first), wp_first)])
    x2 = x.reshape(T, D)
    tgt = loss_target.reshape(T, D)

    proj0 = _in_proj_fwd(x2, sb_norm, wall_first, W_IN, "sb_in_fwd")
    o0, ctot, wall_rest, hnall = _sb_fwd(heads(proj0), sb_q_gain, sb_k_gain, wp_rest, hg_norm)
    wall_rest, = in_hbm([own_slot(wall_rest, wp_rest)])
    hgn = own_slot(hnall, hg_norm).reshape(1, D)
    h1 = _out_proj_fwd(flat(o0), proj0, x2, wall_rest, W_OUT_SB, "sb_out_fwd")
    proj1 = _in_proj_fwd(h1, hgn, wall_rest, W_IN, "hg_in_fwd")
    o1, states = _hg_fwd(heads(proj1), hg_lb_logits)
    dh2, loss_terms = _out_proj_fwd(flat(o1), proj1, h1, wall_rest, W_OUT_HG, "hg_out_fwd",
                                    o_gain=hg_o_gain, target=tgt)

    do1, dproj1, gout_hg, d_ogain = _out_proj_bwd(dh2, flat(o1), proj1, wall_rest, W_OUT_HG, "hg_out_bwd",
                                                  o_gain=hg_o_gain)
    dproj1, dlb = _hg_bwd(heads(proj1), states, heads(do1), heads(dproj1), hg_lb_logits)
    dproj1 = flat(dproj1)
    dh1, d_hgn = _in_proj_bwd_x(dproj1, wall_rest, W_IN, h1, hgn, dh2, "hg_in_bwd_x")
    big_hg = in_hbm([halved(_in_proj_bwd_w(dproj1, h1, hgn, "hg_in_bwd_w")), halved(gout_hg)])
    do0, dproj0, gout_sb, *recv_hg = _out_proj_bwd(dh1, flat(o0), proj0, wall_rest, W_OUT_SB, "sb_out_bwd",
                                                   exchange=big_hg)
    sums_hg = in_hbm(_pair_add(g, r, cidx, "pair_add_" + nm)
                     for g, r, nm in zip(big_hg, in_hbm(recv_hg), ("hg_in", "hg_out")))
    dproj0, d_qg, d_kg, *parts_hg = _sb_bwd(heads(proj0), ctot, heads(do0), heads(dproj0),
                                            sb_q_gain, sb_k_gain, sums_hg)
    dproj0 = flat(dproj0)

    big_sb = in_hbm([halved(_in_proj_bwd_w(dproj0, x2, sb_norm, "sb_in_bwd_w")), halved(gout_sb)])
    sums_sb = in_hbm(_pair_add(g, r, cidx, "pair_add_" + nm)
                     for g, r, nm in zip(big_sb, in_hbm(_pair_exchange(big_sb)), ("sb_in", "sb_out")))
    grad_x, d_sbn, *parts_sb = _in_proj_bwd_x(dproj0, wall_first, W_IN, x2, sb_norm, dh1, "sb_in_bwd_x",
                                              exchange=sums_sb)

    names = ["sb_in", "hg_in", "sb_out", "hg_out"]
    sums = [sums_sb[0], sums_hg[0], sums_sb[1], sums_hg[1]]
    parts = [parts_sb[0], parts_hg[0], parts_sb[1], parts_hg[1]]
    halves = in_hbm(_chip_sum(sm, p, bidx, "chip_sum_" + nm) for sm, p, nm in zip(sums, in_hbm(parts), names))
    gains = jnp.concatenate([d_qg, d_kg, d_ogain, jnp.zeros((1, D - 3 * HD), F32)], axis=1)
    pack = jnp.concatenate([d_sbn, d_hgn, dlb, loss_terms, gains, jnp.zeros((3, D), F32)], axis=0)
    *theirs, allp = _sibling_share(halves, pack)
    theirs = in_hbm(theirs)

    big_w = [sb_w_in, hg_w_in, sb_w_out, hg_w_out]
    big_m = [m_sb_w_in, m_hg_w_in, m_sb_w_out, m_hg_w_out]
    big_v = [v_sb_w_in, v_hg_w_in, v_sb_w_out, v_hg_w_out]
    upd = [_adamw(*in_hbm([w[0], a, b, m[0], v[0]]), cidx, "adamw_" + nm)
           for w, a, b, m, v, nm in zip(big_w, halves, theirs, big_m, big_v, names)]
    (g_sb_in, d_sb_in, nm_sb_in, nv_sb_in), (g_hg_in, d_hg_in, nm_hg_in, nv_hg_in), \
        (g_sb_out, d_sb_out, nm_sb_out, nv_sb_out), (g_hg_out, d_hg_out, nm_hg_out, nv_hg_out) = [
            tuple(a[None] for a in u) for u in upd]

    small = _small_update(
        allp, bidx, hg_lb_logits,
        [sb_norm, sb_q_gain, sb_k_gain, hg_norm, hg_o_gain, hg_lb_logits],
        [m_sb_norm, m_sb_q_gain, m_sb_k_gain, m_hg_norm, m_hg_o_gain, m_hg_lb_logits],
        [v_sb_norm, v_sb_q_gain, v_sb_k_gain, v_hg_norm, v_hg_o_gain, v_hg_lb_logits])
    loss = small[0].reshape(())
    (g_sbn, g_qg, g_kg, g_hgn, g_og, g_lb) = small[1:7]
    (d_sbn2, d_qg2, d_kg2, d_hgn2, d_og2, d_lb2) = small[7:13]
    (nm_sbn, nm_qg, nm_kg, nm_hgn, nm_og, nm_lb) = small[13:19]
    (nv_sbn, nv_qg, nv_kg, nv_hgn, nv_og, nv_lb) = small[19:25]

    return (loss, grad_x.reshape(Bl, S, D),
            g_sbn, g_sb_in, g_qg, g_kg, g_sb_out, g_hgn, g_hg_in, g_og, g_hg_out, g_lb,
            d_sbn2, d_sb_in, d_qg2, d_kg2, d_sb_out, d_hgn2, d_hg_in, d_og2, d_hg_out, d_lb2,
            nm_sbn, nm_sb_in, nm_qg, nm_kg, nm_sb_out, nm_hgn, nm_hg_in, nm_og, nm_hg_out, nm_lb,
            nv_sbn, nv_sb_in, nv_qg, nv_kg, nv_sb_out, nv_hgn, nv_hg_in, nv_og, nv_hg_out, nv_lb)
```

```python
import functools

import jax
import jax.numpy as jnp
from jax import lax
from jax.experimental import pallas as pl
from jax.experimental.pallas import tpu as pltpu

F32 = jnp.float32
BF16 = jnp.bfloat16
MESH = pl.DeviceIdType.MESH
ANY = pl.BlockSpec(memory_space=pl.ANY)

D = 1024
HEADS = 8
HD = 128
NPROJ = 4
RMS_EPS = 1e-6
TK = 256
CH = 64
CH_LOG2 = 6
GR = 128
SCALE = HD ** -0.5
EXP_CLAMP = 60.0
W_IN, W_OUT_SB, W_OUT_HG = 0, 4, 5

ADAM_LR = 0.001
ADAM_B1 = 0.9
ADAM_B2 = 0.999
ADAM_EPS = 1e-08
ADAM_WD = 0.01
ADAM_STEP = 10

NT = (((1,), (1,)), ((), ()))
TN = (((0,), (0,)), ((), ()))
MIB = 1024 * 1024


def _cparams(sem=None, vmem_mib=40):
    return pltpu.CompilerParams(dimension_semantics=sem, vmem_limit_bytes=vmem_mib * MIB)


def _dot(a, b, dims=None):
    if dims is None:
        return jnp.dot(a, b, preferred_element_type=F32)
    return lax.dot_general(a, b, dims, preferred_element_type=F32)


def _sigmoid(x):
    return 1.0 / (1.0 + jnp.exp(-x))


def _rms(x):
    return lax.rsqrt(jnp.mean(x * x, axis=-1, keepdims=True) + RMS_EPS)


def _rms_bwd(x, r, gain, dy):
    a = dy * gain
    dx = r * a - x * (r * r * r) * jnp.mean(x * a, axis=-1, keepdims=True)
    return dx, dy * (x * r)


def _split2(v):
    hi = v.astype(BF16)
    lo = (v - hi.astype(F32)).astype(BF16)
    return hi, lo


def _cum2(v, u):
    hi, lo = _split2(v)
    return _dot(hi, u) + _dot(lo, u)


def _dot3(a, b, dims=None):
    ah, al = _split2(a)
    bh, bl = _split2(b)
    return _dot(ah, bh, dims) + _dot(ah, bl, dims) + _dot(al, bh, dims)


def _cum2l(u, v):
    hi, lo = _split2(v)
    return _dot(u, hi) + _dot(u, lo)


def _in_proj_fwd(h, gain, wall, wblk, name):
    T = h.shape[0]
    tm = min(1024, T)

    def body(h_ref, g_ref, w_ref, o_ref, u_s):
        rows = pl.ds(pl.multiple_of(pl.program_id(1) * tm, tm), tm)

        @pl.when(pl.program_id(0) == 0)
        def _():
            x = h_ref[...]
            u_s[rows, :] = (x * _rms(x) * g_ref[...]).astype(BF16)

        o_ref[...] = _dot(u_s[rows, :], w_ref[...]).astype(BF16)

    return pl.pallas_call(
        body, name=name, grid=(NPROJ, T // tm),
        in_specs=[pl.BlockSpec((tm, D), lambda n, i: (jnp.where(n == 0, i, 0), 0)),
                  pl.BlockSpec((1, D), lambda n, i: (0, 0)),
                  pl.BlockSpec((None, D, D), lambda n, i: (n, wblk, 0))],
        out_specs=pl.BlockSpec((None, tm, D), lambda n, i: (n, i, 0)),
        out_shape=jax.ShapeDtypeStruct((NPROJ, T, D), BF16),
        scratch_shapes=[pltpu.VMEM((T, D), BF16)],
        compiler_params=_cparams(("arbitrary", "arbitrary")),
    )(h, gain, wall)


def _head_norm(x):
    outs = []
    for hh in range(x.shape[1] // HD):
        xs = x[:, hh * HD:(hh + 1) * HD]
        outs.append((xs, _rms(xs)))
    return outs


def _w_out_specs(wblk):
    kb = D // NPROJ
    return [pl.BlockSpec((None, kb, D), functools.partial(lambda j, i: (j, wblk, 0), j)) for j in range(NPROJ)]


def _out_proj_fwd(o, proj, resid, wall, wblk, name, o_gain=None, target=None):
    T = o.shape[0]
    tm = min(512, T)
    kb = D // NPROJ
    with_loss = target is not None

    def body(*refs):
        o_ref, g_ref, r_ref = refs[:3]
        w_refs = refs[3:3 + NPROJ]
        if with_loss:
            og_ref, t_ref, dh_ref, ls_ref = refs[3 + NPROJ:]
        else:
            h_ref, = refs[3 + NPROJ:]
        x = o_ref[...]
        if with_loss:
            x = jnp.concatenate([xs * r * og_ref[...] for xs, r in _head_norm(x)], axis=1)
        g = g_ref[...].astype(F32)
        a = (x * (g * _sigmoid(g))).astype(BF16)
        hnew = r_ref[...]
        for j in range(NPROJ):
            hnew = hnew + _dot(a[:, j * kb:(j + 1) * kb], w_refs[j][...])
        if with_loss:
            err = hnew - t_ref[...]
            dh_ref[...] = err * (1.0 / D)
            part = jnp.sum(err * err, axis=0, keepdims=True)

            @pl.when(pl.program_id(0) == 0)
            def _():
                ls_ref[...] = part

            @pl.when(pl.program_id(0) != 0)
            def _():
                ls_ref[...] += part
        else:
            h_ref[...] = hnew

    tile = pl.BlockSpec((tm, D), lambda i: (i, 0))
    in_specs = [tile, pl.BlockSpec((None, tm, D), lambda i: (3, i, 0)), tile] + _w_out_specs(wblk)
    args = [o, proj, resid] + [wall] * NPROJ
    out_specs = tile
    out_shape = jax.ShapeDtypeStruct((T, D), F32)
    if with_loss:
        in_specs += [pl.BlockSpec((1, HD), lambda i: (0, 0)), tile]
        args += [o_gain, target]
        out_specs = [tile, pl.BlockSpec((1, D), lambda i: (0, 0))]
        out_shape = [out_shape, jax.ShapeDtypeStruct((1, D), F32)]
    return pl.pallas_call(
        body, name=name, grid=(T // tm,), in_specs=in_specs, out_specs=out_specs,
        out_shape=out_shape, compiler_params=_cparams(("arbitrary",)),
    )(*args)


def _out_proj_bwd(dy, o, proj, wall, wblk, name, o_gain=None, exchange=()):
    T = o.shape[0]
    tm = min(512, T)
    kb = D // NPROJ
    normed = o_gain is not None
    ne = len(exchange)

    def body(*refs):
        it = iter(refs)
        dy_ref, o_ref, g_ref = (next(it) for _ in range(3))
        w_refs = [next(it) for _ in range(NPROJ)]
        og_ref = next(it) if normed else None
        xg_refs = [next(it) for _ in range(ne)]
        do_ref, dg_ref, dw_ref = (next(it) for _ in range(3))
        dgain_ref = next(it) if normed else None
        xr_refs = [next(it) for _ in range(ne)]
        wt_s = next(it)
        first = pl.program_id(0) == 0
        if ne:
            start, finish = _pair_ops(xg_refs, xr_refs, next(it), next(it))
            pl.when(first)(start)

        @pl.when(first)
        def _():
            for j in range(NPROJ):
                wt_s[:, j * kb:(j + 1) * kb] = w_refs[j][...].T
        g = g_ref[...].astype(F32)
        s = _sigmoid(g)
        sl = g * s
        x = o_ref[...]
        if normed:
            heads = _head_norm(x)
            on = jnp.concatenate([xs * r * og_ref[...] for xs, r in heads], axis=1)
        else:
            on = x
        dyb = dy_ref[...].astype(BF16)
        a = (on * sl).astype(BF16)
        at = a.T
        for j in range(NPROJ):
            part = _dot(at[j * kb:(j + 1) * kb, :], dyb)

            @pl.when(first)
            def _():
                dw_ref[j] = part

            @pl.when(jnp.logical_not(first))
            def _():
                dw_ref[j] += part

        da = _dot(dyb, wt_s[...])
        d_on = da * sl
        dg_ref[...] = (da * on * (s * (1.0 + g * (1.0 - s)))).astype(BF16)
        if normed:
            dxs, gsum = [], None
            for hh, (xs, r) in enumerate(heads):
                dx, gt = _rms_bwd(xs, r, og_ref[...], d_on[:, hh * HD:(hh + 1) * HD])
                dxs.append(dx)
                gt = jnp.sum(gt, axis=0, keepdims=True)
                gsum = gt if gsum is None else gsum + gt
            do_ref[...] = jnp.concatenate(dxs, axis=1).astype(BF16)

            @pl.when(first)
            def _():
                dgain_ref[...] = gsum

            @pl.when(jnp.logical_not(first))
            def _():
                dgain_ref[...] += gsum
        else:
            do_ref[...] = d_on.astype(BF16)
        if ne:
            pl.when(pl.program_id(0) == T // tm - 1)(finish)

    tile = pl.BlockSpec((tm, D), lambda i: (i, 0))
    gate = pl.BlockSpec((None, tm, D), lambda i: (3, i, 0))
    in_specs = [tile, tile, gate] + _w_out_specs(wblk)
    args = [dy, o, proj] + [wall] * NPROJ
    out_specs = [tile, gate, pl.BlockSpec((NPROJ, kb, D), lambda i: (0, 0, 0))]
    out_shape = [jax.ShapeDtypeStruct((T, D), BF16),
                 jax.ShapeDtypeStruct((NPROJ, T, D), BF16),
                 jax.ShapeDtypeStruct((NPROJ, kb, D), F32)]
    if normed:
        in_specs.append(pl.BlockSpec((1, HD), lambda i: (0, 0)))
        args.append(o_gain)
        out_specs.append(pl.BlockSpec((1, HD), lambda i: (0, 0)))
        out_shape.append(jax.ShapeDtypeStruct((1, HD), F32))
    x_shape, x_sems = _pair_shapes(exchange) if ne else ([], [])
    return pl.pallas_call(
        body, name=name, grid=(T // tm,), in_specs=in_specs + [ANY] * ne, out_specs=out_specs + [ANY] * ne,
        out_shape=out_shape + x_shape, scratch_shapes=[pltpu.VMEM((D, D), BF16)] + x_sems,
        compiler_params=_cparams(("arbitrary",), vmem_mib=48),
    )(*args, *exchange)


def _in_proj_bwd_x(dproj, wall, wblk, h, gain, dres, name, exchange=()):
    T = h.shape[0]
    tm = min(512, T)
    ne = len(exchange)

    def body(d_ref, w_ref, h_ref, g_ref, r_ref, *refs):
        xs_refs, (dh_ref, dgain_ref), xr_refs = refs[:ne], refs[ne:ne + 2], refs[ne + 2:2 * ne + 2]
        du, wt_s = refs[2 * ne + 2:2 * ne + 4]
        i, n = pl.program_id(0), pl.program_id(1)
        if ne:
            start, finish = _chip_ops(xs_refs, xr_refs, *refs[2 * ne + 4:])
            pl.when(jnp.logical_and(i == 0, n == 0))(start)

        @pl.when(i == 0)
        def _():
            wt_s[n] = w_ref[...].T

        part = _dot(d_ref[...], wt_s[n])

        @pl.when(n == 0)
        def _():
            du[...] = part

        @pl.when(n != 0)
        def _():
            du[...] += part

        @pl.when(n == NPROJ - 1)
        def _():
            x = h_ref[...]
            dx, gt = _rms_bwd(x, _rms(x), g_ref[...], du[...])
            dh_ref[...] = r_ref[...] + dx
            gt = jnp.sum(gt, axis=0, keepdims=True)

            @pl.when(i == 0)
            def _():
                dgain_ref[...] = gt

            @pl.when(i != 0)
            def _():
                dgain_ref[...] += gt

        if ne:
            pl.when(jnp.logical_and(i == T // tm - 1, n == NPROJ - 1))(finish)

    x_shape, x_sems = _chip_shapes(exchange) if ne else ([], [])
    return pl.pallas_call(
        body, name=name, grid=(T // tm, NPROJ),
        in_specs=[pl.BlockSpec((None, tm, D), lambda i, n: (n, i, 0)),
                  pl.BlockSpec((None, D, D), lambda i, n: (jnp.where(i == 0, n, NPROJ - 1), wblk, 0)),
                  pl.BlockSpec((tm, D), lambda i, n: (i, 0)),
                  pl.BlockSpec((1, D), lambda i, n: (0, 0)),
                  pl.BlockSpec((tm, D), lambda i, n: (i, 0))] + [ANY] * ne,
        out_specs=[pl.BlockSpec((tm, D), lambda i, n: (i, 0)),
                   pl.BlockSpec((1, D), lambda i, n: (0, 0))] + [ANY] * ne,
        out_shape=[jax.ShapeDtypeStruct((T, D), F32), jax.ShapeDtypeStruct((1, D), F32)] + x_shape,
        scratch_shapes=[pltpu.VMEM((tm, D), F32), pltpu.VMEM((NPROJ, D, D), BF16)] + x_sems,
        compiler_params=_cparams(("arbitrary", "arbitrary")),
    )(dproj, wall, h, gain, dres, *exchange)


def _in_proj_bwd_w(dproj, h, gain, name):
    T = h.shape[0]
    tk = min(1024, T)

    def body(d_ref, h_ref, g_ref, dw_ref, ut_s):
        k = pl.program_id(1)

        @pl.when(pl.program_id(0) == 0)
        def _():
            x = h_ref[...]
            ut_s[k] = (x * _rms(x) * g_ref[...]).astype(BF16).T

        part = _dot(ut_s[k], d_ref[...])

        @pl.when(k == 0)
        def _():
            dw_ref[...] = part

        @pl.when(k != 0)
        def _():
            dw_ref[...] += part

    return pl.pallas_call(
        body, name=name, grid=(NPROJ, T // tk),
        in_specs=[pl.BlockSpec((None, tk, D), lambda n, k: (n, k, 0)),
                  pl.BlockSpec((tk, D), lambda n, k: (jnp.where(n == 0, k, 0), 0)),
                  pl.BlockSpec((1, D), lambda n, k: (0, 0))],
        out_specs=pl.BlockSpec((None, D, D), lambda n, k: (n, 0, 0)),
        out_shape=jax.ShapeDtypeStruct((NPROJ, D, D), F32),
        scratch_shapes=[pltpu.VMEM((T // tk, D, tk), BF16)],
        compiler_params=_cparams(("arbitrary", "arbitrary")),
    )(dproj, h, gain)


def _log_sigmoid_pair(z):
    lb = jnp.minimum(z, 0.0) - jnp.log(1.0 + jnp.exp(-jnp.abs(z)))
    return lb, lb - z


def _slab_consts():
    t = lax.broadcasted_iota(jnp.int32, (TK, TK), 0)
    s = lax.broadcasted_iota(jnp.int32, (TK, TK), 1)
    return s < t, (t > s).astype(BF16), (t < s).astype(BF16)


def _slab_rows(k0, S):
    return [(r0, r1, masked) for r0, r1, masked in ((k0, k0 + TK, True), (k0 + TK, S, False)) if r0 < r1]


def _sb_fwd(proj, q_gain, k_gain, wp, hn):
    _, Bl, S, _ = proj.shape
    steps = Bl * HEADS

    def body(q_ref, k_ref, v_ref, qg_ref, kg_ref, wp_ref, hn_ref, o_ref, ct_ref, wall_ref, hnall_ref,
             qn, kn, vb, ssem, rsem):
        step = pl.program_id(0) * HEADS + pl.program_id(1)
        start, forward, finish = _gather_ops(wp_ref, wall_ref, ssem, rsem, hn_ref, hnall_ref)
        pl.when(step == 0)(start)
        pl.when(step == steps // 2)(forward)
        q = q_ref[...].astype(F32)
        qn[...] = (q * _rms(q) * (qg_ref[...] * SCALE)).astype(BF16)
        k = k_ref[...].astype(F32)
        kn[...] = (k * _rms(k) * kg_ref[...]).astype(BF16)
        vb[...] = v_ref[...]
        tri, u_gt, _ = _slab_consts()
        nb = S // TK
        c_blk = [jnp.zeros((TK, 1), F32)] * nb
        o_blk = [jnp.zeros((TK, HD), F32)] * nb
        for k0 in reversed(range(0, S, TK)):
            kb, vbb = kn[k0:k0 + TK, :], vb[k0:k0 + TK, :]
            for r0, r1, masked in _slab_rows(k0, S):
                blocks = range(r0 // TK, r1 // TK)
                z = _dot(qn[r0:r1, :], kb, NT)
                lb, ls = _log_sigmoid_pair(z)
                if masked:
                    ls = jnp.where(tri, ls, 0.0)
                c = jnp.concatenate([c_blk[b] for b in blocks], axis=0)
                w = jnp.exp(lb + _cum2(ls, u_gt) + c)
                if masked:
                    w = jnp.where(tri, w, 0.0)
                o_new = _dot(w.astype(BF16), vbb)
                c_new = jnp.sum(ls, axis=1, keepdims=True)
                for i, b in enumerate(blocks):
                    o_blk[b] = o_blk[b] + o_new[i * TK:(i + 1) * TK]
                    c_blk[b] = c_blk[b] + c_new[i * TK:(i + 1) * TK]
        o_ref[...] = jnp.concatenate(o_blk, axis=0)
        ct_ref[...] = jnp.concatenate(c_blk, axis=0)
        pl.when(step == steps - 1)(finish)

    def slot(n):
        return pl.BlockSpec((None, None, S, HD), lambda b, h: (n, b, 0, h))

    return pl.pallas_call(
        body, name="sb_fwd", grid=(Bl, HEADS),
        in_specs=[slot(0), slot(1), slot(2),
                  pl.BlockSpec((1, HD), lambda b, h: (0, 0)),
                  pl.BlockSpec((1, HD), lambda b, h: (0, 0)), ANY, ANY],
        out_specs=[pl.BlockSpec((None, S, HD), lambda b, h: (b, 0, h)),
                   pl.BlockSpec((None, None, S, 1), lambda b, h: (b, h, 0, 0)), ANY, ANY],
        out_shape=[jax.ShapeDtypeStruct((Bl, S, D), F32),
                   jax.ShapeDtypeStruct((Bl, HEADS, S, 1), F32),
                   jax.ShapeDtypeStruct((NPROJ,) + wp.shape, BF16),
                   jax.ShapeDtypeStruct((NPROJ,) + hn.shape, F32)],
        scratch_shapes=[pltpu.VMEM((S, HD), BF16)] * 3 + [pltpu.SemaphoreType.DMA((GATHER_SEMS,))] * 2,
        compiler_params=_cparams(("arbitrary", "arbitrary"), vmem_mib=56),
    )(proj, proj, proj, q_gain, k_gain, wp, hn)


def _sb_bwd(proj, ctot, do, dproj, q_gain, k_gain, exchange):
    _, Bl, S, _ = proj.shape
    ne = len(exchange)

    def body(q_ref, k_ref, v_ref, ct_ref, do_ref, qg_ref, kg_ref, _, *refs):
        xs_refs, (dqkv_ref, dqg_ref, dkg_ref), xr_refs = refs[:ne], refs[ne:ne + 3], refs[ne + 3:2 * ne + 3]
        qn, kn, vb, qt_s, dot_s, dqn, dkt_s, dvt_s, ssem, rsem = refs[2 * ne + 3:]
        step = pl.program_id(0) * HEADS + pl.program_id(1)
        first = step == 0
        start, finish = _chip_ops(xs_refs, xr_refs, ssem, rsem)

        @pl.when(first)
        def _():
            start()
            dqg_ref[...] = jnp.zeros_like(dqg_ref)
            dkg_ref[...] = jnp.zeros_like(dkg_ref)

        q = q_ref[...].astype(F32)
        rq = _rms(q)
        qn[...] = (q * rq * (qg_ref[...] * SCALE)).astype(BF16)
        k = k_ref[...].astype(F32)
        rk = _rms(k)
        kn[...] = (k * rk * kg_ref[...]).astype(BF16)
        vb[...] = v_ref[...]
        qt_s[...] = qn[...].T
        dot_s[...] = do_ref[...].T
        for acc in (dqn, dkt_s, dvt_s):
            acc[...] = jnp.zeros_like(acc)
        tri, u_gt, u_lt = _slab_consts()
        nb = S // TK
        passed, e = [jnp.zeros((TK, 1), F32)] * nb, [jnp.zeros((TK, 1), F32)] * nb
        for k0 in range(0, S, TK):
            keys = slice(k0, k0 + TK)
            kb, vbb = kn[keys, :], vb[keys, :]
            for r0, r1, masked in _slab_rows(k0, S):
                rows, blocks = slice(r0, r1), range(r0 // TK, r1 // TK)
                qb, dobb = qn[rows, :], do_ref[rows, :]
                z = _dot(qb, kb, NT)
                lb, ls = _log_sigmoid_pair(z)
                if masked:
                    ls = jnp.where(tri, ls, 0.0)
                p_new = jnp.concatenate([passed[b] for b in blocks], axis=0) + jnp.sum(ls, axis=1, keepdims=True)
                w = jnp.exp(lb + _cum2(ls, u_gt) + (ct_ref[rows, :] - p_new))
                if masked:
                    w = jnp.where(tri, w, 0.0)
                de = w * _dot(dobb, vbb, NT)
                e_old = jnp.concatenate([e[b] for b in blocks], axis=0)
                dls = e_old + _cum2(de, u_lt)
                e_new = e_old + jnp.sum(de, axis=1, keepdims=True)
                for i, b in enumerate(blocks):
                    passed[b], e[b] = p_new[i * TK:(i + 1) * TK], e_new[i * TK:(i + 1) * TK]
                sg = jnp.exp(lb)
                dz = de - sg * (de + dls)
                if masked:
                    dz = jnp.where(tri, dz, 0.0)
                dzb = dz.astype(BF16)
                dqn[rows, :] += _dot(dzb, kb)
                dkt_s[:, keys] += _dot(qt_s[:, rows], dzb)
                dvt_s[:, keys] += _dot(dot_s[:, rows], w.astype(BF16))

        dx, gt = _rms_bwd(q, rq, qg_ref[...], dqn[...] * SCALE)
        dqkv_ref[0] = dx.astype(BF16)
        dqg_ref[...] += jnp.sum(gt, axis=0, keepdims=True)
        dx, gt = _rms_bwd(k, rk, kg_ref[...], dkt_s[...].T)
        dqkv_ref[1] = dx.astype(BF16)
        dkg_ref[...] += jnp.sum(gt, axis=0, keepdims=True)
        dqkv_ref[2] = dvt_s[...].T.astype(BF16)
        pl.when(step == Bl * HEADS - 1)(finish)

    def slot(n):
        return pl.BlockSpec((None, None, S, HD), lambda b, h: (n, b, 0, h))

    head = pl.BlockSpec((None, S, HD), lambda b, h: (b, 0, h))
    gain = pl.BlockSpec((1, HD), lambda b, h: (0, 0))
    x_shape, x_sems = _chip_shapes(exchange)
    return pl.pallas_call(
        body, name="sb_bwd", grid=(Bl, HEADS),
        in_specs=[slot(0), slot(1), slot(2),
                  pl.BlockSpec((None, None, S, 1), lambda b, h: (b, h, 0, 0)), head, gain, gain, ANY] + [ANY] * ne,
        out_specs=[pl.BlockSpec((3, None, S, HD), lambda b, h: (0, b, 0, h)), gain, gain] + [ANY] * ne,
        out_shape=[jax.ShapeDtypeStruct(dproj.shape, dproj.dtype),
                   jax.ShapeDtypeStruct((1, HD), F32), jax.ShapeDtypeStruct((1, HD), F32)] + x_shape,
        scratch_shapes=([pltpu.VMEM((S, HD), BF16)] * 3 + [pltpu.VMEM((HD, S), BF16)] * 2 + [pltpu.VMEM((S, HD), F32)]
                        + [pltpu.VMEM((HD, S), F32)] * 2 + x_sems),
        input_output_aliases={7: 0},
        compiler_params=_cparams(("arbitrary", "arbitrary"), vmem_mib=56),
    )(proj, proj, proj, ctot, do, q_gain, k_gain, dproj, *exchange)


def _lower_bound(logits):
    l0, l1 = logits[0:1, :], logits[1:2, :]
    m = jnp.maximum(l0, l1)
    e0, e1 = jnp.exp(l0 - m), jnp.exp(l1 - m)
    p0, p1 = e0 / (e0 + e1), e1 / (e0 + e1)
    return (p0 + p1) - p0, p0 * p1


def _hg_gates(qr, fp, lbv):
    sq = _sigmoid(qr)
    sp = _sigmoid(fp)
    sn = 1.0 / (1.0 + jnp.exp(fp))
    f = lbv + (1.0 - lbv) * sp
    return qr * sq, sq, sp, sn, f, (1.0 - lbv) * sn


def _group_consts():
    t = lax.broadcasted_iota(jnp.int32, (GR, GR), 0)
    j = lax.broadcasted_iota(jnp.int32, (GR, GR), 1)
    same = lax.shift_right_logical(t, CH_LOG2) == lax.shift_right_logical(j, CH_LOG2)
    tril = jnp.logical_and(same, j <= t)
    return (tril, tril.astype(BF16), jnp.logical_and(same, j >= t).astype(BF16), same.astype(BF16))


def _hg_decays(qa, k, f, t_inc, t_same):
    g = jnp.log(f)
    gc = _cum2l(t_inc, g)
    gl = _cum2l(t_same, g)
    gm = gc - 0.5 * gl
    e_q = jnp.exp(jnp.minimum(gm, EXP_CLAMP))
    e_k = jnp.exp(jnp.minimum(-gm, EXP_CLAMP))
    e_g = jnp.exp(gc)
    e_l = jnp.exp(gl - gc)
    return qa * e_q, k * e_k, qa * e_g, k * e_l, e_q, e_k, e_g, e_l, jnp.exp(gl)


def _hg_fwd(proj, lb_logits):
    _, Bl, S, _ = proj.shape
    nc = S // CH

    def body(q_ref, f_ref, i_ref, lg_ref, o_ref, st_ref, egl_s):
        lbv, _ = _lower_bound(lg_ref[...])
        tril, t_inc, _, t_same = _group_consts()
        st = jnp.zeros((HD, HD), F32)
        for g0 in range(0, S, GR):
            rs = slice(g0, g0 + GR)
            qa, _, _, _, f, k = _hg_gates(q_ref[rs, :].astype(F32), f_ref[rs, :].astype(F32), lbv)
            qt, kt, qg, kd, _, _, _, _, e_gl = _hg_decays(qa, k, f, t_inc, t_same)
            a = jnp.where(tril, _dot(qt.astype(BF16), kt.astype(BF16), NT), 0.0)
            ib, qgb, kdb = i_ref[rs, :], qg.astype(BF16), kd.astype(BF16)
            within = _dot(a.astype(BF16), ib)
            egl_s[rs, :] = e_gl
            outs = []
            for l0 in range(0, GR, CH):
                ls = slice(l0, l0 + CH)
                st_ref[(g0 + l0) // CH] = st
                outs.append(within[ls] + _dot(qgb[ls], st.astype(BF16), NT))
                st = st * egl_s[g0 + l0:g0 + l0 + 1, :] + _dot(ib[ls], kdb[ls], TN)
            o_ref[rs, :] = jnp.concatenate(outs, axis=0)

    def slot(n):
        return pl.BlockSpec((None, None, S, HD), lambda b, h: (n, b, 0, h))

    return pl.pallas_call(
        body, name="hg_fwd", grid=(Bl, HEADS),
        in_specs=[slot(0), slot(1), slot(2), pl.BlockSpec((2, HD), lambda b, h: (0, h))],
        out_specs=[pl.BlockSpec((None, S, HD), lambda b, h: (b, 0, h)),
                   pl.BlockSpec((None, None, nc, HD, HD), lambda b, h: (b, h, 0, 0, 0))],
        out_shape=[jax.ShapeDtypeStruct((Bl, S, D), F32),
                   jax.ShapeDtypeStruct((Bl, HEADS, nc, HD, HD), F32)],
        scratch_shapes=[pltpu.VMEM((S, HD), F32)],
        compiler_params=_cparams(("parallel", "parallel")),
    )(proj, proj, proj, lb_logits)


def _hg_bwd(proj, states, do, dproj, lb_logits):
    _, Bl, S, _ = proj.shape

    def body(q_ref, f_ref, i_ref, st_ref, do_ref, lg_ref, _, dqfi_ref, dlb_ref, egl_s):
        lbv, _ = _lower_bound(lg_ref[...])
        tril, t_inc, t_dec, t_same = _group_consts()
        dst = jnp.zeros((HD, HD), F32)
        dlb = jnp.zeros((1, HD), F32)
        for g0 in reversed(range(0, S, GR)):
            rs = slice(g0, g0 + GR)
            qr, fp = q_ref[rs, :].astype(F32), f_ref[rs, :].astype(F32)
            qa, sq, sp, sn, f, k = _hg_gates(qr, fp, lbv)
            qt, kt, qg, kd, e_q, e_k, e_g, e_l, e_gl = _hg_decays(qa, k, f, t_inc, t_same)
            ib, dob, qgb, kdb = i_ref[rs, :], do_ref[rs, :], qg.astype(BF16), kd.astype(BF16)
            egl_s[rs, :] = e_gl
            ab = jnp.where(tril, _dot(qt.astype(BF16), kt.astype(BF16), NT), 0.0).astype(BF16)
            da = jnp.where(tril, _dot(dob, ib, NT), 0.0)
            dqt = _dot3(da, kt)
            dkt = _dot3(da, qt, TN)
            di_within = _dot(ab, dob, TN)
            dqg, dkd, di, dse = [], [], [], []
            for l0 in reversed(range(0, GR, CH)):
                ls = slice(l0, l0 + CH)
                st = st_ref[(g0 + l0) // CH]
                dstb = dst.astype(BF16)
                dqg.insert(0, _dot(dob[ls], st.astype(BF16)))
                dkd.insert(0, _dot(ib[ls], dstb))
                di.insert(0, _dot(kdb[ls], dstb, NT))
                dse.insert(0, jnp.broadcast_to(jnp.sum(dst * st, axis=0, keepdims=True), (CH, HD)))
                dst = dst * egl_s[g0 + l0:g0 + l0 + 1, :] + _dot(dob[ls], qgb[ls], TN)
            dqg, dkd, di, dse = (jnp.concatenate(p, axis=0) for p in (dqg, dkd, di, dse))
            dqfi_ref[2, rs, :] = (di + di_within).astype(BF16)
            dgc = dqt * qt - dkt * kt + dqg * qg - dkd * kd
            dg = _cum2l(t_dec, dgc) + _cum2l(t_same, dkd * kd) + dse * e_gl
            t1 = dg / f - (dkt * e_k + dkd * e_l)
            dqfi_ref[1, rs, :] = ((1.0 - lbv) * t1 * sp * sn).astype(BF16)
            dqfi_ref[0, rs, :] = ((dqt * e_q + dqg * e_g) * (sq * (1.0 + qr * (1.0 - sq)))).astype(BF16)
            dlb = dlb + jnp.sum(sn * t1, axis=0, keepdims=True)

        @pl.when(pl.program_id(1) == 0)
        def _():
            dlb_ref[...] = dlb

        @pl.when(pl.program_id(1) != 0)
        def _():
            dlb_ref[...] += dlb

    def slot(n):
        return pl.BlockSpec((None, None, S, HD), lambda h, b: (n, b, 0, h))

    return pl.pallas_call(
        body, name="hg_bwd", grid=(HEADS, Bl),
        in_specs=[slot(0), slot(1), slot(2),
                  pl.BlockSpec((None, None, S // CH, HD, HD), lambda h, b: (b, h, 0, 0, 0)),
                  pl.BlockSpec((None, S, HD), lambda h, b: (b, 0, h)),
                  pl.BlockSpec((2, HD), lambda h, b: (0, h)), ANY],
        out_specs=[pl.BlockSpec((3, None, S, HD), lambda h, b: (0, b, 0, h)),
                   pl.BlockSpec((1, HD), lambda h, b: (0, h))],
        out_shape=[jax.ShapeDtypeStruct(dproj.shape, dproj.dtype), jax.ShapeDtypeStruct((1, D), F32)],
        scratch_shapes=[pltpu.VMEM((S, HD), F32)],
        input_output_aliases={6: 0},
        compiler_params=_cparams(("parallel", "arbitrary")),
    )(proj, proj, proj, states, do, lb_logits, dproj)


def _place():
    x, y, c = lax.axis_index("x"), lax.axis_index("y"), lax.axis_index("c")
    return x, y, c, [(1 - x, y), (x, 1 - y), (1 - x, 1 - y)]


def _remote(src, dst, ssem, rsem, dev):
    return pltpu.make_async_remote_copy(src_ref=src, dst_ref=dst, send_sem=ssem, recv_sem=rsem,
                                        device_id=dev, device_id_type=MESH)


GATHER_SEMS = 9


def _gather_ops(wp_ref, wall_ref, ssem, rsem, hn_ref=None, hnall_ref=None):
    half = wp_ref.shape[0] // 2

    def place():
        x, y, c, chips = _place()
        return x, y, c, chips, 2 * x + y, pl.ds(c * half, half), pl.ds((1 - c) * half, half)

    def first_sends():
        x, y, c, chips, b, mine, _ = place()
        cps = [_remote(wp_ref.at[mine], wall_ref.at[b, mine], ssem.at[j], rsem.at[j], (*chip, c))
               for j, chip in enumerate(chips)]
        if hn_ref is not None:
            cps += [_remote(hn_ref, hnall_ref.at[b], ssem.at[6 + j], rsem.at[6 + j], (*chip, c))
                    for j, chip in enumerate(chips)]
        return cps

    def forwards():
        x, y, c, chips, _, mine, _ = place()
        return [_remote(wall_ref.at[2 * cx + cy, mine], wall_ref.at[2 * cx + cy, mine],
                        ssem.at[3 + j], rsem.at[3 + j], (x, y, 1 - c)) for j, (cx, cy) in enumerate(chips)]

    def start():
        for cp in first_sends():
            cp.start()

    def forward():
        x, y, c, chips, _, mine, _ = place()
        for j, (cx, cy) in enumerate(chips):
            landed = wall_ref.at[2 * cx + cy, mine]
            _remote(landed, landed, ssem.at[j], rsem.at[j], (cx, cy, c)).wait_recv()
        for cp in forwards():
            cp.start()

    def finish():
        x, y, c, chips, _, _, other = place()
        for j, (cx, cy) in enumerate(chips):
            passed = wall_ref.at[2 * cx + cy, other]
            _remote(passed, passed, ssem.at[3 + j], rsem.at[3 + j], (x, y, 1 - c)).wait_recv()
            if hn_ref is not None:
                row = hnall_ref.at[2 * cx + cy]
                _remote(row, row, ssem.at[6 + j], rsem.at[6 + j], (cx, cy, c)).wait_recv()
        for cp in first_sends() + forwards():
            cp.wait_send()

    return start, forward, finish


def _gather_weights(wp):
    def body(wp_ref, wall_ref, ssem, rsem):
        for step in _gather_ops(wp_ref, wall_ref, ssem, rsem):
            step()

    return pl.pallas_call(
        body, name="gather_weights", in_specs=[ANY], out_specs=ANY,
        out_shape=jax.ShapeDtypeStruct((NPROJ,) + wp.shape, BF16),
        scratch_shapes=[pltpu.SemaphoreType.DMA((GATHER_SEMS,)), pltpu.SemaphoreType.DMA((GATHER_SEMS,))],
    )(wp)


def _pair_ops(g_refs, r_refs, ssem, rsem):
    def copies():
        x, y, c, _ = _place()
        return [_remote(g.at[n, 1 - c], r.at[n], ssem.at[t * NPROJ + n], rsem.at[t * NPROJ + n], (x, y, 1 - c))
                for t, (g, r) in enumerate(zip(g_refs, r_refs)) for n in range(NPROJ)]

    def start():
        for cp in copies():
            cp.start()

    def finish():
        x, y, c, _ = _place()
        for t, r in enumerate(r_refs):
            for n in range(NPROJ):
                k = t * NPROJ + n
                _remote(r.at[n], r.at[n], ssem.at[k], rsem.at[k], (x, y, 1 - c)).wait_recv()
        for cp in copies():
            cp.wait_send()

    return start, finish


def _pair_shapes(grads):
    return ([jax.ShapeDtypeStruct((NPROJ,) + g.shape[2:], F32) for g in grads],
            [pltpu.SemaphoreType.DMA((len(grads) * NPROJ,))] * 2)


def _pair_exchange(grads):
    ng = len(grads)

    def body(*refs):
        start, finish = _pair_ops(refs[:ng], refs[ng:2 * ng], *refs[2 * ng:])
        start()
        finish()

    out_shape, sems = _pair_shapes(grads)
    return pl.pallas_call(
        body, name="pair_exchange", in_specs=[ANY] * ng, out_specs=[ANY] * ng,
        out_shape=out_shape, scratch_shapes=sems,
    )(*grads)


def _chip_ops(s_refs, r_refs, ssem, rsem):
    def copies():
        x, y, c, chips = _place()
        return [_remote(s.at[2 * cx + cy], r.at[2 * x + y], ssem.at[3 * t + j], rsem.at[3 * t + j], (cx, cy, c))
                for t, (s, r) in enumerate(zip(s_refs, r_refs)) for j, (cx, cy) in enumerate(chips)]

    def start():
        for cp in copies():
            cp.start()

    def finish():
        x, y, c, chips = _place()
        for t, r in enumerate(r_refs):
            for j, (cx, cy) in enumerate(chips):
                slot = r.at[2 * cx + cy]
                _remote(slot, slot, ssem.at[3 * t + j], rsem.at[3 * t + j], (cx, cy, c)).wait_recv()
        for cp in copies():
            cp.wait_send()

    return start, finish


def _chip_shapes(sums):
    return ([jax.ShapeDtypeStruct(s.shape, s.dtype) for s in sums],
            [pltpu.SemaphoreType.DMA((3 * len(sums),))] * 2)


def _sibling_share(halves, pack):
    ng = len(halves)

    def body(*refs):
        h_refs, pack_ref = refs[:ng], refs[ng]
        f_refs, allp_ref = refs[ng + 1:2 * ng + 1], refs[2 * ng + 1]
        ssem, rsem, psend, precv, lsem = refs[2 * ng + 2:]
        x, y, c, _ = _place()
        me = 4 * x + 2 * y + c
        local = pltpu.make_async_copy(pack_ref, allp_ref.at[me], lsem)
        local.start()
        flips = [(fx, fy, fc) for fx in (0, 1) for fy in (0, 1) for fc in (0, 1)][1:]
        peers = [(fx + x - 2 * fx * x, fy + y - 2 * fy * y, fc + c - 2 * fc * c) for fx, fy, fc in flips]
        sends = [_remote(pack_ref, allp_ref.at[me], psend.at[m], precv.at[m], peer) for m, peer in enumerate(peers)]
        sends += [_remote(h, f, ssem.at[t], rsem.at[t], (x, y, 1 - c))
                  for t, (h, f) in enumerate(zip(h_refs, f_refs))]
        for cp in sends:
            cp.start()
        for t, f in enumerate(f_refs):
            _remote(f, f, ssem.at[t], rsem.at[t], (x, y, 1 - c)).wait_recv()
        for m, (px, py, pc) in enumerate(peers):
            row = allp_ref.at[4 * px + 2 * py + pc]
            _remote(row, row, psend.at[m], precv.at[m], (px, py, pc)).wait_recv()
        for cp in sends:
            cp.wait_send()
        local.wait()

    return pl.pallas_call(
        body, name="sibling_share", in_specs=[ANY] * (ng + 1), out_specs=[ANY] * (ng + 1),
        out_shape=[jax.ShapeDtypeStruct(h.shape, F32) for h in halves]
        + [jax.ShapeDtypeStruct((8,) + pack.shape, F32)],
        scratch_shapes=[pltpu.SemaphoreType.DMA((ng,)), pltpu.SemaphoreType.DMA((ng,)),
                        pltpu.SemaphoreType.DMA((7,)), pltpu.SemaphoreType.DMA((7,)), pltpu.SemaphoreType.DMA],
    )(*halves, pack)


def _pair_add(own, recv, cidx, name):
    R = own.shape[2]
    tr = min(256, R)

    def body(c_ref, a_ref, b_ref, o_ref):
        o_ref[...] = (a_ref[...] + b_ref[...]).astype(BF16)

    return pl.pallas_call(
        body, name=name,
        grid_spec=pltpu.PrefetchScalarGridSpec(
            num_scalar_prefetch=1, grid=(NPROJ, R // tr),
            in_specs=[pl.BlockSpec((None, None, tr, D), lambda n, r, c: (n, c[0], r, 0)),
                      pl.BlockSpec((None, tr, D), lambda n, r, c: (n, r, 0))],
            out_specs=pl.BlockSpec((None, tr, D), lambda n, r, c: (n, r, 0))),
        out_shape=jax.ShapeDtypeStruct(recv.shape, BF16),
        compiler_params=_cparams(("parallel", "parallel")),
    )(cidx, own, recv)


def _chip_sum(sums, parts, bidx, name):
    R = parts.shape[1]
    tr = min(256, R)

    def body(b_ref, s_ref, p_ref, o_ref):
        acc = None
        for j in range(NPROJ):
            term = jnp.where(b_ref[0] == j, s_ref[...], p_ref[j]).astype(F32)
            acc = term if acc is None else acc + term
        o_ref[...] = acc

    return pl.pallas_call(
        body, name=name,
        grid_spec=pltpu.PrefetchScalarGridSpec(
            num_scalar_prefetch=1, grid=(R // tr,),
            in_specs=[pl.BlockSpec((None, tr, D), lambda r, b: (b[0], r, 0)),
                      pl.BlockSpec((NPROJ, tr, D), lambda r, b: (0, r, 0))],
            out_specs=pl.BlockSpec((tr, D), lambda r, b: (r, 0))),
        out_shape=jax.ShapeDtypeStruct((R, D), F32),
        compiler_params=_cparams(("parallel",)),
    )(bidx, sums, parts)


def _adamw_math(w, g, m, v):
    m = ADAM_B1 * m + (1.0 - ADAM_B1) * g
    v = ADAM_B2 * v + (1.0 - ADAM_B2) * (g * g)
    m_hat = m / (1.0 - ADAM_B1 ** ADAM_STEP)
    v_hat = v / (1.0 - ADAM_B2 ** ADAM_STEP)
    delta = -ADAM_LR * (m_hat / (jnp.sqrt(v_hat) + ADAM_EPS) + ADAM_WD * w)
    return delta, m, v


def _adamw(w, mine, theirs, m, v, cidx, name):
    R = mine.shape[0]
    tr = min(256, R)
    nr = R // tr

    def body(c_ref, w_ref, a_ref, b_ref, m_ref, v_ref, g_ref, d_ref, nm_ref, nv_ref):
        g = jnp.where(pl.program_id(0) == c_ref[0], a_ref[...], b_ref[...])
        g_ref[...] = g
        d_ref[...], nm_ref[...], nv_ref[...] = _adamw_math(w_ref[...], g, m_ref[...], v_ref[...])

    full = pl.BlockSpec((tr, D), lambda h, r, c: (h * nr + r, 0))
    half = pl.BlockSpec((tr, D), lambda h, r, c: (r, 0))
    return pl.pallas_call(
        body, name=name,
        grid_spec=pltpu.PrefetchScalarGridSpec(
            num_scalar_prefetch=1, grid=(2, nr),
            in_specs=[full, half, half, full, full], out_specs=[full] * 4),
        out_shape=[jax.ShapeDtypeStruct(w.shape, F32)] * 4,
        compiler_params=_cparams(("parallel", "parallel")),
    )(cidx, w, mine, theirs, m, v)


PACK_ROWS = 8


def _small_update(allp, bidx, logits, weights, moments_m, moments_v):
    shapes = [w.shape for w in weights]
    q4 = D // NPROJ

    def body(b_ref, allp_ref, hgp_ref, lg_ref, *refs):
        w_refs, m_refs, v_refs = refs[0:6], refs[6:12], refs[12:18]
        loss_ref = refs[18]
        g_out, d_out, m_out, v_out = refs[19:25], refs[25:31], refs[31:37], refs[37:43]

        def total(ref, row, lo, hi):
            acc = ref[0, row:row + 1, lo:hi]
            for dev in range(1, 8):
                acc = acc + ref[dev, row:row + 1, lo:hi]
            return acc

        _, pp = _lower_bound(lg_ref[...])
        dlb = total(allp_ref, 2, 0, D)
        grads = [total(allp_ref, 0, 0, D), total(allp_ref, 4, 0, HD), total(allp_ref, 4, HD, 2 * HD),
                 total(hgp_ref, 1, 0, q4), total(allp_ref, 4, 2 * HD, 3 * HD), None]
        loss_ref[...] = (0.5 / D) * jnp.sum(total(allp_ref, 3, 0, D), axis=1, keepdims=True)
        for t in range(6):
            if t < 5:
                rows = [(slice(None), grads[t])]
            else:
                rows = [(slice(0, 1), -pp * dlb), (slice(1, 2), pp * dlb)]
            for rs, g in rows:
                g_out[t][rs, :] = g
                d_out[t][rs, :], m_out[t][rs, :], v_out[t][rs, :] = _adamw_math(
                    w_refs[t][rs, :], g, m_refs[t][rs, :], v_refs[t][rs, :])

    whole = [pl.BlockSpec(s, lambda i, b: (0, 0)) for s in shapes]
    return pl.pallas_call(
        body, name="small_update",
        grid_spec=pltpu.PrefetchScalarGridSpec(
            num_scalar_prefetch=1, grid=(1,),
            in_specs=[pl.BlockSpec((8, PACK_ROWS, D), lambda i, b: (0, 0, 0)),
                      pl.BlockSpec((8, PACK_ROWS, q4), lambda i, b: (0, 0, b[0])),
                      pl.BlockSpec((2, D), lambda i, b: (0, 0))] + whole * 3,
            out_specs=[pl.BlockSpec((1, 1), lambda i, b: (0, 0))] + whole * 4),
        out_shape=[jax.ShapeDtypeStruct((1, 1), F32)] + [jax.ShapeDtypeStruct(s, F32) for s in shapes] * 4,
        compiler_params=_cparams(("arbitrary",)),
    )(bidx, allp, allp, logits, *weights, *moments_m, *moments_v)


def kernel(x, sb_norm, sb_w_in, sb_q_gain, sb_k_gain, sb_w_out, hg_norm, hg_w_in, hg_o_gain, hg_w_out, hg_lb_logits, loss_target, m_sb_norm, m_sb_w_in, m_sb_q_gain, m_sb_k_gain, m_sb_w_out, m_hg_norm, m_hg_w_in, m_hg_o_gain, m_hg_w_out, m_hg_lb_logits, v_sb_norm, v_sb_w_in, v_sb_q_gain, v_sb_k_gain, v_sb_w_out, v_hg_norm, v_hg_w_in, v_hg_o_gain, v_hg_w_out, v_hg_lb_logits):
    Bl, S, _ = x.shape
    T = Bl * S
    cidx = lax.axis_index("c").astype(jnp.int32).reshape(1)
    bidx = (2 * lax.axis_index("x") + lax.axis_index("y")).astype(jnp.int32).reshape(1)

    def in_hbm(arrays):
        return [pltpu.with_memory_space_constraint(a, pltpu.HBM) for a in arrays]

    def own_slot(gathered, mine):
        return lax.dynamic_update_slice(gathered, mine[None], (bidx[0],) + (0,) * mine.ndim)

    def halved(g):
        return g.reshape(NPROJ, 2, g.shape[-2] * g.shape[0] // (2 * NPROJ), D)

    def heads(a):
        return a.reshape(a.shape[:-2] + (Bl, S, D))

    def flat(a):
        return a.reshape(a.shape[:-3] + (T, D))

    wp_first = sb_w_in[0].astype(BF16)
    wp_rest = jnp.concatenate([hg_w_in[0], sb_w_out[0], hg_w_out[0]], axis=0).astype(BF16)
    wall_first, = in_hbm([own_slot(_gather_weights(wp_first), wp_first)])
    x2 = x.reshape(T, D)
    tgt = loss_target.reshape(T, D)

    proj0 = _in_proj_fwd(x2, sb_norm, wall_first, W_IN, "sb_in_fwd")
    o0, ctot, wall_rest, hnall = _sb_fwd(heads(proj0), sb_q_gain, sb_k_gain, wp_rest, hg_norm)
    wall_rest, = in_hbm([own_slot(wall_rest, wp_rest)])
    hgn = own_slot(hnall, hg_norm).reshape(1, D)
    h1 = _out_proj_fwd(flat(o0), proj0, x2, wall_rest, W_OUT_SB, "sb_out_fwd")
    proj1 = _in_proj_fwd(h1, hgn, wall_rest, W_IN, "hg_in_fwd")
    o1, states = _hg_fwd(heads(proj1), hg_lb_logits)
    dh2, loss_terms = _out_proj_fwd(flat(o1), proj1, h1, wall_rest, W_OUT_HG, "hg_out_fwd",
                                    o_gain=hg_o_gain, target=tgt)

    do1, dproj1, gout_hg, d_ogain = _out_proj_bwd(dh2, flat(o1), proj1, wall_rest, W_OUT_HG, "hg_out_bwd",
                                                  o_gain=hg_o_gain)
    dproj1, dlb = _hg_bwd(heads(proj1), states, heads(do1), heads(dproj1), hg_lb_logits)
    dproj1 = flat(dproj1)
    dh1, d_hgn = _in_proj_bwd_x(dproj1, wall_rest, W_IN, h1, hgn, dh2, "hg_in_bwd_x")
    big_hg = in_hbm([halved(_in_proj_bwd_w(dproj1, h1, hgn, "hg_in_bwd_w")), halved(gout_hg)])
    do0, dproj0, gout_sb, *recv_hg = _out_proj_bwd(dh1, flat(o0), proj0, wall_rest, W_OUT_SB, "sb_out_bwd",
                                                   exchange=big_hg)
    sums_hg = in_hbm(_pair_add(g, r, cidx, "pair_add_" + nm)
                     for g, r, nm in zip(big_hg, in_hbm(recv_hg), ("hg_in", "hg_out")))
    dproj0, d_qg, d_kg, *parts_hg = _sb_bwd(heads(proj0), ctot, heads(do0), heads(dproj0),
                                            sb_q_gain, sb_k_gain, sums_hg)
    dproj0 = flat(dproj0)

    big_sb = in_hbm([halved(_in_proj_bwd_w(dproj0, x2, sb_norm, "sb_in_bwd_w")), halved(gout_sb)])
    sums_sb = in_hbm(_pair_add(g, r, cidx, "pair_add_" + nm)
                     for g, r, nm in zip(big_sb, in_hbm(_pair_exchange(big_sb)), ("sb_in", "sb_out")))
    grad_x, d_sbn, *parts_sb = _in_proj_bwd_x(dproj0, wall_first, W_IN, x2, sb_norm, dh1, "sb_in_bwd_x",
                                              exchange=sums_sb)

    names = ["sb_in", "hg_in", "sb_out", "hg_out"]
    sums = [sums_sb[0], sums_hg[0], sums_sb[1], sums_hg[1]]
    parts = [parts_sb[0], parts_hg[0], parts_sb[1], parts_hg[1]]
    halves = in_hbm(_chip_sum(sm, p, bidx, "chip_sum_" + nm) for sm, p, nm in zip(sums, in_hbm(parts), names))
    gains = jnp.concatenate([d_qg, d_kg, d_ogain, jnp.zeros((1, D - 3 * HD), F32)], axis=1)
    pack = jnp.concatenate([d_sbn, d_hgn, dlb, loss_terms, gains, jnp.zeros((3, D), F32)], axis=0)
    *theirs, allp = _sibling_share(halves, pack)
    theirs = in_hbm(theirs)

    big_w = [sb_w_in, hg_w_in, sb_w_out, hg_w_out]
    big_m = [m_sb_w_in, m_hg_w_in, m_sb_w_out, m_hg_w_out]
    big_v = [v_sb_w_in, v_hg_w_in, v_sb_w_out, v_hg_w_out]
    upd = [_adamw(*in_hbm([w[0], a, b, m[0], v[0]]), cidx, "adamw_" + nm)
           for w, a, b, m, v, nm in zip(big_w, halves, theirs, big_m, big_v, names)]
    (g_sb_in, d_sb_in, nm_sb_in, nv_sb_in), (g_hg_in, d_hg_in, nm_hg_in, nv_hg_in), \
        (g_sb_out, d_sb_out, nm_sb_out, nv_sb_out), (g_hg_out, d_hg_out, nm_hg_out, nv_hg_out) = [
            tuple(a[None] for a in u) for u in upd]

    small = _small_update(
        allp, bidx, hg_lb_logits,
        [sb_norm, sb_q_gain, sb_k_gain, hg_norm, hg_o_gain, hg_lb_logits],
        [m_sb_norm, m_sb_q_gain, m_sb_k_gain, m_hg_norm, m_hg_o_gain, m_hg_lb_logits],
        [v_sb_norm, v_sb_q_gain, v_sb_k_gain, v_hg_norm, v_hg_o_gain, v_hg_lb_logits])
    loss = small[0].reshape(())
    (g_sbn, g_qg, g_kg, g_hgn, g_og, g_lb) = small[1:7]
    (d_sbn2, d_qg2, d_kg2, d_hgn2, d_og2, d_lb2) = small[7:13]
    (nm_sbn, nm_qg, nm_kg, nm_hgn, nm_og, nm_lb) = small[13:19]
    (nv_sbn, nv_qg, nv_kg, nv_hgn, nv_og, nv_lb) = small[19:25]

    return (loss, grad_x.reshape(Bl, S, D),
            g_sbn, g_sb_in, g_qg, g_kg, g_sb_out, g_hgn, g_hg_in, g_og, g_hg_out, g_lb,
            d_sbn2, d_sb_in, d_qg2, d_kg2, d_sb_out, d_hgn2, d_hg_in, d_og2, d_hg_out, d_lb2,
            nm_sbn, nm_sb_in, nm_qg, nm_kg, nm_sb_out, nm_hgn, nm_hg_in, nm_og, nm_hg_out, nm_lb,
            nv_sbn, nv_sb_in, nv_qg, nv_kg, nv_sb_out, nv_hgn, nv_hg_in, nv_og, nv_hg_out, nv_lb)
```

```python
import functools

import jax
import jax.numpy as jnp
from jax import lax
from jax.experimental import pallas as pl
from jax.experimental.pallas import tpu as pltpu

F32 = jnp.float32
BF16 = jnp.bfloat16
MESH = pl.DeviceIdType.MESH
ANY = pl.BlockSpec(memory_space=pl.ANY)

D = 1024
HEADS = 8
HD = 128
NPROJ = 4
RMS_EPS = 1e-6
TK = 256
CH = 64
CH_LOG2 = 6
GR = 128
SCALE = HD ** -0.5
EXP_CLAMP = 60.0
W_IN, W_OUT_SB, W_OUT_HG = 0, 4, 5

ADAM_LR = 0.001
ADAM_B1 = 0.9
ADAM_B2 = 0.999
ADAM_EPS = 1e-08
ADAM_WD = 0.01
ADAM_STEP = 10

NT = (((1,), (1,)), ((), ()))
TN = (((0,), (0,)), ((), ()))
MIB = 1024 * 1024


def _cparams(sem=None, vmem_mib=40):
    return pltpu.CompilerParams(dimension_semantics=sem, vmem_limit_bytes=vmem_mib * MIB)


def _dot(a, b, dims=None):
    if dims is None:
        return jnp.dot(a, b, preferred_element_type=F32)
    return lax.dot_general(a, b, dims, preferred_element_type=F32)


def _sigmoid(x):
    return 1.0 / (1.0 + jnp.exp(-x))


def _rms(x):
    return lax.rsqrt(jnp.mean(x * x, axis=-1, keepdims=True) + RMS_EPS)


def _rms_bwd(x, r, gain, dy):
    a = dy * gain
    dx = r * a - x * (r * r * r) * jnp.mean(x * a, axis=-1, keepdims=True)
    return dx, dy * (x * r)


def _split2(v):
    hi = v.astype(BF16)
    lo = (v - hi.astype(F32)).astype(BF16)
    return hi, lo


def _cum2(v, u):
    hi, lo = _split2(v)
    return _dot(hi, u) + _dot(lo, u)


def _dot3(a, b, dims=None):
    ah, al = _split2(a)
    bh, bl = _split2(b)
    return _dot(ah, bh, dims) + _dot(ah, bl, dims) + _dot(al, bh, dims)


def _cum2l(u, v):
    hi, lo = _split2(v)
    return _dot(u, hi) + _dot(u, lo)


def _in_proj_fwd(h, gain, wall, wblk, name):
    T = h.shape[0]
    tm = min(1024, T)

    def body(h_ref, g_ref, w_ref, o_ref, u_s):
        rows = pl.ds(pl.multiple_of(pl.program_id(1) * tm, tm), tm)

        @pl.when(pl.program_id(0) == 0)
        def _():
            x = h_ref[...]
            u_s[rows, :] = (x * _rms(x) * g_ref[...]).astype(BF16)

        o_ref[...] = _dot(u_s[rows, :], w_ref[...]).astype(BF16)

    return pl.pallas_call(
        body, name=name, grid=(NPROJ, T // tm),
        in_specs=[pl.BlockSpec((tm, D), lambda n, i: (jnp.where(n == 0, i, 0), 0)),
                  pl.BlockSpec((1, D), lambda n, i: (0, 0)),
                  pl.BlockSpec((None, D, D), lambda n, i: (n, wblk, 0))],
        out_specs=pl.BlockSpec((None, tm, D), lambda n, i: (n, i, 0)),
        out_shape=jax.ShapeDtypeStruct((NPROJ, T, D), BF16),
        scratch_shapes=[pltpu.VMEM((T, D), BF16)],
        compiler_params=_cparams(("arbitrary", "arbitrary")),
    )(h, gain, wall)


def _head_norm(x):
    outs = []
    for hh in range(x.shape[1] // HD):
        xs = x[:, hh * HD:(hh + 1) * HD]
        outs.append((xs, _rms(xs)))
    return outs


def _w_out_specs(wblk):
    kb = D // NPROJ
    return [pl.BlockSpec((None, kb, D), functools.partial(lambda j, i: (j, wblk, 0), j)) for j in range(NPROJ)]


def _out_proj_fwd(o, proj, resid, wall, wblk, name, o_gain=None, target=None):
    T = o.shape[0]
    tm = min(512, T)
    kb = D // NPROJ
    with_loss = target is not None

    def body(*refs):
        o_ref, g_ref, r_ref = refs[:3]
        w_refs = refs[3:3 + NPROJ]
        if with_loss:
            og_ref, t_ref, dh_ref, ls_ref = refs[3 + NPROJ:]
        else:
            h_ref, = refs[3 + NPROJ:]
        x = o_ref[...]
        if with_loss:
            x = jnp.concatenate([xs * r * og_ref[...] for xs, r in _head_norm(x)], axis=1)
        g = g_ref[...].astype(F32)
        a = (x * (g * _sigmoid(g))).astype(BF16)
        hnew = r_ref[...]
        for j in range(NPROJ):
            hnew = hnew + _dot(a[:, j * kb:(j + 1) * kb], w_refs[j][...])
        if with_loss:
            err = hnew - t_ref[...]
            dh_ref[...] = err * (1.0 / D)
            part = jnp.sum(err * err, axis=0, keepdims=True)

            @pl.when(pl.program_id(0) == 0)
            def _():
                ls_ref[...] = part

            @pl.when(pl.program_id(0) != 0)
            def _():
                ls_ref[...] += part
        else:
            h_ref[...] = hnew

    tile = pl.BlockSpec((tm, D), lambda i: (i, 0))
    in_specs = [tile, pl.BlockSpec((None, tm, D), lambda i: (3, i, 0)), tile] + _w_out_specs(wblk)
    args = [o, proj, resid] + [wall] * NPROJ
    out_specs = tile
    out_shape = jax.ShapeDtypeStruct((T, D), F32)
    if with_loss:
        in_specs += [pl.BlockSpec((1, HD), lambda i: (0, 0)), tile]
        args += [o_gain, target]
        out_specs = [tile, pl.BlockSpec((1, D), lambda i: (0, 0))]
        out_shape = [out_shape, jax.ShapeDtypeStruct((1, D), F32)]
    return pl.pallas_call(
        body, name=name, grid=(T // tm,), in_specs=in_specs, out_specs=out_specs,
        out_shape=out_shape, compiler_params=_cparams(("arbitrary",)),
    )(*args)


def _out_proj_bwd(dy, o, proj, wall, wblk, name, o_gain=None, exchange=()):
    T = o.shape[0]
    tm = min(512, T)
    kb = D // NPROJ
    normed = o_gain is not None
    ne = len(exchange)

    def body(*refs):
        it = iter(refs)
        dy_ref, o_ref, g_ref = (next(it) for _ in range(3))
        w_refs = [next(it) for _ in range(NPROJ)]
        og_ref = next(it) if normed else None
        xg_refs = [next(it) for _ in range(ne)]
        do_ref, dg_ref, dw_ref = (next(it) for _ in range(3))
        dgain_ref = next(it) if normed else None
        xr_refs = [next(it) for _ in range(ne)]
        wt_s = next(it)
        first = pl.program_id(0) == 0
        if ne:
            start, finish = _pair_ops(xg_refs, xr_refs, next(it), next(it))
            pl.when(first)(start)

        @pl.when(first)
        def _():
            for j in range(NPROJ):
                wt_s[:, j * kb:(j + 1) * kb] = w_refs[j][...].T
        g = g_ref[...].astype(F32)
        s = _sigmoid(g)
        sl = g * s
        x = o_ref[...]
        if normed:
            heads = _head_norm(x)
            on = jnp.concatenate([xs * r * og_ref[...] for xs, r in heads], axis=1)
        else:
            on = x
        dyb = dy_ref[...].astype(BF16)
        a = (on * sl).astype(BF16)
        at = a.T
        for j in range(NPROJ):
            part = _dot(at[j * kb:(j + 1) * kb, :], dyb)

            @pl.when(first)
            def _():
                dw_ref[j] = part

            @pl.when(jnp.logical_not(first))
            def _():
                dw_ref[j] += part

        da = _dot(dyb, wt_s[...])
        d_on = da * sl
        dg_ref[...] = (da * on * (s * (1.0 + g * (1.0 - s)))).astype(BF16)
        if normed:
            dxs, gsum = [], None
            for hh, (xs, r) in enumerate(heads):
                dx, gt = _rms_bwd(xs, r, og_ref[...], d_on[:, hh * HD:(hh + 1) * HD])
                dxs.append(dx)
                gt = jnp.sum(gt, axis=0, keepdims=True)
                gsum = gt if gsum is None else gsum + gt
            do_ref[...] = jnp.concatenate(dxs, axis=1).astype(BF16)

            @pl.when(first)
            def _():
                dgain_ref[...] = gsum

            @pl.when(jnp.logical_not(first))
            def _():
                dgain_ref[...] += gsum
        else:
            do_ref[...] = d_on.astype(BF16)
        if ne:
            pl.when(pl.program_id(0) == T // tm - 1)(finish)

    tile = pl.BlockSpec((tm, D), lambda i: (i, 0))
    gate = pl.BlockSpec((None, tm, D), lambda i: (3, i, 0))
    in_specs = [tile, tile, gate] + _w_out_specs(wblk)
    args = [dy, o, proj] + [wall] * NPROJ
    out_specs = [tile, gate, pl.BlockSpec((NPROJ, kb, D), lambda i: (0, 0, 0))]
    out_shape = [jax.ShapeDtypeStruct((T, D), BF16),
                 jax.ShapeDtypeStruct((NPROJ, T, D), BF16),
                 jax.ShapeDtypeStruct((NPROJ, kb, D), F32)]
    if normed:
        in_specs.append(pl.BlockSpec((1, HD), lambda i: (0, 0)))
        args.append(o_gain)
        out_specs.append(pl.BlockSpec((1, HD), lambda i: (0, 0)))
        out_shape.append(jax.ShapeDtypeStruct((1, HD), F32))
    x_shape, x_sems = _pair_shapes(exchange) if ne else ([], [])
    return pl.pallas_call(
        body, name=name, grid=(T // tm,), in_specs=in_specs + [ANY] * ne, out_specs=out_specs + [ANY] * ne,
        out_shape=out_shape + x_shape, scratch_shapes=[pltpu.VMEM((D, D), BF16)] + x_sems,
        compiler_params=_cparams(("arbitrary",), vmem_mib=48),
    )(*args, *exchange)


def _in_proj_bwd_x(dproj, wall, wblk, h, gain, dres, name, exchange=()):
    T = h.shape[0]
    tm = min(1024, T)
    ne = len(exchange)

    def body(d_ref, w_ref, h_ref, g_ref, r_ref, *refs):
        xs_refs, (dh_ref, dgain_ref), xr_refs = refs[:ne], refs[ne:ne + 2], refs[ne + 2:2 * ne + 2]
        du, wt_s = refs[2 * ne + 2:2 * ne + 4]
        i, n = pl.program_id(0), pl.program_id(1)
        if ne:
            start, finish = _chip_ops(xs_refs, xr_refs, *refs[2 * ne + 4:])
            pl.when(jnp.logical_and(i == 0, n == 0))(start)

        @pl.when(i == 0)
        def _():
            wt_s[n] = w_ref[...].T

        part = _dot(d_ref[...], wt_s[n])

        @pl.when(n == 0)
        def _():
            du[...] = part

        @pl.when(n != 0)
        def _():
            du[...] += part

        @pl.when(n == NPROJ - 1)
        def _():
            x = h_ref[...]
            dx, gt = _rms_bwd(x, _rms(x), g_ref[...], du[...])
            dh_ref[...] = r_ref[...] + dx
            gt = jnp.sum(gt, axis=0, keepdims=True)

            @pl.when(i == 0)
            def _():
                dgain_ref[...] = gt

            @pl.when(i != 0)
            def _():
                dgain_ref[...] += gt

        if ne:
            pl.when(jnp.logical_and(i == T // tm - 1, n == NPROJ - 1))(finish)

    x_shape, x_sems = _chip_shapes(exchange) if ne else ([], [])
    return pl.pallas_call(
        body, name=name, grid=(T // tm, NPROJ),
        in_specs=[pl.BlockSpec((None, tm, D), lambda i, n: (n, i, 0)),
                  pl.BlockSpec((None, D, D), lambda i, n: (jnp.where(i == 0, n, NPROJ - 1), wblk, 0)),
                  pl.BlockSpec((tm, D), lambda i, n: (i, 0)),
                  pl.BlockSpec((1, D), lambda i, n: (0, 0)),
                  pl.BlockSpec((tm, D), lambda i, n: (i, 0))] + [ANY] * ne,
        out_specs=[pl.BlockSpec((tm, D), lambda i, n: (i, 0)),
                   pl.BlockSpec((1, D), lambda i, n: (0, 0))] + [ANY] * ne,
        out_shape=[jax.ShapeDtypeStruct((T, D), F32), jax.ShapeDtypeStruct((1, D), F32)] + x_shape,
        scratch_shapes=[pltpu.VMEM((tm, D), F32), pltpu.VMEM((NPROJ, D, D), BF16)] + x_sems,
        compiler_params=_cparams(("arbitrary", "arbitrary"), vmem_mib=56),
    )(dproj, wall, h, gain, dres, *exchange)


def _in_proj_bwd_w(dproj, h, gain, name):
    T = h.shape[0]
    tk = min(1024, T)

    def body(d_ref, h_ref, g_ref, dw_ref, ut_s):
        k = pl.program_id(1)

        @pl.when(pl.program_id(0) == 0)
        def _():
            x = h_ref[...]
            ut_s[k] = (x * _rms(x) * g_ref[...]).astype(BF16).T

        part = _dot(ut_s[k], d_ref[...])

        @pl.when(k == 0)
        def _():
            dw_ref[...] = part

        @pl.when(k != 0)
        def _():
            dw_ref[...] += part

    return pl.pallas_call(
        body, name=name, grid=(NPROJ, T // tk),
        in_specs=[pl.BlockSpec((None, tk, D), lambda n, k: (n, k, 0)),
                  pl.BlockSpec((tk, D), lambda n, k: (jnp.where(n == 0, k, 0), 0)),
                  pl.BlockSpec((1, D), lambda n, k: (0, 0))],
        out_specs=pl.BlockSpec((None, D, D), lambda n, k: (n, 0, 0)),
        out_shape=jax.ShapeDtypeStruct((NPROJ, D, D), F32),
        scratch_shapes=[pltpu.VMEM((T // tk, D, tk), BF16)],
        compiler_params=_cparams(("arbitrary", "arbitrary")),
    )(dproj, h, gain)


def _log_sigmoid_pair(z):
    lb = jnp.minimum(z, 0.0) - jnp.log(1.0 + jnp.exp(-jnp.abs(z)))
    return lb, lb - z


def _slab_consts():
    t = lax.broadcasted_iota(jnp.int32, (TK, TK), 0)
    s = lax.broadcasted_iota(jnp.int32, (TK, TK), 1)
    return s < t, (t > s).astype(BF16), (t < s).astype(BF16)


def _slab_rows(k0, S):
    return [(r0, r1, masked) for r0, r1, masked in ((k0, k0 + TK, True), (k0 + TK, S, False)) if r0 < r1]


def _sb_fwd(proj, q_gain, k_gain, wp, hn):
    _, Bl, S, _ = proj.shape
    steps = Bl * HEADS

    def body(q_ref, k_ref, v_ref, qg_ref, kg_ref, wp_ref, hn_ref, o_ref, ct_ref, wall_ref, hnall_ref,
             qn, kn, vb, ssem, rsem):
        step = pl.program_id(0) * HEADS + pl.program_id(1)
        start, forward, finish = _gather_ops(wp_ref, wall_ref, ssem, rsem, hn_ref, hnall_ref)
        pl.when(step == 0)(start)
        pl.when(step == steps // 2)(forward)
        q = q_ref[...].astype(F32)
        qn[...] = (q * _rms(q) * (qg_ref[...] * SCALE)).astype(BF16)
        k = k_ref[...].astype(F32)
        kn[...] = (k * _rms(k) * kg_ref[...]).astype(BF16)
        vb[...] = v_ref[...]
        tri, u_gt, _ = _slab_consts()
        nb = S // TK
        c_blk = [jnp.zeros((TK, 1), F32)] * nb
        o_blk = [jnp.zeros((TK, HD), F32)] * nb
        for k0 in reversed(range(0, S, TK)):
            kb, vbb = kn[k0:k0 + TK, :], vb[k0:k0 + TK, :]
            for r0, r1, masked in _slab_rows(k0, S):
                blocks = range(r0 // TK, r1 // TK)
                z = _dot(qn[r0:r1, :], kb, NT)
                lb, ls = _log_sigmoid_pair(z)
                if masked:
                    ls = jnp.where(tri, ls, 0.0)
                c = jnp.concatenate([c_blk[b] for b in blocks], axis=0)
                w = jnp.exp(lb + _cum2(ls, u_gt) + c)
                if masked:
                    w = jnp.where(tri, w, 0.0)
                o_new = _dot(w.astype(BF16), vbb)
                c_new = jnp.sum(ls, axis=1, keepdims=True)
                for i, b in enumerate(blocks):
                    o_blk[b] = o_blk[b] + o_new[i * TK:(i + 1) * TK]
                    c_blk[b] = c_blk[b] + c_new[i * TK:(i + 1) * TK]
        o_ref[...] = jnp.concatenate(o_blk, axis=0)
        ct_ref[...] = jnp.concatenate(c_blk, axis=0)
        pl.when(step == steps - 1)(finish)

    def slot(n):
        return pl.BlockSpec((None, None, S, HD), lambda b, h: (n, b, 0, h))

    return pl.pallas_call(
        body, name="sb_fwd", grid=(Bl, HEADS),
        in_specs=[slot(0), slot(1), slot(2),
                  pl.BlockSpec((1, HD), lambda b, h: (0, 0)),
                  pl.BlockSpec((1, HD), lambda b, h: (0, 0)), ANY, ANY],
        out_specs=[pl.BlockSpec((None, S, HD), lambda b, h: (b, 0, h)),
                   pl.BlockSpec((None, None, S, 1), lambda b, h: (b, h, 0, 0)), ANY, ANY],
        out_shape=[jax.ShapeDtypeStruct((Bl, S, D), F32),
                   jax.ShapeDtypeStruct((Bl, HEADS, S, 1), F32),
                   jax.ShapeDtypeStruct((NPROJ,) + wp.shape, BF16),
                   jax.ShapeDtypeStruct((NPROJ,) + hn.shape, F32)],
        scratch_shapes=[pltpu.VMEM((S, HD), BF16)] * 3 + [pltpu.SemaphoreType.DMA((GATHER_SEMS,))] * 2,
        compiler_params=_cparams(("arbitrary", "arbitrary"), vmem_mib=56),
    )(proj, proj, proj, q_gain, k_gain, wp, hn)


def _sb_bwd(proj, ctot, do, dproj, q_gain, k_gain, exchange):
    _, Bl, S, _ = proj.shape
    ne = len(exchange)

    def body(q_ref, k_ref, v_ref, ct_ref, do_ref, qg_ref, kg_ref, _, *refs):
        xs_refs, (dqkv_ref, dqg_ref, dkg_ref), xr_refs = refs[:ne], refs[ne:ne + 3], refs[ne + 3:2 * ne + 3]
        qn, kn, vb, qt_s, dot_s, dqn, dkt_s, dvt_s, ssem, rsem = refs[2 * ne + 3:]
        step = pl.program_id(0) * HEADS + pl.program_id(1)
        first = step == 0
        start, finish = _chip_ops(xs_refs, xr_refs, ssem, rsem)

        @pl.when(first)
        def _():
            start()
            dqg_ref[...] = jnp.zeros_like(dqg_ref)
            dkg_ref[...] = jnp.zeros_like(dkg_ref)

        q = q_ref[...].astype(F32)
        rq = _rms(q)
        qn[...] = (q * rq * (qg_ref[...] * SCALE)).astype(BF16)
        k = k_ref[...].astype(F32)
        rk = _rms(k)
        kn[...] = (k * rk * kg_ref[...]).astype(BF16)
        vb[...] = v_ref[...]
        qt_s[...] = qn[...].T
        dot_s[...] = do_ref[...].T
        for acc in (dqn, dkt_s, dvt_s):
            acc[...] = jnp.zeros_like(acc)
        tri, u_gt, u_lt = _slab_consts()
        nb = S // TK
        passed, e = [jnp.zeros((TK, 1), F32)] * nb, [jnp.zeros((TK, 1), F32)] * nb
        for k0 in range(0, S, TK):
            keys = slice(k0, k0 + TK)
            kb, vbb = kn[keys, :], vb[keys, :]
            for r0, r1, masked in _slab_rows(k0, S):
                rows, blocks = slice(r0, r1), range(r0 // TK, r1 // TK)
                qb, dobb = qn[rows, :], do_ref[rows, :]
                z = _dot(qb, kb, NT)
                lb, ls = _log_sigmoid_pair(z)
                if masked:
                    ls = jnp.where(tri, ls, 0.0)
                p_new = jnp.concatenate([passed[b] for b in blocks], axis=0) + jnp.sum(ls, axis=1, keepdims=True)
                w = jnp.exp(lb + _cum2(ls, u_gt) + (ct_ref[rows, :] - p_new))
                if masked:
                    w = jnp.where(tri, w, 0.0)
                de = w * _dot(dobb, vbb, NT)
                e_old = jnp.concatenate([e[b] for b in blocks], axis=0)
                dls = e_old + _cum2(de, u_lt)
                e_new = e_old + jnp.sum(de, axis=1, keepdims=True)
                for i, b in enumerate(blocks):
                    passed[b], e[b] = p_new[i * TK:(i + 1) * TK], e_new[i * TK:(i + 1) * TK]
                sg = jnp.exp(lb)
                dz = de - sg * (de + dls)
                if masked:
                    dz = jnp.where(tri, dz, 0.0)
                dzb = dz.astype(BF16)
                dqn[rows, :] += _dot(dzb, kb)
                dkt_s[:, keys] += _dot(qt_s[:, rows], dzb)
                dvt_s[:, keys] += _dot(dot_s[:, rows], w.astype(BF16))

        dx, gt = _rms_bwd(q, rq, qg_ref[...], dqn[...] * SCALE)
        dqkv_ref[0] = dx.astype(BF16)
        dqg_ref[...] += jnp.sum(gt, axis=0, keepdims=True)
        dx, gt = _rms_bwd(k, rk, kg_ref[...], dkt_s[...].T)
        dqkv_ref[1] = dx.astype(BF16)
        dkg_ref[...] += jnp.sum(gt, axis=0, keepdims=True)
        dqkv_ref[2] = dvt_s[...].T.astype(BF16)
        pl.when(step == Bl * HEADS - 1)(finish)

    def slot(n):
        return pl.BlockSpec((None, None, S, HD), lambda b, h: (n, b, 0, h))

    head = pl.BlockSpec((None, S, HD), lambda b, h: (b, 0, h))
    gain = pl.BlockSpec((1, HD), lambda b, h: (0, 0))
    x_shape, x_sems = _chip_shapes(exchange)
    return pl.pallas_call(
        body, name="sb_bwd", grid=(Bl, HEADS),
        in_specs=[slot(0), slot(1), slot(2),
                  pl.BlockSpec((None, None, S, 1), lambda b, h: (b, h, 0, 0)), head, gain, gain, ANY] + [ANY] * ne,
        out_specs=[pl.BlockSpec((3, None, S, HD), lambda b, h: (0, b, 0, h)), gain, gain] + [ANY] * ne,
        out_shape=[jax.ShapeDtypeStruct(dproj.shape, dproj.dtype),
                   jax.ShapeDtypeStruct((1, HD), F32), jax.ShapeDtypeStruct((1, HD), F32)] + x_shape,
        scratch_shapes=([pltpu.VMEM((S, HD), BF16)] * 3 + [pltpu.VMEM((HD, S), BF16)] * 2 + [pltpu.VMEM((S, HD), F32)]
                        + [pltpu.VMEM((HD, S), F32)] * 2 + x_sems),
        input_output_aliases={7: 0},
        compiler_params=_cparams(("arbitrary", "arbitrary"), vmem_mib=56),
    )(proj, proj, proj, ctot, do, q_gain, k_gain, dproj, *exchange)


def _lower_bound(logits):
    l0, l1 = logits[0:1, :], logits[1:2, :]
    m = jnp.maximum(l0, l1)
    e0, e1 = jnp.exp(l0 - m), jnp.exp(l1 - m)
    p0, p1 = e0 / (e0 + e1), e1 / (e0 + e1)
    return (p0 + p1) - p0, p0 * p1


def _hg_gates(qr, fp, lbv):
    sq = _sigmoid(qr)
    sp = _sigmoid(fp)
    sn = 1.0 / (1.0 + jnp.exp(fp))
    f = lbv + (1.0 - lbv) * sp
    return qr * sq, sq, sp, sn, f, (1.0 - lbv) * sn


def _group_consts():
    t = lax.broadcasted_iota(jnp.int32, (GR, GR), 0)
    j = lax.broadcasted_iota(jnp.int32, (GR, GR), 1)
    same = lax.shift_right_logical(t, CH_LOG2) == lax.shift_right_logical(j, CH_LOG2)
    tril = jnp.logical_and(same, j <= t)
    return (tril, tril.astype(BF16), jnp.logical_and(same, j >= t).astype(BF16), same.astype(BF16))


def _hg_decays(qa, k, f, t_inc, t_same):
    g = jnp.log(f)
    gc = _cum2l(t_inc, g)
    gl = _cum2l(t_same, g)
    gm = gc - 0.5 * gl
    e_q = jnp.exp(jnp.minimum(gm, EXP_CLAMP))
    e_k = jnp.exp(jnp.minimum(-gm, EXP_CLAMP))
    e_g = jnp.exp(gc)
    e_l = jnp.exp(gl - gc)
    return qa * e_q, k * e_k, qa * e_g, k * e_l, e_q, e_k, e_g, e_l, jnp.exp(gl)


def _hg_fwd(proj, lb_logits):
    _, Bl, S, _ = proj.shape
    nc = S // CH

    def body(q_ref, f_ref, i_ref, lg_ref, o_ref, st_ref, egl_s):
        lbv, _ = _lower_bound(lg_ref[...])
        tril, t_inc, _, t_same = _group_consts()
        st = jnp.zeros((HD, HD), F32)
        for g0 in range(0, S, GR):
            rs = slice(g0, g0 + GR)
            qa, _, _, _, f, k = _hg_gates(q_ref[rs, :].astype(F32), f_ref[rs, :].astype(F32), lbv)
            qt, kt, qg, kd, _, _, _, _, e_gl = _hg_decays(qa, k, f, t_inc, t_same)
            a = jnp.where(tril, _dot(qt.astype(BF16), kt.astype(BF16), NT), 0.0)
            ib, qgb, kdb = i_ref[rs, :], qg.astype(BF16), kd.astype(BF16)
            within = _dot(a.astype(BF16), ib)
            egl_s[rs, :] = e_gl
            outs = []
            for l0 in range(0, GR, CH):
                ls = slice(l0, l0 + CH)
                st_ref[(g0 + l0) // CH] = st
                outs.append(within[ls] + _dot(qgb[ls], st.astype(BF16), NT))
                st = st * egl_s[g0 + l0:g0 + l0 + 1, :] + _dot(ib[ls], kdb[ls], TN)
            o_ref[rs, :] = jnp.concatenate(outs, axis=0)

    def slot(n):
        return pl.BlockSpec((None, None, S, HD), lambda b, h: (n, b, 0, h))

    return pl.pallas_call(
        body, name="hg_fwd", grid=(Bl, HEADS),
        in_specs=[slot(0), slot(1), slot(2), pl.BlockSpec((2, HD), lambda b, h: (0, h))],
        out_specs=[pl.BlockSpec((None, S, HD), lambda b, h: (b, 0, h)),
                   pl.BlockSpec((None, None, nc, HD, HD), lambda b, h: (b, h, 0, 0, 0))],
        out_shape=[jax.ShapeDtypeStruct((Bl, S, D), F32),
                   jax.ShapeDtypeStruct((Bl, HEADS, nc, HD, HD), F32)],
        scratch_shapes=[pltpu.VMEM((S, HD), F32)],
        compiler_params=_cparams(("parallel", "parallel")),
    )(proj, proj, proj, lb_logits)


def _hg_bwd(proj, states, do, dproj, lb_logits):
    _, Bl, S, _ = proj.shape

    def body(q_ref, f_ref, i_ref, st_ref, do_ref, lg_ref, _, dqfi_ref, dlb_ref, egl_s):
        lbv, _ = _lower_bound(lg_ref[...])
        tril, t_inc, t_dec, t_same = _group_consts()
        dst = jnp.zeros((HD, HD), F32)
        dlb = jnp.zeros((1, HD), F32)
        for g0 in reversed(range(0, S, GR)):
            rs = slice(g0, g0 + GR)
            qr, fp = q_ref[rs, :].astype(F32), f_ref[rs, :].astype(F32)
            qa, sq, sp, sn, f, k = _hg_gates(qr, fp, lbv)
            qt, kt, qg, kd, e_q, e_k, e_g, e_l, e_gl = _hg_decays(qa, k, f, t_inc, t_same)
            ib, dob, qgb, kdb = i_ref[rs, :], do_ref[rs, :], qg.astype(BF16), kd.astype(BF16)
            egl_s[rs, :] = e_gl
            ab = jnp.where(tril, _dot(qt.astype(BF16), kt.astype(BF16), NT), 0.0).astype(BF16)
            da = jnp.where(tril, _dot(dob, ib, NT), 0.0)
            dqt = _dot3(da, kt)
            dkt = _dot3(da, qt, TN)
            di_within = _dot(ab, dob, TN)
            dqg, dkd, di, dse = [], [], [], []
            for l0 in reversed(range(0, GR, CH)):
                ls = slice(l0, l0 + CH)
                st = st_ref[(g0 + l0) // CH]
                dstb = dst.astype(BF16)
                dqg.insert(0, _dot(dob[ls], st.astype(BF16)))
                dkd.insert(0, _dot(ib[ls], dstb))
                di.insert(0, _dot(kdb[ls], dstb, NT))
                dse.insert(0, jnp.broadcast_to(jnp.sum(dst * st, axis=0, keepdims=True), (CH, HD)))
                dst = dst * egl_s[g0 + l0:g0 + l0 + 1, :] + _dot(dob[ls], qgb[ls], TN)
            dqg, dkd, di, dse = (jnp.concatenate(p, axis=0) for p in (dqg, dkd, di, dse))
            dqfi_ref[2, rs, :] = (di + di_within).astype(BF16)
            dgc = dqt * qt - dkt * kt + dqg * qg - dkd * kd
            dg = _cum2l(t_dec, dgc) + _cum2l(t_same, dkd * kd) + dse * e_gl
            t1 = dg / f - (dkt * e_k + dkd * e_l)
            dqfi_ref[1, rs, :] = ((1.0 - lbv) * t1 * sp * sn).astype(BF16)
            dqfi_ref[0, rs, :] = ((dqt * e_q + dqg * e_g) * (sq * (1.0 + qr * (1.0 - sq)))).astype(BF16)
            dlb = dlb + jnp.sum(sn * t1, axis=0, keepdims=True)

        @pl.when(pl.program_id(1) == 0)
        def _():
            dlb_ref[...] = dlb

        @pl.when(pl.program_id(1) != 0)
        def _():
            dlb_ref[...] += dlb

    def slot(n):
        return pl.BlockSpec((None, None, S, HD), lambda h, b: (n, b, 0, h))

    return pl.pallas_call(
        body, name="hg_bwd", grid=(HEADS, Bl),
        in_specs=[slot(0), slot(1), slot(2),
                  pl.BlockSpec((None, None, S // CH, HD, HD), lambda h, b: (b, h, 0, 0, 0)),
                  pl.BlockSpec((None, S, HD), lambda h, b: (b, 0, h)),
                  pl.BlockSpec((2, HD), lambda h, b: (0, h)), ANY],
        out_specs=[pl.BlockSpec((3, None, S, HD), lambda h, b: (0, b, 0, h)),
                   pl.BlockSpec((1, HD), lambda h, b: (0, h))],
        out_shape=[jax.ShapeDtypeStruct(dproj.shape, dproj.dtype), jax.ShapeDtypeStruct((1, D), F32)],
        scratch_shapes=[pltpu.VMEM((S, HD), F32)],
        input_output_aliases={6: 0},
        compiler_params=_cparams(("parallel", "arbitrary")),
    )(proj, proj, proj, states, do, lb_logits, dproj)


def _place():
    x, y, c = lax.axis_index("x"), lax.axis_index("y"), lax.axis_index("c")
    return x, y, c, [(1 - x, y), (x, 1 - y), (1 - x, 1 - y)]


def _remote(src, dst, ssem, rsem, dev):
    return pltpu.make_async_remote_copy(src_ref=src, dst_ref=dst, send_sem=ssem, recv_sem=rsem,
                                        device_id=dev, device_id_type=MESH)


GATHER_SEMS = 9


def _gather_ops(wp_ref, wall_ref, ssem, rsem, hn_ref=None, hnall_ref=None):
    half = wp_ref.shape[0] // 2

    def place():
        x, y, c, chips = _place()
        return x, y, c, chips, 2 * x + y, pl.ds(c * half, half), pl.ds((1 - c) * half, half)

    def first_sends():
        x, y, c, chips, b, mine, _ = place()
        cps = [_remote(wp_ref.at[mine], wall_ref.at[b, mine], ssem.at[j], rsem.at[j], (*chip, c))
               for j, chip in enumerate(chips)]
        if hn_ref is not None:
            cps += [_remote(hn_ref, hnall_ref.at[b], ssem.at[6 + j], rsem.at[6 + j], (*chip, c))
                    for j, chip in enumerate(chips)]
        return cps

    def forwards():
        x, y, c, chips, _, mine, _ = place()
        return [_remote(wall_ref.at[2 * cx + cy, mine], wall_ref.at[2 * cx + cy, mine],
                        ssem.at[3 + j], rsem.at[3 + j], (x, y, 1 - c)) for j, (cx, cy) in enumerate(chips)]

    def start():
        for cp in first_sends():
            cp.start()

    def forward():
        x, y, c, chips, _, mine, _ = place()
        for j, (cx, cy) in enumerate(chips):
            landed = wall_ref.at[2 * cx + cy, mine]
            _remote(landed, landed, ssem.at[j], rsem.at[j], (cx, cy, c)).wait_recv()
        for cp in forwards():
            cp.start()

    def finish():
        x, y, c, chips, _, _, other = place()
        for j, (cx, cy) in enumerate(chips):
            passed = wall_ref.at[2 * cx + cy, other]
            _remote(passed, passed, ssem.at[3 + j], rsem.at[3 + j], (x, y, 1 - c)).wait_recv()
            if hn_ref is not None:
                row = hnall_ref.at[2 * cx + cy]
                _remote(row, row, ssem.at[6 + j], rsem.at[6 + j], (cx, cy, c)).wait_recv()
        for cp in first_sends() + forwards():
            cp.wait_send()

    return start, forward, finish


def _gather_weights(wp):
    def body(wp_ref, wall_ref, ssem, rsem):
        for step in _gather_ops(wp_ref, wall_ref, ssem, rsem):
            step()

    return pl.pallas_call(
        body, name="gather_weights", in_specs=[ANY], out_specs=ANY,
        out_shape=jax.ShapeDtypeStruct((NPROJ,) + wp.shape, BF16),
        scratch_shapes=[pltpu.SemaphoreType.DMA((GATHER_SEMS,)), pltpu.SemaphoreType.DMA((GATHER_SEMS,))],
    )(wp)


def _pair_ops(g_refs, r_refs, ssem, rsem):
    def copies():
        x, y, c, _ = _place()
        return [_remote(g.at[n, 1 - c], r.at[n], ssem.at[t * NPROJ + n], rsem.at[t * NPROJ + n], (x, y, 1 - c))
                for t, (g, r) in enumerate(zip(g_refs, r_refs)) for n in range(NPROJ)]

    def start():
        for cp in copies():
            cp.start()

    def finish():
        x, y, c, _ = _place()
        for t, r in enumerate(r_refs):
            for n in range(NPROJ):
                k = t * NPROJ + n
                _remote(r.at[n], r.at[n], ssem.at[k], rsem.at[k], (x, y, 1 - c)).wait_recv()
        for cp in copies():
            cp.wait_send()

    return start, finish


def _pair_shapes(grads):
    return ([jax.ShapeDtypeStruct((NPROJ,) + g.shape[2:], F32) for g in grads],
            [pltpu.SemaphoreType.DMA((len(grads) * NPROJ,))] * 2)


def _pair_exchange(grads):
    ng = len(grads)

    def body(*refs):
        start, finish = _pair_ops(refs[:ng], refs[ng:2 * ng], *refs[2 * ng:])
        start()
        finish()

    out_shape, sems = _pair_shapes(grads)
    return pl.pallas_call(
        body, name="pair_exchange", in_specs=[ANY] * ng, out_specs=[ANY] * ng,
        out_shape=out_shape, scratch_shapes=sems,
    )(*grads)


def _chip_ops(s_refs, r_refs, ssem, rsem):
    def copies():
        x, y, c, chips = _place()
        return [_remote(s.at[2 * cx + cy], r.at[2 * x + y], ssem.at[3 * t + j], rsem.at[3 * t + j], (cx, cy, c))
                for t, (s, r) in enumerate(zip(s_refs, r_refs)) for j, (cx, cy) in enumerate(chips)]

    def start():
        for cp in copies():
            cp.start()

    def finish():
        x, y, c, chips = _place()
        for t, r in enumerate(r_refs):
            for j, (cx, cy) in enumerate(chips):
                slot = r.at[2 * cx + cy]
                _remote(slot, slot, ssem.at[3 * t + j], rsem.at[3 * t + j], (cx, cy, c)).wait_recv()
        for cp in copies():
            cp.wait_send()

    return start, finish


def _chip_shapes(sums):
    return ([jax.ShapeDtypeStruct(s.shape, s.dtype) for s in sums],
            [pltpu.SemaphoreType.DMA((3 * len(sums),))] * 2)


def _sibling_share(halves, pack):
    ng = len(halves)

    def body(*refs):
        h_refs, pack_ref = refs[:ng], refs[ng]
        f_refs, allp_ref = refs[ng + 1:2 * ng + 1], refs[2 * ng + 1]
        ssem, rsem, psend, precv, lsem = refs[2 * ng + 2:]
        x, y, c, _ = _place()
        me = 4 * x + 2 * y + c
        local = pltpu.make_async_copy(pack_ref, allp_ref.at[me], lsem)
        local.start()
        flips = [(fx, fy, fc) for fx in (0, 1) for fy in (0, 1) for fc in (0, 1)][1:]
        peers = [(fx + x - 2 * fx * x, fy + y - 2 * fy * y, fc + c - 2 * fc * c) for fx, fy, fc in flips]
        sends = [_remote(pack_ref, allp_ref.at[me], psend.at[m], precv.at[m], peer) for m, peer in enumerate(peers)]
        sends += [_remote(h, f, ssem.at[t], rsem.at[t], (x, y, 1 - c))
                  for t, (h, f) in enumerate(zip(h_refs, f_refs))]
        for cp in sends:
            cp.start()
        for t, f in enumerate(f_refs):
            _remote(f, f, ssem.at[t], rsem.at[t], (x, y, 1 - c)).wait_recv()
        for m, (px, py, pc) in enumerate(peers):
            row = allp_ref.at[4 * px + 2 * py + pc]
            _remote(row, row, psend.at[m], precv.at[m], (px, py, pc)).wait_recv()
        for cp in sends:
            cp.wait_send()
        local.wait()

    return pl.pallas_call(
        body, name="sibling_share", in_specs=[ANY] * (ng + 1), out_specs=[ANY] * (ng + 1),
        out_shape=[jax.ShapeDtypeStruct(h.shape, F32) for h in halves]
        + [jax.ShapeDtypeStruct((8,) + pack.shape, F32)],
        scratch_shapes=[pltpu.SemaphoreType.DMA((ng,)), pltpu.SemaphoreType.DMA((ng,)),
                        pltpu.SemaphoreType.DMA((7,)), pltpu.SemaphoreType.DMA((7,)), pltpu.SemaphoreType.DMA],
    )(*halves, pack)


def _pair_add(own, recv, cidx, name):
    R = own.shape[2]
    tr = min(256, R)

    def body(c_ref, a_ref, b_ref, o_ref):
        o_ref[...] = (a_ref[...] + b_ref[...]).astype(BF16)

    return pl.pallas_call(
        body, name=name,
        grid_spec=pltpu.PrefetchScalarGridSpec(
            num_scalar_prefetch=1, grid=(NPROJ, R // tr),
            in_specs=[pl.BlockSpec((None, None, tr, D), lambda n, r, c: (n, c[0], r, 0)),
                      pl.BlockSpec((None, tr, D), lambda n, r, c: (n, r, 0))],
            out_specs=pl.BlockSpec((None, tr, D), lambda n, r, c: (n, r, 0))),
        out_shape=jax.ShapeDtypeStruct(recv.shape, BF16),
        compiler_params=_cparams(("parallel", "parallel")),
    )(cidx, own, recv)


def _chip_sum(sums, parts, bidx, name):
    R = parts.shape[1]
    tr = min(256, R)

    def body(b_ref, s_ref, p_ref, o_ref):
        acc = None
        for j in range(NPROJ):
            term = jnp.where(b_ref[0] == j, s_ref[...], p_ref[j]).astype(F32)
            acc = term if acc is None else acc + term
        o_ref[...] = acc

    return pl.pallas_call(
        body, name=name,
        grid_spec=pltpu.PrefetchScalarGridSpec(
            num_scalar_prefetch=1, grid=(R // tr,),
            in_specs=[pl.BlockSpec((None, tr, D), lambda r, b: (b[0], r, 0)),
                      pl.BlockSpec((NPROJ, tr, D), lambda r, b: (0, r, 0))],
            out_specs=pl.BlockSpec((tr, D), lambda r, b: (r, 0))),
        out_shape=jax.ShapeDtypeStruct((R, D), F32),
        compiler_params=_cparams(("parallel",)),
    )(bidx, sums, parts)


def _adamw_math(w, g, m, v):
    m = ADAM_B1 * m + (1.0 - ADAM_B1) * g
    v = ADAM_B2 * v + (1.0 - ADAM_B2) * (g * g)
    m_hat = m / (1.0 - ADAM_B1 ** ADAM_STEP)
    v_hat = v / (1.0 - ADAM_B2 ** ADAM_STEP)
    delta = -ADAM_LR * (m_hat / (jnp.sqrt(v_hat) + ADAM_EPS) + ADAM_WD * w)
    return delta, m, v


def _adamw(w, mine, theirs, m, v, cidx, name):
    R = mine.shape[0]
    tr = min(256, R)
    nr = R // tr

    def body(c_ref, w_ref, a_ref, b_ref, m_ref, v_ref, g_ref, d_ref, nm_ref, nv_ref):
        g = jnp.where(pl.program_id(0) == c_ref[0], a_ref[...], b_ref[...])
        g_ref[...] = g
        d_ref[...], nm_ref[...], nv_ref[...] = _adamw_math(w_ref[...], g, m_ref[...], v_ref[...])

    full = pl.BlockSpec((tr, D), lambda h, r, c: (h * nr + r, 0))
    half = pl.BlockSpec((tr, D), lambda h, r, c: (r, 0))
    return pl.pallas_call(
        body, name=name,
        grid_spec=pltpu.PrefetchScalarGridSpec(
            num_scalar_prefetch=1, grid=(2, nr),
            in_specs=[full, half, half, full, full], out_specs=[full] * 4),
        out_shape=[jax.ShapeDtypeStruct(w.shape, F32)] * 4,
        compiler_params=_cparams(("parallel", "parallel")),
    )(cidx, w, mine, theirs, m, v)


PACK_ROWS = 8


def _small_update(allp, bidx, logits, weights, moments_m, moments_v):
    shapes = [w.shape for w in weights]
    q4 = D // NPROJ

    def body(b_ref, allp_ref, hgp_ref, lg_ref, *refs):
        w_refs, m_refs, v_refs = refs[0:6], refs[6:12], refs[12:18]
        loss_ref = refs[18]
        g_out, d_out, m_out, v_out = refs[19:25], refs[25:31], refs[31:37], refs[37:43]

        def total(ref, row, lo, hi):
            acc = ref[0, row:row + 1, lo:hi]
            for dev in range(1, 8):
                acc = acc + ref[dev, row:row + 1, lo:hi]
            return acc

        _, pp = _lower_bound(lg_ref[...])
        dlb = total(allp_ref, 2, 0, D)
        grads = [total(allp_ref, 0, 0, D), total(allp_ref, 4, 0, HD), total(allp_ref, 4, HD, 2 * HD),
                 total(hgp_ref, 1, 0, q4), total(allp_ref, 4, 2 * HD, 3 * HD), None]
        loss_ref[...] = (0.5 / D) * jnp.sum(total(allp_ref, 3, 0, D), axis=1, keepdims=True)
        for t in range(6):
            if t < 5:
                rows = [(slice(None), grads[t])]
            else:
                rows = [(slice(0, 1), -pp * dlb), (slice(1, 2), pp * dlb)]
            for rs, g in rows:
                g_out[t][rs, :] = g
                d_out[t][rs, :], m_out[t][rs, :], v_out[t][rs, :] = _adamw_math(
                    w_refs[t][rs, :], g, m_refs[t][rs, :], v_refs[t][rs, :])

    whole = [pl.BlockSpec(s, lambda i, b: (0, 0)) for s in shapes]
    return pl.pallas_call(
        body, name="small_update",
        grid_spec=pltpu.PrefetchScalarGridSpec(
            num_scalar_prefetch=1, grid=(1,),
            in_specs=[pl.BlockSpec((8, PACK_ROWS, D), lambda i, b: (0, 0, 0)),
                      pl.BlockSpec((8, PACK_ROWS, q4), lambda i, b: (0, 0, b[0])),
                      pl.BlockSpec((2, D), lambda i, b: (0, 0))] + whole * 3,
            out_specs=[pl.BlockSpec((1, 1), lambda i, b: (0, 0))] + whole * 4),
        out_shape=[jax.ShapeDtypeStruct((1, 1), F32)] + [jax.ShapeDtypeStruct(s, F32) for s in shapes] * 4,
        compiler_params=_cparams(("arbitrary",)),
    )(bidx, allp, allp, logits, *weights, *moments_m, *moments_v)


def kernel(x, sb_norm, sb_w_in, sb_q_gain, sb_k_gain, sb_w_out, hg_norm, hg_w_in, hg_o_gain, hg_w_out, hg_lb_logits, loss_target, m_sb_norm, m_sb_w_in, m_sb_q_gain, m_sb_k_gain, m_sb_w_out, m_hg_norm, m_hg_w_in, m_hg_o_gain, m_hg_w_out, m_hg_lb_logits, v_sb_norm, v_sb_w_in, v_sb_q_gain, v_sb_k_gain, v_sb_w_out, v_hg_norm, v_hg_w_in, v_hg_o_gain, v_hg_w_out, v_hg_lb_logits):
    Bl, S, _ = x.shape
    T = Bl * S
    cidx = lax.axis_index("c").astype(jnp.int32).reshape(1)
    bidx = (2 * lax.axis_index("x") + lax.axis_index("y")).astype(jnp.int32).reshape(1)

    def in_hbm(arrays):
        return [pltpu.with_memory_space_constraint(a, pltpu.HBM) for a in arrays]

    def own_slot(gathered, mine):
        return lax.dynamic_update_slice(gathered, mine[None], (bidx[0],) + (0,) * mine.ndim)

    def halved(g):
        return g.reshape(NPROJ, 2, g.shape[-2] * g.shape[0] // (2 * NPROJ), D)

    def heads(a):
        return a.reshape(a.shape[:-2] + (Bl, S, D))

    def flat(a):
        return a.reshape(a.shape[:-3] + (T, D))

    wp_first = sb_w_in[0].astype(BF16)
    wp_rest = jnp.concatenate([hg_w_in[0], sb_w_out[0], hg_w_out[0]], axis=0).astype(BF16)
    wall_first, = in_hbm([own_slot(_gather_weights(wp_first), wp_first)])
    x2 = x.reshape(T, D)
    tgt = loss_target.reshape(T, D)

    proj0 = _in_proj_fwd(x2, sb_norm, wall_first, W_IN, "sb_in_fwd")
    o0, ctot, wall_rest, hnall = _sb_fwd(heads(proj0), sb_q_gain, sb_k_gain, wp_rest, hg_norm)
    wall_rest, = in_hbm([own_slot(wall_rest, wp_rest)])
    hgn = own_slot(hnall, hg_norm).reshape(1, D)
    h1 = _out_proj_fwd(flat(o0), proj0, x2, wall_rest, W_OUT_SB, "sb_out_fwd")
    proj1 = _in_proj_fwd(h1, hgn, wall_rest, W_IN, "hg_in_fwd")
    o1, states = _hg_fwd(heads(proj1), hg_lb_logits)
    dh2, loss_terms = _out_proj_fwd(flat(o1), proj1, h1, wall_rest, W_OUT_HG, "hg_out_fwd",
                                    o_gain=hg_o_gain, target=tgt)

    do1, dproj1, gout_hg, d_ogain = _out_proj_bwd(dh2, flat(o1), proj1, wall_rest, W_OUT_HG, "hg_out_bwd",
                                                  o_gain=hg_o_gain)
    dproj1, dlb = _hg_bwd(heads(proj1), states, heads(do1), heads(dproj1), hg_lb_logits)
    dproj1 = flat(dproj1)
    dh1, d_hgn = _in_proj_bwd_x(dproj1, wall_rest, W_IN, h1, hgn, dh2, "hg_in_bwd_x")
    big_hg = in_hbm([halved(_in_proj_bwd_w(dproj1, h1, hgn, "hg_in_bwd_w")), halved(gout_hg)])
    do0, dproj0, gout_sb, *recv_hg = _out_proj_bwd(dh1, flat(o0), proj0, wall_rest, W_OUT_SB, "sb_out_bwd",
                                                   exchange=big_hg)
    sums_hg = in_hbm(_pair_add(g, r, cidx, "pair_add_" + nm)
                     for g, r, nm in zip(big_hg, in_hbm(recv_hg), ("hg_in", "hg_out")))
    dproj0, d_qg, d_kg, *parts_hg = _sb_bwd(heads(proj0), ctot, heads(do0), heads(dproj0),
                                            sb_q_gain, sb_k_gain, sums_hg)
    dproj0 = flat(dproj0)

    big_sb = in_hbm([halved(_in_proj_bwd_w(dproj0, x2, sb_norm, "sb_in_bwd_w")), halved(gout_sb)])
    sums_sb = in_hbm(_pair_add(g, r, cidx, "pair_add_" + nm)
                     for g, r, nm in zip(big_sb, in_hbm(_pair_exchange(big_sb)), ("sb_in", "sb_out")))
    grad_x, d_sbn, *parts_sb = _in_proj_bwd_x(dproj0, wall_first, W_IN, x2, sb_norm, dh1, "sb_in_bwd_x",
                                              exchange=sums_sb)

    names = ["sb_in", "hg_in", "sb_out", "hg_out"]
    sums = [sums_sb[0], sums_hg[0], sums_sb[1], sums_hg[1]]
    parts = [parts_sb[0], parts_hg[0], parts_sb[1], parts_hg[1]]
    halves = in_hbm(_chip_sum(sm, p, bidx, "chip_sum_" + nm) for sm, p, nm in zip(sums, in_hbm(parts), names))
    gains = jnp.concatenate([d_qg, d_kg, d_ogain, jnp.zeros((1, D - 3 * HD), F32)], axis=1)
    pack = jnp.concatenate([d_sbn, d_hgn, dlb, loss_terms, gains, jnp.zeros((3, D), F32)], axis=0)
    *theirs, allp = _sibling_share(halves, pack)
    theirs = in_hbm(theirs)

    big_w = [sb_w_in, hg_w_in, sb_w_out, hg_w_out]
    big_m = [m_sb_w_in, m_hg_w_in, m_sb_w_out, m_hg_w_out]
    big_v = [v_sb_w_in, v_hg_w_in, v_sb_w_out, v_hg_w_out]
    upd = [_adamw(*in_hbm([w[0], a, b, m[0], v[0]]), cidx, "adamw_" + nm)
           for w, a, b, m, v, nm in zip(big_w, halves, theirs, big_m, big_v, names)]
    (g_sb_in, d_sb_in, nm_sb_in, nv_sb_in), (g_hg_in, d_hg_in, nm_hg_in, nv_hg_in), \
        (g_sb_out, d_sb_out, nm_sb_out, nv_sb_out), (g_hg_out, d_hg_out, nm_hg_out, nv_hg_out) = [
            tuple(a[None] for a in u) for u in upd]

    small = _small_update(
        allp, bidx, hg_lb_logits,
        [sb_norm, sb_q_gain, sb_k_gain, hg_norm, hg_o_gain, hg_lb_logits],
        [m_sb_norm, m_sb_q_gain, m_sb_k_gain, m_hg_norm, m_hg_o_gain, m_hg_lb_logits],
        [v_sb_norm, v_sb_q_gain, v_sb_k_gain, v_hg_norm, v_hg_o_gain, v_hg_lb_logits])
    loss = small[0].reshape(())
    (g_sbn, g_qg, g_kg, g_hgn, g_og, g_lb) = small[1:7]
    (d_sbn2, d_qg2, d_kg2, d_hgn2, d_og2, d_lb2) = small[7:13]
    (nm_sbn, nm_qg, nm_kg, nm_hgn, nm_og, nm_lb) = small[13:19]
    (nv_sbn, nv_qg, nv_kg, nv_hgn, nv_og, nv_lb) = small[19:25]

    return (loss, grad_x.reshape(Bl, S, D),
            g_sbn, g_sb_in, g_qg, g_kg, g_sb_out, g_hgn, g_hg_in, g_og, g_hg_out, g_lb,
            d_sbn2, d_sb_in, d_qg2, d_kg2, d_sb_out, d_hgn2, d_hg_in, d_og2, d_hg_out, d_lb2,
            nm_sbn, nm_sb_in, nm_qg, nm_kg, nm_sb_out, nm_hgn, nm_hg_in, nm_og, nm_hg_out, nm_lb,
            nv_sbn, nv_sb_in, nv_qg, nv_kg, nv_sb_out, nv_hgn, nv_hg_in, nv_og, nv_hg_out, nv_lb)
```

```python
import functools

import jax
import jax.numpy as jnp
from jax import lax
from jax.experimental import pallas as pl
from jax.experimental.pallas import tpu as pltpu

F32 = jnp.float32
BF16 = jnp.bfloat16
MESH = pl.DeviceIdType.MESH
ANY = pl.BlockSpec(memory_space=pl.ANY)

D = 1024
HEADS = 8
HD = 128
NPROJ = 4
RMS_EPS = 1e-6
TK = 256
CH = 64
CH_LOG2 = 6
GR = 128
SCALE = HD ** -0.5
EXP_CLAMP = 60.0
W_IN, W_OUT_SB, W_OUT_HG = 0, 4, 5

ADAM_LR = 0.001
ADAM_B1 = 0.9
ADAM_B2 = 0.999
ADAM_EPS = 1e-08
ADAM_WD = 0.01
ADAM_STEP = 10

NT = (((1,), (1,)), ((), ()))
TN = (((0,), (0,)), ((), ()))
MIB = 1024 * 1024


def _cparams(sem=None, vmem_mib=40):
    return pltpu.CompilerParams(dimension_semantics=sem, vmem_limit_bytes=vmem_mib * MIB)


def _dot(a, b, dims=None):
    if dims is None:
        return jnp.dot(a, b, preferred_element_type=F32)
    return lax.dot_general(a, b, dims, preferred_element_type=F32)


def _sigmoid(x):
    return 1.0 / (1.0 + jnp.exp(-x))


def _rms(x):
    return lax.rsqrt(jnp.mean(x * x, axis=-1, keepdims=True) + RMS_EPS)


def _rms_bwd(x, r, gain, dy):
    a = dy * gain
    dx = r * a - x * (r * r * r) * jnp.mean(x * a, axis=-1, keepdims=True)
    return dx, dy * (x * r)


def _split2(v):
    hi = v.astype(BF16)
    lo = (v - hi.astype(F32)).astype(BF16)
    return hi, lo


def _cum2(v, u):
    hi, lo = _split2(v)
    return _dot(hi, u) + _dot(lo, u)


def _dot3(a, b, dims=None):
    ah, al = _split2(a)
    bh, bl = _split2(b)
    return _dot(ah, bh, dims) + _dot(ah, bl, dims) + _dot(al, bh, dims)


def _cum2l(u, v):
    hi, lo = _split2(v)
    return _dot(u, hi) + _dot(u, lo)


def _in_proj_fwd(h, gain, wall, wblk, name):
    T = h.shape[0]
    tm = min(1024, T)

    def body(h_ref, g_ref, w_ref, o_ref, u_s):
        rows = pl.ds(pl.multiple_of(pl.program_id(1) * tm, tm), tm)

        @pl.when(pl.program_id(0) == 0)
        def _():
            x = h_ref[...]
            u_s[rows, :] = (x * _rms(x) * g_ref[...]).astype(BF16)

        o_ref[...] = _dot(u_s[rows, :], w_ref[...]).astype(BF16)

    return pl.pallas_call(
        body, name=name, grid=(NPROJ, T // tm),
        in_specs=[pl.BlockSpec((tm, D), lambda n, i: (jnp.where(n == 0, i, 0), 0)),
                  pl.BlockSpec((1, D), lambda n, i: (0, 0)),
                  pl.BlockSpec((None, D, D), lambda n, i: (n, wblk, 0))],
        out_specs=pl.BlockSpec((None, tm, D), lambda n, i: (n, i, 0)),
        out_shape=jax.ShapeDtypeStruct((NPROJ, T, D), BF16),
        scratch_shapes=[pltpu.VMEM((T, D), BF16)],
        compiler_params=_cparams(("arbitrary", "arbitrary")),
    )(h, gain, wall)


def _head_norm(x):
    outs = []
    for hh in range(x.shape[1] // HD):
        xs = x[:, hh * HD:(hh + 1) * HD]
        outs.append((xs, _rms(xs)))
    return outs


def _w_out_specs(wblk):
    kb = D // NPROJ
    return [pl.BlockSpec((None, kb, D), functools.partial(lambda j, i: (j, wblk, 0), j)) for j in range(NPROJ)]


def _out_proj_fwd(o, proj, resid, wall, wblk, name, o_gain=None, target=None):
    T = o.shape[0]
    tm = min(512, T)
    kb = D // NPROJ
    with_loss = target is not None

    def body(*refs):
        o_ref, g_ref, r_ref = refs[:3]
        w_refs = refs[3:3 + NPROJ]
        if with_loss:
            og_ref, t_ref, dh_ref, ls_ref = refs[3 + NPROJ:]
        else:
            h_ref, = refs[3 + NPROJ:]
        x = o_ref[...]
        if with_loss:
            x = jnp.concatenate([xs * r * og_ref[...] for xs, r in _head_norm(x)], axis=1)
        g = g_ref[...].astype(F32)
        a = (x * (g * _sigmoid(g))).astype(BF16)
        hnew = r_ref[...]
        for j in range(NPROJ):
            hnew = hnew + _dot(a[:, j * kb:(j + 1) * kb], w_refs[j][...])
        if with_loss:
            err = hnew - t_ref[...]
            dh_ref[...] = err * (1.0 / D)
            part = jnp.sum(err * err, axis=0, keepdims=True)

            @pl.when(pl.program_id(0) == 0)
            def _():
                ls_ref[...] = part

            @pl.when(pl.program_id(0) != 0)
            def _():
                ls_ref[...] += part
        else:
            h_ref[...] = hnew

    tile = pl.BlockSpec((tm, D), lambda i: (i, 0))
    in_specs = [tile, pl.BlockSpec((None, tm, D), lambda i: (3, i, 0)), tile] + _w_out_specs(wblk)
    args = [o, proj, resid] + [wall] * NPROJ
    out_specs = tile
    out_shape = jax.ShapeDtypeStruct((T, D), F32)
    if with_loss:
        in_specs += [pl.BlockSpec((1, HD), lambda i: (0, 0)), tile]
        args += [o_gain, target]
        out_specs = [tile, pl.BlockSpec((1, D), lambda i: (0, 0))]
        out_shape = [out_shape, jax.ShapeDtypeStruct((1, D), F32)]
    return pl.pallas_call(
        body, name=name, grid=(T // tm,), in_specs=in_specs, out_specs=out_specs,
        out_shape=out_shape, compiler_params=_cparams(("arbitrary",)),
    )(*args)


def _out_proj_bwd(dy, o, proj, wall, wblk, name, o_gain=None, exchange=()):
    T = o.shape[0]
    tm = min(512, T)
    kb = D // NPROJ
    normed = o_gain is not None
    ne = len(exchange)

    def body(*refs):
        it = iter(refs)
        dy_ref, o_ref, g_ref = (next(it) for _ in range(3))
        w_refs = [next(it) for _ in range(NPROJ)]
        og_ref = next(it) if normed else None
        xg_refs = [next(it) for _ in range(ne)]
        do_ref, dg_ref, dw_ref = (next(it) for _ in range(3))
        dgain_ref = next(it) if normed else None
        xr_refs = [next(it) for _ in range(ne)]
        wt_s = next(it)
        first = pl.program_id(0) == 0
        if ne:
            start, finish = _pair_ops(xg_refs, xr_refs, next(it), next(it))
            pl.when(first)(start)

        @pl.when(first)
        def _():
            for j in range(NPROJ):
                wt_s[:, j * kb:(j + 1) * kb] = w_refs[j][...].T
        g = g_ref[...].astype(F32)
        s = _sigmoid(g)
        sl = g * s
        x = o_ref[...]
        if normed:
            heads = _head_norm(x)
            on = jnp.concatenate([xs * r * og_ref[...] for xs, r in heads], axis=1)
        else:
            on = x
        dyb = dy_ref[...].astype(BF16)
        a = (on * sl).astype(BF16)
        at = a.T
        for j in range(NPROJ):
            part = _dot(at[j * kb:(j + 1) * kb, :], dyb)

            @pl.when(first)
            def _():
                dw_ref[j] = part

            @pl.when(jnp.logical_not(first))
            def _():
                dw_ref[j] += part

        da = _dot(dyb, wt_s[...])
        d_on = da * sl
        dg_ref[...] = (da * on * (s * (1.0 + g * (1.0 - s)))).astype(BF16)
        if normed:
            dxs, gsum = [], None
            for hh, (xs, r) in enumerate(heads):
                dx, gt = _rms_bwd(xs, r, og_ref[...], d_on[:, hh * HD:(hh + 1) * HD])
                dxs.append(dx)
                gt = jnp.sum(gt, axis=0, keepdims=True)
                gsum = gt if gsum is None else gsum + gt
            do_ref[...] = jnp.concatenate(dxs, axis=1).astype(BF16)

            @pl.when(first)
            def _():
                dgain_ref[...] = gsum

            @pl.when(jnp.logical_not(first))
            def _():
                dgain_ref[...] += gsum
        else:
            do_ref[...] = d_on.astype(BF16)
        if ne:
            pl.when(pl.program_id(0) == T // tm - 1)(finish)

    tile = pl.BlockSpec((tm, D), lambda i: (i, 0))
    gate = pl.BlockSpec((None, tm, D), lambda i: (3, i, 0))
    in_specs = [tile, tile, gate] + _w_out_specs(wblk)
    args = [dy, o, proj] + [wall] * NPROJ
    out_specs = [tile, gate, pl.BlockSpec((NPROJ, kb, D), lambda i: (0, 0, 0))]
    out_shape = [jax.ShapeDtypeStruct((T, D), BF16),
                 jax.ShapeDtypeStruct((NPROJ, T, D), BF16),
                 jax.ShapeDtypeStruct((NPROJ, kb, D), F32)]
    if normed:
        in_specs.append(pl.BlockSpec((1, HD), lambda i: (0, 0)))
        args.append(o_gain)
        out_specs.append(pl.BlockSpec((1, HD), lambda i: (0, 0)))
        out_shape.append(jax.ShapeDtypeStruct((1, HD), F32))
    x_shape, x_sems = _pair_shapes(exchange) if ne else ([], [])
    return pl.pallas_call(
        body, name=name, grid=(T // tm,), in_specs=in_specs + [ANY] * ne, out_specs=out_specs + [ANY] * ne,
        out_shape=out_shape + x_shape, scratch_shapes=[pltpu.VMEM((D, D), BF16)] + x_sems,
        compiler_params=_cparams(("arbitrary",), vmem_mib=48),
    )(*args, *exchange)


def _in_proj_bwd_x(dproj, wall, wblk, h, gain, dres, name, exchange=()):
    T = h.shape[0]
    tm = min(1024, T)
    ne = len(exchange)

    def body(d_ref, w_ref, h_ref, g_ref, r_ref, *refs):
        xs_refs, (dh_ref, dgain_ref), xr_refs = refs[:ne], refs[ne:ne + 2], refs[ne + 2:2 * ne + 2]
        du, wt_s = refs[2 * ne + 2:2 * ne + 4]
        i, n = pl.program_id(0), pl.program_id(1)
        if ne:
            start, finish = _chip_ops(xs_refs, xr_refs, *refs[2 * ne + 4:])
            pl.when(jnp.logical_and(i == 0, n == 0))(start)

        @pl.when(i == 0)
        def _():
            wt_s[n] = w_ref[...].T

        part = _dot(d_ref[...], wt_s[n])

        @pl.when(n == 0)
        def _():
            du[...] = part

        @pl.when(n != 0)
        def _():
            du[...] += part

        @pl.when(n == NPROJ - 1)
        def _():
            x = h_ref[...]
            dx, gt = _rms_bwd(x, _rms(x), g_ref[...], du[...])
            dh_ref[...] = r_ref[...] + dx
            gt = jnp.sum(gt, axis=0, keepdims=True)

            @pl.when(i == 0)
            def _():
                dgain_ref[...] = gt

            @pl.when(i != 0)
            def _():
                dgain_ref[...] += gt

        if ne:
            pl.when(jnp.logical_and(i == T // tm - 1, n == NPROJ - 1))(finish)

    x_shape, x_sems = _chip_shapes(exchange) if ne else ([], [])
    return pl.pallas_call(
        body, name=name, grid=(T // tm, NPROJ),
        in_specs=[pl.BlockSpec((None, tm, D), lambda i, n: (n, i, 0)),
                  pl.BlockSpec((None, D, D), lambda i, n: (jnp.where(i == 0, n, NPROJ - 1), wblk, 0)),
                  pl.BlockSpec((tm, D), lambda i, n: (i, 0)),
                  pl.BlockSpec((1, D), lambda i, n: (0, 0)),
                  pl.BlockSpec((tm, D), lambda i, n: (i, 0))] + [ANY] * ne,
        out_specs=[pl.BlockSpec((tm, D), lambda i, n: (i, 0)),
                   pl.BlockSpec((1, D), lambda i, n: (0, 0))] + [ANY] * ne,
        out_shape=[jax.ShapeDtypeStruct((T, D), F32), jax.ShapeDtypeStruct((1, D), F32)] + x_shape,
        scratch_shapes=[pltpu.VMEM((tm, D), F32), pltpu.VMEM((NPROJ, D, D), BF16)] + x_sems,
        compiler_params=_cparams(("arbitrary", "arbitrary"), vmem_mib=56),
    )(dproj, wall, h, gain, dres, *exchange)


def _in_proj_bwd_w(dproj, h, gain, name):
    T = h.shape[0]
    tk = min(1024, T)

    def body(d_ref, h_ref, g_ref, dw_ref, ut_s):
        k = pl.program_id(1)

        @pl.when(pl.program_id(0) == 0)
        def _():
            x = h_ref[...]
            ut_s[k] = (x * _rms(x) * g_ref[...]).astype(BF16).T

        part = _dot(ut_s[k], d_ref[...])

        @pl.when(k == 0)
        def _():
            dw_ref[...] = part

        @pl.when(k != 0)
        def _():
            dw_ref[...] += part

    return pl.pallas_call(
        body, name=name, grid=(NPROJ, T // tk),
        in_specs=[pl.BlockSpec((None, tk, D), lambda n, k: (n, k, 0)),
                  pl.BlockSpec((tk, D), lambda n, k: (jnp.where(n == 0, k, 0), 0)),
                  pl.BlockSpec((1, D), lambda n, k: (0, 0))],
        out_specs=pl.BlockSpec((None, D, D), lambda n, k: (n, 0, 0)),
        out_shape=jax.ShapeDtypeStruct((NPROJ, D, D), F32),
        scratch_shapes=[pltpu.VMEM((T // tk, D, tk), BF16)],
        compiler_params=_cparams(("arbitrary", "arbitrary")),
    )(dproj, h, gain)


def _log_sigmoid_pair(z):
    lb = jnp.minimum(z, 0.0) - jnp.log(1.0 + jnp.exp(-jnp.abs(z)))
    return lb, lb - z


def _slab_consts():
    t = lax.broadcasted_iota(jnp.int32, (TK, TK), 0)
    s = lax.broadcasted_iota(jnp.int32, (TK, TK), 1)
    return s < t, (t > s).astype(BF16), (t < s).astype(BF16)


def _slab_rows(k0, S):
    return [(r0, r1, masked) for r0, r1, masked in ((k0, k0 + TK, True), (k0 + TK, S, False)) if r0 < r1]


def _sb_fwd(proj, q_gain, k_gain, wp, hn):
    _, Bl, S, _ = proj.shape
    steps = Bl * HEADS

    def body(q_ref, k_ref, v_ref, qg_ref, kg_ref, wp_ref, hn_ref, o_ref, ct_ref, wall_ref, hnall_ref,
             qn, kn, vb, ssem, rsem):
        step = pl.program_id(0) * HEADS + pl.program_id(1)
        start, forward, finish = _gather_ops(wp_ref, wall_ref, ssem, rsem, hn_ref, hnall_ref)
        pl.when(step == 0)(start)
        pl.when(step == steps // 2)(forward)
        q = q_ref[...].astype(F32)
        qn[...] = (q * _rms(q) * (qg_ref[...] * SCALE)).astype(BF16)
        k = k_ref[...].astype(F32)
        kn[...] = (k * _rms(k) * kg_ref[...]).astype(BF16)
        vb[...] = v_ref[...]
        tri, u_gt, _ = _slab_consts()
        nb = S // TK
        c_blk = [jnp.zeros((TK, 1), F32)] * nb
        o_blk = [jnp.zeros((TK, HD), F32)] * nb
        for k0 in reversed(range(0, S, TK)):
            kb, vbb = kn[k0:k0 + TK, :], vb[k0:k0 + TK, :]
            for r0, r1, masked in _slab_rows(k0, S):
                blocks = range(r0 // TK, r1 // TK)
                z = _dot(qn[r0:r1, :], kb, NT)
                lb, ls = _log_sigmoid_pair(z)
                if masked:
                    ls = jnp.where(tri, ls, 0.0)
                c = jnp.concatenate([c_blk[b] for b in blocks], axis=0)
                w = jnp.exp(lb + _cum2(ls, u_gt) + c)
                if masked:
                    w = jnp.where(tri, w, 0.0)
                o_new = _dot(w.astype(BF16), vbb)
                c_new = jnp.sum(ls, axis=1, keepdims=True)
                for i, b in enumerate(blocks):
                    o_blk[b] = o_blk[b] + o_new[i * TK:(i + 1) * TK]
                    c_blk[b] = c_blk[b] + c_new[i * TK:(i + 1) * TK]
        o_ref[...] = jnp.concatenate(o_blk, axis=0)
        ct_ref[...] = jnp.concatenate(c_blk, axis=0)
        pl.when(step == steps - 1)(finish)

    def slot(n):
        return pl.BlockSpec((None, None, S, HD), lambda b, h: (n, b, 0, h))

    return pl.pallas_call(
        body, name="sb_fwd", grid=(Bl, HEADS),
        in_specs=[slot(0), slot(1), slot(2),
                  pl.BlockSpec((1, HD), lambda b, h: (0, 0)),
                  pl.BlockSpec((1, HD), lambda b, h: (0, 0)), ANY, ANY],
        out_specs=[pl.BlockSpec((None, S, HD), lambda b, h: (b, 0, h)),
                   pl.BlockSpec((None, None, S, 1), lambda b, h: (b, h, 0, 0)), ANY, ANY],
        out_shape=[jax.ShapeDtypeStruct((Bl, S, D), F32),
                   jax.ShapeDtypeStruct((Bl, HEADS, S, 1), F32),
                   jax.ShapeDtypeStruct((NPROJ,) + wp.shape, BF16),
                   jax.ShapeDtypeStruct((NPROJ,) + hn.shape, F32)],
        scratch_shapes=[pltpu.VMEM((S, HD), BF16)] * 3 + [pltpu.SemaphoreType.DMA((GATHER_SEMS,))] * 2,
        compiler_params=_cparams(("arbitrary", "arbitrary"), vmem_mib=56),
    )(proj, proj, proj, q_gain, k_gain, wp, hn)


def _sb_bwd(proj, ctot, do, dproj, q_gain, k_gain, exchange):
    _, Bl, S, _ = proj.shape
    ne = len(exchange)

    def body(q_ref, k_ref, v_ref, ct_ref, do_ref, qg_ref, kg_ref, _, *refs):
        xs_refs, (dqkv_ref, dqg_ref, dkg_ref), xr_refs = refs[:ne], refs[ne:ne + 3], refs[ne + 3:2 * ne + 3]
        qn, kn, vb, qt_s, dot_s, dqn, dkt_s, dvt_s, ssem, rsem = refs[2 * ne + 3:]
        step = pl.program_id(0) * HEADS + pl.program_id(1)
        first = step == 0
        start, finish = _chip_ops(xs_refs, xr_refs, ssem, rsem)

        @pl.when(first)
        def _():
            start()
            dqg_ref[...] = jnp.zeros_like(dqg_ref)
            dkg_ref[...] = jnp.zeros_like(dkg_ref)

        q = q_ref[...].astype(F32)
        rq = _rms(q)
        qn[...] = (q * rq * (qg_ref[...] * SCALE)).astype(BF16)
        k = k_ref[...].astype(F32)
        rk = _rms(k)
        kn[...] = (k * rk * kg_ref[...]).astype(BF16)
        vb[...] = v_ref[...]
        qt_s[...] = qn[...].T
        dot_s[...] = do_ref[...].T
        for acc in (dqn, dkt_s, dvt_s):
            acc[...] = jnp.zeros_like(acc)
        tri, u_gt, u_lt = _slab_consts()
        nb = S // TK
        passed, e = [jnp.zeros((TK, 1), F32)] * nb, [jnp.zeros((TK, 1), F32)] * nb
        for k0 in range(0, S, TK):
            keys = slice(k0, k0 + TK)
            kb, vbb = kn[keys, :], vb[keys, :]
            for r0, r1, masked in _slab_rows(k0, S):
                rows, blocks = slice(r0, r1), range(r0 // TK, r1 // TK)
                qb, dobb = qn[rows, :], do_ref[rows, :]
                z = _dot(qb, kb, NT)
                lb, ls = _log_sigmoid_pair(z)
                if masked:
                    ls = jnp.where(tri, ls, 0.0)
                p_new = jnp.concatenate([passed[b] for b in blocks], axis=0) + jnp.sum(ls, axis=1, keepdims=True)
                w = jnp.exp(lb + _cum2(ls, u_gt) + (ct_ref[rows, :] - p_new))
                if masked:
                    w = jnp.where(tri, w, 0.0)
                de = w * _dot(dobb, vbb, NT)
                e_old = jnp.concatenate([e[b] for b in blocks], axis=0)
                dls = e_old + _cum2(de, u_lt)
                e_new = e_old + jnp.sum(de, axis=1, keepdims=True)
                for i, b in enumerate(blocks):
                    passed[b], e[b] = p_new[i * TK:(i + 1) * TK], e_new[i * TK:(i + 1) * TK]
                sg = jnp.exp(lb)
                dz = de - sg * (de + dls)
                if masked:
                    dz = jnp.where(tri, dz, 0.0)
                dzb = dz.astype(BF16)
                dqn[rows, :] += _dot(dzb, kb)
                dkt_s[:, keys] += _dot(qt_s[:, rows], dzb)
                dvt_s[:, keys] += _dot(dot_s[:, rows], w.astype(BF16))

        dx, gt = _rms_bwd(q, rq, qg_ref[...], dqn[...] * SCALE)
        dqkv_ref[0] = dx.astype(BF16)
        dqg_ref[...] += jnp.sum(gt, axis=0, keepdims=True)
        dx, gt = _rms_bwd(k, rk, kg_ref[...], dkt_s[...].T)
        dqkv_ref[1] = dx.astype(BF16)
        dkg_ref[...] += jnp.sum(gt, axis=0, keepdims=True)
        dqkv_ref[2] = dvt_s[...].T.astype(BF16)
        pl.when(step == Bl * HEADS - 1)(finish)

    def slot(n):
        return pl.BlockSpec((None, None, S, HD), lambda b, h: (n, b, 0, h))

    head = pl.BlockSpec((None, S, HD), lambda b, h: (b, 0, h))
    gain = pl.BlockSpec((1, HD), lambda b, h: (0, 0))
    x_shape, x_sems = _chip_shapes(exchange)
    return pl.pallas_call(
        body, name="sb_bwd", grid=(Bl, HEADS),
        in_specs=[slot(0), slot(1), slot(2),
                  pl.BlockSpec((None, None, S, 1), lambda b, h: (b, h, 0, 0)), head, gain, gain, ANY] + [ANY] * ne,
        out_specs=[pl.BlockSpec((3, None, S, HD), lambda b, h: (0, b, 0, h)), gain, gain] + [ANY] * ne,
        out_shape=[jax.ShapeDtypeStruct(dproj.shape, dproj.dtype),
                   jax.ShapeDtypeStruct((1, HD), F32), jax.ShapeDtypeStruct((1, HD), F32)] + x_shape,
        scratch_shapes=([pltpu.VMEM((S, HD), BF16)] * 3 + [pltpu.VMEM((HD, S), BF16)] * 2 + [pltpu.VMEM((S, HD), F32)]
                        + [pltpu.VMEM((HD, S), F32)] * 2 + x_sems),
        input_output_aliases={7: 0},
        compiler_params=_cparams(("arbitrary", "arbitrary"), vmem_mib=56),
    )(proj, proj, proj, ctot, do, q_gain, k_gain, dproj, *exchange)


def _lower_bound(logits):
    l0, l1 = logits[0:1, :], logits[1:2, :]
    m = jnp.maximum(l0, l1)
    e0, e1 = jnp.exp(l0 - m), jnp.exp(l1 - m)
    p0, p1 = e0 / (e0 + e1), e1 / (e0 + e1)
    return (p0 + p1) - p0, p0 * p1


def _hg_gates(qr, fp, lbv):
    sq = _sigmoid(qr)
    sp = _sigmoid(fp)
    sn = 1.0 / (1.0 + jnp.exp(fp))
    f = lbv + (1.0 - lbv) * sp
    return qr * sq, sq, sp, sn, f, (1.0 - lbv) * sn


def _group_consts():
    t = lax.broadcasted_iota(jnp.int32, (GR, GR), 0)
    j = lax.broadcasted_iota(jnp.int32, (GR, GR), 1)
    same = lax.shift_right_logical(t, CH_LOG2) == lax.shift_right_logical(j, CH_LOG2)
    tril = jnp.logical_and(same, j <= t)
    return (tril, tril.astype(BF16), jnp.logical_and(same, j >= t).astype(BF16), same.astype(BF16))


def _hg_decays(qa, k, f, t_inc, t_same):
    g = jnp.log(f)
    gc = _cum2l(t_inc, g)
    gl = _cum2l(t_same, g)
    gm = gc - 0.5 * gl
    e_q = jnp.exp(jnp.minimum(gm, EXP_CLAMP))
    e_k = jnp.exp(jnp.minimum(-gm, EXP_CLAMP))
    e_g = jnp.exp(gc)
    e_l = jnp.exp(gl - gc)
    return qa * e_q, k * e_k, qa * e_g, k * e_l, e_q, e_k, e_g, e_l, jnp.exp(gl)


def _hg_fwd(proj, lb_logits):
    _, Bl, S, _ = proj.shape
    nc = S // CH

    def body(q_ref, f_ref, i_ref, lg_ref, o_ref, st_ref, egl_s):
        lbv, _ = _lower_bound(lg_ref[...])
        tril, t_inc, _, t_same = _group_consts()
        st = jnp.zeros((HD, HD), F32)
        for g0 in range(0, S, GR):
            rs = slice(g0, g0 + GR)
            qa, _, _, _, f, k = _hg_gates(q_ref[rs, :].astype(F32), f_ref[rs, :].astype(F32), lbv)
            qt, kt, qg, kd, _, _, _, _, e_gl = _hg_decays(qa, k, f, t_inc, t_same)
            a = jnp.where(tril, _dot(qt.astype(BF16), kt.astype(BF16), NT), 0.0)
            ib, qgb, kdb = i_ref[rs, :], qg.astype(BF16), kd.astype(BF16)
            within = _dot(a.astype(BF16), ib)
            egl_s[rs, :] = e_gl
            outs = []
            for l0 in range(0, GR, CH):
                ls = slice(l0, l0 + CH)
                st_ref[(g0 + l0) // CH] = st
                outs.append(within[ls] + _dot(qgb[ls], st.astype(BF16), NT))
                st = st * egl_s[g0 + l0:g0 + l0 + 1, :] + _dot(ib[ls], kdb[ls], TN)
            o_ref[rs, :] = jnp.concatenate(outs, axis=0)

    def slot(n):
        return pl.BlockSpec((None, None, S, HD), lambda b, h: (n, b, 0, h))

    return pl.pallas_call(
        body, name="hg_fwd", grid=(Bl, HEADS),
        in_specs=[slot(0), slot(1), slot(2), pl.BlockSpec((2, HD), lambda b, h: (0, h))],
        out_specs=[pl.BlockSpec((None, S, HD), lambda b, h: (b, 0, h)),
                   pl.BlockSpec((None, None, nc, HD, HD), lambda b, h: (b, h, 0, 0, 0))],
        out_shape=[jax.ShapeDtypeStruct((Bl, S, D), F32),
                   jax.ShapeDtypeStruct((Bl, HEADS, nc, HD, HD), F32)],
        scratch_shapes=[pltpu.VMEM((S, HD), F32)],
        compiler_params=_cparams(("parallel", "parallel")),
    )(proj, proj, proj, lb_logits)


def _hg_bwd(proj, states, do, dproj, lb_logits):
    _, Bl, S, _ = proj.shape

    def body(q_ref, f_ref, i_ref, st_ref, do_ref, lg_ref, _, dqfi_ref, dlb_ref, egl_s):
        lbv, _ = _lower_bound(lg_ref[...])
        tril, t_inc, t_dec, t_same = _group_consts()
        dst = jnp.zeros((HD, HD), F32)
        dlb = jnp.zeros((1, HD), F32)
        for g0 in reversed(range(0, S, GR)):
            rs = slice(g0, g0 + GR)
            qr, fp = q_ref[rs, :].astype(F32), f_ref[rs, :].astype(F32)
            qa, sq, sp, sn, f, k = _hg_gates(qr, fp, lbv)
            qt, kt, qg, kd, e_q, e_k, e_g, e_l, e_gl = _hg_decays(qa, k, f, t_inc, t_same)
            ib, dob, qgb, kdb = i_ref[rs, :], do_ref[rs, :], qg.astype(BF16), kd.astype(BF16)
            egl_s[rs, :] = e_gl
            ab = jnp.where(tril, _dot(qt.astype(BF16), kt.astype(BF16), NT), 0.0).astype(BF16)
            da = jnp.where(tril, _dot(dob, ib, NT), 0.0)
            dqt = _dot3(da, kt)
            dkt = _dot3(da, qt, TN)
            di_within = _dot(ab, dob, TN)
            dqg, dkd, di, dse = [], [], [], []
            for l0 in reversed(range(0, GR, CH)):
                ls = slice(l0, l0 + CH)
                st = st_ref[(g0 + l0) // CH]
                dstb = dst.astype(BF16)
                dqg.insert(0, _dot(dob[ls], st.astype(BF16)))
                dkd.insert(0, _dot(ib[ls], dstb))
                di.insert(0, _dot(kdb[ls], dstb, NT))
                dse.insert(0, jnp.broadcast_to(jnp.sum(dst * st, axis=0, keepdims=True), (CH, HD)))
                dst = dst * egl_s[g0 + l0:g0 + l0 + 1, :] + _dot(dob[ls], qgb[ls], TN)
            dqg, dkd, di, dse = (jnp.concatenate(p, axis=0) for p in (dqg, dkd, di, dse))
            dqfi_ref[2, rs, :] = (di + di_within).astype(BF16)
            dgc = dqt * qt - dkt * kt + dqg * qg - dkd * kd
            dg = _cum2l(t_dec, dgc) + _cum2l(t_same, dkd * kd) + dse * e_gl
            t1 = dg / f - (dkt * e_k + dkd * e_l)
            dqfi_ref[1, rs, :] = ((1.0 - lbv) * t1 * sp * sn).astype(BF16)
            dqfi_ref[0, rs, :] = ((dqt * e_q + dqg * e_g) * (sq * (1.0 + qr * (1.0 - sq)))).astype(BF16)
            dlb = dlb + jnp.sum(sn * t1, axis=0, keepdims=True)

        @pl.when(pl.program_id(1) == 0)
        def _():
            dlb_ref[...] = dlb

        @pl.when(pl.program_id(1) != 0)
        def _():
            dlb_ref[...] += dlb

    def slot(n):
        return pl.BlockSpec((None, None, S, HD), lambda h, b: (n, b, 0, h))

    return pl.pallas_call(
        body, name="hg_bwd", grid=(HEADS, Bl),
        in_specs=[slot(0), slot(1), slot(2),
                  pl.BlockSpec((None, None, S // CH, HD, HD), lambda h, b: (b, h, 0, 0, 0)),
                  pl.BlockSpec((None, S, HD), lambda h, b: (b, 0, h)),
                  pl.BlockSpec((2, HD), lambda h, b: (0, h)), ANY],
        out_specs=[pl.BlockSpec((3, None, S, HD), lambda h, b: (0, b, 0, h)),
                   pl.BlockSpec((1, HD), lambda h, b: (0, h))],
        out_shape=[jax.ShapeDtypeStruct(dproj.shape, dproj.dtype), jax.ShapeDtypeStruct((1, D), F32)],
        scratch_shapes=[pltpu.VMEM((S, HD), F32)],
        input_output_aliases={6: 0},
        compiler_params=_cparams(("parallel", "arbitrary")),
    )(proj, proj, proj, states, do, lb_logits, dproj)


def _place():
    x, y, c = lax.axis_index("x"), lax.axis_index("y"), lax.axis_index("c")
    return x, y, c, [(1 - x, y), (x, 1 - y), (1 - x, 1 - y)]


def _remote(src, dst, ssem, rsem, dev):
    return pltpu.make_async_remote_copy(src_ref=src, dst_ref=dst, send_sem=ssem, recv_sem=rsem,
                                        device_id=dev, device_id_type=MESH)


GATHER_SEMS = 9


def _gather_ops(wp_ref, wall_ref, ssem, rsem, hn_ref=None, hnall_ref=None):
    half = wp_ref.shape[0] // 2

    def place():
        x, y, c, chips = _place()
        return x, y, c, chips, 2 * x + y, pl.ds(c * half, half), pl.ds((1 - c) * half, half)

    def first_sends():
        x, y, c, chips, b, mine, _ = place()
        cps = [_remote(wp_ref.at[mine], wall_ref.at[b, mine], ssem.at[j], rsem.at[j], (*chip, c))
               for j, chip in enumerate(chips)]
        if hn_ref is not None:
            cps += [_remote(hn_ref, hnall_ref.at[b], ssem.at[6 + j], rsem.at[6 + j], (*chip, c))
                    for j, chip in enumerate(chips)]
        return cps

    def forwards():
        x, y, c, chips, _, mine, _ = place()
        return [_remote(wall_ref.at[2 * cx + cy, mine], wall_ref.at[2 * cx + cy, mine],
                        ssem.at[3 + j], rsem.at[3 + j], (x, y, 1 - c)) for j, (cx, cy) in enumerate(chips)]

    def start():
        for cp in first_sends():
            cp.start()

    def forward():
        x, y, c, chips, _, mine, _ = place()
        for j, (cx, cy) in enumerate(chips):
            landed = wall_ref.at[2 * cx + cy, mine]
            _remote(landed, landed, ssem.at[j], rsem.at[j], (cx, cy, c)).wait_recv()
        for cp in forwards():
            cp.start()

    def finish():
        x, y, c, chips, _, _, other = place()
        for j, (cx, cy) in enumerate(chips):
            passed = wall_ref.at[2 * cx + cy, other]
            _remote(passed, passed, ssem.at[3 + j], rsem.at[3 + j], (x, y, 1 - c)).wait_recv()
            if hn_ref is not None:
                row = hnall_ref.at[2 * cx + cy]
                _remote(row, row, ssem.at[6 + j], rsem.at[6 + j], (cx, cy, c)).wait_recv()
        for cp in first_sends() + forwards():
            cp.wait_send()

    return start, forward, finish


def _gather_weights(wp):
    def body(wp_ref, wall_ref, ssem, rsem):
        for step in _gather_ops(wp_ref, wall_ref, ssem, rsem):
            step()

    return pl.pallas_call(
        body, name="gather_weights", in_specs=[ANY], out_specs=ANY,
        out_shape=jax.ShapeDtypeStruct((NPROJ,) + wp.shape, BF16),
        scratch_shapes=[pltpu.SemaphoreType.DMA((GATHER_SEMS,)), pltpu.SemaphoreType.DMA((GATHER_SEMS,))],
    )(wp)


def _pair_ops(g_refs, r_refs, ssem, rsem):
    def copies():
        x, y, c, _ = _place()
        return [_remote(g.at[n, 1 - c], r.at[n], ssem.at[t * NPROJ + n], rsem.at[t * NPROJ + n], (x, y, 1 - c))
                for t, (g, r) in enumerate(zip(g_refs, r_refs)) for n in range(NPROJ)]

    def start():
        for cp in copies():
            cp.start()

    def finish():
        x, y, c, _ = _place()
        for t, r in enumerate(r_refs):
            for n in range(NPROJ):
                k = t * NPROJ + n
                _remote(r.at[n], r.at[n], ssem.at[k], rsem.at[k], (x, y, 1 - c)).wait_recv()
        for cp in copies():
            cp.wait_send()

    return start, finish


def _pair_shapes(grads):
    return ([jax.ShapeDtypeStruct((NPROJ,) + g.shape[2:], F32) for g in grads],
            [pltpu.SemaphoreType.DMA((len(grads) * NPROJ,))] * 2)


def _pair_exchange(grads):
    ng = len(grads)

    def body(*refs):
        start, finish = _pair_ops(refs[:ng], refs[ng:2 * ng], *refs[2 * ng:])
        start()
        finish()

    out_shape, sems = _pair_shapes(grads)
    return pl.pallas_call(
        body, name="pair_exchange", in_specs=[ANY] * ng, out_specs=[ANY] * ng,
        out_shape=out_shape, scratch_shapes=sems,
    )(*grads)


def _chip_ops(s_refs, r_refs, ssem, rsem):
    def copies():
        x, y, c, chips = _place()
        return [_remote(s.at[2 * cx + cy], r.at[2 * x + y], ssem.at[3 * t + j], rsem.at[3 * t + j], (cx, cy, c))
                for t, (s, r) in enumerate(zip(s_refs, r_refs)) for j, (cx, cy) in enumerate(chips)]

    def start():
        for cp in copies():
            cp.start()

    def finish():
        x, y, c, chips = _place()
        for t, r in enumerate(r_refs):
            for j, (cx, cy) in enumerate(chips):
                slot = r.at[2 * cx + cy]
                _remote(slot, slot, ssem.at[3 * t + j], rsem.at[3 * t + j], (cx, cy, c)).wait_recv()
        for cp in copies():
            cp.wait_send()

    return start, finish


def _chip_shapes(sums):
    return ([jax.ShapeDtypeStruct(s.shape, s.dtype) for s in sums],
            [pltpu.SemaphoreType.DMA((3 * len(sums),))] * 2)


def _sibling_share(halves, pack):
    ng = len(halves)

    def body(*refs):
        h_refs, pack_ref = refs[:ng], refs[ng]
        f_refs, allp_ref = refs[ng + 1:2 * ng + 1], refs[2 * ng + 1]
        ssem, rsem, psend, precv, lsem = refs[2 * ng + 2:]
        x, y, c, _ = _place()
        me = 4 * x + 2 * y + c
        local = pltpu.make_async_copy(pack_ref, allp_ref.at[me], lsem)
        local.start()
        flips = [(fx, fy, fc) for fx in (0, 1) for fy in (0, 1) for fc in (0, 1)][1:]
        peers = [(fx + x - 2 * fx * x, fy + y - 2 * fy * y, fc + c - 2 * fc * c) for fx, fy, fc in flips]
        sends = [_remote(pack_ref, allp_ref.at[me], psend.at[m], precv.at[m], peer) for m, peer in enumerate(peers)]
        sends += [_remote(h, f, ssem.at[t], rsem.at[t], (x, y, 1 - c))
                  for t, (h, f) in enumerate(zip(h_refs, f_refs))]
        for cp in sends:
            cp.start()
        for t, f in enumerate(f_refs):
            _remote(f, f, ssem.at[t], rsem.at[t], (x, y, 1 - c)).wait_recv()
        for m, (px, py, pc) in enumerate(peers):
            row = allp_ref.at[4 * px + 2 * py + pc]
            _remote(row, row, psend.at[m], precv.at[m], (px, py, pc)).wait_recv()
        for cp in sends:
            cp.wait_send()
        local.wait()

    return pl.pallas_call(
        body, name="sibling_share", in_specs=[ANY] * (ng + 1), out_specs=[ANY] * (ng + 1),
        out_shape=[jax.ShapeDtypeStruct(h.shape, F32) for h in halves]
        + [jax.ShapeDtypeStruct((8,) + pack.shape, F32)],
        scratch_shapes=[pltpu.SemaphoreType.DMA((ng,)), pltpu.SemaphoreType.DMA((ng,)),
                        pltpu.SemaphoreType.DMA((7,)), pltpu.SemaphoreType.DMA((7,)), pltpu.SemaphoreType.DMA],
    )(*halves, pack)


def _pair_add(own, recv, cidx, name):
    R = own.shape[2]
    tr = min(512, R)

    def body(c_ref, a_ref, b_ref, o_ref):
        o_ref[...] = (a_ref[...] + b_ref[...]).astype(BF16)

    return pl.pallas_call(
        body, name=name,
        grid_spec=pltpu.PrefetchScalarGridSpec(
            num_scalar_prefetch=1, grid=(NPROJ, R // tr),
            in_specs=[pl.BlockSpec((None, None, tr, D), lambda n, r, c: (n, c[0], r, 0)),
                      pl.BlockSpec((None, tr, D), lambda n, r, c: (n, r, 0))],
            out_specs=pl.BlockSpec((None, tr, D), lambda n, r, c: (n, r, 0))),
        out_shape=jax.ShapeDtypeStruct(recv.shape, BF16),
        compiler_params=_cparams(("parallel", "parallel")),
    )(cidx, own, recv)


def _chip_sum(sums, parts, bidx, name):
    R = parts.shape[1]
    tr = min(256, R)

    def body(b_ref, s_ref, p_ref, o_ref):
        acc = None
        for j in range(NPROJ):
            term = jnp.where(b_ref[0] == j, s_ref[...], p_ref[j]).astype(F32)
            acc = term if acc is None else acc + term
        o_ref[...] = acc

    return pl.pallas_call(
        body, name=name,
        grid_spec=pltpu.PrefetchScalarGridSpec(
            num_scalar_prefetch=1, grid=(R // tr,),
            in_specs=[pl.BlockSpec((None, tr, D), lambda r, b: (b[0], r, 0)),
                      pl.BlockSpec((NPROJ, tr, D), lambda r, b: (0, r, 0))],
            out_specs=pl.BlockSpec((tr, D), lambda r, b: (r, 0))),
        out_shape=jax.ShapeDtypeStruct((R, D), F32),
        compiler_params=_cparams(("parallel",)),
    )(bidx, sums, parts)


def _adamw_math(w, g, m, v):
    m = ADAM_B1 * m + (1.0 - ADAM_B1) * g
    v = ADAM_B2 * v + (1.0 - ADAM_B2) * (g * g)
    m_hat = m / (1.0 - ADAM_B1 ** ADAM_STEP)
    v_hat = v / (1.0 - ADAM_B2 ** ADAM_STEP)
    delta = -ADAM_LR * (m_hat / (jnp.sqrt(v_hat) + ADAM_EPS) + ADAM_WD * w)
    return delta, m, v


def _adamw(w, mine, theirs, m, v, cidx, name):
    R = mine.shape[0]
    tr = min(512, R)
    nr = R // tr

    def body(c_ref, w_ref, a_ref, b_ref, m_ref, v_ref, g_ref, d_ref, nm_ref, nv_ref):
        g = jnp.where(pl.program_id(0) == c_ref[0], a_ref[...], b_ref[...])
        g_ref[...] = g
        d_ref[...], nm_ref[...], nv_ref[...] = _adamw_math(w_ref[...], g, m_ref[...], v_ref[...])

    full = pl.BlockSpec((tr, D), lambda h, r, c: (h * nr + r, 0))
    half = pl.BlockSpec((tr, D), lambda h, r, c: (r, 0))
    return pl.pallas_call(
        body, name=name,
        grid_spec=pltpu.PrefetchScalarGridSpec(
            num_scalar_prefetch=1, grid=(2, nr),
            in_specs=[full, half, half, full, full], out_specs=[full] * 4),
        out_shape=[jax.ShapeDtypeStruct(w.shape, F32)] * 4,
        compiler_params=_cparams(("parallel", "parallel"), vmem_mib=48),
    )(cidx, w, mine, theirs, m, v)


PACK_ROWS = 8


def _small_update(allp, bidx, logits, weights, moments_m, moments_v):
    shapes = [w.shape for w in weights]
    q4 = D // NPROJ

    def body(b_ref, allp_ref, hgp_ref, lg_ref, *refs):
        w_refs, m_refs, v_refs = refs[0:6], refs[6:12], refs[12:18]
        loss_ref = refs[18]
        g_out, d_out, m_out, v_out = refs[19:25], refs[25:31], refs[31:37], refs[37:43]

        def total(ref, row, lo, hi):
            acc = ref[0, row:row + 1, lo:hi]
            for dev in range(1, 8):
                acc = acc + ref[dev, row:row + 1, lo:hi]
            return acc

        _, pp = _lower_bound(lg_ref[...])
        dlb = total(allp_ref, 2, 0, D)
        grads = [total(allp_ref, 0, 0, D), total(allp_ref, 4, 0, HD), total(allp_ref, 4, HD, 2 * HD),
                 total(hgp_ref, 1, 0, q4), total(allp_ref, 4, 2 * HD, 3 * HD), None]
        loss_ref[...] = (0.5 / D) * jnp.sum(total(allp_ref, 3, 0, D), axis=1, keepdims=True)
        for t in range(6):
            if t < 5:
                rows = [(slice(None), grads[t])]
            else:
                rows = [(slice(0, 1), -pp * dlb), (slice(1, 2), pp * dlb)]
            for rs, g in rows:
                g_out[t][rs, :] = g
                d_out[t][rs, :], m_out[t][rs, :], v_out[t][rs, :] = _adamw_math(
                    w_refs[t][rs, :], g, m_refs[t][rs, :], v_refs[t][rs, :])

    whole = [pl.BlockSpec(s, lambda i, b: (0, 0)) for s in shapes]
    return pl.pallas_call(
        body, name="small_update",
        grid_spec=pltpu.PrefetchScalarGridSpec(
            num_scalar_prefetch=1, grid=(1,),
            in_specs=[pl.BlockSpec((8, PACK_ROWS, D), lambda i, b: (0, 0, 0)),
                      pl.BlockSpec((8, PACK_ROWS, q4), lambda i, b: (0, 0, b[0])),
                      pl.BlockSpec((2, D), lambda i, b: (0, 0))] + whole * 3,
            out_specs=[pl.BlockSpec((1, 1), lambda i, b: (0, 0))] + whole * 4),
        out_shape=[jax.ShapeDtypeStruct((1, 1), F32)] + [jax.ShapeDtypeStruct(s, F32) for s in shapes] * 4,
        compiler_params=_cparams(("arbitrary",)),
    )(bidx, allp, allp, logits, *weights, *moments_m, *moments_v)


def kernel(x, sb_norm, sb_w_in, sb_q_gain, sb_k_gain, sb_w_out, hg_norm, hg_w_in, hg_o_gain, hg_w_out, hg_lb_logits, loss_target, m_sb_norm, m_sb_w_in, m_sb_q_gain, m_sb_k_gain, m_sb_w_out, m_hg_norm, m_hg_w_in, m_hg_o_gain, m_hg_w_out, m_hg_lb_logits, v_sb_norm, v_sb_w_in, v_sb_q_gain, v_sb_k_gain, v_sb_w_out, v_hg_norm, v_hg_w_in, v_hg_o_gain, v_hg_w_out, v_hg_lb_logits):
    Bl, S, _ = x.shape
    T = Bl * S
    cidx = lax.axis_index("c").astype(jnp.int32).reshape(1)
    bidx = (2 * lax.axis_index("x") + lax.axis_index("y")).astype(jnp.int32).reshape(1)

    def in_hbm(arrays):
        return [pltpu.with_memory_space_constraint(a, pltpu.HBM) for a in arrays]

    def own_slot(gathered, mine):
        return lax.dynamic_update_slice(gathered, mine[None], (bidx[0],) + (0,) * mine.ndim)

    def halved(g):
        return g.reshape(NPROJ, 2, g.shape[-2] * g.shape[0] // (2 * NPROJ), D)

    def heads(a):
        return a.reshape(a.shape[:-2] + (Bl, S, D))

    def flat(a):
        return a.reshape(a.shape[:-3] + (T, D))

    wp_first = sb_w_in[0].astype(BF16)
    wp_rest = jnp.concatenate([hg_w_in[0], sb_w_out[0], hg_w_out[0]], axis=0).astype(BF16)
    wall_first, = in_hbm([own_slot(_gather_weights(wp_first), wp_first)])
    x2 = x.reshape(T, D)
    tgt = loss_target.reshape(T, D)

    proj0 = _in_proj_fwd(x2, sb_norm, wall_first, W_IN, "sb_in_fwd")
    o0, ctot, wall_rest, hnall = _sb_fwd(heads(proj0), sb_q_gain, sb_k_gain, wp_rest, hg_norm)
    wall_rest, = in_hbm([own_slot(wall_rest, wp_rest)])
    hgn = own_slot(hnall, hg_norm).reshape(1, D)
    h1 = _out_proj_fwd(flat(o0), proj0, x2, wall_rest, W_OUT_SB, "sb_out_fwd")
    proj1 = _in_proj_fwd(h1, hgn, wall_rest, W_IN, "hg_in_fwd")
    o1, states = _hg_fwd(heads(proj1), hg_lb_logits)
    dh2, loss_terms = _out_proj_fwd(flat(o1), proj1, h1, wall_rest, W_OUT_HG, "hg_out_fwd",
                                    o_gain=hg_o_gain, target=tgt)

    do1, dproj1, gout_hg, d_ogain = _out_proj_bwd(dh2, flat(o1), proj1, wall_rest, W_OUT_HG, "hg_out_bwd",
                                                  o_gain=hg_o_gain)
    dproj1, dlb = _hg_bwd(heads(proj1), states, heads(do1), heads(dproj1), hg_lb_logits)
    dproj1 = flat(dproj1)
    dh1, d_hgn = _in_proj_bwd_x(dproj1, wall_rest, W_IN, h1, hgn, dh2, "hg_in_bwd_x")
    big_hg = in_hbm([halved(_in_proj_bwd_w(dproj1, h1, hgn, "hg_in_bwd_w")), halved(gout_hg)])
    do0, dproj0, gout_sb, *recv_hg = _out_proj_bwd(dh1, flat(o0), proj0, wall_rest, W_OUT_SB, "sb_out_bwd",
                                                   exchange=big_hg)
    sums_hg = in_hbm(_pair_add(g, r, cidx, "pair_add_" + nm)
                     for g, r, nm in zip(big_hg, in_hbm(recv_hg), ("hg_in", "hg_out")))
    dproj0, d_qg, d_kg, *parts_hg = _sb_bwd(heads(proj0), ctot, heads(do0), heads(dproj0),
                                            sb_q_gain, sb_k_gain, sums_hg)
    dproj0 = flat(dproj0)

    big_sb = in_hbm([halved(_in_proj_bwd_w(dproj0, x2, sb_norm, "sb_in_bwd_w")), halved(gout_sb)])
    sums_sb = in_hbm(_pair_add(g, r, cidx, "pair_add_" + nm)
                     for g, r, nm in zip(big_sb, in_hbm(_pair_exchange(big_sb)), ("sb_in", "sb_out")))
    grad_x, d_sbn, *parts_sb = _in_proj_bwd_x(dproj0, wall_first, W_IN, x2, sb_norm, dh1, "sb_in_bwd_x",
                                              exchange=sums_sb)

    names = ["sb_in", "hg_in", "sb_out", "hg_out"]
    sums = [sums_sb[0], sums_hg[0], sums_sb[1], sums_hg[1]]
    parts = [parts_sb[0], parts_hg[0], parts_sb[1], parts_hg[1]]
    halves = in_hbm(_chip_sum(sm, p, bidx, "chip_sum_" + nm) for sm, p, nm in zip(sums, in_hbm(parts), names))
    gains = jnp.concatenate([d_qg, d_kg, d_ogain, jnp.zeros((1, D - 3 * HD), F32)], axis=1)
    pack = jnp.concatenate([d_sbn, d_hgn, dlb, loss_terms, gains, jnp.zeros((3, D), F32)], axis=0)
    *theirs, allp = _sibling_share(halves, pack)
    theirs = in_hbm(theirs)

    big_w = [sb_w_in, hg_w_in, sb_w_out, hg_w_out]
    big_m = [m_sb_w_in, m_hg_w_in, m_sb_w_out, m_hg_w_out]
    big_v = [v_sb_w_in, v_hg_w_in, v_sb_w_out, v_hg_w_out]
    upd = [_adamw(*in_hbm([w[0], a, b, m[0], v[0]]), cidx, "adamw_" + nm)
           for w, a, b, m, v, nm in zip(big_w, halves, theirs, big_m, big_v, names)]
    (g_sb_in, d_sb_in, nm_sb_in, nv_sb_in), (g_hg_in, d_hg_in, nm_hg_in, nv_hg_in), \
        (g_sb_out, d_sb_out, nm_sb_out, nv_sb_out), (g_hg_out, d_hg_out, nm_hg_out, nv_hg_out) = [
            tuple(a[None] for a in u) for u in upd]

    small = _small_update(
        allp, bidx, hg_lb_logits,
        [sb_norm, sb_q_gain, sb_k_gain, hg_norm, hg_o_gain, hg_lb_logits],
        [m_sb_norm, m_sb_q_gain, m_sb_k_gain, m_hg_norm, m_hg_o_gain, m_hg_lb_logits],
        [v_sb_norm, v_sb_q_gain, v_sb_k_gain, v_hg_norm, v_hg_o_gain, v_hg_lb_logits])
    loss = small[0].reshape(())
    (g_sbn, g_qg, g_kg, g_hgn, g_og, g_lb) = small[1:7]
    (d_sbn2, d_qg2, d_kg2, d_hgn2, d_og2, d_lb2) = small[7:13]
    (nm_sbn, nm_qg, nm_kg, nm_hgn, nm_og, nm_lb) = small[13:19]
    (nv_sbn, nv_qg, nv_kg, nv_hgn, nv_og, nv_lb) = small[19:25]

    return (loss, grad_x.reshape(Bl, S, D),
            g_sbn, g_sb_in, g_qg, g_kg, g_sb_out, g_hgn, g_hg_in, g_og, g_hg_out, g_lb,
            d_sbn2, d_sb_in, d_qg2, d_kg2, d_sb_out, d_hgn2, d_hg_in, d_og2, d_hg_out, d_lb2,
            nm_sbn, nm_sb_in, nm_qg, nm_kg, nm_sb_out, nm_hgn, nm_hg_in, nm_og, nm_hg_out, nm_lb,
            nv_sbn, nv_sb_in, nv_qg, nv_kg, nv_sb_out, nv_hgn, nv_hg_in, nv_og, nv_hg_out, nv_lb)
```
